```python
import math
import jax, jax.numpy as jnp
from jax import lax
import numpy as np


D_MODEL = 1024
BATCH = 8
SEQ = 2048
DEPTH = 2

A_HEADS = 4
A_QK_DIM = 64
A_WIDTH = A_HEADS * 2 * A_QK_DIM
Q_BLOCK = 128
N_BUCKETS = 32
MAX_DISTANCE = 128
B_HEADS = 4
B_HEAD_DIM = 64
B_WIDTH = B_HEADS * B_HEAD_DIM
B_DECAY_LORA = 64
B_AAA_LORA = 64
B_GATE_LORA = 128
B_LNX_EPS = 64e-5
C_GROUPS = 4
C_GROUP_DIM = 64
C_WIDTH = C_GROUPS * C_GROUP_DIM
CHUNK = 128
A_COLS = 3 * A_WIDTH
B_COLS = 3 * B_WIDTH + B_DECAY_LORA + B_AAA_LORA + B_GATE_LORA
C_COLS = 2 * C_WIDTH
IN_COLS = A_COLS + B_COLS + C_COLS
MIX_WIDTH = A_WIDTH + B_WIDTH + C_WIDTH
N_EXPERTS = 64
TOP_K = 8
N_GROUPS = 8
TOPK_GROUPS = 4
EXPERTS_PER_GROUP = N_EXPERTS // N_GROUPS
D_EXPERT = 256
D_SHARED = 256
ROUTED_SCALE = 2.5
EXPERT_BLOCK = 256
RMS_EPS = 1e-6

kernel_name = 'hybrid_diffattn_rwkv7_gmlp_moe_block'


def rmsnorm(x, g):
    xf = x.astype(jnp.float32)
    y = xf * lax.rsqrt(jnp.mean(xf * xf, axis=-1, keepdims=True) + RMS_EPS) * g.astype(jnp.float32)
    return y.astype(x.dtype)


def layernorm(x, g, b, eps=1e-5):
    xf = x.astype(jnp.float32)
    mu = jnp.mean(xf, axis=-1, keepdims=True)
    var = jnp.mean((xf - mu) ** 2, axis=-1, keepdims=True)
    y = (xf - mu) * lax.rsqrt(var + eps) * g.astype(jnp.float32) + b.astype(jnp.float32)
    return y.astype(x.dtype)


def t5_bucket(dist):
    n = jnp.maximum(dist, 0)
    max_exact = N_BUCKETS // 2
    nf = jnp.maximum(n, 1).astype(jnp.float32)
    large = max_exact + (jnp.log(nf / max_exact) / math.log(MAX_DISTANCE / max_exact)
                         * (N_BUCKETS - max_exact)).astype(jnp.int32)
    large = jnp.minimum(large, N_BUCKETS - 1)
    return jnp.where(n < max_exact, n, large)


def diff_attention(pa, table, lam_par, subln_g, lambda_init):
    Bn, S, _ = pa.shape
    q = pa[..., :A_WIDTH].reshape(Bn, S, A_HEADS, 2, A_QK_DIM)
    k = pa[..., A_WIDTH:2 * A_WIDTH].reshape(Bn, S, A_HEADS, 2, A_QK_DIM)
    v = pa[..., 2 * A_WIDTH:].reshape(Bn, S, A_HEADS, 2 * A_QK_DIM)
    lp = lam_par.astype(jnp.float32)
    lam = jnp.exp(jnp.sum(lp[0] * lp[1])) - jnp.exp(jnp.sum(lp[2] * lp[3])) + lambda_init
    scale = A_QK_DIM ** -0.5
    outs = []
    for i in range(S // Q_BLOCK):
        q0 = i * Q_BLOCK
        L = q0 + Q_BLOCK
        dist = jnp.arange(q0, L)[:, None] - jnp.arange(L)[None, :]
        bias = jnp.transpose(table[t5_bucket(dist)].astype(jnp.float32), (2, 0, 1))
        logits = jnp.einsum('bqhmd,bkhmd->bmhqk', q[:, q0:L], k[:, :L]).astype(jnp.float32) * scale + bias
        logits = jnp.where(dist >= 0, logits, -jnp.inf)
        probs = jax.nn.softmax(logits, axis=-1)
        attn = (probs[:, 0] - lam * probs[:, 1]).astype(v.dtype)
        outs.append(jnp.einsum('bhqk,bkhe->bqhe', attn, v[:, :L]))
    o = jnp.concatenate(outs, axis=1)
    o = rmsnorm(o, subln_g) * (1.0 - lambda_init)
    return o.reshape(Bn, S, A_WIDTH)


def _rwkv7_step(state, inp):
    r, w, k, v, a, b = inp
    sa = jnp.einsum('bhij,bhj->bhi', state, a)
    state = state * w[:, :, None, :] + sa[..., None] * b[:, :, None, :] + v[..., None] * k[:, :, None, :]
    y = jnp.einsum('bhij,bhj->bhi', state, r)
    return state, y


def rwkv7_time_mix(pb, mu, w0, w2, a0, a2, g2, k_k, k_a, r_k, lnx_g, lnx_b):
    Bn, S, _ = pb.shape
    prev = jnp.pad(pb[:, :-1], ((0, 0), (1, 0), (0, 0)))
    p = pb + (prev - pb) * mu
    o1, o2, o3 = B_WIDTH, 2 * B_WIDTH, 3 * B_WIDTH
    o4 = o3 + B_DECAY_LORA
    o5 = o4 + B_AAA_LORA
    r, k, v = p[..., :o1], p[..., o1:o2], p[..., o2:o3]
    wd, ad, gd = p[..., o3:o4], p[..., o4:o5], p[..., o5:]
    w = -jax.nn.softplus(-(w0 + jnp.tanh(wd) @ w2)) - 0.5
    decay = jnp.exp(-jnp.exp(w.astype(jnp.float32)))
    a = jax.nn.sigmoid(a0 + ad @ a2)
    g = jax.nn.sigmoid(gd) @ g2
    hs = lambda t: t.reshape(Bn, S, B_HEADS, B_HEAD_DIM).astype(jnp.float32)
    kk = hs(k * k_k)
    kk = kk * lax.rsqrt(jnp.maximum(jnp.sum(kk * kk, axis=-1, keepdims=True), 1e-24))
    k = k * (1 + (a - 1) * k_a)
    rh, kh, vh, ah, wh = hs(r), hs(k), hs(v), hs(a), hs(decay)
    xs = tuple(jnp.moveaxis(t, 1, 0) for t in (rh, wh, kh, vh, -kk, kk * ah))
    state0 = jnp.zeros((Bn, B_HEADS, B_HEAD_DIM, B_HEAD_DIM), jnp.float32)
    _, y = lax.scan(_rwkv7_step, state0, xs)
    y = jnp.moveaxis(y, 0, 1)
    mean = jnp.mean(y, axis=-1, keepdims=True)
    var = jnp.mean((y - mean) ** 2, axis=-1, keepdims=True)
    y = ((y - mean) * lax.rsqrt(var + B_LNX_EPS)).reshape(Bn, S, B_WIDTH) * lnx_g.astype(jnp.float32) \
        + lnx_b.astype(jnp.float32)
    bonus = jnp.sum(rh * kh * r_k.astype(jnp.float32), axis=-1, keepdims=True) * vh
    y = y + bonus.reshape(Bn, S, B_WIDTH)
    return (y * g.astype(jnp.float32)).astype(pb.dtype)


def chunk_gmlp(pc, ln_g, ln_b, w_s, b_s):
    Bn, S, _ = pc.shape
    z = jax.nn.gelu(pc)
    u, v = z[..., :C_WIDTH], z[..., C_WIDTH:]
    v = layernorm(v, ln_g, ln_b)
    vc = v.reshape(Bn, S // CHUNK, CHUNK, C_GROUPS, C_GROUP_DIM)
    ws = w_s * jnp.tril(jnp.ones((CHUNK, CHUNK), w_s.dtype))
    sv = jnp.einsum('gts,bcsgd->bctgd', ws, vc) + jnp.transpose(b_s)[None, None, :, :, None]
    return u * sv.reshape(Bn, S, C_WIDTH)


def moe_ffn(h, w_router, e_bias, w1, w3, w2, ws1, ws3, ws2):
    Bn, S, D = h.shape
    T = Bn * S
    xt = h.reshape(T, D)
    scores = jax.nn.sigmoid((xt @ w_router).astype(jnp.float32))
    biased = scores + e_bias.astype(jnp.float32)
    grp_score = jnp.sum(lax.top_k(biased.reshape(T, N_GROUPS, EXPERTS_PER_GROUP), 2)[0], axis=-1)
    _, top_g = lax.top_k(grp_score, TOPK_GROUPS)
    gmask = jnp.any(top_g[:, :, None] == jnp.arange(N_GROUPS)[None, None, :], axis=1)
    masked = jnp.where(jnp.repeat(gmask, EXPERTS_PER_GROUP, axis=1), biased, -jnp.inf)
    _, top_idx = lax.top_k(masked, TOP_K)
    top_s = jnp.take_along_axis(scores, top_idx, axis=1)
    top_w = top_s / (jnp.sum(top_s, axis=-1, keepdims=True) + 1e-20) * ROUTED_SCALE
    M = T * TOP_K
    flat_e = top_idx.reshape(M)
    flat_w = top_w.reshape(M)
    order = jnp.argsort(flat_e)
    se = flat_e[order]
    counts = jnp.bincount(flat_e, length=N_EXPERTS)
    padded = (counts + EXPERT_BLOCK - 1) // EXPERT_BLOCK * EXPERT_BLOCK
    pad_end = jnp.cumsum(padded)
    pad_start = pad_end - padded
    grp_start = jnp.cumsum(counts) - counts
    dest = pad_start[se] + jnp.arange(M) - grp_start[se]
    n_blocks = -(-M // EXPERT_BLOCK) + N_EXPERTS
    P = n_blocks * EXPERT_BLOCK
    row_tok = jnp.zeros((P,), jnp.int32).at[dest].set((order // TOP_K).astype(jnp.int32))
    row_w = jnp.zeros((P,), jnp.float32).at[dest].set(flat_w[order])
    blk_e = jnp.minimum(jnp.searchsorted(pad_end, jnp.arange(n_blocks) * EXPERT_BLOCK, side='right'),
                        N_EXPERTS - 1)

    def expert_block(args):
        tok, e = args
        xb = xt[tok]
        hb = jax.nn.silu(xb @ w1[e]) * (xb @ w3[e])
        return hb @ w2[e]

    yb = lax.map(expert_block, (row_tok.reshape(n_blocks, EXPERT_BLOCK), blk_e))
    routed = jnp.zeros_like(xt).at[row_tok].add((yb.reshape(P, D) * row_w[:, None]).astype(xt.dtype))
    shared = (jax.nn.silu(xt @ ws1) * (xt @ ws3)) @ ws2
    return (routed + shared).reshape(Bn, S, D)


def setup_inputs(seed: int = 0) -> dict:
    key = jax.random.key(seed)
    ks = jax.random.split(key, 40)
    L, D, E, F = DEPTH, D_MODEL, N_EXPERTS, D_EXPERT
    nrm = lambda k, shape, s: jax.random.normal(k, shape, jnp.float32) * s
    gain = lambda k, shape: 1.0 + 0.01 * jax.random.normal(k, shape, jnp.float32)
    return {
        'x': nrm(ks[0], (BATCH, SEQ, D), 1.0),
        'c': nrm(ks[1], (BATCH, D), 1.0),
        'w_ada': nrm(ks[2], (L, D, 6 * D), 0.5 * D ** -0.5),
        'b_ada': nrm(ks[3], (L, 6 * D), 0.01),
        'norm_pre_mix': gain(ks[4], (L, D)),
        'norm_post_mix': gain(ks[5], (L, D)),
        'norm_pre_ffn': gain(ks[6], (L, D)),
        'norm_post_ffn': gain(ks[7], (L, D)),
        'w_in': nrm(ks[8], (L, D, IN_COLS), D ** -0.5),
        'w_out': nrm(ks[9], (L, MIX_WIDTH, D), MIX_WIDTH ** -0.5),
        'rel_bias_table': nrm(ks[10], (N_BUCKETS, A_HEADS), 0.5),
        'diff_lambda': nrm(ks[11], (L, 4, A_QK_DIM), 0.1),
        'diff_subln': gain(ks[12], (L, 2 * A_QK_DIM)),
        'rwkv_mu': jax.random.uniform(ks[13], (L, B_COLS), jnp.float32),
        'rwkv_w0': jax.random.uniform(ks[14], (L, B_WIDTH), jnp.float32, -6.0, 0.0),
        'rwkv_w2': nrm(ks[15], (L, B_DECAY_LORA, B_WIDTH), 0.1 * B_DECAY_LORA ** -0.5),
        'rwkv_a0': nrm(ks[16], (L, B_WIDTH), 0.1),
        'rwkv_a2': nrm(ks[17], (L, B_AAA_LORA, B_WIDTH), 0.1 * B_AAA_LORA ** -0.5),
        'rwkv_g2': nrm(ks[18], (L, B_GATE_LORA, B_WIDTH), B_GATE_LORA ** -0.5),
        'rwkv_k_k': 0.85 + nrm(ks[19], (L, B_WIDTH), 0.01),
        'rwkv_k_a': gain(ks[20], (L, B_WIDTH)),
        'rwkv_r_k': nrm(ks[21], (L, B_HEADS, B_HEAD_DIM), 0.1),
        'rwkv_lnx_g': gain(ks[22], (L, B_WIDTH)),
        'rwkv_lnx_b': nrm(ks[23], (L, B_WIDTH), 0.01),
        'gmlp_ln_g': gain(ks[24], (L, C_WIDTH)),
        'gmlp_ln_b': nrm(ks[25], (L, C_WIDTH), 0.01),
        'gmlp_w_s': nrm(ks[26], (L, C_GROUPS, CHUNK, CHUNK), 0.5 * CHUNK ** -0.5),
        'gmlp_b_s': gain(ks[27], (L, C_GROUPS, CHUNK)),
        'router_w': nrm(ks[28], (L, D, E), D ** -0.5),
        'router_bias': nrm(ks[29], (L, E), 0.01),
        'exp_w1': nrm(ks[30], (L, E, D, F), D ** -0.5),
        'exp_w3': nrm(ks[31], (L, E, D, F), D ** -0.5),
        'exp_w2': nrm(ks[32], (L, E, F, D), F ** -0.5),
        'shared_w1': nrm(ks[33], (L, D, D_SHARED), D ** -0.5),
        'shared_w3': nrm(ks[34], (L, D, D_SHARED), D ** -0.5),
        'shared_w2': nrm(ks[35], (L, D_SHARED, D), D_SHARED ** -0.5),
    }


def reference(x, c, w_ada, b_ada, norm_pre_mix, norm_post_mix, norm_pre_ffn, norm_post_ffn,
              w_in, w_out, rel_bias_table, diff_lambda, diff_subln,
              rwkv_mu, rwkv_w0, rwkv_w2, rwkv_a0, rwkv_a2, rwkv_g2, rwkv_k_k, rwkv_k_a, rwkv_r_k,
              rwkv_lnx_g, rwkv_lnx_b, gmlp_ln_g, gmlp_ln_b, gmlp_w_s, gmlp_b_s,
              router_w, router_bias, exp_w1, exp_w3, exp_w2, shared_w1, shared_w3, shared_w2):
    cond = jax.nn.silu(c)
    for l in range(DEPTH):
        mod = cond @ w_ada[l] + b_ada[l]
        sh1, sc1, g1, sh2, sc2, g2 = jnp.split(mod[:, None, :], 6, axis=-1)
        h = rmsnorm(x, norm_pre_mix[l]) * (1 + sc1) + sh1
        proj = h @ w_in[l]
        pa = proj[..., :A_COLS]
        pb = proj[..., A_COLS:A_COLS + B_COLS]
        pc = proj[..., A_COLS + B_COLS:]
        lambda_init = 0.8 - 0.6 * math.exp(-0.3 * l)
        ya = diff_attention(pa, rel_bias_table, diff_lambda[l], diff_subln[l], lambda_init)
        yb = rwkv7_time_mix(pb, rwkv_mu[l], rwkv_w0[l], rwkv_w2[l], rwkv_a0[l], rwkv_a2[l], rwkv_g2[l],
                            rwkv_k_k[l], rwkv_k_a[l], rwkv_r_k[l], rwkv_lnx_g[l], rwkv_lnx_b[l])
        yc = chunk_gmlp(pc, gmlp_ln_g[l], gmlp_ln_b[l], gmlp_w_s[l], gmlp_b_s[l])
        y = jnp.concatenate([ya, yb, yc], axis=-1) @ w_out[l]
        x = x + g1 * rmsnorm(y, norm_post_mix[l])
        h = rmsnorm(x, norm_pre_ffn[l]) * (1 + sc2) + sh2
        y = moe_ffn(h, router_w[l], router_bias[l], exp_w1[l], exp_w3[l], exp_w2[l],
                    shared_w1[l], shared_w3[l], shared_w2[l])
        x = x + g2 * rmsnorm(y, norm_post_ffn[l])
    return x
```

```python
import functools
import math

import jax
import jax.numpy as jnp
from jax import lax
from jax.experimental import pallas as pl
from jax.experimental.pallas import tpu as pltpu

F32 = jnp.float32
BF16 = jnp.bfloat16

A_HEADS = 4
A_QK_DIM = 64
A_HEAD_W = 2 * A_QK_DIM
A_WIDTH = A_HEADS * A_HEAD_W
N_BUCKETS = 32
MAX_DISTANCE = 128
B_HEADS = 4
B_HEAD_DIM = 64
B_WIDTH = B_HEADS * B_HEAD_DIM
B_DECAY_LORA = 64
B_AAA_LORA = 64
B_GATE_LORA = 128
B_LNX_EPS = 64e-5
C_GROUPS = 4
C_GROUP_DIM = 64
C_WIDTH = C_GROUPS * C_GROUP_DIM
CHUNK = 128
A_COLS = 3 * A_WIDTH
B_COLS = 3 * B_WIDTH + B_DECAY_LORA + B_AAA_LORA + B_GATE_LORA
C_COLS = 2 * C_WIDTH
N_EXPERTS = 64
TOP_K = 8
N_GROUPS = 8
TOPK_GROUPS = 4
EXPERTS_PER_GROUP = N_EXPERTS // N_GROUPS
ROUTED_SCALE = 2.5
EXPERT_BLOCK = 256
RMS_EPS = 1e-6
LN_EPS = 1e-5
NEG_BIG = -1e30

V7X_LANES = 128
V7X_VMEM_LIMIT_BYTES = 56 * 1024 * 1024
RWKV_CHUNK = 64

NN = (((1,), (0,)), ((), ()))
NT = (((1,), (1,)), ((), ()))
TN = (((0,), (0,)), ((), ()))


def _cparams(*sem):
    return pltpu.CompilerParams(dimension_semantics=sem, vmem_limit_bytes=V7X_VMEM_LIMIT_BYTES)


def _mm(a, b, dims=NN):
    return lax.dot_general(a.astype(BF16), b.astype(BF16), dims, preferred_element_type=F32)


def _split(a):
    hi = a.astype(BF16)
    lo = (a - hi.astype(F32)).astype(BF16)
    return hi, lo


def _mm3(a, b, dims=NN):
    ah, al = _split(a)
    bh, bl = _split(b)
    d = lambda x, y: lax.dot_general(x, y, dims, preferred_element_type=F32)
    return d(ah, bh) + d(ah, bl) + d(al, bh)


def _mm2(a, b_exact, dims=NN):
    ah, al = _split(a)
    d = lambda x: lax.dot_general(x, b_exact, dims, preferred_element_type=F32)
    return d(ah) + d(al)


def _rms(x, eps=RMS_EPS):
    return x * lax.rsqrt(jnp.mean(x * x, axis=-1, keepdims=True) + eps)


def _sigmoid(x):
    return 1.0 / (1.0 + jnp.exp(-x))


def _silu(x):
    return x * _sigmoid(x)


def _adaln_kernel(c_ref, w_ref, b_ref, o_ref):
    c = c_ref[...]
    o_ref[0] = _mm3(_silu(c), w_ref[0]) + b_ref[0]


def _adaln(c, w_ada, b_ada):
    L, D, N = w_ada.shape
    Bn = c.shape[0]
    tn = min(N, 1536)
    return pl.pallas_call(
        _adaln_kernel,
        grid=(L, N // tn),
        in_specs=[
            pl.BlockSpec((Bn, D), lambda l, j: (0, 0)),
            pl.BlockSpec((1, D, tn), lambda l, j: (l, 0, j)),
            pl.BlockSpec((1, 1, tn), lambda l, j: (l, 0, j)),
        ],
        out_specs=pl.BlockSpec((1, Bn, tn), lambda l, j: (l, 0, j)),
        out_shape=jax.ShapeDtypeStruct((L, Bn, N), F32),
        compiler_params=_cparams("arbitrary", "arbitrary"),
        name="adaln",
    )(c, w_ada, b_ada.reshape(L, 1, N))


def _inproj_kernel(x_ref, g_ref, sc_ref, sh_ref, wa_ref, wbc_ref, oa_ref, obc_ref):
    x = x_ref[0]
    h = _rms(x) * g_ref[...] * (1.0 + sc_ref[0]) + sh_ref[0]
    hb = h.astype(BF16)
    oa_ref[0] = jnp.dot(hb, wa_ref[...], preferred_element_type=F32).astype(BF16)
    obc_ref[0] = jnp.dot(hb, wbc_ref[...], preferred_element_type=F32)


def _inproj(x, g, sc, sh, wa, wbc, tm):
    Bn, S, D = x.shape
    na, nbc = wa.shape[1], wbc.shape[1]
    return pl.pallas_call(
        _inproj_kernel,
        grid=(Bn, S // tm),
        in_specs=[
            pl.BlockSpec((1, tm, D), lambda b, i: (b, i, 0)),
            pl.BlockSpec((1, D), lambda b, i: (0, 0)),
            pl.BlockSpec((1, 1, D), lambda b, i: (b, 0, 0)),
            pl.BlockSpec((1, 1, D), lambda b, i: (b, 0, 0)),
            pl.BlockSpec((D, na), lambda b, i: (0, 0)),
            pl.BlockSpec((D, nbc), lambda b, i: (0, 0)),
        ],
        out_specs=[
            pl.BlockSpec((1, tm, na), lambda b, i: (b, i, 0)),
            pl.BlockSpec((1, tm, nbc), lambda b, i: (b, i, 0)),
        ],
        out_shape=[
            jax.ShapeDtypeStruct((Bn, S, na), BF16),
            jax.ShapeDtypeStruct((Bn, S, nbc), F32),
        ],
        compiler_params=_cparams("arbitrary", "arbitrary"),
        name="inproj",
    )(x, g.reshape(1, D), sc, sh, wa, wbc)


def _t5_bucket(dist):
    n = jnp.maximum(dist, 0)
    max_exact = N_BUCKETS // 2
    nf = jnp.maximum(n, 1).astype(F32)
    large = max_exact + (jnp.log(nf / max_exact) / math.log(MAX_DISTANCE / max_exact)
                         * (N_BUCKETS - max_exact)).astype(jnp.int32)
    large = jnp.minimum(large, N_BUCKETS - 1)
    return jnp.where(n < max_exact, n, large)


def _attn_band(table, tq):
    qi = jnp.arange(tq)[:, None]
    kk = jnp.arange(2 * tq)[None, :]
    far = table[N_BUCKETS - 1].astype(F32)
    bands = []
    for off in (0, tq):
        dist = qi - kk + off
        bias = jnp.transpose(table[_t5_bucket(dist)].astype(F32), (2, 0, 1)) - far[:, None, None]
        bands.append(jnp.where(dist[None] >= 0, bias, NEG_BIG))
    band = jnp.stack(bands)
    return jnp.concatenate([band, band], axis=2)


def _attn_kernel(lam_ref, q_ref, k_ref, v_ref, band_ref, g_ref, o_ref, *, tq, lambda_init):
    i = pl.program_id(2)
    q = q_ref[0] * jnp.asarray(A_QK_DIM ** -0.5, BF16)
    lane = lax.broadcasted_iota(jnp.int32, q.shape, 1)
    zero = jnp.zeros_like(q)
    qq = jnp.concatenate([jnp.where(lane < A_QK_DIM, q, zero),
                          jnp.where(lane >= A_QK_DIM, q, zero)], axis=0)

    kb0 = pl.multiple_of(jnp.maximum(i - 1, 0) * tq, tq)
    kb = k_ref[0, pl.ds(kb0, 2 * tq), :]
    vb = v_ref[0, pl.ds(kb0, 2 * tq), :]
    s = lax.dot_general(qq, kb, NT, preferred_element_type=F32) + band_ref[0, 0]
    m = jnp.max(s, axis=-1, keepdims=True)
    p = jnp.exp(s - m)
    l = jnp.sum(p, axis=-1, keepdims=True)
    acc = jnp.dot(p.astype(BF16), vb, preferred_element_type=F32)

    def body(j, carry):
        m, l, acc = carry
        k0 = pl.multiple_of(j * tq, tq)
        kj = k_ref[0, pl.ds(k0, tq), :]
        vj = v_ref[0, pl.ds(k0, tq), :]
        s = lax.dot_general(qq, kj, NT, preferred_element_type=F32)
        m_new = jnp.maximum(m, jnp.max(s, axis=-1, keepdims=True))
        alpha = jnp.exp(m - m_new)
        p = jnp.exp(s - m_new)
        l = alpha * l + jnp.sum(p, axis=-1, keepdims=True)
        acc = alpha * acc + jnp.dot(p.astype(BF16), vj, preferred_element_type=F32)
        return m_new, l, acc

    m, l, acc = lax.fori_loop(0, jnp.maximum(i - 1, 0), body, (m, l, acc))

    lp = lam_ref[...]
    lam = (jnp.exp(jnp.sum(lp[0:1] * lp[1:2], axis=-1, keepdims=True))
           - jnp.exp(jnp.sum(lp[2:3] * lp[3:4], axis=-1, keepdims=True)) + lambda_init)
    o = acc / l
    o = o[:tq] - lam * o[tq:]
    o_ref[0] = _rms(o) * g_ref[...] * (1.0 - lambda_init)


def _diff_attention(pa, band, lam_par, subln_g, lambda_init, tq):
    Bn, S, _ = pa.shape
    W = A_HEAD_W
    kern = functools.partial(_attn_kernel, tq=tq, lambda_init=lambda_init)
    return pl.pallas_call(
        kern,
        grid=(Bn, A_HEADS, S // tq),
        in_specs=[
            pl.BlockSpec((4, A_QK_DIM), lambda b, h, i: (0, 0)),
            pl.BlockSpec((1, tq, W), lambda b, h, i: (b, i, h)),
            pl.BlockSpec((1, S, W), lambda b, h, i: (b, 0, A_HEADS + h)),
            pl.BlockSpec((1, S, W), lambda b, h, i: (b, 0, 2 * A_HEADS + h)),
            pl.BlockSpec((1, 1, 2 * tq, 2 * tq), lambda b, h, i: (jnp.minimum(i, 1), h, 0, 0)),
            pl.BlockSpec((1, W), lambda b, h, i: (0, 0)),
        ],
        out_specs=pl.BlockSpec((1, tq, W), lambda b, h, i: (b, i, h)),
        out_shape=jax.ShapeDtypeStruct((Bn, S, A_WIDTH), F32),
        compiler_params=_cparams("arbitrary", "arbitrary", "arbitrary"),
        name="diff_attn",
    )(lam_par, pa, pa, pa, band, subln_g.reshape(1, W))


def _head_ones(n):
    r = lax.broadcasted_iota(jnp.int32, (n, n), 0) // B_HEAD_DIM
    c = lax.broadcasted_iota(jnp.int32, (n, n), 1) // B_HEAD_DIM
    return (r == c).astype(BF16)


def _rwkv_prep_kernel(pb_ref, prev_ref, mu_ref, w0_ref, w2_ref, a0_ref, a2_ref, g2_ref,
                      kk_ref, ka_ref, rk_ref,
                      rt_ref, at_ref, kt_ref, bt_ref, v_ref, wc_ref, bonus_ref, g_ref, *, tm):
    i = pl.program_id(1)
    C = RWKV_CHUNK
    x = pb_ref[0]
    row = lax.broadcasted_iota(jnp.int32, x.shape, 0)
    last = prev_ref[0, 7:8, :] * (i > 0).astype(F32)
    prev = jnp.where(row == 0, last, pltpu.roll(x, 1, 0))
    p = x + (prev - x) * mu_ref[...]
    o1, o2, o3 = B_WIDTH, 2 * B_WIDTH, 3 * B_WIDTH
    r, k, v = p[:, :o1], p[:, o1:o2], p[:, o2:o3]
    lora = p[:, o3:o3 + B_DECAY_LORA + B_AAA_LORA]
    gd = p[:, o3 + B_DECAY_LORA + B_AAA_LORA:]

    z = -(w0_ref[...] + _mm3(jnp.tanh(lora), w2_ref[...]))
    softplus = jnp.maximum(z, 0.0) + jnp.log(1.0 + jnp.exp(-jnp.abs(z)))
    logw = -jnp.exp(-softplus - 0.5)
    a = _sigmoid(a0_ref[...] + _mm3(lora, a2_ref[...]))
    g_ref[0] = _mm3(_sigmoid(gd), g2_ref[...])

    ones = _head_ones(B_WIDTH)
    kk = k * kk_ref[...]
    kk = kk * lax.rsqrt(jnp.maximum(_mm2(kk * kk, ones), 1e-24))
    k2 = k * (1.0 + (a - 1.0) * ka_ref[...])
    bonus_ref[0] = _mm2(r * k2 * rk_ref[...], ones) * v

    t_in = lax.broadcasted_iota(jnp.int32, (tm, B_WIDTH), 0) % C
    cum = logw
    sh = 1
    while sh < C:
        cum = cum + jnp.where(t_in >= sh, pltpu.roll(cum, sh, 0), 0.0)
        sh *= 2
    n = tm // C
    wc_ref[0] = jnp.exp(jnp.sum(logw.reshape(n, C, B_WIDTH), axis=1))
    e_pos = jnp.exp(cum)
    e_neg = jnp.exp(-cum)
    rt_ref[0] = r * e_pos
    at_ref[0] = -kk * jnp.exp(cum - logw)
    kt_ref[0] = k2 * e_neg
    bt_ref[0] = kk * a * e_neg
    v_ref[0] = v


def _rwkv_prep(pbc, mu, w0, w2p, a0, a2p, g2, k_k, k_a, r_k, tm):
    Bn, S, _ = pbc.shape
    W = B_WIDTH
    nl = B_DECAY_LORA + B_AAA_LORA
    row = lambda a: a.reshape(1, -1)
    full = lambda shp: pl.BlockSpec(shp, lambda b, i: (0,) * len(shp))
    seq = pl.BlockSpec((1, tm, W), lambda b, i: (b, i, 0))
    seq_shape = jax.ShapeDtypeStruct((Bn, S, W), F32)
    n = tm // RWKV_CHUNK
    return pl.pallas_call(
        functools.partial(_rwkv_prep_kernel, tm=tm),
        grid=(Bn, S // tm),
        in_specs=[
            pl.BlockSpec((1, tm, B_COLS), lambda b, i: (b, i, 0)),
            pl.BlockSpec((1, 8, B_COLS), lambda b, i: (b, jnp.maximum(i * (tm // 8) - 1, 0), 0)),
            full((1, B_COLS)), full((1, W)), full((nl, W)), full((1, W)), full((nl, W)),
            full((B_GATE_LORA, W)), full((1, W)), full((1, W)), full((1, W)),
        ],
        out_specs=[seq, seq, seq, seq, seq,
                   pl.BlockSpec((1, n, W), lambda b, i: (b, i, 0)), seq, seq],
        out_shape=[seq_shape] * 5 + [jax.ShapeDtypeStruct((Bn, S // RWKV_CHUNK, W), F32)] + [seq_shape] * 2,
        compiler_params=_cparams("arbitrary", "arbitrary"),
        name="rwkv_prep",
    )(pbc, pbc, row(mu), row(w0), w2p, row(a0), a2p, g2, row(k_k), row(k_a), row(r_k))


def _rwkv_scan_kernel(rt_ref, at_ref, kt_ref, bt_ref, v_ref, wc_ref, bonus_ref, g_ref,
                      lng_ref, lnb_ref, o_ref, state, *, tt):
    C = RWKV_CHUNK
    W = B_WIDTH

    @pl.when(pl.program_id(1) == 0)
    def _():
        state[...] = jnp.zeros_like(state)

    lane_head = lax.broadcasted_iota(jnp.int32, (C, W), 1) // B_HEAD_DIM
    tt_i = lax.broadcasted_iota(jnp.int32, (C, W), 0)
    ss_i = lax.broadcasted_iota(jnp.int32, (C, W), 1) % C
    strict = tt_i > ss_i
    incl = tt_i >= ss_i
    eye = (tt_i == ss_i).astype(F32)
    ones = _head_ones(W)
    bd_mask = ones.astype(F32)

    def bd(x):
        return jnp.concatenate([jnp.where(lane_head == h, x, 0.0) for h in range(B_HEADS)], axis=0)

    def chunk(c, carry):
        r0 = pl.multiple_of(c * C, C)
        sl = pl.ds(r0, C)
        rt, at, kt, bt, v = rt_ref[0, sl, :], at_ref[0, sl, :], kt_ref[0, sl, :], bt_ref[0, sl, :], v_ref[0, sl, :]
        wc = wc_ref[0, pl.ds(c, 1), :]
        ar = jnp.concatenate([at, rt], axis=0)
        a_b = _mm3(ar, bd(bt), NT)
        a_k = _mm3(ar, bd(kt), NT)
        lo = jnp.where(strict, a_b[:C], 0.0)
        a_ak = jnp.where(strict, a_k[:C], 0.0)
        a_rb = jnp.where(incl, a_b[C:], 0.0)
        a_rk = jnp.where(incl, a_k[C:], 0.0)
        pw = lo
        tinv = eye + lo
        span = 2
        while span < C:
            pw = _mm3(pw, bd(pw))
            tinv = tinv + _mm3(tinv, bd(pw))
            span *= 2
        bdv = bd(v)
        abar = _mm3(tinv, bd(at))
        u0 = _mm3(tinv, bd(_mm3(a_ak, bdv)))
        s0 = state[...]
        a_s = _mm3(jnp.concatenate([abar, rt], axis=0), s0, NT)
        u = a_s[:C] + u0
        y = a_s[C:] + _mm3(jnp.concatenate([a_rb, a_rk], axis=1),
                           jnp.concatenate([bd(u), bdv], axis=0))
        upd = _mm3(jnp.concatenate([u, v], axis=0), jnp.concatenate([bt * wc, kt * wc], axis=0), TN)
        state[...] = s0 * wc + upd * bd_mask
        mean = _mm2(y, ones) * (1.0 / B_HEAD_DIM)
        d = y - mean
        var = _mm2(d * d, ones) * (1.0 / B_HEAD_DIM)
        yn = d * lax.rsqrt(var + B_LNX_EPS) * lng_ref[...] + lnb_ref[...]
        o_ref[0, sl, :] = (yn + bonus_ref[0, sl, :]) * g_ref[0, sl, :]
        return carry

    lax.fori_loop(0, tt // C, chunk, 0)


def _rwkv_scan(rt, at, kt, bt, v, wc, bonus, g, lnx_g, lnx_b, tt):
    Bn, S, W = rt.shape
    n = tt // RWKV_CHUNK
    seq = pl.BlockSpec((1, tt, W), lambda b, i: (b, i, 0))
    vec = pl.BlockSpec((1, W), lambda b, i: (0, 0))
    return pl.pallas_call(
        functools.partial(_rwkv_scan_kernel, tt=tt),
        grid=(Bn, S // tt),
        in_specs=[seq, seq, seq, seq, seq, pl.BlockSpec((1, n, W), lambda b, i: (b, i, 0)), seq, seq, vec, vec],
        out_specs=seq,
        out_shape=jax.ShapeDtypeStruct((Bn, S, W), F32),
        scratch_shapes=[pltpu.VMEM((B_HEADS * B_HEAD_DIM, W), F32)],
        compiler_params=_cparams("arbitrary", "arbitrary"),
        name="rwkv_scan",
    )(rt, at, kt, bt, v, wc, bonus, g, lnx_g.reshape(1, W), lnx_b.reshape(1, W))


def _gmlp_kernel(pc_ref, lng_ref, lnb_ref, ws_ref, bs_ref, o_ref, *, tm):
    x = pc_ref[0]
    z = x * (0.5 * (1.0 + jnp.tanh(math.sqrt(2.0 / math.pi) * (x + 0.044715 * (x * x * x)))))
    u, v = z[:, :C_WIDTH], z[:, C_WIDTH:]
    mu = jnp.mean(v, axis=-1, keepdims=True)
    d = v - mu
    var = jnp.mean(d * d, axis=-1, keepdims=True)
    vn = d * lax.rsqrt(var + LN_EPS) * lng_ref[...] + lnb_ref[...]
    group = lax.broadcasted_iota(jnp.int32, (CHUNK, C_WIDTH), 1) // C_GROUP_DIM
    tril = (lax.broadcasted_iota(jnp.int32, (CHUNK, CHUNK), 0)
            >= lax.broadcasted_iota(jnp.int32, (CHUNK, CHUNK), 1))
    ws = [jnp.where(tril, ws_ref[gi], 0.0).astype(BF16) for gi in range(C_GROUPS)]
    for c in range(tm // CHUNK):
        sl = slice(c * CHUNK, (c + 1) * CHUNK)
        vc = vn[sl].astype(BF16)
        sv = bs_ref[...]
        for gi in range(C_GROUPS):
            t = jnp.dot(ws[gi], vc, preferred_element_type=F32)
            sv = sv + jnp.where(group == gi, t, 0.0)
        o_ref[0, sl, :] = u[sl] * sv


def _gmlp(pbc, ln_g, ln_b, w_s, b_s, tm):
    Bn, S, _ = pbc.shape
    bs_wide = jnp.repeat(jnp.transpose(b_s), C_GROUP_DIM, axis=1)
    return pl.pallas_call(
        functools.partial(_gmlp_kernel, tm=tm),
        grid=(Bn, S // tm),
        in_specs=[
            pl.BlockSpec((1, tm, C_COLS), lambda b, i: (b, i, B_COLS // C_COLS)),
            pl.BlockSpec((1, C_WIDTH), lambda b, i: (0, 0)),
            pl.BlockSpec((1, C_WIDTH), lambda b, i: (0, 0)),
            pl.BlockSpec((C_GROUPS, CHUNK, CHUNK), lambda b, i: (0, 0, 0)),
            pl.BlockSpec((CHUNK, C_WIDTH), lambda b, i: (0, 0)),
        ],
        out_specs=pl.BlockSpec((1, tm, C_WIDTH), lambda b, i: (b, i, 0)),
        out_shape=jax.ShapeDtypeStruct((Bn, S, C_WIDTH), F32),
        compiler_params=_cparams("arbitrary", "arbitrary"),
        name="gmlp",
    )(pbc, ln_g.reshape(1, -1), ln_b.reshape(1, -1), w_s, bs_wide)


def _mid_kernel(ya_ref, yb_ref, yc_ref, x_ref, woa_ref, wob_ref, woc_ref, gpost_ref, g1_ref,
                gpre_ref, sc_ref, sh_ref, wr_ref, ws1_ref, ws3_ref, ws2_ref,
                xo_ref, h_ref, score_ref, shared_ref):
    y = (_mm(ya_ref[0], woa_ref[...]) + _mm(yb_ref[0], wob_ref[...]) + _mm(yc_ref[0], woc_ref[...]))
    xn = x_ref[0] + g1_ref[0] * (_rms(y) * gpost_ref[...])
    xo_ref[0] = xn
    h = _rms(xn) * gpre_ref[...] * (1.0 + sc_ref[0]) + sh_ref[0]
    h_ref[0] = h
    score_ref[0] = _sigmoid(_mm3(h, wr_ref[...]))
    hb = h.astype(BF16)
    t = _silu(jnp.dot(hb, ws1_ref[...], preferred_element_type=F32)) * jnp.dot(
        hb, ws3_ref[...], preferred_element_type=F32)
    shared_ref[0] = jnp.dot(t.astype(BF16), ws2_ref[...], preferred_element_type=F32)


def _mid(ya, yb, yc, x, woa, wob, woc, gpost, g1, gpre, sc, sh, wr, ws1, ws3, ws2, tm):
    Bn, S, D = x.shape
    NR = wr.shape[1]
    F = ws1.shape[1]
    seq = lambda w: pl.BlockSpec((1, tm, w), lambda b, i: (b, i, 0))
    full = lambda shp: pl.BlockSpec(shp, lambda b, i: (0,) * len(shp))
    per_b = pl.BlockSpec((1, 1, D), lambda b, i: (b, 0, 0))
    return pl.pallas_call(
        _mid_kernel,
        grid=(Bn, S // tm),
        in_specs=[seq(A_WIDTH), seq(B_WIDTH), seq(C_WIDTH), seq(D),
                  full((A_WIDTH, D)), full((B_WIDTH, D)), full((C_WIDTH, D)),
                  full((1, D)), per_b, full((1, D)), per_b, per_b,
                  full((D, NR)), full((D, F)), full((D, F)), full((F, D))],
        out_specs=[seq(D), seq(D), seq(NR), seq(D)],
        out_shape=[jax.ShapeDtypeStruct((Bn, S, D), F32), jax.ShapeDtypeStruct((Bn, S, D), F32),
                   jax.ShapeDtypeStruct((Bn, S, NR), F32), jax.ShapeDtypeStruct((Bn, S, D), F32)],
        compiler_params=_cparams("arbitrary", "arbitrary"),
        name="mid",
    )(ya, yb, yc, x, woa, wob, woc, gpost.reshape(1, D), g1, gpre.reshape(1, D), sc, sh, wr, ws1, ws3, ws2)


def _expert_kernel(blk_e_ref, n_used_ref, tok_hbm, h_hbm, w_ref, w1_ref, w3_ref, w2_ref, o_ref,
                   tok_smem, xbuf, sem_idx, sem_rows):
    i = pl.program_id(0)
    EB = EXPERT_BLOCK

    @pl.when(i < n_used_ref[0])
    def _():
        idx_copy = pltpu.make_async_copy(tok_hbm.at[pl.ds(i * EB, EB)], tok_smem, sem_idx)
        idx_copy.start()
        idx_copy.wait()

        def row_copy(r):
            return pltpu.make_async_copy(h_hbm.at[pl.ds(tok_smem[r], 1), :], xbuf.at[pl.ds(r, 1), :], sem_rows)

        def issue(r, c):
            row_copy(r).start()
            return c

        def drain(r, c):
            row_copy(r).wait()
            return c

        lax.fori_loop(0, EB, issue, 0)
        lax.fori_loop(0, EB, drain, 0)
        xb = xbuf[...].astype(BF16)
        t = _silu(jnp.dot(xb, w1_ref[0], preferred_element_type=F32)) * jnp.dot(
            xb, w3_ref[0], preferred_element_type=F32)
        yb = jnp.dot(t.astype(BF16), w2_ref[0], preferred_element_type=F32)
        rr = lax.broadcasted_iota(jnp.int32, (EB, EB), 0)
        cc = lax.broadcasted_iota(jnp.int32, (EB, EB), 1)
        w_col = jnp.sum(jnp.where(rr == cc, w_ref[0], 0.0), axis=-1, keepdims=True)
        o_ref[...] = yb * w_col

    @pl.when(i >= n_used_ref[0])
    def _():
        o_ref[...] = jnp.zeros_like(o_ref)


def _experts(blk_e, n_used, row_tok, row_w, h, w1, w3, w2):
    T, D = h.shape
    EB = EXPERT_BLOCK
    n_blocks = blk_e.shape[0]
    F = w1.shape[2]
    grid_spec = pltpu.PrefetchScalarGridSpec(
        num_scalar_prefetch=2,
        grid=(n_blocks,),
        in_specs=[
            pl.BlockSpec(memory_space=pl.ANY),
            pl.BlockSpec(memory_space=pl.ANY),
            pl.BlockSpec((1, 1, EB), lambda i, be, nu: (i, 0, 0)),
            pl.BlockSpec((1, D, F), lambda i, be, nu: (be[i], 0, 0)),
            pl.BlockSpec((1, D, F), lambda i, be, nu: (be[i], 0, 0)),
            pl.BlockSpec((1, F, D), lambda i, be, nu: (be[i], 0, 0)),
        ],
        out_specs=pl.BlockSpec((EB, D), lambda i, be, nu: (i, 0)),
        scratch_shapes=[pltpu.SMEM((EB,), jnp.int32), pltpu.VMEM((EB, D), F32),
                        pltpu.SemaphoreType.DMA(()), pltpu.SemaphoreType.DMA(())],
    )
    return pl.pallas_call(
        _expert_kernel,
        grid_spec=grid_spec,
        out_shape=jax.ShapeDtypeStruct((n_blocks * EB, D), F32),
        compiler_params=_cparams("arbitrary"),
        name="experts",
    )(blk_e, n_used, row_tok, h, row_w.reshape(n_blocks, 1, EB), w1, w3, w2)


def _route(scores, e_bias):
    T = scores.shape[0]
    biased = scores + e_bias.astype(F32)
    grp_score = jnp.sum(lax.top_k(biased.reshape(T, N_GROUPS, EXPERTS_PER_GROUP), 2)[0], axis=-1)
    _, top_g = lax.top_k(grp_score, TOPK_GROUPS)
    gmask = jnp.any(top_g[:, :, None] == jnp.arange(N_GROUPS)[None, None, :], axis=1)
    masked = jnp.where(jnp.repeat(gmask, EXPERTS_PER_GROUP, axis=1), biased, -jnp.inf)
    _, top_idx = lax.top_k(masked, TOP_K)
    top_s = jnp.take_along_axis(scores, top_idx, axis=1)
    top_w = top_s / (jnp.sum(top_s, axis=-1, keepdims=True) + 1e-20) * ROUTED_SCALE
    M = T * TOP_K
    flat_e = top_idx.reshape(M)
    flat_w = top_w.reshape(M)
    order = jnp.argsort(flat_e)
    se = flat_e[order]
    counts = jnp.bincount(flat_e, length=N_EXPERTS)
    padded = (counts + EXPERT_BLOCK - 1) // EXPERT_BLOCK * EXPERT_BLOCK
    pad_end = jnp.cumsum(padded)
    pad_start = pad_end - padded
    grp_start = jnp.cumsum(counts) - counts
    dest = pad_start[se] + jnp.arange(M) - grp_start[se]
    n_blocks = -(-M // EXPERT_BLOCK) + N_EXPERTS
    P = n_blocks * EXPERT_BLOCK
    row_tok = jnp.zeros((P,), jnp.int32).at[dest].set((order // TOP_K).astype(jnp.int32))
    row_w = jnp.zeros((P,), F32).at[dest].set(flat_w[order])
    blk_e = jnp.minimum(jnp.searchsorted(pad_end, jnp.arange(n_blocks) * EXPERT_BLOCK, side='right'),
                        N_EXPERTS - 1).astype(jnp.int32)
    n_used = (pad_end[-1] // EXPERT_BLOCK).astype(jnp.int32).reshape(1)
    return row_tok, row_w, blk_e, n_used


def _final_kernel(x_ref, routed_ref, shared_ref, gpost_ref, g2_ref, o_ref):
    y = routed_ref[0] + shared_ref[0]
    o_ref[0] = x_ref[0] + g2_ref[0] * (_rms(y) * gpost_ref[...])


def _final(x, routed, shared, gpost, g2, tm):
    Bn, S, D = x.shape
    seq = pl.BlockSpec((1, tm, D), lambda b, i: (b, i, 0))
    return pl.pallas_call(
        _final_kernel,
        grid=(Bn, S // tm),
        in_specs=[seq, seq, seq, pl.BlockSpec((1, D), lambda b, i: (0, 0)),
                  pl.BlockSpec((1, 1, D), lambda b, i: (b, 0, 0))],
        out_specs=seq,
        out_shape=jax.ShapeDtypeStruct((Bn, S, D), F32),
        compiler_params=_cparams("arbitrary", "arbitrary"),
        name="final",
    )(x, routed, shared, gpost.reshape(1, D), g2)


def kernel(x, c, w_ada, b_ada, norm_pre_mix, norm_post_mix, norm_pre_ffn, norm_post_ffn, w_in, w_out, rel_bias_table, diff_lambda, diff_subln, rwkv_mu, rwkv_w0, rwkv_w2, rwkv_a0, rwkv_a2, rwkv_g2, rwkv_k_k, rwkv_k_a, rwkv_r_k, rwkv_lnx_g, rwkv_lnx_b, gmlp_ln_g, gmlp_ln_b, gmlp_w_s, gmlp_b_s, router_w, router_bias, exp_w1, exp_w3, exp_w2, shared_w1, shared_w3, shared_w2):
    Bn, S, D = x.shape
    depth = w_ada.shape[0]
    tm = min(256, S)
    tq = min(256, S // 2)
    t_rwkv = min(512, S)

    mod = _adaln(c, w_ada, b_ada)
    band = _attn_band(rel_bias_table, tq)
    zpad = jnp.zeros((B_DECAY_LORA, B_WIDTH), F32)
    for l in range(depth):
        sh1, sc1, g1, sh2, sc2, g2 = [m.reshape(Bn, 1, D) for m in jnp.split(mod[l], 6, axis=-1)]
        w_in_b = w_in[l].astype(BF16)
        pa, pbc = _inproj(x, norm_pre_mix[l], sc1, sh1, w_in_b[:, :A_COLS], w_in_b[:, A_COLS:], tm)
        lambda_init = 0.8 - 0.6 * math.exp(-0.3 * l)
        ya = _diff_attention(pa, band, diff_lambda[l], diff_subln[l], lambda_init, tq)
        prep = _rwkv_prep(pbc, rwkv_mu[l], rwkv_w0[l], jnp.concatenate([rwkv_w2[l], zpad], axis=0),
                          rwkv_a0[l], jnp.concatenate([zpad, rwkv_a2[l]], axis=0), rwkv_g2[l],
                          rwkv_k_k[l], rwkv_k_a[l], rwkv_r_k[l].reshape(-1), t_rwkv)
        yb = _rwkv_scan(*prep, rwkv_lnx_g[l], rwkv_lnx_b[l], t_rwkv)
        yc = _gmlp(pbc, gmlp_ln_g[l], gmlp_ln_b[l], gmlp_w_s[l], gmlp_b_s[l], tm)

        w_out_b = w_out[l].astype(BF16)
        wr = jnp.pad(router_w[l], ((0, 0), (0, V7X_LANES - N_EXPERTS)))
        x, h, scores, shared = _mid(
            ya, yb, yc, x, w_out_b[:A_WIDTH], w_out_b[A_WIDTH:A_WIDTH + B_WIDTH], w_out_b[A_WIDTH + B_WIDTH:],
            norm_post_mix[l], g1, norm_pre_ffn[l], sc2, sh2, wr,
            shared_w1[l].astype(BF16), shared_w3[l].astype(BF16), shared_w2[l].astype(BF16), tm)

        T = Bn * S
        row_tok, row_w, blk_e, n_used = _route(scores.reshape(T, -1)[:, :N_EXPERTS], router_bias[l])
        yrows = _experts(blk_e, n_used, row_tok, row_w, h.reshape(T, D),
                         exp_w1[l].astype(BF16), exp_w3[l].astype(BF16), exp_w2[l].astype(BF16))
        routed = jnp.zeros((T, D), F32).at[row_tok].add(yrows)
        x = _final(x, routed.reshape(Bn, S, D), shared, norm_post_ffn[l], g2, tm)
    return x
```

```python
import functools
import math

import jax
import jax.numpy as jnp
from jax import lax
from jax.experimental import pallas as pl
from jax.experimental.pallas import tpu as pltpu

F32 = jnp.float32
BF16 = jnp.bfloat16

A_HEADS = 4
A_QK_DIM = 64
A_HEAD_W = 2 * A_QK_DIM
A_WIDTH = A_HEADS * A_HEAD_W
N_BUCKETS = 32
MAX_DISTANCE = 128
B_HEADS = 4
B_HEAD_DIM = 64
B_WIDTH = B_HEADS * B_HEAD_DIM
B_DECAY_LORA = 64
B_AAA_LORA = 64
B_GATE_LORA = 128
B_LNX_EPS = 64e-5
C_GROUPS = 4
C_GROUP_DIM = 64
C_WIDTH = C_GROUPS * C_GROUP_DIM
CHUNK = 128
A_COLS = 3 * A_WIDTH
B_COLS = 3 * B_WIDTH + B_DECAY_LORA + B_AAA_LORA + B_GATE_LORA
C_COLS = 2 * C_WIDTH
N_EXPERTS = 64
TOP_K = 8
N_GROUPS = 8
TOPK_GROUPS = 4
EXPERTS_PER_GROUP = N_EXPERTS // N_GROUPS
ROUTED_SCALE = 2.5
EXPERT_BLOCK = 256
RMS_EPS = 1e-6
LN_EPS = 1e-5
NEG_BIG = -1e30

V7X_LANES = 128
V7X_VMEM_LIMIT_BYTES = 56 * 1024 * 1024
RWKV_CHUNK = 64

NN = (((1,), (0,)), ((), ()))
NT = (((1,), (1,)), ((), ()))
TN = (((0,), (0,)), ((), ()))


def _cparams(*sem):
    return pltpu.CompilerParams(dimension_semantics=sem, vmem_limit_bytes=V7X_VMEM_LIMIT_BYTES)


def _mm(a, b, dims=NN):
    return lax.dot_general(a.astype(BF16), b.astype(BF16), dims, preferred_element_type=F32)


def _split(a):
    hi = a.astype(BF16)
    lo = (a - hi.astype(F32)).astype(BF16)
    return hi, lo


def _mm3(a, b, dims=NN):
    ah, al = _split(a)
    bh, bl = _split(b)
    d = lambda x, y: lax.dot_general(x, y, dims, preferred_element_type=F32)
    return d(ah, bh) + d(ah, bl) + d(al, bh)


def _mm2(a, b_exact, dims=NN):
    ah, al = _split(a)
    d = lambda x: lax.dot_general(x, b_exact, dims, preferred_element_type=F32)
    return d(ah) + d(al)


def _rms(x, eps=RMS_EPS):
    return x * lax.rsqrt(jnp.mean(x * x, axis=-1, keepdims=True) + eps)


def _sigmoid(x):
    return 1.0 / (1.0 + jnp.exp(-x))


def _silu(x):
    return x * _sigmoid(x)


def _adaln_kernel(c_ref, w_ref, b_ref, o_ref):
    c = c_ref[...]
    o_ref[0] = _mm3(_silu(c), w_ref[0]) + b_ref[0]


def _adaln(c, w_ada, b_ada):
    L, D, N = w_ada.shape
    Bn = c.shape[0]
    tn = min(N, 1536)
    return pl.pallas_call(
        _adaln_kernel,
        grid=(L, N // tn),
        in_specs=[
            pl.BlockSpec((Bn, D), lambda l, j: (0, 0)),
            pl.BlockSpec((1, D, tn), lambda l, j: (l, 0, j)),
            pl.BlockSpec((1, 1, tn), lambda l, j: (l, 0, j)),
        ],
        out_specs=pl.BlockSpec((1, Bn, tn), lambda l, j: (l, 0, j)),
        out_shape=jax.ShapeDtypeStruct((L, Bn, N), F32),
        compiler_params=_cparams("arbitrary", "arbitrary"),
        name="adaln",
    )(c, w_ada, b_ada.reshape(L, 1, N))


def _inproj_kernel(x_ref, g_ref, sc_ref, sh_ref, wa_ref, wbc_ref, oa_ref, obc_ref):
    x = x_ref[0]
    h = _rms(x) * g_ref[...] * (1.0 + sc_ref[0]) + sh_ref[0]
    hb = h.astype(BF16)
    oa_ref[0] = jnp.dot(hb, wa_ref[...], preferred_element_type=F32).astype(BF16)
    obc_ref[0] = jnp.dot(hb, wbc_ref[...], preferred_element_type=F32)


def _inproj(x, g, sc, sh, wa, wbc, tm):
    Bn, S, D = x.shape
    na, nbc = wa.shape[1], wbc.shape[1]
    return pl.pallas_call(
        _inproj_kernel,
        grid=(Bn, S // tm),
        in_specs=[
            pl.BlockSpec((1, tm, D), lambda b, i: (b, i, 0)),
            pl.BlockSpec((1, D), lambda b, i: (0, 0)),
            pl.BlockSpec((1, 1, D), lambda b, i: (b, 0, 0)),
            pl.BlockSpec((1, 1, D), lambda b, i: (b, 0, 0)),
            pl.BlockSpec((D, na), lambda b, i: (0, 0)),
            pl.BlockSpec((D, nbc), lambda b, i: (0, 0)),
        ],
        out_specs=[
            pl.BlockSpec((1, tm, na), lambda b, i: (b, i, 0)),
            pl.BlockSpec((1, tm, nbc), lambda b, i: (b, i, 0)),
        ],
        out_shape=[
            jax.ShapeDtypeStruct((Bn, S, na), BF16),
            jax.ShapeDtypeStruct((Bn, S, nbc), F32),
        ],
        compiler_params=_cparams("arbitrary", "arbitrary"),
        name="inproj",
    )(x, g.reshape(1, D), sc, sh, wa, wbc)


def _t5_bucket(dist):
    n = jnp.maximum(dist, 0)
    max_exact = N_BUCKETS // 2
    nf = jnp.maximum(n, 1).astype(F32)
    large = max_exact + (jnp.log(nf / max_exact) / math.log(MAX_DISTANCE / max_exact)
                         * (N_BUCKETS - max_exact)).astype(jnp.int32)
    large = jnp.minimum(large, N_BUCKETS - 1)
    return jnp.where(n < max_exact, n, large)


def _attn_band(table, tq):
    qi = jnp.arange(tq)[:, None]
    kk = jnp.arange(2 * tq)[None, :]
    far = table[N_BUCKETS - 1].astype(F32)
    bands = []
    for off in (0, tq):
        dist = qi - kk + off
        bias = jnp.transpose(table[_t5_bucket(dist)].astype(F32), (2, 0, 1)) - far[:, None, None]
        bands.append(jnp.where(dist[None] >= 0, bias, NEG_BIG))
    band = jnp.stack(bands)
    return jnp.concatenate([band, band], axis=2)


def _attn_kernel(lam_ref, q_ref, k_ref, v_ref, band_ref, g_ref, o_ref, *, tq, lambda_init):
    i = pl.program_id(2)
    q = q_ref[0] * jnp.asarray(A_QK_DIM ** -0.5, BF16)
    lane = lax.broadcasted_iota(jnp.int32, q.shape, 1)
    zero = jnp.zeros_like(q)
    qq = jnp.concatenate([jnp.where(lane < A_QK_DIM, q, zero),
                          jnp.where(lane >= A_QK_DIM, q, zero)], axis=0)

    kb0 = pl.multiple_of(jnp.maximum(i - 1, 0) * tq, tq)
    kb = k_ref[0, pl.ds(kb0, 2 * tq), :]
    vb = v_ref[0, pl.ds(kb0, 2 * tq), :]
    s = lax.dot_general(qq, kb, NT, preferred_element_type=F32) + band_ref[0, 0]
    m = jnp.max(s, axis=-1, keepdims=True)
    p = jnp.exp(s - m)
    l = jnp.sum(p, axis=-1, keepdims=True)
    acc = jnp.dot(p.astype(BF16), vb, preferred_element_type=F32)

    def body(j, carry):
        m, l, acc = carry
        k0 = pl.multiple_of(j * tq, tq)
        kj = k_ref[0, pl.ds(k0, tq), :]
        vj = v_ref[0, pl.ds(k0, tq), :]
        s = lax.dot_general(qq, kj, NT, preferred_element_type=F32)
        m_new = jnp.maximum(m, jnp.max(s, axis=-1, keepdims=True))
        alpha = jnp.exp(m - m_new)
        p = jnp.exp(s - m_new)
        l = alpha * l + jnp.sum(p, axis=-1, keepdims=True)
        acc = alpha * acc + jnp.dot(p.astype(BF16), vj, preferred_element_type=F32)
        return m_new, l, acc

    m, l, acc = lax.fori_loop(0, jnp.maximum(i - 1, 0), body, (m, l, acc))

    lp = lam_ref[...]
    lam = (jnp.exp(jnp.sum(lp[0:1] * lp[1:2], axis=-1, keepdims=True))
           - jnp.exp(jnp.sum(lp[2:3] * lp[3:4], axis=-1, keepdims=True)) + lambda_init)
    o = acc / l
    o = o[:tq] - lam * o[tq:]
    o_ref[0] = _rms(o) * g_ref[...] * (1.0 - lambda_init)


def _diff_attention(pa, band, lam_par, subln_g, lambda_init, tq):
    Bn, S, _ = pa.shape
    W = A_HEAD_W
    kern = functools.partial(_attn_kernel, tq=tq, lambda_init=lambda_init)
    return pl.pallas_call(
        kern,
        grid=(Bn, A_HEADS, S // tq),
        in_specs=[
            pl.BlockSpec((4, A_QK_DIM), lambda b, h, i: (0, 0)),
            pl.BlockSpec((1, tq, W), lambda b, h, i: (b, i, h)),
            pl.BlockSpec((1, S, W), lambda b, h, i: (b, 0, A_HEADS + h)),
            pl.BlockSpec((1, S, W), lambda b, h, i: (b, 0, 2 * A_HEADS + h)),
            pl.BlockSpec((1, 1, 2 * tq, 2 * tq), lambda b, h, i: (jnp.minimum(i, 1), h, 0, 0)),
            pl.BlockSpec((1, W), lambda b, h, i: (0, 0)),
        ],
        out_specs=pl.BlockSpec((1, tq, W), lambda b, h, i: (b, i, h)),
        out_shape=jax.ShapeDtypeStruct((Bn, S, A_WIDTH), F32),
        compiler_params=_cparams("arbitrary", "arbitrary", "arbitrary"),
        name="diff_attn",
    )(lam_par, pa, pa, pa, band, subln_g.reshape(1, W))


def _head_ones(n):
    r = lax.broadcasted_iota(jnp.int32, (n, n), 0) // B_HEAD_DIM
    c = lax.broadcasted_iota(jnp.int32, (n, n), 1) // B_HEAD_DIM
    return (r == c).astype(BF16)


def _rwkv_prep_kernel(pb_ref, prev_ref, mu_ref, w0_ref, w2_ref, a0_ref, a2_ref, g2_ref,
                      kk_ref, ka_ref, rk_ref,
                      rt_ref, at_ref, kt_ref, bt_ref, v_ref, wc_ref, bonus_ref, g_ref, *, tm):
    i = pl.program_id(1)
    C = RWKV_CHUNK
    x = pb_ref[0]
    row = lax.broadcasted_iota(jnp.int32, x.shape, 0)
    last = prev_ref[0, 7:8, :] * (i > 0).astype(F32)
    prev = jnp.where(row == 0, last, pltpu.roll(x, 1, 0))
    p = x + (prev - x) * mu_ref[...]
    o1, o2, o3 = B_WIDTH, 2 * B_WIDTH, 3 * B_WIDTH
    r, k, v = p[:, :o1], p[:, o1:o2], p[:, o2:o3]
    lora = p[:, o3:o3 + B_DECAY_LORA + B_AAA_LORA]
    gd = p[:, o3 + B_DECAY_LORA + B_AAA_LORA:]

    z = -(w0_ref[...] + _mm3(jnp.tanh(lora), w2_ref[...]))
    softplus = jnp.maximum(z, 0.0) + jnp.log(1.0 + jnp.exp(-jnp.abs(z)))
    logw = -jnp.exp(-softplus - 0.5)
    a = _sigmoid(a0_ref[...] + _mm3(lora, a2_ref[...]))
    g_ref[0] = _mm3(_sigmoid(gd), g2_ref[...])

    ones = _head_ones(B_WIDTH)
    kk = k * kk_ref[...]
    kk = kk * lax.rsqrt(jnp.maximum(_mm2(kk * kk, ones), 1e-24))
    k2 = k * (1.0 + (a - 1.0) * ka_ref[...])
    bonus_ref[0] = _mm2(r * k2 * rk_ref[...], ones) * v

    t_in = lax.broadcasted_iota(jnp.int32, (tm, B_WIDTH), 0) % C
    cum = logw
    sh = 1
    while sh < C:
        cum = cum + jnp.where(t_in >= sh, pltpu.roll(cum, sh, 0), 0.0)
        sh *= 2
    n = tm // C
    wc_ref[0] = jnp.exp(jnp.sum(logw.reshape(n, C, B_WIDTH), axis=1))
    e_pos = jnp.exp(cum)
    e_neg = jnp.exp(-cum)
    rt_ref[0] = r * e_pos
    at_ref[0] = -kk * jnp.exp(cum - logw)
    kt_ref[0] = k2 * e_neg
    bt_ref[0] = kk * a * e_neg
    v_ref[0] = v


def _rwkv_prep(pbc, mu, w0, w2p, a0, a2p, g2, k_k, k_a, r_k, tm):
    Bn, S, _ = pbc.shape
    W = B_WIDTH
    nl = B_DECAY_LORA + B_AAA_LORA
    row = lambda a: a.reshape(1, -1)
    full = lambda shp: pl.BlockSpec(shp, lambda b, i: (0,) * len(shp))
    seq = pl.BlockSpec((1, tm, W), lambda b, i: (b, i, 0))
    seq_shape = jax.ShapeDtypeStruct((Bn, S, W), F32)
    n = tm // RWKV_CHUNK
    return pl.pallas_call(
        functools.partial(_rwkv_prep_kernel, tm=tm),
        grid=(Bn, S // tm),
        in_specs=[
            pl.BlockSpec((1, tm, B_COLS), lambda b, i: (b, i, 0)),
            pl.BlockSpec((1, 8, B_COLS), lambda b, i: (b, jnp.maximum(i * (tm // 8) - 1, 0), 0)),
            full((1, B_COLS)), full((1, W)), full((nl, W)), full((1, W)), full((nl, W)),
            full((B_GATE_LORA, W)), full((1, W)), full((1, W)), full((1, W)),
        ],
        out_specs=[seq, seq, seq, seq, seq,
                   pl.BlockSpec((1, n, W), lambda b, i: (b, i, 0)), seq, seq],
        out_shape=[seq_shape] * 5 + [jax.ShapeDtypeStruct((Bn, S // RWKV_CHUNK, W), F32)] + [seq_shape] * 2,
        compiler_params=_cparams("arbitrary", "arbitrary"),
        name="rwkv_prep",
    )(pbc, pbc, row(mu), row(w0), w2p, row(a0), a2p, g2, row(k_k), row(k_a), row(r_k))


def _rwkv_scan_kernel(rt_ref, at_ref, kt_ref, bt_ref, v_ref, wc_ref, bonus_ref, g_ref,
                      lng_ref, lnb_ref, o_ref, state, *, tt):
    C = RWKV_CHUNK
    W = B_WIDTH

    @pl.when(pl.program_id(1) == 0)
    def _():
        state[...] = jnp.zeros_like(state)

    lane_head = lax.broadcasted_iota(jnp.int32, (C, W), 1) // B_HEAD_DIM
    tt_i = lax.broadcasted_iota(jnp.int32, (C, W), 0)
    ss_i = lax.broadcasted_iota(jnp.int32, (C, W), 1) % C
    strict = tt_i > ss_i
    incl = tt_i >= ss_i
    eye = (tt_i == ss_i).astype(F32)
    ones = _head_ones(W)
    bd_mask = ones.astype(F32)

    def bd(x):
        return jnp.concatenate([jnp.where(lane_head == h, x, 0.0) for h in range(B_HEADS)], axis=0)

    def chunk(c, carry):
        r0 = pl.multiple_of(c * C, C)
        sl = pl.ds(r0, C)
        rt, at, kt, bt, v = rt_ref[0, sl, :], at_ref[0, sl, :], kt_ref[0, sl, :], bt_ref[0, sl, :], v_ref[0, sl, :]
        wc = wc_ref[0, pl.ds(c, 1), :]
        ar = jnp.concatenate([at, rt], axis=0)
        a_b = _mm3(ar, bd(bt), NT)
        a_k = _mm3(ar, bd(kt), NT)
        lo = jnp.where(strict, a_b[:C], 0.0)
        a_ak = jnp.where(strict, a_k[:C], 0.0)
        a_rb = jnp.where(incl, a_b[C:], 0.0)
        a_rk = jnp.where(incl, a_k[C:], 0.0)
        pw = lo
        tinv = eye + lo
        span = 2
        while span < C:
            pw = _mm3(pw, bd(pw))
            tinv = tinv + _mm3(tinv, bd(pw))
            span *= 2
        bdv = bd(v)
        abar = _mm3(tinv, bd(at))
        u0 = _mm3(tinv, bd(_mm3(a_ak, bdv)))
        s0 = state[...]
        a_s = _mm3(jnp.concatenate([abar, rt], axis=0), s0, NT)
        u = a_s[:C] + u0
        y = a_s[C:] + _mm3(jnp.concatenate([a_rb, a_rk], axis=1),
                           jnp.concatenate([bd(u), bdv], axis=0))
        upd = _mm3(jnp.concatenate([u, v], axis=0), jnp.concatenate([bt * wc, kt * wc], axis=0), TN)
        state[...] = s0 * wc + upd * bd_mask
        mean = _mm2(y, ones) * (1.0 / B_HEAD_DIM)
        d = y - mean
        var = _mm2(d * d, ones) * (1.0 / B_HEAD_DIM)
        yn = d * lax.rsqrt(var + B_LNX_EPS) * lng_ref[...] + lnb_ref[...]
        o_ref[0, sl, :] = (yn + bonus_ref[0, sl, :]) * g_ref[0, sl, :]
        return carry

    lax.fori_loop(0, tt // C, chunk, 0)


def _rwkv_scan(rt, at, kt, bt, v, wc, bonus, g, lnx_g, lnx_b, tt):
    Bn, S, W = rt.shape
    n = tt // RWKV_CHUNK
    seq = pl.BlockSpec((1, tt, W), lambda b, i: (b, i, 0))
    vec = pl.BlockSpec((1, W), lambda b, i: (0, 0))
    return pl.pallas_call(
        functools.partial(_rwkv_scan_kernel, tt=tt),
        grid=(Bn, S // tt),
        in_specs=[seq, seq, seq, seq, seq, pl.BlockSpec((1, n, W), lambda b, i: (b, i, 0)), seq, seq, vec, vec],
        out_specs=seq,
        out_shape=jax.ShapeDtypeStruct((Bn, S, W), F32),
        scratch_shapes=[pltpu.VMEM((B_HEADS * B_HEAD_DIM, W), F32)],
        compiler_params=_cparams("arbitrary", "arbitrary"),
        name="rwkv_scan",
    )(rt, at, kt, bt, v, wc, bonus, g, lnx_g.reshape(1, W), lnx_b.reshape(1, W))


def _gmlp_kernel(pc_ref, lng_ref, lnb_ref, ws_ref, bs_ref, o_ref, *, tm):
    x = pc_ref[0]
    z = x * (0.5 * (1.0 + jnp.tanh(math.sqrt(2.0 / math.pi) * (x + 0.044715 * (x * x * x)))))
    u, v = z[:, :C_WIDTH], z[:, C_WIDTH:]
    mu = jnp.mean(v, axis=-1, keepdims=True)
    d = v - mu
    var = jnp.mean(d * d, axis=-1, keepdims=True)
    vn = d * lax.rsqrt(var + LN_EPS) * lng_ref[...] + lnb_ref[...]
    group = lax.broadcasted_iota(jnp.int32, (CHUNK, C_WIDTH), 1) // C_GROUP_DIM
    tril = (lax.broadcasted_iota(jnp.int32, (CHUNK, CHUNK), 0)
            >= lax.broadcasted_iota(jnp.int32, (CHUNK, CHUNK), 1))
    ws = [jnp.where(tril, ws_ref[gi], 0.0).astype(BF16) for gi in range(C_GROUPS)]
    for c in range(tm // CHUNK):
        sl = slice(c * CHUNK, (c + 1) * CHUNK)
        vc = vn[sl].astype(BF16)
        sv = bs_ref[...]
        for gi in range(C_GROUPS):
            t = jnp.dot(ws[gi], vc, preferred_element_type=F32)
            sv = sv + jnp.where(group == gi, t, 0.0)
        o_ref[0, sl, :] = u[sl] * sv


def _gmlp(pbc, ln_g, ln_b, w_s, b_s, tm):
    Bn, S, _ = pbc.shape
    bs_wide = jnp.repeat(jnp.transpose(b_s), C_GROUP_DIM, axis=1)
    return pl.pallas_call(
        functools.partial(_gmlp_kernel, tm=tm),
        grid=(Bn, S // tm),
        in_specs=[
            pl.BlockSpec((1, tm, C_COLS), lambda b, i: (b, i, B_COLS // C_COLS)),
            pl.BlockSpec((1, C_WIDTH), lambda b, i: (0, 0)),
            pl.BlockSpec((1, C_WIDTH), lambda b, i: (0, 0)),
            pl.BlockSpec((C_GROUPS, CHUNK, CHUNK), lambda b, i: (0, 0, 0)),
            pl.BlockSpec((CHUNK, C_WIDTH), lambda b, i: (0, 0)),
        ],
        out_specs=pl.BlockSpec((1, tm, C_WIDTH), lambda b, i: (b, i, 0)),
        out_shape=jax.ShapeDtypeStruct((Bn, S, C_WIDTH), F32),
        compiler_params=_cparams("arbitrary", "arbitrary"),
        name="gmlp",
    )(pbc, ln_g.reshape(1, -1), ln_b.reshape(1, -1), w_s, bs_wide)


def _mid_kernel(ya_ref, yb_ref, yc_ref, x_ref, woa_ref, wob_ref, woc_ref, gpost_ref, g1_ref,
                gpre_ref, sc_ref, sh_ref, wr_ref, ws1_ref, ws3_ref, ws2_ref,
                xo_ref, h_ref, score_ref, shared_ref):
    y = (_mm(ya_ref[0], woa_ref[...]) + _mm(yb_ref[0], wob_ref[...]) + _mm(yc_ref[0], woc_ref[...]))
    xn = x_ref[0] + g1_ref[0] * (_rms(y) * gpost_ref[...])
    xo_ref[0] = xn
    h = _rms(xn) * gpre_ref[...] * (1.0 + sc_ref[0]) + sh_ref[0]
    h_ref[0] = h
    score_ref[0] = _sigmoid(_mm3(wr_ref[...], h, NT))
    hb = h.astype(BF16)
    t = _silu(jnp.dot(hb, ws1_ref[...], preferred_element_type=F32)) * jnp.dot(
        hb, ws3_ref[...], preferred_element_type=F32)
    shared_ref[0] = jnp.dot(t.astype(BF16), ws2_ref[...], preferred_element_type=F32)


def _mid(ya, yb, yc, x, woa, wob, woc, gpost, g1, gpre, sc, sh, wr, ws1, ws3, ws2, tm):
    Bn, S, D = x.shape
    NR = wr.shape[0]
    F = ws1.shape[1]
    seq = lambda w: pl.BlockSpec((1, tm, w), lambda b, i: (b, i, 0))
    full = lambda shp: pl.BlockSpec(shp, lambda b, i: (0,) * len(shp))
    per_b = pl.BlockSpec((1, 1, D), lambda b, i: (b, 0, 0))
    return pl.pallas_call(
        _mid_kernel,
        grid=(Bn, S // tm),
        in_specs=[seq(A_WIDTH), seq(B_WIDTH), seq(C_WIDTH), seq(D),
                  full((A_WIDTH, D)), full((B_WIDTH, D)), full((C_WIDTH, D)),
                  full((1, D)), per_b, full((1, D)), per_b, per_b,
                  full((NR, D)), full((D, F)), full((D, F)), full((F, D))],
        out_specs=[seq(D), seq(D), pl.BlockSpec((1, NR, tm), lambda b, i: (b, 0, i)), seq(D)],
        out_shape=[jax.ShapeDtypeStruct((Bn, S, D), F32), jax.ShapeDtypeStruct((Bn, S, D), F32),
                   jax.ShapeDtypeStruct((Bn, NR, S), F32), jax.ShapeDtypeStruct((Bn, S, D), F32)],
        compiler_params=_cparams("arbitrary", "arbitrary"),
        name="mid",
    )(ya, yb, yc, x, woa, wob, woc, gpost.reshape(1, D), g1, gpre.reshape(1, D), sc, sh, wr, ws1, ws3, ws2)


def _first_argmax(vals, iota, n):
    m = jnp.max(vals, axis=0, keepdims=True)
    idx = jnp.min(jnp.where(vals == m, iota, n), axis=0, keepdims=True)
    return m, idx


def _route_kernel(sc_ref, bias_ref, e_ref, w_ref, r_ref, cnt_ref, carry, *, tm):
    @pl.when((pl.program_id(0) == 0) & (pl.program_id(1) == 0))
    def _():
        carry[...] = jnp.zeros_like(carry)

    G = EXPERTS_PER_GROUP
    s = sc_ref[0]
    biased = s + bias_ref[...]
    neg_inf = jnp.float32(-jnp.inf)
    io8 = lax.broadcasted_iota(jnp.int32, (G, tm), 0)
    gs_rows = []
    for g in range(N_GROUPS):
        blk = biased[g * G:(g + 1) * G]
        m1, i1 = _first_argmax(blk, io8, G)
        m2 = jnp.max(jnp.where(io8 == i1, neg_inf, blk), axis=0, keepdims=True)
        gs_rows.append(m1 + m2)
    gs = jnp.concatenate(gs_rows, axis=0)
    gio = lax.broadcasted_iota(jnp.int32, (N_GROUPS, tm), 0)
    gsel = jnp.zeros((N_GROUPS, tm), jnp.bool_)
    for _ in range(TOPK_GROUPS):
        _, gi = _first_argmax(gs, gio, N_GROUPS)
        pick = gio == gi
        gsel = gsel | pick
        gs = jnp.where(pick, neg_inf, gs)
    masked = jnp.concatenate(
        [jnp.where(gsel[g:g + 1], biased[g * G:(g + 1) * G], neg_inf) for g in range(N_GROUPS)], axis=0)

    eio = lax.broadcasted_iota(jnp.int32, (N_EXPERTS, tm), 0)
    picks, e_rows, s_rows = [], [], []
    for _ in range(TOP_K):
        _, ei = _first_argmax(masked, eio, N_EXPERTS)
        pick = eio == ei
        picks.append(pick)
        e_rows.append(ei)
        s_rows.append(jnp.sum(jnp.where(pick, s, 0.0), axis=0, keepdims=True))
        masked = jnp.where(pick, neg_inf, masked)
    top_s = jnp.concatenate(s_rows, axis=0)
    w_ref[...] = top_s / (jnp.sum(top_s, axis=0, keepdims=True) + 1e-20) * ROUTED_SCALE
    e_ref[...] = jnp.concatenate(e_rows, axis=0)

    sel = jnp.zeros((N_EXPERTS, tm), F32)
    for pick in picks:
        sel = sel + pick.astype(F32)
    before = (lax.broadcasted_iota(jnp.int32, (tm, tm), 0) < lax.broadcasted_iota(jnp.int32, (tm, tm), 1))
    pos = carry[...] + jnp.dot(sel.astype(BF16), before.astype(BF16), preferred_element_type=F32)
    r_ref[...] = jnp.concatenate(
        [jnp.sum(jnp.where(pick, pos, 0.0), axis=0, keepdims=True) for pick in picks], axis=0).astype(jnp.int32)
    total = carry[...] + jnp.sum(sel, axis=1, keepdims=True)
    carry[...] = total
    cnt_ref[...] = jnp.broadcast_to(total, cnt_ref.shape).astype(jnp.int32)


def _route(scores_t, e_bias, tm):
    Bn, _, S = scores_t.shape
    T = Bn * S
    nt = S // tm
    tok = pl.BlockSpec((TOP_K, tm), lambda b, i: (0, b * nt + i))
    return pl.pallas_call(
        functools.partial(_route_kernel, tm=tm),
        grid=(Bn, nt),
        in_specs=[pl.BlockSpec((1, N_EXPERTS, tm), lambda b, i: (b, 0, i)),
                  pl.BlockSpec((N_EXPERTS, 1), lambda b, i: (0, 0))],
        out_specs=[tok, tok, tok, pl.BlockSpec((N_EXPERTS, V7X_LANES), lambda b, i: (0, 0))],
        out_shape=[jax.ShapeDtypeStruct((TOP_K, T), jnp.int32), jax.ShapeDtypeStruct((TOP_K, T), F32),
                   jax.ShapeDtypeStruct((TOP_K, T), jnp.int32),
                   jax.ShapeDtypeStruct((N_EXPERTS, V7X_LANES), jnp.int32)],
        scratch_shapes=[pltpu.VMEM((N_EXPERTS, 1), F32)],
        compiler_params=_cparams("arbitrary", "arbitrary"),
        name="route",
    )(scores_t, e_bias.reshape(N_EXPERTS, 1))


def _dest_kernel(start_ref, e_ref, r_ref, o_ref):
    e = e_ref[...]
    acc = r_ref[...]
    for ex in range(N_EXPERTS):
        acc = acc + jnp.where(e == ex, start_ref[ex], 0)
    o_ref[0] = acc


def _dest_rows(pad_start, eidx, rank, tt):
    K_, T = eidx.shape
    grid_spec = pltpu.PrefetchScalarGridSpec(
        num_scalar_prefetch=1,
        grid=(T // tt,),
        in_specs=[pl.BlockSpec((K_, tt), lambda i, st: (0, i)), pl.BlockSpec((K_, tt), lambda i, st: (0, i))],
        out_specs=pl.BlockSpec((1, K_, tt), lambda i, st: (i, 0, 0)),
    )
    return pl.pallas_call(
        _dest_kernel,
        grid_spec=grid_spec,
        out_shape=jax.ShapeDtypeStruct((T // tt, K_, tt), jnp.int32),
        compiler_params=_cparams("arbitrary"),
        name="dest_rows",
    )(pad_start, eidx, rank)


def _dispatch_kernel(fill_ref, dest_hbm, h_hbm, xs_hbm, dest_smem, zbuf, sem_idx, sem_rows, sem_zero, *, tt):
    i = pl.program_id(0)
    EB = EXPERT_BLOCK

    def zero_copy(ex):
        return pltpu.make_async_copy(zbuf, xs_hbm.at[pl.ds(pl.multiple_of(fill_ref[ex], EB), EB), :], sem_zero)

    @pl.when(i == 0)
    def _():
        zbuf[...] = jnp.zeros_like(zbuf)

        def z_start(ex, c):
            @pl.when(fill_ref[ex] >= 0)
            def _():
                zero_copy(ex).start()
            return c

        def z_wait(ex, c):
            @pl.when(fill_ref[ex] >= 0)
            def _():
                zero_copy(ex).wait()
            return c

        lax.fori_loop(0, N_EXPERTS, z_start, 0)
        lax.fori_loop(0, N_EXPERTS, z_wait, 0)

    idx_copy = pltpu.make_async_copy(dest_hbm.at[i], dest_smem, sem_idx)
    idx_copy.start()
    idx_copy.wait()
    t0 = i * tt

    def row_copy(t, k):
        return pltpu.make_async_copy(h_hbm.at[pl.ds(t0 + t, 1), :], xs_hbm.at[pl.ds(dest_smem[k, t], 1), :], sem_rows)

    def issue(t, c):
        for k in range(TOP_K):
            row_copy(t, k).start()
        return c

    def drain(t, c):
        for k in range(TOP_K):
            row_copy(t, k).wait()
        return c

    lax.fori_loop(0, tt, issue, 0)
    lax.fori_loop(0, tt, drain, 0)


def _dispatch(fill_blocks, dest, h, n_rows):
    T, D = h.shape
    nt, K_, tt = dest.shape
    grid_spec = pltpu.PrefetchScalarGridSpec(
        num_scalar_prefetch=1,
        grid=(nt,),
        in_specs=[pl.BlockSpec(memory_space=pl.ANY), pl.BlockSpec(memory_space=pl.ANY)],
        out_specs=pl.BlockSpec(memory_space=pl.ANY),
        scratch_shapes=[pltpu.SMEM((K_, tt), jnp.int32), pltpu.VMEM((EXPERT_BLOCK, D), F32),
                        pltpu.SemaphoreType.DMA(()), pltpu.SemaphoreType.DMA(()), pltpu.SemaphoreType.DMA(())],
    )
    return pl.pallas_call(
        functools.partial(_dispatch_kernel, tt=tt),
        grid_spec=grid_spec,
        out_shape=jax.ShapeDtypeStruct((n_rows, D), F32),
        compiler_params=_cparams("arbitrary"),
        name="dispatch",
    )(fill_blocks, dest, h)


def _expert_kernel(blk_e_ref, n_used_ref, x_ref, w1_ref, w3_ref, w2_ref, o_ref):
    @pl.when(pl.program_id(0) < n_used_ref[0])
    def _():
        xb = x_ref[...].astype(BF16)
        t = _silu(jnp.dot(xb, w1_ref[0], preferred_element_type=F32)) * jnp.dot(
            xb, w3_ref[0], preferred_element_type=F32)
        o_ref[...] = jnp.dot(t.astype(BF16), w2_ref[0], preferred_element_type=F32)


def _experts(blk_e, n_used, xs, w1, w3, w2):
    P, D = xs.shape
    EB = EXPERT_BLOCK
    n_blocks = blk_e.shape[0]
    F = w1.shape[2]
    rows = pl.BlockSpec((EB, D), lambda i, be, nu: (jnp.minimum(i, nu[0] - 1), 0))
    grid_spec = pltpu.PrefetchScalarGridSpec(
        num_scalar_prefetch=2,
        grid=(n_blocks,),
        in_specs=[
            rows,
            pl.BlockSpec((1, D, F), lambda i, be, nu: (be[i], 0, 0)),
            pl.BlockSpec((1, D, F), lambda i, be, nu: (be[i], 0, 0)),
            pl.BlockSpec((1, F, D), lambda i, be, nu: (be[i], 0, 0)),
        ],
        out_specs=rows,
    )
    return pl.pallas_call(
        _expert_kernel,
        grid_spec=grid_spec,
        out_shape=jax.ShapeDtypeStruct((P, D), F32),
        compiler_params=_cparams("arbitrary"),
        name="experts",
    )(blk_e, n_used, xs, w1, w3, w2)


def _block_layout(counts, n_blocks):
    EB = EXPERT_BLOCK
    padded = (counts + EB - 1) // EB * EB
    pad_end = jnp.cumsum(padded)
    pad_start = pad_end - padded
    blk_e = jnp.minimum(jnp.searchsorted(pad_end, jnp.arange(n_blocks) * EB, side='right'),
                        N_EXPERTS - 1).astype(jnp.int32)
    n_used = (pad_end[-1] // EB).astype(jnp.int32).reshape(1)
    fill = jnp.where(padded > counts, pad_end - EB, -1).astype(jnp.int32)
    return pad_start.astype(jnp.int32), blk_e, n_used, fill


def _combine_kernel(dest_hbm, ys_hbm, w_ref, x_ref, shared_ref, gpost_ref, g2_ref, o_ref,
                    dest_smem, buf, sem_idx, sem_rows, *, tt, nt):
    i = pl.program_id(0) * nt + pl.program_id(1)
    idx_copy = pltpu.make_async_copy(dest_hbm.at[i], dest_smem, sem_idx)
    idx_copy.start()
    idx_copy.wait()

    def row_copy(t, k):
        return pltpu.make_async_copy(ys_hbm.at[pl.ds(dest_smem[k, t], 1), :], buf.at[k, pl.ds(t, 1), :], sem_rows)

    def issue(t, c):
        for k in range(TOP_K):
            row_copy(t, k).start()
        return c

    def drain(t, c):
        for k in range(TOP_K):
            row_copy(t, k).wait()
        return c

    lax.fori_loop(0, tt, issue, 0)
    lax.fori_loop(0, tt, drain, 0)
    w = w_ref[...]
    y = shared_ref[0]
    for k in range(TOP_K):
        y = y + w[:, k:k + 1] * buf[k]
    o_ref[0] = x_ref[0] + g2_ref[0] * (_rms(y) * gpost_ref[...])


def _combine(dest, ys, w_tok, x, shared, gpost, g2):
    Bn, S, D = x.shape
    _, K_, tt = dest.shape
    nt = S // tt
    seq = pl.BlockSpec((1, tt, D), lambda b, i: (b, i, 0))
    return pl.pallas_call(
        functools.partial(_combine_kernel, tt=tt, nt=nt),
        grid=(Bn, nt),
        in_specs=[pl.BlockSpec(memory_space=pl.ANY), pl.BlockSpec(memory_space=pl.ANY),
                  pl.BlockSpec((tt, K_), lambda b, i: (b * nt + i, 0)), seq, seq,
                  pl.BlockSpec((1, D), lambda b, i: (0, 0)), pl.BlockSpec((1, 1, D), lambda b, i: (b, 0, 0))],
        out_specs=seq,
        out_shape=jax.ShapeDtypeStruct((Bn, S, D), F32),
        scratch_shapes=[pltpu.SMEM((K_, tt), jnp.int32), pltpu.VMEM((K_, tt, D), F32),
                        pltpu.SemaphoreType.DMA(()), pltpu.SemaphoreType.DMA(())],
        compiler_params=_cparams("arbitrary", "arbitrary"),
        name="combine",
    )(dest, ys, w_tok, x, shared, gpost.reshape(1, D), g2)


def kernel(x, c, w_ada, b_ada, norm_pre_mix, norm_post_mix, norm_pre_ffn, norm_post_ffn, w_in, w_out, rel_bias_table, diff_lambda, diff_subln, rwkv_mu, rwkv_w0, rwkv_w2, rwkv_a0, rwkv_a2, rwkv_g2, rwkv_k_k, rwkv_k_a, rwkv_r_k, rwkv_lnx_g, rwkv_lnx_b, gmlp_ln_g, gmlp_ln_b, gmlp_w_s, gmlp_b_s, router_w, router_bias, exp_w1, exp_w3, exp_w2, shared_w1, shared_w3, shared_w2):
    Bn, S, D = x.shape
    depth = w_ada.shape[0]
    tm = min(256, S)
    tq = min(256, S // 2)
    t_rwkv = min(512, S)

    mod = _adaln(c, w_ada, b_ada)
    band = _attn_band(rel_bias_table, tq)
    zpad = jnp.zeros((B_DECAY_LORA, B_WIDTH), F32)
    for l in range(depth):
        sh1, sc1, g1, sh2, sc2, g2 = [m.reshape(Bn, 1, D) for m in jnp.split(mod[l], 6, axis=-1)]
        w_in_b = w_in[l].astype(BF16)
        pa, pbc = _inproj(x, norm_pre_mix[l], sc1, sh1, w_in_b[:, :A_COLS], w_in_b[:, A_COLS:], tm)
        lambda_init = 0.8 - 0.6 * math.exp(-0.3 * l)
        ya = _diff_attention(pa, band, diff_lambda[l], diff_subln[l], lambda_init, tq)
        prep = _rwkv_prep(pbc, rwkv_mu[l], rwkv_w0[l], jnp.concatenate([rwkv_w2[l], zpad], axis=0),
                          rwkv_a0[l], jnp.concatenate([zpad, rwkv_a2[l]], axis=0), rwkv_g2[l],
                          rwkv_k_k[l], rwkv_k_a[l], rwkv_r_k[l].reshape(-1), t_rwkv)
        yb = _rwkv_scan(*prep, rwkv_lnx_g[l], rwkv_lnx_b[l], t_rwkv)
        yc = _gmlp(pbc, gmlp_ln_g[l], gmlp_ln_b[l], gmlp_w_s[l], gmlp_b_s[l], tm)

        w_out_b = w_out[l].astype(BF16)
        wr_t = jnp.pad(jnp.transpose(router_w[l]), ((0, V7X_LANES - N_EXPERTS), (0, 0)))
        x, h, scores_t, shared = _mid(
            ya, yb, yc, x, w_out_b[:A_WIDTH], w_out_b[A_WIDTH:A_WIDTH + B_WIDTH], w_out_b[A_WIDTH + B_WIDTH:],
            norm_post_mix[l], g1, norm_pre_ffn[l], sc2, sh2, wr_t,
            shared_w1[l].astype(BF16), shared_w3[l].astype(BF16), shared_w2[l].astype(BF16), tm)

        T = Bn * S
        n_blocks = -(-T * TOP_K // EXPERT_BLOCK) + N_EXPERTS
        eidx, wgt, rank, cnt = _route(scores_t, router_bias[l], tm)
        pad_start, blk_e, n_used, fill = _block_layout(cnt[:, 0], n_blocks)
        dest = _dest_rows(pad_start, eidx, rank, tm)
        xs = _dispatch(fill, dest, h.reshape(T, D), n_blocks * EXPERT_BLOCK)
        ys = _experts(blk_e, n_used, xs, exp_w1[l].astype(BF16), exp_w3[l].astype(BF16), exp_w2[l].astype(BF16))
        x = _combine(dest, ys, jnp.transpose(wgt), x, shared, norm_post_ffn[l], g2)
    return x
```

```python
import functools
import math

import jax
import jax.numpy as jnp
from jax import lax
from jax.experimental import pallas as pl
from jax.experimental.pallas import tpu as pltpu

F32 = jnp.float32
BF16 = jnp.bfloat16

A_HEADS = 4
A_QK_DIM = 64
A_HEAD_W = 2 * A_QK_DIM
A_WIDTH = A_HEADS * A_HEAD_W
N_BUCKETS = 32
MAX_DISTANCE = 128
B_HEADS = 4
B_HEAD_DIM = 64
B_WIDTH = B_HEADS * B_HEAD_DIM
B_DECAY_LORA = 64
B_AAA_LORA = 64
B_GATE_LORA = 128
B_LNX_EPS = 64e-5
C_GROUPS = 4
C_GROUP_DIM = 64
C_WIDTH = C_GROUPS * C_GROUP_DIM
CHUNK = 128
A_COLS = 3 * A_WIDTH
B_COLS = 3 * B_WIDTH + B_DECAY_LORA + B_AAA_LORA + B_GATE_LORA
C_COLS = 2 * C_WIDTH
N_EXPERTS = 64
TOP_K = 8
N_GROUPS = 8
TOPK_GROUPS = 4
EXPERTS_PER_GROUP = N_EXPERTS // N_GROUPS
ROUTED_SCALE = 2.5
EXPERT_BLOCK = 256
RMS_EPS = 1e-6
LN_EPS = 1e-5
NEG_BIG = -1e30

V7X_LANES = 128
V7X_VMEM_LIMIT_BYTES = 56 * 1024 * 1024
RWKV_CHUNK = 64

NN = (((1,), (0,)), ((), ()))
NT = (((1,), (1,)), ((), ()))
TN = (((0,), (0,)), ((), ()))


def _cparams(*sem):
    return pltpu.CompilerParams(dimension_semantics=sem, vmem_limit_bytes=V7X_VMEM_LIMIT_BYTES)


def _mm(a, b, dims=NN):
    return lax.dot_general(a.astype(BF16), b.astype(BF16), dims, preferred_element_type=F32)


def _split(a):
    hi = a.astype(BF16)
    lo = (a - hi.astype(F32)).astype(BF16)
    return hi, lo


def _mm3(a, b, dims=NN):
    ah, al = _split(a)
    bh, bl = _split(b)
    d = lambda x, y: lax.dot_general(x, y, dims, preferred_element_type=F32)
    return d(ah, bh) + d(ah, bl) + d(al, bh)


def _mm2(a, b_exact, dims=NN):
    ah, al = _split(a)
    d = lambda x: lax.dot_general(x, b_exact, dims, preferred_element_type=F32)
    return d(ah) + d(al)


def _rms(x, eps=RMS_EPS):
    return x * lax.rsqrt(jnp.mean(x * x, axis=-1, keepdims=True) + eps)


def _sigmoid(x):
    return 1.0 / (1.0 + jnp.exp(-x))


def _silu(x):
    return x * _sigmoid(x)


def _adaln_kernel(c_ref, w_ref, b_ref, o_ref):
    c = c_ref[...]
    o_ref[0] = _mm3(_silu(c), w_ref[0]) + b_ref[0]


def _adaln(c, w_ada, b_ada):
    L, D, N = w_ada.shape
    Bn = c.shape[0]
    tn = min(N, 1536)
    return pl.pallas_call(
        _adaln_kernel,
        grid=(L, N // tn),
        in_specs=[
            pl.BlockSpec((Bn, D), lambda l, j: (0, 0)),
            pl.BlockSpec((1, D, tn), lambda l, j: (l, 0, j)),
            pl.BlockSpec((1, 1, tn), lambda l, j: (l, 0, j)),
        ],
        out_specs=pl.BlockSpec((1, Bn, tn), lambda l, j: (l, 0, j)),
        out_shape=jax.ShapeDtypeStruct((L, Bn, N), F32),
        compiler_params=_cparams("arbitrary", "arbitrary"),
        name="adaln",
    )(c, w_ada, b_ada.reshape(L, 1, N))


def _inproj_kernel(x_ref, g_ref, sc_ref, sh_ref, wa_ref, wbc_ref, oa_ref, obc_ref):
    x = x_ref[0]
    h = _rms(x) * g_ref[...] * (1.0 + sc_ref[0]) + sh_ref[0]
    hb = h.astype(BF16)
    oa_ref[0] = jnp.dot(hb, wa_ref[...], preferred_element_type=F32).astype(BF16)
    obc_ref[0] = jnp.dot(hb, wbc_ref[...], preferred_element_type=F32)


def _inproj(x, g, sc, sh, wa, wbc, tm):
    Bn, S, D = x.shape
    na, nbc = wa.shape[1], wbc.shape[1]
    return pl.pallas_call(
        _inproj_kernel,
        grid=(Bn, S // tm),
        in_specs=[
            pl.BlockSpec((1, tm, D), lambda b, i: (b, i, 0)),
            pl.BlockSpec((1, D), lambda b, i: (0, 0)),
            pl.BlockSpec((1, 1, D), lambda b, i: (b, 0, 0)),
            pl.BlockSpec((1, 1, D), lambda b, i: (b, 0, 0)),
            pl.BlockSpec((D, na), lambda b, i: (0, 0)),
            pl.BlockSpec((D, nbc), lambda b, i: (0, 0)),
        ],
        out_specs=[
            pl.BlockSpec((1, tm, na), lambda b, i: (b, i, 0)),
            pl.BlockSpec((1, tm, nbc), lambda b, i: (b, i, 0)),
        ],
        out_shape=[
            jax.ShapeDtypeStruct((Bn, S, na), BF16),
            jax.ShapeDtypeStruct((Bn, S, nbc), F32),
        ],
        compiler_params=_cparams("arbitrary", "arbitrary"),
        name="inproj",
    )(x, g.reshape(1, D), sc, sh, wa, wbc)


def _t5_bucket(dist):
    n = jnp.maximum(dist, 0)
    max_exact = N_BUCKETS // 2
    nf = jnp.maximum(n, 1).astype(F32)
    large = max_exact + (jnp.log(nf / max_exact) / math.log(MAX_DISTANCE / max_exact)
                         * (N_BUCKETS - max_exact)).astype(jnp.int32)
    large = jnp.minimum(large, N_BUCKETS - 1)
    return jnp.where(n < max_exact, n, large)


def _attn_band(table, tq):
    far = table[N_BUCKETS - 1].astype(F32)
    L = 3 * tq
    m = jnp.arange(L)
    m = jnp.where(m < 2 * tq, m, m - L)
    bands = []
    for off in (0, tq):
        dist = off - m
        vals = jnp.where(dist[None] >= 0, jnp.transpose(table[_t5_bucket(dist)].astype(F32)) - far[:, None],
                         NEG_BIG)
        toe = jnp.tile(vals, (1, tq))[:, :tq * (L - 1)].reshape(-1, tq, L - 1)
        bands.append(toe[:, :, :2 * tq])
    band = jnp.stack(bands)
    return jnp.concatenate([band, band], axis=2)


def _attn_kernel(lam_ref, q_ref, k_ref, v_ref, band_ref, g_ref, o_ref, *, tq, lambda_init):
    i = pl.program_id(2)
    q = q_ref[0] * jnp.asarray(A_QK_DIM ** -0.5, BF16)
    lane = lax.broadcasted_iota(jnp.int32, q.shape, 1)
    zero = jnp.zeros_like(q)
    qq = jnp.concatenate([jnp.where(lane < A_QK_DIM, q, zero),
                          jnp.where(lane >= A_QK_DIM, q, zero)], axis=0)

    kb0 = pl.multiple_of(jnp.maximum(i - 1, 0) * tq, tq)
    kb = k_ref[0, pl.ds(kb0, 2 * tq), :]
    vb = v_ref[0, pl.ds(kb0, 2 * tq), :]
    s = lax.dot_general(qq, kb, NT, preferred_element_type=F32) + band_ref[0, 0]
    m = jnp.max(s, axis=-1, keepdims=True)
    p = jnp.exp(s - m)
    l = jnp.sum(p, axis=-1, keepdims=True)
    acc = jnp.dot(p.astype(BF16), vb, preferred_element_type=F32)

    def body(j, carry):
        m, l, acc = carry
        k0 = pl.multiple_of(j * tq, tq)
        kj = k_ref[0, pl.ds(k0, tq), :]
        vj = v_ref[0, pl.ds(k0, tq), :]
        s = lax.dot_general(qq, kj, NT, preferred_element_type=F32)
        m_new = jnp.maximum(m, jnp.max(s, axis=-1, keepdims=True))
        alpha = jnp.exp(m - m_new)
        p = jnp.exp(s - m_new)
        l = alpha * l + jnp.sum(p, axis=-1, keepdims=True)
        acc = alpha * acc + jnp.dot(p.astype(BF16), vj, preferred_element_type=F32)
        return m_new, l, acc

    m, l, acc = lax.fori_loop(0, jnp.maximum(i - 1, 0), body, (m, l, acc))

    lp = lam_ref[...]
    lam = (jnp.exp(jnp.sum(lp[0:1] * lp[1:2], axis=-1, keepdims=True))
           - jnp.exp(jnp.sum(lp[2:3] * lp[3:4], axis=-1, keepdims=True)) + lambda_init)
    o = acc / l
    o = o[:tq] - lam * o[tq:]
    o_ref[0] = _rms(o) * g_ref[...] * (1.0 - lambda_init)


def _diff_attention(pa, band, lam_par, subln_g, lambda_init, tq):
    Bn, S, _ = pa.shape
    W = A_HEAD_W
    kern = functools.partial(_attn_kernel, tq=tq, lambda_init=lambda_init)
    return pl.pallas_call(
        kern,
        grid=(Bn, A_HEADS, S // tq),
        in_specs=[
            pl.BlockSpec((4, A_QK_DIM), lambda b, h, i: (0, 0)),
            pl.BlockSpec((1, tq, W), lambda b, h, i: (b, i, h)),
            pl.BlockSpec((1, S, W), lambda b, h, i: (b, 0, A_HEADS + h)),
            pl.BlockSpec((1, S, W), lambda b, h, i: (b, 0, 2 * A_HEADS + h)),
            pl.BlockSpec((1, 1, 2 * tq, 2 * tq), lambda b, h, i: (jnp.minimum(i, 1), h, 0, 0)),
            pl.BlockSpec((1, W), lambda b, h, i: (0, 0)),
        ],
        out_specs=pl.BlockSpec((1, tq, W), lambda b, h, i: (b, i, h)),
        out_shape=jax.ShapeDtypeStruct((Bn, S, A_WIDTH), F32),
        compiler_params=_cparams("arbitrary", "arbitrary", "arbitrary"),
        name="diff_attn",
    )(lam_par, pa, pa, pa, band, subln_g.reshape(1, W))


def _head_ones(n):
    r = lax.broadcasted_iota(jnp.int32, (n, n), 0) // B_HEAD_DIM
    c = lax.broadcasted_iota(jnp.int32, (n, n), 1) // B_HEAD_DIM
    return (r == c).astype(BF16)


def _rwkv_prep_kernel(pb_ref, prev_ref, mu_ref, w0_ref, w2_ref, a0_ref, a2_ref, g2_ref,
                      kk_ref, ka_ref, rk_ref,
                      rt_ref, at_ref, kt_ref, bt_ref, v_ref, wc_ref, bonus_ref, g_ref, *, tm):
    i = pl.program_id(1)
    C = RWKV_CHUNK
    x = pb_ref[0]
    row = lax.broadcasted_iota(jnp.int32, x.shape, 0)
    last = prev_ref[0, 7:8, :] * (i > 0).astype(F32)
    prev = jnp.where(row == 0, last, pltpu.roll(x, 1, 0))
    p = x + (prev - x) * mu_ref[...]
    o1, o2, o3 = B_WIDTH, 2 * B_WIDTH, 3 * B_WIDTH
    r, k, v = p[:, :o1], p[:, o1:o2], p[:, o2:o3]
    lora = p[:, o3:o3 + B_DECAY_LORA + B_AAA_LORA]
    gd = p[:, o3 + B_DECAY_LORA + B_AAA_LORA:]

    z = -(w0_ref[...] + _mm3(jnp.tanh(lora), w2_ref[...]))
    softplus = jnp.maximum(z, 0.0) + jnp.log(1.0 + jnp.exp(-jnp.abs(z)))
    logw = -jnp.exp(-softplus - 0.5)
    a = _sigmoid(a0_ref[...] + _mm3(lora, a2_ref[...]))
    g_ref[0] = _mm3(_sigmoid(gd), g2_ref[...])

    ones = _head_ones(B_WIDTH)
    kk = k * kk_ref[...]
    kk = kk * lax.rsqrt(jnp.maximum(_mm2(kk * kk, ones), 1e-24))
    k2 = k * (1.0 + (a - 1.0) * ka_ref[...])
    bonus_ref[0] = _mm2(r * k2 * rk_ref[...], ones) * v

    t_in = lax.broadcasted_iota(jnp.int32, (tm, B_WIDTH), 0) % C
    cum = logw
    sh = 1
    while sh < C:
        cum = cum + jnp.where(t_in >= sh, pltpu.roll(cum, sh, 0), 0.0)
        sh *= 2
    n = tm // C
    wc_ref[0] = jnp.exp(jnp.sum(logw.reshape(n, C, B_WIDTH), axis=1))
    e_pos = jnp.exp(cum)
    e_neg = jnp.exp(-cum)
    rt_ref[0] = r * e_pos
    at_ref[0] = -kk * jnp.exp(cum - logw)
    kt_ref[0] = k2 * e_neg
    bt_ref[0] = kk * a * e_neg
    v_ref[0] = v


def _rwkv_prep(pbc, mu, w0, w2p, a0, a2p, g2, k_k, k_a, r_k, tm):
    Bn, S, _ = pbc.shape
    W = B_WIDTH
    nl = B_DECAY_LORA + B_AAA_LORA
    row = lambda a: a.reshape(1, -1)
    full = lambda shp: pl.BlockSpec(shp, lambda b, i: (0,) * len(shp))
    seq = pl.BlockSpec((1, tm, W), lambda b, i: (b, i, 0))
    seq_shape = jax.ShapeDtypeStruct((Bn, S, W), F32)
    n = tm // RWKV_CHUNK
    return pl.pallas_call(
        functools.partial(_rwkv_prep_kernel, tm=tm),
        grid=(Bn, S // tm),
        in_specs=[
            pl.BlockSpec((1, tm, B_COLS), lambda b, i: (b, i, 0)),
            pl.BlockSpec((1, 8, B_COLS), lambda b, i: (b, jnp.maximum(i * (tm // 8) - 1, 0), 0)),
            full((1, B_COLS)), full((1, W)), full((nl, W)), full((1, W)), full((nl, W)),
            full((B_GATE_LORA, W)), full((1, W)), full((1, W)), full((1, W)),
        ],
        out_specs=[seq, seq, seq, seq, seq,
                   pl.BlockSpec((1, n, W), lambda b, i: (b, i, 0)), seq, seq],
        out_shape=[seq_shape] * 5 + [jax.ShapeDtypeStruct((Bn, S // RWKV_CHUNK, W), F32)] + [seq_shape] * 2,
        compiler_params=_cparams("arbitrary", "arbitrary"),
        name="rwkv_prep",
    )(pbc, pbc, row(mu), row(w0), w2p, row(a0), a2p, g2, row(k_k), row(k_a), row(r_k))


def _rwkv_scan_kernel(rt_ref, at_ref, kt_ref, bt_ref, v_ref, wc_ref, bonus_ref, g_ref,
                      lng_ref, lnb_ref, o_ref, state, *, tt):
    C = RWKV_CHUNK
    W = B_WIDTH

    @pl.when(pl.program_id(1) == 0)
    def _():
        state[...] = jnp.zeros_like(state)

    lane_head = lax.broadcasted_iota(jnp.int32, (C, W), 1) // B_HEAD_DIM
    tt_i = lax.broadcasted_iota(jnp.int32, (C, W), 0)
    ss_i = lax.broadcasted_iota(jnp.int32, (C, W), 1) % C
    strict = tt_i > ss_i
    incl = tt_i >= ss_i
    eye = (tt_i == ss_i).astype(F32)
    ones = _head_ones(W)
    bd_mask = ones.astype(F32)

    def bd(x):
        return jnp.concatenate([jnp.where(lane_head == h, x, 0.0) for h in range(B_HEADS)], axis=0)

    def chunk(c, carry):
        r0 = pl.multiple_of(c * C, C)
        sl = pl.ds(r0, C)
        rt, at, kt, bt, v = rt_ref[0, sl, :], at_ref[0, sl, :], kt_ref[0, sl, :], bt_ref[0, sl, :], v_ref[0, sl, :]
        wc = wc_ref[0, pl.ds(c, 1), :]
        ar = jnp.concatenate([at, rt], axis=0)
        a_b = _mm3(ar, bd(bt), NT)
        a_k = _mm3(ar, bd(kt), NT)
        lo = jnp.where(strict, a_b[:C], 0.0)
        a_ak = jnp.where(strict, a_k[:C], 0.0)
        a_rb = jnp.where(incl, a_b[C:], 0.0)
        a_rk = jnp.where(incl, a_k[C:], 0.0)
        pw = lo
        tinv = eye + lo
        span = 2
        while span < C:
            pw = _mm3(pw, bd(pw))
            tinv = tinv + _mm3(tinv, bd(pw))
            span *= 2
        bdv = bd(v)
        abar = _mm3(tinv, bd(at))
        u0 = _mm3(tinv, bd(_mm3(a_ak, bdv)))
        s0 = state[...]
        a_s = _mm3(jnp.concatenate([abar, rt], axis=0), s0, NT)
        u = a_s[:C] + u0
        y = a_s[C:] + _mm3(jnp.concatenate([a_rb, a_rk], axis=1),
                           jnp.concatenate([bd(u), bdv], axis=0))
        upd = _mm3(jnp.concatenate([u, v], axis=0), jnp.concatenate([bt * wc, kt * wc], axis=0), TN)
        state[...] = s0 * wc + upd * bd_mask
        mean = _mm2(y, ones) * (1.0 / B_HEAD_DIM)
        d = y - mean
        var = _mm2(d * d, ones) * (1.0 / B_HEAD_DIM)
        yn = d * lax.rsqrt(var + B_LNX_EPS) * lng_ref[...] + lnb_ref[...]
        o_ref[0, sl, :] = (yn + bonus_ref[0, sl, :]) * g_ref[0, sl, :]
        return carry

    lax.fori_loop(0, tt // C, chunk, 0)


def _rwkv_scan(rt, at, kt, bt, v, wc, bonus, g, lnx_g, lnx_b, tt):
    Bn, S, W = rt.shape
    n = tt // RWKV_CHUNK
    seq = pl.BlockSpec((1, tt, W), lambda b, i: (b, i, 0))
    vec = pl.BlockSpec((1, W), lambda b, i: (0, 0))
    return pl.pallas_call(
        functools.partial(_rwkv_scan_kernel, tt=tt),
        grid=(Bn, S // tt),
        in_specs=[seq, seq, seq, seq, seq, pl.BlockSpec((1, n, W), lambda b, i: (b, i, 0)), seq, seq, vec, vec],
        out_specs=seq,
        out_shape=jax.ShapeDtypeStruct((Bn, S, W), F32),
        scratch_shapes=[pltpu.VMEM((B_HEADS * B_HEAD_DIM, W), F32)],
        compiler_params=_cparams("arbitrary", "arbitrary"),
        name="rwkv_scan",
    )(rt, at, kt, bt, v, wc, bonus, g, lnx_g.reshape(1, W), lnx_b.reshape(1, W))


def _gmlp_kernel(pc_ref, lng_ref, lnb_ref, ws_ref, bs_ref, o_ref, *, tm):
    x = pc_ref[0]
    z = x * (0.5 * (1.0 + jnp.tanh(math.sqrt(2.0 / math.pi) * (x + 0.044715 * (x * x * x)))))
    u, v = z[:, :C_WIDTH], z[:, C_WIDTH:]
    mu = jnp.mean(v, axis=-1, keepdims=True)
    d = v - mu
    var = jnp.mean(d * d, axis=-1, keepdims=True)
    vn = d * lax.rsqrt(var + LN_EPS) * lng_ref[...] + lnb_ref[...]
    group = lax.broadcasted_iota(jnp.int32, (CHUNK, C_WIDTH), 1) // C_GROUP_DIM
    tril = (lax.broadcasted_iota(jnp.int32, (CHUNK, CHUNK), 0)
            >= lax.broadcasted_iota(jnp.int32, (CHUNK, CHUNK), 1))
    ws = [jnp.where(tril, ws_ref[gi], 0.0).astype(BF16) for gi in range(C_GROUPS)]
    for c in range(tm // CHUNK):
        sl = slice(c * CHUNK, (c + 1) * CHUNK)
        vc = vn[sl].astype(BF16)
        sv = bs_ref[...]
        for gi in range(C_GROUPS):
            t = jnp.dot(ws[gi], vc, preferred_element_type=F32)
            sv = sv + jnp.where(group == gi, t, 0.0)
        o_ref[0, sl, :] = u[sl] * sv


def _gmlp(pbc, ln_g, ln_b, w_s, b_s, tm):
    Bn, S, _ = pbc.shape
    bs_wide = jnp.repeat(jnp.transpose(b_s), C_GROUP_DIM, axis=1)
    return pl.pallas_call(
        functools.partial(_gmlp_kernel, tm=tm),
        grid=(Bn, S // tm),
        in_specs=[
            pl.BlockSpec((1, tm, C_COLS), lambda b, i: (b, i, B_COLS // C_COLS)),
            pl.BlockSpec((1, C_WIDTH), lambda b, i: (0, 0)),
            pl.BlockSpec((1, C_WIDTH), lambda b, i: (0, 0)),
            pl.BlockSpec((C_GROUPS, CHUNK, CHUNK), lambda b, i: (0, 0, 0)),
            pl.BlockSpec((CHUNK, C_WIDTH), lambda b, i: (0, 0)),
        ],
        out_specs=pl.BlockSpec((1, tm, C_WIDTH), lambda b, i: (b, i, 0)),
        out_shape=jax.ShapeDtypeStruct((Bn, S, C_WIDTH), F32),
        compiler_params=_cparams("arbitrary", "arbitrary"),
        name="gmlp",
    )(pbc, ln_g.reshape(1, -1), ln_b.reshape(1, -1), w_s, bs_wide)


def _mid_kernel(ya_ref, yb_ref, yc_ref, x_ref, woa_ref, wob_ref, woc_ref, gpost_ref, g1_ref,
                gpre_ref, sc_ref, sh_ref, wr_ref, ws1_ref, ws3_ref, ws2_ref,
                xo_ref, h_ref, score_ref, shared_ref):
    y = (_mm(ya_ref[0], woa_ref[...]) + _mm(yb_ref[0], wob_ref[...]) + _mm(yc_ref[0], woc_ref[...]))
    xn = x_ref[0] + g1_ref[0] * (_rms(y) * gpost_ref[...])
    xo_ref[0] = xn
    h = _rms(xn) * gpre_ref[...] * (1.0 + sc_ref[0]) + sh_ref[0]
    h_ref[0] = h
    score_ref[0] = _sigmoid(_mm3(wr_ref[...], h, NT))
    hb = h.astype(BF16)
    t = _silu(jnp.dot(hb, ws1_ref[...], preferred_element_type=F32)) * jnp.dot(
        hb, ws3_ref[...], preferred_element_type=F32)
    shared_ref[0] = jnp.dot(t.astype(BF16), ws2_ref[...], preferred_element_type=F32)


def _mid(ya, yb, yc, x, woa, wob, woc, gpost, g1, gpre, sc, sh, wr, ws1, ws3, ws2, tm):
    Bn, S, D = x.shape
    NR = wr.shape[0]
    F = ws1.shape[1]
    seq = lambda w: pl.BlockSpec((1, tm, w), lambda b, i: (b, i, 0))
    full = lambda shp: pl.BlockSpec(shp, lambda b, i: (0,) * len(shp))
    per_b = pl.BlockSpec((1, 1, D), lambda b, i: (b, 0, 0))
    return pl.pallas_call(
        _mid_kernel,
        grid=(Bn, S // tm),
        in_specs=[seq(A_WIDTH), seq(B_WIDTH), seq(C_WIDTH), seq(D),
                  full((A_WIDTH, D)), full((B_WIDTH, D)), full((C_WIDTH, D)),
                  full((1, D)), per_b, full((1, D)), per_b, per_b,
                  full((NR, D)), full((D, F)), full((D, F)), full((F, D))],
        out_specs=[seq(D), seq(D), pl.BlockSpec((1, NR, tm), lambda b, i: (b, 0, i)), seq(D)],
        out_shape=[jax.ShapeDtypeStruct((Bn, S, D), F32), jax.ShapeDtypeStruct((Bn, S, D), F32),
                   jax.ShapeDtypeStruct((Bn, NR, S), F32), jax.ShapeDtypeStruct((Bn, S, D), F32)],
        compiler_params=_cparams("arbitrary", "arbitrary"),
        name="mid",
    )(ya, yb, yc, x, woa, wob, woc, gpost.reshape(1, D), g1, gpre.reshape(1, D), sc, sh, wr, ws1, ws3, ws2)


def _first_argmax(vals, iota, n):
    m = jnp.max(vals, axis=0, keepdims=True)
    idx = jnp.min(jnp.where(vals == m, iota, n), axis=0, keepdims=True)
    return m, idx


def _route_kernel(sc_ref, bias_ref, e_ref, w_ref, r_ref, cnt_ref, carry, *, tm):
    @pl.when((pl.program_id(0) == 0) & (pl.program_id(1) == 0))
    def _():
        carry[...] = jnp.zeros_like(carry)

    G = EXPERTS_PER_GROUP
    s = sc_ref[0]
    biased = s + bias_ref[...]
    neg_inf = jnp.float32(-jnp.inf)
    io8 = lax.broadcasted_iota(jnp.int32, (G, tm), 0)
    gs_rows = []
    for g in range(N_GROUPS):
        blk = biased[g * G:(g + 1) * G]
        m1, i1 = _first_argmax(blk, io8, G)
        m2 = jnp.max(jnp.where(io8 == i1, neg_inf, blk), axis=0, keepdims=True)
        gs_rows.append(m1 + m2)
    gs = jnp.concatenate(gs_rows, axis=0)
    gio = lax.broadcasted_iota(jnp.int32, (N_GROUPS, tm), 0)
    gsel = jnp.zeros((N_GROUPS, tm), jnp.bool_)
    for _ in range(TOPK_GROUPS):
        _, gi = _first_argmax(gs, gio, N_GROUPS)
        pick = gio == gi
        gsel = gsel | pick
        gs = jnp.where(pick, neg_inf, gs)
    masked = jnp.concatenate(
        [jnp.where(gsel[g:g + 1], biased[g * G:(g + 1) * G], neg_inf) for g in range(N_GROUPS)], axis=0)

    eio = lax.broadcasted_iota(jnp.int32, (N_EXPERTS, tm), 0)
    picks, e_rows, s_rows = [], [], []
    for _ in range(TOP_K):
        _, ei = _first_argmax(masked, eio, N_EXPERTS)
        pick = eio == ei
        picks.append(pick)
        e_rows.append(ei)
        s_rows.append(jnp.sum(jnp.where(pick, s, 0.0), axis=0, keepdims=True))
        masked = jnp.where(pick, neg_inf, masked)
    top_s = jnp.concatenate(s_rows, axis=0)
    w_ref[...] = top_s / (jnp.sum(top_s, axis=0, keepdims=True) + 1e-20) * ROUTED_SCALE
    e_ref[...] = jnp.concatenate(e_rows, axis=0)

    sel = jnp.zeros((N_EXPERTS, tm), F32)
    for pick in picks:
        sel = sel + pick.astype(F32)
    before = (lax.broadcasted_iota(jnp.int32, (tm, tm), 0) < lax.broadcasted_iota(jnp.int32, (tm, tm), 1))
    pos = carry[...] + jnp.dot(sel.astype(BF16), before.astype(BF16), preferred_element_type=F32)
    r_ref[...] = jnp.concatenate(
        [jnp.sum(jnp.where(pick, pos, 0.0), axis=0, keepdims=True) for pick in picks], axis=0).astype(jnp.int32)
    total = carry[...] + jnp.sum(sel, axis=1, keepdims=True)
    carry[...] = total
    cnt_ref[...] = jnp.broadcast_to(total, cnt_ref.shape).astype(jnp.int32)


def _route(scores_t, e_bias, tm):
    Bn, _, S = scores_t.shape
    T = Bn * S
    nt = S // tm
    tok = pl.BlockSpec((TOP_K, tm), lambda b, i: (0, b * nt + i))
    return pl.pallas_call(
        functools.partial(_route_kernel, tm=tm),
        grid=(Bn, nt),
        in_specs=[pl.BlockSpec((1, N_EXPERTS, tm), lambda b, i: (b, 0, i)),
                  pl.BlockSpec((N_EXPERTS, 1), lambda b, i: (0, 0))],
        out_specs=[tok, tok, tok, pl.BlockSpec((N_EXPERTS, V7X_LANES), lambda b, i: (0, 0))],
        out_shape=[jax.ShapeDtypeStruct((TOP_K, T), jnp.int32), jax.ShapeDtypeStruct((TOP_K, T), F32),
                   jax.ShapeDtypeStruct((TOP_K, T), jnp.int32),
                   jax.ShapeDtypeStruct((N_EXPERTS, V7X_LANES), jnp.int32)],
        scratch_shapes=[pltpu.VMEM((N_EXPERTS, 1), F32)],
        compiler_params=_cparams("arbitrary", "arbitrary"),
        name="route",
    )(scores_t, e_bias.reshape(N_EXPERTS, 1))


def _dest_kernel(start_ref, e_ref, r_ref, o_ref):
    e = e_ref[...]
    acc = r_ref[...]
    for ex in range(N_EXPERTS):
        acc = acc + jnp.where(e == ex, start_ref[ex], 0)
    o_ref[0] = acc


def _dest_rows(pad_start, eidx, rank, tt):
    K_, T = eidx.shape
    grid_spec = pltpu.PrefetchScalarGridSpec(
        num_scalar_prefetch=1,
        grid=(T // tt,),
        in_specs=[pl.BlockSpec((K_, tt), lambda i, st: (0, i)), pl.BlockSpec((K_, tt), lambda i, st: (0, i))],
        out_specs=pl.BlockSpec((1, K_, tt), lambda i, st: (i, 0, 0)),
    )
    return pl.pallas_call(
        _dest_kernel,
        grid_spec=grid_spec,
        out_shape=jax.ShapeDtypeStruct((T // tt, K_, tt), jnp.int32),
        compiler_params=_cparams("arbitrary"),
        name="dest_rows",
    )(pad_start, eidx, rank)


def _dispatch_kernel(fill_ref, dest_hbm, h_ref, xs_hbm, dest_smem, zbuf, sem_idx, sem_rows, sem_zero):
    i = pl.program_id(0)
    EB = EXPERT_BLOCK

    def zero_copy(ex):
        return pltpu.make_async_copy(zbuf, xs_hbm.at[pl.ds(pl.multiple_of(fill_ref[ex], EB), EB), :], sem_zero)

    @pl.when(i == 0)
    def _():
        zbuf[...] = jnp.zeros_like(zbuf)

        def z_start(ex, c):
            @pl.when(fill_ref[ex] >= 0)
            def _():
                zero_copy(ex).start()
            return c

        def z_wait(ex, c):
            @pl.when(fill_ref[ex] >= 0)
            def _():
                zero_copy(ex).wait()
            return c

        lax.fori_loop(0, N_EXPERTS, z_start, 0)
        lax.fori_loop(0, N_EXPERTS, z_wait, 0)

    idx_copy = pltpu.make_async_copy(dest_hbm.at[i], dest_smem, sem_idx)
    idx_copy.start()
    idx_copy.wait()
    tt = h_ref.shape[0]

    def row_copy(t, k):
        return pltpu.make_async_copy(h_ref.at[pl.ds(t, 1), :], xs_hbm.at[pl.ds(dest_smem[k, t], 1), :], sem_rows)

    def issue(t, c):
        for k in range(TOP_K):
            row_copy(t, k).start()
        return c

    def drain(t, c):
        for k in range(TOP_K):
            row_copy(t, k).wait()
        return c

    lax.fori_loop(0, tt, issue, 0)
    lax.fori_loop(0, tt, drain, 0)


def _dispatch(fill_blocks, dest, h, n_rows):
    T, D = h.shape
    nt, K_, tt = dest.shape
    grid_spec = pltpu.PrefetchScalarGridSpec(
        num_scalar_prefetch=1,
        grid=(nt,),
        in_specs=[pl.BlockSpec(memory_space=pl.ANY), pl.BlockSpec((tt, D), lambda i, fl: (i, 0))],
        out_specs=pl.BlockSpec(memory_space=pl.ANY),
        scratch_shapes=[pltpu.SMEM((K_, tt), jnp.int32), pltpu.VMEM((EXPERT_BLOCK, D), F32),
                        pltpu.SemaphoreType.DMA(()), pltpu.SemaphoreType.DMA(()), pltpu.SemaphoreType.DMA(())],
    )
    return pl.pallas_call(
        _dispatch_kernel,
        grid_spec=grid_spec,
        out_shape=jax.ShapeDtypeStruct((n_rows, D), F32),
        compiler_params=_cparams("arbitrary"),
        name="dispatch",
    )(fill_blocks, dest, h)


def _expert_kernel(blk_e_ref, n_used_ref, x_ref, w1_ref, w3_ref, w2_ref, o_ref):
    @pl.when(pl.program_id(0) < n_used_ref[0])
    def _():
        xb = x_ref[...].astype(BF16)
        t = _silu(jnp.dot(xb, w1_ref[0], preferred_element_type=F32)) * jnp.dot(
            xb, w3_ref[0], preferred_element_type=F32)
        o_ref[...] = jnp.dot(t.astype(BF16), w2_ref[0], preferred_element_type=F32)


def _experts(blk_e, n_used, xs, w1, w3, w2):
    P, D = xs.shape
    EB = EXPERT_BLOCK
    n_blocks = blk_e.shape[0]
    F = w1.shape[2]
    rows = pl.BlockSpec((EB, D), lambda i, be, nu: (jnp.minimum(i, nu[0] - 1), 0))
    grid_spec = pltpu.PrefetchScalarGridSpec(
        num_scalar_prefetch=2,
        grid=(n_blocks,),
        in_specs=[
            rows,
            pl.BlockSpec((1, D, F), lambda i, be, nu: (be[i], 0, 0)),
            pl.BlockSpec((1, D, F), lambda i, be, nu: (be[i], 0, 0)),
            pl.BlockSpec((1, F, D), lambda i, be, nu: (be[i], 0, 0)),
        ],
        out_specs=rows,
    )
    return pl.pallas_call(
        _expert_kernel,
        grid_spec=grid_spec,
        out_shape=jax.ShapeDtypeStruct((P, D), F32),
        compiler_params=_cparams("arbitrary"),
        name="experts",
    )(blk_e, n_used, xs, w1, w3, w2)


def _block_layout(counts, n_blocks):
    EB = EXPERT_BLOCK
    padded = (counts + EB - 1) // EB * EB
    pad_end = jnp.cumsum(padded)
    pad_start = pad_end - padded
    blk_row = (jnp.arange(n_blocks) * EB)[:, None]
    blk_e = jnp.minimum(jnp.sum((pad_end[None, :] <= blk_row).astype(jnp.int32), axis=1), N_EXPERTS - 1)
    n_used = (pad_end[-1] // EB).astype(jnp.int32).reshape(1)
    fill = jnp.where(padded > counts, pad_end - EB, -1).astype(jnp.int32)
    return pad_start.astype(jnp.int32), blk_e, n_used, fill


def _combine_kernel(dest_hbm, ys_hbm, w_ref, x_ref, shared_ref, gpost_ref, g2_ref, o_ref,
                    dest_smem, buf, sem_idx, sem_rows, *, tt, nt):
    i = pl.program_id(0) * nt + pl.program_id(1)
    idx_copy = pltpu.make_async_copy(dest_hbm.at[i], dest_smem, sem_idx)
    idx_copy.start()
    idx_copy.wait()

    def row_copy(t, k):
        return pltpu.make_async_copy(ys_hbm.at[pl.ds(dest_smem[k, t], 1), :], buf.at[k, pl.ds(t, 1), :], sem_rows)

    def issue(t, c):
        for k in range(TOP_K):
            row_copy(t, k).start()
        return c

    def drain(t, c):
        for k in range(TOP_K):
            row_copy(t, k).wait()
        return c

    lax.fori_loop(0, tt, issue, 0)
    lax.fori_loop(0, tt, drain, 0)
    w = w_ref[...]
    y = shared_ref[0]
    for k in range(TOP_K):
        y = y + w[:, k:k + 1] * buf[k]
    o_ref[0] = x_ref[0] + g2_ref[0] * (_rms(y) * gpost_ref[...])


def _combine(dest, ys, w_tok, x, shared, gpost, g2):
    Bn, S, D = x.shape
    _, K_, tt = dest.shape
    nt = S // tt
    seq = pl.BlockSpec((1, tt, D), lambda b, i: (b, i, 0))
    return pl.pallas_call(
        functools.partial(_combine_kernel, tt=tt, nt=nt),
        grid=(Bn, nt),
        in_specs=[pl.BlockSpec(memory_space=pl.ANY), pl.BlockSpec(memory_space=pl.ANY),
                  pl.BlockSpec((tt, K_), lambda b, i: (b * nt + i, 0)), seq, seq,
                  pl.BlockSpec((1, D), lambda b, i: (0, 0)), pl.BlockSpec((1, 1, D), lambda b, i: (b, 0, 0))],
        out_specs=seq,
        out_shape=jax.ShapeDtypeStruct((Bn, S, D), F32),
        scratch_shapes=[pltpu.SMEM((K_, tt), jnp.int32), pltpu.VMEM((K_, tt, D), F32),
                        pltpu.SemaphoreType.DMA(()), pltpu.SemaphoreType.DMA(())],
        compiler_params=_cparams("arbitrary", "arbitrary"),
        name="combine",
    )(dest, ys, w_tok, x, shared, gpost.reshape(1, D), g2)


def kernel(x, c, w_ada, b_ada, norm_pre_mix, norm_post_mix, norm_pre_ffn, norm_post_ffn, w_in, w_out, rel_bias_table, diff_lambda, diff_subln, rwkv_mu, rwkv_w0, rwkv_w2, rwkv_a0, rwkv_a2, rwkv_g2, rwkv_k_k, rwkv_k_a, rwkv_r_k, rwkv_lnx_g, rwkv_lnx_b, gmlp_ln_g, gmlp_ln_b, gmlp_w_s, gmlp_b_s, router_w, router_bias, exp_w1, exp_w3, exp_w2, shared_w1, shared_w3, shared_w2):
    Bn, S, D = x.shape
    depth = w_ada.shape[0]
    tm = min(256, S)
    tq = min(256, S // 2)
    t_rwkv = min(512, S)

    mod = _adaln(c, w_ada, b_ada)
    band = _attn_band(rel_bias_table, tq)
    zpad = jnp.zeros((B_DECAY_LORA, B_WIDTH), F32)
    for l in range(depth):
        sh1, sc1, g1, sh2, sc2, g2 = [m.reshape(Bn, 1, D) for m in jnp.split(mod[l], 6, axis=-1)]
        w_in_b = w_in[l].astype(BF16)
        pa, pbc = _inproj(x, norm_pre_mix[l], sc1, sh1, w_in_b[:, :A_COLS], w_in_b[:, A_COLS:], tm)
        lambda_init = 0.8 - 0.6 * math.exp(-0.3 * l)
        ya = _diff_attention(pa, band, diff_lambda[l], diff_subln[l], lambda_init, tq)
        prep = _rwkv_prep(pbc, rwkv_mu[l], rwkv_w0[l], jnp.concatenate([rwkv_w2[l], zpad], axis=0),
                          rwkv_a0[l], jnp.concatenate([zpad, rwkv_a2[l]], axis=0), rwkv_g2[l],
                          rwkv_k_k[l], rwkv_k_a[l], rwkv_r_k[l].reshape(-1), t_rwkv)
        yb = _rwkv_scan(*prep, rwkv_lnx_g[l], rwkv_lnx_b[l], t_rwkv)
        yc = _gmlp(pbc, gmlp_ln_g[l], gmlp_ln_b[l], gmlp_w_s[l], gmlp_b_s[l], tm)

        w_out_b = w_out[l].astype(BF16)
        wr_t = jnp.pad(jnp.transpose(router_w[l]), ((0, V7X_LANES - N_EXPERTS), (0, 0)))
        x, h, scores_t, shared = _mid(
            ya, yb, yc, x, w_out_b[:A_WIDTH], w_out_b[A_WIDTH:A_WIDTH + B_WIDTH], w_out_b[A_WIDTH + B_WIDTH:],
            norm_post_mix[l], g1, norm_pre_ffn[l], sc2, sh2, wr_t,
            shared_w1[l].astype(BF16), shared_w3[l].astype(BF16), shared_w2[l].astype(BF16), tm)

        T = Bn * S
        n_blocks = -(-T * TOP_K // EXPERT_BLOCK) + N_EXPERTS
        eidx, wgt, rank, cnt = _route(scores_t, router_bias[l], tm)
        pad_start, blk_e, n_used, fill = _block_layout(cnt[:, 0], n_blocks)
        dest = _dest_rows(pad_start, eidx, rank, tm)
        xs = _dispatch(fill, dest, h.reshape(T, D), n_blocks * EXPERT_BLOCK)
        ys = _experts(blk_e, n_used, xs, exp_w1[l].astype(BF16), exp_w3[l].astype(BF16), exp_w2[l].astype(BF16))
        x = _combine(dest, ys, jnp.transpose(wgt), x, shared, norm_post_ffn[l], g2)
    return x
```

```python
import functools
import math

import jax
import jax.numpy as jnp
from jax import lax
from jax.experimental import pallas as pl
from jax.experimental.pallas import tpu as pltpu

F32 = jnp.float32
BF16 = jnp.bfloat16

A_HEADS = 4
A_QK_DIM = 64
A_HEAD_W = 2 * A_QK_DIM
A_WIDTH = A_HEADS * A_HEAD_W
N_BUCKETS = 32
MAX_DISTANCE = 128
B_HEADS = 4
B_HEAD_DIM = 64
B_WIDTH = B_HEADS * B_HEAD_DIM
B_DECAY_LORA = 64
B_AAA_LORA = 64
B_GATE_LORA = 128
B_LNX_EPS = 64e-5
C_GROUPS = 4
C_GROUP_DIM = 64
C_WIDTH = C_GROUPS * C_GROUP_DIM
CHUNK = 128
A_COLS = 3 * A_WIDTH
B_COLS = 3 * B_WIDTH + B_DECAY_LORA + B_AAA_LORA + B_GATE_LORA
C_COLS = 2 * C_WIDTH
N_EXPERTS = 64
TOP_K = 8
N_GROUPS = 8
TOPK_GROUPS = 4
EXPERTS_PER_GROUP = N_EXPERTS // N_GROUPS
ROUTED_SCALE = 2.5
EXPERT_BLOCK = 256
RMS_EPS = 1e-6
LN_EPS = 1e-5
NEG_BIG = -1e30

V7X_LANES = 128
V7X_VMEM_LIMIT_BYTES = 56 * 1024 * 1024
RWKV_CHUNK = 64
RWKV_GROUP = 4

NN = (((1,), (0,)), ((), ()))
NT = (((1,), (1,)), ((), ()))
TN = (((0,), (0,)), ((), ()))


def _cparams(*sem):
    return pltpu.CompilerParams(dimension_semantics=sem, vmem_limit_bytes=V7X_VMEM_LIMIT_BYTES)


def _mm(a, b, dims=NN):
    return lax.dot_general(a.astype(BF16), b.astype(BF16), dims, preferred_element_type=F32)


def _split(a):
    hi = a.astype(BF16)
    lo = (a - hi.astype(F32)).astype(BF16)
    return hi, lo


def _mm3(a, b, dims=NN):
    ah, al = _split(a)
    bh, bl = _split(b)
    d = lambda x, y: lax.dot_general(x, y, dims, preferred_element_type=F32)
    return d(ah, bh) + d(ah, bl) + d(al, bh)


def _mm2(a, b_exact, dims=NN):
    ah, al = _split(a)
    d = lambda x: lax.dot_general(x, b_exact, dims, preferred_element_type=F32)
    return d(ah) + d(al)


def _rms(x, eps=RMS_EPS):
    return x * lax.rsqrt(jnp.mean(x * x, axis=-1, keepdims=True) + eps)


def _sigmoid(x):
    return 1.0 / (1.0 + jnp.exp(-x))


def _silu(x):
    return x * _sigmoid(x)


def _adaln_kernel(c_ref, w_ref, b_ref, o_ref):
    c = c_ref[...]
    o_ref[0] = _mm3(_silu(c), w_ref[0]) + b_ref[0]


def _adaln(c, w_ada, b_ada):
    L, D, N = w_ada.shape
    Bn = c.shape[0]
    tn = min(N, 1536)
    return pl.pallas_call(
        _adaln_kernel,
        grid=(L, N // tn),
        in_specs=[
            pl.BlockSpec((Bn, D), lambda l, j: (0, 0)),
            pl.BlockSpec((1, D, tn), lambda l, j: (l, 0, j)),
            pl.BlockSpec((1, 1, tn), lambda l, j: (l, 0, j)),
        ],
        out_specs=pl.BlockSpec((1, Bn, tn), lambda l, j: (l, 0, j)),
        out_shape=jax.ShapeDtypeStruct((L, Bn, N), F32),
        compiler_params=_cparams("arbitrary", "arbitrary"),
        name="adaln",
    )(c, w_ada, b_ada.reshape(L, 1, N))


def _inproj_kernel(x_ref, g_ref, sc_ref, sh_ref, wa_ref, wbc_ref, oa_ref, obc_ref):
    x = x_ref[0]
    h = _rms(x) * g_ref[...] * (1.0 + sc_ref[0]) + sh_ref[0]
    hb = h.astype(BF16)
    oa_ref[0] = jnp.dot(hb, wa_ref[...], preferred_element_type=F32).astype(BF16)
    obc_ref[0] = jnp.dot(hb, wbc_ref[...], preferred_element_type=F32)


def _inproj(x, g, sc, sh, wa, wbc, tm):
    Bn, S, D = x.shape
    na, nbc = wa.shape[1], wbc.shape[1]
    return pl.pallas_call(
        _inproj_kernel,
        grid=(Bn, S // tm),
        in_specs=[
            pl.BlockSpec((1, tm, D), lambda b, i: (b, i, 0)),
            pl.BlockSpec((1, D), lambda b, i: (0, 0)),
            pl.BlockSpec((1, 1, D), lambda b, i: (b, 0, 0)),
            pl.BlockSpec((1, 1, D), lambda b, i: (b, 0, 0)),
            pl.BlockSpec((D, na), lambda b, i: (0, 0)),
            pl.BlockSpec((D, nbc), lambda b, i: (0, 0)),
        ],
        out_specs=[
            pl.BlockSpec((1, tm, na), lambda b, i: (b, i, 0)),
            pl.BlockSpec((1, tm, nbc), lambda b, i: (b, i, 0)),
        ],
        out_shape=[
            jax.ShapeDtypeStruct((Bn, S, na), BF16),
            jax.ShapeDtypeStruct((Bn, S, nbc), F32),
        ],
        compiler_params=_cparams("arbitrary", "arbitrary"),
        name="inproj",
    )(x, g.reshape(1, D), sc, sh, wa, wbc)


def _t5_bucket(dist):
    n = jnp.maximum(dist, 0)
    max_exact = N_BUCKETS // 2
    nf = jnp.maximum(n, 1).astype(F32)
    large = max_exact + (jnp.log(nf / max_exact) / math.log(MAX_DISTANCE / max_exact)
                         * (N_BUCKETS - max_exact)).astype(jnp.int32)
    large = jnp.minimum(large, N_BUCKETS - 1)
    return jnp.where(n < max_exact, n, large)


def _attn_band(table, tq):
    far = table[N_BUCKETS - 1].astype(F32)
    L = 3 * tq
    m = jnp.arange(L)
    m = jnp.where(m < 2 * tq, m, m - L)
    bands = []
    for off in (0, tq):
        dist = off - m
        vals = jnp.where(dist[None] >= 0, jnp.transpose(table[_t5_bucket(dist)].astype(F32)) - far[:, None],
                         NEG_BIG)
        toe = jnp.tile(vals, (1, tq))[:, :tq * (L - 1)].reshape(-1, tq, L - 1)
        bands.append(toe[:, :, :2 * tq])
    band = jnp.stack(bands)
    return jnp.concatenate([band, band], axis=2)


def _attn_kernel(lam_ref, q_ref, k_ref, v_ref, band_ref, g_ref, o_ref, *, tq, lambda_init):
    i = pl.program_id(2)
    q = q_ref[0] * jnp.asarray(A_QK_DIM ** -0.5, BF16)
    lane = lax.broadcasted_iota(jnp.int32, q.shape, 1)
    zero = jnp.zeros_like(q)
    qq = jnp.concatenate([jnp.where(lane < A_QK_DIM, q, zero),
                          jnp.where(lane >= A_QK_DIM, q, zero)], axis=0)

    kb0 = pl.multiple_of(jnp.maximum(i - 1, 0) * tq, tq)
    kb = k_ref[0, pl.ds(kb0, 2 * tq), :]
    vb = v_ref[0, pl.ds(kb0, 2 * tq), :]
    s = lax.dot_general(qq, kb, NT, preferred_element_type=F32) + band_ref[0, 0]
    m = jnp.max(s, axis=-1, keepdims=True)
    p = jnp.exp(s - m)
    l = jnp.sum(p, axis=-1, keepdims=True)
    acc = jnp.dot(p.astype(BF16), vb, preferred_element_type=F32)

    n_far = jnp.maximum(i - 1, 0)

    def logits(j):
        return lax.dot_general(qq, k_ref[0, pl.ds(pl.multiple_of(j * tq, tq), tq), :], NT,
                               preferred_element_type=F32)

    def body(j, carry):
        m, l, acc, s = carry
        s_next = logits(jnp.minimum(j + 1, n_far - 1))
        vj = v_ref[0, pl.ds(pl.multiple_of(j * tq, tq), tq), :]
        m_new = jnp.maximum(m, jnp.max(s, axis=-1, keepdims=True))
        alpha = jnp.exp(m - m_new)
        p = jnp.exp(s - m_new)
        l = alpha * l + jnp.sum(p, axis=-1, keepdims=True)
        acc = alpha * acc + jnp.dot(p.astype(BF16), vj, preferred_element_type=F32)
        return m_new, l, acc, s_next

    m, l, acc, _ = lax.fori_loop(0, n_far, body, (m, l, acc, logits(0)))

    lp = lam_ref[...]
    lam = (jnp.exp(jnp.sum(lp[0:1] * lp[1:2], axis=-1, keepdims=True))
           - jnp.exp(jnp.sum(lp[2:3] * lp[3:4], axis=-1, keepdims=True)) + lambda_init)
    o = acc / l
    o = o[:tq] - lam * o[tq:]
    o_ref[0] = _rms(o) * g_ref[...] * (1.0 - lambda_init)


def _diff_attention(pa, band, lam_par, subln_g, lambda_init, tq):
    Bn, S, _ = pa.shape
    W = A_HEAD_W
    kern = functools.partial(_attn_kernel, tq=tq, lambda_init=lambda_init)
    return pl.pallas_call(
        kern,
        grid=(Bn, A_HEADS, S // tq),
        in_specs=[
            pl.BlockSpec((4, A_QK_DIM), lambda b, h, i: (0, 0)),
            pl.BlockSpec((1, tq, W), lambda b, h, i: (b, i, h)),
            pl.BlockSpec((1, S, W), lambda b, h, i: (b, 0, A_HEADS + h)),
            pl.BlockSpec((1, S, W), lambda b, h, i: (b, 0, 2 * A_HEADS + h)),
            pl.BlockSpec((1, 1, 2 * tq, 2 * tq), lambda b, h, i: (jnp.minimum(i, 1), h, 0, 0)),
            pl.BlockSpec((1, W), lambda b, h, i: (0, 0)),
        ],
        out_specs=pl.BlockSpec((1, tq, W), lambda b, h, i: (b, i, h)),
        out_shape=jax.ShapeDtypeStruct((Bn, S, A_WIDTH), F32),
        compiler_params=_cparams("arbitrary", "arbitrary", "arbitrary"),
        name="diff_attn",
    )(lam_par, pa, pa, pa, band, subln_g.reshape(1, W))


def _head_ones(n):
    r = lax.broadcasted_iota(jnp.int32, (n, n), 0) // B_HEAD_DIM
    c = lax.broadcasted_iota(jnp.int32, (n, n), 1) // B_HEAD_DIM
    return (r == c).astype(BF16)


def _rwkv_prep_kernel(pb_ref, prev_ref, mu_ref, w0_ref, w2_ref, a0_ref, a2_ref, g2_ref,
                      kk_ref, ka_ref, rk_ref,
                      rt_ref, at_ref, kt_ref, bt_ref, v_ref, wc_ref, bonus_ref, g_ref, *, tm):
    i = pl.program_id(1)
    C = RWKV_CHUNK
    x = pb_ref[0]
    row = lax.broadcasted_iota(jnp.int32, x.shape, 0)
    last = prev_ref[0, 7:8, :] * (i > 0).astype(F32)
    prev = jnp.where(row == 0, last, pltpu.roll(x, 1, 0))
    p = x + (prev - x) * mu_ref[...]
    o1, o2, o3 = B_WIDTH, 2 * B_WIDTH, 3 * B_WIDTH
    r, k, v = p[:, :o1], p[:, o1:o2], p[:, o2:o3]
    lora = p[:, o3:o3 + B_DECAY_LORA + B_AAA_LORA]
    gd = p[:, o3 + B_DECAY_LORA + B_AAA_LORA:]

    z = -(w0_ref[...] + _mm3(jnp.tanh(lora), w2_ref[...]))
    softplus = jnp.maximum(z, 0.0) + jnp.log(1.0 + jnp.exp(-jnp.abs(z)))
    logw = -jnp.exp(-softplus - 0.5)
    a = _sigmoid(a0_ref[...] + _mm3(lora, a2_ref[...]))
    g_ref[0] = _mm3(_sigmoid(gd), g2_ref[...])

    ones = _head_ones(B_WIDTH)
    kk = k * kk_ref[...]
    kk = kk * lax.rsqrt(jnp.maximum(_mm2(kk * kk, ones), 1e-24))
    k2 = k * (1.0 + (a - 1.0) * ka_ref[...])
    bonus_ref[0] = _mm2(r * k2 * rk_ref[...], ones) * v

    t_in = lax.broadcasted_iota(jnp.int32, (tm, B_WIDTH), 0) % C
    cum = logw
    sh = 1
    while sh < C:
        cum = cum + jnp.where(t_in >= sh, pltpu.roll(cum, sh, 0), 0.0)
        sh *= 2
    n = tm // C
    wc_ref[0] = jnp.exp(jnp.sum(logw.reshape(n, C, B_WIDTH), axis=1))
    e_pos = jnp.exp(cum)
    e_neg = jnp.exp(-cum)
    rt_ref[0] = r * e_pos
    at_ref[0] = -kk * jnp.exp(cum - logw)
    kt_ref[0] = k2 * e_neg
    bt_ref[0] = kk * a * e_neg
    v_ref[0] = v


def _rwkv_prep(pbc, mu, w0, w2p, a0, a2p, g2, k_k, k_a, r_k, tm):
    Bn, S, _ = pbc.shape
    W = B_WIDTH
    nl = B_DECAY_LORA + B_AAA_LORA
    row = lambda a: a.reshape(1, -1)
    full = lambda shp: pl.BlockSpec(shp, lambda b, i: (0,) * len(shp))
    seq = pl.BlockSpec((1, tm, W), lambda b, i: (b, i, 0))
    seq_shape = jax.ShapeDtypeStruct((Bn, S, W), F32)
    n = tm // RWKV_CHUNK
    return pl.pallas_call(
        functools.partial(_rwkv_prep_kernel, tm=tm),
        grid=(Bn, S // tm),
        in_specs=[
            pl.BlockSpec((1, tm, B_COLS), lambda b, i: (b, i, 0)),
            pl.BlockSpec((1, 8, B_COLS), lambda b, i: (b, jnp.maximum(i * (tm // 8) - 1, 0), 0)),
            full((1, B_COLS)), full((1, W)), full((nl, W)), full((1, W)), full((nl, W)),
            full((B_GATE_LORA, W)), full((1, W)), full((1, W)), full((1, W)),
        ],
        out_specs=[seq, seq, seq, seq, seq,
                   pl.BlockSpec((1, n, W), lambda b, i: (b, i, 0)), seq, seq],
        out_shape=[seq_shape] * 5 + [jax.ShapeDtypeStruct((Bn, S // RWKV_CHUNK, W), F32)] + [seq_shape] * 2,
        compiler_params=_cparams("arbitrary", "arbitrary"),
        name="rwkv_prep",
    )(pbc, pbc, row(mu), row(w0), w2p, row(a0), a2p, g2, row(k_k), row(k_a), row(r_k))


def _rwkv_scan_kernel(rt_ref, at_ref, kt_ref, bt_ref, v_ref, wc_ref, bonus_ref, g_ref,
                      lng_ref, lnb_ref, o_ref, state, *, tt):
    C = RWKV_CHUNK
    W = B_WIDTH

    @pl.when(pl.program_id(1) == 0)
    def _():
        state[...] = jnp.zeros_like(state)

    lane_head = lax.broadcasted_iota(jnp.int32, (C, W), 1) // B_HEAD_DIM
    tt_i = lax.broadcasted_iota(jnp.int32, (C, W), 0)
    ss_i = lax.broadcasted_iota(jnp.int32, (C, W), 1) % C
    strict = tt_i > ss_i
    incl = tt_i >= ss_i
    eye = (tt_i == ss_i).astype(F32)
    ones = _head_ones(W)
    bd_mask = ones.astype(F32)

    def bd(x):
        return jnp.concatenate([jnp.where(lane_head == h, x, 0.0) for h in range(B_HEADS)], axis=0)

    def state_free(c):
        sl = pl.ds(pl.multiple_of(c * C, C), C)
        rt, at, kt, bt, v = rt_ref[0, sl, :], at_ref[0, sl, :], kt_ref[0, sl, :], bt_ref[0, sl, :], v_ref[0, sl, :]
        wc = wc_ref[0, pl.ds(c, 1), :]
        ar = jnp.concatenate([at, rt], axis=0)
        a_b = _mm3(ar, bd(bt), NT)
        a_k = _mm3(ar, bd(kt), NT)
        lo = jnp.where(strict, a_b[:C], 0.0)
        a_ak = jnp.where(strict, a_k[:C], 0.0)
        a_rb = jnp.where(incl, a_b[C:], 0.0)
        a_rk = jnp.where(incl, a_k[C:], 0.0)
        pw = lo
        tinv = eye + lo
        span = 2
        while span < C:
            pw = _mm3(pw, bd(pw))
            tinv = tinv + _mm3(tinv, bd(pw))
            span *= 2
        bdv = bd(v)
        abar = _mm3(tinv, bd(at))
        u0 = _mm3(tinv, bd(_mm3(a_ak, bdv)))
        y0 = _mm3(a_rk, bdv)
        kv = _mm3(v, kt * wc, TN) * bd_mask
        return jnp.concatenate([abar, rt], axis=0), u0, y0, a_rb, bt * wc, kv, wc

    def group(gi, carry):
        pre = [state_free(gi * RWKV_GROUP + j) for j in range(RWKV_GROUP)]
        s = state[...]
        ys = []
        for abar_rt, u0, y0, a_rb, btw, kv, wc in pre:
            a_s = _mm3(abar_rt, s, NT)
            u = a_s[:C] + u0
            ys.append(a_s[C:] + y0 + _mm3(a_rb, bd(u)))
            s = s * wc + _mm3(u, btw, TN) * bd_mask + kv
        state[...] = s
        y = jnp.concatenate(ys, axis=0)
        sl = pl.ds(pl.multiple_of(gi * (RWKV_GROUP * C), RWKV_GROUP * C), RWKV_GROUP * C)
        mean = _mm2(y, ones) * (1.0 / B_HEAD_DIM)
        d = y - mean
        var = _mm2(d * d, ones) * (1.0 / B_HEAD_DIM)
        yn = d * lax.rsqrt(var + B_LNX_EPS) * lng_ref[...] + lnb_ref[...]
        o_ref[0, sl, :] = (yn + bonus_ref[0, sl, :]) * g_ref[0, sl, :]
        return carry

    lax.fori_loop(0, tt // (RWKV_GROUP * C), group, 0)


def _rwkv_scan(rt, at, kt, bt, v, wc, bonus, g, lnx_g, lnx_b, tt):
    Bn, S, W = rt.shape
    n = tt // RWKV_CHUNK
    seq = pl.BlockSpec((1, tt, W), lambda b, i: (b, i, 0))
    vec = pl.BlockSpec((1, W), lambda b, i: (0, 0))
    return pl.pallas_call(
        functools.partial(_rwkv_scan_kernel, tt=tt),
        grid=(Bn, S // tt),
        in_specs=[seq, seq, seq, seq, seq, pl.BlockSpec((1, n, W), lambda b, i: (b, i, 0)), seq, seq, vec, vec],
        out_specs=seq,
        out_shape=jax.ShapeDtypeStruct((Bn, S, W), F32),
        scratch_shapes=[pltpu.VMEM((B_HEADS * B_HEAD_DIM, W), F32)],
        compiler_params=_cparams("arbitrary", "arbitrary"),
        name="rwkv_scan",
    )(rt, at, kt, bt, v, wc, bonus, g, lnx_g.reshape(1, W), lnx_b.reshape(1, W))


def _gmlp_kernel(pc_ref, lng_ref, lnb_ref, ws_ref, bs_ref, o_ref, *, tm):
    x = pc_ref[0]
    z = x * (0.5 * (1.0 + jnp.tanh(math.sqrt(2.0 / math.pi) * (x + 0.044715 * (x * x * x)))))
    u, v = z[:, :C_WIDTH], z[:, C_WIDTH:]
    mu = jnp.mean(v, axis=-1, keepdims=True)
    d = v - mu
    var = jnp.mean(d * d, axis=-1, keepdims=True)
    vn = d * lax.rsqrt(var + LN_EPS) * lng_ref[...] + lnb_ref[...]
    group = lax.broadcasted_iota(jnp.int32, (CHUNK, C_WIDTH), 1) // C_GROUP_DIM
    tril = (lax.broadcasted_iota(jnp.int32, (CHUNK, CHUNK), 0)
            >= lax.broadcasted_iota(jnp.int32, (CHUNK, CHUNK), 1))
    ws = [jnp.where(tril, ws_ref[gi], 0.0).astype(BF16) for gi in range(C_GROUPS)]
    for c in range(tm // CHUNK):
        sl = slice(c * CHUNK, (c + 1) * CHUNK)
        vc = vn[sl].astype(BF16)
        sv = bs_ref[...]
        for gi in range(C_GROUPS):
            t = jnp.dot(ws[gi], vc, preferred_element_type=F32)
            sv = sv + jnp.where(group == gi, t, 0.0)
        o_ref[0, sl, :] = u[sl] * sv


def _gmlp(pbc, ln_g, ln_b, w_s, b_s, tm):
    Bn, S, _ = pbc.shape
    bs_wide = jnp.repeat(jnp.transpose(b_s), C_GROUP_DIM, axis=1)
    return pl.pallas_call(
        functools.partial(_gmlp_kernel, tm=tm),
        grid=(Bn, S // tm),
        in_specs=[
            pl.BlockSpec((1, tm, C_COLS), lambda b, i: (b, i, B_COLS // C_COLS)),
            pl.BlockSpec((1, C_WIDTH), lambda b, i: (0, 0)),
            pl.BlockSpec((1, C_WIDTH), lambda b, i: (0, 0)),
            pl.BlockSpec((C_GROUPS, CHUNK, CHUNK), lambda b, i: (0, 0, 0)),
            pl.BlockSpec((CHUNK, C_WIDTH), lambda b, i: (0, 0)),
        ],
        out_specs=pl.BlockSpec((1, tm, C_WIDTH), lambda b, i: (b, i, 0)),
        out_shape=jax.ShapeDtypeStruct((Bn, S, C_WIDTH), F32),
        compiler_params=_cparams("arbitrary", "arbitrary"),
        name="gmlp",
    )(pbc, ln_g.reshape(1, -1), ln_b.reshape(1, -1), w_s, bs_wide)


def _mid_kernel(ya_ref, yb_ref, yc_ref, x_ref, woa_ref, wob_ref, woc_ref, gpost_ref, g1_ref,
                gpre_ref, sc_ref, sh_ref, wr_ref, ws1_ref, ws3_ref, ws2_ref,
                xo_ref, h_ref, score_ref, shared_ref):
    y = (_mm(ya_ref[0], woa_ref[...]) + _mm(yb_ref[0], wob_ref[...]) + _mm(yc_ref[0], woc_ref[...]))
    xn = x_ref[0] + g1_ref[0] * (_rms(y) * gpost_ref[...])
    xo_ref[0] = xn
    h = _rms(xn) * gpre_ref[...] * (1.0 + sc_ref[0]) + sh_ref[0]
    h_ref[0] = h
    score_ref[0] = _sigmoid(_mm3(wr_ref[...], h, NT))
    hb = h.astype(BF16)
    t = _silu(jnp.dot(hb, ws1_ref[...], preferred_element_type=F32)) * jnp.dot(
        hb, ws3_ref[...], preferred_element_type=F32)
    shared_ref[0] = jnp.dot(t.astype(BF16), ws2_ref[...], preferred_element_type=F32)


def _mid(ya, yb, yc, x, woa, wob, woc, gpost, g1, gpre, sc, sh, wr, ws1, ws3, ws2, tm):
    Bn, S, D = x.shape
    NR = wr.shape[0]
    F = ws1.shape[1]
    seq = lambda w: pl.BlockSpec((1, tm, w), lambda b, i: (b, i, 0))
    full = lambda shp: pl.BlockSpec(shp, lambda b, i: (0,) * len(shp))
    per_b = pl.BlockSpec((1, 1, D), lambda b, i: (b, 0, 0))
    return pl.pallas_call(
        _mid_kernel,
        grid=(Bn, S // tm),
        in_specs=[seq(A_WIDTH), seq(B_WIDTH), seq(C_WIDTH), seq(D),
                  full((A_WIDTH, D)), full((B_WIDTH, D)), full((C_WIDTH, D)),
                  full((1, D)), per_b, full((1, D)), per_b, per_b,
                  full((NR, D)), full((D, F)), full((D, F)), full((F, D))],
        out_specs=[seq(D), seq(D), pl.BlockSpec((1, NR, tm), lambda b, i: (b, 0, i)), seq(D)],
        out_shape=[jax.ShapeDtypeStruct((Bn, S, D), F32), jax.ShapeDtypeStruct((Bn, S, D), F32),
                   jax.ShapeDtypeStruct((Bn, NR, S), F32), jax.ShapeDtypeStruct((Bn, S, D), F32)],
        compiler_params=_cparams("arbitrary", "arbitrary"),
        name="mid",
    )(ya, yb, yc, x, woa, wob, woc, gpost.reshape(1, D), g1, gpre.reshape(1, D), sc, sh, wr, ws1, ws3, ws2)


def _first_argmax(vals, iota, n):
    m = jnp.max(vals, axis=0, keepdims=True)
    idx = jnp.min(jnp.where(vals == m, iota, n), axis=0, keepdims=True)
    return m, idx


def _route_kernel(sc_ref, bias_ref, e_ref, w_ref, r_ref, cnt_ref, carry, *, tm):
    @pl.when((pl.program_id(0) == 0) & (pl.program_id(1) == 0))
    def _():
        carry[...] = jnp.zeros_like(carry)

    G = EXPERTS_PER_GROUP
    s = sc_ref[0]
    biased = s + bias_ref[...]
    neg_inf = jnp.float32(-jnp.inf)
    io8 = lax.broadcasted_iota(jnp.int32, (G, tm), 0)
    gs_rows = []
    for g in range(N_GROUPS):
        blk = biased[g * G:(g + 1) * G]
        m1, i1 = _first_argmax(blk, io8, G)
        m2 = jnp.max(jnp.where(io8 == i1, neg_inf, blk), axis=0, keepdims=True)
        gs_rows.append(m1 + m2)
    gs = jnp.concatenate(gs_rows, axis=0)
    gio = lax.broadcasted_iota(jnp.int32, (N_GROUPS, tm), 0)
    gsel = jnp.zeros((N_GROUPS, tm), jnp.bool_)
    for _ in range(TOPK_GROUPS):
        _, gi = _first_argmax(gs, gio, N_GROUPS)
        pick = gio == gi
        gsel = gsel | pick
        gs = jnp.where(pick, neg_inf, gs)
    masked = jnp.concatenate(
        [jnp.where(gsel[g:g + 1], biased[g * G:(g + 1) * G], neg_inf) for g in range(N_GROUPS)], axis=0)

    eio = lax.broadcasted_iota(jnp.int32, (N_EXPERTS, tm), 0)
    picks, e_rows, s_rows = [], [], []
    for _ in range(TOP_K):
        _, ei = _first_argmax(masked, eio, N_EXPERTS)
        pick = eio == ei
        picks.append(pick)
        e_rows.append(ei)
        s_rows.append(jnp.sum(jnp.where(pick, s, 0.0), axis=0, keepdims=True))
        masked = jnp.where(pick, neg_inf, masked)
    top_s = jnp.concatenate(s_rows, axis=0)
    w_ref[...] = top_s / (jnp.sum(top_s, axis=0, keepdims=True) + 1e-20) * ROUTED_SCALE
    e_ref[...] = jnp.concatenate(e_rows, axis=0)

    sel = jnp.zeros((N_EXPERTS, tm), F32)
    for pick in picks:
        sel = sel + pick.astype(F32)
    before = (lax.broadcasted_iota(jnp.int32, (tm, tm), 0) < lax.broadcasted_iota(jnp.int32, (tm, tm), 1))
    pos = carry[...] + jnp.dot(sel.astype(BF16), before.astype(BF16), preferred_element_type=F32)
    r_ref[...] = jnp.concatenate(
        [jnp.sum(jnp.where(pick, pos, 0.0), axis=0, keepdims=True) for pick in picks], axis=0).astype(jnp.int32)
    total = carry[...] + jnp.sum(sel, axis=1, keepdims=True)
    carry[...] = total
    cnt_ref[...] = jnp.broadcast_to(total, cnt_ref.shape).astype(jnp.int32)


def _route(scores_t, e_bias, tm):
    Bn, _, S = scores_t.shape
    T = Bn * S
    nt = S // tm
    tok = pl.BlockSpec((TOP_K, tm), lambda b, i: (0, b * nt + i))
    return pl.pallas_call(
        functools.partial(_route_kernel, tm=tm),
        grid=(Bn, nt),
        in_specs=[pl.BlockSpec((1, N_EXPERTS, tm), lambda b, i: (b, 0, i)),
                  pl.BlockSpec((N_EXPERTS, 1), lambda b, i: (0, 0))],
        out_specs=[tok, tok, tok, pl.BlockSpec((N_EXPERTS, V7X_LANES), lambda b, i: (0, 0))],
        out_shape=[jax.ShapeDtypeStruct((TOP_K, T), jnp.int32), jax.ShapeDtypeStruct((TOP_K, T), F32),
                   jax.ShapeDtypeStruct((TOP_K, T), jnp.int32),
                   jax.ShapeDtypeStruct((N_EXPERTS, V7X_LANES), jnp.int32)],
        scratch_shapes=[pltpu.VMEM((N_EXPERTS, 1), F32)],
        compiler_params=_cparams("arbitrary", "arbitrary"),
        name="route",
    )(scores_t, e_bias.reshape(N_EXPERTS, 1))


def _dest_kernel(start_ref, e_ref, r_ref, o_ref):
    e = e_ref[...]
    acc = r_ref[...]
    for ex in range(N_EXPERTS):
        acc = acc + jnp.where(e == ex, start_ref[ex], 0)
    o_ref[0] = acc


def _dest_rows(pad_start, eidx, rank, tt):
    K_, T = eidx.shape
    grid_spec = pltpu.PrefetchScalarGridSpec(
        num_scalar_prefetch=1,
        grid=(T // tt,),
        in_specs=[pl.BlockSpec((K_, tt), lambda i, st: (0, i)), pl.BlockSpec((K_, tt), lambda i, st: (0, i))],
        out_specs=pl.BlockSpec((1, K_, tt), lambda i, st: (i, 0, 0)),
    )
    return pl.pallas_call(
        _dest_kernel,
        grid_spec=grid_spec,
        out_shape=jax.ShapeDtypeStruct((T // tt, K_, tt), jnp.int32),
        compiler_params=_cparams("arbitrary"),
        name="dest_rows",
    )(pad_start, eidx, rank)


def _dispatch_kernel(fill_ref, dest_hbm, h_ref, xs_hbm, dest_smem, zbuf, sem_idx, sem_rows, sem_zero):
    i = pl.program_id(0)
    EB = EXPERT_BLOCK

    def zero_copy(ex):
        return pltpu.make_async_copy(zbuf, xs_hbm.at[pl.ds(pl.multiple_of(fill_ref[ex], EB), EB), :], sem_zero)

    @pl.when(i == 0)
    def _():
        zbuf[...] = jnp.zeros_like(zbuf)

        def z_start(ex, c):
            @pl.when(fill_ref[ex] >= 0)
            def _():
                zero_copy(ex).start()
            return c

        def z_wait(ex, c):
            @pl.when(fill_ref[ex] >= 0)
            def _():
                zero_copy(ex).wait()
            return c

        lax.fori_loop(0, N_EXPERTS, z_start, 0)
        lax.fori_loop(0, N_EXPERTS, z_wait, 0)

    idx_copy = pltpu.make_async_copy(dest_hbm.at[i], dest_smem, sem_idx)
    idx_copy.start()
    idx_copy.wait()
    tt = h_ref.shape[0]

    def row_copy(t, k):
        return pltpu.make_async_copy(h_ref.at[pl.ds(t, 1), :], xs_hbm.at[pl.ds(dest_smem[k, t], 1), :], sem_rows)

    def issue(t, c):
        for k in range(TOP_K):
            row_copy(t, k).start()
        return c

    def drain(t, c):
        for k in range(TOP_K):
            row_copy(t, k).wait()
        return c

    lax.fori_loop(0, tt, issue, 0)
    lax.fori_loop(0, tt, drain, 0)


def _dispatch(fill_blocks, dest, h, n_rows):
    T, D = h.shape
    nt, K_, tt = dest.shape
    grid_spec = pltpu.PrefetchScalarGridSpec(
        num_scalar_prefetch=1,
        grid=(nt,),
        in_specs=[pl.BlockSpec(memory_space=pl.ANY), pl.BlockSpec((tt, D), lambda i, fl: (i, 0))],
        out_specs=pl.BlockSpec(memory_space=pl.ANY),
        scratch_shapes=[pltpu.SMEM((K_, tt), jnp.int32), pltpu.VMEM((EXPERT_BLOCK, D), F32),
                        pltpu.SemaphoreType.DMA(()), pltpu.SemaphoreType.DMA(()), pltpu.SemaphoreType.DMA(())],
    )
    return pl.pallas_call(
        _dispatch_kernel,
        grid_spec=grid_spec,
        out_shape=jax.ShapeDtypeStruct((n_rows, D), F32),
        compiler_params=_cparams("arbitrary"),
        name="dispatch",
    )(fill_blocks, dest, h)


def _expert_kernel(blk_e_ref, n_used_ref, x_ref, w1_ref, w3_ref, w2_ref, o_ref, w1b, w3b, w2b):
    i = pl.program_id(0)

    @pl.when((i == 0) | (blk_e_ref[i] != blk_e_ref[jnp.maximum(i - 1, 0)]))
    def _():
        w1b[...] = w1_ref[0].astype(BF16)
        w3b[...] = w3_ref[0].astype(BF16)
        w2b[...] = w2_ref[0].astype(BF16)

    @pl.when(i < n_used_ref[0])
    def _():
        xb = x_ref[...].astype(BF16)
        t = _silu(jnp.dot(xb, w1b[...], preferred_element_type=F32)) * jnp.dot(
            xb, w3b[...], preferred_element_type=F32)
        o_ref[...] = jnp.dot(t.astype(BF16), w2b[...], preferred_element_type=F32)


def _experts(blk_e, n_used, xs, w1, w3, w2, layer):
    P, D = xs.shape
    EB = EXPERT_BLOCK
    n_blocks = blk_e.shape[0]
    F = w1.shape[3]
    rows = pl.BlockSpec((EB, D), lambda i, be, nu: (jnp.minimum(i, nu[0] - 1), 0))
    grid_spec = pltpu.PrefetchScalarGridSpec(
        num_scalar_prefetch=2,
        grid=(n_blocks,),
        in_specs=[
            rows,
            pl.BlockSpec((None, 1, D, F), lambda i, be, nu: (layer, be[i], 0, 0)),
            pl.BlockSpec((None, 1, D, F), lambda i, be, nu: (layer, be[i], 0, 0)),
            pl.BlockSpec((None, 1, F, D), lambda i, be, nu: (layer, be[i], 0, 0)),
        ],
        out_specs=rows,
        scratch_shapes=[pltpu.VMEM((D, F), BF16), pltpu.VMEM((D, F), BF16), pltpu.VMEM((F, D), BF16)],
    )
    return pl.pallas_call(
        _expert_kernel,
        grid_spec=grid_spec,
        out_shape=jax.ShapeDtypeStruct((P, D), F32),
        compiler_params=_cparams("arbitrary"),
        name="experts",
    )(blk_e, n_used, xs, w1, w3, w2)


def _block_layout(counts, n_blocks):
    EB = EXPERT_BLOCK
    padded = (counts + EB - 1) // EB * EB
    pad_end = jnp.cumsum(padded)
    pad_start = pad_end - padded
    blk_row = (jnp.arange(n_blocks) * EB)[:, None]
    blk_e = jnp.minimum(jnp.sum((pad_end[None, :] <= blk_row).astype(jnp.int32), axis=1), N_EXPERTS - 1)
    n_used = (pad_end[-1] // EB).astype(jnp.int32).reshape(1)
    fill = jnp.where(padded > counts, pad_end - EB, -1).astype(jnp.int32)
    return pad_start.astype(jnp.int32), blk_e, n_used, fill


def _combine_kernel(dest_hbm, ys_hbm, w_ref, x_ref, shared_ref, gpost_ref, g2_ref, o_ref,
                    dest_smem, buf, sem_idx, sem_rows, *, tt, nt):
    i = pl.program_id(0) * nt + pl.program_id(1)
    idx_copy = pltpu.make_async_copy(dest_hbm.at[i], dest_smem, sem_idx)
    idx_copy.start()
    idx_copy.wait()

    def row_copy(t, k):
        return pltpu.make_async_copy(ys_hbm.at[pl.ds(dest_smem[k, t], 1), :], buf.at[k, pl.ds(t, 1), :], sem_rows)

    def issue(t, c):
        for k in range(TOP_K):
            row_copy(t, k).start()
        return c

    def drain(t, c):
        for k in range(TOP_K):
            row_copy(t, k).wait()
        return c

    lax.fori_loop(0, tt, issue, 0)
    lax.fori_loop(0, tt, drain, 0)
    w = w_ref[...]
    y = shared_ref[0]
    for k in range(TOP_K):
        y = y + w[:, k:k + 1] * buf[k]
    o_ref[0] = x_ref[0] + g2_ref[0] * (_rms(y) * gpost_ref[...])


def _combine(dest, ys, w_tok, x, shared, gpost, g2):
    Bn, S, D = x.shape
    _, K_, tt = dest.shape
    nt = S // tt
    seq = pl.BlockSpec((1, tt, D), lambda b, i: (b, i, 0))
    return pl.pallas_call(
        functools.partial(_combine_kernel, tt=tt, nt=nt),
        grid=(Bn, nt),
        in_specs=[pl.BlockSpec(memory_space=pl.ANY), pl.BlockSpec(memory_space=pl.ANY),
                  pl.BlockSpec((tt, K_), lambda b, i: (b * nt + i, 0)), seq, seq,
                  pl.BlockSpec((1, D), lambda b, i: (0, 0)), pl.BlockSpec((1, 1, D), lambda b, i: (b, 0, 0))],
        out_specs=seq,
        out_shape=jax.ShapeDtypeStruct((Bn, S, D), F32),
        scratch_shapes=[pltpu.SMEM((K_, tt), jnp.int32), pltpu.VMEM((K_, tt, D), F32),
                        pltpu.SemaphoreType.DMA(()), pltpu.SemaphoreType.DMA(())],
        compiler_params=_cparams("arbitrary", "arbitrary"),
        name="combine",
    )(dest, ys, w_tok, x, shared, gpost.reshape(1, D), g2)


def kernel(x, c, w_ada, b_ada, norm_pre_mix, norm_post_mix, norm_pre_ffn, norm_post_ffn, w_in, w_out, rel_bias_table, diff_lambda, diff_subln, rwkv_mu, rwkv_w0, rwkv_w2, rwkv_a0, rwkv_a2, rwkv_g2, rwkv_k_k, rwkv_k_a, rwkv_r_k, rwkv_lnx_g, rwkv_lnx_b, gmlp_ln_g, gmlp_ln_b, gmlp_w_s, gmlp_b_s, router_w, router_bias, exp_w1, exp_w3, exp_w2, shared_w1, shared_w3, shared_w2):
    Bn, S, D = x.shape
    depth = w_ada.shape[0]
    tm = min(256, S)
    tq = min(256, S // 2)
    t_rwkv = min(512, S)

    mod = _adaln(c, w_ada, b_ada)
    band = _attn_band(rel_bias_table, tq)
    zpad = jnp.zeros((B_DECAY_LORA, B_WIDTH), F32)
    for l in range(depth):
        sh1, sc1, g1, sh2, sc2, g2 = [m.reshape(Bn, 1, D) for m in jnp.split(mod[l], 6, axis=-1)]
        w_in_b = w_in[l].astype(BF16)
        pa, pbc = _inproj(x, norm_pre_mix[l], sc1, sh1, w_in_b[:, :A_COLS], w_in_b[:, A_COLS:], tm)
        lambda_init = 0.8 - 0.6 * math.exp(-0.3 * l)
        ya = _diff_attention(pa, band, diff_lambda[l], diff_subln[l], lambda_init, tq)
        prep = _rwkv_prep(pbc, rwkv_mu[l], rwkv_w0[l], jnp.concatenate([rwkv_w2[l], zpad], axis=0),
                          rwkv_a0[l], jnp.concatenate([zpad, rwkv_a2[l]], axis=0), rwkv_g2[l],
                          rwkv_k_k[l], rwkv_k_a[l], rwkv_r_k[l].reshape(-1), t_rwkv)
        yb = _rwkv_scan(*prep, rwkv_lnx_g[l], rwkv_lnx_b[l], t_rwkv)
        yc = _gmlp(pbc, gmlp_ln_g[l], gmlp_ln_b[l], gmlp_w_s[l], gmlp_b_s[l], tm)

        w_out_b = w_out[l].astype(BF16)
        wr_t = jnp.pad(jnp.transpose(router_w[l]), ((0, V7X_LANES - N_EXPERTS), (0, 0)))
        x, h, scores_t, shared = _mid(
            ya, yb, yc, x, w_out_b[:A_WIDTH], w_out_b[A_WIDTH:A_WIDTH + B_WIDTH], w_out_b[A_WIDTH + B_WIDTH:],
            norm_post_mix[l], g1, norm_pre_ffn[l], sc2, sh2, wr_t,
            shared_w1[l].astype(BF16), shared_w3[l].astype(BF16), shared_w2[l].astype(BF16), tm)

        T = Bn * S
        n_blocks = -(-T * TOP_K // EXPERT_BLOCK) + N_EXPERTS
        eidx, wgt, rank, cnt = _route(scores_t, router_bias[l], tm)
        pad_start, blk_e, n_used, fill = _block_layout(cnt[:, 0], n_blocks)
        dest = _dest_rows(pad_start, eidx, rank, tm)
        xs = _dispatch(fill, dest, h.reshape(T, D), n_blocks * EXPERT_BLOCK)
        ys = _experts(blk_e, n_used, xs, exp_w1, exp_w3, exp_w2, l)
        x = _combine(dest, ys, jnp.transpose(wgt), x, shared, norm_post_ffn[l], g2)
    return x
```

```python
import functools
import math

import jax
import jax.numpy as jnp
from jax import lax
from jax.experimental import pallas as pl
from jax.experimental.pallas import tpu as pltpu

F32 = jnp.float32
BF16 = jnp.bfloat16

A_HEADS = 4
A_QK_DIM = 64
A_HEAD_W = 2 * A_QK_DIM
A_WIDTH = A_HEADS * A_HEAD_W
N_BUCKETS = 32
MAX_DISTANCE = 128
B_HEADS = 4
B_HEAD_DIM = 64
B_WIDTH = B_HEADS * B_HEAD_DIM
B_DECAY_LORA = 64
B_AAA_LORA = 64
B_GATE_LORA = 128
B_LNX_EPS = 64e-5
C_GROUPS = 4
C_GROUP_DIM = 64
C_WIDTH = C_GROUPS * C_GROUP_DIM
CHUNK = 128
A_COLS = 3 * A_WIDTH
B_COLS = 3 * B_WIDTH + B_DECAY_LORA + B_AAA_LORA + B_GATE_LORA
C_COLS = 2 * C_WIDTH
N_EXPERTS = 64
TOP_K = 8
N_GROUPS = 8
TOPK_GROUPS = 4
EXPERTS_PER_GROUP = N_EXPERTS // N_GROUPS
ROUTED_SCALE = 2.5
EXPERT_BLOCK = 256
RMS_EPS = 1e-6
LN_EPS = 1e-5
NEG_BIG = -1e30

V7X_LANES = 128
V7X_VMEM_LIMIT_BYTES = 56 * 1024 * 1024
RWKV_CHUNK = 64
RWKV_GROUP = 4

NN = (((1,), (0,)), ((), ()))
NT = (((1,), (1,)), ((), ()))
TN = (((0,), (0,)), ((), ()))


def _cparams(*sem):
    return pltpu.CompilerParams(dimension_semantics=sem, vmem_limit_bytes=V7X_VMEM_LIMIT_BYTES)


def _mm(a, b, dims=NN):
    return lax.dot_general(a.astype(BF16), b.astype(BF16), dims, preferred_element_type=F32)


def _split(a):
    hi = a.astype(BF16)
    lo = (a - hi.astype(F32)).astype(BF16)
    return hi, lo


def _mm3(a, b, dims=NN):
    ah, al = _split(a)
    bh, bl = _split(b)
    d = lambda x, y: lax.dot_general(x, y, dims, preferred_element_type=F32)
    return d(ah, bh) + d(ah, bl) + d(al, bh)


def _mm2(a, b_exact, dims=NN):
    ah, al = _split(a)
    d = lambda x: lax.dot_general(x, b_exact, dims, preferred_element_type=F32)
    return d(ah) + d(al)


def _pack_bf16_pair(x):
    n = x.shape[1] // 2
    bits = lax.bitcast_convert_type(x.astype(BF16).astype(F32), jnp.uint32)
    return (bits[:, :n] >> 16) | bits[:, n:]


def _unpack_bf16_pair(u):
    lo = lax.bitcast_convert_type(u << 16, F32)
    hi = lax.bitcast_convert_type(u & jnp.uint32(0xFFFF0000), F32)
    return lo, hi


def _rms(x, eps=RMS_EPS):
    return x * lax.rsqrt(jnp.mean(x * x, axis=-1, keepdims=True) + eps)


def _sigmoid(x):
    return 1.0 / (1.0 + jnp.exp(-x))


def _silu(x):
    return x * _sigmoid(x)


def _adaln_kernel(c_ref, w_ref, b_ref, o_ref):
    c = c_ref[...]
    o_ref[0] = _mm3(_silu(c), w_ref[0]) + b_ref[0]


def _adaln(c, w_ada, b_ada):
    L, D, N = w_ada.shape
    Bn = c.shape[0]
    tn = min(N, 1536)
    return pl.pallas_call(
        _adaln_kernel,
        grid=(L, N // tn),
        in_specs=[
            pl.BlockSpec((Bn, D), lambda l, j: (0, 0)),
            pl.BlockSpec((1, D, tn), lambda l, j: (l, 0, j)),
            pl.BlockSpec((1, 1, tn), lambda l, j: (l, 0, j)),
        ],
        out_specs=pl.BlockSpec((1, Bn, tn), lambda l, j: (l, 0, j)),
        out_shape=jax.ShapeDtypeStruct((L, Bn, N), F32),
        compiler_params=_cparams("arbitrary", "arbitrary"),
        name="adaln",
    )(c, w_ada, b_ada.reshape(L, 1, N))


def _inproj_kernel(x_ref, g_ref, sc_ref, sh_ref, wa_ref, wbc_ref, oa_ref, obc_ref):
    x = x_ref[0]
    h = _rms(x) * g_ref[...] * (1.0 + sc_ref[0]) + sh_ref[0]
    hb = h.astype(BF16)
    oa_ref[0] = jnp.dot(hb, wa_ref[...], preferred_element_type=F32).astype(BF16)
    obc_ref[0] = jnp.dot(hb, wbc_ref[...], preferred_element_type=F32)


def _inproj(x, g, sc, sh, wa, wbc, tm):
    Bn, S, D = x.shape
    na, nbc = wa.shape[1], wbc.shape[1]
    return pl.pallas_call(
        _inproj_kernel,
        grid=(Bn, S // tm),
        in_specs=[
            pl.BlockSpec((1, tm, D), lambda b, i: (b, i, 0)),
            pl.BlockSpec((1, D), lambda b, i: (0, 0)),
            pl.BlockSpec((1, 1, D), lambda b, i: (b, 0, 0)),
            pl.BlockSpec((1, 1, D), lambda b, i: (b, 0, 0)),
            pl.BlockSpec((D, na), lambda b, i: (0, 0)),
            pl.BlockSpec((D, nbc), lambda b, i: (0, 0)),
        ],
        out_specs=[
            pl.BlockSpec((1, tm, na), lambda b, i: (b, i, 0)),
            pl.BlockSpec((1, tm, nbc), lambda b, i: (b, i, 0)),
        ],
        out_shape=[
            jax.ShapeDtypeStruct((Bn, S, na), BF16),
            jax.ShapeDtypeStruct((Bn, S, nbc), F32),
        ],
        compiler_params=_cparams("arbitrary", "arbitrary"),
        name="inproj",
    )(x, g.reshape(1, D), sc, sh, wa, wbc)


def _t5_bucket(dist):
    n = jnp.maximum(dist, 0)
    max_exact = N_BUCKETS // 2
    nf = jnp.maximum(n, 1).astype(F32)
    large = max_exact + (jnp.log(nf / max_exact) / math.log(MAX_DISTANCE / max_exact)
                         * (N_BUCKETS - max_exact)).astype(jnp.int32)
    large = jnp.minimum(large, N_BUCKETS - 1)
    return jnp.where(n < max_exact, n, large)


def _attn_band(table, tq):
    far = table[N_BUCKETS - 1].astype(F32)
    L = 3 * tq
    m = jnp.arange(L)
    m = jnp.where(m < 2 * tq, m, m - L)
    bands = []
    for off in (0, tq):
        dist = off - m
        vals = jnp.where(dist[None] >= 0, jnp.transpose(table[_t5_bucket(dist)].astype(F32)) - far[:, None],
                         NEG_BIG)
        toe = jnp.tile(vals, (1, tq))[:, :tq * (L - 1)].reshape(-1, tq, L - 1)
        bands.append(toe[:, :, :2 * tq])
    band = jnp.stack(bands)
    return jnp.concatenate([band, band], axis=2)


def _attn_kernel(lam_ref, q_ref, k_ref, v_ref, band_ref, g_ref, o_ref, *, tq, lambda_init):
    i = pl.program_id(2)
    q = q_ref[0] * jnp.asarray(A_QK_DIM ** -0.5, BF16)
    lane = lax.broadcasted_iota(jnp.int32, q.shape, 1)
    zero = jnp.zeros_like(q)
    qq = jnp.concatenate([jnp.where(lane < A_QK_DIM, q, zero),
                          jnp.where(lane >= A_QK_DIM, q, zero)], axis=0)

    kb0 = pl.multiple_of(jnp.maximum(i - 1, 0) * tq, tq)
    kb = k_ref[0, pl.ds(kb0, 2 * tq), :]
    vb = v_ref[0, pl.ds(kb0, 2 * tq), :]
    s = lax.dot_general(qq, kb, NT, preferred_element_type=F32) + band_ref[0, 0]
    m = jnp.max(s, axis=-1, keepdims=True)
    p = jnp.exp(s - m)
    l = jnp.sum(p, axis=-1, keepdims=True)
    acc = jnp.dot(p.astype(BF16), vb, preferred_element_type=F32)

    n_far = jnp.maximum(i - 1, 0)

    def logits(j):
        return lax.dot_general(qq, k_ref[0, pl.ds(pl.multiple_of(j * tq, tq), tq), :], NT,
                               preferred_element_type=F32)

    def body(j, carry):
        m, l, acc, s = carry
        s_next = logits(jnp.minimum(j + 1, n_far - 1))
        vj = v_ref[0, pl.ds(pl.multiple_of(j * tq, tq), tq), :]
        m_new = jnp.maximum(m, jnp.max(s, axis=-1, keepdims=True))
        alpha = jnp.exp(m - m_new)
        p = jnp.exp(s - m_new)
        l = alpha * l + jnp.sum(p, axis=-1, keepdims=True)
        acc = alpha * acc + jnp.dot(p.astype(BF16), vj, preferred_element_type=F32)
        return m_new, l, acc, s_next

    m, l, acc, _ = lax.fori_loop(0, n_far, body, (m, l, acc, logits(0)))

    lp = lam_ref[...]
    lam = (jnp.exp(jnp.sum(lp[0:1] * lp[1:2], axis=-1, keepdims=True))
           - jnp.exp(jnp.sum(lp[2:3] * lp[3:4], axis=-1, keepdims=True)) + lambda_init)
    o = acc / l
    o = o[:tq] - lam * o[tq:]
    o_ref[0] = _rms(o) * g_ref[...] * (1.0 - lambda_init)


def _diff_attention(pa, band, lam_par, subln_g, lambda_init, tq):
    Bn, S, _ = pa.shape
    W = A_HEAD_W
    kern = functools.partial(_attn_kernel, tq=tq, lambda_init=lambda_init)
    return pl.pallas_call(
        kern,
        grid=(Bn, A_HEADS, S // tq),
        in_specs=[
            pl.BlockSpec((4, A_QK_DIM), lambda b, h, i: (0, 0)),
            pl.BlockSpec((1, tq, W), lambda b, h, i: (b, i, h)),
            pl.BlockSpec((1, S, W), lambda b, h, i: (b, 0, A_HEADS + h)),
            pl.BlockSpec((1, S, W), lambda b, h, i: (b, 0, 2 * A_HEADS + h)),
            pl.BlockSpec((1, 1, 2 * tq, 2 * tq), lambda b, h, i: (jnp.minimum(i, 1), h, 0, 0)),
            pl.BlockSpec((1, W), lambda b, h, i: (0, 0)),
        ],
        out_specs=pl.BlockSpec((1, tq, W), lambda b, h, i: (b, i, h)),
        out_shape=jax.ShapeDtypeStruct((Bn, S, A_WIDTH), F32),
        compiler_params=_cparams("arbitrary", "arbitrary", "arbitrary"),
        name="diff_attn",
    )(lam_par, pa, pa, pa, band, subln_g.reshape(1, W))


def _head_ones(n):
    r = lax.broadcasted_iota(jnp.int32, (n, n), 0) // B_HEAD_DIM
    c = lax.broadcasted_iota(jnp.int32, (n, n), 1) // B_HEAD_DIM
    return (r == c).astype(BF16)


def _rwkv_prep_kernel(pb_ref, prev_ref, mu_ref, w0_ref, w2_ref, a0_ref, a2_ref, g2_ref,
                      kk_ref, ka_ref, rk_ref,
                      rt_ref, at_ref, kt_ref, bt_ref, v_ref, wc_ref, bonus_ref, g_ref, *, tm):
    i = pl.program_id(1)
    C = RWKV_CHUNK
    x = pb_ref[0]
    row = lax.broadcasted_iota(jnp.int32, x.shape, 0)
    last = prev_ref[0, 7:8, :] * (i > 0).astype(F32)
    prev = jnp.where(row == 0, last, pltpu.roll(x, 1, 0))
    p = x + (prev - x) * mu_ref[...]
    o1, o2, o3 = B_WIDTH, 2 * B_WIDTH, 3 * B_WIDTH
    r, k, v = p[:, :o1], p[:, o1:o2], p[:, o2:o3]
    lora = p[:, o3:o3 + B_DECAY_LORA + B_AAA_LORA]
    gd = p[:, o3 + B_DECAY_LORA + B_AAA_LORA:]

    z = -(w0_ref[...] + _mm3(jnp.tanh(lora), w2_ref[...]))
    softplus = jnp.maximum(z, 0.0) + jnp.log(1.0 + jnp.exp(-jnp.abs(z)))
    logw = -jnp.exp(-softplus - 0.5)
    a = _sigmoid(a0_ref[...] + _mm3(lora, a2_ref[...]))
    g_ref[0] = _mm3(_sigmoid(gd), g2_ref[...])

    ones = _head_ones(B_WIDTH)
    kk = k * kk_ref[...]
    kk = kk * lax.rsqrt(jnp.maximum(_mm2(kk * kk, ones), 1e-24))
    k2 = k * (1.0 + (a - 1.0) * ka_ref[...])
    bonus_ref[0] = _mm2(r * k2 * rk_ref[...], ones) * v

    t_in = lax.broadcasted_iota(jnp.int32, (tm, B_WIDTH), 0) % C
    cum = logw
    sh = 1
    while sh < C:
        cum = cum + jnp.where(t_in >= sh, pltpu.roll(cum, sh, 0), 0.0)
        sh *= 2
    n = tm // C
    wc_ref[0] = jnp.exp(jnp.sum(logw.reshape(n, C, B_WIDTH), axis=1))
    e_pos = jnp.exp(cum)
    e_neg = jnp.exp(-cum)
    rt_ref[0] = r * e_pos
    at_ref[0] = -kk * jnp.exp(cum - logw)
    kt_ref[0] = k2 * e_neg
    bt_ref[0] = kk * a * e_neg
    v_ref[0] = v


def _rwkv_prep(pbc, mu, w0, w2p, a0, a2p, g2, k_k, k_a, r_k, tm):
    Bn, S, _ = pbc.shape
    W = B_WIDTH
    nl = B_DECAY_LORA + B_AAA_LORA
    row = lambda a: a.reshape(1, -1)
    full = lambda shp: pl.BlockSpec(shp, lambda b, i: (0,) * len(shp))
    seq = pl.BlockSpec((1, tm, W), lambda b, i: (b, i, 0))
    seq_shape = jax.ShapeDtypeStruct((Bn, S, W), F32)
    n = tm // RWKV_CHUNK
    return pl.pallas_call(
        functools.partial(_rwkv_prep_kernel, tm=tm),
        grid=(Bn, S // tm),
        in_specs=[
            pl.BlockSpec((1, tm, B_COLS), lambda b, i: (b, i, 0)),
            pl.BlockSpec((1, 8, B_COLS), lambda b, i: (b, jnp.maximum(i * (tm // 8) - 1, 0), 0)),
            full((1, B_COLS)), full((1, W)), full((nl, W)), full((1, W)), full((nl, W)),
            full((B_GATE_LORA, W)), full((1, W)), full((1, W)), full((1, W)),
        ],
        out_specs=[seq, seq, seq, seq, seq,
                   pl.BlockSpec((1, n, W), lambda b, i: (b, i, 0)), seq, seq],
        out_shape=[seq_shape] * 5 + [jax.ShapeDtypeStruct((Bn, S // RWKV_CHUNK, W), F32)] + [seq_shape] * 2,
        compiler_params=_cparams("arbitrary", "arbitrary"),
        name="rwkv_prep",
    )(pbc, pbc, row(mu), row(w0), w2p, row(a0), a2p, g2, row(k_k), row(k_a), row(r_k))


def _rwkv_scan_kernel(rt_ref, at_ref, kt_ref, bt_ref, v_ref, wc_ref, bonus_ref, g_ref,
                      lng_ref, lnb_ref, o_ref, state, *, tt):
    C = RWKV_CHUNK
    W = B_WIDTH

    @pl.when(pl.program_id(1) == 0)
    def _():
        state[...] = jnp.zeros_like(state)

    lane_head = lax.broadcasted_iota(jnp.int32, (C, W), 1) // B_HEAD_DIM
    tt_i = lax.broadcasted_iota(jnp.int32, (C, W), 0)
    ss_i = lax.broadcasted_iota(jnp.int32, (C, W), 1) % C
    strict = tt_i > ss_i
    incl = tt_i >= ss_i
    eye = (tt_i == ss_i).astype(F32)
    ones = _head_ones(W)
    bd_mask = ones.astype(F32)

    head_mask = [(lane_head == h).astype(BF16) for h in range(B_HEADS)]

    def bd_split(x):
        hi, lo = _split(x)
        return (jnp.concatenate([hi * mk for mk in head_mask], axis=0),
                jnp.concatenate([lo * mk for mk in head_mask], axis=0))

    def mm_bd(a, b_split, dims=NN):
        ah, al = _split(a)
        bh, bl = b_split
        d = lambda x, y: lax.dot_general(x, y, dims, preferred_element_type=F32)
        return d(ah, bh) + d(ah, bl) + d(al, bh)

    def state_free(gi):
        G = range(RWKV_GROUP)
        sls = [pl.ds(pl.multiple_of((gi * RWKV_GROUP + j) * C, C), C) for j in G]
        rt = [rt_ref[0, sl, :] for sl in sls]
        at = [at_ref[0, sl, :] for sl in sls]
        kt = [kt_ref[0, sl, :] for sl in sls]
        bt = [bt_ref[0, sl, :] for sl in sls]
        v = [v_ref[0, sl, :] for sl in sls]
        wc = [wc_ref[0, pl.ds(gi * RWKV_GROUP + j, 1), :] for j in G]
        ar = [jnp.concatenate([at[j], rt[j]], axis=0) for j in G]
        bdb = [bd_split(bt[j]) for j in G]
        bdk = [bd_split(kt[j]) for j in G]
        a_b = [mm_bd(ar[j], bdb[j], NT) for j in G]
        a_k = [mm_bd(ar[j], bdk[j], NT) for j in G]
        lo = [jnp.where(strict, a_b[j][:C], 0.0) for j in G]
        a_ak = [jnp.where(strict, a_k[j][:C], 0.0) for j in G]
        a_rb = [jnp.where(incl, a_b[j][C:], 0.0) for j in G]
        a_rk = [jnp.where(incl, a_k[j][C:], 0.0) for j in G]
        pw = lo
        tinv = [eye + lo[j] for j in G]
        bdp = [bd_split(pw[j]) for j in G]
        span = 2
        while span < C:
            pw = [mm_bd(pw[j], bdp[j]) for j in G]
            bdp = [bd_split(pw[j]) for j in G]
            tinv = [tinv[j] + mm_bd(tinv[j], bdp[j]) for j in G]
            span *= 2
        bdv = [bd_split(v[j]) for j in G]
        bda = [bd_split(at[j]) for j in G]
        abar = [mm_bd(tinv[j], bda[j]) for j in G]
        akv = [bd_split(mm_bd(a_ak[j], bdv[j])) for j in G]
        u0 = [mm_bd(tinv[j], akv[j]) for j in G]
        y0 = [mm_bd(a_rk[j], bdv[j]) for j in G]
        kv = [_mm3(v[j], kt[j] * wc[j], TN) * bd_mask for j in G]
        return [(jnp.concatenate([abar[j], rt[j]], axis=0), u0[j], y0[j], a_rb[j], bt[j] * wc[j], kv[j], wc[j])
                for j in G]

    def group(gi, carry):
        pre = state_free(gi)
        s = state[...]
        ys = []
        for abar_rt, u0, y0, a_rb, btw, kv, wc in pre:
            a_s = _mm3(abar_rt, s, NT)
            u = a_s[:C] + u0
            ys.append(a_s[C:] + y0 + mm_bd(a_rb, bd_split(u)))
            s = s * wc + _mm3(u, btw, TN) * bd_mask + kv
        state[...] = s
        y = jnp.concatenate(ys, axis=0)
        sl = pl.ds(pl.multiple_of(gi * (RWKV_GROUP * C), RWKV_GROUP * C), RWKV_GROUP * C)
        mean = _mm2(y, ones) * (1.0 / B_HEAD_DIM)
        d = y - mean
        var = _mm2(d * d, ones) * (1.0 / B_HEAD_DIM)
        yn = d * lax.rsqrt(var + B_LNX_EPS) * lng_ref[...] + lnb_ref[...]
        o_ref[0, sl, :] = (yn + bonus_ref[0, sl, :]) * g_ref[0, sl, :]
        return carry

    lax.fori_loop(0, tt // (RWKV_GROUP * C), group, 0)


def _rwkv_scan(rt, at, kt, bt, v, wc, bonus, g, lnx_g, lnx_b, tt):
    Bn, S, W = rt.shape
    n = tt // RWKV_CHUNK
    seq = pl.BlockSpec((1, tt, W), lambda b, i: (b, i, 0))
    vec = pl.BlockSpec((1, W), lambda b, i: (0, 0))
    return pl.pallas_call(
        functools.partial(_rwkv_scan_kernel, tt=tt),
        grid=(Bn, S // tt),
        in_specs=[seq, seq, seq, seq, seq, pl.BlockSpec((1, n, W), lambda b, i: (b, i, 0)), seq, seq, vec, vec],
        out_specs=seq,
        out_shape=jax.ShapeDtypeStruct((Bn, S, W), F32),
        scratch_shapes=[pltpu.VMEM((B_HEADS * B_HEAD_DIM, W), F32)],
        compiler_params=_cparams("arbitrary", "arbitrary"),
        name="rwkv_scan",
    )(rt, at, kt, bt, v, wc, bonus, g, lnx_g.reshape(1, W), lnx_b.reshape(1, W))


def _gmlp_kernel(pc_ref, lng_ref, lnb_ref, ws_ref, bs_ref, o_ref, *, tm):
    x = pc_ref[0]
    z = x * (0.5 * (1.0 + jnp.tanh(math.sqrt(2.0 / math.pi) * (x + 0.044715 * (x * x * x)))))
    u, v = z[:, :C_WIDTH], z[:, C_WIDTH:]
    mu = jnp.mean(v, axis=-1, keepdims=True)
    d = v - mu
    var = jnp.mean(d * d, axis=-1, keepdims=True)
    vn = d * lax.rsqrt(var + LN_EPS) * lng_ref[...] + lnb_ref[...]
    group = lax.broadcasted_iota(jnp.int32, (CHUNK, C_WIDTH), 1) // C_GROUP_DIM
    tril = (lax.broadcasted_iota(jnp.int32, (CHUNK, CHUNK), 0)
            >= lax.broadcasted_iota(jnp.int32, (CHUNK, CHUNK), 1))
    ws = [jnp.where(tril, ws_ref[gi], 0.0).astype(BF16) for gi in range(C_GROUPS)]
    for c in range(tm // CHUNK):
        sl = slice(c * CHUNK, (c + 1) * CHUNK)
        vc = vn[sl].astype(BF16)
        sv = bs_ref[...]
        for gi in range(C_GROUPS):
            t = jnp.dot(ws[gi], vc, preferred_element_type=F32)
            sv = sv + jnp.where(group == gi, t, 0.0)
        o_ref[0, sl, :] = u[sl] * sv


def _gmlp(pbc, ln_g, ln_b, w_s, b_s, tm):
    Bn, S, _ = pbc.shape
    bs_wide = jnp.repeat(jnp.transpose(b_s), C_GROUP_DIM, axis=1)
    return pl.pallas_call(
        functools.partial(_gmlp_kernel, tm=tm),
        grid=(Bn, S // tm),
        in_specs=[
            pl.BlockSpec((1, tm, C_COLS), lambda b, i: (b, i, B_COLS // C_COLS)),
            pl.BlockSpec((1, C_WIDTH), lambda b, i: (0, 0)),
            pl.BlockSpec((1, C_WIDTH), lambda b, i: (0, 0)),
            pl.BlockSpec((C_GROUPS, CHUNK, CHUNK), lambda b, i: (0, 0, 0)),
            pl.BlockSpec((CHUNK, C_WIDTH), lambda b, i: (0, 0)),
        ],
        out_specs=pl.BlockSpec((1, tm, C_WIDTH), lambda b, i: (b, i, 0)),
        out_shape=jax.ShapeDtypeStruct((Bn, S, C_WIDTH), F32),
        compiler_params=_cparams("arbitrary", "arbitrary"),
        name="gmlp",
    )(pbc, ln_g.reshape(1, -1), ln_b.reshape(1, -1), w_s, bs_wide)


def _mid_kernel(ya_ref, yb_ref, yc_ref, x_ref, woa_ref, wob_ref, woc_ref, gpost_ref, g1_ref,
                gpre_ref, sc_ref, sh_ref, wr_ref, ws1_ref, ws3_ref, ws2_ref,
                xo_ref, h_ref, score_ref, shared_ref):
    y = (_mm(ya_ref[0], woa_ref[...]) + _mm(yb_ref[0], wob_ref[...]) + _mm(yc_ref[0], woc_ref[...]))
    xn = x_ref[0] + g1_ref[0] * (_rms(y) * gpost_ref[...])
    xo_ref[0] = xn
    h = _rms(xn) * gpre_ref[...] * (1.0 + sc_ref[0]) + sh_ref[0]
    h_ref[0] = _pack_bf16_pair(h)
    score_ref[0] = _sigmoid(_mm3(wr_ref[...], h, NT))
    hb = h.astype(BF16)
    t = _silu(jnp.dot(hb, ws1_ref[...], preferred_element_type=F32)) * jnp.dot(
        hb, ws3_ref[...], preferred_element_type=F32)
    shared_ref[0] = jnp.dot(t.astype(BF16), ws2_ref[...], preferred_element_type=F32)


def _mid(ya, yb, yc, x, woa, wob, woc, gpost, g1, gpre, sc, sh, wr, ws1, ws3, ws2, tm):
    Bn, S, D = x.shape
    NR = wr.shape[0]
    F = ws1.shape[1]
    seq = lambda w: pl.BlockSpec((1, tm, w), lambda b, i: (b, i, 0))
    full = lambda shp: pl.BlockSpec(shp, lambda b, i: (0,) * len(shp))
    per_b = pl.BlockSpec((1, 1, D), lambda b, i: (b, 0, 0))
    return pl.pallas_call(
        _mid_kernel,
        grid=(Bn, S // tm),
        in_specs=[seq(A_WIDTH), seq(B_WIDTH), seq(C_WIDTH), seq(D),
                  full((A_WIDTH, D)), full((B_WIDTH, D)), full((C_WIDTH, D)),
                  full((1, D)), per_b, full((1, D)), per_b, per_b,
                  full((NR, D)), full((D, F)), full((D, F)), full((F, D))],
        out_specs=[seq(D), seq(D // 2), pl.BlockSpec((1, NR, tm), lambda b, i: (b, 0, i)), seq(D)],
        out_shape=[jax.ShapeDtypeStruct((Bn, S, D), F32), jax.ShapeDtypeStruct((Bn, S, D // 2), jnp.uint32),
                   jax.ShapeDtypeStruct((Bn, NR, S), F32), jax.ShapeDtypeStruct((Bn, S, D), F32)],
        compiler_params=_cparams("arbitrary", "arbitrary"),
        name="mid",
    )(ya, yb, yc, x, woa, wob, woc, gpost.reshape(1, D), g1, gpre.reshape(1, D), sc, sh, wr, ws1, ws3, ws2)


def _first_argmax(vals, iota, n):
    m = jnp.max(vals, axis=0, keepdims=True)
    idx = jnp.min(jnp.where(vals == m, iota, n), axis=0, keepdims=True)
    return m, idx


def _route_kernel(sc_ref, bias_ref, e_ref, w_ref, r_ref, cnt_ref, carry, *, tm):
    @pl.when((pl.program_id(0) == 0) & (pl.program_id(1) == 0))
    def _():
        carry[...] = jnp.zeros_like(carry)

    G = EXPERTS_PER_GROUP
    s = sc_ref[0]
    biased = s + bias_ref[...]
    neg_inf = jnp.float32(-jnp.inf)
    io8 = lax.broadcasted_iota(jnp.int32, (G, tm), 0)
    gs_rows = []
    for g in range(N_GROUPS):
        blk = biased[g * G:(g + 1) * G]
        m1, i1 = _first_argmax(blk, io8, G)
        m2 = jnp.max(jnp.where(io8 == i1, neg_inf, blk), axis=0, keepdims=True)
        gs_rows.append(m1 + m2)
    gs = jnp.concatenate(gs_rows, axis=0)
    gio = lax.broadcasted_iota(jnp.int32, (N_GROUPS, tm), 0)
    gsel = jnp.zeros((N_GROUPS, tm), jnp.bool_)
    for _ in range(TOPK_GROUPS):
        _, gi = _first_argmax(gs, gio, N_GROUPS)
        pick = gio == gi
        gsel = gsel | pick
        gs = jnp.where(pick, neg_inf, gs)
    masked = jnp.concatenate(
        [jnp.where(gsel[g:g + 1], biased[g * G:(g + 1) * G], neg_inf) for g in range(N_GROUPS)], axis=0)

    eio = lax.broadcasted_iota(jnp.int32, (N_EXPERTS, tm), 0)
    picks, e_rows, s_rows = [], [], []
    for _ in range(TOP_K):
        _, ei = _first_argmax(masked, eio, N_EXPERTS)
        pick = eio == ei
        picks.append(pick)
        e_rows.append(ei)
        s_rows.append(jnp.sum(jnp.where(pick, s, 0.0), axis=0, keepdims=True))
        masked = jnp.where(pick, neg_inf, masked)
    top_s = jnp.concatenate(s_rows, axis=0)
    w_ref[...] = top_s / (jnp.sum(top_s, axis=0, keepdims=True) + 1e-20) * ROUTED_SCALE
    e_ref[...] = jnp.concatenate(e_rows, axis=0)

    sel = jnp.zeros((N_EXPERTS, tm), F32)
    for pick in picks:
        sel = sel + pick.astype(F32)
    before = (lax.broadcasted_iota(jnp.int32, (tm, tm), 0) < lax.broadcasted_iota(jnp.int32, (tm, tm), 1))
    pos = carry[...] + jnp.dot(sel.astype(BF16), before.astype(BF16), preferred_element_type=F32)
    r_ref[...] = jnp.concatenate(
        [jnp.sum(jnp.where(pick, pos, 0.0), axis=0, keepdims=True) for pick in picks], axis=0).astype(jnp.int32)
    total = carry[...] + jnp.sum(sel, axis=1, keepdims=True)
    carry[...] = total
    cnt_ref[...] = jnp.broadcast_to(total, cnt_ref.shape).astype(jnp.int32)


def _route(scores_t, e_bias, tm):
    Bn, _, S = scores_t.shape
    T = Bn * S
    nt = S // tm
    tok = pl.BlockSpec((TOP_K, tm), lambda b, i: (0, b * nt + i))
    return pl.pallas_call(
        functools.partial(_route_kernel, tm=tm),
        grid=(Bn, nt),
        in_specs=[pl.BlockSpec((1, N_EXPERTS, tm), lambda b, i: (b, 0, i)),
                  pl.BlockSpec((N_EXPERTS, 1), lambda b, i: (0, 0))],
        out_specs=[tok, tok, tok, pl.BlockSpec((N_EXPERTS, V7X_LANES), lambda b, i: (0, 0))],
        out_shape=[jax.ShapeDtypeStruct((TOP_K, T), jnp.int32), jax.ShapeDtypeStruct((TOP_K, T), F32),
                   jax.ShapeDtypeStruct((TOP_K, T), jnp.int32),
                   jax.ShapeDtypeStruct((N_EXPERTS, V7X_LANES), jnp.int32)],
        scratch_shapes=[pltpu.VMEM((N_EXPERTS, 1), F32)],
        compiler_params=_cparams("arbitrary", "arbitrary"),
        name="route",
    )(scores_t, e_bias.reshape(N_EXPERTS, 1))


def _dest_kernel(start_ref, e_ref, r_ref, o_ref):
    e = e_ref[...]
    acc = r_ref[...]
    for ex in range(N_EXPERTS):
        acc = acc + jnp.where(e == ex, start_ref[ex], 0)
    o_ref[0] = acc


def _dest_rows(pad_start, eidx, rank, tt):
    K_, T = eidx.shape
    grid_spec = pltpu.PrefetchScalarGridSpec(
        num_scalar_prefetch=1,
        grid=(T // tt,),
        in_specs=[pl.BlockSpec((K_, tt), lambda i, st: (0, i)), pl.BlockSpec((K_, tt), lambda i, st: (0, i))],
        out_specs=pl.BlockSpec((1, K_, tt), lambda i, st: (i, 0, 0)),
    )
    return pl.pallas_call(
        _dest_kernel,
        grid_spec=grid_spec,
        out_shape=jax.ShapeDtypeStruct((T // tt, K_, tt), jnp.int32),
        compiler_params=_cparams("arbitrary"),
        name="dest_rows",
    )(pad_start, eidx, rank)


def _dispatch_kernel(fill_ref, dest_hbm, h_ref, xs_hbm, dest_smem, zbuf, sem_idx, sem_rows, sem_zero):
    i = pl.program_id(0)
    EB = EXPERT_BLOCK

    def zero_copy(ex):
        return pltpu.make_async_copy(zbuf, xs_hbm.at[pl.ds(pl.multiple_of(fill_ref[ex], EB), EB), :], sem_zero)

    @pl.when(i == 0)
    def _():
        zbuf[...] = jnp.zeros_like(zbuf)

        def z_start(ex, c):
            @pl.when(fill_ref[ex] >= 0)
            def _():
                zero_copy(ex).start()
            return c

        def z_wait(ex, c):
            @pl.when(fill_ref[ex] >= 0)
            def _():
                zero_copy(ex).wait()
            return c

        lax.fori_loop(0, N_EXPERTS, z_start, 0)
        lax.fori_loop(0, N_EXPERTS, z_wait, 0)

    idx_copy = pltpu.make_async_copy(dest_hbm.at[i], dest_smem, sem_idx)
    idx_copy.start()
    idx_copy.wait()
    tt = h_ref.shape[0]

    def row_copy(t, k):
        return pltpu.make_async_copy(h_ref.at[pl.ds(t, 1), :], xs_hbm.at[pl.ds(dest_smem[k, t], 1), :], sem_rows)

    def issue(t, c):
        for k in range(TOP_K):
            row_copy(t, k).start()
        return c

    def drain(t, c):
        for k in range(TOP_K):
            row_copy(t, k).wait()
        return c

    lax.fori_loop(0, tt, issue, 0)
    lax.fori_loop(0, tt, drain, 0)


def _dispatch(fill_blocks, dest, h, n_rows):
    T, D = h.shape
    nt, K_, tt = dest.shape
    grid_spec = pltpu.PrefetchScalarGridSpec(
        num_scalar_prefetch=1,
        grid=(nt,),
        in_specs=[pl.BlockSpec(memory_space=pl.ANY), pl.BlockSpec((tt, D), lambda i, fl: (i, 0))],
        out_specs=pl.BlockSpec(memory_space=pl.ANY),
        scratch_shapes=[pltpu.SMEM((K_, tt), jnp.int32), pltpu.VMEM((EXPERT_BLOCK, D), h.dtype),
                        pltpu.SemaphoreType.DMA(()), pltpu.SemaphoreType.DMA(()), pltpu.SemaphoreType.DMA(())],
    )
    return pl.pallas_call(
        _dispatch_kernel,
        grid_spec=grid_spec,
        out_shape=jax.ShapeDtypeStruct((n_rows, D), h.dtype),
        compiler_params=_cparams("arbitrary"),
        name="dispatch",
    )(fill_blocks, dest, h)


def _expert_kernel(blk_e_ref, n_used_ref, x_ref, w1_ref, w3_ref, w2_ref, o_ref, w1b, w3b, w2b):
    i = pl.program_id(0)

    @pl.when((i == 0) | (blk_e_ref[i] != blk_e_ref[jnp.maximum(i - 1, 0)]))
    def _():
        w1b[...] = w1_ref[0].astype(BF16)
        w3b[...] = w3_ref[0].astype(BF16)
        w2b[...] = w2_ref[0].astype(BF16)

    @pl.when(i < n_used_ref[0])
    def _():
        x_lo, x_hi = _unpack_bf16_pair(x_ref[...])
        x_lo, x_hi = x_lo.astype(BF16), x_hi.astype(BF16)
        half = x_lo.shape[1]

        def up(wb):
            return (jnp.dot(x_lo, wb[:half, :], preferred_element_type=F32)
                    + jnp.dot(x_hi, wb[half:, :], preferred_element_type=F32))

        t = _silu(up(w1b)) * up(w3b)
        o_ref[...] = _pack_bf16_pair(jnp.dot(t.astype(BF16), w2b[...], preferred_element_type=F32))


def _experts(blk_e, n_used, xs, w1, w3, w2, layer):
    P, DP = xs.shape
    EB = EXPERT_BLOCK
    n_blocks = blk_e.shape[0]
    D, F = w1.shape[2], w1.shape[3]
    rows = pl.BlockSpec((EB, DP), lambda i, be, nu: (jnp.minimum(i, nu[0] - 1), 0))
    grid_spec = pltpu.PrefetchScalarGridSpec(
        num_scalar_prefetch=2,
        grid=(n_blocks,),
        in_specs=[
            rows,
            pl.BlockSpec((None, 1, D, F), lambda i, be, nu: (layer, be[i], 0, 0)),
            pl.BlockSpec((None, 1, D, F), lambda i, be, nu: (layer, be[i], 0, 0)),
            pl.BlockSpec((None, 1, F, D), lambda i, be, nu: (layer, be[i], 0, 0)),
        ],
        out_specs=rows,
        scratch_shapes=[pltpu.VMEM((D, F), BF16), pltpu.VMEM((D, F), BF16), pltpu.VMEM((F, D), BF16)],
    )
    return pl.pallas_call(
        _expert_kernel,
        grid_spec=grid_spec,
        out_shape=jax.ShapeDtypeStruct((P, DP), jnp.uint32),
        compiler_params=_cparams("arbitrary"),
        name="experts",
    )(blk_e, n_used, xs, w1, w3, w2)


def _block_layout(counts, n_blocks):
    EB = EXPERT_BLOCK
    padded = (counts + EB - 1) // EB * EB
    pad_end = jnp.cumsum(padded)
    pad_start = pad_end - padded
    blk_row = (jnp.arange(n_blocks) * EB)[:, None]
    blk_e = jnp.minimum(jnp.sum((pad_end[None, :] <= blk_row).astype(jnp.int32), axis=1), N_EXPERTS - 1)
    n_used = (pad_end[-1] // EB).astype(jnp.int32).reshape(1)
    fill = jnp.where(padded > counts, pad_end - EB, -1).astype(jnp.int32)
    return pad_start.astype(jnp.int32), blk_e, n_used, fill


def _combine_kernel(dest_hbm, ys_hbm, w_ref, x_ref, shared_ref, gpost_ref, g2_ref, o_ref,
                    dest_smem, buf, sem_idx, sem_rows, *, tt, nt):
    i = pl.program_id(0) * nt + pl.program_id(1)
    idx_copy = pltpu.make_async_copy(dest_hbm.at[i], dest_smem, sem_idx)
    idx_copy.start()
    idx_copy.wait()

    def row_copy(t, k):
        return pltpu.make_async_copy(ys_hbm.at[pl.ds(dest_smem[k, t], 1), :], buf.at[k, pl.ds(t, 1), :], sem_rows)

    def issue(t, c):
        for k in range(TOP_K):
            row_copy(t, k).start()
        return c

    def drain(t, c):
        for k in range(TOP_K):
            row_copy(t, k).wait()
        return c

    lax.fori_loop(0, tt, issue, 0)
    lax.fori_loop(0, tt, drain, 0)
    w = w_ref[...]
    y_lo = jnp.zeros((tt, buf.shape[2]), F32)
    y_hi = jnp.zeros((tt, buf.shape[2]), F32)
    for k in range(TOP_K):
        lo, hi = _unpack_bf16_pair(buf[k])
        y_lo = y_lo + w[:, k:k + 1] * lo
        y_hi = y_hi + w[:, k:k + 1] * hi
    y = shared_ref[0] + jnp.concatenate([y_lo, y_hi], axis=1)
    o_ref[0] = x_ref[0] + g2_ref[0] * (_rms(y) * gpost_ref[...])


def _combine(dest, ys, w_tok, x, shared, gpost, g2):
    Bn, S, D = x.shape
    _, K_, tt = dest.shape
    nt = S // tt
    seq = pl.BlockSpec((1, tt, D), lambda b, i: (b, i, 0))
    return pl.pallas_call(
        functools.partial(_combine_kernel, tt=tt, nt=nt),
        grid=(Bn, nt),
        in_specs=[pl.BlockSpec(memory_space=pl.ANY), pl.BlockSpec(memory_space=pl.ANY),
                  pl.BlockSpec((tt, K_), lambda b, i: (b * nt + i, 0)), seq, seq,
                  pl.BlockSpec((1, D), lambda b, i: (0, 0)), pl.BlockSpec((1, 1, D), lambda b, i: (b, 0, 0))],
        out_specs=seq,
        out_shape=jax.ShapeDtypeStruct((Bn, S, D), F32),
        scratch_shapes=[pltpu.SMEM((K_, tt), jnp.int32), pltpu.VMEM((K_, tt, ys.shape[1]), ys.dtype),
                        pltpu.SemaphoreType.DMA(()), pltpu.SemaphoreType.DMA(())],
        compiler_params=_cparams("arbitrary", "arbitrary"),
        name="combine",
    )(dest, ys, w_tok, x, shared, gpost.reshape(1, D), g2)


def kernel(x, c, w_ada, b_ada, norm_pre_mix, norm_post_mix, norm_pre_ffn, norm_post_ffn, w_in, w_out, rel_bias_table, diff_lambda, diff_subln, rwkv_mu, rwkv_w0, rwkv_w2, rwkv_a0, rwkv_a2, rwkv_g2, rwkv_k_k, rwkv_k_a, rwkv_r_k, rwkv_lnx_g, rwkv_lnx_b, gmlp_ln_g, gmlp_ln_b, gmlp_w_s, gmlp_b_s, router_w, router_bias, exp_w1, exp_w3, exp_w2, shared_w1, shared_w3, shared_w2):
    Bn, S, D = x.shape
    depth = w_ada.shape[0]
    tm = min(256, S)
    tq = min(256, S // 2)
    t_rwkv = min(512, S)

    mod = _adaln(c, w_ada, b_ada)
    band = _attn_band(rel_bias_table, tq)
    zpad = jnp.zeros((B_DECAY_LORA, B_WIDTH), F32)
    for l in range(depth):
        sh1, sc1, g1, sh2, sc2, g2 = [m.reshape(Bn, 1, D) for m in jnp.split(mod[l], 6, axis=-1)]
        w_in_b = w_in[l].astype(BF16)
        pa, pbc = _inproj(x, norm_pre_mix[l], sc1, sh1, w_in_b[:, :A_COLS], w_in_b[:, A_COLS:], tm)
        lambda_init = 0.8 - 0.6 * math.exp(-0.3 * l)
        ya = _diff_attention(pa, band, diff_lambda[l], diff_subln[l], lambda_init, tq)
        prep = _rwkv_prep(pbc, rwkv_mu[l], rwkv_w0[l], jnp.concatenate([rwkv_w2[l], zpad], axis=0),
                          rwkv_a0[l], jnp.concatenate([zpad, rwkv_a2[l]], axis=0), rwkv_g2[l],
                          rwkv_k_k[l], rwkv_k_a[l], rwkv_r_k[l].reshape(-1), t_rwkv)
        yb = _rwkv_scan(*prep, rwkv_lnx_g[l], rwkv_lnx_b[l], t_rwkv)
        yc = _gmlp(pbc, gmlp_ln_g[l], gmlp_ln_b[l], gmlp_w_s[l], gmlp_b_s[l], tm)

        w_out_b = w_out[l].astype(BF16)
        wr_t = jnp.pad(jnp.transpose(router_w[l]), ((0, V7X_LANES - N_EXPERTS), (0, 0)))
        x, h, scores_t, shared = _mid(
            ya, yb, yc, x, w_out_b[:A_WIDTH], w_out_b[A_WIDTH:A_WIDTH + B_WIDTH], w_out_b[A_WIDTH + B_WIDTH:],
            norm_post_mix[l], g1, norm_pre_ffn[l], sc2, sh2, wr_t,
            shared_w1[l].astype(BF16), shared_w3[l].astype(BF16), shared_w2[l].astype(BF16), tm)

        T = Bn * S
        n_blocks = -(-T * TOP_K // EXPERT_BLOCK) + N_EXPERTS
        eidx, wgt, rank, cnt = _route(scores_t, router_bias[l], tm)
        pad_start, blk_e, n_used, fill = _block_layout(cnt[:, 0], n_blocks)
        dest = _dest_rows(pad_start, eidx, rank, tm)
        xs = _dispatch(fill, dest, h.reshape(T, D // 2), n_blocks * EXPERT_BLOCK)
        ys = _experts(blk_e, n_used, xs, exp_w1, exp_w3, exp_w2, l)
        x = _combine(dest, ys, jnp.transpose(wgt), x, shared, norm_post_ffn[l], g2)
    return x
```

```python
import functools
import math

import jax
import jax.numpy as jnp
from jax import lax
from jax.experimental import pallas as pl
from jax.experimental.pallas import tpu as pltpu

F32 = jnp.float32
BF16 = jnp.bfloat16

A_HEADS = 4
A_QK_DIM = 64
A_HEAD_W = 2 * A_QK_DIM
A_WIDTH = A_HEADS * A_HEAD_W
N_BUCKETS = 32
MAX_DISTANCE = 128
B_HEADS = 4
B_HEAD_DIM = 64
B_WIDTH = B_HEADS * B_HEAD_DIM
B_DECAY_LORA = 64
B_AAA_LORA = 64
B_GATE_LORA = 128
B_LNX_EPS = 64e-5
C_GROUPS = 4
C_GROUP_DIM = 64
C_WIDTH = C_GROUPS * C_GROUP_DIM
CHUNK = 128
A_COLS = 3 * A_WIDTH
B_COLS = 3 * B_WIDTH + B_DECAY_LORA + B_AAA_LORA + B_GATE_LORA
C_COLS = 2 * C_WIDTH
N_EXPERTS = 64
TOP_K = 8
N_GROUPS = 8
TOPK_GROUPS = 4
EXPERTS_PER_GROUP = N_EXPERTS // N_GROUPS
ROUTED_SCALE = 2.5
EXPERT_BLOCK = 256
RMS_EPS = 1e-6
LN_EPS = 1e-5
NEG_BIG = -1e30

V7X_LANES = 128
V7X_VMEM_LIMIT_BYTES = 56 * 1024 * 1024
RWKV_CHUNK = 64
RWKV_GROUP = 8

NN = (((1,), (0,)), ((), ()))
NT = (((1,), (1,)), ((), ()))
TN = (((0,), (0,)), ((), ()))


def _cparams(*sem):
    return pltpu.CompilerParams(dimension_semantics=sem, vmem_limit_bytes=V7X_VMEM_LIMIT_BYTES)


def _mm(a, b, dims=NN):
    return lax.dot_general(a.astype(BF16), b.astype(BF16), dims, preferred_element_type=F32)


def _split(a):
    hi = a.astype(BF16)
    lo = (a - hi.astype(F32)).astype(BF16)
    return hi, lo


def _mm3(a, b, dims=NN):
    ah, al = _split(a)
    bh, bl = _split(b)
    d = lambda x, y: lax.dot_general(x, y, dims, preferred_element_type=F32)
    return d(ah, bh) + d(ah, bl) + d(al, bh)


def _mm2(a, b_exact, dims=NN):
    ah, al = _split(a)
    d = lambda x: lax.dot_general(x, b_exact, dims, preferred_element_type=F32)
    return d(ah) + d(al)


def _pack_bf16_pair(x):
    n = x.shape[1] // 2
    bits = lax.bitcast_convert_type(x.astype(BF16).astype(F32), jnp.uint32)
    return (bits[:, :n] >> 16) | bits[:, n:]


def _unpack_bf16_pair(u):
    lo = lax.bitcast_convert_type(u << 16, F32)
    hi = lax.bitcast_convert_type(u & jnp.uint32(0xFFFF0000), F32)
    return lo, hi


def _rms(x, eps=RMS_EPS):
    return x * lax.rsqrt(jnp.mean(x * x, axis=-1, keepdims=True) + eps)


def _sigmoid(x):
    return 1.0 / (1.0 + jnp.exp(-x))


def _silu(x):
    return x * _sigmoid(x)


def _adaln_kernel(c_ref, w_ref, b_ref, o_ref):
    c = c_ref[...]
    o_ref[0] = _mm3(_silu(c), w_ref[0]) + b_ref[0]


def _adaln(c, w_ada, b_ada):
    L, D, N = w_ada.shape
    Bn = c.shape[0]
    tn = min(N, 1536)
    return pl.pallas_call(
        _adaln_kernel,
        grid=(L, N // tn),
        in_specs=[
            pl.BlockSpec((Bn, D), lambda l, j: (0, 0)),
            pl.BlockSpec((1, D, tn), lambda l, j: (l, 0, j)),
            pl.BlockSpec((1, 1, tn), lambda l, j: (l, 0, j)),
        ],
        out_specs=pl.BlockSpec((1, Bn, tn), lambda l, j: (l, 0, j)),
        out_shape=jax.ShapeDtypeStruct((L, Bn, N), F32),
        compiler_params=_cparams("arbitrary", "arbitrary"),
        name="adaln",
    )(c, w_ada, b_ada.reshape(L, 1, N))


def _inproj_kernel(x_ref, g_ref, sc_ref, sh_ref, wa_ref, wbc_ref, oa_ref, obc_ref):
    x = x_ref[0]
    h = _rms(x) * g_ref[...] * (1.0 + sc_ref[0]) + sh_ref[0]
    hb = h.astype(BF16)
    oa_ref[0] = jnp.dot(hb, wa_ref[...], preferred_element_type=F32).astype(BF16)
    obc_ref[0] = jnp.dot(hb, wbc_ref[...], preferred_element_type=F32)


def _inproj(x, g, sc, sh, wa, wbc, tm):
    Bn, S, D = x.shape
    na, nbc = wa.shape[1], wbc.shape[1]
    return pl.pallas_call(
        _inproj_kernel,
        grid=(Bn, S // tm),
        in_specs=[
            pl.BlockSpec((1, tm, D), lambda b, i: (b, i, 0)),
            pl.BlockSpec((1, D), lambda b, i: (0, 0)),
            pl.BlockSpec((1, 1, D), lambda b, i: (b, 0, 0)),
            pl.BlockSpec((1, 1, D), lambda b, i: (b, 0, 0)),
            pl.BlockSpec((D, na), lambda b, i: (0, 0)),
            pl.BlockSpec((D, nbc), lambda b, i: (0, 0)),
        ],
        out_specs=[
            pl.BlockSpec((1, tm, na), lambda b, i: (b, i, 0)),
            pl.BlockSpec((1, tm, nbc), lambda b, i: (b, i, 0)),
        ],
        out_shape=[
            jax.ShapeDtypeStruct((Bn, S, na), BF16),
            jax.ShapeDtypeStruct((Bn, S, nbc), F32),
        ],
        compiler_params=_cparams("arbitrary", "arbitrary"),
        name="inproj",
    )(x, g.reshape(1, D), sc, sh, wa, wbc)


def _t5_bucket(dist):
    n = jnp.maximum(dist, 0)
    max_exact = N_BUCKETS // 2
    nf = jnp.maximum(n, 1).astype(F32)
    large = max_exact + (jnp.log(nf / max_exact) / math.log(MAX_DISTANCE / max_exact)
                         * (N_BUCKETS - max_exact)).astype(jnp.int32)
    large = jnp.minimum(large, N_BUCKETS - 1)
    return jnp.where(n < max_exact, n, large)


def _attn_band(table, tq):
    far = table[N_BUCKETS - 1].astype(F32)
    L = 3 * tq
    m = jnp.arange(L)
    m = jnp.where(m < 2 * tq, m, m - L)
    bands = []
    for off in (0, tq):
        dist = off - m
        vals = jnp.where(dist[None] >= 0, jnp.transpose(table[_t5_bucket(dist)].astype(F32)) - far[:, None],
                         NEG_BIG)
        toe = jnp.tile(vals, (1, tq))[:, :tq * (L - 1)].reshape(-1, tq, L - 1)
        bands.append(toe[:, :, :2 * tq])
    band = jnp.stack(bands)
    return jnp.concatenate([band, band], axis=2)


def _attn_kernel(lam_ref, q_ref, k_ref, v_ref, band_ref, g_ref, o_ref, *, tq, lambda_init):
    i = pl.program_id(2)
    q = q_ref[0] * jnp.asarray(A_QK_DIM ** -0.5, BF16)
    lane = lax.broadcasted_iota(jnp.int32, q.shape, 1)
    zero = jnp.zeros_like(q)
    qq = jnp.concatenate([jnp.where(lane < A_QK_DIM, q, zero),
                          jnp.where(lane >= A_QK_DIM, q, zero)], axis=0)

    kb0 = pl.multiple_of(jnp.maximum(i - 1, 0) * tq, tq)
    kb = k_ref[0, pl.ds(kb0, 2 * tq), :]
    vb = v_ref[0, pl.ds(kb0, 2 * tq), :]
    s = lax.dot_general(qq, kb, NT, preferred_element_type=F32) + band_ref[0, 0]
    m = jnp.max(s, axis=-1, keepdims=True)
    p = jnp.exp(s - m)
    l = jnp.sum(p, axis=-1, keepdims=True)
    acc = jnp.dot(p.astype(BF16), vb, preferred_element_type=F32)

    n_far = jnp.maximum(i - 1, 0)

    def logits(j):
        return lax.dot_general(qq, k_ref[0, pl.ds(pl.multiple_of(j * tq, tq), tq), :], NT,
                               preferred_element_type=F32)

    def body(j, carry):
        m, l, acc, s = carry
        s_next = logits(jnp.minimum(j + 1, n_far - 1))
        vj = v_ref[0, pl.ds(pl.multiple_of(j * tq, tq), tq), :]
        m_new = jnp.maximum(m, jnp.max(s, axis=-1, keepdims=True))
        alpha = jnp.exp(m - m_new)
        p = jnp.exp(s - m_new)
        l = alpha * l + jnp.sum(p, axis=-1, keepdims=True)
        acc = alpha * acc + jnp.dot(p.astype(BF16), vj, preferred_element_type=F32)
        return m_new, l, acc, s_next

    m, l, acc, _ = lax.fori_loop(0, n_far, body, (m, l, acc, logits(0)))

    lp = lam_ref[...]
    lam = (jnp.exp(jnp.sum(lp[0:1] * lp[1:2], axis=-1, keepdims=True))
           - jnp.exp(jnp.sum(lp[2:3] * lp[3:4], axis=-1, keepdims=True)) + lambda_init)
    o = acc / l
    o = o[:tq] - lam * o[tq:]
    o_ref[0] = _rms(o) * g_ref[...] * (1.0 - lambda_init)


def _diff_attention(pa, band, lam_par, subln_g, lambda_init, tq):
    Bn, S, _ = pa.shape
    W = A_HEAD_W
    kern = functools.partial(_attn_kernel, tq=tq, lambda_init=lambda_init)
    return pl.pallas_call(
        kern,
        grid=(Bn, A_HEADS, S // tq),
        in_specs=[
            pl.BlockSpec((4, A_QK_DIM), lambda b, h, i: (0, 0)),
            pl.BlockSpec((1, tq, W), lambda b, h, i: (b, i, h)),
            pl.BlockSpec((1, S, W), lambda b, h, i: (b, 0, A_HEADS + h)),
            pl.BlockSpec((1, S, W), lambda b, h, i: (b, 0, 2 * A_HEADS + h)),
            pl.BlockSpec((1, 1, 2 * tq, 2 * tq), lambda b, h, i: (jnp.minimum(i, 1), h, 0, 0)),
            pl.BlockSpec((1, W), lambda b, h, i: (0, 0)),
        ],
        out_specs=pl.BlockSpec((1, tq, W), lambda b, h, i: (b, i, h)),
        out_shape=jax.ShapeDtypeStruct((Bn, S, A_WIDTH), F32),
        compiler_params=_cparams("arbitrary", "arbitrary", "arbitrary"),
        name="diff_attn",
    )(lam_par, pa, pa, pa, band, subln_g.reshape(1, W))


def _head_ones(n):
    r = lax.broadcasted_iota(jnp.int32, (n, n), 0) // B_HEAD_DIM
    c = lax.broadcasted_iota(jnp.int32, (n, n), 1) // B_HEAD_DIM
    return (r == c).astype(BF16)


def _rwkv_prep_kernel(pb_ref, prev_ref, mu_ref, w0_ref, w2_ref, a0_ref, a2_ref, g2_ref,
                      kk_ref, ka_ref, rk_ref,
                      rt_ref, at_ref, kt_ref, bt_ref, v_ref, wc_ref, bonus_ref, g_ref, *, tm):
    i = pl.program_id(1)
    C = RWKV_CHUNK
    x = pb_ref[0]
    row = lax.broadcasted_iota(jnp.int32, x.shape, 0)
    last = prev_ref[0, 7:8, :] * (i > 0).astype(F32)
    prev = jnp.where(row == 0, last, pltpu.roll(x, 1, 0))
    p = x + (prev - x) * mu_ref[...]
    o1, o2, o3 = B_WIDTH, 2 * B_WIDTH, 3 * B_WIDTH
    r, k, v = p[:, :o1], p[:, o1:o2], p[:, o2:o3]
    lora = p[:, o3:o3 + B_DECAY_LORA + B_AAA_LORA]
    gd = p[:, o3 + B_DECAY_LORA + B_AAA_LORA:]

    z = -(w0_ref[...] + _mm3(jnp.tanh(lora), w2_ref[...]))
    softplus = jnp.maximum(z, 0.0) + jnp.log(1.0 + jnp.exp(-jnp.abs(z)))
    logw = -jnp.exp(-softplus - 0.5)
    a = _sigmoid(a0_ref[...] + _mm3(lora, a2_ref[...]))
    g_ref[0] = _mm3(_sigmoid(gd), g2_ref[...])

    ones = _head_ones(B_WIDTH)
    kk = k * kk_ref[...]
    kk = kk * lax.rsqrt(jnp.maximum(_mm2(kk * kk, ones), 1e-24))
    k2 = k * (1.0 + (a - 1.0) * ka_ref[...])
    bonus_ref[0] = _mm2(r * k2 * rk_ref[...], ones) * v

    t_in = lax.broadcasted_iota(jnp.int32, (tm, B_WIDTH), 0) % C
    cum = logw
    sh = 1
    while sh < C:
        cum = cum + jnp.where(t_in >= sh, pltpu.roll(cum, sh, 0), 0.0)
        sh *= 2
    n = tm // C
    wc_ref[0] = jnp.exp(jnp.sum(logw.reshape(n, C, B_WIDTH), axis=1))
    e_pos = jnp.exp(cum)
    e_neg = jnp.exp(-cum)
    rt_ref[0] = r * e_pos
    at_ref[0] = -kk * jnp.exp(cum - logw)
    kt_ref[0] = k2 * e_neg
    bt_ref[0] = kk * a * e_neg
    v_ref[0] = v


def _rwkv_prep(pbc, mu, w0, w2p, a0, a2p, g2, k_k, k_a, r_k, tm):
    Bn, S, _ = pbc.shape
    W = B_WIDTH
    nl = B_DECAY_LORA + B_AAA_LORA
    row = lambda a: a.reshape(1, -1)
    full = lambda shp: pl.BlockSpec(shp, lambda b, i: (0,) * len(shp))
    seq = pl.BlockSpec((1, tm, W), lambda b, i: (b, i, 0))
    seq_shape = jax.ShapeDtypeStruct((Bn, S, W), F32)
    n = tm // RWKV_CHUNK
    return pl.pallas_call(
        functools.partial(_rwkv_prep_kernel, tm=tm),
        grid=(Bn, S // tm),
        in_specs=[
            pl.BlockSpec((1, tm, B_COLS), lambda b, i: (b, i, 0)),
            pl.BlockSpec((1, 8, B_COLS), lambda b, i: (b, jnp.maximum(i * (tm // 8) - 1, 0), 0)),
            full((1, B_COLS)), full((1, W)), full((nl, W)), full((1, W)), full((nl, W)),
            full((B_GATE_LORA, W)), full((1, W)), full((1, W)), full((1, W)),
        ],
        out_specs=[seq, seq, seq, seq, seq,
                   pl.BlockSpec((1, n, W), lambda b, i: (b, i, 0)), seq, seq],
        out_shape=[seq_shape] * 5 + [jax.ShapeDtypeStruct((Bn, S // RWKV_CHUNK, W), F32)] + [seq_shape] * 2,
        compiler_params=_cparams("arbitrary", "arbitrary"),
        name="rwkv_prep",
    )(pbc, pbc, row(mu), row(w0), w2p, row(a0), a2p, g2, row(k_k), row(k_a), row(r_k))


def _rwkv_scan_kernel(rt_ref, at_ref, kt_ref, bt_ref, v_ref, wc_ref, bonus_ref, g_ref,
                      lng_ref, lnb_ref, o_ref, state, *, tt):
    C = RWKV_CHUNK
    W = B_WIDTH

    @pl.when(pl.program_id(1) == 0)
    def _():
        state[...] = jnp.zeros_like(state)

    lane_head = lax.broadcasted_iota(jnp.int32, (C, W), 1) // B_HEAD_DIM
    tt_i = lax.broadcasted_iota(jnp.int32, (C, W), 0)
    ss_i = lax.broadcasted_iota(jnp.int32, (C, W), 1) % C
    strict = tt_i > ss_i
    incl = tt_i >= ss_i
    eye = (tt_i == ss_i).astype(F32)
    ones = _head_ones(W)
    bd_mask = ones.astype(F32)

    head_mask = [(lane_head == h).astype(BF16) for h in range(B_HEADS)]

    def bd_split(x):
        xb = x.astype(BF16)
        return jnp.concatenate([xb * mk for mk in head_mask], axis=0)

    def mm_bd(a, b_bd, dims=NN):
        return lax.dot_general(a.astype(BF16), b_bd, dims, preferred_element_type=F32)

    def state_free(gi):
        G = range(RWKV_GROUP)
        sls = [pl.ds(pl.multiple_of((gi * RWKV_GROUP + j) * C, C), C) for j in G]
        rt = [rt_ref[0, sl, :] for sl in sls]
        at = [at_ref[0, sl, :] for sl in sls]
        kt = [kt_ref[0, sl, :] for sl in sls]
        bt = [bt_ref[0, sl, :] for sl in sls]
        v = [v_ref[0, sl, :] for sl in sls]
        wc = [wc_ref[0, pl.ds(gi * RWKV_GROUP + j, 1), :] for j in G]
        ar = [jnp.concatenate([at[j], rt[j]], axis=0) for j in G]
        bdb = [bd_split(bt[j]) for j in G]
        bdk = [bd_split(kt[j]) for j in G]
        a_b = [mm_bd(ar[j], bdb[j], NT) for j in G]
        a_k = [mm_bd(ar[j], bdk[j], NT) for j in G]
        lo = [jnp.where(strict, a_b[j][:C], 0.0) for j in G]
        a_ak = [jnp.where(strict, a_k[j][:C], 0.0) for j in G]
        a_rb = [jnp.where(incl, a_b[j][C:], 0.0) for j in G]
        a_rk = [jnp.where(incl, a_k[j][C:], 0.0) for j in G]
        pw = lo
        tinv = [eye + lo[j] for j in G]
        bdp = [bd_split(pw[j]) for j in G]
        span = 2
        while span < C:
            pw = [mm_bd(pw[j], bdp[j]) for j in G]
            bdp = [bd_split(pw[j]) for j in G]
            tinv = [tinv[j] + mm_bd(tinv[j], bdp[j]) for j in G]
            span *= 2
        bdv = [bd_split(v[j]) for j in G]
        bda = [bd_split(at[j]) for j in G]
        abar = [mm_bd(tinv[j], bda[j]) for j in G]
        akv = [bd_split(mm_bd(a_ak[j], bdv[j])) for j in G]
        u0 = [mm_bd(tinv[j], akv[j]) for j in G]
        y0 = [mm_bd(a_rk[j], bdv[j]) for j in G]
        kv = [_mm(v[j], kt[j] * wc[j], TN) * bd_mask for j in G]
        return [(jnp.concatenate([abar[j], rt[j]], axis=0), u0[j], y0[j], a_rb[j], bt[j] * wc[j], kv[j], wc[j])
                for j in G]

    def group(gi, carry):
        pre = state_free(gi)
        s = state[...]
        ys = []
        for abar_rt, u0, y0, a_rb, btw, kv, wc in pre:
            a_s = _mm(abar_rt, s, NT)
            u = a_s[:C] + u0
            ys.append(a_s[C:] + y0 + mm_bd(a_rb, bd_split(u)))
            s = s * wc + _mm(u, btw, TN) * bd_mask + kv
        state[...] = s
        y = jnp.concatenate(ys, axis=0)
        sl = pl.ds(pl.multiple_of(gi * (RWKV_GROUP * C), RWKV_GROUP * C), RWKV_GROUP * C)
        mean = _mm2(y, ones) * (1.0 / B_HEAD_DIM)
        d = y - mean
        var = _mm2(d * d, ones) * (1.0 / B_HEAD_DIM)
        yn = d * lax.rsqrt(var + B_LNX_EPS) * lng_ref[...] + lnb_ref[...]
        o_ref[0, sl, :] = (yn + bonus_ref[0, sl, :]) * g_ref[0, sl, :]
        return carry

    lax.fori_loop(0, tt // (RWKV_GROUP * C), group, 0)


def _rwkv_scan(rt, at, kt, bt, v, wc, bonus, g, lnx_g, lnx_b, tt):
    Bn, S, W = rt.shape
    n = tt // RWKV_CHUNK
    seq = pl.BlockSpec((1, tt, W), lambda b, i: (b, i, 0))
    vec = pl.BlockSpec((1, W), lambda b, i: (0, 0))
    return pl.pallas_call(
        functools.partial(_rwkv_scan_kernel, tt=tt),
        grid=(Bn, S // tt),
        in_specs=[seq, seq, seq, seq, seq, pl.BlockSpec((1, n, W), lambda b, i: (b, i, 0)), seq, seq, vec, vec],
        out_specs=seq,
        out_shape=jax.ShapeDtypeStruct((Bn, S, W), F32),
        scratch_shapes=[pltpu.VMEM((B_HEADS * B_HEAD_DIM, W), F32)],
        compiler_params=_cparams("arbitrary", "arbitrary"),
        name="rwkv_scan",
    )(rt, at, kt, bt, v, wc, bonus, g, lnx_g.reshape(1, W), lnx_b.reshape(1, W))


def _gmlp_kernel(pc_ref, lng_ref, lnb_ref, ws_ref, bs_ref, o_ref, *, tm):
    x = pc_ref[0]
    z = x * (0.5 * (1.0 + jnp.tanh(math.sqrt(2.0 / math.pi) * (x + 0.044715 * (x * x * x)))))
    u, v = z[:, :C_WIDTH], z[:, C_WIDTH:]
    mu = jnp.mean(v, axis=-1, keepdims=True)
    d = v - mu
    var = jnp.mean(d * d, axis=-1, keepdims=True)
    vn = d * lax.rsqrt(var + LN_EPS) * lng_ref[...] + lnb_ref[...]
    group = lax.broadcasted_iota(jnp.int32, (CHUNK, C_WIDTH), 1) // C_GROUP_DIM
    tril = (lax.broadcasted_iota(jnp.int32, (CHUNK, CHUNK), 0)
            >= lax.broadcasted_iota(jnp.int32, (CHUNK, CHUNK), 1))
    ws = [jnp.where(tril, ws_ref[gi], 0.0).astype(BF16) for gi in range(C_GROUPS)]
    for c in range(tm // CHUNK):
        sl = slice(c * CHUNK, (c + 1) * CHUNK)
        vc = vn[sl].astype(BF16)
        sv = bs_ref[...]
        for gi in range(C_GROUPS):
            t = jnp.dot(ws[gi], vc, preferred_element_type=F32)
            sv = sv + jnp.where(group == gi, t, 0.0)
        o_ref[0, sl, :] = u[sl] * sv


def _gmlp(pbc, ln_g, ln_b, w_s, b_s, tm):
    Bn, S, _ = pbc.shape
    bs_wide = jnp.repeat(jnp.transpose(b_s), C_GROUP_DIM, axis=1)
    return pl.pallas_call(
        functools.partial(_gmlp_kernel, tm=tm),
        grid=(Bn, S // tm),
        in_specs=[
            pl.BlockSpec((1, tm, C_COLS), lambda b, i: (b, i, B_COLS // C_COLS)),
            pl.BlockSpec((1, C_WIDTH), lambda b, i: (0, 0)),
            pl.BlockSpec((1, C_WIDTH), lambda b, i: (0, 0)),
            pl.BlockSpec((C_GROUPS, CHUNK, CHUNK), lambda b, i: (0, 0, 0)),
            pl.BlockSpec((CHUNK, C_WIDTH), lambda b, i: (0, 0)),
        ],
        out_specs=pl.BlockSpec((1, tm, C_WIDTH), lambda b, i: (b, i, 0)),
        out_shape=jax.ShapeDtypeStruct((Bn, S, C_WIDTH), F32),
        compiler_params=_cparams("arbitrary", "arbitrary"),
        name="gmlp",
    )(pbc, ln_g.reshape(1, -1), ln_b.reshape(1, -1), w_s, bs_wide)


def _mid_kernel(ya_ref, yb_ref, yc_ref, x_ref, woa_ref, wob_ref, woc_ref, gpost_ref, g1_ref,
                gpre_ref, sc_ref, sh_ref, wr_ref, ws1_ref, ws3_ref, ws2_ref,
                xo_ref, h_ref, score_ref, shared_ref):
    y = (_mm(ya_ref[0], woa_ref[...]) + _mm(yb_ref[0], wob_ref[...]) + _mm(yc_ref[0], woc_ref[...]))
    xn = x_ref[0] + g1_ref[0] * (_rms(y) * gpost_ref[...])
    xo_ref[0] = xn
    h = _rms(xn) * gpre_ref[...] * (1.0 + sc_ref[0]) + sh_ref[0]
    h_ref[0] = _pack_bf16_pair(h)
    score_ref[0] = _sigmoid(_mm3(wr_ref[...], h, NT))
    hb = h.astype(BF16)
    t = _silu(jnp.dot(hb, ws1_ref[...], preferred_element_type=F32)) * jnp.dot(
        hb, ws3_ref[...], preferred_element_type=F32)
    shared_ref[0] = jnp.dot(t.astype(BF16), ws2_ref[...], preferred_element_type=F32)


def _mid(ya, yb, yc, x, woa, wob, woc, gpost, g1, gpre, sc, sh, wr, ws1, ws3, ws2, tm):
    Bn, S, D = x.shape
    NR = wr.shape[0]
    F = ws1.shape[1]
    seq = lambda w: pl.BlockSpec((1, tm, w), lambda b, i: (b, i, 0))
    full = lambda shp: pl.BlockSpec(shp, lambda b, i: (0,) * len(shp))
    per_b = pl.BlockSpec((1, 1, D), lambda b, i: (b, 0, 0))
    return pl.pallas_call(
        _mid_kernel,
        grid=(Bn, S // tm),
        in_specs=[seq(A_WIDTH), seq(B_WIDTH), seq(C_WIDTH), seq(D),
                  full((A_WIDTH, D)), full((B_WIDTH, D)), full((C_WIDTH, D)),
                  full((1, D)), per_b, full((1, D)), per_b, per_b,
                  full((NR, D)), full((D, F)), full((D, F)), full((F, D))],
        out_specs=[seq(D), seq(D // 2), pl.BlockSpec((1, NR, tm), lambda b, i: (b, 0, i)), seq(D)],
        out_shape=[jax.ShapeDtypeStruct((Bn, S, D), F32), jax.ShapeDtypeStruct((Bn, S, D // 2), jnp.uint32),
                   jax.ShapeDtypeStruct((Bn, NR, S), F32), jax.ShapeDtypeStruct((Bn, S, D), F32)],
        compiler_params=_cparams("arbitrary", "arbitrary"),
        name="mid",
    )(ya, yb, yc, x, woa, wob, woc, gpost.reshape(1, D), g1, gpre.reshape(1, D), sc, sh, wr, ws1, ws3, ws2)


def _first_argmax(vals, iota, n):
    m = jnp.max(vals, axis=0, keepdims=True)
    idx = jnp.min(jnp.where(vals == m, iota, n), axis=0, keepdims=True)
    return m, idx


def _route_kernel(sc_ref, bias_ref, e_ref, w_ref, r_ref, cnt_ref, carry, *, tm):
    @pl.when((pl.program_id(0) == 0) & (pl.program_id(1) == 0))
    def _():
        carry[...] = jnp.zeros_like(carry)

    G = EXPERTS_PER_GROUP
    s = sc_ref[0]
    biased = s + bias_ref[...]
    neg_inf = jnp.float32(-jnp.inf)
    io8 = lax.broadcasted_iota(jnp.int32, (G, tm), 0)
    gs_rows = []
    for g in range(N_GROUPS):
        blk = biased[g * G:(g + 1) * G]
        m1, i1 = _first_argmax(blk, io8, G)
        m2 = jnp.max(jnp.where(io8 == i1, neg_inf, blk), axis=0, keepdims=True)
        gs_rows.append(m1 + m2)
    gs = jnp.concatenate(gs_rows, axis=0)
    gio = lax.broadcasted_iota(jnp.int32, (N_GROUPS, tm), 0)
    gsel = jnp.zeros((N_GROUPS, tm), jnp.bool_)
    for _ in range(TOPK_GROUPS):
        _, gi = _first_argmax(gs, gio, N_GROUPS)
        pick = gio == gi
        gsel = gsel | pick
        gs = jnp.where(pick, neg_inf, gs)
    masked = jnp.concatenate(
        [jnp.where(gsel[g:g + 1], biased[g * G:(g + 1) * G], neg_inf) for g in range(N_GROUPS)], axis=0)

    eio = lax.broadcasted_iota(jnp.int32, (N_EXPERTS, tm), 0)
    picks, e_rows, s_rows = [], [], []
    for _ in range(TOP_K):
        _, ei = _first_argmax(masked, eio, N_EXPERTS)
        pick = eio == ei
        picks.append(pick)
        e_rows.append(ei)
        s_rows.append(jnp.sum(jnp.where(pick, s, 0.0), axis=0, keepdims=True))
        masked = jnp.where(pick, neg_inf, masked)
    top_s = jnp.concatenate(s_rows, axis=0)
    w_ref[...] = top_s / (jnp.sum(top_s, axis=0, keepdims=True) + 1e-20) * ROUTED_SCALE
    e_ref[...] = jnp.concatenate(e_rows, axis=0)

    sel = jnp.zeros((N_EXPERTS, tm), F32)
    for pick in picks:
        sel = sel + pick.astype(F32)
    before = (lax.broadcasted_iota(jnp.int32, (tm, tm), 0) < lax.broadcasted_iota(jnp.int32, (tm, tm), 1))
    pos = carry[...] + jnp.dot(sel.astype(BF16), before.astype(BF16), preferred_element_type=F32)
    r_ref[...] = jnp.concatenate(
        [jnp.sum(jnp.where(pick, pos, 0.0), axis=0, keepdims=True) for pick in picks], axis=0).astype(jnp.int32)
    total = carry[...] + jnp.sum(sel, axis=1, keepdims=True)
    carry[...] = total
    cnt_ref[...] = jnp.broadcast_to(total, cnt_ref.shape).astype(jnp.int32)


def _route(scores_t, e_bias, tm):
    Bn, _, S = scores_t.shape
    T = Bn * S
    nt = S // tm
    tok = pl.BlockSpec((TOP_K, tm), lambda b, i: (0, b * nt + i))
    return pl.pallas_call(
        functools.partial(_route_kernel, tm=tm),
        grid=(Bn, nt),
        in_specs=[pl.BlockSpec((1, N_EXPERTS, tm), lambda b, i: (b, 0, i)),
                  pl.BlockSpec((N_EXPERTS, 1), lambda b, i: (0, 0))],
        out_specs=[tok, tok, tok, pl.BlockSpec((N_EXPERTS, V7X_LANES), lambda b, i: (0, 0))],
        out_shape=[jax.ShapeDtypeStruct((TOP_K, T), jnp.int32), jax.ShapeDtypeStruct((TOP_K, T), F32),
                   jax.ShapeDtypeStruct((TOP_K, T), jnp.int32),
                   jax.ShapeDtypeStruct((N_EXPERTS, V7X_LANES), jnp.int32)],
        scratch_shapes=[pltpu.VMEM((N_EXPERTS, 1), F32)],
        compiler_params=_cparams("arbitrary", "arbitrary"),
        name="route",
    )(scores_t, e_bias.reshape(N_EXPERTS, 1))


def _dest_kernel(start_ref, e_ref, r_ref, o_ref):
    e = e_ref[...]
    acc = r_ref[...]
    for ex in range(N_EXPERTS):
        acc = acc + jnp.where(e == ex, start_ref[ex], 0)
    o_ref[0] = acc


def _dest_rows(pad_start, eidx, rank, tt):
    K_, T = eidx.shape
    grid_spec = pltpu.PrefetchScalarGridSpec(
        num_scalar_prefetch=1,
        grid=(T // tt,),
        in_specs=[pl.BlockSpec((K_, tt), lambda i, st: (0, i)), pl.BlockSpec((K_, tt), lambda i, st: (0, i))],
        out_specs=pl.BlockSpec((1, K_, tt), lambda i, st: (i, 0, 0)),
    )
    return pl.pallas_call(
        _dest_kernel,
        grid_spec=grid_spec,
        out_shape=jax.ShapeDtypeStruct((T // tt, K_, tt), jnp.int32),
        compiler_params=_cparams("arbitrary"),
        name="dest_rows",
    )(pad_start, eidx, rank)


def _dispatch_kernel(fill_ref, dest_hbm, h_ref, xs_hbm, dest_smem, zbuf, sem_idx, sem_rows, sem_zero):
    i = pl.program_id(0)
    EB = EXPERT_BLOCK

    def zero_copy(ex):
        return pltpu.make_async_copy(zbuf, xs_hbm.at[pl.ds(pl.multiple_of(fill_ref[ex], EB), EB), :], sem_zero)

    @pl.when(i == 0)
    def _():
        zbuf[...] = jnp.zeros_like(zbuf)

        def z_start(ex, c):
            @pl.when(fill_ref[ex] >= 0)
            def _():
                zero_copy(ex).start()
            return c

        def z_wait(ex, c):
            @pl.when(fill_ref[ex] >= 0)
            def _():
                zero_copy(ex).wait()
            return c

        lax.fori_loop(0, N_EXPERTS, z_start, 0)
        lax.fori_loop(0, N_EXPERTS, z_wait, 0)

    idx_copy = pltpu.make_async_copy(dest_hbm.at[i], dest_smem, sem_idx)
    idx_copy.start()
    idx_copy.wait()
    tt = h_ref.shape[0]

    def row_copy(t, k):
        return pltpu.make_async_copy(h_ref.at[pl.ds(t, 1), :], xs_hbm.at[pl.ds(dest_smem[k, t], 1), :], sem_rows)

    def issue(t, c):
        for k in range(TOP_K):
            row_copy(t, k).start()
        return c

    def drain(t, c):
        for k in range(TOP_K):
            row_copy(t, k).wait()
        return c

    lax.fori_loop(0, tt, issue, 0)
    lax.fori_loop(0, tt, drain, 0)


def _dispatch(fill_blocks, dest, h, n_rows):
    T, D = h.shape
    nt, K_, tt = dest.shape
    grid_spec = pltpu.PrefetchScalarGridSpec(
        num_scalar_prefetch=1,
        grid=(nt,),
        in_specs=[pl.BlockSpec(memory_space=pl.ANY), pl.BlockSpec((tt, D), lambda i, fl: (i, 0))],
        out_specs=pl.BlockSpec(memory_space=pl.ANY),
        scratch_shapes=[pltpu.SMEM((K_, tt), jnp.int32), pltpu.VMEM((EXPERT_BLOCK, D), h.dtype),
                        pltpu.SemaphoreType.DMA(()), pltpu.SemaphoreType.DMA(()), pltpu.SemaphoreType.DMA(())],
    )
    return pl.pallas_call(
        _dispatch_kernel,
        grid_spec=grid_spec,
        out_shape=jax.ShapeDtypeStruct((n_rows, D), h.dtype),
        compiler_params=_cparams("arbitrary"),
        name="dispatch",
    )(fill_blocks, dest, h)


def _expert_kernel(blk_e_ref, n_used_ref, x_ref, w1_ref, w3_ref, w2_ref, o_ref, w1b, w3b, w2b):
    i = pl.program_id(0)

    @pl.when((i == 0) | (blk_e_ref[i] != blk_e_ref[jnp.maximum(i - 1, 0)]))
    def _():
        w1b[...] = w1_ref[0].astype(BF16)
        w3b[...] = w3_ref[0].astype(BF16)
        w2b[...] = w2_ref[0].astype(BF16)

    @pl.when(i < n_used_ref[0])
    def _():
        x_lo, x_hi = _unpack_bf16_pair(x_ref[...])
        x_lo, x_hi = x_lo.astype(BF16), x_hi.astype(BF16)
        half = x_lo.shape[1]

        def up(wb):
            return (jnp.dot(x_lo, wb[:half, :], preferred_element_type=F32)
                    + jnp.dot(x_hi, wb[half:, :], preferred_element_type=F32))

        t = _silu(up(w1b)) * up(w3b)
        o_ref[...] = _pack_bf16_pair(jnp.dot(t.astype(BF16), w2b[...], preferred_element_type=F32))


def _experts(blk_e, n_used, xs, w1, w3, w2, layer):
    P, DP = xs.shape
    EB = EXPERT_BLOCK
    n_blocks = blk_e.shape[0]
    D, F = w1.shape[2], w1.shape[3]
    rows = pl.BlockSpec((EB, DP), lambda i, be, nu: (jnp.minimum(i, nu[0] - 1), 0))
    grid_spec = pltpu.PrefetchScalarGridSpec(
        num_scalar_prefetch=2,
        grid=(n_blocks,),
        in_specs=[
            rows,
            pl.BlockSpec((None, 1, D, F), lambda i, be, nu: (layer, be[i], 0, 0)),
            pl.BlockSpec((None, 1, D, F), lambda i, be, nu: (layer, be[i], 0, 0)),
            pl.BlockSpec((None, 1, F, D), lambda i, be, nu: (layer, be[i], 0, 0)),
        ],
        out_specs=rows,
        scratch_shapes=[pltpu.VMEM((D, F), BF16), pltpu.VMEM((D, F), BF16), pltpu.VMEM((F, D), BF16)],
    )
    return pl.pallas_call(
        _expert_kernel,
        grid_spec=grid_spec,
        out_shape=jax.ShapeDtypeStruct((P, DP), jnp.uint32),
        compiler_params=_cparams("arbitrary"),
        name="experts",
    )(blk_e, n_used, xs, w1, w3, w2)


def _block_layout(counts, n_blocks):
    EB = EXPERT_BLOCK
    padded = (counts + EB - 1) // EB * EB
    pad_end = jnp.cumsum(padded)
    pad_start = pad_end - padded
    blk_row = (jnp.arange(n_blocks) * EB)[:, None]
    blk_e = jnp.minimum(jnp.sum((pad_end[None, :] <= blk_row).astype(jnp.int32), axis=1), N_EXPERTS - 1)
    n_used = (pad_end[-1] // EB).astype(jnp.int32).reshape(1)
    fill = jnp.where(padded > counts, pad_end - EB, -1).astype(jnp.int32)
    return pad_start.astype(jnp.int32), blk_e, n_used, fill


def _combine_kernel(dest_hbm, ys_hbm, w_ref, x_ref, shared_ref, gpost_ref, g2_ref, o_ref,
                    dest_smem, buf, sem_idx, sem_rows, *, tt, nt):
    i = pl.program_id(0) * nt + pl.program_id(1)
    idx_copy = pltpu.make_async_copy(dest_hbm.at[i], dest_smem, sem_idx)
    idx_copy.start()
    idx_copy.wait()

    def row_copy(t, k):
        return pltpu.make_async_copy(ys_hbm.at[pl.ds(dest_smem[k, t], 1), :], buf.at[k, pl.ds(t, 1), :], sem_rows)

    def issue(t, c):
        for k in range(TOP_K):
            row_copy(t, k).start()
        return c

    def drain(t, c):
        for k in range(TOP_K):
            row_copy(t, k).wait()
        return c

    lax.fori_loop(0, tt, issue, 0)
    lax.fori_loop(0, tt, drain, 0)
    w = w_ref[...]
    y_lo = jnp.zeros((tt, buf.shape[2]), F32)
    y_hi = jnp.zeros((tt, buf.shape[2]), F32)
    for k in range(TOP_K):
        lo, hi = _unpack_bf16_pair(buf[k])
        y_lo = y_lo + w[:, k:k + 1] * lo
        y_hi = y_hi + w[:, k:k + 1] * hi
    y = shared_ref[0] + jnp.concatenate([y_lo, y_hi], axis=1)
    o_ref[0] = x_ref[0] + g2_ref[0] * (_rms(y) * gpost_ref[...])


def _combine(dest, ys, w_tok, x, shared, gpost, g2):
    Bn, S, D = x.shape
    _, K_, tt = dest.shape
    nt = S // tt
    seq = pl.BlockSpec((1, tt, D), lambda b, i: (b, i, 0))
    return pl.pallas_call(
        functools.partial(_combine_kernel, tt=tt, nt=nt),
        grid=(Bn, nt),
        in_specs=[pl.BlockSpec(memory_space=pl.ANY), pl.BlockSpec(memory_space=pl.ANY),
                  pl.BlockSpec((tt, K_), lambda b, i: (b * nt + i, 0)), seq, seq,
                  pl.BlockSpec((1, D), lambda b, i: (0, 0)), pl.BlockSpec((1, 1, D), lambda b, i: (b, 0, 0))],
        out_specs=seq,
        out_shape=jax.ShapeDtypeStruct((Bn, S, D), F32),
        scratch_shapes=[pltpu.SMEM((K_, tt), jnp.int32), pltpu.VMEM((K_, tt, ys.shape[1]), ys.dtype),
                        pltpu.SemaphoreType.DMA(()), pltpu.SemaphoreType.DMA(())],
        compiler_params=_cparams("arbitrary", "arbitrary"),
        name="combine",
    )(dest, ys, w_tok, x, shared, gpost.reshape(1, D), g2)


def kernel(x, c, w_ada, b_ada, norm_pre_mix, norm_post_mix, norm_pre_ffn, norm_post_ffn, w_in, w_out, rel_bias_table, diff_lambda, diff_subln, rwkv_mu, rwkv_w0, rwkv_w2, rwkv_a0, rwkv_a2, rwkv_g2, rwkv_k_k, rwkv_k_a, rwkv_r_k, rwkv_lnx_g, rwkv_lnx_b, gmlp_ln_g, gmlp_ln_b, gmlp_w_s, gmlp_b_s, router_w, router_bias, exp_w1, exp_w3, exp_w2, shared_w1, shared_w3, shared_w2):
    Bn, S, D = x.shape
    depth = w_ada.shape[0]
    tm = min(256, S)
    tq = min(256, S // 2)
    t_rwkv = min(512, S)

    mod = _adaln(c, w_ada, b_ada)
    band = _attn_band(rel_bias_table, tq)
    zpad = jnp.zeros((B_DECAY_LORA, B_WIDTH), F32)
    for l in range(depth):
        sh1, sc1, g1, sh2, sc2, g2 = [m.reshape(Bn, 1, D) for m in jnp.split(mod[l], 6, axis=-1)]
        w_in_b = w_in[l].astype(BF16)
        pa, pbc = _inproj(x, norm_pre_mix[l], sc1, sh1, w_in_b[:, :A_COLS], w_in_b[:, A_COLS:], tm)
        lambda_init = 0.8 - 0.6 * math.exp(-0.3 * l)
        ya = _diff_attention(pa, band, diff_lambda[l], diff_subln[l], lambda_init, tq)
        prep = _rwkv_prep(pbc, rwkv_mu[l], rwkv_w0[l], jnp.concatenate([rwkv_w2[l], zpad], axis=0),
                          rwkv_a0[l], jnp.concatenate([zpad, rwkv_a2[l]], axis=0), rwkv_g2[l],
                          rwkv_k_k[l], rwkv_k_a[l], rwkv_r_k[l].reshape(-1), t_rwkv)
        yb = _rwkv_scan(*prep, rwkv_lnx_g[l], rwkv_lnx_b[l], t_rwkv)
        yc = _gmlp(pbc, gmlp_ln_g[l], gmlp_ln_b[l], gmlp_w_s[l], gmlp_b_s[l], tm)

        w_out_b = w_out[l].astype(BF16)
        wr_t = jnp.pad(jnp.transpose(router_w[l]), ((0, V7X_LANES - N_EXPERTS), (0, 0)))
        x, h, scores_t, shared = _mid(
            ya, yb, yc, x, w_out_b[:A_WIDTH], w_out_b[A_WIDTH:A_WIDTH + B_WIDTH], w_out_b[A_WIDTH + B_WIDTH:],
            norm_post_mix[l], g1, norm_pre_ffn[l], sc2, sh2, wr_t,
            shared_w1[l].astype(BF16), shared_w3[l].astype(BF16), shared_w2[l].astype(BF16), tm)

        T = Bn * S
        n_blocks = -(-T * TOP_K // EXPERT_BLOCK) + N_EXPERTS
        eidx, wgt, rank, cnt = _route(scores_t, router_bias[l], tm)
        pad_start, blk_e, n_used, fill = _block_layout(cnt[:, 0], n_blocks)
        dest = _dest_rows(pad_start, eidx, rank, tm)
        xs = _dispatch(fill, dest, h.reshape(T, D // 2), n_blocks * EXPERT_BLOCK)
        ys = _experts(blk_e, n_used, xs, exp_w1, exp_w3, exp_w2, l)
        x = _combine(dest, ys, jnp.transpose(wgt), x, shared, norm_post_ffn[l], g2)
    return x
```

```python
import functools
import math

import jax
import jax.numpy as jnp
from jax import lax
from jax.experimental import pallas as pl
from jax.experimental.pallas import tpu as pltpu
from jax.experimental.pallas import tpu_sc as plsc

F32 = jnp.float32
BF16 = jnp.bfloat16

A_HEADS = 4
A_QK_DIM = 64
A_HEAD_W = 2 * A_QK_DIM
A_WIDTH = A_HEADS * A_HEAD_W
N_BUCKETS = 32
MAX_DISTANCE = 128
B_HEADS = 4
B_HEAD_DIM = 64
B_WIDTH = B_HEADS * B_HEAD_DIM
B_DECAY_LORA = 64
B_AAA_LORA = 64
B_GATE_LORA = 128
B_LNX_EPS = 64e-5
C_GROUPS = 4
C_GROUP_DIM = 64
C_WIDTH = C_GROUPS * C_GROUP_DIM
CHUNK = 128
A_COLS = 3 * A_WIDTH
B_COLS = 3 * B_WIDTH + B_DECAY_LORA + B_AAA_LORA + B_GATE_LORA
C_COLS = 2 * C_WIDTH
N_EXPERTS = 64
TOP_K = 8
N_GROUPS = 8
TOPK_GROUPS = 4
EXPERTS_PER_GROUP = N_EXPERTS // N_GROUPS
ROUTED_SCALE = 2.5
EXPERT_BLOCK = 256
RMS_EPS = 1e-6
LN_EPS = 1e-5
NEG_BIG = -1e30

V7X_LANES = 128
V7X_VMEM_LIMIT_BYTES = 56 * 1024 * 1024
RWKV_CHUNK = 64
RWKV_GROUP = 8

NN = (((1,), (0,)), ((), ()))
NT = (((1,), (1,)), ((), ()))
TN = (((0,), (0,)), ((), ()))


def _cparams(*sem):
    return pltpu.CompilerParams(dimension_semantics=sem, vmem_limit_bytes=V7X_VMEM_LIMIT_BYTES)


def _mm(a, b, dims=NN):
    return lax.dot_general(a.astype(BF16), b.astype(BF16), dims, preferred_element_type=F32)


def _split(a):
    hi = a.astype(BF16)
    lo = (a - hi.astype(F32)).astype(BF16)
    return hi, lo


def _mm3(a, b, dims=NN):
    ah, al = _split(a)
    bh, bl = _split(b)
    d = lambda x, y: lax.dot_general(x, y, dims, preferred_element_type=F32)
    return d(ah, bh) + d(ah, bl) + d(al, bh)


def _mm2(a, b_exact, dims=NN):
    ah, al = _split(a)
    d = lambda x: lax.dot_general(x, b_exact, dims, preferred_element_type=F32)
    return d(ah) + d(al)


def _pack_bf16_pair(x):
    n = x.shape[1] // 2
    bits = lax.bitcast_convert_type(x.astype(BF16).astype(F32), jnp.uint32)
    return (bits[:, :n] >> 16) | bits[:, n:]


def _unpack_bf16_pair(u):
    lo = lax.bitcast_convert_type(u << 16, F32)
    hi = lax.bitcast_convert_type(u & jnp.uint32(0xFFFF0000), F32)
    return lo, hi


def _rms(x, eps=RMS_EPS):
    return x * lax.rsqrt(jnp.mean(x * x, axis=-1, keepdims=True) + eps)


def _sigmoid(x):
    return 1.0 / (1.0 + jnp.exp(-x))


def _silu(x):
    return x * _sigmoid(x)


def _adaln_kernel(c_ref, w_ref, b_ref, o_ref):
    c = c_ref[...]
    o_ref[0] = _mm3(_silu(c), w_ref[0]) + b_ref[0]


def _adaln(c, w_ada, b_ada):
    L, D, N = w_ada.shape
    Bn = c.shape[0]
    tn = min(N, 1536)
    return pl.pallas_call(
        _adaln_kernel,
        grid=(L, N // tn),
        in_specs=[
            pl.BlockSpec((Bn, D), lambda l, j: (0, 0)),
            pl.BlockSpec((1, D, tn), lambda l, j: (l, 0, j)),
            pl.BlockSpec((1, 1, tn), lambda l, j: (l, 0, j)),
        ],
        out_specs=pl.BlockSpec((1, Bn, tn), lambda l, j: (l, 0, j)),
        out_shape=jax.ShapeDtypeStruct((L, Bn, N), F32),
        compiler_params=_cparams("arbitrary", "arbitrary"),
        name="adaln",
    )(c, w_ada, b_ada.reshape(L, 1, N))


def _inproj_kernel(x_ref, g_ref, sc_ref, sh_ref, wa_ref, wbc_ref, oa_ref, obc_ref):
    x = x_ref[0]
    h = _rms(x) * g_ref[...] * (1.0 + sc_ref[0]) + sh_ref[0]
    hb = h.astype(BF16)
    oa_ref[0] = jnp.dot(hb, wa_ref[...], preferred_element_type=F32).astype(BF16)
    obc_ref[0] = jnp.dot(hb, wbc_ref[...], preferred_element_type=F32)


def _inproj(x, g, sc, sh, wa, wbc, tm):
    Bn, S, D = x.shape
    na, nbc = wa.shape[1], wbc.shape[1]
    return pl.pallas_call(
        _inproj_kernel,
        grid=(Bn, S // tm),
        in_specs=[
            pl.BlockSpec((1, tm, D), lambda b, i: (b, i, 0)),
            pl.BlockSpec((1, D), lambda b, i: (0, 0)),
            pl.BlockSpec((1, 1, D), lambda b, i: (b, 0, 0)),
            pl.BlockSpec((1, 1, D), lambda b, i: (b, 0, 0)),
            pl.BlockSpec((D, na), lambda b, i: (0, 0)),
            pl.BlockSpec((D, nbc), lambda b, i: (0, 0)),
        ],
        out_specs=[
            pl.BlockSpec((1, tm, na), lambda b, i: (b, i, 0)),
            pl.BlockSpec((1, tm, nbc), lambda b, i: (b, i, 0)),
        ],
        out_shape=[
            jax.ShapeDtypeStruct((Bn, S, na), BF16),
            jax.ShapeDtypeStruct((Bn, S, nbc), F32),
        ],
        compiler_params=_cparams("arbitrary", "arbitrary"),
        name="inproj",
    )(x, g.reshape(1, D), sc, sh, wa, wbc)


def _t5_bucket(dist):
    n = jnp.maximum(dist, 0)
    max_exact = N_BUCKETS // 2
    nf = jnp.maximum(n, 1).astype(F32)
    large = max_exact + (jnp.log(nf / max_exact) / math.log(MAX_DISTANCE / max_exact)
                         * (N_BUCKETS - max_exact)).astype(jnp.int32)
    large = jnp.minimum(large, N_BUCKETS - 1)
    return jnp.where(n < max_exact, n, large)


def _attn_band(table, tq):
    far = table[N_BUCKETS - 1].astype(F32)
    L = 3 * tq
    m = jnp.arange(L)
    m = jnp.where(m < 2 * tq, m, m - L)
    bands = []
    for off in (0, tq):
        dist = off - m
        vals = jnp.where(dist[None] >= 0, jnp.transpose(table[_t5_bucket(dist)].astype(F32)) - far[:, None],
                         NEG_BIG)
        toe = jnp.tile(vals, (1, tq))[:, :tq * (L - 1)].reshape(-1, tq, L - 1)
        bands.append(toe[:, :, :2 * tq])
    band = jnp.stack(bands)
    return jnp.concatenate([band, band], axis=2)


def _attn_kernel(lam_ref, q_ref, k_ref, v_ref, band_ref, g_ref, o_ref, *, tq, lambda_init):
    i = pl.program_id(2)
    q = q_ref[0] * jnp.asarray(A_QK_DIM ** -0.5, BF16)
    lane = lax.broadcasted_iota(jnp.int32, q.shape, 1)
    zero = jnp.zeros_like(q)
    qq = jnp.concatenate([jnp.where(lane < A_QK_DIM, q, zero),
                          jnp.where(lane >= A_QK_DIM, q, zero)], axis=0)

    kb0 = pl.multiple_of(jnp.maximum(i - 1, 0) * tq, tq)
    kb = k_ref[0, pl.ds(kb0, 2 * tq), :]
    vb = v_ref[0, pl.ds(kb0, 2 * tq), :]
    s = lax.dot_general(qq, kb, NT, preferred_element_type=F32) + band_ref[0, 0]
    m = jnp.max(s, axis=-1, keepdims=True)
    p = jnp.exp(s - m)
    l = jnp.sum(p, axis=-1, keepdims=True)
    acc = jnp.dot(p.astype(BF16), vb, preferred_element_type=F32)

    n_far = jnp.maximum(i - 1, 0)

    def logits(j):
        return lax.dot_general(qq, k_ref[0, pl.ds(pl.multiple_of(j * tq, tq), tq), :], NT,
                               preferred_element_type=F32)

    def body(j, carry):
        m, l, acc, s = carry
        s_next = logits(jnp.minimum(j + 1, n_far - 1))
        vj = v_ref[0, pl.ds(pl.multiple_of(j * tq, tq), tq), :]
        m_new = jnp.maximum(m, jnp.max(s, axis=-1, keepdims=True))
        alpha = jnp.exp(m - m_new)
        p = jnp.exp(s - m_new)
        l = alpha * l + jnp.sum(p, axis=-1, keepdims=True)
        acc = alpha * acc + jnp.dot(p.astype(BF16), vj, preferred_element_type=F32)
        return m_new, l, acc, s_next

    m, l, acc, _ = lax.fori_loop(0, n_far, body, (m, l, acc, logits(0)))

    lp = lam_ref[...]
    lam = (jnp.exp(jnp.sum(lp[0:1] * lp[1:2], axis=-1, keepdims=True))
           - jnp.exp(jnp.sum(lp[2:3] * lp[3:4], axis=-1, keepdims=True)) + lambda_init)
    o = acc / l
    o = o[:tq] - lam * o[tq:]
    o_ref[0] = _rms(o) * g_ref[...] * (1.0 - lambda_init)


def _diff_attention(pa, band, lam_par, subln_g, lambda_init, tq):
    Bn, S, _ = pa.shape
    W = A_HEAD_W
    kern = functools.partial(_attn_kernel, tq=tq, lambda_init=lambda_init)
    return pl.pallas_call(
        kern,
        grid=(Bn, A_HEADS, S // tq),
        in_specs=[
            pl.BlockSpec((4, A_QK_DIM), lambda b, h, i: (0, 0)),
            pl.BlockSpec((1, tq, W), lambda b, h, i: (b, i, h)),
            pl.BlockSpec((1, S, W), lambda b, h, i: (b, 0, A_HEADS + h)),
            pl.BlockSpec((1, S, W), lambda b, h, i: (b, 0, 2 * A_HEADS + h)),
            pl.BlockSpec((1, 1, 2 * tq, 2 * tq), lambda b, h, i: (jnp.minimum(i, 1), h, 0, 0)),
            pl.BlockSpec((1, W), lambda b, h, i: (0, 0)),
        ],
        out_specs=pl.BlockSpec((1, tq, W), lambda b, h, i: (b, i, h)),
        out_shape=jax.ShapeDtypeStruct((Bn, S, A_WIDTH), F32),
        compiler_params=_cparams("arbitrary", "arbitrary", "arbitrary"),
        name="diff_attn",
    )(lam_par, pa, pa, pa, band, subln_g.reshape(1, W))


def _head_ones(n):
    r = lax.broadcasted_iota(jnp.int32, (n, n), 0) // B_HEAD_DIM
    c = lax.broadcasted_iota(jnp.int32, (n, n), 1) // B_HEAD_DIM
    return (r == c).astype(BF16)


def _rwkv_prep_kernel(pb_ref, prev_ref, mu_ref, w0_ref, w2_ref, a0_ref, a2_ref, g2_ref,
                      kk_ref, ka_ref, rk_ref,
                      rt_ref, at_ref, kt_ref, bt_ref, v_ref, wc_ref, bonus_ref, g_ref, *, tm):
    i = pl.program_id(1)
    C = RWKV_CHUNK
    x = pb_ref[0]
    row = lax.broadcasted_iota(jnp.int32, x.shape, 0)
    last = prev_ref[0, 7:8, :] * (i > 0).astype(F32)
    prev = jnp.where(row == 0, last, pltpu.roll(x, 1, 0))
    p = x + (prev - x) * mu_ref[...]
    o1, o2, o3 = B_WIDTH, 2 * B_WIDTH, 3 * B_WIDTH
    r, k, v = p[:, :o1], p[:, o1:o2], p[:, o2:o3]
    lora = p[:, o3:o3 + B_DECAY_LORA + B_AAA_LORA]
    gd = p[:, o3 + B_DECAY_LORA + B_AAA_LORA:]

    z = -(w0_ref[...] + _mm3(jnp.tanh(lora), w2_ref[...]))
    softplus = jnp.maximum(z, 0.0) + jnp.log(1.0 + jnp.exp(-jnp.abs(z)))
    logw = -jnp.exp(-softplus - 0.5)
    a = _sigmoid(a0_ref[...] + _mm3(lora, a2_ref[...]))
    g_ref[0] = _mm3(_sigmoid(gd), g2_ref[...])

    ones = _head_ones(B_WIDTH)
    kk = k * kk_ref[...]
    kk = kk * lax.rsqrt(jnp.maximum(_mm2(kk * kk, ones), 1e-24))
    k2 = k * (1.0 + (a - 1.0) * ka_ref[...])
    bonus_ref[0] = _mm2(r * k2 * rk_ref[...], ones) * v

    t_in = lax.broadcasted_iota(jnp.int32, (tm, B_WIDTH), 0) % C
    cum = logw
    sh = 1
    while sh < C:
        cum = cum + jnp.where(t_in >= sh, pltpu.roll(cum, sh, 0), 0.0)
        sh *= 2
    n = tm // C
    wc_ref[0] = jnp.exp(jnp.sum(logw.reshape(n, C, B_WIDTH), axis=1))
    e_pos = jnp.exp(cum)
    e_neg = jnp.exp(-cum)
    rt_ref[0] = r * e_pos
    at_ref[0] = -kk * jnp.exp(cum - logw)
    kt_ref[0] = k2 * e_neg
    bt_ref[0] = kk * a * e_neg
    v_ref[0] = v


def _rwkv_prep(pbc, mu, w0, w2p, a0, a2p, g2, k_k, k_a, r_k, tm):
    Bn, S, _ = pbc.shape
    W = B_WIDTH
    nl = B_DECAY_LORA + B_AAA_LORA
    row = lambda a: a.reshape(1, -1)
    full = lambda shp: pl.BlockSpec(shp, lambda b, i: (0,) * len(shp))
    seq = pl.BlockSpec((1, tm, W), lambda b, i: (b, i, 0))
    seq_shape = jax.ShapeDtypeStruct((Bn, S, W), F32)
    n = tm // RWKV_CHUNK
    return pl.pallas_call(
        functools.partial(_rwkv_prep_kernel, tm=tm),
        grid=(Bn, S // tm),
        in_specs=[
            pl.BlockSpec((1, tm, B_COLS), lambda b, i: (b, i, 0)),
            pl.BlockSpec((1, 8, B_COLS), lambda b, i: (b, jnp.maximum(i * (tm // 8) - 1, 0), 0)),
            full((1, B_COLS)), full((1, W)), full((nl, W)), full((1, W)), full((nl, W)),
            full((B_GATE_LORA, W)), full((1, W)), full((1, W)), full((1, W)),
        ],
        out_specs=[seq, seq, seq, seq, seq,
                   pl.BlockSpec((1, n, W), lambda b, i: (b, i, 0)), seq, seq],
        out_shape=[seq_shape] * 5 + [jax.ShapeDtypeStruct((Bn, S // RWKV_CHUNK, W), F32)] + [seq_shape] * 2,
        compiler_params=_cparams("arbitrary", "arbitrary"),
        name="rwkv_prep",
    )(pbc, pbc, row(mu), row(w0), w2p, row(a0), a2p, g2, row(k_k), row(k_a), row(r_k))


def _rwkv_scan_kernel(rt_ref, at_ref, kt_ref, bt_ref, v_ref, wc_ref, bonus_ref, g_ref,
                      lng_ref, lnb_ref, o_ref, state, *, tt):
    C = RWKV_CHUNK
    W = B_WIDTH

    @pl.when(pl.program_id(1) == 0)
    def _():
        state[...] = jnp.zeros_like(state)

    lane_head = lax.broadcasted_iota(jnp.int32, (C, W), 1) // B_HEAD_DIM
    tt_i = lax.broadcasted_iota(jnp.int32, (C, W), 0)
    ss_i = lax.broadcasted_iota(jnp.int32, (C, W), 1) % C
    strict = tt_i > ss_i
    incl = tt_i >= ss_i
    eye = (tt_i == ss_i).astype(F32)
    ones = _head_ones(W)
    bd_mask = ones.astype(F32)

    head_mask = [(lane_head == h).astype(BF16) for h in range(B_HEADS)]

    def bd_split(x):
        xb = x.astype(BF16)
        return jnp.concatenate([xb * mk for mk in head_mask], axis=0)

    def mm_bd(a, b_bd, dims=NN):
        return lax.dot_general(a.astype(BF16), b_bd, dims, preferred_element_type=F32)

    def state_free(gi):
        G = range(RWKV_GROUP)
        sls = [pl.ds(pl.multiple_of((gi * RWKV_GROUP + j) * C, C), C) for j in G]
        rt = [rt_ref[0, sl, :] for sl in sls]
        at = [at_ref[0, sl, :] for sl in sls]
        kt = [kt_ref[0, sl, :] for sl in sls]
        bt = [bt_ref[0, sl, :] for sl in sls]
        v = [v_ref[0, sl, :] for sl in sls]
        wc = [wc_ref[0, pl.ds(gi * RWKV_GROUP + j, 1), :] for j in G]
        ar = [jnp.concatenate([at[j], rt[j]], axis=0) for j in G]
        bdb = [bd_split(bt[j]) for j in G]
        bdk = [bd_split(kt[j]) for j in G]
        a_b = [mm_bd(ar[j], bdb[j], NT) for j in G]
        a_k = [mm_bd(ar[j], bdk[j], NT) for j in G]
        lo = [jnp.where(strict, a_b[j][:C], 0.0) for j in G]
        a_ak = [jnp.where(strict, a_k[j][:C], 0.0) for j in G]
        a_rb = [jnp.where(incl, a_b[j][C:], 0.0) for j in G]
        a_rk = [jnp.where(incl, a_k[j][C:], 0.0) for j in G]
        pw = lo
        tinv = [eye + lo[j] for j in G]
        bdp = [bd_split(pw[j]) for j in G]
        span = 2
        while span < C:
            pw = [mm_bd(pw[j], bdp[j]) for j in G]
            bdp = [bd_split(pw[j]) for j in G]
            tinv = [tinv[j] + mm_bd(tinv[j], bdp[j]) for j in G]
            span *= 2
        bdv = [bd_split(v[j]) for j in G]
        bda = [bd_split(at[j]) for j in G]
        abar = [mm_bd(tinv[j], bda[j]) for j in G]
        akv = [bd_split(mm_bd(a_ak[j], bdv[j])) for j in G]
        u0 = [mm_bd(tinv[j], akv[j]) for j in G]
        y0 = [mm_bd(a_rk[j], bdv[j]) for j in G]
        kv = [_mm(v[j], kt[j] * wc[j], TN) * bd_mask for j in G]
        return [(jnp.concatenate([abar[j], rt[j]], axis=0), u0[j], y0[j], a_rb[j], bt[j] * wc[j], kv[j], wc[j])
                for j in G]

    def group(gi, carry):
        pre = state_free(gi)
        s = state[...]
        ys = []
        for abar_rt, u0, y0, a_rb, btw, kv, wc in pre:
            a_s = _mm(abar_rt, s, NT)
            u = a_s[:C] + u0
            ys.append(a_s[C:] + y0 + mm_bd(a_rb, bd_split(u)))
            s = s * wc + _mm(u, btw, TN) * bd_mask + kv
        state[...] = s
        y = jnp.concatenate(ys, axis=0)
        sl = pl.ds(pl.multiple_of(gi * (RWKV_GROUP * C), RWKV_GROUP * C), RWKV_GROUP * C)
        mean = _mm2(y, ones) * (1.0 / B_HEAD_DIM)
        d = y - mean
        var = _mm2(d * d, ones) * (1.0 / B_HEAD_DIM)
        yn = d * lax.rsqrt(var + B_LNX_EPS) * lng_ref[...] + lnb_ref[...]
        o_ref[0, sl, :] = (yn + bonus_ref[0, sl, :]) * g_ref[0, sl, :]
        return carry

    lax.fori_loop(0, tt // (RWKV_GROUP * C), group, 0)


def _rwkv_scan(rt, at, kt, bt, v, wc, bonus, g, lnx_g, lnx_b, tt):
    Bn, S, W = rt.shape
    n = tt // RWKV_CHUNK
    seq = pl.BlockSpec((1, tt, W), lambda b, i: (b, i, 0))
    vec = pl.BlockSpec((1, W), lambda b, i: (0, 0))
    return pl.pallas_call(
        functools.partial(_rwkv_scan_kernel, tt=tt),
        grid=(Bn, S // tt),
        in_specs=[seq, seq, seq, seq, seq, pl.BlockSpec((1, n, W), lambda b, i: (b, i, 0)), seq, seq, vec, vec],
        out_specs=seq,
        out_shape=jax.ShapeDtypeStruct((Bn, S, W), F32),
        scratch_shapes=[pltpu.VMEM((B_HEADS * B_HEAD_DIM, W), F32)],
        compiler_params=_cparams("arbitrary", "arbitrary"),
        name="rwkv_scan",
    )(rt, at, kt, bt, v, wc, bonus, g, lnx_g.reshape(1, W), lnx_b.reshape(1, W))


def _gmlp_kernel(pc_ref, lng_ref, lnb_ref, ws_ref, bs_ref, o_ref, *, tm):
    x = pc_ref[0]
    z = x * (0.5 * (1.0 + jnp.tanh(math.sqrt(2.0 / math.pi) * (x + 0.044715 * (x * x * x)))))
    u, v = z[:, :C_WIDTH], z[:, C_WIDTH:]
    mu = jnp.mean(v, axis=-1, keepdims=True)
    d = v - mu
    var = jnp.mean(d * d, axis=-1, keepdims=True)
    vn = d * lax.rsqrt(var + LN_EPS) * lng_ref[...] + lnb_ref[...]
    group = lax.broadcasted_iota(jnp.int32, (CHUNK, C_WIDTH), 1) // C_GROUP_DIM
    tril = (lax.broadcasted_iota(jnp.int32, (CHUNK, CHUNK), 0)
            >= lax.broadcasted_iota(jnp.int32, (CHUNK, CHUNK), 1))
    ws = [jnp.where(tril, ws_ref[gi], 0.0).astype(BF16) for gi in range(C_GROUPS)]
    for c in range(tm // CHUNK):
        sl = slice(c * CHUNK, (c + 1) * CHUNK)
        vc = vn[sl].astype(BF16)
        sv = bs_ref[...]
        for gi in range(C_GROUPS):
            t = jnp.dot(ws[gi], vc, preferred_element_type=F32)
            sv = sv + jnp.where(group == gi, t, 0.0)
        o_ref[0, sl, :] = u[sl] * sv


def _gmlp(pbc, ln_g, ln_b, w_s, b_s, tm):
    Bn, S, _ = pbc.shape
    bs_wide = jnp.repeat(jnp.transpose(b_s), C_GROUP_DIM, axis=1)
    return pl.pallas_call(
        functools.partial(_gmlp_kernel, tm=tm),
        grid=(Bn, S // tm),
        in_specs=[
            pl.BlockSpec((1, tm, C_COLS), lambda b, i: (b, i, B_COLS // C_COLS)),
            pl.BlockSpec((1, C_WIDTH), lambda b, i: (0, 0)),
            pl.BlockSpec((1, C_WIDTH), lambda b, i: (0, 0)),
            pl.BlockSpec((C_GROUPS, CHUNK, CHUNK), lambda b, i: (0, 0, 0)),
            pl.BlockSpec((CHUNK, C_WIDTH), lambda b, i: (0, 0)),
        ],
        out_specs=pl.BlockSpec((1, tm, C_WIDTH), lambda b, i: (b, i, 0)),
        out_shape=jax.ShapeDtypeStruct((Bn, S, C_WIDTH), F32),
        compiler_params=_cparams("arbitrary", "arbitrary"),
        name="gmlp",
    )(pbc, ln_g.reshape(1, -1), ln_b.reshape(1, -1), w_s, bs_wide)


def _mid_kernel(ya_ref, yb_ref, yc_ref, x_ref, woa_ref, wob_ref, woc_ref, gpost_ref, g1_ref,
                gpre_ref, sc_ref, sh_ref, wr_ref, ws1_ref, ws3_ref, ws2_ref,
                xo_ref, h_ref, score_ref, shared_ref):
    y = (_mm(ya_ref[0], woa_ref[...]) + _mm(yb_ref[0], wob_ref[...]) + _mm(yc_ref[0], woc_ref[...]))
    xn = x_ref[0] + g1_ref[0] * (_rms(y) * gpost_ref[...])
    xo_ref[0] = xn
    h = _rms(xn) * gpre_ref[...] * (1.0 + sc_ref[0]) + sh_ref[0]
    h_ref[0] = _pack_bf16_pair(h)
    score_ref[0] = _sigmoid(_mm3(wr_ref[...], h, NT))
    hb = h.astype(BF16)
    t = _silu(jnp.dot(hb, ws1_ref[...], preferred_element_type=F32)) * jnp.dot(
        hb, ws3_ref[...], preferred_element_type=F32)
    shared_ref[0] = jnp.dot(t.astype(BF16), ws2_ref[...], preferred_element_type=F32)


def _mid(ya, yb, yc, x, woa, wob, woc, gpost, g1, gpre, sc, sh, wr, ws1, ws3, ws2, tm):
    Bn, S, D = x.shape
    NR = wr.shape[0]
    F = ws1.shape[1]
    seq = lambda w: pl.BlockSpec((1, tm, w), lambda b, i: (b, i, 0))
    full = lambda shp: pl.BlockSpec(shp, lambda b, i: (0,) * len(shp))
    per_b = pl.BlockSpec((1, 1, D), lambda b, i: (b, 0, 0))
    return pl.pallas_call(
        _mid_kernel,
        grid=(Bn, S // tm),
        in_specs=[seq(A_WIDTH), seq(B_WIDTH), seq(C_WIDTH), seq(D),
                  full((A_WIDTH, D)), full((B_WIDTH, D)), full((C_WIDTH, D)),
                  full((1, D)), per_b, full((1, D)), per_b, per_b,
                  full((NR, D)), full((D, F)), full((D, F)), full((F, D))],
        out_specs=[seq(D), seq(D // 2), pl.BlockSpec((1, NR, tm), lambda b, i: (b, 0, i)), seq(D)],
        out_shape=[jax.ShapeDtypeStruct((Bn, S, D), F32), jax.ShapeDtypeStruct((Bn, S, D // 2), jnp.uint32),
                   jax.ShapeDtypeStruct((Bn, NR, S), F32), jax.ShapeDtypeStruct((Bn, S, D), F32)],
        compiler_params=_cparams("arbitrary", "arbitrary"),
        name="mid",
    )(ya, yb, yc, x, woa, wob, woc, gpost.reshape(1, D), g1, gpre.reshape(1, D), sc, sh, wr, ws1, ws3, ws2)


def _first_argmax(vals, iota, n):
    m = jnp.max(vals, axis=0, keepdims=True)
    idx = jnp.min(jnp.where(vals == m, iota, n), axis=0, keepdims=True)
    return m, idx


def _route_kernel(sc_ref, bias_ref, e_ref, w_ref, r_ref, cnt_ref, carry, *, tm):
    @pl.when((pl.program_id(0) == 0) & (pl.program_id(1) == 0))
    def _():
        carry[...] = jnp.zeros_like(carry)

    G = EXPERTS_PER_GROUP
    s = sc_ref[0]
    biased = s + bias_ref[...]
    neg_inf = jnp.float32(-jnp.inf)
    io8 = lax.broadcasted_iota(jnp.int32, (G, tm), 0)
    gs_rows = []
    for g in range(N_GROUPS):
        blk = biased[g * G:(g + 1) * G]
        m1, i1 = _first_argmax(blk, io8, G)
        m2 = jnp.max(jnp.where(io8 == i1, neg_inf, blk), axis=0, keepdims=True)
        gs_rows.append(m1 + m2)
    gs = jnp.concatenate(gs_rows, axis=0)
    gio = lax.broadcasted_iota(jnp.int32, (N_GROUPS, tm), 0)
    gsel = jnp.zeros((N_GROUPS, tm), jnp.bool_)
    for _ in range(TOPK_GROUPS):
        _, gi = _first_argmax(gs, gio, N_GROUPS)
        pick = gio == gi
        gsel = gsel | pick
        gs = jnp.where(pick, neg_inf, gs)
    masked = jnp.concatenate(
        [jnp.where(gsel[g:g + 1], biased[g * G:(g + 1) * G], neg_inf) for g in range(N_GROUPS)], axis=0)

    eio = lax.broadcasted_iota(jnp.int32, (N_EXPERTS, tm), 0)
    picks, e_rows, s_rows = [], [], []
    for _ in range(TOP_K):
        _, ei = _first_argmax(masked, eio, N_EXPERTS)
        pick = eio == ei
        picks.append(pick)
        e_rows.append(ei)
        s_rows.append(jnp.sum(jnp.where(pick, s, 0.0), axis=0, keepdims=True))
        masked = jnp.where(pick, neg_inf, masked)
    top_s = jnp.concatenate(s_rows, axis=0)
    w_ref[...] = top_s / (jnp.sum(top_s, axis=0, keepdims=True) + 1e-20) * ROUTED_SCALE
    e_ref[...] = jnp.concatenate(e_rows, axis=0)

    sel = jnp.zeros((N_EXPERTS, tm), F32)
    for pick in picks:
        sel = sel + pick.astype(F32)
    before = (lax.broadcasted_iota(jnp.int32, (tm, tm), 0) < lax.broadcasted_iota(jnp.int32, (tm, tm), 1))
    pos = carry[...] + jnp.dot(sel.astype(BF16), before.astype(BF16), preferred_element_type=F32)
    r_ref[...] = jnp.concatenate(
        [jnp.sum(jnp.where(pick, pos, 0.0), axis=0, keepdims=True) for pick in picks], axis=0).astype(jnp.int32)
    total = carry[...] + jnp.sum(sel, axis=1, keepdims=True)
    carry[...] = total
    cnt_ref[...] = jnp.broadcast_to(total, cnt_ref.shape).astype(jnp.int32)


def _route(scores_t, e_bias, tm):
    Bn, _, S = scores_t.shape
    T = Bn * S
    nt = S // tm
    tok = pl.BlockSpec((TOP_K, tm), lambda b, i: (0, b * nt + i))
    return pl.pallas_call(
        functools.partial(_route_kernel, tm=tm),
        grid=(Bn, nt),
        in_specs=[pl.BlockSpec((1, N_EXPERTS, tm), lambda b, i: (b, 0, i)),
                  pl.BlockSpec((N_EXPERTS, 1), lambda b, i: (0, 0))],
        out_specs=[tok, tok, tok, pl.BlockSpec((N_EXPERTS, V7X_LANES), lambda b, i: (0, 0))],
        out_shape=[jax.ShapeDtypeStruct((TOP_K, T), jnp.int32), jax.ShapeDtypeStruct((TOP_K, T), F32),
                   jax.ShapeDtypeStruct((TOP_K, T), jnp.int32),
                   jax.ShapeDtypeStruct((N_EXPERTS, V7X_LANES), jnp.int32)],
        scratch_shapes=[pltpu.VMEM((N_EXPERTS, 1), F32)],
        compiler_params=_cparams("arbitrary", "arbitrary"),
        name="route",
    )(scores_t, e_bias.reshape(N_EXPERTS, 1))


def _dest_kernel(start_ref, e_ref, r_ref, o_ref):
    e = e_ref[...]
    acc = r_ref[...]
    for ex in range(N_EXPERTS):
        acc = acc + jnp.where(e == ex, start_ref[ex], 0)
    o_ref[0] = acc


def _dest_rows(pad_start, eidx, rank, tt):
    K_, T = eidx.shape
    grid_spec = pltpu.PrefetchScalarGridSpec(
        num_scalar_prefetch=1,
        grid=(T // tt,),
        in_specs=[pl.BlockSpec((K_, tt), lambda i, st: (0, i)), pl.BlockSpec((K_, tt), lambda i, st: (0, i))],
        out_specs=pl.BlockSpec((1, K_, tt), lambda i, st: (i, 0, 0)),
    )
    return pl.pallas_call(
        _dest_kernel,
        grid_spec=grid_spec,
        out_shape=jax.ShapeDtypeStruct((T // tt, K_, tt), jnp.int32),
        compiler_params=_cparams("arbitrary"),
        name="dest_rows",
    )(pad_start, eidx, rank)


def _dispatch_kernel(fill_ref, dest_hbm, h_ref, xs_hbm, dest_smem, zbuf, sem_idx, sem_rows, sem_zero):
    i = pl.program_id(0)
    EB = EXPERT_BLOCK

    def zero_copy(ex):
        return pltpu.make_async_copy(zbuf, xs_hbm.at[pl.ds(pl.multiple_of(fill_ref[ex], EB), EB), :], sem_zero)

    @pl.when(i == 0)
    def _():
        zbuf[...] = jnp.zeros_like(zbuf)

        def z_start(ex, c):
            @pl.when(fill_ref[ex] >= 0)
            def _():
                zero_copy(ex).start()
            return c

        def z_wait(ex, c):
            @pl.when(fill_ref[ex] >= 0)
            def _():
                zero_copy(ex).wait()
            return c

        lax.fori_loop(0, N_EXPERTS, z_start, 0)
        lax.fori_loop(0, N_EXPERTS, z_wait, 0)

    idx_copy = pltpu.make_async_copy(dest_hbm.at[i], dest_smem, sem_idx)
    idx_copy.start()
    idx_copy.wait()
    tt = h_ref.shape[0]

    def row_copy(t, k):
        return pltpu.make_async_copy(h_ref.at[pl.ds(t, 1), :], xs_hbm.at[pl.ds(dest_smem[k, t], 1), :], sem_rows)

    def issue(t, c):
        for k in range(TOP_K):
            row_copy(t, k).start()
        return c

    def drain(t, c):
        for k in range(TOP_K):
            row_copy(t, k).wait()
        return c

    lax.fori_loop(0, tt, issue, 0)
    lax.fori_loop(0, tt, drain, 0)


def _dispatch(fill_blocks, dest, h, n_rows):
    T, D = h.shape
    nt, K_, tt = dest.shape
    grid_spec = pltpu.PrefetchScalarGridSpec(
        num_scalar_prefetch=1,
        grid=(nt,),
        in_specs=[pl.BlockSpec(memory_space=pl.ANY), pl.BlockSpec((tt, D), lambda i, fl: (i, 0))],
        out_specs=pl.BlockSpec(memory_space=pl.ANY),
        scratch_shapes=[pltpu.SMEM((K_, tt), jnp.int32), pltpu.VMEM((EXPERT_BLOCK, D), h.dtype),
                        pltpu.SemaphoreType.DMA(()), pltpu.SemaphoreType.DMA(()), pltpu.SemaphoreType.DMA(())],
    )
    return pl.pallas_call(
        _dispatch_kernel,
        grid_spec=grid_spec,
        out_shape=jax.ShapeDtypeStruct((n_rows, D), h.dtype),
        compiler_params=_cparams("arbitrary"),
        name="dispatch",
    )(fill_blocks, dest, h)


def _expert_kernel(blk_e_ref, n_used_ref, x_ref, w1_ref, w3_ref, w2_ref, o_ref, w1b, w3b, w2b):
    i = pl.program_id(0)

    @pl.when((i == 0) | (blk_e_ref[i] != blk_e_ref[jnp.maximum(i - 1, 0)]))
    def _():
        w1b[...] = w1_ref[0].astype(BF16)
        w3b[...] = w3_ref[0].astype(BF16)
        w2b[...] = w2_ref[0].astype(BF16)

    @pl.when(i < n_used_ref[0])
    def _():
        x_lo, x_hi = _unpack_bf16_pair(x_ref[...])
        x_lo, x_hi = x_lo.astype(BF16), x_hi.astype(BF16)
        half = x_lo.shape[1]

        def up(wb):
            return (jnp.dot(x_lo, wb[:half, :], preferred_element_type=F32)
                    + jnp.dot(x_hi, wb[half:, :], preferred_element_type=F32))

        t = _silu(up(w1b)) * up(w3b)
        o_ref[...] = _pack_bf16_pair(jnp.dot(t.astype(BF16), w2b[...], preferred_element_type=F32))


def _experts(blk_e, n_used, xs, w1, w3, w2, layer):
    P, DP = xs.shape
    EB = EXPERT_BLOCK
    n_blocks = blk_e.shape[0]
    D, F = w1.shape[2], w1.shape[3]
    rows = pl.BlockSpec((EB, DP), lambda i, be, nu: (jnp.minimum(i, nu[0] - 1), 0))
    grid_spec = pltpu.PrefetchScalarGridSpec(
        num_scalar_prefetch=2,
        grid=(n_blocks,),
        in_specs=[
            rows,
            pl.BlockSpec((None, 1, D, F), lambda i, be, nu: (layer, be[i], 0, 0)),
            pl.BlockSpec((None, 1, D, F), lambda i, be, nu: (layer, be[i], 0, 0)),
            pl.BlockSpec((None, 1, F, D), lambda i, be, nu: (layer, be[i], 0, 0)),
        ],
        out_specs=rows,
        scratch_shapes=[pltpu.VMEM((D, F), BF16), pltpu.VMEM((D, F), BF16), pltpu.VMEM((F, D), BF16)],
    )
    return pl.pallas_call(
        _expert_kernel,
        grid_spec=grid_spec,
        out_shape=jax.ShapeDtypeStruct((P, DP), jnp.uint32),
        compiler_params=_cparams("arbitrary"),
        name="experts",
    )(blk_e, n_used, xs, w1, w3, w2)


def _block_layout(counts, n_blocks):
    EB = EXPERT_BLOCK
    padded = (counts + EB - 1) // EB * EB
    pad_end = jnp.cumsum(padded)
    pad_start = pad_end - padded
    blk_row = (jnp.arange(n_blocks) * EB)[:, None]
    blk_e = jnp.minimum(jnp.sum((pad_end[None, :] <= blk_row).astype(jnp.int32), axis=1), N_EXPERTS - 1)
    n_used = (pad_end[-1] // EB).astype(jnp.int32).reshape(1)
    fill = jnp.where(padded > counts, pad_end - EB, -1).astype(jnp.int32)
    return pad_start.astype(jnp.int32), blk_e, n_used, fill


def _combine_kernel(dest_hbm, ys_hbm, w_ref, x_ref, shared_ref, gpost_ref, g2_ref, o_ref,
                    dest_smem, buf, sem_idx, sem_rows, *, tt, nt):
    i = pl.program_id(0) * nt + pl.program_id(1)
    idx_copy = pltpu.make_async_copy(dest_hbm.at[i], dest_smem, sem_idx)
    idx_copy.start()
    idx_copy.wait()

    def row_copy(t, k):
        return pltpu.make_async_copy(ys_hbm.at[pl.ds(dest_smem[k, t], 1), :], buf.at[k, pl.ds(t, 1), :], sem_rows)

    def issue(t, c):
        for k in range(TOP_K):
            row_copy(t, k).start()
        return c

    def drain(t, c):
        for k in range(TOP_K):
            row_copy(t, k).wait()
        return c

    lax.fori_loop(0, tt, issue, 0)
    lax.fori_loop(0, tt, drain, 0)
    w = w_ref[...]
    y_lo = jnp.zeros((tt, buf.shape[2]), F32)
    y_hi = jnp.zeros((tt, buf.shape[2]), F32)
    for k in range(TOP_K):
        lo, hi = _unpack_bf16_pair(buf[k])
        y_lo = y_lo + w[:, k:k + 1] * lo
        y_hi = y_hi + w[:, k:k + 1] * hi
    y = shared_ref[0] + jnp.concatenate([y_lo, y_hi], axis=1)
    o_ref[0] = x_ref[0] + g2_ref[0] * (_rms(y) * gpost_ref[...])


def _combine(dest, ys, w_tok, x, shared, gpost, g2):
    Bn, S, D = x.shape
    _, K_, tt = dest.shape
    nt = S // tt
    seq = pl.BlockSpec((1, tt, D), lambda b, i: (b, i, 0))
    return pl.pallas_call(
        functools.partial(_combine_kernel, tt=tt, nt=nt),
        grid=(Bn, nt),
        in_specs=[pl.BlockSpec(memory_space=pl.ANY), pl.BlockSpec(memory_space=pl.ANY),
                  pl.BlockSpec((tt, K_), lambda b, i: (b * nt + i, 0)), seq, seq,
                  pl.BlockSpec((1, D), lambda b, i: (0, 0)), pl.BlockSpec((1, 1, D), lambda b, i: (b, 0, 0))],
        out_specs=seq,
        out_shape=jax.ShapeDtypeStruct((Bn, S, D), F32),
        scratch_shapes=[pltpu.SMEM((K_, tt), jnp.int32), pltpu.VMEM((K_, tt, ys.shape[1]), ys.dtype),
                        pltpu.SemaphoreType.DMA(()), pltpu.SemaphoreType.DMA(())],
        compiler_params=_cparams("arbitrary", "arbitrary"),
        name="combine",
    )(dest, ys, w_tok, x, shared, gpost.reshape(1, D), g2)


SC_GATHER_ROWS = 64


def _sc_gather_rows(table, idx):
    info = plsc.get_sparse_core_info()
    nc, ns = info.num_cores, info.num_subcores
    M = idx.shape[0]
    W = table.shape[1]
    b = SC_GATHER_ROWS
    per_worker = M // (nc * ns)
    steps = per_worker // b
    assert per_worker * nc * ns == M and steps * b == per_worker
    mesh = plsc.VectorSubcoreMesh(core_axis_name="c", subcore_axis_name="s")

    @functools.partial(
        pl.kernel, mesh=mesh,
        out_type=jax.ShapeDtypeStruct((M, W), table.dtype),
        scratch_types=[pltpu.VMEM((b,), jnp.int32), pltpu.VMEM((b, W), table.dtype), pltpu.SemaphoreType.DMA],
        name="sc_gather_rows",
    )
    def gather(table_hbm, idx_hbm, out_hbm, idx_v, rows_v, sem):
        wid = lax.axis_index("s") * nc + lax.axis_index("c")

        @pl.loop(0, steps)
        def _(s):
            base = pl.multiple_of(wid * per_worker + s * b, b)
            pltpu.sync_copy(idx_hbm.at[pl.ds(base, b)], idx_v)
            pltpu.async_copy(table_hbm.at[idx_v], rows_v, sem).wait()
            pltpu.sync_copy(rows_v, out_hbm.at[pl.ds(base, b)])

    return gather(table, idx)


def _combine_dense_kernel(rows_ref, w_ref, x_ref, shared_ref, gpost_ref, g2_ref, o_ref):
    w = w_ref[...]
    tt, half = rows_ref.shape[1], rows_ref.shape[2]
    y_lo = jnp.zeros((tt, half), F32)
    y_hi = jnp.zeros((tt, half), F32)
    for k in range(TOP_K):
        lo, hi = _unpack_bf16_pair(rows_ref[k])
        y_lo = y_lo + w[:, k:k + 1] * lo
        y_hi = y_hi + w[:, k:k + 1] * hi
    y = shared_ref[0] + jnp.concatenate([y_lo, y_hi], axis=1)
    o_ref[0] = x_ref[0] + g2_ref[0] * (_rms(y) * gpost_ref[...])


def _combine_dense(rows, w_tok, x, shared, gpost, g2, tt):
    Bn, S, D = x.shape
    K_, T, DP = rows.shape
    nt = S // tt
    seq = pl.BlockSpec((1, tt, D), lambda b, i: (b, i, 0))
    return pl.pallas_call(
        _combine_dense_kernel,
        grid=(Bn, nt),
        in_specs=[pl.BlockSpec((K_, tt, DP), lambda b, i: (0, b * nt + i, 0)),
                  pl.BlockSpec((tt, K_), lambda b, i: (b * nt + i, 0)), seq, seq,
                  pl.BlockSpec((1, D), lambda b, i: (0, 0)), pl.BlockSpec((1, 1, D), lambda b, i: (b, 0, 0))],
        out_specs=seq,
        out_shape=jax.ShapeDtypeStruct((Bn, S, D), F32),
        compiler_params=_cparams("arbitrary", "arbitrary"),
        name="combine_dense",
    )(rows, w_tok, x, shared, gpost.reshape(1, D), g2)


def kernel(x, c, w_ada, b_ada, norm_pre_mix, norm_post_mix, norm_pre_ffn, norm_post_ffn, w_in, w_out, rel_bias_table, diff_lambda, diff_subln, rwkv_mu, rwkv_w0, rwkv_w2, rwkv_a0, rwkv_a2, rwkv_g2, rwkv_k_k, rwkv_k_a, rwkv_r_k, rwkv_lnx_g, rwkv_lnx_b, gmlp_ln_g, gmlp_ln_b, gmlp_w_s, gmlp_b_s, router_w, router_bias, exp_w1, exp_w3, exp_w2, shared_w1, shared_w3, shared_w2):
    Bn, S, D = x.shape
    depth = w_ada.shape[0]
    tm = min(256, S)
    tq = min(256, S // 2)
    t_rwkv = min(512, S)

    mod = _adaln(c, w_ada, b_ada)
    band = _attn_band(rel_bias_table, tq)
    zpad = jnp.zeros((B_DECAY_LORA, B_WIDTH), F32)
    for l in range(depth):
        sh1, sc1, g1, sh2, sc2, g2 = [m.reshape(Bn, 1, D) for m in jnp.split(mod[l], 6, axis=-1)]
        w_in_b = w_in[l].astype(BF16)
        pa, pbc = _inproj(x, norm_pre_mix[l], sc1, sh1, w_in_b[:, :A_COLS], w_in_b[:, A_COLS:], tm)
        lambda_init = 0.8 - 0.6 * math.exp(-0.3 * l)
        ya = _diff_attention(pa, band, diff_lambda[l], diff_subln[l], lambda_init, tq)
        prep = _rwkv_prep(pbc, rwkv_mu[l], rwkv_w0[l], jnp.concatenate([rwkv_w2[l], zpad], axis=0),
                          rwkv_a0[l], jnp.concatenate([zpad, rwkv_a2[l]], axis=0), rwkv_g2[l],
                          rwkv_k_k[l], rwkv_k_a[l], rwkv_r_k[l].reshape(-1), t_rwkv)
        yb = _rwkv_scan(*prep, rwkv_lnx_g[l], rwkv_lnx_b[l], t_rwkv)
        yc = _gmlp(pbc, gmlp_ln_g[l], gmlp_ln_b[l], gmlp_w_s[l], gmlp_b_s[l], tm)

        w_out_b = w_out[l].astype(BF16)
        wr_t = jnp.pad(jnp.transpose(router_w[l]), ((0, V7X_LANES - N_EXPERTS), (0, 0)))
        x, h, scores_t, shared = _mid(
            ya, yb, yc, x, w_out_b[:A_WIDTH], w_out_b[A_WIDTH:A_WIDTH + B_WIDTH], w_out_b[A_WIDTH + B_WIDTH:],
            norm_post_mix[l], g1, norm_pre_ffn[l], sc2, sh2, wr_t,
            shared_w1[l].astype(BF16), shared_w3[l].astype(BF16), shared_w2[l].astype(BF16), tm)

        T = Bn * S
        n_blocks = -(-T * TOP_K // EXPERT_BLOCK) + N_EXPERTS
        eidx, wgt, rank, cnt = _route(scores_t, router_bias[l], tm)
        pad_start, blk_e, n_used, fill = _block_layout(cnt[:, 0], n_blocks)
        dest = _dest_rows(pad_start, eidx, rank, tm)
        xs = _dispatch(fill, dest, h.reshape(T, D // 2), n_blocks * EXPERT_BLOCK)
        ys = _experts(blk_e, n_used, xs, exp_w1, exp_w3, exp_w2, l)
        dest_kt = jnp.transpose(dest, (1, 0, 2)).reshape(TOP_K * T)
        rows = _sc_gather_rows(lax.bitcast_convert_type(ys, jnp.int32), dest_kt)
        rows = lax.bitcast_convert_type(rows, jnp.uint32).reshape(TOP_K, T, D // 2)
        x = _combine_dense(rows, jnp.transpose(wgt), x, shared, norm_post_ffn[l], g2, tm)
    return x
```

```python
import functools
import math

import jax
import jax.numpy as jnp
from jax import lax
from jax.experimental import pallas as pl
from jax.experimental.pallas import tpu as pltpu
from jax.experimental.pallas import tpu_sc as plsc

F32 = jnp.float32
BF16 = jnp.bfloat16

A_HEADS = 4
A_QK_DIM = 64
A_HEAD_W = 2 * A_QK_DIM
A_WIDTH = A_HEADS * A_HEAD_W
N_BUCKETS = 32
MAX_DISTANCE = 128
B_HEADS = 4
B_HEAD_DIM = 64
B_WIDTH = B_HEADS * B_HEAD_DIM
B_DECAY_LORA = 64
B_AAA_LORA = 64
B_GATE_LORA = 128
B_LNX_EPS = 64e-5
C_GROUPS = 4
C_GROUP_DIM = 64
C_WIDTH = C_GROUPS * C_GROUP_DIM
CHUNK = 128
A_COLS = 3 * A_WIDTH
B_COLS = 3 * B_WIDTH + B_DECAY_LORA + B_AAA_LORA + B_GATE_LORA
C_COLS = 2 * C_WIDTH
N_EXPERTS = 64
TOP_K = 8
N_GROUPS = 8
TOPK_GROUPS = 4
EXPERTS_PER_GROUP = N_EXPERTS // N_GROUPS
ROUTED_SCALE = 2.5
EXPERT_BLOCK = 256
RMS_EPS = 1e-6
LN_EPS = 1e-5
NEG_BIG = -1e30

V7X_LANES = 128
V7X_VMEM_LIMIT_BYTES = 56 * 1024 * 1024
RWKV_CHUNK = 64
RWKV_GROUP = 8

NN = (((1,), (0,)), ((), ()))
NT = (((1,), (1,)), ((), ()))
TN = (((0,), (0,)), ((), ()))


def _cparams(*sem):
    return pltpu.CompilerParams(dimension_semantics=sem, vmem_limit_bytes=V7X_VMEM_LIMIT_BYTES)


def _mm(a, b, dims=NN):
    return lax.dot_general(a.astype(BF16), b.astype(BF16), dims, preferred_element_type=F32)


def _split(a):
    hi = a.astype(BF16)
    lo = (a - hi.astype(F32)).astype(BF16)
    return hi, lo


def _mm3(a, b, dims=NN):
    ah, al = _split(a)
    bh, bl = _split(b)
    d = lambda x, y: lax.dot_general(x, y, dims, preferred_element_type=F32)
    return d(ah, bh) + d(ah, bl) + d(al, bh)


def _mm2(a, b_exact, dims=NN):
    ah, al = _split(a)
    d = lambda x: lax.dot_general(x, b_exact, dims, preferred_element_type=F32)
    return d(ah) + d(al)


def _pack_bf16_pair(x):
    n = x.shape[1] // 2
    bits = lax.bitcast_convert_type(x.astype(BF16).astype(F32), jnp.int32)
    return ((bits[:, :n] >> 16) & 0xFFFF) | bits[:, n:]


def _unpack_bf16_pair(u):
    lo = lax.bitcast_convert_type(u << 16, F32)
    hi = lax.bitcast_convert_type(u & jnp.int32(-65536), F32)
    return lo, hi


def _rms(x, eps=RMS_EPS):
    return x * lax.rsqrt(jnp.mean(x * x, axis=-1, keepdims=True) + eps)


def _sigmoid(x):
    return 1.0 / (1.0 + jnp.exp(-x))


def _silu(x):
    return x * _sigmoid(x)


def _adaln_kernel(c_ref, w_ref, b_ref, o_ref):
    c = c_ref[...]
    o_ref[0] = _mm3(_silu(c), w_ref[0]) + b_ref[0]


def _adaln(c, w_ada, b_ada):
    L, D, N = w_ada.shape
    Bn = c.shape[0]
    tn = min(N, 1536)
    return pl.pallas_call(
        _adaln_kernel,
        grid=(L, N // tn),
        in_specs=[
            pl.BlockSpec((Bn, D), lambda l, j: (0, 0)),
            pl.BlockSpec((1, D, tn), lambda l, j: (l, 0, j)),
            pl.BlockSpec((1, 1, tn), lambda l, j: (l, 0, j)),
        ],
        out_specs=pl.BlockSpec((1, Bn, tn), lambda l, j: (l, 0, j)),
        out_shape=jax.ShapeDtypeStruct((L, Bn, N), F32),
        compiler_params=_cparams("arbitrary", "arbitrary"),
        name="adaln",
    )(c, w_ada, b_ada.reshape(L, 1, N))


def _inproj_kernel(x_ref, g_ref, sc_ref, sh_ref, wa_ref, wbc_ref, oa_ref, obc_ref):
    x = x_ref[0]
    h = _rms(x) * g_ref[...] * (1.0 + sc_ref[0]) + sh_ref[0]
    hb = h.astype(BF16)
    oa_ref[0] = jnp.dot(hb, wa_ref[...], preferred_element_type=F32).astype(BF16)
    obc_ref[0] = jnp.dot(hb, wbc_ref[...], preferred_element_type=F32)


def _inproj(x, g, sc, sh, wa, wbc, tm):
    Bn, S, D = x.shape
    na, nbc = wa.shape[1], wbc.shape[1]
    return pl.pallas_call(
        _inproj_kernel,
        grid=(Bn, S // tm),
        in_specs=[
            pl.BlockSpec((1, tm, D), lambda b, i: (b, i, 0)),
            pl.BlockSpec((1, D), lambda b, i: (0, 0)),
            pl.BlockSpec((1, 1, D), lambda b, i: (b, 0, 0)),
            pl.BlockSpec((1, 1, D), lambda b, i: (b, 0, 0)),
            pl.BlockSpec((D, na), lambda b, i: (0, 0)),
            pl.BlockSpec((D, nbc), lambda b, i: (0, 0)),
        ],
        out_specs=[
            pl.BlockSpec((1, tm, na), lambda b, i: (b, i, 0)),
            pl.BlockSpec((1, tm, nbc), lambda b, i: (b, i, 0)),
        ],
        out_shape=[
            jax.ShapeDtypeStruct((Bn, S, na), BF16),
            jax.ShapeDtypeStruct((Bn, S, nbc), F32),
        ],
        compiler_params=_cparams("arbitrary", "arbitrary"),
        name="inproj",
    )(x, g.reshape(1, D), sc, sh, wa, wbc)


def _t5_bucket(dist):
    n = jnp.maximum(dist, 0)
    max_exact = N_BUCKETS // 2
    nf = jnp.maximum(n, 1).astype(F32)
    large = max_exact + (jnp.log(nf / max_exact) / math.log(MAX_DISTANCE / max_exact)
                         * (N_BUCKETS - max_exact)).astype(jnp.int32)
    large = jnp.minimum(large, N_BUCKETS - 1)
    return jnp.where(n < max_exact, n, large)


def _attn_band(table, tq):
    far = table[N_BUCKETS - 1].astype(F32)
    L = 3 * tq
    m = jnp.arange(L)
    m = jnp.where(m < 2 * tq, m, m - L)
    bands = []
    for off in (0, tq):
        dist = off - m
        vals = jnp.where(dist[None] >= 0, jnp.transpose(table[_t5_bucket(dist)].astype(F32)) - far[:, None],
                         NEG_BIG)
        toe = jnp.tile(vals, (1, tq))[:, :tq * (L - 1)].reshape(-1, tq, L - 1)
        bands.append(toe[:, :, :2 * tq])
    band = jnp.stack(bands)
    return jnp.concatenate([band, band], axis=2)


def _attn_kernel(lam_ref, q_ref, k_ref, v_ref, band_ref, g_ref, o_ref, *, tq, lambda_init):
    i = pl.program_id(2)
    q = q_ref[0] * jnp.asarray(A_QK_DIM ** -0.5, BF16)
    lane = lax.broadcasted_iota(jnp.int32, q.shape, 1)
    zero = jnp.zeros_like(q)
    qq = jnp.concatenate([jnp.where(lane < A_QK_DIM, q, zero),
                          jnp.where(lane >= A_QK_DIM, q, zero)], axis=0)

    kb0 = pl.multiple_of(jnp.maximum(i - 1, 0) * tq, tq)
    kb = k_ref[0, pl.ds(kb0, 2 * tq), :]
    vb = v_ref[0, pl.ds(kb0, 2 * tq), :]
    s = lax.dot_general(qq, kb, NT, preferred_element_type=F32) + band_ref[0, 0]
    m = jnp.max(s, axis=-1, keepdims=True)
    p = jnp.exp(s - m)
    l = jnp.sum(p, axis=-1, keepdims=True)
    acc = jnp.dot(p.astype(BF16), vb, preferred_element_type=F32)

    n_far = jnp.maximum(i - 1, 0)

    def logits(j):
        return lax.dot_general(qq, k_ref[0, pl.ds(pl.multiple_of(j * tq, tq), tq), :], NT,
                               preferred_element_type=F32)

    def body(j, carry):
        m, l, acc, s = carry
        s_next = logits(jnp.minimum(j + 1, n_far - 1))
        vj = v_ref[0, pl.ds(pl.multiple_of(j * tq, tq), tq), :]
        m_new = jnp.maximum(m, jnp.max(s, axis=-1, keepdims=True))
        alpha = jnp.exp(m - m_new)
        p = jnp.exp(s - m_new)
        l = alpha * l + jnp.sum(p, axis=-1, keepdims=True)
        acc = alpha * acc + jnp.dot(p.astype(BF16), vj, preferred_element_type=F32)
        return m_new, l, acc, s_next

    m, l, acc, _ = lax.fori_loop(0, n_far, body, (m, l, acc, logits(0)))

    lp = lam_ref[...]
    lam = (jnp.exp(jnp.sum(lp[0:1] * lp[1:2], axis=-1, keepdims=True))
           - jnp.exp(jnp.sum(lp[2:3] * lp[3:4], axis=-1, keepdims=True)) + lambda_init)
    o = acc / l
    o = o[:tq] - lam * o[tq:]
    o_ref[0] = _rms(o) * g_ref[...] * (1.0 - lambda_init)


def _diff_attention(pa, band, lam_par, subln_g, lambda_init, tq):
    Bn, S, _ = pa.shape
    W = A_HEAD_W
    kern = functools.partial(_attn_kernel, tq=tq, lambda_init=lambda_init)
    return pl.pallas_call(
        kern,
        grid=(Bn, A_HEADS, S // tq),
        in_specs=[
            pl.BlockSpec((4, A_QK_DIM), lambda b, h, i: (0, 0)),
            pl.BlockSpec((1, tq, W), lambda b, h, i: (b, i, h)),
            pl.BlockSpec((1, S, W), lambda b, h, i: (b, 0, A_HEADS + h)),
            pl.BlockSpec((1, S, W), lambda b, h, i: (b, 0, 2 * A_HEADS + h)),
            pl.BlockSpec((1, 1, 2 * tq, 2 * tq), lambda b, h, i: (jnp.minimum(i, 1), h, 0, 0)),
            pl.BlockSpec((1, W), lambda b, h, i: (0, 0)),
        ],
        out_specs=pl.BlockSpec((1, tq, W), lambda b, h, i: (b, i, h)),
        out_shape=jax.ShapeDtypeStruct((Bn, S, A_WIDTH), F32),
        compiler_params=_cparams("arbitrary", "arbitrary", "arbitrary"),
        name="diff_attn",
    )(lam_par, pa, pa, pa, band, subln_g.reshape(1, W))


def _head_ones(n):
    r = lax.broadcasted_iota(jnp.int32, (n, n), 0) // B_HEAD_DIM
    c = lax.broadcasted_iota(jnp.int32, (n, n), 1) // B_HEAD_DIM
    return (r == c).astype(BF16)


def _rwkv_prep_kernel(pb_ref, prev_ref, mu_ref, w0_ref, w2_ref, a0_ref, a2_ref, g2_ref,
                      kk_ref, ka_ref, rk_ref,
                      rt_ref, at_ref, kt_ref, bt_ref, v_ref, wc_ref, bonus_ref, g_ref, *, tm):
    i = pl.program_id(1)
    C = RWKV_CHUNK
    x = pb_ref[0]
    row = lax.broadcasted_iota(jnp.int32, x.shape, 0)
    last = prev_ref[0, 7:8, :] * (i > 0).astype(F32)
    prev = jnp.where(row == 0, last, pltpu.roll(x, 1, 0))
    p = x + (prev - x) * mu_ref[...]
    o1, o2, o3 = B_WIDTH, 2 * B_WIDTH, 3 * B_WIDTH
    r, k, v = p[:, :o1], p[:, o1:o2], p[:, o2:o3]
    lora = p[:, o3:o3 + B_DECAY_LORA + B_AAA_LORA]
    gd = p[:, o3 + B_DECAY_LORA + B_AAA_LORA:]

    z = -(w0_ref[...] + _mm3(jnp.tanh(lora), w2_ref[...]))
    softplus = jnp.maximum(z, 0.0) + jnp.log(1.0 + jnp.exp(-jnp.abs(z)))
    logw = -jnp.exp(-softplus - 0.5)
    a = _sigmoid(a0_ref[...] + _mm3(lora, a2_ref[...]))
    g_ref[0] = _mm3(_sigmoid(gd), g2_ref[...])

    ones = _head_ones(B_WIDTH)
    kk = k * kk_ref[...]
    kk = kk * lax.rsqrt(jnp.maximum(_mm2(kk * kk, ones), 1e-24))
    k2 = k * (1.0 + (a - 1.0) * ka_ref[...])
    bonus_ref[0] = _mm2(r * k2 * rk_ref[...], ones) * v

    t_in = lax.broadcasted_iota(jnp.int32, (tm, B_WIDTH), 0) % C
    cum = logw
    sh = 1
    while sh < C:
        cum = cum + jnp.where(t_in >= sh, pltpu.roll(cum, sh, 0), 0.0)
        sh *= 2
    n = tm // C
    wc_ref[0] = jnp.exp(jnp.sum(logw.reshape(n, C, B_WIDTH), axis=1))
    e_pos = jnp.exp(cum)
    e_neg = jnp.exp(-cum)
    rt_ref[0] = r * e_pos
    at_ref[0] = -kk * jnp.exp(cum - logw)
    kt_ref[0] = k2 * e_neg
    bt_ref[0] = kk * a * e_neg
    v_ref[0] = v


def _rwkv_prep(pbc, mu, w0, w2p, a0, a2p, g2, k_k, k_a, r_k, tm):
    Bn, S, _ = pbc.shape
    W = B_WIDTH
    nl = B_DECAY_LORA + B_AAA_LORA
    row = lambda a: a.reshape(1, -1)
    full = lambda shp: pl.BlockSpec(shp, lambda b, i: (0,) * len(shp))
    seq = pl.BlockSpec((1, tm, W), lambda b, i: (b, i, 0))
    seq_shape = jax.ShapeDtypeStruct((Bn, S, W), F32)
    n = tm // RWKV_CHUNK
    return pl.pallas_call(
        functools.partial(_rwkv_prep_kernel, tm=tm),
        grid=(Bn, S // tm),
        in_specs=[
            pl.BlockSpec((1, tm, B_COLS), lambda b, i: (b, i, 0)),
            pl.BlockSpec((1, 8, B_COLS), lambda b, i: (b, jnp.maximum(i * (tm // 8) - 1, 0), 0)),
            full((1, B_COLS)), full((1, W)), full((nl, W)), full((1, W)), full((nl, W)),
            full((B_GATE_LORA, W)), full((1, W)), full((1, W)), full((1, W)),
        ],
        out_specs=[seq, seq, seq, seq, seq,
                   pl.BlockSpec((1, n, W), lambda b, i: (b, i, 0)), seq, seq],
        out_shape=[seq_shape] * 5 + [jax.ShapeDtypeStruct((Bn, S // RWKV_CHUNK, W), F32)] + [seq_shape] * 2,
        compiler_params=_cparams("arbitrary", "arbitrary"),
        name="rwkv_prep",
    )(pbc, pbc, row(mu), row(w0), w2p, row(a0), a2p, g2, row(k_k), row(k_a), row(r_k))


def _rwkv_scan_kernel(rt_ref, at_ref, kt_ref, bt_ref, v_ref, wc_ref, bonus_ref, g_ref,
                      lng_ref, lnb_ref, o_ref, state, *, tt):
    C = RWKV_CHUNK
    W = B_WIDTH

    @pl.when(pl.program_id(1) == 0)
    def _():
        state[...] = jnp.zeros_like(state)

    lane_head = lax.broadcasted_iota(jnp.int32, (C, W), 1) // B_HEAD_DIM
    tt_i = lax.broadcasted_iota(jnp.int32, (C, W), 0)
    ss_i = lax.broadcasted_iota(jnp.int32, (C, W), 1) % C
    strict = tt_i > ss_i
    incl = tt_i >= ss_i
    eye = (tt_i == ss_i).astype(F32)
    ones = _head_ones(W)
    bd_mask = ones.astype(F32)

    head_mask = [(lane_head == h).astype(BF16) for h in range(B_HEADS)]

    def bd_split(x):
        xb = x.astype(BF16)
        return jnp.concatenate([xb * mk for mk in head_mask], axis=0)

    def mm_bd(a, b_bd, dims=NN):
        return lax.dot_general(a.astype(BF16), b_bd, dims, preferred_element_type=F32)

    def state_free(gi):
        G = range(RWKV_GROUP)
        sls = [pl.ds(pl.multiple_of((gi * RWKV_GROUP + j) * C, C), C) for j in G]
        rt = [rt_ref[0, sl, :] for sl in sls]
        at = [at_ref[0, sl, :] for sl in sls]
        kt = [kt_ref[0, sl, :] for sl in sls]
        bt = [bt_ref[0, sl, :] for sl in sls]
        v = [v_ref[0, sl, :] for sl in sls]
        wc = [wc_ref[0, pl.ds(gi * RWKV_GROUP + j, 1), :] for j in G]
        ar = [jnp.concatenate([at[j], rt[j]], axis=0) for j in G]
        bdb = [bd_split(bt[j]) for j in G]
        bdk = [bd_split(kt[j]) for j in G]
        a_b = [mm_bd(ar[j], bdb[j], NT) for j in G]
        a_k = [mm_bd(ar[j], bdk[j], NT) for j in G]
        lo = [jnp.where(strict, a_b[j][:C], 0.0) for j in G]
        a_ak = [jnp.where(strict, a_k[j][:C], 0.0) for j in G]
        a_rb = [jnp.where(incl, a_b[j][C:], 0.0) for j in G]
        a_rk = [jnp.where(incl, a_k[j][C:], 0.0) for j in G]
        pw = lo
        tinv = [eye + lo[j] for j in G]
        bdp = [bd_split(pw[j]) for j in G]
        span = 2
        while span < C:
            pw = [mm_bd(pw[j], bdp[j]) for j in G]
            bdp = [bd_split(pw[j]) for j in G]
            tinv = [tinv[j] + mm_bd(tinv[j], bdp[j]) for j in G]
            span *= 2
        bdv = [bd_split(v[j]) for j in G]
        bda = [bd_split(at[j]) for j in G]
        abar = [mm_bd(tinv[j], bda[j]) for j in G]
        akv = [bd_split(mm_bd(a_ak[j], bdv[j])) for j in G]
        u0 = [mm_bd(tinv[j], akv[j]) for j in G]
        y0 = [mm_bd(a_rk[j], bdv[j]) for j in G]
        kv = [_mm(v[j], kt[j] * wc[j], TN) * bd_mask for j in G]
        return [(jnp.concatenate([abar[j], rt[j]], axis=0), u0[j], y0[j], a_rb[j], bt[j] * wc[j], kv[j], wc[j])
                for j in G]

    def group(gi, carry):
        pre = state_free(gi)
        s = state[...]
        ys = []
        for abar_rt, u0, y0, a_rb, btw, kv, wc in pre:
            a_s = _mm(abar_rt, s, NT)
            u = a_s[:C] + u0
            ys.append(a_s[C:] + y0 + mm_bd(a_rb, bd_split(u)))
            s = s * wc + _mm(u, btw, TN) * bd_mask + kv
        state[...] = s
        y = jnp.concatenate(ys, axis=0)
        sl = pl.ds(pl.multiple_of(gi * (RWKV_GROUP * C), RWKV_GROUP * C), RWKV_GROUP * C)
        mean = _mm2(y, ones) * (1.0 / B_HEAD_DIM)
        d = y - mean
        var = _mm2(d * d, ones) * (1.0 / B_HEAD_DIM)
        yn = d * lax.rsqrt(var + B_LNX_EPS) * lng_ref[...] + lnb_ref[...]
        o_ref[0, sl, :] = (yn + bonus_ref[0, sl, :]) * g_ref[0, sl, :]
        return carry

    lax.fori_loop(0, tt // (RWKV_GROUP * C), group, 0)


def _rwkv_scan(rt, at, kt, bt, v, wc, bonus, g, lnx_g, lnx_b, tt):
    Bn, S, W = rt.shape
    n = tt // RWKV_CHUNK
    seq = pl.BlockSpec((1, tt, W), lambda b, i: (b, i, 0))
    vec = pl.BlockSpec((1, W), lambda b, i: (0, 0))
    return pl.pallas_call(
        functools.partial(_rwkv_scan_kernel, tt=tt),
        grid=(Bn, S // tt),
        in_specs=[seq, seq, seq, seq, seq, pl.BlockSpec((1, n, W), lambda b, i: (b, i, 0)), seq, seq, vec, vec],
        out_specs=seq,
        out_shape=jax.ShapeDtypeStruct((Bn, S, W), F32),
        scratch_shapes=[pltpu.VMEM((B_HEADS * B_HEAD_DIM, W), F32)],
        compiler_params=_cparams("arbitrary", "arbitrary"),
        name="rwkv_scan",
    )(rt, at, kt, bt, v, wc, bonus, g, lnx_g.reshape(1, W), lnx_b.reshape(1, W))


def _gmlp_kernel(pc_ref, lng_ref, lnb_ref, ws_ref, bs_ref, o_ref, *, tm):
    x = pc_ref[0]
    z = x * (0.5 * (1.0 + jnp.tanh(math.sqrt(2.0 / math.pi) * (x + 0.044715 * (x * x * x)))))
    u, v = z[:, :C_WIDTH], z[:, C_WIDTH:]
    mu = jnp.mean(v, axis=-1, keepdims=True)
    d = v - mu
    var = jnp.mean(d * d, axis=-1, keepdims=True)
    vn = d * lax.rsqrt(var + LN_EPS) * lng_ref[...] + lnb_ref[...]
    group = lax.broadcasted_iota(jnp.int32, (CHUNK, C_WIDTH), 1) // C_GROUP_DIM
    tril = (lax.broadcasted_iota(jnp.int32, (CHUNK, CHUNK), 0)
            >= lax.broadcasted_iota(jnp.int32, (CHUNK, CHUNK), 1))
    ws = [jnp.where(tril, ws_ref[gi], 0.0).astype(BF16) for gi in range(C_GROUPS)]
    for c in range(tm // CHUNK):
        sl = slice(c * CHUNK, (c + 1) * CHUNK)
        vc = vn[sl].astype(BF16)
        sv = bs_ref[...]
        for gi in range(C_GROUPS):
            t = jnp.dot(ws[gi], vc, preferred_element_type=F32)
            sv = sv + jnp.where(group == gi, t, 0.0)
        o_ref[0, sl, :] = u[sl] * sv


def _gmlp(pbc, ln_g, ln_b, w_s, b_s, tm):
    Bn, S, _ = pbc.shape
    bs_wide = jnp.repeat(jnp.transpose(b_s), C_GROUP_DIM, axis=1)
    return pl.pallas_call(
        functools.partial(_gmlp_kernel, tm=tm),
        grid=(Bn, S // tm),
        in_specs=[
            pl.BlockSpec((1, tm, C_COLS), lambda b, i: (b, i, B_COLS // C_COLS)),
            pl.BlockSpec((1, C_WIDTH), lambda b, i: (0, 0)),
            pl.BlockSpec((1, C_WIDTH), lambda b, i: (0, 0)),
            pl.BlockSpec((C_GROUPS, CHUNK, CHUNK), lambda b, i: (0, 0, 0)),
            pl.BlockSpec((CHUNK, C_WIDTH), lambda b, i: (0, 0)),
        ],
        out_specs=pl.BlockSpec((1, tm, C_WIDTH), lambda b, i: (b, i, 0)),
        out_shape=jax.ShapeDtypeStruct((Bn, S, C_WIDTH), F32),
        compiler_params=_cparams("arbitrary", "arbitrary"),
        name="gmlp",
    )(pbc, ln_g.reshape(1, -1), ln_b.reshape(1, -1), w_s, bs_wide)


def _mid_kernel(ya_ref, yb_ref, yc_ref, x_ref, woa_ref, wob_ref, woc_ref, gpost_ref, g1_ref,
                gpre_ref, sc_ref, sh_ref, wr_ref, ws1_ref, ws3_ref, ws2_ref,
                xo_ref, h_ref, score_ref, shared_ref):
    y = (_mm(ya_ref[0], woa_ref[...]) + _mm(yb_ref[0], wob_ref[...]) + _mm(yc_ref[0], woc_ref[...]))
    xn = x_ref[0] + g1_ref[0] * (_rms(y) * gpost_ref[...])
    xo_ref[0] = xn
    h = _rms(xn) * gpre_ref[...] * (1.0 + sc_ref[0]) + sh_ref[0]
    h_ref[0] = _pack_bf16_pair(h)
    score_ref[0] = _sigmoid(_mm3(wr_ref[...], h, NT))
    hb = h.astype(BF16)
    t = _silu(jnp.dot(hb, ws1_ref[...], preferred_element_type=F32)) * jnp.dot(
        hb, ws3_ref[...], preferred_element_type=F32)
    shared_ref[0] = jnp.dot(t.astype(BF16), ws2_ref[...], preferred_element_type=F32)


def _mid(ya, yb, yc, x, woa, wob, woc, gpost, g1, gpre, sc, sh, wr, ws1, ws3, ws2, tm):
    Bn, S, D = x.shape
    NR = wr.shape[0]
    F = ws1.shape[1]
    seq = lambda w: pl.BlockSpec((1, tm, w), lambda b, i: (b, i, 0))
    full = lambda shp: pl.BlockSpec(shp, lambda b, i: (0,) * len(shp))
    per_b = pl.BlockSpec((1, 1, D), lambda b, i: (b, 0, 0))
    return pl.pallas_call(
        _mid_kernel,
        grid=(Bn, S // tm),
        in_specs=[seq(A_WIDTH), seq(B_WIDTH), seq(C_WIDTH), seq(D),
                  full((A_WIDTH, D)), full((B_WIDTH, D)), full((C_WIDTH, D)),
                  full((1, D)), per_b, full((1, D)), per_b, per_b,
                  full((NR, D)), full((D, F)), full((D, F)), full((F, D))],
        out_specs=[seq(D), seq(D // 2), pl.BlockSpec((1, NR, tm), lambda b, i: (b, 0, i)), seq(D)],
        out_shape=[jax.ShapeDtypeStruct((Bn, S, D), F32), jax.ShapeDtypeStruct((Bn, S, D // 2), jnp.int32),
                   jax.ShapeDtypeStruct((Bn, NR, S), F32), jax.ShapeDtypeStruct((Bn, S, D), F32)],
        compiler_params=_cparams("arbitrary", "arbitrary"),
        name="mid",
    )(ya, yb, yc, x, woa, wob, woc, gpost.reshape(1, D), g1, gpre.reshape(1, D), sc, sh, wr, ws1, ws3, ws2)


def _first_argmax(vals, iota, n):
    m = jnp.max(vals, axis=0, keepdims=True)
    idx = jnp.min(jnp.where(vals == m, iota, n), axis=0, keepdims=True)
    return m, idx


def _route_kernel(sc_ref, bias_ref, e_ref, w_ref, r_ref, cnt_ref, carry, *, tm):
    @pl.when((pl.program_id(0) == 0) & (pl.program_id(1) == 0))
    def _():
        carry[...] = jnp.zeros_like(carry)

    G = EXPERTS_PER_GROUP
    s = sc_ref[0]
    biased = s + bias_ref[...]
    neg_inf = jnp.float32(-jnp.inf)
    io8 = lax.broadcasted_iota(jnp.int32, (G, tm), 0)
    gs_rows = []
    for g in range(N_GROUPS):
        blk = biased[g * G:(g + 1) * G]
        m1, i1 = _first_argmax(blk, io8, G)
        m2 = jnp.max(jnp.where(io8 == i1, neg_inf, blk), axis=0, keepdims=True)
        gs_rows.append(m1 + m2)
    gs = jnp.concatenate(gs_rows, axis=0)
    gio = lax.broadcasted_iota(jnp.int32, (N_GROUPS, tm), 0)
    gsel = jnp.zeros((N_GROUPS, tm), jnp.bool_)
    for _ in range(TOPK_GROUPS):
        _, gi = _first_argmax(gs, gio, N_GROUPS)
        pick = gio == gi
        gsel = gsel | pick
        gs = jnp.where(pick, neg_inf, gs)
    masked = jnp.concatenate(
        [jnp.where(gsel[g:g + 1], biased[g * G:(g + 1) * G], neg_inf) for g in range(N_GROUPS)], axis=0)

    eio = lax.broadcasted_iota(jnp.int32, (N_EXPERTS, tm), 0)
    picks, e_rows, s_rows = [], [], []
    for _ in range(TOP_K):
        _, ei = _first_argmax(masked, eio, N_EXPERTS)
        pick = eio == ei
        picks.append(pick)
        e_rows.append(ei)
        s_rows.append(jnp.sum(jnp.where(pick, s, 0.0), axis=0, keepdims=True))
        masked = jnp.where(pick, neg_inf, masked)
    top_s = jnp.concatenate(s_rows, axis=0)
    w_ref[...] = top_s / (jnp.sum(top_s, axis=0, keepdims=True) + 1e-20) * ROUTED_SCALE
    e_ref[...] = jnp.concatenate(e_rows, axis=0)

    sel = jnp.zeros((N_EXPERTS, tm), F32)
    for pick in picks:
        sel = sel + pick.astype(F32)
    before = (lax.broadcasted_iota(jnp.int32, (tm, tm), 0) < lax.broadcasted_iota(jnp.int32, (tm, tm), 1))
    pos = carry[...] + jnp.dot(sel.astype(BF16), before.astype(BF16), preferred_element_type=F32)
    r_ref[...] = jnp.concatenate(
        [jnp.sum(jnp.where(pick, pos, 0.0), axis=0, keepdims=True) for pick in picks], axis=0).astype(jnp.int32)
    total = carry[...] + jnp.sum(sel, axis=1, keepdims=True)
    carry[...] = total
    cnt_ref[...] = jnp.broadcast_to(total, cnt_ref.shape).astype(jnp.int32)


def _route(scores_t, e_bias, tm):
    Bn, _, S = scores_t.shape
    T = Bn * S
    nt = S // tm
    tok = pl.BlockSpec((TOP_K, tm), lambda b, i: (0, b * nt + i))
    return pl.pallas_call(
        functools.partial(_route_kernel, tm=tm),
        grid=(Bn, nt),
        in_specs=[pl.BlockSpec((1, N_EXPERTS, tm), lambda b, i: (b, 0, i)),
                  pl.BlockSpec((N_EXPERTS, 1), lambda b, i: (0, 0))],
        out_specs=[tok, tok, tok, pl.BlockSpec((N_EXPERTS, V7X_LANES), lambda b, i: (0, 0))],
        out_shape=[jax.ShapeDtypeStruct((TOP_K, T), jnp.int32), jax.ShapeDtypeStruct((TOP_K, T), F32),
                   jax.ShapeDtypeStruct((TOP_K, T), jnp.int32),
                   jax.ShapeDtypeStruct((N_EXPERTS, V7X_LANES), jnp.int32)],
        scratch_shapes=[pltpu.VMEM((N_EXPERTS, 1), F32)],
        compiler_params=_cparams("arbitrary", "arbitrary"),
        name="route",
    )(scores_t, e_bias.reshape(N_EXPERTS, 1))


def _dest_kernel(start_ref, e_ref, r_ref, o_ref):
    e = e_ref[...]
    acc = r_ref[...]
    for ex in range(N_EXPERTS):
        acc = acc + jnp.where(e == ex, start_ref[ex], 0)
    o_ref[0] = acc


def _dest_rows(pad_start, eidx, rank, tt):
    K_, T = eidx.shape
    grid_spec = pltpu.PrefetchScalarGridSpec(
        num_scalar_prefetch=1,
        grid=(T // tt,),
        in_specs=[pl.BlockSpec((K_, tt), lambda i, st: (0, i)), pl.BlockSpec((K_, tt), lambda i, st: (0, i))],
        out_specs=pl.BlockSpec((1, K_, tt), lambda i, st: (i, 0, 0)),
    )
    return pl.pallas_call(
        _dest_kernel,
        grid_spec=grid_spec,
        out_shape=jax.ShapeDtypeStruct((T // tt, K_, tt), jnp.int32),
        compiler_params=_cparams("arbitrary"),
        name="dest_rows",
    )(pad_start, eidx, rank)


def _dispatch_kernel(fill_ref, dest_hbm, h_ref, xs_hbm, dest_smem, zbuf, sem_idx, sem_rows, sem_zero):
    i = pl.program_id(0)
    EB = EXPERT_BLOCK

    def zero_copy(ex):
        return pltpu.make_async_copy(zbuf, xs_hbm.at[pl.ds(pl.multiple_of(fill_ref[ex], EB), EB), :], sem_zero)

    @pl.when(i == 0)
    def _():
        zbuf[...] = jnp.zeros_like(zbuf)

        def z_start(ex, c):
            @pl.when(fill_ref[ex] >= 0)
            def _():
                zero_copy(ex).start()
            return c

        def z_wait(ex, c):
            @pl.when(fill_ref[ex] >= 0)
            def _():
                zero_copy(ex).wait()
            return c

        lax.fori_loop(0, N_EXPERTS, z_start, 0)
        lax.fori_loop(0, N_EXPERTS, z_wait, 0)

    idx_copy = pltpu.make_async_copy(dest_hbm.at[i], dest_smem, sem_idx)
    idx_copy.start()
    idx_copy.wait()
    tt = h_ref.shape[0]

    def row_copy(t, k):
        return pltpu.make_async_copy(h_ref.at[pl.ds(t, 1), :], xs_hbm.at[pl.ds(dest_smem[k, t], 1), :], sem_rows)

    def issue(t, c):
        for k in range(TOP_K):
            row_copy(t, k).start()
        return c

    def drain(t, c):
        for k in range(TOP_K):
            row_copy(t, k).wait()
        return c

    lax.fori_loop(0, tt, issue, 0)
    lax.fori_loop(0, tt, drain, 0)


def _dispatch(fill_blocks, dest, h, n_rows):
    T, D = h.shape
    nt, K_, tt = dest.shape
    grid_spec = pltpu.PrefetchScalarGridSpec(
        num_scalar_prefetch=1,
        grid=(nt,),
        in_specs=[pl.BlockSpec(memory_space=pl.ANY), pl.BlockSpec((tt, D), lambda i, fl: (i, 0))],
        out_specs=pl.BlockSpec(memory_space=pl.ANY),
        scratch_shapes=[pltpu.SMEM((K_, tt), jnp.int32), pltpu.VMEM((EXPERT_BLOCK, D), h.dtype),
                        pltpu.SemaphoreType.DMA(()), pltpu.SemaphoreType.DMA(()), pltpu.SemaphoreType.DMA(())],
    )
    return pl.pallas_call(
        _dispatch_kernel,
        grid_spec=grid_spec,
        out_shape=jax.ShapeDtypeStruct((n_rows, D), h.dtype),
        compiler_params=_cparams("arbitrary"),
        name="dispatch",
    )(fill_blocks, dest, h)


def _expert_kernel(blk_e_ref, n_used_ref, x_ref, w1_ref, w3_ref, w2_ref, o_ref, w1b, w3b, w2b):
    i = pl.program_id(0)

    @pl.when((i == 0) | (blk_e_ref[i] != blk_e_ref[jnp.maximum(i - 1, 0)]))
    def _():
        w1b[...] = w1_ref[0].astype(BF16)
        w3b[...] = w3_ref[0].astype(BF16)
        w2b[...] = w2_ref[0].astype(BF16)

    @pl.when(i < n_used_ref[0])
    def _():
        x_lo, x_hi = _unpack_bf16_pair(x_ref[...])
        x_lo, x_hi = x_lo.astype(BF16), x_hi.astype(BF16)
        half = x_lo.shape[1]

        def up(wb):
            return (jnp.dot(x_lo, wb[:half, :], preferred_element_type=F32)
                    + jnp.dot(x_hi, wb[half:, :], preferred_element_type=F32))

        t = _silu(up(w1b)) * up(w3b)
        o_ref[...] = _pack_bf16_pair(jnp.dot(t.astype(BF16), w2b[...], preferred_element_type=F32))


def _experts(blk_e, n_used, xs, w1, w3, w2, layer):
    P, DP = xs.shape
    EB = EXPERT_BLOCK
    n_blocks = blk_e.shape[0]
    D, F = w1.shape[2], w1.shape[3]
    rows = pl.BlockSpec((EB, DP), lambda i, be, nu: (jnp.minimum(i, nu[0] - 1), 0))
    grid_spec = pltpu.PrefetchScalarGridSpec(
        num_scalar_prefetch=2,
        grid=(n_blocks,),
        in_specs=[
            rows,
            pl.BlockSpec((None, 1, D, F), lambda i, be, nu: (layer, be[i], 0, 0)),
            pl.BlockSpec((None, 1, D, F), lambda i, be, nu: (layer, be[i], 0, 0)),
            pl.BlockSpec((None, 1, F, D), lambda i, be, nu: (layer, be[i], 0, 0)),
        ],
        out_specs=rows,
        scratch_shapes=[pltpu.VMEM((D, F), BF16), pltpu.VMEM((D, F), BF16), pltpu.VMEM((F, D), BF16)],
    )
    return pl.pallas_call(
        _expert_kernel,
        grid_spec=grid_spec,
        out_shape=jax.ShapeDtypeStruct((P, DP), jnp.int32),
        compiler_params=_cparams("arbitrary"),
        name="experts",
    )(blk_e, n_used, xs, w1, w3, w2)


def _block_layout(counts, n_blocks):
    EB = EXPERT_BLOCK
    padded = (counts + EB - 1) // EB * EB
    pad_end = jnp.cumsum(padded)
    pad_start = pad_end - padded
    blk_row = (jnp.arange(n_blocks) * EB)[:, None]
    blk_e = jnp.minimum(jnp.sum((pad_end[None, :] <= blk_row).astype(jnp.int32), axis=1), N_EXPERTS - 1)
    n_used = (pad_end[-1] // EB).astype(jnp.int32).reshape(1)
    fill = jnp.where(padded > counts, pad_end - EB, -1).astype(jnp.int32)
    return pad_start.astype(jnp.int32), blk_e, n_used, fill


SC_GATHER_ROWS = 64


def _sc_gather_rows(table, idx):
    info = plsc.get_sparse_core_info()
    nc, ns = info.num_cores, info.num_subcores
    M = idx.shape[0]
    W = table.shape[1]
    b = SC_GATHER_ROWS
    per_worker = M // (nc * ns)
    steps = per_worker // b
    assert per_worker * nc * ns == M and steps * b == per_worker
    mesh = plsc.VectorSubcoreMesh(core_axis_name="c", subcore_axis_name="s")

    @functools.partial(
        pl.kernel, mesh=mesh,
        out_type=jax.ShapeDtypeStruct((M, W), table.dtype),
        scratch_types=[pltpu.VMEM((b,), jnp.int32), pltpu.VMEM((b, W), table.dtype), pltpu.SemaphoreType.DMA],
        name="sc_gather_rows",
    )
    def gather(table_hbm, idx_hbm, out_hbm, idx_v, rows_v, sem):
        wid = lax.axis_index("s") * nc + lax.axis_index("c")

        @pl.loop(0, steps)
        def _(s):
            base = pl.multiple_of(wid * per_worker + s * b, b)
            pltpu.sync_copy(idx_hbm.at[pl.ds(base, b)], idx_v)
            pltpu.async_copy(table_hbm.at[idx_v], rows_v, sem).wait()
            pltpu.sync_copy(rows_v, out_hbm.at[pl.ds(base, b)])

    return gather(table, idx)


def _combine_dense_kernel(rows_ref, w_ref, x_ref, shared_ref, gpost_ref, g2_ref, o_ref):
    w = w_ref[...]
    tt, half = rows_ref.shape[1], rows_ref.shape[2]
    y_lo = jnp.zeros((tt, half), F32)
    y_hi = jnp.zeros((tt, half), F32)
    for k in range(TOP_K):
        lo, hi = _unpack_bf16_pair(rows_ref[k])
        y_lo = y_lo + w[:, k:k + 1] * lo
        y_hi = y_hi + w[:, k:k + 1] * hi
    y = shared_ref[0] + jnp.concatenate([y_lo, y_hi], axis=1)
    o_ref[0] = x_ref[0] + g2_ref[0] * (_rms(y) * gpost_ref[...])


def _combine_dense(rows, w_tok, x, shared, gpost, g2, tt):
    Bn, S, D = x.shape
    K_, T, DP = rows.shape
    nt = S // tt
    seq = pl.BlockSpec((1, tt, D), lambda b, i: (b, i, 0))
    return pl.pallas_call(
        _combine_dense_kernel,
        grid=(Bn, nt),
        in_specs=[pl.BlockSpec((K_, tt, DP), lambda b, i: (0, b * nt + i, 0)),
                  pl.BlockSpec((tt, K_), lambda b, i: (b * nt + i, 0)), seq, seq,
                  pl.BlockSpec((1, D), lambda b, i: (0, 0)), pl.BlockSpec((1, 1, D), lambda b, i: (b, 0, 0))],
        out_specs=seq,
        out_shape=jax.ShapeDtypeStruct((Bn, S, D), F32),
        compiler_params=_cparams("arbitrary", "arbitrary"),
        name="combine_dense",
    )(rows, w_tok, x, shared, gpost.reshape(1, D), g2)


def kernel(x, c, w_ada, b_ada, norm_pre_mix, norm_post_mix, norm_pre_ffn, norm_post_ffn, w_in, w_out, rel_bias_table, diff_lambda, diff_subln, rwkv_mu, rwkv_w0, rwkv_w2, rwkv_a0, rwkv_a2, rwkv_g2, rwkv_k_k, rwkv_k_a, rwkv_r_k, rwkv_lnx_g, rwkv_lnx_b, gmlp_ln_g, gmlp_ln_b, gmlp_w_s, gmlp_b_s, router_w, router_bias, exp_w1, exp_w3, exp_w2, shared_w1, shared_w3, shared_w2):
    Bn, S, D = x.shape
    depth = w_ada.shape[0]
    tm = min(256, S)
    tq = min(256, S // 2)
    t_rwkv = min(512, S)

    mod = _adaln(c, w_ada, b_ada)
    band = _attn_band(rel_bias_table, tq)
    zpad = jnp.zeros((B_DECAY_LORA, B_WIDTH), F32)
    for l in range(depth):
        sh1, sc1, g1, sh2, sc2, g2 = [m.reshape(Bn, 1, D) for m in jnp.split(mod[l], 6, axis=-1)]
        w_in_b = w_in[l].astype(BF16)
        pa, pbc = _inproj(x, norm_pre_mix[l], sc1, sh1, w_in_b[:, :A_COLS], w_in_b[:, A_COLS:], tm)
        lambda_init = 0.8 - 0.6 * math.exp(-0.3 * l)
        ya = _diff_attention(pa, band, diff_lambda[l], diff_subln[l], lambda_init, tq)
        prep = _rwkv_prep(pbc, rwkv_mu[l], rwkv_w0[l], jnp.concatenate([rwkv_w2[l], zpad], axis=0),
                          rwkv_a0[l], jnp.concatenate([zpad, rwkv_a2[l]], axis=0), rwkv_g2[l],
                          rwkv_k_k[l], rwkv_k_a[l], rwkv_r_k[l].reshape(-1), t_rwkv)
        yb = _rwkv_scan(*prep, rwkv_lnx_g[l], rwkv_lnx_b[l], t_rwkv)
        yc = _gmlp(pbc, gmlp_ln_g[l], gmlp_ln_b[l], gmlp_w_s[l], gmlp_b_s[l], tm)

        w_out_b = w_out[l].astype(BF16)
        wr_t = jnp.pad(jnp.transpose(router_w[l]), ((0, V7X_LANES - N_EXPERTS), (0, 0)))
        x, h, scores_t, shared = _mid(
            ya, yb, yc, x, w_out_b[:A_WIDTH], w_out_b[A_WIDTH:A_WIDTH + B_WIDTH], w_out_b[A_WIDTH + B_WIDTH:],
            norm_post_mix[l], g1, norm_pre_ffn[l], sc2, sh2, wr_t,
            shared_w1[l].astype(BF16), shared_w3[l].astype(BF16), shared_w2[l].astype(BF16), tm)

        T = Bn * S
        n_blocks = -(-T * TOP_K // EXPERT_BLOCK) + N_EXPERTS
        eidx, wgt, rank, cnt = _route(scores_t, router_bias[l], tm)
        pad_start, blk_e, n_used, fill = _block_layout(cnt[:, 0], n_blocks)
        dest = _dest_rows(pad_start, eidx, rank, tm)
        xs = _dispatch(fill, dest, h.reshape(T, D // 2), n_blocks * EXPERT_BLOCK)
        ys = _experts(blk_e, n_used, xs, exp_w1, exp_w3, exp_w2, l)
        dest_kt = jnp.transpose(dest, (1, 0, 2)).reshape(TOP_K * T)
        rows = _sc_gather_rows(ys, dest_kt).reshape(TOP_K, T, D // 2)
        x = _combine_dense(rows, jnp.transpose(wgt), x, shared, norm_post_ffn[l], g2, tm)
    return x
```

```python
import functools
import math

import jax
import jax.numpy as jnp
from jax import lax
from jax.experimental import pallas as pl
from jax.experimental.pallas import tpu as pltpu
from jax.experimental.pallas import tpu_sc as plsc

F32 = jnp.float32
BF16 = jnp.bfloat16

A_HEADS = 4
A_QK_DIM = 64
A_HEAD_W = 2 * A_QK_DIM
A_WIDTH = A_HEADS * A_HEAD_W
N_BUCKETS = 32
MAX_DISTANCE = 128
B_HEADS = 4
B_HEAD_DIM = 64
B_WIDTH = B_HEADS * B_HEAD_DIM
B_DECAY_LORA = 64
B_AAA_LORA = 64
B_GATE_LORA = 128
B_LNX_EPS = 64e-5
C_GROUPS = 4
C_GROUP_DIM = 64
C_WIDTH = C_GROUPS * C_GROUP_DIM
CHUNK = 128
A_COLS = 3 * A_WIDTH
B_COLS = 3 * B_WIDTH + B_DECAY_LORA + B_AAA_LORA + B_GATE_LORA
C_COLS = 2 * C_WIDTH
N_EXPERTS = 64
TOP_K = 8
N_GROUPS = 8
TOPK_GROUPS = 4
EXPERTS_PER_GROUP = N_EXPERTS // N_GROUPS
ROUTED_SCALE = 2.5
EXPERT_BLOCK = 256
RMS_EPS = 1e-6
LN_EPS = 1e-5
NEG_BIG = -1e30

V7X_LANES = 128
V7X_VMEM_LIMIT_BYTES = 56 * 1024 * 1024
RWKV_CHUNK = 64
RWKV_GROUP = 8

NN = (((1,), (0,)), ((), ()))
NT = (((1,), (1,)), ((), ()))
TN = (((0,), (0,)), ((), ()))


def _cparams(*sem):
    return pltpu.CompilerParams(dimension_semantics=sem, vmem_limit_bytes=V7X_VMEM_LIMIT_BYTES)


def _mm(a, b, dims=NN):
    return lax.dot_general(a.astype(BF16), b.astype(BF16), dims, preferred_element_type=F32)


def _split(a):
    hi = a.astype(BF16)
    lo = (a - hi.astype(F32)).astype(BF16)
    return hi, lo


def _mm3(a, b, dims=NN):
    ah, al = _split(a)
    bh, bl = _split(b)
    d = lambda x, y: lax.dot_general(x, y, dims, preferred_element_type=F32)
    return d(ah, bh) + d(ah, bl) + d(al, bh)


def _mm2(a, b_exact, dims=NN):
    ah, al = _split(a)
    d = lambda x: lax.dot_general(x, b_exact, dims, preferred_element_type=F32)
    return d(ah) + d(al)


def _pack_bf16_pair(x):
    n = x.shape[1] // 2
    bits = lax.bitcast_convert_type(x.astype(BF16).astype(F32), jnp.int32)
    return ((bits[:, :n] >> 16) & 0xFFFF) | bits[:, n:]


def _unpack_bf16_pair(u):
    lo = lax.bitcast_convert_type(u << 16, F32)
    hi = lax.bitcast_convert_type(u & jnp.int32(-65536), F32)
    return lo, hi


def _rms(x, eps=RMS_EPS):
    return x * lax.rsqrt(jnp.mean(x * x, axis=-1, keepdims=True) + eps)


def _sigmoid(x):
    return 1.0 / (1.0 + jnp.exp(-x))


def _silu(x):
    return x * _sigmoid(x)


def _adaln_kernel(c_ref, w_ref, b_ref, o_ref):
    c = c_ref[...]
    o_ref[0] = _mm3(_silu(c), w_ref[0]) + b_ref[0]


def _adaln(c, w_ada, b_ada):
    L, D, N = w_ada.shape
    Bn = c.shape[0]
    tn = min(N, 1536)
    return pl.pallas_call(
        _adaln_kernel,
        grid=(L, N // tn),
        in_specs=[
            pl.BlockSpec((Bn, D), lambda l, j: (0, 0)),
            pl.BlockSpec((1, D, tn), lambda l, j: (l, 0, j)),
            pl.BlockSpec((1, 1, tn), lambda l, j: (l, 0, j)),
        ],
        out_specs=pl.BlockSpec((1, Bn, tn), lambda l, j: (l, 0, j)),
        out_shape=jax.ShapeDtypeStruct((L, Bn, N), F32),
        compiler_params=_cparams("arbitrary", "arbitrary"),
        name="adaln",
    )(c, w_ada, b_ada.reshape(L, 1, N))


def _inproj_kernel(x_ref, g_ref, sc_ref, sh_ref, wa_ref, wbc_ref, oa_ref, obc_ref):
    x = x_ref[0]
    h = _rms(x) * g_ref[...] * (1.0 + sc_ref[0]) + sh_ref[0]
    hb = h.astype(BF16)
    oa_ref[0] = jnp.dot(hb, wa_ref[...], preferred_element_type=F32).astype(BF16)
    obc_ref[0] = jnp.dot(hb, wbc_ref[...], preferred_element_type=F32)


def _inproj(x, g, sc, sh, wa, wbc, tm):
    Bn, S, D = x.shape
    na, nbc = wa.shape[1], wbc.shape[1]
    return pl.pallas_call(
        _inproj_kernel,
        grid=(Bn, S // tm),
        in_specs=[
            pl.BlockSpec((1, tm, D), lambda b, i: (b, i, 0)),
            pl.BlockSpec((1, D), lambda b, i: (0, 0)),
            pl.BlockSpec((1, 1, D), lambda b, i: (b, 0, 0)),
            pl.BlockSpec((1, 1, D), lambda b, i: (b, 0, 0)),
            pl.BlockSpec((D, na), lambda b, i: (0, 0)),
            pl.BlockSpec((D, nbc), lambda b, i: (0, 0)),
        ],
        out_specs=[
            pl.BlockSpec((1, tm, na), lambda b, i: (b, i, 0)),
            pl.BlockSpec((1, tm, nbc), lambda b, i: (b, i, 0)),
        ],
        out_shape=[
            jax.ShapeDtypeStruct((Bn, S, na), BF16),
            jax.ShapeDtypeStruct((Bn, S, nbc), F32),
        ],
        compiler_params=_cparams("arbitrary", "arbitrary"),
        name="inproj",
    )(x, g.reshape(1, D), sc, sh, wa, wbc)


def _t5_bucket(dist):
    n = jnp.maximum(dist, 0)
    max_exact = N_BUCKETS // 2
    nf = jnp.maximum(n, 1).astype(F32)
    large = max_exact + (jnp.log(nf / max_exact) / math.log(MAX_DISTANCE / max_exact)
                         * (N_BUCKETS - max_exact)).astype(jnp.int32)
    large = jnp.minimum(large, N_BUCKETS - 1)
    return jnp.where(n < max_exact, n, large)


def _attn_band(table, tq):
    far = table[N_BUCKETS - 1].astype(F32)
    L = 3 * tq
    m = jnp.arange(L)
    m = jnp.where(m < 2 * tq, m, m - L)
    bands = []
    for off in (0, tq):
        dist = off - m
        vals = jnp.where(dist[None] >= 0, jnp.transpose(table[_t5_bucket(dist)].astype(F32)) - far[:, None],
                         NEG_BIG)
        toe = jnp.tile(vals, (1, tq))[:, :tq * (L - 1)].reshape(-1, tq, L - 1)
        bands.append(toe[:, :, :2 * tq])
    band = jnp.stack(bands)
    return jnp.concatenate([band, band], axis=2)


def _attn_kernel(lam_ref, q_ref, k_ref, v_ref, band_ref, g_ref, o_ref, *, tq, lambda_init):
    i = pl.program_id(2)
    q = q_ref[0] * jnp.asarray(A_QK_DIM ** -0.5, BF16)
    lane = lax.broadcasted_iota(jnp.int32, q.shape, 1)
    zero = jnp.zeros_like(q)
    qq = jnp.concatenate([jnp.where(lane < A_QK_DIM, q, zero),
                          jnp.where(lane >= A_QK_DIM, q, zero)], axis=0)

    kb0 = pl.multiple_of(jnp.maximum(i - 1, 0) * tq, tq)
    kb = k_ref[0, pl.ds(kb0, 2 * tq), :]
    vb = v_ref[0, pl.ds(kb0, 2 * tq), :]
    s = lax.dot_general(qq, kb, NT, preferred_element_type=F32) + band_ref[0, 0]
    m = jnp.max(s, axis=-1, keepdims=True)
    p = jnp.exp(s - m)
    l = jnp.sum(p, axis=-1, keepdims=True)
    acc = jnp.dot(p.astype(BF16), vb, preferred_element_type=F32)

    n_far = jnp.maximum(i - 1, 0)

    def logits(j):
        return lax.dot_general(qq, k_ref[0, pl.ds(pl.multiple_of(j * tq, tq), tq), :], NT,
                               preferred_element_type=F32)

    def body(j, carry):
        m, l, acc, s = carry
        s_next = logits(jnp.minimum(j + 1, n_far - 1))
        vj = v_ref[0, pl.ds(pl.multiple_of(j * tq, tq), tq), :]
        m_new = jnp.maximum(m, jnp.max(s, axis=-1, keepdims=True))
        alpha = jnp.exp(m - m_new)
        p = jnp.exp(s - m_new)
        l = alpha * l + jnp.sum(p, axis=-1, keepdims=True)
        acc = alpha * acc + jnp.dot(p.astype(BF16), vj, preferred_element_type=F32)
        return m_new, l, acc, s_next

    m, l, acc, _ = lax.fori_loop(0, n_far, body, (m, l, acc, logits(0)))

    lp = lam_ref[...]
    lam = (jnp.exp(jnp.sum(lp[0:1] * lp[1:2], axis=-1, keepdims=True))
           - jnp.exp(jnp.sum(lp[2:3] * lp[3:4], axis=-1, keepdims=True)) + lambda_init)
    o = acc / l
    o = o[:tq] - lam * o[tq:]
    o_ref[0] = _rms(o) * g_ref[...] * (1.0 - lambda_init)


def _diff_attention(pa, band, lam_par, subln_g, lambda_init, tq):
    Bn, S, _ = pa.shape
    W = A_HEAD_W
    kern = functools.partial(_attn_kernel, tq=tq, lambda_init=lambda_init)
    return pl.pallas_call(
        kern,
        grid=(Bn, A_HEADS, S // tq),
        in_specs=[
            pl.BlockSpec((4, A_QK_DIM), lambda b, h, i: (0, 0)),
            pl.BlockSpec((1, tq, W), lambda b, h, i: (b, i, h)),
            pl.BlockSpec((1, S, W), lambda b, h, i: (b, 0, A_HEADS + h)),
            pl.BlockSpec((1, S, W), lambda b, h, i: (b, 0, 2 * A_HEADS + h)),
            pl.BlockSpec((1, 1, 2 * tq, 2 * tq), lambda b, h, i: (jnp.minimum(i, 1), h, 0, 0)),
            pl.BlockSpec((1, W), lambda b, h, i: (0, 0)),
        ],
        out_specs=pl.BlockSpec((1, tq, W), lambda b, h, i: (b, i, h)),
        out_shape=jax.ShapeDtypeStruct((Bn, S, A_WIDTH), F32),
        compiler_params=_cparams("arbitrary", "arbitrary", "arbitrary"),
        name="diff_attn",
    )(lam_par, pa, pa, pa, band, subln_g.reshape(1, W))


def _head_ones(n):
    r = lax.broadcasted_iota(jnp.int32, (n, n), 0) // B_HEAD_DIM
    c = lax.broadcasted_iota(jnp.int32, (n, n), 1) // B_HEAD_DIM
    return (r == c).astype(BF16)


def _rwkv_prep_kernel(pb_ref, prev_ref, mu_ref, w0_ref, w2_ref, a0_ref, a2_ref, g2_ref,
                      kk_ref, ka_ref, rk_ref,
                      rt_ref, at_ref, kt_ref, bt_ref, v_ref, wc_ref, bonus_ref, g_ref, *, tm):
    i = pl.program_id(1)
    C = RWKV_CHUNK
    x = pb_ref[0]
    row = lax.broadcasted_iota(jnp.int32, x.shape, 0)
    last = prev_ref[0, 7:8, :] * (i > 0).astype(F32)
    prev = jnp.where(row == 0, last, pltpu.roll(x, 1, 0))
    p = x + (prev - x) * mu_ref[...]
    o1, o2, o3 = B_WIDTH, 2 * B_WIDTH, 3 * B_WIDTH
    r, k, v = p[:, :o1], p[:, o1:o2], p[:, o2:o3]
    lora = p[:, o3:o3 + B_DECAY_LORA + B_AAA_LORA]
    gd = p[:, o3 + B_DECAY_LORA + B_AAA_LORA:]

    z = -(w0_ref[...] + _mm3(jnp.tanh(lora), w2_ref[...]))
    softplus = jnp.maximum(z, 0.0) + jnp.log(1.0 + jnp.exp(-jnp.abs(z)))
    logw = -jnp.exp(-softplus - 0.5)
    a = _sigmoid(a0_ref[...] + _mm3(lora, a2_ref[...]))
    g_ref[0] = _mm3(_sigmoid(gd), g2_ref[...])

    ones = _head_ones(B_WIDTH)
    kk = k * kk_ref[...]
    kk = kk * lax.rsqrt(jnp.maximum(_mm2(kk * kk, ones), 1e-24))
    k2 = k * (1.0 + (a - 1.0) * ka_ref[...])
    bonus_ref[0] = _mm2(r * k2 * rk_ref[...], ones) * v

    t_in = lax.broadcasted_iota(jnp.int32, (tm, B_WIDTH), 0) % C
    cum = logw
    sh = 1
    while sh < C:
        cum = cum + jnp.where(t_in >= sh, pltpu.roll(cum, sh, 0), 0.0)
        sh *= 2
    n = tm // C
    wc_ref[0] = jnp.exp(jnp.sum(logw.reshape(n, C, B_WIDTH), axis=1))
    e_pos = jnp.exp(cum)
    e_neg = jnp.exp(-cum)
    rt_ref[0] = r * e_pos
    at_ref[0] = -kk * jnp.exp(cum - logw)
    kt_ref[0] = k2 * e_neg
    bt_ref[0] = kk * a * e_neg
    v_ref[0] = v


def _rwkv_prep(pbc, mu, w0, w2p, a0, a2p, g2, k_k, k_a, r_k, tm):
    Bn, S, _ = pbc.shape
    W = B_WIDTH
    nl = B_DECAY_LORA + B_AAA_LORA
    row = lambda a: a.reshape(1, -1)
    full = lambda shp: pl.BlockSpec(shp, lambda b, i: (0,) * len(shp))
    seq = pl.BlockSpec((1, tm, W), lambda b, i: (b, i, 0))
    seq_shape = jax.ShapeDtypeStruct((Bn, S, W), F32)
    n = tm // RWKV_CHUNK
    return pl.pallas_call(
        functools.partial(_rwkv_prep_kernel, tm=tm),
        grid=(Bn, S // tm),
        in_specs=[
            pl.BlockSpec((1, tm, B_COLS), lambda b, i: (b, i, 0)),
            pl.BlockSpec((1, 8, B_COLS), lambda b, i: (b, jnp.maximum(i * (tm // 8) - 1, 0), 0)),
            full((1, B_COLS)), full((1, W)), full((nl, W)), full((1, W)), full((nl, W)),
            full((B_GATE_LORA, W)), full((1, W)), full((1, W)), full((1, W)),
        ],
        out_specs=[seq, seq, seq, seq, seq,
                   pl.BlockSpec((1, n, W), lambda b, i: (b, i, 0)), seq, seq],
        out_shape=[seq_shape] * 5 + [jax.ShapeDtypeStruct((Bn, S // RWKV_CHUNK, W), F32)] + [seq_shape] * 2,
        compiler_params=_cparams("arbitrary", "arbitrary"),
        name="rwkv_prep",
    )(pbc, pbc, row(mu), row(w0), w2p, row(a0), a2p, g2, row(k_k), row(k_a), row(r_k))


def _rwkv_scan_kernel(rt_ref, at_ref, kt_ref, bt_ref, v_ref, wc_ref, bonus_ref, g_ref,
                      lng_ref, lnb_ref, o_ref, state, *, tt):
    C = RWKV_CHUNK
    W = B_WIDTH

    @pl.when(pl.program_id(1) == 0)
    def _():
        state[...] = jnp.zeros_like(state)

    lane_head = lax.broadcasted_iota(jnp.int32, (C, W), 1) // B_HEAD_DIM
    tt_i = lax.broadcasted_iota(jnp.int32, (C, W), 0)
    ss_i = lax.broadcasted_iota(jnp.int32, (C, W), 1) % C
    strict = tt_i > ss_i
    incl = tt_i >= ss_i
    eye = (tt_i == ss_i).astype(F32)
    ones = _head_ones(W)
    bd_mask = ones.astype(F32)

    head_mask = [(lane_head == h).astype(BF16) for h in range(B_HEADS)]

    def bd_split(x):
        xb = x.astype(BF16)
        return jnp.concatenate([xb * mk for mk in head_mask], axis=0)

    def mm_bd(a, b_bd, dims=NN):
        return lax.dot_general(a.astype(BF16), b_bd, dims, preferred_element_type=F32)

    def state_free(gi):
        G = range(RWKV_GROUP)
        sls = [pl.ds(pl.multiple_of((gi * RWKV_GROUP + j) * C, C), C) for j in G]
        rt = [rt_ref[0, sl, :] for sl in sls]
        at = [at_ref[0, sl, :] for sl in sls]
        kt = [kt_ref[0, sl, :] for sl in sls]
        bt = [bt_ref[0, sl, :] for sl in sls]
        v = [v_ref[0, sl, :] for sl in sls]
        wc = [wc_ref[0, pl.ds(gi * RWKV_GROUP + j, 1), :] for j in G]
        ar = [jnp.concatenate([at[j], rt[j]], axis=0) for j in G]
        bdb = [bd_split(bt[j]) for j in G]
        bdk = [bd_split(kt[j]) for j in G]
        a_b = [mm_bd(ar[j], bdb[j], NT) for j in G]
        a_k = [mm_bd(ar[j], bdk[j], NT) for j in G]
        lo = [jnp.where(strict, a_b[j][:C], 0.0) for j in G]
        a_ak = [jnp.where(strict, a_k[j][:C], 0.0) for j in G]
        a_rb = [jnp.where(incl, a_b[j][C:], 0.0) for j in G]
        a_rk = [jnp.where(incl, a_k[j][C:], 0.0) for j in G]
        pw = lo
        tinv = [eye + lo[j] for j in G]
        bdp = [bd_split(pw[j]) for j in G]
        span = 2
        while span < C:
            pw = [mm_bd(pw[j], bdp[j]) for j in G]
            bdp = [bd_split(pw[j]) for j in G]
            tinv = [tinv[j] + mm_bd(tinv[j], bdp[j]) for j in G]
            span *= 2
        bdv = [bd_split(v[j]) for j in G]
        bda = [bd_split(at[j]) for j in G]
        abar = [mm_bd(tinv[j], bda[j]) for j in G]
        akv = [bd_split(mm_bd(a_ak[j], bdv[j])) for j in G]
        u0 = [mm_bd(tinv[j], akv[j]) for j in G]
        y0 = [mm_bd(a_rk[j], bdv[j]) for j in G]
        kv = [_mm(v[j], kt[j] * wc[j], TN) * bd_mask for j in G]
        return [(jnp.concatenate([abar[j], rt[j]], axis=0), u0[j], y0[j], a_rb[j], bt[j] * wc[j], kv[j], wc[j])
                for j in G]

    def group(gi, carry):
        pre = state_free(gi)
        s = state[...]
        ys = []
        for abar_rt, u0, y0, a_rb, btw, kv, wc in pre:
            a_s = _mm(abar_rt, s, NT)
            u = a_s[:C] + u0
            ys.append(a_s[C:] + y0 + mm_bd(a_rb, bd_split(u)))
            s = s * wc + _mm(u, btw, TN) * bd_mask + kv
        state[...] = s
        y = jnp.concatenate(ys, axis=0)
        sl = pl.ds(pl.multiple_of(gi * (RWKV_GROUP * C), RWKV_GROUP * C), RWKV_GROUP * C)
        mean = _mm2(y, ones) * (1.0 / B_HEAD_DIM)
        d = y - mean
        var = _mm2(d * d, ones) * (1.0 / B_HEAD_DIM)
        yn = d * lax.rsqrt(var + B_LNX_EPS) * lng_ref[...] + lnb_ref[...]
        o_ref[0, sl, :] = (yn + bonus_ref[0, sl, :]) * g_ref[0, sl, :]
        return carry

    lax.fori_loop(0, tt // (RWKV_GROUP * C), group, 0)


def _rwkv_scan(rt, at, kt, bt, v, wc, bonus, g, lnx_g, lnx_b, tt):
    Bn, S, W = rt.shape
    n = tt // RWKV_CHUNK
    seq = pl.BlockSpec((1, tt, W), lambda b, i: (b, i, 0))
    vec = pl.BlockSpec((1, W), lambda b, i: (0, 0))
    return pl.pallas_call(
        functools.partial(_rwkv_scan_kernel, tt=tt),
        grid=(Bn, S // tt),
        in_specs=[seq, seq, seq, seq, seq, pl.BlockSpec((1, n, W), lambda b, i: (b, i, 0)), seq, seq, vec, vec],
        out_specs=seq,
        out_shape=jax.ShapeDtypeStruct((Bn, S, W), F32),
        scratch_shapes=[pltpu.VMEM((B_HEADS * B_HEAD_DIM, W), F32)],
        compiler_params=_cparams("arbitrary", "arbitrary"),
        name="rwkv_scan",
    )(rt, at, kt, bt, v, wc, bonus, g, lnx_g.reshape(1, W), lnx_b.reshape(1, W))


def _gmlp_kernel(pc_ref, lng_ref, lnb_ref, ws_ref, bs_ref, o_ref, *, tm):
    x = pc_ref[0]
    z = x * (0.5 * (1.0 + jnp.tanh(math.sqrt(2.0 / math.pi) * (x + 0.044715 * (x * x * x)))))
    u, v = z[:, :C_WIDTH], z[:, C_WIDTH:]
    mu = jnp.mean(v, axis=-1, keepdims=True)
    d = v - mu
    var = jnp.mean(d * d, axis=-1, keepdims=True)
    vn = d * lax.rsqrt(var + LN_EPS) * lng_ref[...] + lnb_ref[...]
    group = lax.broadcasted_iota(jnp.int32, (CHUNK, C_WIDTH), 1) // C_GROUP_DIM
    tril = (lax.broadcasted_iota(jnp.int32, (CHUNK, CHUNK), 0)
            >= lax.broadcasted_iota(jnp.int32, (CHUNK, CHUNK), 1))
    ws = [jnp.where(tril, ws_ref[gi], 0.0).astype(BF16) for gi in range(C_GROUPS)]
    for c in range(tm // CHUNK):
        sl = slice(c * CHUNK, (c + 1) * CHUNK)
        vc = vn[sl].astype(BF16)
        sv = bs_ref[...]
        for gi in range(C_GROUPS):
            t = jnp.dot(ws[gi], vc, preferred_element_type=F32)
            sv = sv + jnp.where(group == gi, t, 0.0)
        o_ref[0, sl, :] = u[sl] * sv


def _gmlp(pbc, ln_g, ln_b, w_s, b_s, tm):
    Bn, S, _ = pbc.shape
    bs_wide = jnp.repeat(jnp.transpose(b_s), C_GROUP_DIM, axis=1)
    return pl.pallas_call(
        functools.partial(_gmlp_kernel, tm=tm),
        grid=(Bn, S // tm),
        in_specs=[
            pl.BlockSpec((1, tm, C_COLS), lambda b, i: (b, i, B_COLS // C_COLS)),
            pl.BlockSpec((1, C_WIDTH), lambda b, i: (0, 0)),
            pl.BlockSpec((1, C_WIDTH), lambda b, i: (0, 0)),
            pl.BlockSpec((C_GROUPS, CHUNK, CHUNK), lambda b, i: (0, 0, 0)),
            pl.BlockSpec((CHUNK, C_WIDTH), lambda b, i: (0, 0)),
        ],
        out_specs=pl.BlockSpec((1, tm, C_WIDTH), lambda b, i: (b, i, 0)),
        out_shape=jax.ShapeDtypeStruct((Bn, S, C_WIDTH), F32),
        compiler_params=_cparams("arbitrary", "arbitrary"),
        name="gmlp",
    )(pbc, ln_g.reshape(1, -1), ln_b.reshape(1, -1), w_s, bs_wide)


def _mid_kernel(ya_ref, yb_ref, yc_ref, x_ref, woa_ref, wob_ref, woc_ref, gpost_ref, g1_ref,
                gpre_ref, sc_ref, sh_ref, wr_ref, ws1_ref, ws3_ref, ws2_ref,
                xo_ref, h_ref, score_ref, shared_ref):
    y = (_mm(ya_ref[0], woa_ref[...]) + _mm(yb_ref[0], wob_ref[...]) + _mm(yc_ref[0], woc_ref[...]))
    xn = x_ref[0] + g1_ref[0] * (_rms(y) * gpost_ref[...])
    xo_ref[0] = xn
    h = _rms(xn) * gpre_ref[...] * (1.0 + sc_ref[0]) + sh_ref[0]
    h_ref[0] = _pack_bf16_pair(h)
    score_ref[0] = _sigmoid(_mm3(wr_ref[...], h, NT))
    hb = h.astype(BF16)
    t = _silu(jnp.dot(hb, ws1_ref[...], preferred_element_type=F32)) * jnp.dot(
        hb, ws3_ref[...], preferred_element_type=F32)
    shared_ref[0] = jnp.dot(t.astype(BF16), ws2_ref[...], preferred_element_type=F32)


def _mid(ya, yb, yc, x, woa, wob, woc, gpost, g1, gpre, sc, sh, wr, ws1, ws3, ws2, tm):
    Bn, S, D = x.shape
    NR = wr.shape[0]
    F = ws1.shape[1]
    seq = lambda w: pl.BlockSpec((1, tm, w), lambda b, i: (b, i, 0))
    full = lambda shp: pl.BlockSpec(shp, lambda b, i: (0,) * len(shp))
    per_b = pl.BlockSpec((1, 1, D), lambda b, i: (b, 0, 0))
    return pl.pallas_call(
        _mid_kernel,
        grid=(Bn, S // tm),
        in_specs=[seq(A_WIDTH), seq(B_WIDTH), seq(C_WIDTH), seq(D),
                  full((A_WIDTH, D)), full((B_WIDTH, D)), full((C_WIDTH, D)),
                  full((1, D)), per_b, full((1, D)), per_b, per_b,
                  full((NR, D)), full((D, F)), full((D, F)), full((F, D))],
        out_specs=[seq(D), seq(D // 2), pl.BlockSpec((1, NR, tm), lambda b, i: (b, 0, i)), seq(D)],
        out_shape=[jax.ShapeDtypeStruct((Bn, S, D), F32), jax.ShapeDtypeStruct((Bn, S, D // 2), jnp.int32),
                   jax.ShapeDtypeStruct((Bn, NR, S), F32), jax.ShapeDtypeStruct((Bn, S, D), F32)],
        compiler_params=_cparams("arbitrary", "arbitrary"),
        name="mid",
    )(ya, yb, yc, x, woa, wob, woc, gpost.reshape(1, D), g1, gpre.reshape(1, D), sc, sh, wr, ws1, ws3, ws2)


def _first_argmax(vals, iota, n):
    m = jnp.max(vals, axis=0, keepdims=True)
    idx = jnp.min(jnp.where(vals == m, iota, n), axis=0, keepdims=True)
    return m, idx


def _route_kernel(sc_ref, bias_ref, e_ref, w_ref, r_ref, cnt_ref, carry, *, tm):
    @pl.when((pl.program_id(0) == 0) & (pl.program_id(1) == 0))
    def _():
        carry[...] = jnp.zeros_like(carry)

    G = EXPERTS_PER_GROUP
    s = sc_ref[0]
    biased = s + bias_ref[...]
    neg_inf = jnp.float32(-jnp.inf)
    io8 = lax.broadcasted_iota(jnp.int32, (G, tm), 0)
    gs_rows = []
    for g in range(N_GROUPS):
        blk = biased[g * G:(g + 1) * G]
        m1, i1 = _first_argmax(blk, io8, G)
        m2 = jnp.max(jnp.where(io8 == i1, neg_inf, blk), axis=0, keepdims=True)
        gs_rows.append(m1 + m2)
    gs = jnp.concatenate(gs_rows, axis=0)
    gio = lax.broadcasted_iota(jnp.int32, (N_GROUPS, tm), 0)
    gsel = jnp.zeros((N_GROUPS, tm), jnp.bool_)
    for _ in range(TOPK_GROUPS):
        _, gi = _first_argmax(gs, gio, N_GROUPS)
        pick = gio == gi
        gsel = gsel | pick
        gs = jnp.where(pick, neg_inf, gs)
    masked = jnp.concatenate(
        [jnp.where(gsel[g:g + 1], biased[g * G:(g + 1) * G], neg_inf) for g in range(N_GROUPS)], axis=0)

    eio = lax.broadcasted_iota(jnp.int32, (N_EXPERTS, tm), 0)
    picks, e_rows, s_rows = [], [], []
    for _ in range(TOP_K):
        _, ei = _first_argmax(masked, eio, N_EXPERTS)
        pick = eio == ei
        picks.append(pick)
        e_rows.append(ei)
        s_rows.append(jnp.sum(jnp.where(pick, s, 0.0), axis=0, keepdims=True))
        masked = jnp.where(pick, neg_inf, masked)
    top_s = jnp.concatenate(s_rows, axis=0)
    w_ref[...] = top_s / (jnp.sum(top_s, axis=0, keepdims=True) + 1e-20) * ROUTED_SCALE
    e_ref[...] = jnp.concatenate(e_rows, axis=0)

    sel = jnp.zeros((N_EXPERTS, tm), F32)
    for pick in picks:
        sel = sel + pick.astype(F32)
    before = (lax.broadcasted_iota(jnp.int32, (tm, tm), 0) < lax.broadcasted_iota(jnp.int32, (tm, tm), 1))
    pos = carry[...] + jnp.dot(sel.astype(BF16), before.astype(BF16), preferred_element_type=F32)
    r_ref[...] = jnp.concatenate(
        [jnp.sum(jnp.where(pick, pos, 0.0), axis=0, keepdims=True) for pick in picks], axis=0).astype(jnp.int32)
    total = carry[...] + jnp.sum(sel, axis=1, keepdims=True)
    carry[...] = total
    cnt_ref[...] = jnp.broadcast_to(total, cnt_ref.shape).astype(jnp.int32)


def _route(scores_t, e_bias, tm):
    Bn, _, S = scores_t.shape
    T = Bn * S
    nt = S // tm
    tok = pl.BlockSpec((TOP_K, tm), lambda b, i: (0, b * nt + i))
    return pl.pallas_call(
        functools.partial(_route_kernel, tm=tm),
        grid=(Bn, nt),
        in_specs=[pl.BlockSpec((1, N_EXPERTS, tm), lambda b, i: (b, 0, i)),
                  pl.BlockSpec((N_EXPERTS, 1), lambda b, i: (0, 0))],
        out_specs=[tok, tok, tok, pl.BlockSpec((N_EXPERTS, V7X_LANES), lambda b, i: (0, 0))],
        out_shape=[jax.ShapeDtypeStruct((TOP_K, T), jnp.int32), jax.ShapeDtypeStruct((TOP_K, T), F32),
                   jax.ShapeDtypeStruct((TOP_K, T), jnp.int32),
                   jax.ShapeDtypeStruct((N_EXPERTS, V7X_LANES), jnp.int32)],
        scratch_shapes=[pltpu.VMEM((N_EXPERTS, 1), F32)],
        compiler_params=_cparams("arbitrary", "arbitrary"),
        name="route",
    )(scores_t, e_bias.reshape(N_EXPERTS, 1))


def _dest_kernel(start_ref, e_ref, r_ref, o_ref):
    e = e_ref[...]
    acc = r_ref[...]
    for ex in range(N_EXPERTS):
        acc = acc + jnp.where(e == ex, start_ref[ex], 0)
    o_ref[0] = acc


def _dest_rows(pad_start, eidx, rank, tt):
    K_, T = eidx.shape
    grid_spec = pltpu.PrefetchScalarGridSpec(
        num_scalar_prefetch=1,
        grid=(T // tt,),
        in_specs=[pl.BlockSpec((K_, tt), lambda i, st: (0, i)), pl.BlockSpec((K_, tt), lambda i, st: (0, i))],
        out_specs=pl.BlockSpec((1, K_, tt), lambda i, st: (i, 0, 0)),
    )
    return pl.pallas_call(
        _dest_kernel,
        grid_spec=grid_spec,
        out_shape=jax.ShapeDtypeStruct((T // tt, K_, tt), jnp.int32),
        compiler_params=_cparams("arbitrary"),
        name="dest_rows",
    )(pad_start, eidx, rank)


def _dispatch_kernel(fill_ref, dest_hbm, h_ref, xs_hbm, dest_smem, zbuf, sem_idx, sem_rows, sem_zero):
    i = pl.program_id(0)
    EB = EXPERT_BLOCK

    def zero_copy(ex):
        return pltpu.make_async_copy(zbuf, xs_hbm.at[pl.ds(pl.multiple_of(fill_ref[ex], EB), EB), :], sem_zero)

    @pl.when(i == 0)
    def _():
        zbuf[...] = jnp.zeros_like(zbuf)

        def z_start(ex, c):
            @pl.when(fill_ref[ex] >= 0)
            def _():
                zero_copy(ex).start()
            return c

        def z_wait(ex, c):
            @pl.when(fill_ref[ex] >= 0)
            def _():
                zero_copy(ex).wait()
            return c

        lax.fori_loop(0, N_EXPERTS, z_start, 0)
        lax.fori_loop(0, N_EXPERTS, z_wait, 0)

    idx_copy = pltpu.make_async_copy(dest_hbm.at[i], dest_smem, sem_idx)
    idx_copy.start()
    idx_copy.wait()
    tt = h_ref.shape[0]

    def row_copy(t, k):
        return pltpu.make_async_copy(h_ref.at[pl.ds(t, 1), :], xs_hbm.at[pl.ds(dest_smem[k, t], 1), :], sem_rows)

    def issue(t, c):
        for k in range(TOP_K):
            row_copy(t, k).start()
        return c

    def drain(t, c):
        for k in range(TOP_K):
            row_copy(t, k).wait()
        return c

    lax.fori_loop(0, tt, issue, 0)
    lax.fori_loop(0, tt, drain, 0)


def _dispatch(fill_blocks, dest, h, n_rows):
    T, D = h.shape
    nt, K_, tt = dest.shape
    grid_spec = pltpu.PrefetchScalarGridSpec(
        num_scalar_prefetch=1,
        grid=(nt,),
        in_specs=[pl.BlockSpec(memory_space=pl.ANY), pl.BlockSpec((tt, D), lambda i, fl: (i, 0))],
        out_specs=pl.BlockSpec(memory_space=pl.ANY),
        scratch_shapes=[pltpu.SMEM((K_, tt), jnp.int32), pltpu.VMEM((EXPERT_BLOCK, D), h.dtype),
                        pltpu.SemaphoreType.DMA(()), pltpu.SemaphoreType.DMA(()), pltpu.SemaphoreType.DMA(())],
    )
    return pl.pallas_call(
        _dispatch_kernel,
        grid_spec=grid_spec,
        out_shape=jax.ShapeDtypeStruct((n_rows, D), h.dtype),
        compiler_params=_cparams("arbitrary"),
        name="dispatch",
    )(fill_blocks, dest, h)


def _expert_kernel(blk_e_ref, n_used_ref, n_valid_ref, x_ref, w1_ref, w3_ref, w2_ref, o_ref, w1b, w3b, w2b):
    i = pl.program_id(0)

    @pl.when((i == 0) | (blk_e_ref[i] != blk_e_ref[jnp.maximum(i - 1, 0)]))
    def _():
        w1b[...] = w1_ref[0].astype(BF16)
        w3b[...] = w3_ref[0].astype(BF16)
        w2b[...] = w2_ref[0].astype(BF16)

    @pl.when(i < n_used_ref[0])
    def _():
        row = lax.broadcasted_iota(jnp.int32, x_ref.shape, 0)
        x_lo, x_hi = _unpack_bf16_pair(jnp.where(row < n_valid_ref[i], x_ref[...], 0))
        x_lo, x_hi = x_lo.astype(BF16), x_hi.astype(BF16)
        half = x_lo.shape[1]

        def up(wb):
            return (jnp.dot(x_lo, wb[:half, :], preferred_element_type=F32)
                    + jnp.dot(x_hi, wb[half:, :], preferred_element_type=F32))

        t = _silu(up(w1b)) * up(w3b)
        o_ref[...] = _pack_bf16_pair(jnp.dot(t.astype(BF16), w2b[...], preferred_element_type=F32))


def _experts(blk_e, n_used, n_valid, xs, w1, w3, w2, layer):
    P, DP = xs.shape
    EB = EXPERT_BLOCK
    n_blocks = blk_e.shape[0]
    D, F = w1.shape[2], w1.shape[3]
    rows = pl.BlockSpec((EB, DP), lambda i, be, nu, nv: (jnp.minimum(i, nu[0] - 1), 0))
    grid_spec = pltpu.PrefetchScalarGridSpec(
        num_scalar_prefetch=3,
        grid=(n_blocks,),
        in_specs=[
            rows,
            pl.BlockSpec((None, 1, D, F), lambda i, be, nu, nv: (layer, be[i], 0, 0)),
            pl.BlockSpec((None, 1, D, F), lambda i, be, nu, nv: (layer, be[i], 0, 0)),
            pl.BlockSpec((None, 1, F, D), lambda i, be, nu, nv: (layer, be[i], 0, 0)),
        ],
        out_specs=rows,
        scratch_shapes=[pltpu.VMEM((D, F), BF16), pltpu.VMEM((D, F), BF16), pltpu.VMEM((F, D), BF16)],
    )
    return pl.pallas_call(
        _expert_kernel,
        grid_spec=grid_spec,
        out_shape=jax.ShapeDtypeStruct((P, DP), jnp.int32),
        compiler_params=_cparams("arbitrary"),
        name="experts",
    )(blk_e, n_used, n_valid, xs, w1, w3, w2)


def _block_layout(counts, n_blocks):
    EB = EXPERT_BLOCK
    padded = (counts + EB - 1) // EB * EB
    pad_end = jnp.cumsum(padded)
    pad_start = pad_end - padded
    blk_row = (jnp.arange(n_blocks) * EB)[:, None]
    blk_e = jnp.minimum(jnp.sum((pad_end[None, :] <= blk_row).astype(jnp.int32), axis=1), N_EXPERTS - 1)
    n_used = (pad_end[-1] // EB).astype(jnp.int32).reshape(1)
    n_valid = jnp.clip(counts[blk_e] - (blk_row[:, 0] - pad_start[blk_e]), 0, EB).astype(jnp.int32)
    return pad_start.astype(jnp.int32), blk_e, n_used, n_valid


SC_GATHER_ROWS = 64


def _sc_gather_rows(table, idx):
    info = plsc.get_sparse_core_info()
    nc, ns = info.num_cores, info.num_subcores
    M = idx.shape[0]
    W = table.shape[1]
    b = SC_GATHER_ROWS
    per_worker = M // (nc * ns)
    steps = per_worker // b
    assert per_worker * nc * ns == M and steps * b == per_worker
    mesh = plsc.VectorSubcoreMesh(core_axis_name="c", subcore_axis_name="s")

    @functools.partial(
        pl.kernel, mesh=mesh,
        out_type=jax.ShapeDtypeStruct((M, W), table.dtype),
        scratch_types=[pltpu.VMEM((b,), jnp.int32), pltpu.VMEM((b, W), table.dtype), pltpu.SemaphoreType.DMA],
        name="sc_gather_rows",
    )
    def gather(table_hbm, idx_hbm, out_hbm, idx_v, rows_v, sem):
        wid = lax.axis_index("s") * nc + lax.axis_index("c")

        @pl.loop(0, steps)
        def _(s):
            base = pl.multiple_of(wid * per_worker + s * b, b)
            pltpu.sync_copy(idx_hbm.at[pl.ds(base, b)], idx_v)
            pltpu.async_copy(table_hbm.at[idx_v], rows_v, sem).wait()
            pltpu.sync_copy(rows_v, out_hbm.at[pl.ds(base, b)])

    return gather(table, idx)


def _sc_scatter_rows(rows, idx, n_out):
    info = plsc.get_sparse_core_info()
    nc, ns = info.num_cores, info.num_subcores
    T, W = rows.shape
    G, K_, b = idx.shape
    steps = G // (nc * ns)
    assert steps * nc * ns == G and G * b == T
    mesh = plsc.VectorSubcoreMesh(core_axis_name="c", subcore_axis_name="s")

    @functools.partial(
        pl.kernel, mesh=mesh,
        out_type=jax.ShapeDtypeStruct((n_out, W), rows.dtype),
        scratch_types=[pltpu.VMEM((K_, b), jnp.int32), pltpu.VMEM((b, W), rows.dtype), pltpu.SemaphoreType.DMA],
        name="sc_scatter_rows",
    )
    def scatter(rows_hbm, idx_hbm, out_hbm, idx_v, rows_v, sem):
        wid = lax.axis_index("s") * nc + lax.axis_index("c")

        @pl.loop(0, steps)
        def _(s):
            g = wid * steps + s
            pltpu.sync_copy(idx_hbm.at[g], idx_v)
            pltpu.sync_copy(rows_hbm.at[pl.ds(pl.multiple_of(g * b, b), b)], rows_v)
            for k in range(K_):
                pltpu.async_copy(rows_v, out_hbm.at[idx_v.at[k]], sem).wait()

    return scatter(rows, idx)


def _combine_dense_kernel(rows_ref, w_ref, x_ref, shared_ref, gpost_ref, g2_ref, o_ref):
    w = w_ref[...]
    tt, half = rows_ref.shape[1], rows_ref.shape[2]
    y_lo = jnp.zeros((tt, half), F32)
    y_hi = jnp.zeros((tt, half), F32)
    for k in range(TOP_K):
        lo, hi = _unpack_bf16_pair(rows_ref[k])
        y_lo = y_lo + w[:, k:k + 1] * lo
        y_hi = y_hi + w[:, k:k + 1] * hi
    y = shared_ref[0] + jnp.concatenate([y_lo, y_hi], axis=1)
    o_ref[0] = x_ref[0] + g2_ref[0] * (_rms(y) * gpost_ref[...])


def _combine_dense(rows, w_tok, x, shared, gpost, g2, tt):
    Bn, S, D = x.shape
    K_, T, DP = rows.shape
    nt = S // tt
    seq = pl.BlockSpec((1, tt, D), lambda b, i: (b, i, 0))
    return pl.pallas_call(
        _combine_dense_kernel,
        grid=(Bn, nt),
        in_specs=[pl.BlockSpec((K_, tt, DP), lambda b, i: (0, b * nt + i, 0)),
                  pl.BlockSpec((tt, K_), lambda b, i: (b * nt + i, 0)), seq, seq,
                  pl.BlockSpec((1, D), lambda b, i: (0, 0)), pl.BlockSpec((1, 1, D), lambda b, i: (b, 0, 0))],
        out_specs=seq,
        out_shape=jax.ShapeDtypeStruct((Bn, S, D), F32),
        compiler_params=_cparams("arbitrary", "arbitrary"),
        name="combine_dense",
    )(rows, w_tok, x, shared, gpost.reshape(1, D), g2)


def kernel(x, c, w_ada, b_ada, norm_pre_mix, norm_post_mix, norm_pre_ffn, norm_post_ffn, w_in, w_out, rel_bias_table, diff_lambda, diff_subln, rwkv_mu, rwkv_w0, rwkv_w2, rwkv_a0, rwkv_a2, rwkv_g2, rwkv_k_k, rwkv_k_a, rwkv_r_k, rwkv_lnx_g, rwkv_lnx_b, gmlp_ln_g, gmlp_ln_b, gmlp_w_s, gmlp_b_s, router_w, router_bias, exp_w1, exp_w3, exp_w2, shared_w1, shared_w3, shared_w2):
    Bn, S, D = x.shape
    depth = w_ada.shape[0]
    tm = min(256, S)
    tq = min(256, S // 2)
    t_rwkv = min(512, S)

    mod = _adaln(c, w_ada, b_ada)
    band = _attn_band(rel_bias_table, tq)
    zpad = jnp.zeros((B_DECAY_LORA, B_WIDTH), F32)
    for l in range(depth):
        sh1, sc1, g1, sh2, sc2, g2 = [m.reshape(Bn, 1, D) for m in jnp.split(mod[l], 6, axis=-1)]
        w_in_b = w_in[l].astype(BF16)
        pa, pbc = _inproj(x, norm_pre_mix[l], sc1, sh1, w_in_b[:, :A_COLS], w_in_b[:, A_COLS:], tm)
        lambda_init = 0.8 - 0.6 * math.exp(-0.3 * l)
        ya = _diff_attention(pa, band, diff_lambda[l], diff_subln[l], lambda_init, tq)
        prep = _rwkv_prep(pbc, rwkv_mu[l], rwkv_w0[l], jnp.concatenate([rwkv_w2[l], zpad], axis=0),
                          rwkv_a0[l], jnp.concatenate([zpad, rwkv_a2[l]], axis=0), rwkv_g2[l],
                          rwkv_k_k[l], rwkv_k_a[l], rwkv_r_k[l].reshape(-1), t_rwkv)
        yb = _rwkv_scan(*prep, rwkv_lnx_g[l], rwkv_lnx_b[l], t_rwkv)
        yc = _gmlp(pbc, gmlp_ln_g[l], gmlp_ln_b[l], gmlp_w_s[l], gmlp_b_s[l], tm)

        w_out_b = w_out[l].astype(BF16)
        wr_t = jnp.pad(jnp.transpose(router_w[l]), ((0, V7X_LANES - N_EXPERTS), (0, 0)))
        x, h, scores_t, shared = _mid(
            ya, yb, yc, x, w_out_b[:A_WIDTH], w_out_b[A_WIDTH:A_WIDTH + B_WIDTH], w_out_b[A_WIDTH + B_WIDTH:],
            norm_post_mix[l], g1, norm_pre_ffn[l], sc2, sh2, wr_t,
            shared_w1[l].astype(BF16), shared_w3[l].astype(BF16), shared_w2[l].astype(BF16), tm)

        T = Bn * S
        n_blocks = -(-T * TOP_K // EXPERT_BLOCK) + N_EXPERTS
        eidx, wgt, rank, cnt = _route(scores_t, router_bias[l], tm)
        pad_start, blk_e, n_used, n_valid = _block_layout(cnt[:, 0], n_blocks)
        dest = _dest_rows(pad_start, eidx, rank, tm)
        b = SC_GATHER_ROWS
        dest_sc = jnp.transpose(dest.reshape(T // tm, TOP_K, tm // b, b), (0, 2, 1, 3)).reshape(T // b, TOP_K, b)
        xs = _sc_scatter_rows(h.reshape(T, D // 2), dest_sc, n_blocks * EXPERT_BLOCK)
        ys = _experts(blk_e, n_used, n_valid, xs, exp_w1, exp_w3, exp_w2, l)
        dest_kt = jnp.transpose(dest, (1, 0, 2)).reshape(TOP_K * T)
        rows = _sc_gather_rows(ys, dest_kt).reshape(TOP_K, T, D // 2)
        x = _combine_dense(rows, jnp.transpose(wgt), x, shared, norm_post_ffn[l], g2, tm)
    return x
```

```python
import functools
import math

import jax
import jax.numpy as jnp
from jax import lax
from jax.experimental import pallas as pl
from jax.experimental.pallas import tpu as pltpu
from jax.experimental.pallas import tpu_sc as plsc

F32 = jnp.float32
BF16 = jnp.bfloat16

A_HEADS = 4
A_QK_DIM = 64
A_HEAD_W = 2 * A_QK_DIM
A_WIDTH = A_HEADS * A_HEAD_W
N_BUCKETS = 32
MAX_DISTANCE = 128
B_HEADS = 4
B_HEAD_DIM = 64
B_WIDTH = B_HEADS * B_HEAD_DIM
B_DECAY_LORA = 64
B_AAA_LORA = 64
B_GATE_LORA = 128
B_LNX_EPS = 64e-5
C_GROUPS = 4
C_GROUP_DIM = 64
C_WIDTH = C_GROUPS * C_GROUP_DIM
CHUNK = 128
A_COLS = 3 * A_WIDTH
B_COLS = 3 * B_WIDTH + B_DECAY_LORA + B_AAA_LORA + B_GATE_LORA
C_COLS = 2 * C_WIDTH
N_EXPERTS = 64
TOP_K = 8
N_GROUPS = 8
TOPK_GROUPS = 4
EXPERTS_PER_GROUP = N_EXPERTS // N_GROUPS
ROUTED_SCALE = 2.5
EXPERT_BLOCK = 512
RMS_EPS = 1e-6
LN_EPS = 1e-5
NEG_BIG = -1e30

V7X_LANES = 128
V7X_VMEM_LIMIT_BYTES = 56 * 1024 * 1024
RWKV_CHUNK = 64
RWKV_GROUP = 8

NN = (((1,), (0,)), ((), ()))
NT = (((1,), (1,)), ((), ()))
TN = (((0,), (0,)), ((), ()))


def _cparams(*sem):
    return pltpu.CompilerParams(dimension_semantics=sem, vmem_limit_bytes=V7X_VMEM_LIMIT_BYTES)


def _mm(a, b, dims=NN):
    return lax.dot_general(a.astype(BF16), b.astype(BF16), dims, preferred_element_type=F32)


def _split(a):
    hi = a.astype(BF16)
    lo = (a - hi.astype(F32)).astype(BF16)
    return hi, lo


def _mm3(a, b, dims=NN):
    ah, al = _split(a)
    bh, bl = _split(b)
    d = lambda x, y: lax.dot_general(x, y, dims, preferred_element_type=F32)
    return d(ah, bh) + d(ah, bl) + d(al, bh)


def _mm2(a, b_exact, dims=NN):
    ah, al = _split(a)
    d = lambda x: lax.dot_general(x, b_exact, dims, preferred_element_type=F32)
    return d(ah) + d(al)


def _pack_bf16_pair(x):
    n = x.shape[1] // 2
    bits = lax.bitcast_convert_type(x.astype(BF16).astype(F32), jnp.int32)
    return ((bits[:, :n] >> 16) & 0xFFFF) | bits[:, n:]


def _unpack_bf16_pair(u):
    lo = lax.bitcast_convert_type(u << 16, F32)
    hi = lax.bitcast_convert_type(u & jnp.int32(-65536), F32)
    return lo, hi


def _rms(x, eps=RMS_EPS):
    return x * lax.rsqrt(jnp.mean(x * x, axis=-1, keepdims=True) + eps)


def _sigmoid(x):
    return 1.0 / (1.0 + jnp.exp(-x))


def _silu(x):
    return x * _sigmoid(x)


def _adaln_kernel(c_ref, w_ref, b_ref, o_ref):
    c = c_ref[...]
    o_ref[0] = _mm3(_silu(c), w_ref[0]) + b_ref[0]


def _adaln(c, w_ada, b_ada):
    L, D, N = w_ada.shape
    Bn = c.shape[0]
    tn = min(N, 1536)
    return pl.pallas_call(
        _adaln_kernel,
        grid=(L, N // tn),
        in_specs=[
            pl.BlockSpec((Bn, D), lambda l, j: (0, 0)),
            pl.BlockSpec((1, D, tn), lambda l, j: (l, 0, j)),
            pl.BlockSpec((1, 1, tn), lambda l, j: (l, 0, j)),
        ],
        out_specs=pl.BlockSpec((1, Bn, tn), lambda l, j: (l, 0, j)),
        out_shape=jax.ShapeDtypeStruct((L, Bn, N), F32),
        compiler_params=_cparams("arbitrary", "arbitrary"),
        name="adaln",
    )(c, w_ada, b_ada.reshape(L, 1, N))


def _inproj_kernel(x_ref, g_ref, sc_ref, sh_ref, wa_ref, wbc_ref, oa_ref, obc_ref):
    x = x_ref[0]
    h = _rms(x) * g_ref[...] * (1.0 + sc_ref[0]) + sh_ref[0]
    hb = h.astype(BF16)
    oa_ref[0] = jnp.dot(hb, wa_ref[...], preferred_element_type=F32).astype(BF16)
    obc_ref[0] = jnp.dot(hb, wbc_ref[...], preferred_element_type=F32)


def _inproj(x, g, sc, sh, wa, wbc, tm):
    Bn, S, D = x.shape
    na, nbc = wa.shape[1], wbc.shape[1]
    return pl.pallas_call(
        _inproj_kernel,
        grid=(Bn, S // tm),
        in_specs=[
            pl.BlockSpec((1, tm, D), lambda b, i: (b, i, 0)),
            pl.BlockSpec((1, D), lambda b, i: (0, 0)),
            pl.BlockSpec((1, 1, D), lambda b, i: (b, 0, 0)),
            pl.BlockSpec((1, 1, D), lambda b, i: (b, 0, 0)),
            pl.BlockSpec((D, na), lambda b, i: (0, 0)),
            pl.BlockSpec((D, nbc), lambda b, i: (0, 0)),
        ],
        out_specs=[
            pl.BlockSpec((1, tm, na), lambda b, i: (b, i, 0)),
            pl.BlockSpec((1, tm, nbc), lambda b, i: (b, i, 0)),
        ],
        out_shape=[
            jax.ShapeDtypeStruct((Bn, S, na), BF16),
            jax.ShapeDtypeStruct((Bn, S, nbc), F32),
        ],
        compiler_params=_cparams("arbitrary", "arbitrary"),
        name="inproj",
    )(x, g.reshape(1, D), sc, sh, wa, wbc)


def _t5_bucket(dist):
    n = jnp.maximum(dist, 0)
    max_exact = N_BUCKETS // 2
    nf = jnp.maximum(n, 1).astype(F32)
    large = max_exact + (jnp.log(nf / max_exact) / math.log(MAX_DISTANCE / max_exact)
                         * (N_BUCKETS - max_exact)).astype(jnp.int32)
    large = jnp.minimum(large, N_BUCKETS - 1)
    return jnp.where(n < max_exact, n, large)


def _attn_band(table, tq):
    far = table[N_BUCKETS - 1].astype(F32)
    L = 3 * tq
    m = jnp.arange(L)
    m = jnp.where(m < 2 * tq, m, m - L)
    bands = []
    for off in (0, tq):
        dist = off - m
        vals = jnp.where(dist[None] >= 0, jnp.transpose(table[_t5_bucket(dist)].astype(F32)) - far[:, None],
                         NEG_BIG)
        toe = jnp.tile(vals, (1, tq))[:, :tq * (L - 1)].reshape(-1, tq, L - 1)
        bands.append(toe[:, :, :2 * tq])
    band = jnp.stack(bands)
    return jnp.concatenate([band, band], axis=2)


def _attn_kernel(lam_ref, q_ref, k_ref, v_ref, band_ref, g_ref, o_ref, *, tq, lambda_init):
    i = pl.program_id(2)
    q = q_ref[0] * jnp.asarray(A_QK_DIM ** -0.5, BF16)
    lane = lax.broadcasted_iota(jnp.int32, q.shape, 1)
    zero = jnp.zeros_like(q)
    qq = jnp.concatenate([jnp.where(lane < A_QK_DIM, q, zero),
                          jnp.where(lane >= A_QK_DIM, q, zero)], axis=0)

    kb0 = pl.multiple_of(jnp.maximum(i - 1, 0) * tq, tq)
    kb = k_ref[0, pl.ds(kb0, 2 * tq), :]
    vb = v_ref[0, pl.ds(kb0, 2 * tq), :]
    s = lax.dot_general(qq, kb, NT, preferred_element_type=F32) + band_ref[0, 0]
    m = jnp.max(s, axis=-1, keepdims=True)
    p = jnp.exp(s - m)
    l = jnp.sum(p, axis=-1, keepdims=True)
    acc = jnp.dot(p.astype(BF16), vb, preferred_element_type=F32)

    n_far = jnp.maximum(i - 1, 0)

    def logits(j):
        return lax.dot_general(qq, k_ref[0, pl.ds(pl.multiple_of(j * tq, tq), tq), :], NT,
                               preferred_element_type=F32)

    def body(j, carry):
        m, l, acc, s = carry
        s_next = logits(jnp.minimum(j + 1, n_far - 1))
        vj = v_ref[0, pl.ds(pl.multiple_of(j * tq, tq), tq), :]
        m_new = jnp.maximum(m, jnp.max(s, axis=-1, keepdims=True))
        alpha = jnp.exp(m - m_new)
        p = jnp.exp(s - m_new)
        l = alpha * l + jnp.sum(p, axis=-1, keepdims=True)
        acc = alpha * acc + jnp.dot(p.astype(BF16), vj, preferred_element_type=F32)
        return m_new, l, acc, s_next

    m, l, acc, _ = lax.fori_loop(0, n_far, body, (m, l, acc, logits(0)))

    lp = lam_ref[...]
    lam = (jnp.exp(jnp.sum(lp[0:1] * lp[1:2], axis=-1, keepdims=True))
           - jnp.exp(jnp.sum(lp[2:3] * lp[3:4], axis=-1, keepdims=True)) + lambda_init)
    o = acc / l
    o = o[:tq] - lam * o[tq:]
    o_ref[0] = _rms(o) * g_ref[...] * (1.0 - lambda_init)


def _diff_attention(pa, band, lam_par, subln_g, lambda_init, tq):
    Bn, S, _ = pa.shape
    W = A_HEAD_W
    kern = functools.partial(_attn_kernel, tq=tq, lambda_init=lambda_init)
    return pl.pallas_call(
        kern,
        grid=(Bn, A_HEADS, S // tq),
        in_specs=[
            pl.BlockSpec((4, A_QK_DIM), lambda b, h, i: (0, 0)),
            pl.BlockSpec((1, tq, W), lambda b, h, i: (b, i, h)),
            pl.BlockSpec((1, S, W), lambda b, h, i: (b, 0, A_HEADS + h)),
            pl.BlockSpec((1, S, W), lambda b, h, i: (b, 0, 2 * A_HEADS + h)),
            pl.BlockSpec((1, 1, 2 * tq, 2 * tq), lambda b, h, i: (jnp.minimum(i, 1), h, 0, 0)),
            pl.BlockSpec((1, W), lambda b, h, i: (0, 0)),
        ],
        out_specs=pl.BlockSpec((1, tq, W), lambda b, h, i: (b, i, h)),
        out_shape=jax.ShapeDtypeStruct((Bn, S, A_WIDTH), F32),
        compiler_params=_cparams("arbitrary", "arbitrary", "arbitrary"),
        name="diff_attn",
    )(lam_par, pa, pa, pa, band, subln_g.reshape(1, W))


def _head_ones(n):
    r = lax.broadcasted_iota(jnp.int32, (n, n), 0) // B_HEAD_DIM
    c = lax.broadcasted_iota(jnp.int32, (n, n), 1) // B_HEAD_DIM
    return (r == c).astype(BF16)


def _rwkv_prep_kernel(pb_ref, prev_ref, mu_ref, w0_ref, w2_ref, a0_ref, a2_ref, g2_ref,
                      kk_ref, ka_ref, rk_ref,
                      rt_ref, at_ref, kt_ref, bt_ref, v_ref, wc_ref, bonus_ref, g_ref, *, tm):
    i = pl.program_id(1)
    C = RWKV_CHUNK
    x = pb_ref[0]
    row = lax.broadcasted_iota(jnp.int32, x.shape, 0)
    last = prev_ref[0, 7:8, :] * (i > 0).astype(F32)
    prev = jnp.where(row == 0, last, pltpu.roll(x, 1, 0))
    p = x + (prev - x) * mu_ref[...]
    o1, o2, o3 = B_WIDTH, 2 * B_WIDTH, 3 * B_WIDTH
    r, k, v = p[:, :o1], p[:, o1:o2], p[:, o2:o3]
    lora = p[:, o3:o3 + B_DECAY_LORA + B_AAA_LORA]
    gd = p[:, o3 + B_DECAY_LORA + B_AAA_LORA:]

    z = -(w0_ref[...] + _mm3(jnp.tanh(lora), w2_ref[...]))
    softplus = jnp.maximum(z, 0.0) + jnp.log(1.0 + jnp.exp(-jnp.abs(z)))
    logw = -jnp.exp(-softplus - 0.5)
    a = _sigmoid(a0_ref[...] + _mm3(lora, a2_ref[...]))
    g_ref[0] = _mm3(_sigmoid(gd), g2_ref[...])

    ones = _head_ones(B_WIDTH)
    kk = k * kk_ref[...]
    kk = kk * lax.rsqrt(jnp.maximum(_mm2(kk * kk, ones), 1e-24))
    k2 = k * (1.0 + (a - 1.0) * ka_ref[...])
    bonus_ref[0] = _mm2(r * k2 * rk_ref[...], ones) * v

    t_in = lax.broadcasted_iota(jnp.int32, (tm, B_WIDTH), 0) % C
    cum = logw
    sh = 1
    while sh < C:
        cum = cum + jnp.where(t_in >= sh, pltpu.roll(cum, sh, 0), 0.0)
        sh *= 2
    n = tm // C
    wc_ref[0] = jnp.exp(jnp.sum(logw.reshape(n, C, B_WIDTH), axis=1))
    e_pos = jnp.exp(cum)
    e_neg = jnp.exp(-cum)
    rt_ref[0] = r * e_pos
    at_ref[0] = -kk * jnp.exp(cum - logw)
    kt_ref[0] = k2 * e_neg
    bt_ref[0] = kk * a * e_neg
    v_ref[0] = v


def _rwkv_prep(pbc, mu, w0, w2p, a0, a2p, g2, k_k, k_a, r_k, tm):
    Bn, S, _ = pbc.shape
    W = B_WIDTH
    nl = B_DECAY_LORA + B_AAA_LORA
    row = lambda a: a.reshape(1, -1)
    full = lambda shp: pl.BlockSpec(shp, lambda b, i: (0,) * len(shp))
    seq = pl.BlockSpec((1, tm, W), lambda b, i: (b, i, 0))
    seq_shape = jax.ShapeDtypeStruct((Bn, S, W), F32)
    n = tm // RWKV_CHUNK
    return pl.pallas_call(
        functools.partial(_rwkv_prep_kernel, tm=tm),
        grid=(Bn, S // tm),
        in_specs=[
            pl.BlockSpec((1, tm, B_COLS), lambda b, i: (b, i, 0)),
            pl.BlockSpec((1, 8, B_COLS), lambda b, i: (b, jnp.maximum(i * (tm // 8) - 1, 0), 0)),
            full((1, B_COLS)), full((1, W)), full((nl, W)), full((1, W)), full((nl, W)),
            full((B_GATE_LORA, W)), full((1, W)), full((1, W)), full((1, W)),
        ],
        out_specs=[seq, seq, seq, seq, seq,
                   pl.BlockSpec((1, n, W), lambda b, i: (b, i, 0)), seq, seq],
        out_shape=[seq_shape] * 5 + [jax.ShapeDtypeStruct((Bn, S // RWKV_CHUNK, W), F32)] + [seq_shape] * 2,
        compiler_params=_cparams("arbitrary", "arbitrary"),
        name="rwkv_prep",
    )(pbc, pbc, row(mu), row(w0), w2p, row(a0), a2p, g2, row(k_k), row(k_a), row(r_k))


def _rwkv_scan_kernel(rt_ref, at_ref, kt_ref, bt_ref, v_ref, wc_ref, bonus_ref, g_ref,
                      lng_ref, lnb_ref, o_ref, state, *, tt):
    C = RWKV_CHUNK
    W = B_WIDTH

    @pl.when(pl.program_id(1) == 0)
    def _():
        state[...] = jnp.zeros_like(state)

    lane_head = lax.broadcasted_iota(jnp.int32, (C, W), 1) // B_HEAD_DIM
    tt_i = lax.broadcasted_iota(jnp.int32, (C, W), 0)
    ss_i = lax.broadcasted_iota(jnp.int32, (C, W), 1) % C
    strict = tt_i > ss_i
    incl = tt_i >= ss_i
    eye = (tt_i == ss_i).astype(F32)
    ones = _head_ones(W)
    bd_mask = ones.astype(F32)

    head_mask = [(lane_head == h).astype(BF16) for h in range(B_HEADS)]

    def bd_split(x):
        xb = x.astype(BF16)
        return jnp.concatenate([xb * mk for mk in head_mask], axis=0)

    def mm_bd(a, b_bd, dims=NN):
        return lax.dot_general(a.astype(BF16), b_bd, dims, preferred_element_type=F32)

    def state_free(gi):
        G = range(RWKV_GROUP)
        sls = [pl.ds(pl.multiple_of((gi * RWKV_GROUP + j) * C, C), C) for j in G]
        rt = [rt_ref[0, sl, :] for sl in sls]
        at = [at_ref[0, sl, :] for sl in sls]
        kt = [kt_ref[0, sl, :] for sl in sls]
        bt = [bt_ref[0, sl, :] for sl in sls]
        v = [v_ref[0, sl, :] for sl in sls]
        wc = [wc_ref[0, pl.ds(gi * RWKV_GROUP + j, 1), :] for j in G]
        ar = [jnp.concatenate([at[j], rt[j]], axis=0) for j in G]
        bdb = [bd_split(bt[j]) for j in G]
        bdk = [bd_split(kt[j]) for j in G]
        a_b = [mm_bd(ar[j], bdb[j], NT) for j in G]
        a_k = [mm_bd(ar[j], bdk[j], NT) for j in G]
        lo = [jnp.where(strict, a_b[j][:C], 0.0) for j in G]
        a_ak = [jnp.where(strict, a_k[j][:C], 0.0) for j in G]
        a_rb = [jnp.where(incl, a_b[j][C:], 0.0) for j in G]
        a_rk = [jnp.where(incl, a_k[j][C:], 0.0) for j in G]
        pw = lo
        tinv = [eye + lo[j] for j in G]
        bdp = [bd_split(pw[j]) for j in G]
        span = 2
        while span < C:
            pw = [mm_bd(pw[j], bdp[j]) for j in G]
            bdp = [bd_split(pw[j]) for j in G]
            tinv = [tinv[j] + mm_bd(tinv[j], bdp[j]) for j in G]
            span *= 2
        bdv = [bd_split(v[j]) for j in G]
        bda = [bd_split(at[j]) for j in G]
        abar = [mm_bd(tinv[j], bda[j]) for j in G]
        akv = [bd_split(mm_bd(a_ak[j], bdv[j])) for j in G]
        u0 = [mm_bd(tinv[j], akv[j]) for j in G]
        y0 = [mm_bd(a_rk[j], bdv[j]) for j in G]
        kv = [_mm(v[j], kt[j] * wc[j], TN) * bd_mask for j in G]
        return [(jnp.concatenate([abar[j], rt[j]], axis=0), u0[j], y0[j], a_rb[j], bt[j] * wc[j], kv[j], wc[j])
                for j in G]

    def group(gi, carry):
        pre = state_free(gi)
        s = state[...]
        ys = []
        for abar_rt, u0, y0, a_rb, btw, kv, wc in pre:
            a_s = _mm(abar_rt, s, NT)
            u = a_s[:C] + u0
            ys.append(a_s[C:] + y0 + mm_bd(a_rb, bd_split(u)))
            s = s * wc + _mm(u, btw, TN) * bd_mask + kv
        state[...] = s
        y = jnp.concatenate(ys, axis=0)
        sl = pl.ds(pl.multiple_of(gi * (RWKV_GROUP * C), RWKV_GROUP * C), RWKV_GROUP * C)
        mean = _mm2(y, ones) * (1.0 / B_HEAD_DIM)
        d = y - mean
        var = _mm2(d * d, ones) * (1.0 / B_HEAD_DIM)
        yn = d * lax.rsqrt(var + B_LNX_EPS) * lng_ref[...] + lnb_ref[...]
        o_ref[0, sl, :] = (yn + bonus_ref[0, sl, :]) * g_ref[0, sl, :]
        return carry

    lax.fori_loop(0, tt // (RWKV_GROUP * C), group, 0)


def _rwkv_scan(rt, at, kt, bt, v, wc, bonus, g, lnx_g, lnx_b, tt):
    Bn, S, W = rt.shape
    n = tt // RWKV_CHUNK
    seq = pl.BlockSpec((1, tt, W), lambda b, i: (b, i, 0))
    vec = pl.BlockSpec((1, W), lambda b, i: (0, 0))
    return pl.pallas_call(
        functools.partial(_rwkv_scan_kernel, tt=tt),
        grid=(Bn, S // tt),
        in_specs=[seq, seq, seq, seq, seq, pl.BlockSpec((1, n, W), lambda b, i: (b, i, 0)), seq, seq, vec, vec],
        out_specs=seq,
        out_shape=jax.ShapeDtypeStruct((Bn, S, W), F32),
        scratch_shapes=[pltpu.VMEM((B_HEADS * B_HEAD_DIM, W), F32)],
        compiler_params=_cparams("arbitrary", "arbitrary"),
        name="rwkv_scan",
    )(rt, at, kt, bt, v, wc, bonus, g, lnx_g.reshape(1, W), lnx_b.reshape(1, W))


def _gmlp_kernel(pc_ref, lng_ref, lnb_ref, ws_ref, bs_ref, o_ref, *, tm):
    x = pc_ref[0]
    z = x * (0.5 * (1.0 + jnp.tanh(math.sqrt(2.0 / math.pi) * (x + 0.044715 * (x * x * x)))))
    u, v = z[:, :C_WIDTH], z[:, C_WIDTH:]
    mu = jnp.mean(v, axis=-1, keepdims=True)
    d = v - mu
    var = jnp.mean(d * d, axis=-1, keepdims=True)
    vn = d * lax.rsqrt(var + LN_EPS) * lng_ref[...] + lnb_ref[...]
    group = lax.broadcasted_iota(jnp.int32, (CHUNK, C_WIDTH), 1) // C_GROUP_DIM
    tril = (lax.broadcasted_iota(jnp.int32, (CHUNK, CHUNK), 0)
            >= lax.broadcasted_iota(jnp.int32, (CHUNK, CHUNK), 1))
    ws = [jnp.where(tril, ws_ref[gi], 0.0).astype(BF16) for gi in range(C_GROUPS)]
    for c in range(tm // CHUNK):
        sl = slice(c * CHUNK, (c + 1) * CHUNK)
        vc = vn[sl].astype(BF16)
        sv = bs_ref[...]
        for gi in range(C_GROUPS):
            t = jnp.dot(ws[gi], vc, preferred_element_type=F32)
            sv = sv + jnp.where(group == gi, t, 0.0)
        o_ref[0, sl, :] = u[sl] * sv


def _gmlp(pbc, ln_g, ln_b, w_s, b_s, tm):
    Bn, S, _ = pbc.shape
    bs_wide = jnp.repeat(jnp.transpose(b_s), C_GROUP_DIM, axis=1)
    return pl.pallas_call(
        functools.partial(_gmlp_kernel, tm=tm),
        grid=(Bn, S // tm),
        in_specs=[
            pl.BlockSpec((1, tm, C_COLS), lambda b, i: (b, i, B_COLS // C_COLS)),
            pl.BlockSpec((1, C_WIDTH), lambda b, i: (0, 0)),
            pl.BlockSpec((1, C_WIDTH), lambda b, i: (0, 0)),
            pl.BlockSpec((C_GROUPS, CHUNK, CHUNK), lambda b, i: (0, 0, 0)),
            pl.BlockSpec((CHUNK, C_WIDTH), lambda b, i: (0, 0)),
        ],
        out_specs=pl.BlockSpec((1, tm, C_WIDTH), lambda b, i: (b, i, 0)),
        out_shape=jax.ShapeDtypeStruct((Bn, S, C_WIDTH), F32),
        compiler_params=_cparams("arbitrary", "arbitrary"),
        name="gmlp",
    )(pbc, ln_g.reshape(1, -1), ln_b.reshape(1, -1), w_s, bs_wide)


def _mid_kernel(ya_ref, yb_ref, yc_ref, x_ref, woa_ref, wob_ref, woc_ref, gpost_ref, g1_ref,
                gpre_ref, sc_ref, sh_ref, wr_ref, ws1_ref, ws3_ref, ws2_ref,
                xo_ref, h_ref, score_ref, shared_ref):
    y = (_mm(ya_ref[0], woa_ref[...]) + _mm(yb_ref[0], wob_ref[...]) + _mm(yc_ref[0], woc_ref[...]))
    xn = x_ref[0] + g1_ref[0] * (_rms(y) * gpost_ref[...])
    xo_ref[0] = xn
    h = _rms(xn) * gpre_ref[...] * (1.0 + sc_ref[0]) + sh_ref[0]
    h_ref[0] = _pack_bf16_pair(h)
    score_ref[0] = _sigmoid(_mm3(wr_ref[...], h, NT))
    hb = h.astype(BF16)
    t = _silu(jnp.dot(hb, ws1_ref[...], preferred_element_type=F32)) * jnp.dot(
        hb, ws3_ref[...], preferred_element_type=F32)
    shared_ref[0] = jnp.dot(t.astype(BF16), ws2_ref[...], preferred_element_type=F32)


def _mid(ya, yb, yc, x, woa, wob, woc, gpost, g1, gpre, sc, sh, wr, ws1, ws3, ws2, tm):
    Bn, S, D = x.shape
    NR = wr.shape[0]
    F = ws1.shape[1]
    seq = lambda w: pl.BlockSpec((1, tm, w), lambda b, i: (b, i, 0))
    full = lambda shp: pl.BlockSpec(shp, lambda b, i: (0,) * len(shp))
    per_b = pl.BlockSpec((1, 1, D), lambda b, i: (b, 0, 0))
    return pl.pallas_call(
        _mid_kernel,
        grid=(Bn, S // tm),
        in_specs=[seq(A_WIDTH), seq(B_WIDTH), seq(C_WIDTH), seq(D),
                  full((A_WIDTH, D)), full((B_WIDTH, D)), full((C_WIDTH, D)),
                  full((1, D)), per_b, full((1, D)), per_b, per_b,
                  full((NR, D)), full((D, F)), full((D, F)), full((F, D))],
        out_specs=[seq(D), seq(D // 2), pl.BlockSpec((1, NR, tm), lambda b, i: (b, 0, i)), seq(D)],
        out_shape=[jax.ShapeDtypeStruct((Bn, S, D), F32), jax.ShapeDtypeStruct((Bn, S, D // 2), jnp.int32),
                   jax.ShapeDtypeStruct((Bn, NR, S), F32), jax.ShapeDtypeStruct((Bn, S, D), F32)],
        compiler_params=_cparams("arbitrary", "arbitrary"),
        name="mid",
    )(ya, yb, yc, x, woa, wob, woc, gpost.reshape(1, D), g1, gpre.reshape(1, D), sc, sh, wr, ws1, ws3, ws2)


def _first_argmax(vals, iota, n):
    m = jnp.max(vals, axis=0, keepdims=True)
    idx = jnp.min(jnp.where(vals == m, iota, n), axis=0, keepdims=True)
    return m, idx


def _route_kernel(sc_ref, bias_ref, e_ref, w_ref, r_ref, cnt_ref, carry, *, tm):
    @pl.when((pl.program_id(0) == 0) & (pl.program_id(1) == 0))
    def _():
        carry[...] = jnp.zeros_like(carry)

    G = EXPERTS_PER_GROUP
    s = sc_ref[0]
    biased = s + bias_ref[...]
    neg_inf = jnp.float32(-jnp.inf)
    io8 = lax.broadcasted_iota(jnp.int32, (G, tm), 0)
    gs_rows = []
    for g in range(N_GROUPS):
        blk = biased[g * G:(g + 1) * G]
        m1, i1 = _first_argmax(blk, io8, G)
        m2 = jnp.max(jnp.where(io8 == i1, neg_inf, blk), axis=0, keepdims=True)
        gs_rows.append(m1 + m2)
    gs = jnp.concatenate(gs_rows, axis=0)
    gio = lax.broadcasted_iota(jnp.int32, (N_GROUPS, tm), 0)
    gsel = jnp.zeros((N_GROUPS, tm), jnp.bool_)
    for _ in range(TOPK_GROUPS):
        _, gi = _first_argmax(gs, gio, N_GROUPS)
        pick = gio == gi
        gsel = gsel | pick
        gs = jnp.where(pick, neg_inf, gs)
    masked = jnp.concatenate(
        [jnp.where(gsel[g:g + 1], biased[g * G:(g + 1) * G], neg_inf) for g in range(N_GROUPS)], axis=0)

    eio = lax.broadcasted_iota(jnp.int32, (N_EXPERTS, tm), 0)
    picks, e_rows, s_rows = [], [], []
    for _ in range(TOP_K):
        _, ei = _first_argmax(masked, eio, N_EXPERTS)
        pick = eio == ei
        picks.append(pick)
        e_rows.append(ei)
        s_rows.append(jnp.sum(jnp.where(pick, s, 0.0), axis=0, keepdims=True))
        masked = jnp.where(pick, neg_inf, masked)
    top_s = jnp.concatenate(s_rows, axis=0)
    w_ref[...] = top_s / (jnp.sum(top_s, axis=0, keepdims=True) + 1e-20) * ROUTED_SCALE
    e_ref[...] = jnp.concatenate(e_rows, axis=0)

    sel = jnp.zeros((N_EXPERTS, tm), F32)
    for pick in picks:
        sel = sel + pick.astype(F32)
    before = (lax.broadcasted_iota(jnp.int32, (tm, tm), 0) < lax.broadcasted_iota(jnp.int32, (tm, tm), 1))
    pos = carry[...] + jnp.dot(sel.astype(BF16), before.astype(BF16), preferred_element_type=F32)
    r_ref[...] = jnp.concatenate(
        [jnp.sum(jnp.where(pick, pos, 0.0), axis=0, keepdims=True) for pick in picks], axis=0).astype(jnp.int32)
    total = carry[...] + jnp.sum(sel, axis=1, keepdims=True)
    carry[...] = total
    cnt_ref[...] = jnp.broadcast_to(total, cnt_ref.shape).astype(jnp.int32)


def _route(scores_t, e_bias, tm):
    Bn, _, S = scores_t.shape
    T = Bn * S
    nt = S // tm
    tok = pl.BlockSpec((TOP_K, tm), lambda b, i: (0, b * nt + i))
    return pl.pallas_call(
        functools.partial(_route_kernel, tm=tm),
        grid=(Bn, nt),
        in_specs=[pl.BlockSpec((1, N_EXPERTS, tm), lambda b, i: (b, 0, i)),
                  pl.BlockSpec((N_EXPERTS, 1), lambda b, i: (0, 0))],
        out_specs=[tok, tok, tok, pl.BlockSpec((N_EXPERTS, V7X_LANES), lambda b, i: (0, 0))],
        out_shape=[jax.ShapeDtypeStruct((TOP_K, T), jnp.int32), jax.ShapeDtypeStruct((TOP_K, T), F32),
                   jax.ShapeDtypeStruct((TOP_K, T), jnp.int32),
                   jax.ShapeDtypeStruct((N_EXPERTS, V7X_LANES), jnp.int32)],
        scratch_shapes=[pltpu.VMEM((N_EXPERTS, 1), F32)],
        compiler_params=_cparams("arbitrary", "arbitrary"),
        name="route",
    )(scores_t, e_bias.reshape(N_EXPERTS, 1))


def _dest_kernel(start_ref, e_ref, r_ref, o_ref):
    e = e_ref[...]
    acc = r_ref[...]
    for ex in range(N_EXPERTS):
        acc = acc + jnp.where(e == ex, start_ref[ex], 0)
    o_ref[0] = acc


def _dest_rows(pad_start, eidx, rank, tt):
    K_, T = eidx.shape
    grid_spec = pltpu.PrefetchScalarGridSpec(
        num_scalar_prefetch=1,
        grid=(T // tt,),
        in_specs=[pl.BlockSpec((K_, tt), lambda i, st: (0, i)), pl.BlockSpec((K_, tt), lambda i, st: (0, i))],
        out_specs=pl.BlockSpec((1, K_, tt), lambda i, st: (i, 0, 0)),
    )
    return pl.pallas_call(
        _dest_kernel,
        grid_spec=grid_spec,
        out_shape=jax.ShapeDtypeStruct((T // tt, K_, tt), jnp.int32),
        compiler_params=_cparams("arbitrary"),
        name="dest_rows",
    )(pad_start, eidx, rank)


def _expert_kernel(blk_e_ref, n_used_ref, n_valid_ref, x_ref, w1_ref, w3_ref, w2_ref, o_ref, w1b, w3b, w2b):
    i = pl.program_id(0)

    @pl.when((i == 0) | (blk_e_ref[i] != blk_e_ref[jnp.maximum(i - 1, 0)]))
    def _():
        w1b[...] = w1_ref[0].astype(BF16)
        w3b[...] = w3_ref[0].astype(BF16)
        w2b[...] = w2_ref[0].astype(BF16)

    @pl.when(i < n_used_ref[0])
    def _():
        row = lax.broadcasted_iota(jnp.int32, x_ref.shape, 0)
        x_lo, x_hi = _unpack_bf16_pair(jnp.where(row < n_valid_ref[i], x_ref[...], 0))
        x_lo, x_hi = x_lo.astype(BF16), x_hi.astype(BF16)
        half = x_lo.shape[1]

        def up(wb):
            return (jnp.dot(x_lo, wb[:half, :], preferred_element_type=F32)
                    + jnp.dot(x_hi, wb[half:, :], preferred_element_type=F32))

        t = _silu(up(w1b)) * up(w3b)
        o_ref[...] = _pack_bf16_pair(jnp.dot(t.astype(BF16), w2b[...], preferred_element_type=F32))


def _experts(blk_e, n_used, n_valid, xs, w1, w3, w2, layer):
    P, DP = xs.shape
    EB = EXPERT_BLOCK
    n_blocks = blk_e.shape[0]
    D, F = w1.shape[2], w1.shape[3]
    rows = pl.BlockSpec((EB, DP), lambda i, be, nu, nv: (jnp.minimum(i, nu[0] - 1), 0))
    grid_spec = pltpu.PrefetchScalarGridSpec(
        num_scalar_prefetch=3,
        grid=(n_blocks,),
        in_specs=[
            rows,
            pl.BlockSpec((None, 1, D, F), lambda i, be, nu, nv: (layer, be[i], 0, 0)),
            pl.BlockSpec((None, 1, D, F), lambda i, be, nu, nv: (layer, be[i], 0, 0)),
            pl.BlockSpec((None, 1, F, D), lambda i, be, nu, nv: (layer, be[i], 0, 0)),
        ],
        out_specs=rows,
        scratch_shapes=[pltpu.VMEM((D, F), BF16), pltpu.VMEM((D, F), BF16), pltpu.VMEM((F, D), BF16)],
    )
    return pl.pallas_call(
        _expert_kernel,
        grid_spec=grid_spec,
        out_shape=jax.ShapeDtypeStruct((P, DP), jnp.int32),
        compiler_params=_cparams("arbitrary"),
        name="experts",
    )(blk_e, n_used, n_valid, xs, w1, w3, w2)


def _block_layout(counts, n_blocks):
    EB = EXPERT_BLOCK
    padded = (counts + EB - 1) // EB * EB
    pad_end = jnp.cumsum(padded)
    pad_start = pad_end - padded
    blk_row = (jnp.arange(n_blocks) * EB)[:, None]
    blk_e = jnp.minimum(jnp.sum((pad_end[None, :] <= blk_row).astype(jnp.int32), axis=1), N_EXPERTS - 1)
    n_used = (pad_end[-1] // EB).astype(jnp.int32).reshape(1)
    n_valid = jnp.clip(counts[blk_e] - (blk_row[:, 0] - pad_start[blk_e]), 0, EB).astype(jnp.int32)
    return pad_start.astype(jnp.int32), blk_e, n_used, n_valid


SC_GATHER_ROWS = 64


def _sc_gather_rows(table, idx):
    info = plsc.get_sparse_core_info()
    nc, ns = info.num_cores, info.num_subcores
    M = idx.shape[0]
    W = table.shape[1]
    b = SC_GATHER_ROWS
    per_worker = M // (nc * ns)
    steps = per_worker // b
    assert per_worker * nc * ns == M and steps * b == per_worker
    mesh = plsc.VectorSubcoreMesh(core_axis_name="c", subcore_axis_name="s")

    @functools.partial(
        pl.kernel, mesh=mesh,
        out_type=jax.ShapeDtypeStruct((M, W), table.dtype),
        scratch_types=[pltpu.VMEM((b,), jnp.int32), pltpu.VMEM((b, W), table.dtype), pltpu.SemaphoreType.DMA],
        name="sc_gather_rows",
    )
    def gather(table_hbm, idx_hbm, out_hbm, idx_v, rows_v, sem):
        wid = lax.axis_index("s") * nc + lax.axis_index("c")

        @pl.loop(0, steps)
        def _(s):
            base = pl.multiple_of(wid * per_worker + s * b, b)
            pltpu.sync_copy(idx_hbm.at[pl.ds(base, b)], idx_v)
            pltpu.async_copy(table_hbm.at[idx_v], rows_v, sem).wait()
            pltpu.sync_copy(rows_v, out_hbm.at[pl.ds(base, b)])

    return gather(table, idx)


def _sc_scatter_rows(rows, idx, n_out):
    info = plsc.get_sparse_core_info()
    nc, ns = info.num_cores, info.num_subcores
    T, W = rows.shape
    G, K_, b = idx.shape
    steps = G // (nc * ns)
    assert steps * nc * ns == G and G * b == T
    mesh = plsc.VectorSubcoreMesh(core_axis_name="c", subcore_axis_name="s")

    @functools.partial(
        pl.kernel, mesh=mesh,
        out_type=jax.ShapeDtypeStruct((n_out, W), rows.dtype),
        scratch_types=[pltpu.VMEM((K_, b), jnp.int32), pltpu.VMEM((b, W), rows.dtype), pltpu.SemaphoreType.DMA],
        name="sc_scatter_rows",
    )
    def scatter(rows_hbm, idx_hbm, out_hbm, idx_v, rows_v, sem):
        wid = lax.axis_index("s") * nc + lax.axis_index("c")

        @pl.loop(0, steps)
        def _(s):
            g = wid * steps + s
            pltpu.sync_copy(idx_hbm.at[g], idx_v)
            pltpu.sync_copy(rows_hbm.at[pl.ds(pl.multiple_of(g * b, b), b)], rows_v)
            for k in range(K_):
                pltpu.async_copy(rows_v, out_hbm.at[idx_v.at[k]], sem).wait()

    return scatter(rows, idx)


def _combine_dense_kernel(rows_ref, w_ref, x_ref, shared_ref, gpost_ref, g2_ref, o_ref):
    w = w_ref[...]
    tt, half = rows_ref.shape[1], rows_ref.shape[2]
    y_lo = jnp.zeros((tt, half), F32)
    y_hi = jnp.zeros((tt, half), F32)
    for k in range(TOP_K):
        lo, hi = _unpack_bf16_pair(rows_ref[k])
        y_lo = y_lo + w[:, k:k + 1] * lo
        y_hi = y_hi + w[:, k:k + 1] * hi
    y = shared_ref[0] + jnp.concatenate([y_lo, y_hi], axis=1)
    o_ref[0] = x_ref[0] + g2_ref[0] * (_rms(y) * gpost_ref[...])


def _combine_dense(rows, w_tok, x, shared, gpost, g2, tt):
    Bn, S, D = x.shape
    K_, T, DP = rows.shape
    nt = S // tt
    seq = pl.BlockSpec((1, tt, D), lambda b, i: (b, i, 0))
    return pl.pallas_call(
        _combine_dense_kernel,
        grid=(Bn, nt),
        in_specs=[pl.BlockSpec((K_, tt, DP), lambda b, i: (0, b * nt + i, 0)),
                  pl.BlockSpec((tt, K_), lambda b, i: (b * nt + i, 0)), seq, seq,
                  pl.BlockSpec((1, D), lambda b, i: (0, 0)), pl.BlockSpec((1, 1, D), lambda b, i: (b, 0, 0))],
        out_specs=seq,
        out_shape=jax.ShapeDtypeStruct((Bn, S, D), F32),
        compiler_params=_cparams("arbitrary", "arbitrary"),
        name="combine_dense",
    )(rows, w_tok, x, shared, gpost.reshape(1, D), g2)


def kernel(x, c, w_ada, b_ada, norm_pre_mix, norm_post_mix, norm_pre_ffn, norm_post_ffn, w_in, w_out, rel_bias_table, diff_lambda, diff_subln, rwkv_mu, rwkv_w0, rwkv_w2, rwkv_a0, rwkv_a2, rwkv_g2, rwkv_k_k, rwkv_k_a, rwkv_r_k, rwkv_lnx_g, rwkv_lnx_b, gmlp_ln_g, gmlp_ln_b, gmlp_w_s, gmlp_b_s, router_w, router_bias, exp_w1, exp_w3, exp_w2, shared_w1, shared_w3, shared_w2):
    Bn, S, D = x.shape
    depth = w_ada.shape[0]
    tm = min(256, S)
    tq = min(256, S // 2)
    t_rwkv = min(512, S)

    mod = _adaln(c, w_ada, b_ada)
    band = _attn_band(rel_bias_table, tq)
    zpad = jnp.zeros((B_DECAY_LORA, B_WIDTH), F32)
    for l in range(depth):
        sh1, sc1, g1, sh2, sc2, g2 = [m.reshape(Bn, 1, D) for m in jnp.split(mod[l], 6, axis=-1)]
        w_in_b = w_in[l].astype(BF16)
        pa, pbc = _inproj(x, norm_pre_mix[l], sc1, sh1, w_in_b[:, :A_COLS], w_in_b[:, A_COLS:], tm)
        lambda_init = 0.8 - 0.6 * math.exp(-0.3 * l)
        ya = _diff_attention(pa, band, diff_lambda[l], diff_subln[l], lambda_init, tq)
        prep = _rwkv_prep(pbc, rwkv_mu[l], rwkv_w0[l], jnp.concatenate([rwkv_w2[l], zpad], axis=0),
                          rwkv_a0[l], jnp.concatenate([zpad, rwkv_a2[l]], axis=0), rwkv_g2[l],
                          rwkv_k_k[l], rwkv_k_a[l], rwkv_r_k[l].reshape(-1), t_rwkv)
        yb = _rwkv_scan(*prep, rwkv_lnx_g[l], rwkv_lnx_b[l], t_rwkv)
        yc = _gmlp(pbc, gmlp_ln_g[l], gmlp_ln_b[l], gmlp_w_s[l], gmlp_b_s[l], tm)

        w_out_b = w_out[l].astype(BF16)
        wr_t = jnp.pad(jnp.transpose(router_w[l]), ((0, V7X_LANES - N_EXPERTS), (0, 0)))
        x, h, scores_t, shared = _mid(
            ya, yb, yc, x, w_out_b[:A_WIDTH], w_out_b[A_WIDTH:A_WIDTH + B_WIDTH], w_out_b[A_WIDTH + B_WIDTH:],
            norm_post_mix[l], g1, norm_pre_ffn[l], sc2, sh2, wr_t,
            shared_w1[l].astype(BF16), shared_w3[l].astype(BF16), shared_w2[l].astype(BF16), tm)

        T = Bn * S
        n_blocks = -(-T * TOP_K // EXPERT_BLOCK) + N_EXPERTS
        eidx, wgt, rank, cnt = _route(scores_t, router_bias[l], tm)
        pad_start, blk_e, n_used, n_valid = _block_layout(cnt[:, 0], n_blocks)
        dest = _dest_rows(pad_start, eidx, rank, tm)
        b = SC_GATHER_ROWS
        dest_sc = jnp.transpose(dest.reshape(T // tm, TOP_K, tm // b, b), (0, 2, 1, 3)).reshape(T // b, TOP_K, b)
        xs = _sc_scatter_rows(h.reshape(T, D // 2), dest_sc, n_blocks * EXPERT_BLOCK)
        ys = _experts(blk_e, n_used, n_valid, xs, exp_w1, exp_w3, exp_w2, l)
        dest_kt = jnp.transpose(dest, (1, 0, 2)).reshape(TOP_K * T)
        rows = _sc_gather_rows(ys, dest_kt).reshape(TOP_K, T, D // 2)
        x = _combine_dense(rows, jnp.transpose(wgt), x, shared, norm_post_ffn[l], g2, tm)
    return x
```

```python
import functools
import math

import jax
import jax.numpy as jnp
from jax import lax
from jax.experimental import pallas as pl
from jax.experimental.pallas import tpu as pltpu
from jax.experimental.pallas import tpu_sc as plsc

F32 = jnp.float32
BF16 = jnp.bfloat16

A_HEADS = 4
A_QK_DIM = 64
A_HEAD_W = 2 * A_QK_DIM
A_WIDTH = A_HEADS * A_HEAD_W
N_BUCKETS = 32
MAX_DISTANCE = 128
B_HEADS = 4
B_HEAD_DIM = 64
B_WIDTH = B_HEADS * B_HEAD_DIM
B_DECAY_LORA = 64
B_AAA_LORA = 64
B_GATE_LORA = 128
B_LNX_EPS = 64e-5
C_GROUPS = 4
C_GROUP_DIM = 64
C_WIDTH = C_GROUPS * C_GROUP_DIM
CHUNK = 128
A_COLS = 3 * A_WIDTH
B_COLS = 3 * B_WIDTH + B_DECAY_LORA + B_AAA_LORA + B_GATE_LORA
C_COLS = 2 * C_WIDTH
N_EXPERTS = 64
TOP_K = 8
N_GROUPS = 8
TOPK_GROUPS = 4
EXPERTS_PER_GROUP = N_EXPERTS // N_GROUPS
ROUTED_SCALE = 2.5
EXPERT_BLOCK = 512
RMS_EPS = 1e-6
LN_EPS = 1e-5
NEG_BIG = -1e30

V7X_LANES = 128
V7X_VMEM_LIMIT_BYTES = 56 * 1024 * 1024
RWKV_CHUNK = 64
RWKV_GROUP = 8

NN = (((1,), (0,)), ((), ()))
NT = (((1,), (1,)), ((), ()))
TN = (((0,), (0,)), ((), ()))


def _cparams(*sem):
    return pltpu.CompilerParams(dimension_semantics=sem, vmem_limit_bytes=V7X_VMEM_LIMIT_BYTES)


def _mm(a, b, dims=NN):
    return lax.dot_general(a.astype(BF16), b.astype(BF16), dims, preferred_element_type=F32)


def _split(a):
    hi = a.astype(BF16)
    lo = (a - hi.astype(F32)).astype(BF16)
    return hi, lo


def _mm3(a, b, dims=NN):
    ah, al = _split(a)
    bh, bl = _split(b)
    d = lambda x, y: lax.dot_general(x, y, dims, preferred_element_type=F32)
    return d(ah, bh) + d(ah, bl) + d(al, bh)


def _mm2(a, b_exact, dims=NN):
    ah, al = _split(a)
    d = lambda x: lax.dot_general(x, b_exact, dims, preferred_element_type=F32)
    return d(ah) + d(al)


def _pack_bf16_pair(x):
    n = x.shape[1] // 2
    bits = lax.bitcast_convert_type(x.astype(BF16).astype(F32), jnp.int32)
    return ((bits[:, :n] >> 16) & 0xFFFF) | bits[:, n:]


def _unpack_bf16_pair(u):
    lo = lax.bitcast_convert_type(u << 16, F32)
    hi = lax.bitcast_convert_type(u & jnp.int32(-65536), F32)
    return lo, hi


def _rms(x, eps=RMS_EPS):
    return x * lax.rsqrt(jnp.mean(x * x, axis=-1, keepdims=True) + eps)


def _sigmoid(x):
    return 1.0 / (1.0 + jnp.exp(-x))


def _silu(x):
    return x * _sigmoid(x)


def _adaln_kernel(c_ref, w_ref, b_ref, o_ref):
    c = c_ref[...]
    o_ref[0] = _mm3(_silu(c), w_ref[0]) + b_ref[0]


def _adaln(c, w_ada, b_ada):
    L, D, N = w_ada.shape
    Bn = c.shape[0]
    tn = min(N, 1536)
    return pl.pallas_call(
        _adaln_kernel,
        grid=(L, N // tn),
        in_specs=[
            pl.BlockSpec((Bn, D), lambda l, j: (0, 0)),
            pl.BlockSpec((1, D, tn), lambda l, j: (l, 0, j)),
            pl.BlockSpec((1, 1, tn), lambda l, j: (l, 0, j)),
        ],
        out_specs=pl.BlockSpec((1, Bn, tn), lambda l, j: (l, 0, j)),
        out_shape=jax.ShapeDtypeStruct((L, Bn, N), F32),
        compiler_params=_cparams("arbitrary", "arbitrary"),
        name="adaln",
    )(c, w_ada, b_ada.reshape(L, 1, N))


def _inproj_kernel(x_ref, g_ref, sc_ref, sh_ref, wa_ref, wbc_ref, oa_ref, obc_ref):
    x = x_ref[0]
    h = _rms(x) * g_ref[...] * (1.0 + sc_ref[0]) + sh_ref[0]
    hb = h.astype(BF16)
    oa_ref[0] = jnp.dot(hb, wa_ref[...], preferred_element_type=F32).astype(BF16)
    obc_ref[0] = jnp.dot(hb, wbc_ref[...], preferred_element_type=F32)


def _inproj(x, g, sc, sh, wa, wbc, tm):
    Bn, S, D = x.shape
    na, nbc = wa.shape[1], wbc.shape[1]
    return pl.pallas_call(
        _inproj_kernel,
        grid=(Bn, S // tm),
        in_specs=[
            pl.BlockSpec((1, tm, D), lambda b, i: (b, i, 0)),
            pl.BlockSpec((1, D), lambda b, i: (0, 0)),
            pl.BlockSpec((1, 1, D), lambda b, i: (b, 0, 0)),
            pl.BlockSpec((1, 1, D), lambda b, i: (b, 0, 0)),
            pl.BlockSpec((D, na), lambda b, i: (0, 0)),
            pl.BlockSpec((D, nbc), lambda b, i: (0, 0)),
        ],
        out_specs=[
            pl.BlockSpec((1, tm, na), lambda b, i: (b, i, 0)),
            pl.BlockSpec((1, tm, nbc), lambda b, i: (b, i, 0)),
        ],
        out_shape=[
            jax.ShapeDtypeStruct((Bn, S, na), BF16),
            jax.ShapeDtypeStruct((Bn, S, nbc), F32),
        ],
        compiler_params=_cparams("arbitrary", "arbitrary"),
        name="inproj",
    )(x, g.reshape(1, D), sc, sh, wa, wbc)


def _t5_bucket(dist):
    n = jnp.maximum(dist, 0)
    max_exact = N_BUCKETS // 2
    nf = jnp.maximum(n, 1).astype(F32)
    large = max_exact + (jnp.log(nf / max_exact) / math.log(MAX_DISTANCE / max_exact)
                         * (N_BUCKETS - max_exact)).astype(jnp.int32)
    large = jnp.minimum(large, N_BUCKETS - 1)
    return jnp.where(n < max_exact, n, large)


def _attn_band(table, tq):
    far = table[N_BUCKETS - 1].astype(F32)
    L = 3 * tq
    m = jnp.arange(L)
    m = jnp.where(m < 2 * tq, m, m - L)
    bands = []
    for off in (0, tq):
        dist = off - m
        vals = jnp.where(dist[None] >= 0, jnp.transpose(table[_t5_bucket(dist)].astype(F32)) - far[:, None],
                         NEG_BIG)
        toe = jnp.tile(vals, (1, tq))[:, :tq * (L - 1)].reshape(-1, tq, L - 1)
        bands.append(toe[:, :, :2 * tq])
    band = jnp.stack(bands)
    return jnp.concatenate([band, band], axis=2)


def _attn_kernel(lam_ref, q_ref, k_ref, v_ref, band_ref, g_ref, o_ref, *, tq, lambda_init):
    i = pl.program_id(2)
    q = q_ref[0] * jnp.asarray(A_QK_DIM ** -0.5, BF16)
    lane = lax.broadcasted_iota(jnp.int32, q.shape, 1)
    zero = jnp.zeros_like(q)
    qq = jnp.concatenate([jnp.where(lane < A_QK_DIM, q, zero),
                          jnp.where(lane >= A_QK_DIM, q, zero)], axis=0)

    kb0 = pl.multiple_of(jnp.maximum(i - 1, 0) * tq, tq)
    kb = k_ref[0, pl.ds(kb0, 2 * tq), :]
    vb = v_ref[0, pl.ds(kb0, 2 * tq), :]
    s = lax.dot_general(qq, kb, NT, preferred_element_type=F32) + band_ref[0, 0]
    m = jnp.max(s, axis=-1, keepdims=True)
    p = jnp.exp(s - m)
    l = jnp.sum(p, axis=-1, keepdims=True)
    acc = jnp.dot(p.astype(BF16), vb, preferred_element_type=F32)

    n_far = jnp.maximum(i - 1, 0)

    def logits(j):
        return lax.dot_general(qq, k_ref[0, pl.ds(pl.multiple_of(j * tq, tq), tq), :], NT,
                               preferred_element_type=F32)

    def body(j, carry):
        m, l, acc, s = carry
        s_next = logits(jnp.minimum(j + 1, n_far - 1))
        vj = v_ref[0, pl.ds(pl.multiple_of(j * tq, tq), tq), :]
        m_new = jnp.maximum(m, jnp.max(s, axis=-1, keepdims=True))
        alpha = jnp.exp(m - m_new)
        p = jnp.exp(s - m_new)
        l = alpha * l + jnp.sum(p, axis=-1, keepdims=True)
        acc = alpha * acc + jnp.dot(p.astype(BF16), vj, preferred_element_type=F32)
        return m_new, l, acc, s_next

    m, l, acc, _ = lax.fori_loop(0, n_far, body, (m, l, acc, logits(0)))

    lp = lam_ref[...]
    lam = (jnp.exp(jnp.sum(lp[0:1] * lp[1:2], axis=-1, keepdims=True))
           - jnp.exp(jnp.sum(lp[2:3] * lp[3:4], axis=-1, keepdims=True)) + lambda_init)
    o = acc / l
    o = o[:tq] - lam * o[tq:]
    o_ref[0] = _rms(o) * g_ref[...] * (1.0 - lambda_init)


def _diff_attention(pa, band, lam_par, subln_g, lambda_init, tq):
    Bn, S, _ = pa.shape
    W = A_HEAD_W
    kern = functools.partial(_attn_kernel, tq=tq, lambda_init=lambda_init)
    return pl.pallas_call(
        kern,
        grid=(Bn, A_HEADS, S // tq),
        in_specs=[
            pl.BlockSpec((4, A_QK_DIM), lambda b, h, i: (0, 0)),
            pl.BlockSpec((1, tq, W), lambda b, h, i: (b, i, h)),
            pl.BlockSpec((1, S, W), lambda b, h, i: (b, 0, A_HEADS + h)),
            pl.BlockSpec((1, S, W), lambda b, h, i: (b, 0, 2 * A_HEADS + h)),
            pl.BlockSpec((1, 1, 2 * tq, 2 * tq), lambda b, h, i: (jnp.minimum(i, 1), h, 0, 0)),
            pl.BlockSpec((1, W), lambda b, h, i: (0, 0)),
        ],
        out_specs=pl.BlockSpec((1, tq, W), lambda b, h, i: (b, i, h)),
        out_shape=jax.ShapeDtypeStruct((Bn, S, A_WIDTH), F32),
        compiler_params=_cparams("arbitrary", "arbitrary", "arbitrary"),
        name="diff_attn",
    )(lam_par, pa, pa, pa, band, subln_g.reshape(1, W))


def _head_ones(n):
    r = lax.broadcasted_iota(jnp.int32, (n, n), 0) // B_HEAD_DIM
    c = lax.broadcasted_iota(jnp.int32, (n, n), 1) // B_HEAD_DIM
    return (r == c).astype(BF16)


def _rwkv_prep_kernel(pb_ref, prev_ref, mu_ref, w0_ref, w2_ref, a0_ref, a2_ref, g2_ref,
                      kk_ref, ka_ref, rk_ref,
                      rt_ref, at_ref, kt_ref, bt_ref, v_ref, wc_ref, bonus_ref, g_ref, *, tm):
    i = pl.program_id(1)
    C = RWKV_CHUNK
    x = pb_ref[0]
    row = lax.broadcasted_iota(jnp.int32, x.shape, 0)
    last = prev_ref[0, 7:8, :] * (i > 0).astype(F32)
    prev = jnp.where(row == 0, last, pltpu.roll(x, 1, 0))
    p = x + (prev - x) * mu_ref[...]
    o1, o2, o3 = B_WIDTH, 2 * B_WIDTH, 3 * B_WIDTH
    r, k, v = p[:, :o1], p[:, o1:o2], p[:, o2:o3]
    lora = p[:, o3:o3 + B_DECAY_LORA + B_AAA_LORA]
    gd = p[:, o3 + B_DECAY_LORA + B_AAA_LORA:]

    z = -(w0_ref[...] + _mm3(jnp.tanh(lora), w2_ref[...]))
    softplus = jnp.maximum(z, 0.0) + jnp.log(1.0 + jnp.exp(-jnp.abs(z)))
    logw = -jnp.exp(-softplus - 0.5)
    a = _sigmoid(a0_ref[...] + _mm3(lora, a2_ref[...]))
    g_ref[0] = _mm3(_sigmoid(gd), g2_ref[...])

    ones = _head_ones(B_WIDTH)
    kk = k * kk_ref[...]
    kk = kk * lax.rsqrt(jnp.maximum(_mm2(kk * kk, ones), 1e-24))
    k2 = k * (1.0 + (a - 1.0) * ka_ref[...])
    bonus_ref[0] = _mm2(r * k2 * rk_ref[...], ones) * v

    t_in = lax.broadcasted_iota(jnp.int32, (tm, B_WIDTH), 0) % C
    cum = logw
    sh = 1
    while sh < C:
        cum = cum + jnp.where(t_in >= sh, pltpu.roll(cum, sh, 0), 0.0)
        sh *= 2
    n = tm // C
    wc_ref[0] = jnp.exp(jnp.sum(logw.reshape(n, C, B_WIDTH), axis=1))
    e_pos = jnp.exp(cum)
    e_neg = jnp.exp(-cum)
    rt_ref[0] = r * e_pos
    at_ref[0] = -kk * jnp.exp(cum - logw)
    kt_ref[0] = k2 * e_neg
    bt_ref[0] = kk * a * e_neg
    v_ref[0] = v


def _rwkv_prep(pbc, mu, w0, w2p, a0, a2p, g2, k_k, k_a, r_k, tm):
    Bn, S, _ = pbc.shape
    W = B_WIDTH
    nl = B_DECAY_LORA + B_AAA_LORA
    row = lambda a: a.reshape(1, -1)
    full = lambda shp: pl.BlockSpec(shp, lambda b, i: (0,) * len(shp))
    seq = pl.BlockSpec((1, tm, W), lambda b, i: (b, i, 0))
    seq_shape = jax.ShapeDtypeStruct((Bn, S, W), F32)
    n = tm // RWKV_CHUNK
    return pl.pallas_call(
        functools.partial(_rwkv_prep_kernel, tm=tm),
        grid=(Bn, S // tm),
        in_specs=[
            pl.BlockSpec((1, tm, B_COLS), lambda b, i: (b, i, 0)),
            pl.BlockSpec((1, 8, B_COLS), lambda b, i: (b, jnp.maximum(i * (tm // 8) - 1, 0), 0)),
            full((1, B_COLS)), full((1, W)), full((nl, W)), full((1, W)), full((nl, W)),
            full((B_GATE_LORA, W)), full((1, W)), full((1, W)), full((1, W)),
        ],
        out_specs=[seq, seq, seq, seq, seq,
                   pl.BlockSpec((1, n, W), lambda b, i: (b, i, 0)), seq, seq],
        out_shape=[seq_shape] * 5 + [jax.ShapeDtypeStruct((Bn, S // RWKV_CHUNK, W), F32)] + [seq_shape] * 2,
        compiler_params=_cparams("arbitrary", "arbitrary"),
        name="rwkv_prep",
    )(pbc, pbc, row(mu), row(w0), w2p, row(a0), a2p, g2, row(k_k), row(k_a), row(r_k))


def _rwkv_scan_kernel(rt_ref, at_ref, kt_ref, bt_ref, v_ref, wc_ref, bonus_ref, g_ref,
                      lng_ref, lnb_ref, o_ref, state, *, tt):
    C = RWKV_CHUNK
    W = B_WIDTH

    @pl.when(pl.program_id(1) == 0)
    def _():
        state[...] = jnp.zeros_like(state)

    lane_head = lax.broadcasted_iota(jnp.int32, (C, W), 1) // B_HEAD_DIM
    tt_i = lax.broadcasted_iota(jnp.int32, (C, W), 0)
    ss_i = lax.broadcasted_iota(jnp.int32, (C, W), 1) % C
    strict = tt_i > ss_i
    incl = tt_i >= ss_i
    eye = (tt_i == ss_i).astype(F32)
    ones = _head_ones(W)
    bd_mask = ones.astype(F32)

    head_mask = [(lane_head == h).astype(BF16) for h in range(B_HEADS)]

    def bd_split(x):
        xb = x.astype(BF16)
        return jnp.concatenate([xb * mk for mk in head_mask], axis=0)

    def mm_bd(a, b_bd, dims=NN):
        return lax.dot_general(a.astype(BF16), b_bd, dims, preferred_element_type=F32)

    def state_free(gi):
        G = range(RWKV_GROUP)
        sls = [pl.ds(pl.multiple_of((gi * RWKV_GROUP + j) * C, C), C) for j in G]
        rt = [rt_ref[0, sl, :] for sl in sls]
        at = [at_ref[0, sl, :] for sl in sls]
        kt = [kt_ref[0, sl, :] for sl in sls]
        bt = [bt_ref[0, sl, :] for sl in sls]
        v = [v_ref[0, sl, :] for sl in sls]
        wc = [wc_ref[0, pl.ds(gi * RWKV_GROUP + j, 1), :] for j in G]
        ar = [jnp.concatenate([at[j], rt[j]], axis=0) for j in G]
        bdb = [bd_split(bt[j]) for j in G]
        bdk = [bd_split(kt[j]) for j in G]
        a_b = [mm_bd(ar[j], bdb[j], NT) for j in G]
        a_k = [mm_bd(ar[j], bdk[j], NT) for j in G]
        lo = [jnp.where(strict, a_b[j][:C], 0.0) for j in G]
        a_ak = [jnp.where(strict, a_k[j][:C], 0.0) for j in G]
        a_rb = [jnp.where(incl, a_b[j][C:], 0.0) for j in G]
        a_rk = [jnp.where(incl, a_k[j][C:], 0.0) for j in G]
        pw = lo
        tinv = [eye + lo[j] for j in G]
        bdp = [bd_split(pw[j]) for j in G]
        span = 2
        while span < C:
            pw = [mm_bd(pw[j], bdp[j]) for j in G]
            bdp = [bd_split(pw[j]) for j in G]
            tinv = [tinv[j] + mm_bd(tinv[j], bdp[j]) for j in G]
            span *= 2
        bdv = [bd_split(v[j]) for j in G]
        bda = [bd_split(at[j]) for j in G]
        abar = [mm_bd(tinv[j], bda[j]) for j in G]
        akv = [bd_split(mm_bd(a_ak[j], bdv[j])) for j in G]
        u0 = [mm_bd(tinv[j], akv[j]) for j in G]
        y0 = [mm_bd(a_rk[j], bdv[j]) for j in G]
        kv = [_mm(v[j], kt[j] * wc[j], TN) * bd_mask for j in G]
        return [(jnp.concatenate([abar[j], rt[j]], axis=0), u0[j], y0[j], a_rb[j], bt[j] * wc[j], kv[j], wc[j])
                for j in G]

    def group(gi, carry):
        pre = state_free(gi)
        s = state[...]
        ys = []
        for abar_rt, u0, y0, a_rb, btw, kv, wc in pre:
            a_s = _mm(abar_rt, s, NT)
            u = a_s[:C] + u0
            ys.append(a_s[C:] + y0 + mm_bd(a_rb, bd_split(u)))
            s = s * wc + _mm(u, btw, TN) * bd_mask + kv
        state[...] = s
        y = jnp.concatenate(ys, axis=0)
        sl = pl.ds(pl.multiple_of(gi * (RWKV_GROUP * C), RWKV_GROUP * C), RWKV_GROUP * C)
        mean = _mm2(y, ones) * (1.0 / B_HEAD_DIM)
        d = y - mean
        var = _mm2(d * d, ones) * (1.0 / B_HEAD_DIM)
        yn = d * lax.rsqrt(var + B_LNX_EPS) * lng_ref[...] + lnb_ref[...]
        o_ref[0, sl, :] = (yn + bonus_ref[0, sl, :]) * g_ref[0, sl, :]
        return carry

    lax.fori_loop(0, tt // (RWKV_GROUP * C), group, 0)


def _rwkv_scan(rt, at, kt, bt, v, wc, bonus, g, lnx_g, lnx_b, tt):
    Bn, S, W = rt.shape
    n = tt // RWKV_CHUNK
    seq = pl.BlockSpec((1, tt, W), lambda b, i: (b, i, 0))
    vec = pl.BlockSpec((1, W), lambda b, i: (0, 0))
    return pl.pallas_call(
        functools.partial(_rwkv_scan_kernel, tt=tt),
        grid=(Bn, S // tt),
        in_specs=[seq, seq, seq, seq, seq, pl.BlockSpec((1, n, W), lambda b, i: (b, i, 0)), seq, seq, vec, vec],
        out_specs=seq,
        out_shape=jax.ShapeDtypeStruct((Bn, S, W), F32),
        scratch_shapes=[pltpu.VMEM((B_HEADS * B_HEAD_DIM, W), F32)],
        compiler_params=_cparams("arbitrary", "arbitrary"),
        name="rwkv_scan",
    )(rt, at, kt, bt, v, wc, bonus, g, lnx_g.reshape(1, W), lnx_b.reshape(1, W))


def _gmlp_kernel(pc_ref, lng_ref, lnb_ref, ws_ref, bs_ref, o_ref, *, tm):
    x = pc_ref[0]
    z = x * (0.5 * (1.0 + jnp.tanh(math.sqrt(2.0 / math.pi) * (x + 0.044715 * (x * x * x)))))
    u, v = z[:, :C_WIDTH], z[:, C_WIDTH:]
    mu = jnp.mean(v, axis=-1, keepdims=True)
    d = v - mu
    var = jnp.mean(d * d, axis=-1, keepdims=True)
    vn = d * lax.rsqrt(var + LN_EPS) * lng_ref[...] + lnb_ref[...]
    group = lax.broadcasted_iota(jnp.int32, (CHUNK, C_WIDTH), 1) // C_GROUP_DIM
    tril = (lax.broadcasted_iota(jnp.int32, (CHUNK, CHUNK), 0)
            >= lax.broadcasted_iota(jnp.int32, (CHUNK, CHUNK), 1))
    ws = [jnp.where(tril, ws_ref[gi], 0.0).astype(BF16) for gi in range(C_GROUPS)]
    for c in range(tm // CHUNK):
        sl = slice(c * CHUNK, (c + 1) * CHUNK)
        vc = vn[sl].astype(BF16)
        sv = bs_ref[...]
        for gi in range(C_GROUPS):
            t = jnp.dot(ws[gi], vc, preferred_element_type=F32)
            sv = sv + jnp.where(group == gi, t, 0.0)
        o_ref[0, sl, :] = u[sl] * sv


def _gmlp(pbc, ln_g, ln_b, w_s, b_s, tm):
    Bn, S, _ = pbc.shape
    bs_wide = jnp.repeat(jnp.transpose(b_s), C_GROUP_DIM, axis=1)
    return pl.pallas_call(
        functools.partial(_gmlp_kernel, tm=tm),
        grid=(Bn, S // tm),
        in_specs=[
            pl.BlockSpec((1, tm, C_COLS), lambda b, i: (b, i, B_COLS // C_COLS)),
            pl.BlockSpec((1, C_WIDTH), lambda b, i: (0, 0)),
            pl.BlockSpec((1, C_WIDTH), lambda b, i: (0, 0)),
            pl.BlockSpec((C_GROUPS, CHUNK, CHUNK), lambda b, i: (0, 0, 0)),
            pl.BlockSpec((CHUNK, C_WIDTH), lambda b, i: (0, 0)),
        ],
        out_specs=pl.BlockSpec((1, tm, C_WIDTH), lambda b, i: (b, i, 0)),
        out_shape=jax.ShapeDtypeStruct((Bn, S, C_WIDTH), F32),
        compiler_params=_cparams("arbitrary", "arbitrary"),
        name="gmlp",
    )(pbc, ln_g.reshape(1, -1), ln_b.reshape(1, -1), w_s, bs_wide)


def _mid_kernel(ya_ref, yb_ref, yc_ref, x_ref, woa_ref, wob_ref, woc_ref, gpost_ref, g1_ref,
                gpre_ref, sc_ref, sh_ref, wr_ref, ws1_ref, ws3_ref, ws2_ref,
                xo_ref, h_ref, score_ref, shared_ref):
    y = (_mm(ya_ref[0], woa_ref[...]) + _mm(yb_ref[0], wob_ref[...]) + _mm(yc_ref[0], woc_ref[...]))
    xn = x_ref[0] + g1_ref[0] * (_rms(y) * gpost_ref[...])
    xo_ref[0] = xn
    h = _rms(xn) * gpre_ref[...] * (1.0 + sc_ref[0]) + sh_ref[0]
    h_ref[0] = _pack_bf16_pair(h)
    score_ref[0] = _sigmoid(_mm3(wr_ref[...], h, NT))
    hb = h.astype(BF16)
    t = _silu(jnp.dot(hb, ws1_ref[...], preferred_element_type=F32)) * jnp.dot(
        hb, ws3_ref[...], preferred_element_type=F32)
    shared_ref[0] = jnp.dot(t.astype(BF16), ws2_ref[...], preferred_element_type=F32)


def _mid(ya, yb, yc, x, woa, wob, woc, gpost, g1, gpre, sc, sh, wr, ws1, ws3, ws2, tm):
    Bn, S, D = x.shape
    NR = wr.shape[0]
    F = ws1.shape[1]
    seq = lambda w: pl.BlockSpec((1, tm, w), lambda b, i: (b, i, 0))
    full = lambda shp: pl.BlockSpec(shp, lambda b, i: (0,) * len(shp))
    per_b = pl.BlockSpec((1, 1, D), lambda b, i: (b, 0, 0))
    return pl.pallas_call(
        _mid_kernel,
        grid=(Bn, S // tm),
        in_specs=[seq(A_WIDTH), seq(B_WIDTH), seq(C_WIDTH), seq(D),
                  full((A_WIDTH, D)), full((B_WIDTH, D)), full((C_WIDTH, D)),
                  full((1, D)), per_b, full((1, D)), per_b, per_b,
                  full((NR, D)), full((D, F)), full((D, F)), full((F, D))],
        out_specs=[seq(D), seq(D // 2), pl.BlockSpec((1, NR, tm), lambda b, i: (b, 0, i)), seq(D)],
        out_shape=[jax.ShapeDtypeStruct((Bn, S, D), F32), jax.ShapeDtypeStruct((Bn, S, D // 2), jnp.int32),
                   jax.ShapeDtypeStruct((Bn, NR, S), F32), jax.ShapeDtypeStruct((Bn, S, D), F32)],
        compiler_params=_cparams("arbitrary", "arbitrary"),
        name="mid",
    )(ya, yb, yc, x, woa, wob, woc, gpost.reshape(1, D), g1, gpre.reshape(1, D), sc, sh, wr, ws1, ws3, ws2)


def _first_argmax(vals, iota, n):
    m = jnp.max(vals, axis=0, keepdims=True)
    idx = jnp.min(jnp.where(vals == m, iota, n), axis=0, keepdims=True)
    return m, idx


def _route_kernel(sc_ref, bias_ref, e_ref, w_ref, r_ref, cnt_ref, carry, *, tm):
    @pl.when((pl.program_id(0) == 0) & (pl.program_id(1) == 0))
    def _():
        carry[...] = jnp.zeros_like(carry)

    G = EXPERTS_PER_GROUP
    s = sc_ref[0]
    biased = s + bias_ref[...]
    neg_inf = jnp.float32(-jnp.inf)
    io8 = lax.broadcasted_iota(jnp.int32, (G, tm), 0)
    gs_rows = []
    for g in range(N_GROUPS):
        blk = biased[g * G:(g + 1) * G]
        m1, i1 = _first_argmax(blk, io8, G)
        m2 = jnp.max(jnp.where(io8 == i1, neg_inf, blk), axis=0, keepdims=True)
        gs_rows.append(m1 + m2)
    gs = jnp.concatenate(gs_rows, axis=0)
    gio = lax.broadcasted_iota(jnp.int32, (N_GROUPS, tm), 0)
    gsel = jnp.zeros((N_GROUPS, tm), jnp.bool_)
    for _ in range(TOPK_GROUPS):
        _, gi = _first_argmax(gs, gio, N_GROUPS)
        pick = gio == gi
        gsel = gsel | pick
        gs = jnp.where(pick, neg_inf, gs)
    masked = jnp.concatenate(
        [jnp.where(gsel[g:g + 1], biased[g * G:(g + 1) * G], neg_inf) for g in range(N_GROUPS)], axis=0)

    eio = lax.broadcasted_iota(jnp.int32, (N_EXPERTS, tm), 0)
    picks, e_rows, s_rows = [], [], []
    for _ in range(TOP_K):
        _, ei = _first_argmax(masked, eio, N_EXPERTS)
        pick = eio == ei
        picks.append(pick)
        e_rows.append(ei)
        s_rows.append(jnp.sum(jnp.where(pick, s, 0.0), axis=0, keepdims=True))
        masked = jnp.where(pick, neg_inf, masked)
    top_s = jnp.concatenate(s_rows, axis=0)
    w_ref[...] = top_s / (jnp.sum(top_s, axis=0, keepdims=True) + 1e-20) * ROUTED_SCALE
    e_ref[...] = jnp.concatenate(e_rows, axis=0)

    sel = jnp.zeros((N_EXPERTS, tm), F32)
    for pick in picks:
        sel = sel + pick.astype(F32)
    before = (lax.broadcasted_iota(jnp.int32, (tm, tm), 0) < lax.broadcasted_iota(jnp.int32, (tm, tm), 1))
    pos = carry[...] + jnp.dot(sel.astype(BF16), before.astype(BF16), preferred_element_type=F32)
    r_ref[...] = jnp.concatenate(
        [jnp.sum(jnp.where(pick, pos, 0.0), axis=0, keepdims=True) for pick in picks], axis=0).astype(jnp.int32)
    total = carry[...] + jnp.sum(sel, axis=1, keepdims=True)
    carry[...] = total
    cnt_ref[...] = jnp.broadcast_to(total, cnt_ref.shape).astype(jnp.int32)


def _route(scores_t, e_bias, tm):
    Bn, _, S = scores_t.shape
    T = Bn * S
    nt = S // tm
    tok = pl.BlockSpec((TOP_K, tm), lambda b, i: (0, b * nt + i))
    return pl.pallas_call(
        functools.partial(_route_kernel, tm=tm),
        grid=(Bn, nt),
        in_specs=[pl.BlockSpec((1, N_EXPERTS, tm), lambda b, i: (b, 0, i)),
                  pl.BlockSpec((N_EXPERTS, 1), lambda b, i: (0, 0))],
        out_specs=[tok, tok, tok, pl.BlockSpec((N_EXPERTS, V7X_LANES), lambda b, i: (0, 0))],
        out_shape=[jax.ShapeDtypeStruct((TOP_K, T), jnp.int32), jax.ShapeDtypeStruct((TOP_K, T), F32),
                   jax.ShapeDtypeStruct((TOP_K, T), jnp.int32),
                   jax.ShapeDtypeStruct((N_EXPERTS, V7X_LANES), jnp.int32)],
        scratch_shapes=[pltpu.VMEM((N_EXPERTS, 1), F32)],
        compiler_params=_cparams("arbitrary", "arbitrary"),
        name="route",
    )(scores_t, e_bias.reshape(N_EXPERTS, 1))


def _dest_kernel(start_ref, e_ref, r_ref, o_ref):
    e = e_ref[...]
    acc = r_ref[...]
    for ex in range(N_EXPERTS):
        acc = acc + jnp.where(e == ex, start_ref[ex], 0)
    o_ref[0] = acc


def _dest_rows(pad_start, eidx, rank, tt):
    K_, T = eidx.shape
    grid_spec = pltpu.PrefetchScalarGridSpec(
        num_scalar_prefetch=1,
        grid=(T // tt,),
        in_specs=[pl.BlockSpec((K_, tt), lambda i, st: (0, i)), pl.BlockSpec((K_, tt), lambda i, st: (0, i))],
        out_specs=pl.BlockSpec((1, K_, tt), lambda i, st: (i, 0, 0)),
    )
    return pl.pallas_call(
        _dest_kernel,
        grid_spec=grid_spec,
        out_shape=jax.ShapeDtypeStruct((T // tt, K_, tt), jnp.int32),
        compiler_params=_cparams("arbitrary"),
        name="dest_rows",
    )(pad_start, eidx, rank)


def _expert_kernel(blk_e_ref, n_used_ref, n_valid_ref, x_ref, w1_ref, w3_ref, w2_ref, o_ref, w1b, w3b, w2b):
    i = pl.program_id(0)

    @pl.when((i == 0) | (blk_e_ref[i] != blk_e_ref[jnp.maximum(i - 1, 0)]))
    def _():
        w1b[...] = w1_ref[0].astype(BF16)
        w3b[...] = w3_ref[0].astype(BF16)
        w2b[...] = w2_ref[0].astype(BF16)

    @pl.when(i < n_used_ref[0])
    def _():
        row = lax.broadcasted_iota(jnp.int32, x_ref.shape, 0)
        x_lo, x_hi = _unpack_bf16_pair(jnp.where(row < n_valid_ref[i], x_ref[...], 0))
        x_lo, x_hi = x_lo.astype(BF16), x_hi.astype(BF16)
        half = x_lo.shape[1]

        def up(wb):
            return (jnp.dot(x_lo, wb[:half, :], preferred_element_type=F32)
                    + jnp.dot(x_hi, wb[half:, :], preferred_element_type=F32))

        t = _silu(up(w1b)) * up(w3b)
        o_ref[...] = _pack_bf16_pair(jnp.dot(t.astype(BF16), w2b[...], preferred_element_type=F32))


def _experts(blk_e, n_used, n_valid, xs, w1, w3, w2, layer):
    P, DP = xs.shape
    EB = EXPERT_BLOCK
    n_blocks = blk_e.shape[0]
    D, F = w1.shape[2], w1.shape[3]
    rows = pl.BlockSpec((EB, DP), lambda i, be, nu, nv: (jnp.minimum(i, nu[0] - 1), 0))
    grid_spec = pltpu.PrefetchScalarGridSpec(
        num_scalar_prefetch=3,
        grid=(n_blocks,),
        in_specs=[
            rows,
            pl.BlockSpec((None, 1, D, F), lambda i, be, nu, nv: (layer, be[i], 0, 0)),
            pl.BlockSpec((None, 1, D, F), lambda i, be, nu, nv: (layer, be[i], 0, 0)),
            pl.BlockSpec((None, 1, F, D), lambda i, be, nu, nv: (layer, be[i], 0, 0)),
        ],
        out_specs=rows,
        scratch_shapes=[pltpu.VMEM((D, F), BF16), pltpu.VMEM((D, F), BF16), pltpu.VMEM((F, D), BF16)],
    )
    return pl.pallas_call(
        _expert_kernel,
        grid_spec=grid_spec,
        out_shape=jax.ShapeDtypeStruct((P, DP), jnp.int32),
        compiler_params=_cparams("arbitrary"),
        name="experts",
    )(blk_e, n_used, n_valid, xs, w1, w3, w2)


def _block_layout(counts, n_blocks):
    EB = EXPERT_BLOCK
    padded = (counts + EB - 1) // EB * EB
    ex = jnp.arange(N_EXPERTS)
    pad_end = jnp.sum(jnp.where(ex[:, None] <= ex[None, :], padded[:, None], 0), axis=0)
    pad_start = pad_end - padded
    blk_row = (jnp.arange(n_blocks) * EB)[:, None]
    blk_e = jnp.minimum(jnp.sum((pad_end[None, :] <= blk_row).astype(jnp.int32), axis=1), N_EXPERTS - 1)
    n_used = (jnp.sum(padded) // EB).astype(jnp.int32).reshape(1)
    mine = (pad_start[None, :] <= blk_row) & (blk_row < pad_end[None, :])
    n_valid = jnp.sum(jnp.where(mine, jnp.clip(counts[None, :] - (blk_row - pad_start[None, :]), 0, EB), 0), axis=1)
    return pad_start.astype(jnp.int32), blk_e.astype(jnp.int32), n_used, n_valid.astype(jnp.int32)


SC_GATHER_ROWS = 64


def _sc_gather_rows(table, idx):
    info = plsc.get_sparse_core_info()
    nc, ns = info.num_cores, info.num_subcores
    M = idx.shape[0]
    W = table.shape[1]
    b = SC_GATHER_ROWS
    per_worker = M // (nc * ns)
    steps = per_worker // b
    assert per_worker * nc * ns == M and steps * b == per_worker and steps % 2 == 0
    mesh = plsc.VectorSubcoreMesh(core_axis_name="c", subcore_axis_name="s")

    @functools.partial(
        pl.kernel, mesh=mesh,
        out_type=jax.ShapeDtypeStruct((M, W), table.dtype),
        scratch_types=[pltpu.VMEM((2, b), jnp.int32), pltpu.VMEM((2, b, W), table.dtype),
                       pltpu.SemaphoreType.DMA, pltpu.SemaphoreType.DMA],
        name="sc_gather_rows",
    )
    def gather(table_hbm, idx_hbm, out_hbm, idx_v, rows_v, sem0, sem1):
        wid = lax.axis_index("s") * nc + lax.axis_index("c")
        sems = (sem0, sem1)

        def base(s):
            return pl.multiple_of(wid * per_worker + s * b, b)

        def gather_copy(slot):
            return pltpu.make_async_copy(table_hbm.at[idx_v.at[slot]], rows_v.at[slot], sems[slot])

        def start(s, slot):
            pltpu.sync_copy(idx_hbm.at[pl.ds(base(s), b)], idx_v.at[slot])
            gather_copy(slot).start()

        def finish(s, slot):
            gather_copy(slot).wait()
            pltpu.sync_copy(rows_v.at[slot], out_hbm.at[pl.ds(base(s), b)])

        start(0, 0)

        @pl.loop(0, steps, step=2)
        def _(s):
            start(s + 1, 1)
            finish(s, 0)

            @pl.when(s + 2 < steps)
            def _():
                start(s + 2, 0)

            finish(s + 1, 1)

    return gather(table, idx)


def _sc_scatter_rows(rows, idx, n_out):
    info = plsc.get_sparse_core_info()
    nc, ns = info.num_cores, info.num_subcores
    T, W = rows.shape
    G, K_, b = idx.shape
    steps = G // (nc * ns)
    assert steps * nc * ns == G and G * b == T
    mesh = plsc.VectorSubcoreMesh(core_axis_name="c", subcore_axis_name="s")

    @functools.partial(
        pl.kernel, mesh=mesh,
        out_type=jax.ShapeDtypeStruct((n_out, W), rows.dtype),
        scratch_types=[pltpu.VMEM((K_, b), jnp.int32), pltpu.VMEM((b, W), rows.dtype), pltpu.SemaphoreType.DMA],
        name="sc_scatter_rows",
    )
    def scatter(rows_hbm, idx_hbm, out_hbm, idx_v, rows_v, sem):
        wid = lax.axis_index("s") * nc + lax.axis_index("c")

        @pl.loop(0, steps)
        def _(s):
            g = wid * steps + s
            pltpu.sync_copy(idx_hbm.at[g], idx_v)
            pltpu.sync_copy(rows_hbm.at[pl.ds(pl.multiple_of(g * b, b), b)], rows_v)
            for k in range(K_):
                pltpu.async_copy(rows_v, out_hbm.at[idx_v.at[k]], sem).wait()

    return scatter(rows, idx)


def _combine_dense_kernel(rows_ref, w_ref, x_ref, shared_ref, gpost_ref, g2_ref, o_ref):
    w = w_ref[...]
    tt, half = rows_ref.shape[1], rows_ref.shape[2]
    y_lo = jnp.zeros((tt, half), F32)
    y_hi = jnp.zeros((tt, half), F32)
    for k in range(TOP_K):
        lo, hi = _unpack_bf16_pair(rows_ref[k])
        y_lo = y_lo + w[:, k:k + 1] * lo
        y_hi = y_hi + w[:, k:k + 1] * hi
    y = shared_ref[0] + jnp.concatenate([y_lo, y_hi], axis=1)
    o_ref[0] = x_ref[0] + g2_ref[0] * (_rms(y) * gpost_ref[...])


def _combine_dense(rows, w_tok, x, shared, gpost, g2, tt):
    Bn, S, D = x.shape
    K_, T, DP = rows.shape
    nt = S // tt
    seq = pl.BlockSpec((1, tt, D), lambda b, i: (b, i, 0))
    return pl.pallas_call(
        _combine_dense_kernel,
        grid=(Bn, nt),
        in_specs=[pl.BlockSpec((K_, tt, DP), lambda b, i: (0, b * nt + i, 0)),
                  pl.BlockSpec((tt, K_), lambda b, i: (b * nt + i, 0)), seq, seq,
                  pl.BlockSpec((1, D), lambda b, i: (0, 0)), pl.BlockSpec((1, 1, D), lambda b, i: (b, 0, 0))],
        out_specs=seq,
        out_shape=jax.ShapeDtypeStruct((Bn, S, D), F32),
        compiler_params=_cparams("arbitrary", "arbitrary"),
        name="combine_dense",
    )(rows, w_tok, x, shared, gpost.reshape(1, D), g2)


def kernel(x, c, w_ada, b_ada, norm_pre_mix, norm_post_mix, norm_pre_ffn, norm_post_ffn, w_in, w_out, rel_bias_table, diff_lambda, diff_subln, rwkv_mu, rwkv_w0, rwkv_w2, rwkv_a0, rwkv_a2, rwkv_g2, rwkv_k_k, rwkv_k_a, rwkv_r_k, rwkv_lnx_g, rwkv_lnx_b, gmlp_ln_g, gmlp_ln_b, gmlp_w_s, gmlp_b_s, router_w, router_bias, exp_w1, exp_w3, exp_w2, shared_w1, shared_w3, shared_w2):
    Bn, S, D = x.shape
    depth = w_ada.shape[0]
    tm = min(256, S)
    tq = min(256, S // 2)
    t_rwkv = min(512, S)

    mod = _adaln(c, w_ada, b_ada)
    band = _attn_band(rel_bias_table, tq)
    zpad = jnp.zeros((B_DECAY_LORA, B_WIDTH), F32)
    for l in range(depth):
        sh1, sc1, g1, sh2, sc2, g2 = [m.reshape(Bn, 1, D) for m in jnp.split(mod[l], 6, axis=-1)]
        w_in_b = w_in[l].astype(BF16)
        pa, pbc = _inproj(x, norm_pre_mix[l], sc1, sh1, w_in_b[:, :A_COLS], w_in_b[:, A_COLS:], tm)
        lambda_init = 0.8 - 0.6 * math.exp(-0.3 * l)
        ya = _diff_attention(pa, band, diff_lambda[l], diff_subln[l], lambda_init, tq)
        prep = _rwkv_prep(pbc, rwkv_mu[l], rwkv_w0[l], jnp.concatenate([rwkv_w2[l], zpad], axis=0),
                          rwkv_a0[l], jnp.concatenate([zpad, rwkv_a2[l]], axis=0), rwkv_g2[l],
                          rwkv_k_k[l], rwkv_k_a[l], rwkv_r_k[l].reshape(-1), t_rwkv)
        yb = _rwkv_scan(*prep, rwkv_lnx_g[l], rwkv_lnx_b[l], t_rwkv)
        yc = _gmlp(pbc, gmlp_ln_g[l], gmlp_ln_b[l], gmlp_w_s[l], gmlp_b_s[l], tm)

        w_out_b = w_out[l].astype(BF16)
        wr_t = jnp.pad(jnp.transpose(router_w[l]), ((0, V7X_LANES - N_EXPERTS), (0, 0)))
        x, h, scores_t, shared = _mid(
            ya, yb, yc, x, w_out_b[:A_WIDTH], w_out_b[A_WIDTH:A_WIDTH + B_WIDTH], w_out_b[A_WIDTH + B_WIDTH:],
            norm_post_mix[l], g1, norm_pre_ffn[l], sc2, sh2, wr_t,
            shared_w1[l].astype(BF16), shared_w3[l].astype(BF16), shared_w2[l].astype(BF16), tm)

        T = Bn * S
        n_blocks = -(-T * TOP_K // EXPERT_BLOCK) + N_EXPERTS
        eidx, wgt, rank, cnt = _route(scores_t, router_bias[l], tm)
        pad_start, blk_e, n_used, n_valid = _block_layout(cnt[:, 0], n_blocks)
        dest = _dest_rows(pad_start, eidx, rank, tm)
        b = SC_GATHER_ROWS
        dest_sc = jnp.transpose(dest.reshape(T // tm, TOP_K, tm // b, b), (0, 2, 1, 3)).reshape(T // b, TOP_K, b)
        xs = _sc_scatter_rows(h.reshape(T, D // 2), dest_sc, n_blocks * EXPERT_BLOCK)
        ys = _experts(blk_e, n_used, n_valid, xs, exp_w1, exp_w3, exp_w2, l)
        dest_kt = jnp.transpose(dest, (1, 0, 2)).reshape(TOP_K * T)
        rows = _sc_gather_rows(ys, dest_kt).reshape(TOP_K, T, D // 2)
        x = _combine_dense(rows, jnp.transpose(wgt), x, shared, norm_post_ffn[l], g2, tm)
    return x
```

```python
import functools
import math

import jax
import jax.numpy as jnp
from jax import lax
from jax.experimental import pallas as pl
from jax.experimental.pallas import tpu as pltpu
from jax.experimental.pallas import tpu_sc as plsc

F32 = jnp.float32
BF16 = jnp.bfloat16

A_HEADS = 4
A_QK_DIM = 64
A_HEAD_W = 2 * A_QK_DIM
A_WIDTH = A_HEADS * A_HEAD_W
N_BUCKETS = 32
MAX_DISTANCE = 128
B_HEADS = 4
B_HEAD_DIM = 64
B_WIDTH = B_HEADS * B_HEAD_DIM
B_DECAY_LORA = 64
B_AAA_LORA = 64
B_GATE_LORA = 128
B_LNX_EPS = 64e-5
C_GROUPS = 4
C_GROUP_DIM = 64
C_WIDTH = C_GROUPS * C_GROUP_DIM
CHUNK = 128
A_COLS = 3 * A_WIDTH
B_COLS = 3 * B_WIDTH + B_DECAY_LORA + B_AAA_LORA + B_GATE_LORA
C_COLS = 2 * C_WIDTH
N_EXPERTS = 64
TOP_K = 8
N_GROUPS = 8
TOPK_GROUPS = 4
EXPERTS_PER_GROUP = N_EXPERTS // N_GROUPS
ROUTED_SCALE = 2.5
EXPERT_BLOCK = 512
RMS_EPS = 1e-6
LN_EPS = 1e-5
NEG_BIG = -1e30

V7X_LANES = 128
V7X_VMEM_LIMIT_BYTES = 56 * 1024 * 1024
RWKV_CHUNK = 64
RWKV_GROUP = 8

NN = (((1,), (0,)), ((), ()))
NT = (((1,), (1,)), ((), ()))
TN = (((0,), (0,)), ((), ()))


def _cparams(*sem):
    return pltpu.CompilerParams(dimension_semantics=sem, vmem_limit_bytes=V7X_VMEM_LIMIT_BYTES)


def _mm(a, b, dims=NN):
    return lax.dot_general(a.astype(BF16), b.astype(BF16), dims, preferred_element_type=F32)


def _split(a):
    hi = a.astype(BF16)
    lo = (a - hi.astype(F32)).astype(BF16)
    return hi, lo


def _mm3(a, b, dims=NN):
    ah, al = _split(a)
    bh, bl = _split(b)
    d = lambda x, y: lax.dot_general(x, y, dims, preferred_element_type=F32)
    return d(ah, bh) + d(ah, bl) + d(al, bh)


def _mm2(a, b_exact, dims=NN):
    ah, al = _split(a)
    d = lambda x: lax.dot_general(x, b_exact, dims, preferred_element_type=F32)
    return d(ah) + d(al)


def _pack_bf16_pair(x):
    n = x.shape[1] // 2
    bits = lax.bitcast_convert_type(x.astype(BF16).astype(F32), jnp.int32)
    return ((bits[:, :n] >> 16) & 0xFFFF) | bits[:, n:]


def _unpack_bf16_pair(u):
    lo = lax.bitcast_convert_type(u << 16, F32)
    hi = lax.bitcast_convert_type(u & jnp.int32(-65536), F32)
    return lo, hi


def _rms(x, eps=RMS_EPS):
    return x * lax.rsqrt(jnp.mean(x * x, axis=-1, keepdims=True) + eps)


def _sigmoid(x):
    return 1.0 / (1.0 + jnp.exp(-x))


def _silu(x):
    return x * _sigmoid(x)


def _adaln_kernel(c_ref, w_ref, b_ref, o_ref):
    c = c_ref[...]
    o_ref[0] = _mm3(_silu(c), w_ref[0]) + b_ref[0]


def _adaln(c, w_ada, b_ada):
    L, D, N = w_ada.shape
    Bn = c.shape[0]
    tn = min(N, 1536)
    return pl.pallas_call(
        _adaln_kernel,
        grid=(L, N // tn),
        in_specs=[
            pl.BlockSpec((Bn, D), lambda l, j: (0, 0)),
            pl.BlockSpec((1, D, tn), lambda l, j: (l, 0, j)),
            pl.BlockSpec((1, 1, tn), lambda l, j: (l, 0, j)),
        ],
        out_specs=pl.BlockSpec((1, Bn, tn), lambda l, j: (l, 0, j)),
        out_shape=jax.ShapeDtypeStruct((L, Bn, N), F32),
        compiler_params=_cparams("arbitrary", "arbitrary"),
        name="adaln",
    )(c, w_ada, b_ada.reshape(L, 1, N))


def _inproj_kernel(x_ref, g_ref, sc_ref, sh_ref, wa_ref, wbc_ref, oa_ref, obc_ref):
    x = x_ref[0]
    h = _rms(x) * g_ref[...] * (1.0 + sc_ref[0]) + sh_ref[0]
    hb = h.astype(BF16)
    oa_ref[0] = jnp.dot(hb, wa_ref[...], preferred_element_type=F32).astype(BF16)
    obc_ref[0] = jnp.dot(hb, wbc_ref[...], preferred_element_type=F32)


def _inproj(x, g, sc, sh, wa, wbc, tm):
    Bn, S, D = x.shape
    na, nbc = wa.shape[1], wbc.shape[1]
    return pl.pallas_call(
        _inproj_kernel,
        grid=(Bn, S // tm),
        in_specs=[
            pl.BlockSpec((1, tm, D), lambda b, i: (b, i, 0)),
            pl.BlockSpec((1, D), lambda b, i: (0, 0)),
            pl.BlockSpec((1, 1, D), lambda b, i: (b, 0, 0)),
            pl.BlockSpec((1, 1, D), lambda b, i: (b, 0, 0)),
            pl.BlockSpec((D, na), lambda b, i: (0, 0)),
            pl.BlockSpec((D, nbc), lambda b, i: (0, 0)),
        ],
        out_specs=[
            pl.BlockSpec((1, tm, na), lambda b, i: (b, i, 0)),
            pl.BlockSpec((1, tm, nbc), lambda b, i: (b, i, 0)),
        ],
        out_shape=[
            jax.ShapeDtypeStruct((Bn, S, na), BF16),
            jax.ShapeDtypeStruct((Bn, S, nbc), F32),
        ],
        compiler_params=_cparams("arbitrary", "arbitrary"),
        name="inproj",
    )(x, g.reshape(1, D), sc, sh, wa, wbc)


def _t5_bucket(dist):
    n = jnp.maximum(dist, 0)
    max_exact = N_BUCKETS // 2
    nf = jnp.maximum(n, 1).astype(F32)
    large = max_exact + (jnp.log(nf / max_exact) / math.log(MAX_DISTANCE / max_exact)
                         * (N_BUCKETS - max_exact)).astype(jnp.int32)
    large = jnp.minimum(large, N_BUCKETS - 1)
    return jnp.where(n < max_exact, n, large)


def _attn_band(table, tq):
    far = table[N_BUCKETS - 1].astype(F32)
    L = 3 * tq
    m = jnp.arange(L)
    m = jnp.where(m < tq, m, m - L)
    bands = []
    for off in (0, tq):
        dist = m + off
        vals = jnp.where(dist[None] >= 0, jnp.transpose(table[_t5_bucket(dist)].astype(F32)) - far[:, None],
                         NEG_BIG)
        toe = jnp.tile(vals, (1, 2 * tq))[:, :2 * tq * (L - 1)].reshape(-1, 2 * tq, L - 1)
        bands.append(toe[:, :, :tq])
    return jnp.stack(bands)


def _attn_kernel(lam_ref, q_ref, k_ref, vt_ref, band_ref, g_ref, o_ref, *, tq, lambda_init):
    i = pl.program_id(2)
    q = q_ref[0] * jnp.asarray(A_QK_DIM ** -0.5, BF16)
    lane = lax.broadcasted_iota(jnp.int32, q.shape, 1)
    zero = jnp.zeros_like(q)
    qq = jnp.concatenate([jnp.where(lane < A_QK_DIM, q, zero),
                          jnp.where(lane >= A_QK_DIM, q, zero)], axis=0)

    kb0 = pl.multiple_of(jnp.maximum(i - 1, 0) * tq, tq)
    kb = k_ref[0, pl.ds(kb0, 2 * tq), :]
    band = band_ref[0, 0]
    s = lax.dot_general(kb, qq, NT, preferred_element_type=F32) + jnp.concatenate([band, band], axis=1)
    m = jnp.max(s, axis=0, keepdims=True)
    p = jnp.exp(s - m)
    l = jnp.sum(p, axis=0, keepdims=True)
    acc = jnp.dot(vt_ref[0, :, pl.ds(kb0, 2 * tq)], p.astype(BF16), preferred_element_type=F32)

    n_far = jnp.maximum(i - 1, 0)

    def logits(j):
        return lax.dot_general(k_ref[0, pl.ds(pl.multiple_of(j * tq, tq), tq), :], qq, NT,
                               preferred_element_type=F32)

    def body(j, carry):
        m, l, acc, s = carry
        s_next = logits(jnp.minimum(j + 1, n_far - 1))
        vtj = vt_ref[0, :, pl.ds(pl.multiple_of(j * tq, tq), tq)]
        m_new = jnp.maximum(m, jnp.max(s, axis=0, keepdims=True))
        alpha = jnp.exp(m - m_new)
        p = jnp.exp(s - m_new)
        l = alpha * l + jnp.sum(p, axis=0, keepdims=True)
        acc = alpha * acc + jnp.dot(vtj, p.astype(BF16), preferred_element_type=F32)
        return m_new, l, acc, s_next

    m, l, acc, _ = lax.fori_loop(0, n_far, body, (m, l, acc, logits(0)))

    lp = lam_ref[...]
    lam = (jnp.exp(jnp.sum(lp[0:1] * lp[1:2], axis=-1, keepdims=True))
           - jnp.exp(jnp.sum(lp[2:3] * lp[3:4], axis=-1, keepdims=True)) + lambda_init)
    o = acc / l
    o = o[:, :tq] - lam * o[:, tq:]
    o = o * lax.rsqrt(jnp.mean(o * o, axis=0, keepdims=True) + RMS_EPS) * g_ref[...] * (1.0 - lambda_init)
    o_ref[0] = jnp.transpose(o)


def _diff_attention(pa, vt, band_t, lam_par, subln_g, lambda_init, tq):
    Bn, S, _ = pa.shape
    W = A_HEAD_W
    kern = functools.partial(_attn_kernel, tq=tq, lambda_init=lambda_init)
    return pl.pallas_call(
        kern,
        grid=(Bn, A_HEADS, S // tq),
        in_specs=[
            pl.BlockSpec((4, A_QK_DIM), lambda b, h, i: (0, 0)),
            pl.BlockSpec((1, tq, W), lambda b, h, i: (b, i, h)),
            pl.BlockSpec((1, S, W), lambda b, h, i: (b, 0, A_HEADS + h)),
            pl.BlockSpec((1, W, S), lambda b, h, i: (b, h, 0)),
            pl.BlockSpec((1, 1, 2 * tq, tq), lambda b, h, i: (jnp.minimum(i, 1), h, 0, 0)),
            pl.BlockSpec((W, 1), lambda b, h, i: (0, 0)),
        ],
        out_specs=pl.BlockSpec((1, tq, W), lambda b, h, i: (b, i, h)),
        out_shape=jax.ShapeDtypeStruct((Bn, S, A_WIDTH), F32),
        compiler_params=_cparams("arbitrary", "arbitrary", "arbitrary"),
        name="diff_attn",
    )(lam_par, pa, pa, vt, band_t, subln_g.reshape(W, 1))


def _head_ones(n):
    r = lax.broadcasted_iota(jnp.int32, (n, n), 0) // B_HEAD_DIM
    c = lax.broadcasted_iota(jnp.int32, (n, n), 1) // B_HEAD_DIM
    return (r == c).astype(BF16)


def _rwkv_prep_kernel(pb_ref, prev_ref, mu_ref, w0_ref, w2_ref, a0_ref, a2_ref, g2_ref,
                      kk_ref, ka_ref, rk_ref,
                      rt_ref, at_ref, kt_ref, bt_ref, v_ref, wc_ref, bonus_ref, g_ref, *, tm):
    i = pl.program_id(1)
    C = RWKV_CHUNK
    x = pb_ref[0]
    row = lax.broadcasted_iota(jnp.int32, x.shape, 0)
    last = prev_ref[0, 7:8, :] * (i > 0).astype(F32)
    prev = jnp.where(row == 0, last, pltpu.roll(x, 1, 0))
    p = x + (prev - x) * mu_ref[...]
    o1, o2, o3 = B_WIDTH, 2 * B_WIDTH, 3 * B_WIDTH
    r, k, v = p[:, :o1], p[:, o1:o2], p[:, o2:o3]
    lora = p[:, o3:o3 + B_DECAY_LORA + B_AAA_LORA]
    gd = p[:, o3 + B_DECAY_LORA + B_AAA_LORA:]

    z = -(w0_ref[...] + _mm3(jnp.tanh(lora), w2_ref[...]))
    softplus = jnp.maximum(z, 0.0) + jnp.log(1.0 + jnp.exp(-jnp.abs(z)))
    logw = -jnp.exp(-softplus - 0.5)
    a = _sigmoid(a0_ref[...] + _mm3(lora, a2_ref[...]))
    g_ref[0] = _mm3(_sigmoid(gd), g2_ref[...])

    ones = _head_ones(B_WIDTH)
    kk = k * kk_ref[...]
    kk = kk * lax.rsqrt(jnp.maximum(_mm2(kk * kk, ones), 1e-24))
    k2 = k * (1.0 + (a - 1.0) * ka_ref[...])
    bonus_ref[0] = _mm2(r * k2 * rk_ref[...], ones) * v

    t_in = lax.broadcasted_iota(jnp.int32, (tm, B_WIDTH), 0) % C
    cum = logw
    sh = 1
    while sh < C:
        cum = cum + jnp.where(t_in >= sh, pltpu.roll(cum, sh, 0), 0.0)
        sh *= 2
    n = tm // C
    wc_ref[0] = jnp.exp(jnp.sum(logw.reshape(n, C, B_WIDTH), axis=1))
    e_pos = jnp.exp(cum)
    e_neg = jnp.exp(-cum)
    rt_ref[0] = r * e_pos
    at_ref[0] = -kk * jnp.exp(cum - logw)
    kt_ref[0] = k2 * e_neg
    bt_ref[0] = kk * a * e_neg
    v_ref[0] = v


def _rwkv_prep(pbc, mu, w0, w2p, a0, a2p, g2, k_k, k_a, r_k, tm):
    Bn, S, _ = pbc.shape
    W = B_WIDTH
    nl = B_DECAY_LORA + B_AAA_LORA
    row = lambda a: a.reshape(1, -1)
    full = lambda shp: pl.BlockSpec(shp, lambda b, i: (0,) * len(shp))
    seq = pl.BlockSpec((1, tm, W), lambda b, i: (b, i, 0))
    seq_shape = jax.ShapeDtypeStruct((Bn, S, W), F32)
    n = tm // RWKV_CHUNK
    return pl.pallas_call(
        functools.partial(_rwkv_prep_kernel, tm=tm),
        grid=(Bn, S // tm),
        in_specs=[
            pl.BlockSpec((1, tm, B_COLS), lambda b, i: (b, i, 0)),
            pl.BlockSpec((1, 8, B_COLS), lambda b, i: (b, jnp.maximum(i * (tm // 8) - 1, 0), 0)),
            full((1, B_COLS)), full((1, W)), full((nl, W)), full((1, W)), full((nl, W)),
            full((B_GATE_LORA, W)), full((1, W)), full((1, W)), full((1, W)),
        ],
        out_specs=[seq, seq, seq, seq, seq,
                   pl.BlockSpec((1, n, W), lambda b, i: (b, i, 0)), seq, seq],
        out_shape=[seq_shape] * 5 + [jax.ShapeDtypeStruct((Bn, S // RWKV_CHUNK, W), F32)] + [seq_shape] * 2,
        compiler_params=_cparams("arbitrary", "arbitrary"),
        name="rwkv_prep",
    )(pbc, pbc, row(mu), row(w0), w2p, row(a0), a2p, g2, row(k_k), row(k_a), row(r_k))


def _rwkv_scan_kernel(rt_ref, at_ref, kt_ref, bt_ref, v_ref, wc_ref, bonus_ref, g_ref,
                      lng_ref, lnb_ref, o_ref, state, *, tt):
    C = RWKV_CHUNK
    W = B_WIDTH

    @pl.when(pl.program_id(1) == 0)
    def _():
        state[...] = jnp.zeros_like(state)

    lane_head = lax.broadcasted_iota(jnp.int32, (C, W), 1) // B_HEAD_DIM
    tt_i = lax.broadcasted_iota(jnp.int32, (C, W), 0)
    ss_i = lax.broadcasted_iota(jnp.int32, (C, W), 1) % C
    strict = tt_i > ss_i
    incl = tt_i >= ss_i
    eye = (tt_i == ss_i).astype(F32)
    ones = _head_ones(W)
    bd_mask = ones.astype(F32)

    head_mask = [(lane_head == h).astype(BF16) for h in range(B_HEADS)]

    def bd_split(x):
        xb = x.astype(BF16)
        return jnp.concatenate([xb * mk for mk in head_mask], axis=0)

    def mm_bd(a, b_bd, dims=NN):
        return lax.dot_general(a.astype(BF16), b_bd, dims, preferred_element_type=F32)

    def state_free(gi):
        G = range(RWKV_GROUP)
        sls = [pl.ds(pl.multiple_of((gi * RWKV_GROUP + j) * C, C), C) for j in G]
        rt = [rt_ref[0, sl, :] for sl in sls]
        at = [at_ref[0, sl, :] for sl in sls]
        kt = [kt_ref[0, sl, :] for sl in sls]
        bt = [bt_ref[0, sl, :] for sl in sls]
        v = [v_ref[0, sl, :] for sl in sls]
        wc = [wc_ref[0, pl.ds(gi * RWKV_GROUP + j, 1), :] for j in G]
        ar = [jnp.concatenate([at[j], rt[j]], axis=0) for j in G]
        bdb = [bd_split(bt[j]) for j in G]
        bdk = [bd_split(kt[j]) for j in G]
        a_b = [mm_bd(ar[j], bdb[j], NT) for j in G]
        a_k = [mm_bd(ar[j], bdk[j], NT) for j in G]
        lo = [jnp.where(strict, a_b[j][:C], 0.0) for j in G]
        a_ak = [jnp.where(strict, a_k[j][:C], 0.0) for j in G]
        a_rb = [jnp.where(incl, a_b[j][C:], 0.0) for j in G]
        a_rk = [jnp.where(incl, a_k[j][C:], 0.0) for j in G]
        pw = lo
        tinv = [eye + lo[j] for j in G]
        bdp = [bd_split(pw[j]) for j in G]
        span = 2
        while span < C:
            pw = [mm_bd(pw[j], bdp[j]) for j in G]
            bdp = [bd_split(pw[j]) for j in G]
            tinv = [tinv[j] + mm_bd(tinv[j], bdp[j]) for j in G]
            span *= 2
        bdv = [bd_split(v[j]) for j in G]
        bda = [bd_split(at[j]) for j in G]
        abar = [mm_bd(tinv[j], bda[j]) for j in G]
        akv = [bd_split(mm_bd(a_ak[j], bdv[j])) for j in G]
        u0 = [mm_bd(tinv[j], akv[j]) for j in G]
        y0 = [mm_bd(a_rk[j], bdv[j]) for j in G]
        kv = [_mm(v[j], kt[j] * wc[j], TN) * bd_mask for j in G]
        return [(jnp.concatenate([abar[j], rt[j]], axis=0), u0[j], y0[j], a_rb[j], bt[j] * wc[j], kv[j], wc[j])
                for j in G]

    def group(gi, carry):
        pre = state_free(gi)
        s = state[...]
        ys = []
        for abar_rt, u0, y0, a_rb, btw, kv, wc in pre:
            a_s = _mm(abar_rt, s, NT)
            u = a_s[:C] + u0
            ys.append(a_s[C:] + y0 + mm_bd(a_rb, bd_split(u)))
            s = s * wc + _mm(u, btw, TN) * bd_mask + kv
        state[...] = s
        y = jnp.concatenate(ys, axis=0)
        sl = pl.ds(pl.multiple_of(gi * (RWKV_GROUP * C), RWKV_GROUP * C), RWKV_GROUP * C)
        mean = _mm2(y, ones) * (1.0 / B_HEAD_DIM)
        d = y - mean
        var = _mm2(d * d, ones) * (1.0 / B_HEAD_DIM)
        yn = d * lax.rsqrt(var + B_LNX_EPS) * lng_ref[...] + lnb_ref[...]
        o_ref[0, sl, :] = (yn + bonus_ref[0, sl, :]) * g_ref[0, sl, :]
        return carry

    lax.fori_loop(0, tt // (RWKV_GROUP * C), group, 0)


def _rwkv_scan(rt, at, kt, bt, v, wc, bonus, g, lnx_g, lnx_b, tt):
    Bn, S, W = rt.shape
    n = tt // RWKV_CHUNK
    seq = pl.BlockSpec((1, tt, W), lambda b, i: (b, i, 0))
    vec = pl.BlockSpec((1, W), lambda b, i: (0, 0))
    return pl.pallas_call(
        functools.partial(_rwkv_scan_kernel, tt=tt),
        grid=(Bn, S // tt),
        in_specs=[seq, seq, seq, seq, seq, pl.BlockSpec((1, n, W), lambda b, i: (b, i, 0)), seq, seq, vec, vec],
        out_specs=seq,
        out_shape=jax.ShapeDtypeStruct((Bn, S, W), F32),
        scratch_shapes=[pltpu.VMEM((B_HEADS * B_HEAD_DIM, W), F32)],
        compiler_params=_cparams("arbitrary", "arbitrary"),
        name="rwkv_scan",
    )(rt, at, kt, bt, v, wc, bonus, g, lnx_g.reshape(1, W), lnx_b.reshape(1, W))


def _gmlp_kernel(pc_ref, lng_ref, lnb_ref, ws_ref, bs_ref, o_ref, *, tm):
    x = pc_ref[0]
    z = x * (0.5 * (1.0 + jnp.tanh(math.sqrt(2.0 / math.pi) * (x + 0.044715 * (x * x * x)))))
    u, v = z[:, :C_WIDTH], z[:, C_WIDTH:]
    mu = jnp.mean(v, axis=-1, keepdims=True)
    d = v - mu
    var = jnp.mean(d * d, axis=-1, keepdims=True)
    vn = d * lax.rsqrt(var + LN_EPS) * lng_ref[...] + lnb_ref[...]
    group = lax.broadcasted_iota(jnp.int32, (CHUNK, C_WIDTH), 1) // C_GROUP_DIM
    tril = (lax.broadcasted_iota(jnp.int32, (CHUNK, CHUNK), 0)
            >= lax.broadcasted_iota(jnp.int32, (CHUNK, CHUNK), 1))
    ws = [jnp.where(tril, ws_ref[gi], 0.0).astype(BF16) for gi in range(C_GROUPS)]
    for c in range(tm // CHUNK):
        sl = slice(c * CHUNK, (c + 1) * CHUNK)
        vc = vn[sl].astype(BF16)
        sv = bs_ref[...]
        for gi in range(C_GROUPS):
            t = jnp.dot(ws[gi], vc, preferred_element_type=F32)
            sv = sv + jnp.where(group == gi, t, 0.0)
        o_ref[0, sl, :] = u[sl] * sv


def _gmlp(pbc, ln_g, ln_b, w_s, b_s, tm):
    Bn, S, _ = pbc.shape
    bs_wide = jnp.repeat(jnp.transpose(b_s), C_GROUP_DIM, axis=1)
    return pl.pallas_call(
        functools.partial(_gmlp_kernel, tm=tm),
        grid=(Bn, S // tm),
        in_specs=[
            pl.BlockSpec((1, tm, C_COLS), lambda b, i: (b, i, B_COLS // C_COLS)),
            pl.BlockSpec((1, C_WIDTH), lambda b, i: (0, 0)),
            pl.BlockSpec((1, C_WIDTH), lambda b, i: (0, 0)),
            pl.BlockSpec((C_GROUPS, CHUNK, CHUNK), lambda b, i: (0, 0, 0)),
            pl.BlockSpec((CHUNK, C_WIDTH), lambda b, i: (0, 0)),
        ],
        out_specs=pl.BlockSpec((1, tm, C_WIDTH), lambda b, i: (b, i, 0)),
        out_shape=jax.ShapeDtypeStruct((Bn, S, C_WIDTH), F32),
        compiler_params=_cparams("arbitrary", "arbitrary"),
        name="gmlp",
    )(pbc, ln_g.reshape(1, -1), ln_b.reshape(1, -1), w_s, bs_wide)


def _mid_kernel(ya_ref, yb_ref, yc_ref, x_ref, woa_ref, wob_ref, woc_ref, gpost_ref, g1_ref,
                gpre_ref, sc_ref, sh_ref, wr_ref, ws1_ref, ws3_ref, ws2_ref,
                xo_ref, h_ref, score_ref, shared_ref):
    y = (_mm(ya_ref[0], woa_ref[...]) + _mm(yb_ref[0], wob_ref[...]) + _mm(yc_ref[0], woc_ref[...]))
    xn = x_ref[0] + g1_ref[0] * (_rms(y) * gpost_ref[...])
    xo_ref[0] = xn
    h = _rms(xn) * gpre_ref[...] * (1.0 + sc_ref[0]) + sh_ref[0]
    h_ref[0] = _pack_bf16_pair(h)
    score_ref[0] = _sigmoid(_mm3(wr_ref[...], h, NT))
    hb = h.astype(BF16)
    t = _silu(jnp.dot(hb, ws1_ref[...], preferred_element_type=F32)) * jnp.dot(
        hb, ws3_ref[...], preferred_element_type=F32)
    shared_ref[0] = jnp.dot(t.astype(BF16), ws2_ref[...], preferred_element_type=F32)


def _mid(ya, yb, yc, x, woa, wob, woc, gpost, g1, gpre, sc, sh, wr, ws1, ws3, ws2, tm):
    Bn, S, D = x.shape
    NR = wr.shape[0]
    F = ws1.shape[1]
    seq = lambda w: pl.BlockSpec((1, tm, w), lambda b, i: (b, i, 0))
    full = lambda shp: pl.BlockSpec(shp, lambda b, i: (0,) * len(shp))
    per_b = pl.BlockSpec((1, 1, D), lambda b, i: (b, 0, 0))
    return pl.pallas_call(
        _mid_kernel,
        grid=(Bn, S // tm),
        in_specs=[seq(A_WIDTH), seq(B_WIDTH), seq(C_WIDTH), seq(D),
                  full((A_WIDTH, D)), full((B_WIDTH, D)), full((C_WIDTH, D)),
                  full((1, D)), per_b, full((1, D)), per_b, per_b,
                  full((NR, D)), full((D, F)), full((D, F)), full((F, D))],
        out_specs=[seq(D), seq(D // 2), pl.BlockSpec((1, NR, tm), lambda b, i: (b, 0, i)), seq(D)],
        out_shape=[jax.ShapeDtypeStruct((Bn, S, D), F32), jax.ShapeDtypeStruct((Bn, S, D // 2), jnp.int32),
                   jax.ShapeDtypeStruct((Bn, NR, S), F32), jax.ShapeDtypeStruct((Bn, S, D), F32)],
        compiler_params=_cparams("arbitrary", "arbitrary"),
        name="mid",
    )(ya, yb, yc, x, woa, wob, woc, gpost.reshape(1, D), g1, gpre.reshape(1, D), sc, sh, wr, ws1, ws3, ws2)


def _first_argmax(vals, iota, n):
    m = jnp.max(vals, axis=0, keepdims=True)
    idx = jnp.min(jnp.where(vals == m, iota, n), axis=0, keepdims=True)
    return m, idx


def _route_kernel(sc_ref, bias_ref, e_ref, w_ref, r_ref, cnt_ref, carry, *, tm):
    @pl.when((pl.program_id(0) == 0) & (pl.program_id(1) == 0))
    def _():
        carry[...] = jnp.zeros_like(carry)

    G = EXPERTS_PER_GROUP
    s = sc_ref[0]
    biased = s + bias_ref[...]
    neg_inf = jnp.float32(-jnp.inf)
    io8 = lax.broadcasted_iota(jnp.int32, (G, tm), 0)
    gs_rows = []
    for g in range(N_GROUPS):
        blk = biased[g * G:(g + 1) * G]
        m1, i1 = _first_argmax(blk, io8, G)
        m2 = jnp.max(jnp.where(io8 == i1, neg_inf, blk), axis=0, keepdims=True)
        gs_rows.append(m1 + m2)
    gs = jnp.concatenate(gs_rows, axis=0)
    gio = lax.broadcasted_iota(jnp.int32, (N_GROUPS, tm), 0)
    gsel = jnp.zeros((N_GROUPS, tm), jnp.bool_)
    for _ in range(TOPK_GROUPS):
        _, gi = _first_argmax(gs, gio, N_GROUPS)
        pick = gio == gi
        gsel = gsel | pick
        gs = jnp.where(pick, neg_inf, gs)
    masked = jnp.concatenate(
        [jnp.where(gsel[g:g + 1], biased[g * G:(g + 1) * G], neg_inf) for g in range(N_GROUPS)], axis=0)

    eio = lax.broadcasted_iota(jnp.int32, (N_EXPERTS, tm), 0)
    picks, e_rows, s_rows = [], [], []
    for _ in range(TOP_K):
        _, ei = _first_argmax(masked, eio, N_EXPERTS)
        pick = eio == ei
        picks.append(pick)
        e_rows.append(ei)
        s_rows.append(jnp.sum(jnp.where(pick, s, 0.0), axis=0, keepdims=True))
        masked = jnp.where(pick, neg_inf, masked)
    top_s = jnp.concatenate(s_rows, axis=0)
    w_ref[...] = top_s / (jnp.sum(top_s, axis=0, keepdims=True) + 1e-20) * ROUTED_SCALE
    e_ref[...] = jnp.concatenate(e_rows, axis=0)

    sel = jnp.zeros((N_EXPERTS, tm), F32)
    for pick in picks:
        sel = sel + pick.astype(F32)
    before = (lax.broadcasted_iota(jnp.int32, (tm, tm), 0) < lax.broadcasted_iota(jnp.int32, (tm, tm), 1))
    pos = carry[...] + jnp.dot(sel.astype(BF16), before.astype(BF16), preferred_element_type=F32)
    r_ref[...] = jnp.concatenate(
        [jnp.sum(jnp.where(pick, pos, 0.0), axis=0, keepdims=True) for pick in picks], axis=0).astype(jnp.int32)
    total = carry[...] + jnp.sum(sel, axis=1, keepdims=True)
    carry[...] = total
    cnt_ref[...] = jnp.broadcast_to(total, cnt_ref.shape).astype(jnp.int32)


def _route(scores_t, e_bias, tm):
    Bn, _, S = scores_t.shape
    T = Bn * S
    nt = S // tm
    tok = pl.BlockSpec((TOP_K, tm), lambda b, i: (0, b * nt + i))
    return pl.pallas_call(
        functools.partial(_route_kernel, tm=tm),
        grid=(Bn, nt),
        in_specs=[pl.BlockSpec((1, N_EXPERTS, tm), lambda b, i: (b, 0, i)),
                  pl.BlockSpec((N_EXPERTS, 1), lambda b, i: (0, 0))],
        out_specs=[tok, tok, tok, pl.BlockSpec((N_EXPERTS, V7X_LANES), lambda b, i: (0, 0))],
        out_shape=[jax.ShapeDtypeStruct((TOP_K, T), jnp.int32), jax.ShapeDtypeStruct((TOP_K, T), F32),
                   jax.ShapeDtypeStruct((TOP_K, T), jnp.int32),
                   jax.ShapeDtypeStruct((N_EXPERTS, V7X_LANES), jnp.int32)],
        scratch_shapes=[pltpu.VMEM((N_EXPERTS, 1), F32)],
        compiler_params=_cparams("arbitrary", "arbitrary"),
        name="route",
    )(scores_t, e_bias.reshape(N_EXPERTS, 1))


def _dest_kernel(start_ref, e_ref, r_ref, o_ref):
    e = e_ref[...]
    acc = r_ref[...]
    for ex in range(N_EXPERTS):
        acc = acc + jnp.where(e == ex, start_ref[ex], 0)
    o_ref[0] = acc


def _dest_rows(pad_start, eidx, rank, tt):
    K_, T = eidx.shape
    grid_spec = pltpu.PrefetchScalarGridSpec(
        num_scalar_prefetch=1,
        grid=(T // tt,),
        in_specs=[pl.BlockSpec((K_, tt), lambda i, st: (0, i)), pl.BlockSpec((K_, tt), lambda i, st: (0, i))],
        out_specs=pl.BlockSpec((1, K_, tt), lambda i, st: (i, 0, 0)),
    )
    return pl.pallas_call(
        _dest_kernel,
        grid_spec=grid_spec,
        out_shape=jax.ShapeDtypeStruct((T // tt, K_, tt), jnp.int32),
        compiler_params=_cparams("arbitrary"),
        name="dest_rows",
    )(pad_start, eidx, rank)


def _expert_kernel(blk_e_ref, n_used_ref, n_valid_ref, x_ref, w1_ref, w3_ref, w2_ref, o_ref, w1b, w3b, w2b):
    i = pl.program_id(0)

    @pl.when((i == 0) | (blk_e_ref[i] != blk_e_ref[jnp.maximum(i - 1, 0)]))
    def _():
        w1b[...] = w1_ref[0].astype(BF16)
        w3b[...] = w3_ref[0].astype(BF16)
        w2b[...] = w2_ref[0].astype(BF16)

    @pl.when(i < n_used_ref[0])
    def _():
        row = lax.broadcasted_iota(jnp.int32, x_ref.shape, 0)
        x_lo, x_hi = _unpack_bf16_pair(jnp.where(row < n_valid_ref[i], x_ref[...], 0))
        x_lo, x_hi = x_lo.astype(BF16), x_hi.astype(BF16)
        half = x_lo.shape[1]

        def up(wb):
            return (jnp.dot(x_lo, wb[:half, :], preferred_element_type=F32)
                    + jnp.dot(x_hi, wb[half:, :], preferred_element_type=F32))

        t = _silu(up(w1b)) * up(w3b)
        o_ref[...] = _pack_bf16_pair(jnp.dot(t.astype(BF16), w2b[...], preferred_element_type=F32))


def _experts(blk_e, n_used, n_valid, xs, w1, w3, w2, layer):
    P, DP = xs.shape
    EB = EXPERT_BLOCK
    n_blocks = blk_e.shape[0]
    D, F = w1.shape[2], w1.shape[3]
    rows = pl.BlockSpec((EB, DP), lambda i, be, nu, nv: (jnp.minimum(i, nu[0] - 1), 0))
    grid_spec = pltpu.PrefetchScalarGridSpec(
        num_scalar_prefetch=3,
        grid=(n_blocks,),
        in_specs=[
            rows,
            pl.BlockSpec((None, 1, D, F), lambda i, be, nu, nv: (layer, be[i], 0, 0)),
            pl.BlockSpec((None, 1, D, F), lambda i, be, nu, nv: (layer, be[i], 0, 0)),
            pl.BlockSpec((None, 1, F, D), lambda i, be, nu, nv: (layer, be[i], 0, 0)),
        ],
        out_specs=rows,
        scratch_shapes=[pltpu.VMEM((D, F), BF16), pltpu.VMEM((D, F), BF16), pltpu.VMEM((F, D), BF16)],
    )
    return pl.pallas_call(
        _expert_kernel,
        grid_spec=grid_spec,
        out_shape=jax.ShapeDtypeStruct((P, DP), jnp.int32),
        compiler_params=_cparams("arbitrary"),
        name="experts",
    )(blk_e, n_used, n_valid, xs, w1, w3, w2)


def _block_layout(counts, n_blocks):
    EB = EXPERT_BLOCK
    padded = (counts + EB - 1) // EB * EB
    ex = jnp.arange(N_EXPERTS)
    pad_end = jnp.sum(jnp.where(ex[:, None] <= ex[None, :], padded[:, None], 0), axis=0)
    pad_start = pad_end - padded
    blk_row = (jnp.arange(n_blocks) * EB)[:, None]
    blk_e = jnp.minimum(jnp.sum((pad_end[None, :] <= blk_row).astype(jnp.int32), axis=1), N_EXPERTS - 1)
    n_used = (jnp.sum(padded) // EB).astype(jnp.int32).reshape(1)
    mine = (pad_start[None, :] <= blk_row) & (blk_row < pad_end[None, :])
    n_valid = jnp.sum(jnp.where(mine, jnp.clip(counts[None, :] - (blk_row - pad_start[None, :]), 0, EB), 0), axis=1)
    return pad_start.astype(jnp.int32), blk_e.astype(jnp.int32), n_used, n_valid.astype(jnp.int32)


SC_GATHER_ROWS = 64


def _sc_gather_rows(table, idx):
    info = plsc.get_sparse_core_info()
    nc, ns = info.num_cores, info.num_subcores
    M = idx.shape[0]
    W = table.shape[1]
    b = SC_GATHER_ROWS
    per_worker = M // (nc * ns)
    steps = per_worker // b
    assert per_worker * nc * ns == M and steps * b == per_worker and steps % 2 == 0
    mesh = plsc.VectorSubcoreMesh(core_axis_name="c", subcore_axis_name="s")

    @functools.partial(
        pl.kernel, mesh=mesh,
        out_type=jax.ShapeDtypeStruct((M, W), table.dtype),
        scratch_types=[pltpu.VMEM((2, b), jnp.int32), pltpu.VMEM((2, b, W), table.dtype),
                       pltpu.SemaphoreType.DMA, pltpu.SemaphoreType.DMA],
        name="sc_gather_rows",
    )
    def gather(table_hbm, idx_hbm, out_hbm, idx_v, rows_v, sem0, sem1):
        wid = lax.axis_index("s") * nc + lax.axis_index("c")
        sems = (sem0, sem1)

        def base(s):
            return pl.multiple_of(wid * per_worker + s * b, b)

        def gather_copy(slot):
            return pltpu.make_async_copy(table_hbm.at[idx_v.at[slot]], rows_v.at[slot], sems[slot])

        def start(s, slot):
            pltpu.sync_copy(idx_hbm.at[pl.ds(base(s), b)], idx_v.at[slot])
            gather_copy(slot).start()

        def finish(s, slot):
            gather_copy(slot).wait()
            pltpu.sync_copy(rows_v.at[slot], out_hbm.at[pl.ds(base(s), b)])

        start(0, 0)

        @pl.loop(0, steps, step=2)
        def _(s):
            start(s + 1, 1)
            finish(s, 0)

            @pl.when(s + 2 < steps)
            def _():
                start(s + 2, 0)

            finish(s + 1, 1)

    return gather(table, idx)


def _sc_scatter_rows(rows, idx, n_out):
    info = plsc.get_sparse_core_info()
    nc, ns = info.num_cores, info.num_subcores
    T, W = rows.shape
    G, K_, b = idx.shape
    steps = G // (nc * ns)
    assert steps * nc * ns == G and G * b == T
    mesh = plsc.VectorSubcoreMesh(core_axis_name="c", subcore_axis_name="s")

    @functools.partial(
        pl.kernel, mesh=mesh,
        out_type=jax.ShapeDtypeStruct((n_out, W), rows.dtype),
        scratch_types=[pltpu.VMEM((K_, b), jnp.int32), pltpu.VMEM((b, W), rows.dtype), pltpu.SemaphoreType.DMA],
        name="sc_scatter_rows",
    )
    def scatter(rows_hbm, idx_hbm, out_hbm, idx_v, rows_v, sem):
        wid = lax.axis_index("s") * nc + lax.axis_index("c")

        @pl.loop(0, steps)
        def _(s):
            g = wid * steps + s
            pltpu.sync_copy(idx_hbm.at[g], idx_v)
            pltpu.sync_copy(rows_hbm.at[pl.ds(pl.multiple_of(g * b, b), b)], rows_v)
            for k in range(K_):
                pltpu.async_copy(rows_v, out_hbm.at[idx_v.at[k]], sem).wait()

    return scatter(rows, idx)


def _combine_dense_kernel(rows_ref, w_ref, x_ref, shared_ref, gpost_ref, g2_ref, o_ref):
    w = w_ref[...]
    tt, half = rows_ref.shape[1], rows_ref.shape[2]
    y_lo = jnp.zeros((tt, half), F32)
    y_hi = jnp.zeros((tt, half), F32)
    for k in range(TOP_K):
        lo, hi = _unpack_bf16_pair(rows_ref[k])
        y_lo = y_lo + w[:, k:k + 1] * lo
        y_hi = y_hi + w[:, k:k + 1] * hi
    y = shared_ref[0] + jnp.concatenate([y_lo, y_hi], axis=1)
    o_ref[0] = x_ref[0] + g2_ref[0] * (_rms(y) * gpost_ref[...])


def _combine_dense(rows, w_tok, x, shared, gpost, g2, tt):
    Bn, S, D = x.shape
    K_, T, DP = rows.shape
    nt = S // tt
    seq = pl.BlockSpec((1, tt, D), lambda b, i: (b, i, 0))
    return pl.pallas_call(
        _combine_dense_kernel,
        grid=(Bn, nt),
        in_specs=[pl.BlockSpec((K_, tt, DP), lambda b, i: (0, b * nt + i, 0)),
                  pl.BlockSpec((tt, K_), lambda b, i: (b * nt + i, 0)), seq, seq,
                  pl.BlockSpec((1, D), lambda b, i: (0, 0)), pl.BlockSpec((1, 1, D), lambda b, i: (b, 0, 0))],
        out_specs=seq,
        out_shape=jax.ShapeDtypeStruct((Bn, S, D), F32),
        compiler_params=_cparams("arbitrary", "arbitrary"),
        name="combine_dense",
    )(rows, w_tok, x, shared, gpost.reshape(1, D), g2)


def kernel(x, c, w_ada, b_ada, norm_pre_mix, norm_post_mix, norm_pre_ffn, norm_post_ffn, w_in, w_out, rel_bias_table, diff_lambda, diff_subln, rwkv_mu, rwkv_w0, rwkv_w2, rwkv_a0, rwkv_a2, rwkv_g2, rwkv_k_k, rwkv_k_a, rwkv_r_k, rwkv_lnx_g, rwkv_lnx_b, gmlp_ln_g, gmlp_ln_b, gmlp_w_s, gmlp_b_s, router_w, router_bias, exp_w1, exp_w3, exp_w2, shared_w1, shared_w3, shared_w2):
    Bn, S, D = x.shape
    depth = w_ada.shape[0]
    tm = min(256, S)
    tq = min(512, S // 2)
    t_rwkv = min(512, S)

    mod = _adaln(c, w_ada, b_ada)
    band_t = _attn_band(rel_bias_table, tq)
    zpad = jnp.zeros((B_DECAY_LORA, B_WIDTH), F32)
    for l in range(depth):
        sh1, sc1, g1, sh2, sc2, g2 = [m.reshape(Bn, 1, D) for m in jnp.split(mod[l], 6, axis=-1)]
        w_in_b = w_in[l].astype(BF16)
        pa, pbc = _inproj(x, norm_pre_mix[l], sc1, sh1, w_in_b[:, :A_COLS], w_in_b[:, A_COLS:], tm)
        lambda_init = 0.8 - 0.6 * math.exp(-0.3 * l)
        vt = jnp.swapaxes(pa[..., 2 * A_WIDTH:], 1, 2)
        ya = _diff_attention(pa, vt, band_t, diff_lambda[l], diff_subln[l], lambda_init, tq)
        prep = _rwkv_prep(pbc, rwkv_mu[l], rwkv_w0[l], jnp.concatenate([rwkv_w2[l], zpad], axis=0),
                          rwkv_a0[l], jnp.concatenate([zpad, rwkv_a2[l]], axis=0), rwkv_g2[l],
                          rwkv_k_k[l], rwkv_k_a[l], rwkv_r_k[l].reshape(-1), t_rwkv)
        yb = _rwkv_scan(*prep, rwkv_lnx_g[l], rwkv_lnx_b[l], t_rwkv)
        yc = _gmlp(pbc, gmlp_ln_g[l], gmlp_ln_b[l], gmlp_w_s[l], gmlp_b_s[l], tm)

        w_out_b = w_out[l].astype(BF16)
        wr_t = jnp.pad(jnp.transpose(router_w[l]), ((0, V7X_LANES - N_EXPERTS), (0, 0)))
        x, h, scores_t, shared = _mid(
            ya, yb, yc, x, w_out_b[:A_WIDTH], w_out_b[A_WIDTH:A_WIDTH + B_WIDTH], w_out_b[A_WIDTH + B_WIDTH:],
            norm_post_mix[l], g1, norm_pre_ffn[l], sc2, sh2, wr_t,
            shared_w1[l].astype(BF16), shared_w3[l].astype(BF16), shared_w2[l].astype(BF16), tm)

        T = Bn * S
        n_blocks = -(-T * TOP_K // EXPERT_BLOCK) + N_EXPERTS
        eidx, wgt, rank, cnt = _route(scores_t, router_bias[l], tm)
        pad_start, blk_e, n_used, n_valid = _block_layout(cnt[:, 0], n_blocks)
        dest = _dest_rows(pad_start, eidx, rank, tm)
        b = SC_GATHER_ROWS
        dest_sc = jnp.transpose(dest.reshape(T // tm, TOP_K, tm // b, b), (0, 2, 1, 3)).reshape(T // b, TOP_K, b)
        xs = _sc_scatter_rows(h.reshape(T, D // 2), dest_sc, n_blocks * EXPERT_BLOCK)
        ys = _experts(blk_e, n_used, n_valid, xs, exp_w1, exp_w3, exp_w2, l)
        dest_kt = jnp.transpose(dest, (1, 0, 2)).reshape(TOP_K * T)
        rows = _sc_gather_rows(ys, dest_kt).reshape(TOP_K, T, D // 2)
        x = _combine_dense(rows, jnp.transpose(wgt), x, shared, norm_post_ffn[l], g2, tm)
    return x
```

```python
import functools
import math

import jax
import jax.numpy as jnp
from jax import lax
from jax.experimental import pallas as pl
from jax.experimental.pallas import tpu as pltpu
from jax.experimental.pallas import tpu_sc as plsc

F32 = jnp.float32
BF16 = jnp.bfloat16

A_HEADS = 4
A_QK_DIM = 64
A_HEAD_W = 2 * A_QK_DIM
A_WIDTH = A_HEADS * A_HEAD_W
N_BUCKETS = 32
MAX_DISTANCE = 128
B_HEADS = 4
B_HEAD_DIM = 64
B_WIDTH = B_HEADS * B_HEAD_DIM
B_DECAY_LORA = 64
B_AAA_LORA = 64
B_GATE_LORA = 128
B_LNX_EPS = 64e-5
C_GROUPS = 4
C_GROUP_DIM = 64
C_WIDTH = C_GROUPS * C_GROUP_DIM
CHUNK = 128
A_COLS = 3 * A_WIDTH
B_COLS = 3 * B_WIDTH + B_DECAY_LORA + B_AAA_LORA + B_GATE_LORA
C_COLS = 2 * C_WIDTH
N_EXPERTS = 64
TOP_K = 8
N_GROUPS = 8
TOPK_GROUPS = 4
EXPERTS_PER_GROUP = N_EXPERTS // N_GROUPS
ROUTED_SCALE = 2.5
EXPERT_BLOCK = 512
RMS_EPS = 1e-6
LN_EPS = 1e-5
NEG_BIG = -1e30

V7X_LANES = 128
V7X_VMEM_LIMIT_BYTES = 56 * 1024 * 1024
RWKV_CHUNK = 64
RWKV_GROUP = 8

NN = (((1,), (0,)), ((), ()))
NT = (((1,), (1,)), ((), ()))
TN = (((0,), (0,)), ((), ()))


def _cparams(*sem):
    return pltpu.CompilerParams(dimension_semantics=sem, vmem_limit_bytes=V7X_VMEM_LIMIT_BYTES)


def _mm(a, b, dims=NN):
    return lax.dot_general(a.astype(BF16), b.astype(BF16), dims, preferred_element_type=F32)


def _split(a):
    hi = a.astype(BF16)
    lo = (a - hi.astype(F32)).astype(BF16)
    return hi, lo


def _mm3(a, b, dims=NN):
    ah, al = _split(a)
    bh, bl = _split(b)
    d = lambda x, y: lax.dot_general(x, y, dims, preferred_element_type=F32)
    return d(ah, bh) + d(ah, bl) + d(al, bh)


def _mm2(a, b_exact, dims=NN):
    ah, al = _split(a)
    d = lambda x: lax.dot_general(x, b_exact, dims, preferred_element_type=F32)
    return d(ah) + d(al)


def _pack_bf16_pair(x):
    n = x.shape[1] // 2
    bits = lax.bitcast_convert_type(x.astype(BF16).astype(F32), jnp.int32)
    return ((bits[:, :n] >> 16) & 0xFFFF) | bits[:, n:]


def _unpack_bf16_pair(u):
    lo = lax.bitcast_convert_type(u << 16, F32)
    hi = lax.bitcast_convert_type(u & jnp.int32(-65536), F32)
    return lo, hi


def _rms(x, eps=RMS_EPS):
    return x * lax.rsqrt(jnp.mean(x * x, axis=-1, keepdims=True) + eps)


def _sigmoid(x):
    return 1.0 / (1.0 + jnp.exp(-x))


def _silu(x):
    return x * _sigmoid(x)


def _adaln_kernel(c_ref, w_ref, b_ref, o_ref):
    c = c_ref[...]
    o_ref[0] = _mm3(_silu(c), w_ref[0]) + b_ref[0]


def _adaln(c, w_ada, b_ada):
    L, D, N = w_ada.shape
    Bn = c.shape[0]
    tn = min(N, 1536)
    return pl.pallas_call(
        _adaln_kernel,
        grid=(L, N // tn),
        in_specs=[
            pl.BlockSpec((Bn, D), lambda l, j: (0, 0)),
            pl.BlockSpec((1, D, tn), lambda l, j: (l, 0, j)),
            pl.BlockSpec((1, 1, tn), lambda l, j: (l, 0, j)),
        ],
        out_specs=pl.BlockSpec((1, Bn, tn), lambda l, j: (l, 0, j)),
        out_shape=jax.ShapeDtypeStruct((L, Bn, N), F32),
        compiler_params=_cparams("arbitrary", "arbitrary"),
        name="adaln",
    )(c, w_ada, b_ada.reshape(L, 1, N))


def _inproj_kernel(x_ref, g_ref, sc_ref, sh_ref, wa_ref, wvt_ref, wbc_ref, oa_ref, ovt_ref, obc_ref):
    x = x_ref[0]
    h = _rms(x) * g_ref[...] * (1.0 + sc_ref[0]) + sh_ref[0]
    hb = h.astype(BF16)
    oa_ref[0] = jnp.dot(hb, wa_ref[...], preferred_element_type=F32).astype(BF16)
    ovt_ref[0] = lax.dot_general(wvt_ref[...], hb, NT, preferred_element_type=F32).astype(BF16)
    obc_ref[0] = jnp.dot(hb, wbc_ref[...], preferred_element_type=F32)


def _inproj(x, g, sc, sh, wa, wvt, wbc, tm):
    Bn, S, D = x.shape
    na, nv, nbc = wa.shape[1], wvt.shape[0], wbc.shape[1]
    return pl.pallas_call(
        _inproj_kernel,
        grid=(Bn, S // tm),
        in_specs=[
            pl.BlockSpec((1, tm, D), lambda b, i: (b, i, 0)),
            pl.BlockSpec((1, D), lambda b, i: (0, 0)),
            pl.BlockSpec((1, 1, D), lambda b, i: (b, 0, 0)),
            pl.BlockSpec((1, 1, D), lambda b, i: (b, 0, 0)),
            pl.BlockSpec((D, na), lambda b, i: (0, 0)),
            pl.BlockSpec((nv, D), lambda b, i: (0, 0)),
            pl.BlockSpec((D, nbc), lambda b, i: (0, 0)),
        ],
        out_specs=[
            pl.BlockSpec((1, tm, na), lambda b, i: (b, i, 0)),
            pl.BlockSpec((1, nv, tm), lambda b, i: (b, 0, i)),
            pl.BlockSpec((1, tm, nbc), lambda b, i: (b, i, 0)),
        ],
        out_shape=[
            jax.ShapeDtypeStruct((Bn, S, na), BF16),
            jax.ShapeDtypeStruct((Bn, nv, S), BF16),
            jax.ShapeDtypeStruct((Bn, S, nbc), F32),
        ],
        compiler_params=_cparams("arbitrary", "arbitrary"),
        name="inproj",
    )(x, g.reshape(1, D), sc, sh, wa, wvt, wbc)


def _t5_bucket(dist):
    n = jnp.maximum(dist, 0)
    max_exact = N_BUCKETS // 2
    nf = jnp.maximum(n, 1).astype(F32)
    large = max_exact + (jnp.log(nf / max_exact) / math.log(MAX_DISTANCE / max_exact)
                         * (N_BUCKETS - max_exact)).astype(jnp.int32)
    large = jnp.minimum(large, N_BUCKETS - 1)
    return jnp.where(n < max_exact, n, large)


def _attn_band(table, tq):
    far = table[N_BUCKETS - 1].astype(F32)
    L = 3 * tq
    m = jnp.arange(L)
    m = jnp.where(m < tq, m, m - L)
    bands = []
    for off in (0, tq):
        dist = m + off
        vals = jnp.where(dist[None] >= 0, jnp.transpose(table[_t5_bucket(dist)].astype(F32)) - far[:, None],
                         NEG_BIG)
        toe = jnp.tile(vals, (1, 2 * tq))[:, :2 * tq * (L - 1)].reshape(-1, 2 * tq, L - 1)
        bands.append(toe[:, :, :tq])
    return jnp.stack(bands)


def _attn_kernel(lam_ref, q_ref, k_ref, vt_ref, band_ref, g_ref, o_ref, *, tq, lambda_init):
    i = pl.program_id(2)
    q = q_ref[0] * jnp.asarray(A_QK_DIM ** -0.5, BF16)
    lane = lax.broadcasted_iota(jnp.int32, q.shape, 1)
    zero = jnp.zeros_like(q)
    qq = jnp.concatenate([jnp.where(lane < A_QK_DIM, q, zero),
                          jnp.where(lane >= A_QK_DIM, q, zero)], axis=0)

    kb0 = pl.multiple_of(jnp.maximum(i - 1, 0) * tq, tq)
    kb = k_ref[0, pl.ds(kb0, 2 * tq), :]
    band = band_ref[0, 0]
    s = lax.dot_general(kb, qq, NT, preferred_element_type=F32) + jnp.concatenate([band, band], axis=1)
    m = jnp.max(s, axis=0, keepdims=True)
    p = jnp.exp(s - m)
    l = jnp.sum(p, axis=0, keepdims=True)
    acc = jnp.dot(vt_ref[0, :, pl.ds(kb0, 2 * tq)], p.astype(BF16), preferred_element_type=F32)

    n_far = jnp.maximum(i - 1, 0)

    def logits(j):
        return lax.dot_general(k_ref[0, pl.ds(pl.multiple_of(j * tq, tq), tq), :], qq, NT,
                               preferred_element_type=F32)

    def body(j, carry):
        m, l, acc, s = carry
        s_next = logits(jnp.minimum(j + 1, n_far - 1))
        vtj = vt_ref[0, :, pl.ds(pl.multiple_of(j * tq, tq), tq)]
        m_new = jnp.maximum(m, jnp.max(s, axis=0, keepdims=True))
        alpha = jnp.exp(m - m_new)
        p = jnp.exp(s - m_new)
        l = alpha * l + jnp.sum(p, axis=0, keepdims=True)
        acc = alpha * acc + jnp.dot(vtj, p.astype(BF16), preferred_element_type=F32)
        return m_new, l, acc, s_next

    m, l, acc, _ = lax.fori_loop(0, n_far, body, (m, l, acc, logits(0)))

    lp = lam_ref[...]
    lam = (jnp.exp(jnp.sum(lp[0:1] * lp[1:2], axis=-1, keepdims=True))
           - jnp.exp(jnp.sum(lp[2:3] * lp[3:4], axis=-1, keepdims=True)) + lambda_init)
    o = acc / l
    o = o[:, :tq] - lam * o[:, tq:]
    o = o * lax.rsqrt(jnp.mean(o * o, axis=0, keepdims=True) + RMS_EPS) * g_ref[...] * (1.0 - lambda_init)
    o_ref[0] = jnp.transpose(o)


def _diff_attention(pa, vt, band_t, lam_par, subln_g, lambda_init, tq):
    Bn, S, _ = pa.shape
    W = A_HEAD_W
    kern = functools.partial(_attn_kernel, tq=tq, lambda_init=lambda_init)
    return pl.pallas_call(
        kern,
        grid=(Bn, A_HEADS, S // tq),
        in_specs=[
            pl.BlockSpec((4, A_QK_DIM), lambda b, h, i: (0, 0)),
            pl.BlockSpec((1, tq, W), lambda b, h, i: (b, i, h)),
            pl.BlockSpec((1, S, W), lambda b, h, i: (b, 0, A_HEADS + h)),
            pl.BlockSpec((1, W, S), lambda b, h, i: (b, h, 0)),
            pl.BlockSpec((1, 1, 2 * tq, tq), lambda b, h, i: (jnp.minimum(i, 1), h, 0, 0)),
            pl.BlockSpec((W, 1), lambda b, h, i: (0, 0)),
        ],
        out_specs=pl.BlockSpec((1, tq, W), lambda b, h, i: (b, i, h)),
        out_shape=jax.ShapeDtypeStruct((Bn, S, A_WIDTH), F32),
        compiler_params=_cparams("arbitrary", "arbitrary", "arbitrary"),
        name="diff_attn",
    )(lam_par, pa, pa, vt, band_t, subln_g.reshape(W, 1))


def _head_ones(n):
    r = lax.broadcasted_iota(jnp.int32, (n, n), 0) // B_HEAD_DIM
    c = lax.broadcasted_iota(jnp.int32, (n, n), 1) // B_HEAD_DIM
    return (r == c).astype(BF16)


def _rwkv_prep_kernel(pb_ref, prev_ref, mu_ref, w0_ref, w2_ref, a0_ref, a2_ref, g2_ref,
                      kk_ref, ka_ref, rk_ref,
                      rt_ref, at_ref, kt_ref, bt_ref, v_ref, wc_ref, bonus_ref, g_ref, *, tm):
    i = pl.program_id(1)
    C = RWKV_CHUNK
    x = pb_ref[0]
    row = lax.broadcasted_iota(jnp.int32, x.shape, 0)
    last = prev_ref[0, 7:8, :] * (i > 0).astype(F32)
    prev = jnp.where(row == 0, last, pltpu.roll(x, 1, 0))
    p = x + (prev - x) * mu_ref[...]
    o1, o2, o3 = B_WIDTH, 2 * B_WIDTH, 3 * B_WIDTH
    r, k, v = p[:, :o1], p[:, o1:o2], p[:, o2:o3]
    lora = p[:, o3:o3 + B_DECAY_LORA + B_AAA_LORA]
    gd = p[:, o3 + B_DECAY_LORA + B_AAA_LORA:]

    z = -(w0_ref[...] + _mm3(jnp.tanh(lora), w2_ref[...]))
    softplus = jnp.maximum(z, 0.0) + jnp.log(1.0 + jnp.exp(-jnp.abs(z)))
    logw = -jnp.exp(-softplus - 0.5)
    a = _sigmoid(a0_ref[...] + _mm3(lora, a2_ref[...]))
    g_ref[0] = _mm3(_sigmoid(gd), g2_ref[...])

    ones = _head_ones(B_WIDTH)
    kk = k * kk_ref[...]
    kk = kk * lax.rsqrt(jnp.maximum(_mm2(kk * kk, ones), 1e-24))
    k2 = k * (1.0 + (a - 1.0) * ka_ref[...])
    bonus_ref[0] = _mm2(r * k2 * rk_ref[...], ones) * v

    t_in = lax.broadcasted_iota(jnp.int32, (tm, B_WIDTH), 0) % C
    cum = logw
    sh = 1
    while sh < C:
        cum = cum + jnp.where(t_in >= sh, pltpu.roll(cum, sh, 0), 0.0)
        sh *= 2
    n = tm // C
    wc_ref[0] = jnp.exp(jnp.sum(logw.reshape(n, C, B_WIDTH), axis=1))
    e_pos = jnp.exp(cum)
    e_neg = jnp.exp(-cum)
    rt_ref[0] = r * e_pos
    at_ref[0] = -kk * jnp.exp(cum - logw)
    kt_ref[0] = k2 * e_neg
    bt_ref[0] = kk * a * e_neg
    v_ref[0] = v


def _rwkv_prep(pbc, mu, w0, w2p, a0, a2p, g2, k_k, k_a, r_k, tm):
    Bn, S, _ = pbc.shape
    W = B_WIDTH
    nl = B_DECAY_LORA + B_AAA_LORA
    row = lambda a: a.reshape(1, -1)
    full = lambda shp: pl.BlockSpec(shp, lambda b, i: (0,) * len(shp))
    seq = pl.BlockSpec((1, tm, W), lambda b, i: (b, i, 0))
    seq_shape = jax.ShapeDtypeStruct((Bn, S, W), F32)
    n = tm // RWKV_CHUNK
    return pl.pallas_call(
        functools.partial(_rwkv_prep_kernel, tm=tm),
        grid=(Bn, S // tm),
        in_specs=[
            pl.BlockSpec((1, tm, B_COLS), lambda b, i: (b, i, 0)),
            pl.BlockSpec((1, 8, B_COLS), lambda b, i: (b, jnp.maximum(i * (tm // 8) - 1, 0), 0)),
            full((1, B_COLS)), full((1, W)), full((nl, W)), full((1, W)), full((nl, W)),
            full((B_GATE_LORA, W)), full((1, W)), full((1, W)), full((1, W)),
        ],
        out_specs=[seq, seq, seq, seq, seq,
                   pl.BlockSpec((1, n, W), lambda b, i: (b, i, 0)), seq, seq],
        out_shape=[seq_shape] * 5 + [jax.ShapeDtypeStruct((Bn, S // RWKV_CHUNK, W), F32)] + [seq_shape] * 2,
        compiler_params=_cparams("arbitrary", "arbitrary"),
        name="rwkv_prep",
    )(pbc, pbc, row(mu), row(w0), w2p, row(a0), a2p, g2, row(k_k), row(k_a), row(r_k))


def _rwkv_scan_kernel(rt_ref, at_ref, kt_ref, bt_ref, v_ref, wc_ref, bonus_ref, g_ref,
                      lng_ref, lnb_ref, o_ref, state, *, tt):
    C = RWKV_CHUNK
    W = B_WIDTH

    @pl.when(pl.program_id(1) == 0)
    def _():
        state[...] = jnp.zeros_like(state)

    lane_head = lax.broadcasted_iota(jnp.int32, (C, W), 1) // B_HEAD_DIM
    tt_i = lax.broadcasted_iota(jnp.int32, (C, W), 0)
    ss_i = lax.broadcasted_iota(jnp.int32, (C, W), 1) % C
    strict = tt_i > ss_i
    incl = tt_i >= ss_i
    eye = (tt_i == ss_i).astype(F32)
    ones = _head_ones(W)
    bd_mask = ones.astype(F32)

    head_mask = [(lane_head == h).astype(BF16) for h in range(B_HEADS)]

    def bd_split(x):
        xb = x.astype(BF16)
        return jnp.concatenate([xb * mk for mk in head_mask], axis=0)

    def mm_bd(a, b_bd, dims=NN):
        return lax.dot_general(a.astype(BF16), b_bd, dims, preferred_element_type=F32)

    def state_free(gi):
        G = range(RWKV_GROUP)
        sls = [pl.ds(pl.multiple_of((gi * RWKV_GROUP + j) * C, C), C) for j in G]
        rt = [rt_ref[0, sl, :] for sl in sls]
        at = [at_ref[0, sl, :] for sl in sls]
        kt = [kt_ref[0, sl, :] for sl in sls]
        bt = [bt_ref[0, sl, :] for sl in sls]
        v = [v_ref[0, sl, :] for sl in sls]
        wc = [wc_ref[0, pl.ds(gi * RWKV_GROUP + j, 1), :] for j in G]
        ar = [jnp.concatenate([at[j], rt[j]], axis=0) for j in G]
        bdb = [bd_split(bt[j]) for j in G]
        bdk = [bd_split(kt[j]) for j in G]
        a_b = [mm_bd(ar[j], bdb[j], NT) for j in G]
        a_k = [mm_bd(ar[j], bdk[j], NT) for j in G]
        lo = [jnp.where(strict, a_b[j][:C], 0.0) for j in G]
        a_ak = [jnp.where(strict, a_k[j][:C], 0.0) for j in G]
        a_rb = [jnp.where(incl, a_b[j][C:], 0.0) for j in G]
        a_rk = [jnp.where(incl, a_k[j][C:], 0.0) for j in G]
        pw = lo
        tinv = [eye + lo[j] for j in G]
        bdp = [bd_split(pw[j]) for j in G]
        span = 2
        while span < C:
            pw = [mm_bd(pw[j], bdp[j]) for j in G]
            bdp = [bd_split(pw[j]) for j in G]
            tinv = [tinv[j] + mm_bd(tinv[j], bdp[j]) for j in G]
            span *= 2
        bdv = [bd_split(v[j]) for j in G]
        bda = [bd_split(at[j]) for j in G]
        abar = [mm_bd(tinv[j], bda[j]) for j in G]
        akv = [bd_split(mm_bd(a_ak[j], bdv[j])) for j in G]
        u0 = [mm_bd(tinv[j], akv[j]) for j in G]
        y0 = [mm_bd(a_rk[j], bdv[j]) for j in G]
        kv = [_mm(v[j], kt[j] * wc[j], TN) * bd_mask for j in G]
        return [(jnp.concatenate([abar[j], rt[j]], axis=0), u0[j], y0[j], a_rb[j], bt[j] * wc[j], kv[j], wc[j])
                for j in G]

    def group(gi, carry):
        pre = state_free(gi)
        s = state[...]
        ys = []
        for abar_rt, u0, y0, a_rb, btw, kv, wc in pre:
            a_s = _mm(abar_rt, s, NT)
            u = a_s[:C] + u0
            ys.append(a_s[C:] + y0 + mm_bd(a_rb, bd_split(u)))
            s = s * wc + _mm(u, btw, TN) * bd_mask + kv
        state[...] = s
        y = jnp.concatenate(ys, axis=0)
        sl = pl.ds(pl.multiple_of(gi * (RWKV_GROUP * C), RWKV_GROUP * C), RWKV_GROUP * C)
        mean = _mm2(y, ones) * (1.0 / B_HEAD_DIM)
        d = y - mean
        var = _mm2(d * d, ones) * (1.0 / B_HEAD_DIM)
        yn = d * lax.rsqrt(var + B_LNX_EPS) * lng_ref[...] + lnb_ref[...]
        o_ref[0, sl, :] = (yn + bonus_ref[0, sl, :]) * g_ref[0, sl, :]
        return carry

    lax.fori_loop(0, tt // (RWKV_GROUP * C), group, 0)


def _rwkv_scan(rt, at, kt, bt, v, wc, bonus, g, lnx_g, lnx_b, tt):
    Bn, S, W = rt.shape
    n = tt // RWKV_CHUNK
    seq = pl.BlockSpec((1, tt, W), lambda b, i: (b, i, 0))
    vec = pl.BlockSpec((1, W), lambda b, i: (0, 0))
    return pl.pallas_call(
        functools.partial(_rwkv_scan_kernel, tt=tt),
        grid=(Bn, S // tt),
        in_specs=[seq, seq, seq, seq, seq, pl.BlockSpec((1, n, W), lambda b, i: (b, i, 0)), seq, seq, vec, vec],
        out_specs=seq,
        out_shape=jax.ShapeDtypeStruct((Bn, S, W), F32),
        scratch_shapes=[pltpu.VMEM((B_HEADS * B_HEAD_DIM, W), F32)],
        compiler_params=_cparams("arbitrary", "arbitrary"),
        name="rwkv_scan",
    )(rt, at, kt, bt, v, wc, bonus, g, lnx_g.reshape(1, W), lnx_b.reshape(1, W))


def _gmlp_kernel(pc_ref, lng_ref, lnb_ref, ws_ref, bs_ref, o_ref, *, tm):
    x = pc_ref[0]
    z = x * (0.5 * (1.0 + jnp.tanh(math.sqrt(2.0 / math.pi) * (x + 0.044715 * (x * x * x)))))
    u, v = z[:, :C_WIDTH], z[:, C_WIDTH:]
    mu = jnp.mean(v, axis=-1, keepdims=True)
    d = v - mu
    var = jnp.mean(d * d, axis=-1, keepdims=True)
    vn = d * lax.rsqrt(var + LN_EPS) * lng_ref[...] + lnb_ref[...]
    group = lax.broadcasted_iota(jnp.int32, (CHUNK, C_WIDTH), 1) // C_GROUP_DIM
    tril = (lax.broadcasted_iota(jnp.int32, (CHUNK, CHUNK), 0)
            >= lax.broadcasted_iota(jnp.int32, (CHUNK, CHUNK), 1))
    ws = [jnp.where(tril, ws_ref[gi], 0.0).astype(BF16) for gi in range(C_GROUPS)]
    for c in range(tm // CHUNK):
        sl = slice(c * CHUNK, (c + 1) * CHUNK)
        vc = vn[sl].astype(BF16)
        sv = bs_ref[...]
        for gi in range(C_GROUPS):
            t = jnp.dot(ws[gi], vc, preferred_element_type=F32)
            sv = sv + jnp.where(group == gi, t, 0.0)
        o_ref[0, sl, :] = u[sl] * sv


def _gmlp(pbc, ln_g, ln_b, w_s, b_s, tm):
    Bn, S, _ = pbc.shape
    bs_wide = jnp.repeat(jnp.transpose(b_s), C_GROUP_DIM, axis=1)
    return pl.pallas_call(
        functools.partial(_gmlp_kernel, tm=tm),
        grid=(Bn, S // tm),
        in_specs=[
            pl.BlockSpec((1, tm, C_COLS), lambda b, i: (b, i, B_COLS // C_COLS)),
            pl.BlockSpec((1, C_WIDTH), lambda b, i: (0, 0)),
            pl.BlockSpec((1, C_WIDTH), lambda b, i: (0, 0)),
            pl.BlockSpec((C_GROUPS, CHUNK, CHUNK), lambda b, i: (0, 0, 0)),
            pl.BlockSpec((CHUNK, C_WIDTH), lambda b, i: (0, 0)),
        ],
        out_specs=pl.BlockSpec((1, tm, C_WIDTH), lambda b, i: (b, i, 0)),
        out_shape=jax.ShapeDtypeStruct((Bn, S, C_WIDTH), F32),
        compiler_params=_cparams("arbitrary", "arbitrary"),
        name="gmlp",
    )(pbc, ln_g.reshape(1, -1), ln_b.reshape(1, -1), w_s, bs_wide)


def _mid_kernel(ya_ref, yb_ref, yc_ref, x_ref, woa_ref, wob_ref, woc_ref, gpost_ref, g1_ref,
                gpre_ref, sc_ref, sh_ref, wr_ref, ws1_ref, ws3_ref, ws2_ref,
                xo_ref, h_ref, score_ref, shared_ref):
    y = (_mm(ya_ref[0], woa_ref[...]) + _mm(yb_ref[0], wob_ref[...]) + _mm(yc_ref[0], woc_ref[...]))
    xn = x_ref[0] + g1_ref[0] * (_rms(y) * gpost_ref[...])
    xo_ref[0] = xn
    h = _rms(xn) * gpre_ref[...] * (1.0 + sc_ref[0]) + sh_ref[0]
    h_ref[0] = _pack_bf16_pair(h)
    score_ref[0] = _sigmoid(_mm3(wr_ref[...], h, NT))
    hb = h.astype(BF16)
    t = _silu(jnp.dot(hb, ws1_ref[...], preferred_element_type=F32)) * jnp.dot(
        hb, ws3_ref[...], preferred_element_type=F32)
    shared_ref[0] = jnp.dot(t.astype(BF16), ws2_ref[...], preferred_element_type=F32)


def _mid(ya, yb, yc, x, woa, wob, woc, gpost, g1, gpre, sc, sh, wr, ws1, ws3, ws2, tm):
    Bn, S, D = x.shape
    NR = wr.shape[0]
    F = ws1.shape[1]
    seq = lambda w: pl.BlockSpec((1, tm, w), lambda b, i: (b, i, 0))
    full = lambda shp: pl.BlockSpec(shp, lambda b, i: (0,) * len(shp))
    per_b = pl.BlockSpec((1, 1, D), lambda b, i: (b, 0, 0))
    return pl.pallas_call(
        _mid_kernel,
        grid=(Bn, S // tm),
        in_specs=[seq(A_WIDTH), seq(B_WIDTH), seq(C_WIDTH), seq(D),
                  full((A_WIDTH, D)), full((B_WIDTH, D)), full((C_WIDTH, D)),
                  full((1, D)), per_b, full((1, D)), per_b, per_b,
                  full((NR, D)), full((D, F)), full((D, F)), full((F, D))],
        out_specs=[seq(D), seq(D // 2), pl.BlockSpec((1, NR, tm), lambda b, i: (b, 0, i)), seq(D)],
        out_shape=[jax.ShapeDtypeStruct((Bn, S, D), F32), jax.ShapeDtypeStruct((Bn, S, D // 2), jnp.int32),
                   jax.ShapeDtypeStruct((Bn, NR, S), F32), jax.ShapeDtypeStruct((Bn, S, D), F32)],
        compiler_params=_cparams("arbitrary", "arbitrary"),
        name="mid",
    )(ya, yb, yc, x, woa, wob, woc, gpost.reshape(1, D), g1, gpre.reshape(1, D), sc, sh, wr, ws1, ws3, ws2)


def _first_argmax(vals, iota, n):
    m = jnp.max(vals, axis=0, keepdims=True)
    idx = jnp.min(jnp.where(vals == m, iota, n), axis=0, keepdims=True)
    return m, idx


def _route_kernel(sc_ref, bias_ref, e_ref, w_ref, r_ref, cnt_ref, carry, *, tm):
    @pl.when((pl.program_id(0) == 0) & (pl.program_id(1) == 0))
    def _():
        carry[...] = jnp.zeros_like(carry)

    G = EXPERTS_PER_GROUP
    s = sc_ref[0]
    biased = s + bias_ref[...]
    neg_inf = jnp.float32(-jnp.inf)
    io8 = lax.broadcasted_iota(jnp.int32, (G, tm), 0)
    gs_rows = []
    for g in range(N_GROUPS):
        blk = biased[g * G:(g + 1) * G]
        m1, i1 = _first_argmax(blk, io8, G)
        m2 = jnp.max(jnp.where(io8 == i1, neg_inf, blk), axis=0, keepdims=True)
        gs_rows.append(m1 + m2)
    gs = jnp.concatenate(gs_rows, axis=0)
    gio = lax.broadcasted_iota(jnp.int32, (N_GROUPS, tm), 0)
    gsel = jnp.zeros((N_GROUPS, tm), jnp.bool_)
    for _ in range(TOPK_GROUPS):
        _, gi = _first_argmax(gs, gio, N_GROUPS)
        pick = gio == gi
        gsel = gsel | pick
        gs = jnp.where(pick, neg_inf, gs)
    masked = jnp.concatenate(
        [jnp.where(gsel[g:g + 1], biased[g * G:(g + 1) * G], neg_inf) for g in range(N_GROUPS)], axis=0)

    eio = lax.broadcasted_iota(jnp.int32, (N_EXPERTS, tm), 0)
    picks, e_rows, s_rows = [], [], []
    for _ in range(TOP_K):
        _, ei = _first_argmax(masked, eio, N_EXPERTS)
        pick = eio == ei
        picks.append(pick)
        e_rows.append(ei)
        s_rows.append(jnp.sum(jnp.where(pick, s, 0.0), axis=0, keepdims=True))
        masked = jnp.where(pick, neg_inf, masked)
    top_s = jnp.concatenate(s_rows, axis=0)
    w_ref[...] = top_s / (jnp.sum(top_s, axis=0, keepdims=True) + 1e-20) * ROUTED_SCALE
    e_ref[...] = jnp.concatenate(e_rows, axis=0)

    sel = jnp.zeros((N_EXPERTS, tm), F32)
    for pick in picks:
        sel = sel + pick.astype(F32)
    before = (lax.broadcasted_iota(jnp.int32, (tm, tm), 0) < lax.broadcasted_iota(jnp.int32, (tm, tm), 1))
    pos = carry[...] + jnp.dot(sel.astype(BF16), before.astype(BF16), preferred_element_type=F32)
    r_ref[...] = jnp.concatenate(
        [jnp.sum(jnp.where(pick, pos, 0.0), axis=0, keepdims=True) for pick in picks], axis=0).astype(jnp.int32)
    total = carry[...] + jnp.sum(sel, axis=1, keepdims=True)
    carry[...] = total
    cnt_ref[...] = jnp.broadcast_to(total, cnt_ref.shape).astype(jnp.int32)


def _route(scores_t, e_bias, tm):
    Bn, _, S = scores_t.shape
    T = Bn * S
    nt = S // tm
    tok = pl.BlockSpec((TOP_K, tm), lambda b, i: (0, b * nt + i))
    return pl.pallas_call(
        functools.partial(_route_kernel, tm=tm),
        grid=(Bn, nt),
        in_specs=[pl.BlockSpec((1, N_EXPERTS, tm), lambda b, i: (b, 0, i)),
                  pl.BlockSpec((N_EXPERTS, 1), lambda b, i: (0, 0))],
        out_specs=[tok, tok, tok, pl.BlockSpec((N_EXPERTS, V7X_LANES), lambda b, i: (0, 0))],
        out_shape=[jax.ShapeDtypeStruct((TOP_K, T), jnp.int32), jax.ShapeDtypeStruct((TOP_K, T), F32),
                   jax.ShapeDtypeStruct((TOP_K, T), jnp.int32),
                   jax.ShapeDtypeStruct((N_EXPERTS, V7X_LANES), jnp.int32)],
        scratch_shapes=[pltpu.VMEM((N_EXPERTS, 1), F32)],
        compiler_params=_cparams("arbitrary", "arbitrary"),
        name="route",
    )(scores_t, e_bias.reshape(N_EXPERTS, 1))


def _dest_kernel(start_ref, e_ref, r_ref, o_ref):
    e = e_ref[...]
    acc = r_ref[...]
    for ex in range(N_EXPERTS):
        acc = acc + jnp.where(e == ex, start_ref[ex], 0)
    o_ref[0] = acc


def _dest_rows(pad_start, eidx, rank, tt):
    K_, T = eidx.shape
    grid_spec = pltpu.PrefetchScalarGridSpec(
        num_scalar_prefetch=1,
        grid=(T // tt,),
        in_specs=[pl.BlockSpec((K_, tt), lambda i, st: (0, i)), pl.BlockSpec((K_, tt), lambda i, st: (0, i))],
        out_specs=pl.BlockSpec((1, K_, tt), lambda i, st: (i, 0, 0)),
    )
    return pl.pallas_call(
        _dest_kernel,
        grid_spec=grid_spec,
        out_shape=jax.ShapeDtypeStruct((T // tt, K_, tt), jnp.int32),
        compiler_params=_cparams("arbitrary"),
        name="dest_rows",
    )(pad_start, eidx, rank)


def _expert_kernel(blk_e_ref, n_used_ref, n_valid_ref, x_ref, w1_ref, w3_ref, w2_ref, o_ref, w1b, w3b, w2b):
    i = pl.program_id(0)

    @pl.when((i == 0) | (blk_e_ref[i] != blk_e_ref[jnp.maximum(i - 1, 0)]))
    def _():
        w1b[...] = w1_ref[0].astype(BF16)
        w3b[...] = w3_ref[0].astype(BF16)
        w2b[...] = w2_ref[0].astype(BF16)

    @pl.when(i < n_used_ref[0])
    def _():
        row = lax.broadcasted_iota(jnp.int32, x_ref.shape, 0)
        x_lo, x_hi = _unpack_bf16_pair(jnp.where(row < n_valid_ref[i], x_ref[...], 0))
        x_lo, x_hi = x_lo.astype(BF16), x_hi.astype(BF16)
        half = x_lo.shape[1]

        def up(wb):
            return (jnp.dot(x_lo, wb[:half, :], preferred_element_type=F32)
                    + jnp.dot(x_hi, wb[half:, :], preferred_element_type=F32))

        t = _silu(up(w1b)) * up(w3b)
        o_ref[...] = _pack_bf16_pair(jnp.dot(t.astype(BF16), w2b[...], preferred_element_type=F32))


def _experts(blk_e, n_used, n_valid, xs, w1, w3, w2, layer):
    P, DP = xs.shape
    EB = EXPERT_BLOCK
    n_blocks = blk_e.shape[0]
    D, F = w1.shape[2], w1.shape[3]
    rows = pl.BlockSpec((EB, DP), lambda i, be, nu, nv: (jnp.minimum(i, nu[0] - 1), 0))
    grid_spec = pltpu.PrefetchScalarGridSpec(
        num_scalar_prefetch=3,
        grid=(n_blocks,),
        in_specs=[
            rows,
            pl.BlockSpec((None, 1, D, F), lambda i, be, nu, nv: (layer, be[i], 0, 0)),
            pl.BlockSpec((None, 1, D, F), lambda i, be, nu, nv: (layer, be[i], 0, 0)),
            pl.BlockSpec((None, 1, F, D), lambda i, be, nu, nv: (layer, be[i], 0, 0)),
        ],
        out_specs=rows,
        scratch_shapes=[pltpu.VMEM((D, F), BF16), pltpu.VMEM((D, F), BF16), pltpu.VMEM((F, D), BF16)],
    )
    return pl.pallas_call(
        _expert_kernel,
        grid_spec=grid_spec,
        out_shape=jax.ShapeDtypeStruct((P, DP), jnp.int32),
        compiler_params=_cparams("arbitrary"),
        name="experts",
    )(blk_e, n_used, n_valid, xs, w1, w3, w2)


def _block_layout(counts, n_blocks):
    EB = EXPERT_BLOCK
    padded = (counts + EB - 1) // EB * EB
    ex = jnp.arange(N_EXPERTS)
    pad_end = jnp.sum(jnp.where(ex[:, None] <= ex[None, :], padded[:, None], 0), axis=0)
    pad_start = pad_end - padded
    blk_row = (jnp.arange(n_blocks) * EB)[:, None]
    blk_e = jnp.minimum(jnp.sum((pad_end[None, :] <= blk_row).astype(jnp.int32), axis=1), N_EXPERTS - 1)
    n_used = (jnp.sum(padded) // EB).astype(jnp.int32).reshape(1)
    mine = (pad_start[None, :] <= blk_row) & (blk_row < pad_end[None, :])
    n_valid = jnp.sum(jnp.where(mine, jnp.clip(counts[None, :] - (blk_row - pad_start[None, :]), 0, EB), 0), axis=1)
    return pad_start.astype(jnp.int32), blk_e.astype(jnp.int32), n_used, n_valid.astype(jnp.int32)


SC_GATHER_ROWS = 64


def _sc_gather_rows(table, idx):
    info = plsc.get_sparse_core_info()
    nc, ns = info.num_cores, info.num_subcores
    M = idx.shape[0]
    W = table.shape[1]
    b = SC_GATHER_ROWS
    per_worker = M // (nc * ns)
    steps = per_worker // b
    assert per_worker * nc * ns == M and steps * b == per_worker and steps % 2 == 0
    mesh = plsc.VectorSubcoreMesh(core_axis_name="c", subcore_axis_name="s")

    @functools.partial(
        pl.kernel, mesh=mesh,
        out_type=jax.ShapeDtypeStruct((M, W), table.dtype),
        scratch_types=[pltpu.VMEM((2, b), jnp.int32), pltpu.VMEM((2, b, W), table.dtype),
                       pltpu.SemaphoreType.DMA, pltpu.SemaphoreType.DMA],
        name="sc_gather_rows",
    )
    def gather(table_hbm, idx_hbm, out_hbm, idx_v, rows_v, sem0, sem1):
        wid = lax.axis_index("s") * nc + lax.axis_index("c")
        sems = (sem0, sem1)

        def base(s):
            return pl.multiple_of(wid * per_worker + s * b, b)

        def gather_copy(slot):
            return pltpu.make_async_copy(table_hbm.at[idx_v.at[slot]], rows_v.at[slot], sems[slot])

        def start(s, slot):
            pltpu.sync_copy(idx_hbm.at[pl.ds(base(s), b)], idx_v.at[slot])
            gather_copy(slot).start()

        def finish(s, slot):
            gather_copy(slot).wait()
            pltpu.sync_copy(rows_v.at[slot], out_hbm.at[pl.ds(base(s), b)])

        start(0, 0)

        @pl.loop(0, steps, step=2)
        def _(s):
            start(s + 1, 1)
            finish(s, 0)

            @pl.when(s + 2 < steps)
            def _():
                start(s + 2, 0)

            finish(s + 1, 1)

    return gather(table, idx)


def _sc_scatter_rows(rows, idx, n_out):
    info = plsc.get_sparse_core_info()
    nc, ns = info.num_cores, info.num_subcores
    T, W = rows.shape
    G, K_, b = idx.shape
    steps = G // (nc * ns)
    assert steps * nc * ns == G and G * b == T
    mesh = plsc.VectorSubcoreMesh(core_axis_name="c", subcore_axis_name="s")

    @functools.partial(
        pl.kernel, mesh=mesh,
        out_type=jax.ShapeDtypeStruct((n_out, W), rows.dtype),
        scratch_types=[pltpu.VMEM((K_, b), jnp.int32), pltpu.VMEM((b, W), rows.dtype), pltpu.SemaphoreType.DMA],
        name="sc_scatter_rows",
    )
    def scatter(rows_hbm, idx_hbm, out_hbm, idx_v, rows_v, sem):
        wid = lax.axis_index("s") * nc + lax.axis_index("c")

        @pl.loop(0, steps)
        def _(s):
            g = wid * steps + s
            pltpu.sync_copy(idx_hbm.at[g], idx_v)
            pltpu.sync_copy(rows_hbm.at[pl.ds(pl.multiple_of(g * b, b), b)], rows_v)
            for k in range(K_):
                pltpu.async_copy(rows_v, out_hbm.at[idx_v.at[k]], sem).wait()

    return scatter(rows, idx)


def _combine_dense_kernel(rows_ref, w_ref, x_ref, shared_ref, gpost_ref, g2_ref, o_ref):
    w = w_ref[...]
    tt, half = rows_ref.shape[1], rows_ref.shape[2]
    y_lo = jnp.zeros((tt, half), F32)
    y_hi = jnp.zeros((tt, half), F32)
    for k in range(TOP_K):
        lo, hi = _unpack_bf16_pair(rows_ref[k])
        y_lo = y_lo + w[:, k:k + 1] * lo
        y_hi = y_hi + w[:, k:k + 1] * hi
    y = shared_ref[0] + jnp.concatenate([y_lo, y_hi], axis=1)
    o_ref[0] = x_ref[0] + g2_ref[0] * (_rms(y) * gpost_ref[...])


def _combine_dense(rows, w_tok, x, shared, gpost, g2, tt):
    Bn, S, D = x.shape
    K_, T, DP = rows.shape
    nt = S // tt
    seq = pl.BlockSpec((1, tt, D), lambda b, i: (b, i, 0))
    return pl.pallas_call(
        _combine_dense_kernel,
        grid=(Bn, nt),
        in_specs=[pl.BlockSpec((K_, tt, DP), lambda b, i: (0, b * nt + i, 0)),
                  pl.BlockSpec((tt, K_), lambda b, i: (b * nt + i, 0)), seq, seq,
                  pl.BlockSpec((1, D), lambda b, i: (0, 0)), pl.BlockSpec((1, 1, D), lambda b, i: (b, 0, 0))],
        out_specs=seq,
        out_shape=jax.ShapeDtypeStruct((Bn, S, D), F32),
        compiler_params=_cparams("arbitrary", "arbitrary"),
        name="combine_dense",
    )(rows, w_tok, x, shared, gpost.reshape(1, D), g2)


def kernel(x, c, w_ada, b_ada, norm_pre_mix, norm_post_mix, norm_pre_ffn, norm_post_ffn, w_in, w_out, rel_bias_table, diff_lambda, diff_subln, rwkv_mu, rwkv_w0, rwkv_w2, rwkv_a0, rwkv_a2, rwkv_g2, rwkv_k_k, rwkv_k_a, rwkv_r_k, rwkv_lnx_g, rwkv_lnx_b, gmlp_ln_g, gmlp_ln_b, gmlp_w_s, gmlp_b_s, router_w, router_bias, exp_w1, exp_w3, exp_w2, shared_w1, shared_w3, shared_w2):
    Bn, S, D = x.shape
    depth = w_ada.shape[0]
    tm = min(256, S)
    tq = min(512, S // 2)
    t_rwkv = min(512, S)

    mod = _adaln(c, w_ada, b_ada)
    band_t = _attn_band(rel_bias_table, tq)
    zpad = jnp.zeros((B_DECAY_LORA, B_WIDTH), F32)
    for l in range(depth):
        sh1, sc1, g1, sh2, sc2, g2 = [m.reshape(Bn, 1, D) for m in jnp.split(mod[l], 6, axis=-1)]
        w_in_b = w_in[l].astype(BF16)
        pa, vt, pbc = _inproj(x, norm_pre_mix[l], sc1, sh1, w_in_b[:, :2 * A_WIDTH],
                              jnp.transpose(w_in_b[:, 2 * A_WIDTH:A_COLS]), w_in_b[:, A_COLS:], tm)
        lambda_init = 0.8 - 0.6 * math.exp(-0.3 * l)
        ya = _diff_attention(pa, vt, band_t, diff_lambda[l], diff_subln[l], lambda_init, tq)
        prep = _rwkv_prep(pbc, rwkv_mu[l], rwkv_w0[l], jnp.concatenate([rwkv_w2[l], zpad], axis=0),
                          rwkv_a0[l], jnp.concatenate([zpad, rwkv_a2[l]], axis=0), rwkv_g2[l],
                          rwkv_k_k[l], rwkv_k_a[l], rwkv_r_k[l].reshape(-1), t_rwkv)
        yb = _rwkv_scan(*prep, rwkv_lnx_g[l], rwkv_lnx_b[l], t_rwkv)
        yc = _gmlp(pbc, gmlp_ln_g[l], gmlp_ln_b[l], gmlp_w_s[l], gmlp_b_s[l], tm)

        w_out_b = w_out[l].astype(BF16)
        wr_t = jnp.pad(jnp.transpose(router_w[l]), ((0, V7X_LANES - N_EXPERTS), (0, 0)))
        x, h, scores_t, shared = _mid(
            ya, yb, yc, x, w_out_b[:A_WIDTH], w_out_b[A_WIDTH:A_WIDTH + B_WIDTH], w_out_b[A_WIDTH + B_WIDTH:],
            norm_post_mix[l], g1, norm_pre_ffn[l], sc2, sh2, wr_t,
            shared_w1[l].astype(BF16), shared_w3[l].astype(BF16), shared_w2[l].astype(BF16), tm)

        T = Bn * S
        n_blocks = -(-T * TOP_K // EXPERT_BLOCK) + N_EXPERTS
        eidx, wgt, rank, cnt = _route(scores_t, router_bias[l], tm)
        pad_start, blk_e, n_used, n_valid = _block_layout(cnt[:, 0], n_blocks)
        dest = _dest_rows(pad_start, eidx, rank, tm)
        b = SC_GATHER_ROWS
        dest_sc = jnp.transpose(dest.reshape(T // tm, TOP_K, tm // b, b), (0, 2, 1, 3)).reshape(T // b, TOP_K, b)
        xs = _sc_scatter_rows(h.reshape(T, D // 2), dest_sc, n_blocks * EXPERT_BLOCK)
        ys = _experts(blk_e, n_used, n_valid, xs, exp_w1, exp_w3, exp_w2, l)
        dest_kt = jnp.transpose(dest, (1, 0, 2)).reshape(TOP_K * T)
        rows = _sc_gather_rows(ys, dest_kt).reshape(TOP_K, T, D // 2)
        x = _combine_dense(rows, jnp.transpose(wgt), x, shared, norm_post_ffn[l], g2, tm)
    return x
```

```python
import functools
import math

import jax
import jax.numpy as jnp
from jax import lax
from jax.experimental import pallas as pl
from jax.experimental.pallas import tpu as pltpu
from jax.experimental.pallas import tpu_sc as plsc

F32 = jnp.float32
BF16 = jnp.bfloat16

A_HEADS = 4
A_QK_DIM = 64
A_HEAD_W = 2 * A_QK_DIM
A_WIDTH = A_HEADS * A_HEAD_W
N_BUCKETS = 32
MAX_DISTANCE = 128
B_HEADS = 4
B_HEAD_DIM = 64
B_WIDTH = B_HEADS * B_HEAD_DIM
B_DECAY_LORA = 64
B_AAA_LORA = 64
B_GATE_LORA = 128
B_LNX_EPS = 64e-5
C_GROUPS = 4
C_GROUP_DIM = 64
C_WIDTH = C_GROUPS * C_GROUP_DIM
CHUNK = 128
A_COLS = 3 * A_WIDTH
B_COLS = 3 * B_WIDTH + B_DECAY_LORA + B_AAA_LORA + B_GATE_LORA
C_COLS = 2 * C_WIDTH
N_EXPERTS = 64
TOP_K = 8
N_GROUPS = 8
TOPK_GROUPS = 4
EXPERTS_PER_GROUP = N_EXPERTS // N_GROUPS
ROUTED_SCALE = 2.5
EXPERT_BLOCK = 512
RMS_EPS = 1e-6
LN_EPS = 1e-5
NEG_BIG = -1e30

V7X_LANES = 128
V7X_VMEM_LIMIT_BYTES = 56 * 1024 * 1024
RWKV_CHUNK = 64
RWKV_GROUP = 8

NN = (((1,), (0,)), ((), ()))
NT = (((1,), (1,)), ((), ()))
TN = (((0,), (0,)), ((), ()))


def _cparams(*sem):
    return pltpu.CompilerParams(dimension_semantics=sem, vmem_limit_bytes=V7X_VMEM_LIMIT_BYTES)


def _mm(a, b, dims=NN):
    return lax.dot_general(a.astype(BF16), b.astype(BF16), dims, preferred_element_type=F32)


def _split(a):
    hi = a.astype(BF16)
    lo = (a - hi.astype(F32)).astype(BF16)
    return hi, lo


def _mm3(a, b, dims=NN):
    ah, al = _split(a)
    bh, bl = _split(b)
    d = lambda x, y: lax.dot_general(x, y, dims, preferred_element_type=F32)
    return d(ah, bh) + d(ah, bl) + d(al, bh)


def _mm2(a, b_exact, dims=NN):
    ah, al = _split(a)
    d = lambda x: lax.dot_general(x, b_exact, dims, preferred_element_type=F32)
    return d(ah) + d(al)


def _pack_bf16_pair(x):
    n = x.shape[1] // 2
    bits = lax.bitcast_convert_type(x.astype(BF16).astype(F32), jnp.int32)
    return ((bits[:, :n] >> 16) & 0xFFFF) | bits[:, n:]


def _unpack_bf16_pair(u):
    lo = lax.bitcast_convert_type(u << 16, F32)
    hi = lax.bitcast_convert_type(u & jnp.int32(-65536), F32)
    return lo, hi


def _rms(x, eps=RMS_EPS):
    return x * lax.rsqrt(jnp.mean(x * x, axis=-1, keepdims=True) + eps)


def _sigmoid(x):
    return 1.0 / (1.0 + jnp.exp(-x))


def _silu(x):
    return x * _sigmoid(x)


def _adaln_kernel(c_ref, w_ref, b_ref, o_ref):
    c = c_ref[...]
    o_ref[0] = _mm3(_silu(c), w_ref[0]) + b_ref[0]


def _adaln(c, w_ada, b_ada):
    L, D, N = w_ada.shape
    Bn = c.shape[0]
    tn = min(N, 1536)
    return pl.pallas_call(
        _adaln_kernel,
        grid=(L, N // tn),
        in_specs=[
            pl.BlockSpec((Bn, D), lambda l, j: (0, 0)),
            pl.BlockSpec((1, D, tn), lambda l, j: (l, 0, j)),
            pl.BlockSpec((1, 1, tn), lambda l, j: (l, 0, j)),
        ],
        out_specs=pl.BlockSpec((1, Bn, tn), lambda l, j: (l, 0, j)),
        out_shape=jax.ShapeDtypeStruct((L, Bn, N), F32),
        compiler_params=_cparams("arbitrary", "arbitrary"),
        name="adaln",
    )(c, w_ada, b_ada.reshape(L, 1, N))


def _inproj_kernel(x_ref, g_ref, sc_ref, sh_ref, wa_ref, wvt_ref, wbc_ref, oa_ref, ovt_ref, obc_ref):
    x = x_ref[0]
    h = _rms(x) * g_ref[...] * (1.0 + sc_ref[0]) + sh_ref[0]
    hb = h.astype(BF16)
    oa_ref[0] = jnp.dot(hb, wa_ref[...], preferred_element_type=F32).astype(BF16)
    ovt_ref[0] = lax.dot_general(wvt_ref[...], hb, NT, preferred_element_type=F32).astype(BF16)
    obc_ref[0] = jnp.dot(hb, wbc_ref[...], preferred_element_type=F32)


def _inproj(x, g, sc, sh, wa, wvt, wbc, tm):
    Bn, S, D = x.shape
    na, nv, nbc = wa.shape[1], wvt.shape[0], wbc.shape[1]
    return pl.pallas_call(
        _inproj_kernel,
        grid=(Bn, S // tm),
        in_specs=[
            pl.BlockSpec((1, tm, D), lambda b, i: (b, i, 0)),
            pl.BlockSpec((1, D), lambda b, i: (0, 0)),
            pl.BlockSpec((1, 1, D), lambda b, i: (b, 0, 0)),
            pl.BlockSpec((1, 1, D), lambda b, i: (b, 0, 0)),
            pl.BlockSpec((D, na), lambda b, i: (0, 0)),
            pl.BlockSpec((nv, D), lambda b, i: (0, 0)),
            pl.BlockSpec((D, nbc), lambda b, i: (0, 0)),
        ],
        out_specs=[
            pl.BlockSpec((1, tm, na), lambda b, i: (b, i, 0)),
            pl.BlockSpec((1, nv, tm), lambda b, i: (b, 0, i)),
            pl.BlockSpec((1, tm, nbc), lambda b, i: (b, i, 0)),
        ],
        out_shape=[
            jax.ShapeDtypeStruct((Bn, S, na), BF16),
            jax.ShapeDtypeStruct((Bn, nv, S), BF16),
            jax.ShapeDtypeStruct((Bn, S, nbc), F32),
        ],
        compiler_params=_cparams("arbitrary", "arbitrary"),
        name="inproj",
    )(x, g.reshape(1, D), sc, sh, wa, wvt, wbc)


def _t5_bucket(dist):
    n = jnp.maximum(dist, 0)
    max_exact = N_BUCKETS // 2
    nf = jnp.maximum(n, 1).astype(F32)
    large = max_exact + (jnp.log(nf / max_exact) / math.log(MAX_DISTANCE / max_exact)
                         * (N_BUCKETS - max_exact)).astype(jnp.int32)
    large = jnp.minimum(large, N_BUCKETS - 1)
    return jnp.where(n < max_exact, n, large)


def _attn_band(table, tq):
    far = table[N_BUCKETS - 1].astype(F32)
    H = table.shape[1]
    nb = V7X_LANES
    L = 3 * nb
    m = jnp.arange(L)
    m = jnp.where(m < nb, m, m - L)
    cache = {}

    def block(c):
        if c not in cache:
            if c - (nb - 1) >= MAX_DISTANCE:
                cache[c] = jnp.zeros((H, nb, nb), F32)
            elif c + (nb - 1) < 0:
                cache[c] = jnp.full((H, nb, nb), NEG_BIG, F32)
            else:
                dist = m + c
                vals = jnp.where(dist[None] >= 0,
                                 jnp.transpose(table[_t5_bucket(dist)].astype(F32)) - far[:, None], NEG_BIG)
                cache[c] = jnp.tile(vals, (1, nb))[:, :nb * (L - 1)].reshape(H, nb, L - 1)[:, :, :nb]
        return cache[c]

    bands = []
    for off in (0, tq):
        rows = [jnp.concatenate([block(nb * (a - b) + off) for a in range(tq // nb)], axis=2)
                for b in range(2 * tq // nb)]
        bands.append(jnp.concatenate(rows, axis=1))
    return jnp.stack(bands)


def _attn_kernel(lam_ref, q_ref, k_ref, vt_ref, band_ref, g_ref, o_ref, *, tq, lambda_init):
    i = pl.program_id(2)
    q = q_ref[0] * jnp.asarray(A_QK_DIM ** -0.5, BF16)
    lane = lax.broadcasted_iota(jnp.int32, q.shape, 1)
    zero = jnp.zeros_like(q)
    qq = jnp.concatenate([jnp.where(lane < A_QK_DIM, q, zero),
                          jnp.where(lane >= A_QK_DIM, q, zero)], axis=0)

    kb0 = pl.multiple_of(jnp.maximum(i - 1, 0) * tq, tq)
    kb = k_ref[0, pl.ds(kb0, 2 * tq), :]
    band = band_ref[0, 0]
    s = lax.dot_general(kb, qq, NT, preferred_element_type=F32) + jnp.concatenate([band, band], axis=1)
    m = jnp.max(s, axis=0, keepdims=True)
    p = jnp.exp(s - m)
    l = jnp.sum(p, axis=0, keepdims=True)
    acc = jnp.dot(vt_ref[0, :, pl.ds(kb0, 2 * tq)], p.astype(BF16), preferred_element_type=F32)

    n_far = jnp.maximum(i - 1, 0)

    def logits(j):
        return lax.dot_general(k_ref[0, pl.ds(pl.multiple_of(j * tq, tq), tq), :], qq, NT,
                               preferred_element_type=F32)

    def body(j, carry):
        m, l, acc, s = carry
        s_next = logits(jnp.minimum(j + 1, n_far - 1))
        vtj = vt_ref[0, :, pl.ds(pl.multiple_of(j * tq, tq), tq)]
        m_new = jnp.maximum(m, jnp.max(s, axis=0, keepdims=True))
        alpha = jnp.exp(m - m_new)
        p = jnp.exp(s - m_new)
        l = alpha * l + jnp.sum(p, axis=0, keepdims=True)
        acc = alpha * acc + jnp.dot(vtj, p.astype(BF16), preferred_element_type=F32)
        return m_new, l, acc, s_next

    m, l, acc, _ = lax.fori_loop(0, n_far, body, (m, l, acc, logits(0)))

    lp = lam_ref[...]
    lam = (jnp.exp(jnp.sum(lp[0:1] * lp[1:2], axis=-1, keepdims=True))
           - jnp.exp(jnp.sum(lp[2:3] * lp[3:4], axis=-1, keepdims=True)) + lambda_init)
    o = acc / l
    o = o[:, :tq] - lam * o[:, tq:]
    o = o * lax.rsqrt(jnp.mean(o * o, axis=0, keepdims=True) + RMS_EPS) * g_ref[...] * (1.0 - lambda_init)
    o_ref[0] = jnp.transpose(o)


def _diff_attention(pa, vt, band_t, lam_par, subln_g, lambda_init, tq):
    Bn, S, _ = pa.shape
    W = A_HEAD_W
    kern = functools.partial(_attn_kernel, tq=tq, lambda_init=lambda_init)
    return pl.pallas_call(
        kern,
        grid=(Bn, A_HEADS, S // tq),
        in_specs=[
            pl.BlockSpec((4, A_QK_DIM), lambda b, h, i: (0, 0)),
            pl.BlockSpec((1, tq, W), lambda b, h, i: (b, i, h)),
            pl.BlockSpec((1, S, W), lambda b, h, i: (b, 0, A_HEADS + h)),
            pl.BlockSpec((1, W, S), lambda b, h, i: (b, h, 0)),
            pl.BlockSpec((1, 1, 2 * tq, tq), lambda b, h, i: (jnp.minimum(i, 1), h, 0, 0)),
            pl.BlockSpec((W, 1), lambda b, h, i: (0, 0)),
        ],
        out_specs=pl.BlockSpec((1, tq, W), lambda b, h, i: (b, i, h)),
        out_shape=jax.ShapeDtypeStruct((Bn, S, A_WIDTH), F32),
        compiler_params=_cparams("arbitrary", "arbitrary", "arbitrary"),
        name="diff_attn",
    )(lam_par, pa, pa, vt, band_t, subln_g.reshape(W, 1))


def _head_ones(n):
    r = lax.broadcasted_iota(jnp.int32, (n, n), 0) // B_HEAD_DIM
    c = lax.broadcasted_iota(jnp.int32, (n, n), 1) // B_HEAD_DIM
    return (r == c).astype(BF16)


def _rwkv_prep_kernel(pb_ref, prev_ref, mu_ref, w0_ref, w2_ref, a0_ref, a2_ref, g2_ref,
                      kk_ref, ka_ref, rk_ref,
                      rt_ref, at_ref, kt_ref, bt_ref, v_ref, wc_ref, bonus_ref, g_ref, *, tm):
    i = pl.program_id(1)
    C = RWKV_CHUNK
    x = pb_ref[0]
    row = lax.broadcasted_iota(jnp.int32, x.shape, 0)
    last = prev_ref[0, 7:8, :] * (i > 0).astype(F32)
    prev = jnp.where(row == 0, last, pltpu.roll(x, 1, 0))
    p = x + (prev - x) * mu_ref[...]
    o1, o2, o3 = B_WIDTH, 2 * B_WIDTH, 3 * B_WIDTH
    r, k, v = p[:, :o1], p[:, o1:o2], p[:, o2:o3]
    lora = p[:, o3:o3 + B_DECAY_LORA + B_AAA_LORA]
    gd = p[:, o3 + B_DECAY_LORA + B_AAA_LORA:]

    z = -(w0_ref[...] + _mm3(jnp.tanh(lora), w2_ref[...]))
    softplus = jnp.maximum(z, 0.0) + jnp.log(1.0 + jnp.exp(-jnp.abs(z)))
    logw = -jnp.exp(-softplus - 0.5)
    a = _sigmoid(a0_ref[...] + _mm3(lora, a2_ref[...]))
    g_ref[0] = _mm3(_sigmoid(gd), g2_ref[...])

    ones = _head_ones(B_WIDTH)
    kk = k * kk_ref[...]
    kk = kk * lax.rsqrt(jnp.maximum(_mm2(kk * kk, ones), 1e-24))
    k2 = k * (1.0 + (a - 1.0) * ka_ref[...])
    bonus_ref[0] = _mm2(r * k2 * rk_ref[...], ones) * v

    t_in = lax.broadcasted_iota(jnp.int32, (tm, B_WIDTH), 0) % C
    cum = logw
    sh = 1
    while sh < C:
        cum = cum + jnp.where(t_in >= sh, pltpu.roll(cum, sh, 0), 0.0)
        sh *= 2
    n = tm // C
    wc_ref[0] = jnp.exp(jnp.sum(logw.reshape(n, C, B_WIDTH), axis=1))
    e_pos = jnp.exp(cum)
    e_neg = jnp.exp(-cum)
    rt_ref[0] = r * e_pos
    at_ref[0] = -kk * jnp.exp(cum - logw)
    kt_ref[0] = k2 * e_neg
    bt_ref[0] = kk * a * e_neg
    v_ref[0] = v


def _rwkv_prep(pbc, mu, w0, w2p, a0, a2p, g2, k_k, k_a, r_k, tm):
    Bn, S, _ = pbc.shape
    W = B_WIDTH
    nl = B_DECAY_LORA + B_AAA_LORA
    row = lambda a: a.reshape(1, -1)
    full = lambda shp: pl.BlockSpec(shp, lambda b, i: (0,) * len(shp))
    seq = pl.BlockSpec((1, tm, W), lambda b, i: (b, i, 0))
    seq_shape = jax.ShapeDtypeStruct((Bn, S, W), F32)
    n = tm // RWKV_CHUNK
    return pl.pallas_call(
        functools.partial(_rwkv_prep_kernel, tm=tm),
        grid=(Bn, S // tm),
        in_specs=[
            pl.BlockSpec((1, tm, B_COLS), lambda b, i: (b, i, 0)),
            pl.BlockSpec((1, 8, B_COLS), lambda b, i: (b, jnp.maximum(i * (tm // 8) - 1, 0), 0)),
            full((1, B_COLS)), full((1, W)), full((nl, W)), full((1, W)), full((nl, W)),
            full((B_GATE_LORA, W)), full((1, W)), full((1, W)), full((1, W)),
        ],
        out_specs=[seq, seq, seq, seq, seq,
                   pl.BlockSpec((1, n, W), lambda b, i: (b, i, 0)), seq, seq],
        out_shape=[seq_shape] * 5 + [jax.ShapeDtypeStruct((Bn, S // RWKV_CHUNK, W), F32)] + [seq_shape] * 2,
        compiler_params=_cparams("arbitrary", "arbitrary"),
        name="rwkv_prep",
    )(pbc, pbc, row(mu), row(w0), w2p, row(a0), a2p, g2, row(k_k), row(k_a), row(r_k))


def _rwkv_scan_kernel(rt_ref, at_ref, kt_ref, bt_ref, v_ref, wc_ref, bonus_ref, g_ref,
                      lng_ref, lnb_ref, o_ref, state, *, tt):
    C = RWKV_CHUNK
    W = B_WIDTH

    @pl.when(pl.program_id(1) == 0)
    def _():
        state[...] = jnp.zeros_like(state)

    lane_head = lax.broadcasted_iota(jnp.int32, (C, W), 1) // B_HEAD_DIM
    tt_i = lax.broadcasted_iota(jnp.int32, (C, W), 0)
    ss_i = lax.broadcasted_iota(jnp.int32, (C, W), 1) % C
    strict = tt_i > ss_i
    incl = tt_i >= ss_i
    eye = (tt_i == ss_i).astype(F32)
    ones = _head_ones(W)
    bd_mask = ones.astype(F32)

    head_mask = [(lane_head == h).astype(BF16) for h in range(B_HEADS)]

    def bd_split(x):
        xb = x.astype(BF16)
        return jnp.concatenate([xb * mk for mk in head_mask], axis=0)

    def mm_bd(a, b_bd, dims=NN):
        return lax.dot_general(a.astype(BF16), b_bd, dims, preferred_element_type=F32)

    def state_free(gi):
        G = range(RWKV_GROUP)
        sls = [pl.ds(pl.multiple_of((gi * RWKV_GROUP + j) * C, C), C) for j in G]
        rt = [rt_ref[0, sl, :] for sl in sls]
        at = [at_ref[0, sl, :] for sl in sls]
        kt = [kt_ref[0, sl, :] for sl in sls]
        bt = [bt_ref[0, sl, :] for sl in sls]
        v = [v_ref[0, sl, :] for sl in sls]
        wc = [wc_ref[0, pl.ds(gi * RWKV_GROUP + j, 1), :] for j in G]
        ar = [jnp.concatenate([at[j], rt[j]], axis=0) for j in G]
        bdb = [bd_split(bt[j]) for j in G]
        bdk = [bd_split(kt[j]) for j in G]
        a_b = [mm_bd(ar[j], bdb[j], NT) for j in G]
        a_k = [mm_bd(ar[j], bdk[j], NT) for j in G]
        lo = [jnp.where(strict, a_b[j][:C], 0.0) for j in G]
        a_ak = [jnp.where(strict, a_k[j][:C], 0.0) for j in G]
        a_rb = [jnp.where(incl, a_b[j][C:], 0.0) for j in G]
        a_rk = [jnp.where(incl, a_k[j][C:], 0.0) for j in G]
        pw = lo
        tinv = [eye + lo[j] for j in G]
        bdp = [bd_split(pw[j]) for j in G]
        span = 2
        while span < C:
            pw = [mm_bd(pw[j], bdp[j]) for j in G]
            bdp = [bd_split(pw[j]) for j in G]
            tinv = [tinv[j] + mm_bd(tinv[j], bdp[j]) for j in G]
            span *= 2
        bdv = [bd_split(v[j]) for j in G]
        bda = [bd_split(at[j]) for j in G]
        abar = [mm_bd(tinv[j], bda[j]) for j in G]
        akv = [bd_split(mm_bd(a_ak[j], bdv[j])) for j in G]
        u0 = [mm_bd(tinv[j], akv[j]) for j in G]
        y0 = [mm_bd(a_rk[j], bdv[j]) for j in G]
        kv = [_mm(v[j], kt[j] * wc[j], TN) * bd_mask for j in G]
        return [(jnp.concatenate([abar[j], rt[j]], axis=0), u0[j], y0[j], a_rb[j], bt[j] * wc[j], kv[j], wc[j])
                for j in G]

    def group(gi, carry):
        pre = state_free(gi)
        s = state[...]
        ys = []
        for abar_rt, u0, y0, a_rb, btw, kv, wc in pre:
            a_s = _mm(abar_rt, s, NT)
            u = a_s[:C] + u0
            ys.append(a_s[C:] + y0 + mm_bd(a_rb, bd_split(u)))
            s = s * wc + _mm(u, btw, TN) * bd_mask + kv
        state[...] = s
        y = jnp.concatenate(ys, axis=0)
        sl = pl.ds(pl.multiple_of(gi * (RWKV_GROUP * C), RWKV_GROUP * C), RWKV_GROUP * C)
        mean = _mm2(y, ones) * (1.0 / B_HEAD_DIM)
        d = y - mean
        var = _mm2(d * d, ones) * (1.0 / B_HEAD_DIM)
        yn = d * lax.rsqrt(var + B_LNX_EPS) * lng_ref[...] + lnb_ref[...]
        o_ref[0, sl, :] = (yn + bonus_ref[0, sl, :]) * g_ref[0, sl, :]
        return carry

    lax.fori_loop(0, tt // (RWKV_GROUP * C), group, 0)


def _rwkv_scan(rt, at, kt, bt, v, wc, bonus, g, lnx_g, lnx_b, tt):
    Bn, S, W = rt.shape
    n = tt // RWKV_CHUNK
    seq = pl.BlockSpec((1, tt, W), lambda b, i: (b, i, 0))
    vec = pl.BlockSpec((1, W), lambda b, i: (0, 0))
    return pl.pallas_call(
        functools.partial(_rwkv_scan_kernel, tt=tt),
        grid=(Bn, S // tt),
        in_specs=[seq, seq, seq, seq, seq, pl.BlockSpec((1, n, W), lambda b, i: (b, i, 0)), seq, seq, vec, vec],
        out_specs=seq,
        out_shape=jax.ShapeDtypeStruct((Bn, S, W), F32),
        scratch_shapes=[pltpu.VMEM((B_HEADS * B_HEAD_DIM, W), F32)],
        compiler_params=_cparams("arbitrary", "arbitrary"),
        name="rwkv_scan",
    )(rt, at, kt, bt, v, wc, bonus, g, lnx_g.reshape(1, W), lnx_b.reshape(1, W))


def _gmlp_kernel(pc_ref, lng_ref, lnb_ref, ws_ref, bs_ref, o_ref, *, tm):
    x = pc_ref[0]
    z = x * (0.5 * (1.0 + jnp.tanh(math.sqrt(2.0 / math.pi) * (x + 0.044715 * (x * x * x)))))
    u, v = z[:, :C_WIDTH], z[:, C_WIDTH:]
    mu = jnp.mean(v, axis=-1, keepdims=True)
    d = v - mu
    var = jnp.mean(d * d, axis=-1, keepdims=True)
    vn = d * lax.rsqrt(var + LN_EPS) * lng_ref[...] + lnb_ref[...]
    group = lax.broadcasted_iota(jnp.int32, (CHUNK, C_WIDTH), 1) // C_GROUP_DIM
    tril = (lax.broadcasted_iota(jnp.int32, (CHUNK, CHUNK), 0)
            >= lax.broadcasted_iota(jnp.int32, (CHUNK, CHUNK), 1))
    ws = [jnp.where(tril, ws_ref[gi], 0.0).astype(BF16) for gi in range(C_GROUPS)]
    for c in range(tm // CHUNK):
        sl = slice(c * CHUNK, (c + 1) * CHUNK)
        vc = vn[sl].astype(BF16)
        sv = bs_ref[...]
        for gi in range(C_GROUPS):
            t = jnp.dot(ws[gi], vc, preferred_element_type=F32)
            sv = sv + jnp.where(group == gi, t, 0.0)
        o_ref[0, sl, :] = u[sl] * sv


def _gmlp(pbc, ln_g, ln_b, w_s, b_s, tm):
    Bn, S, _ = pbc.shape
    bs_wide = jnp.repeat(jnp.transpose(b_s), C_GROUP_DIM, axis=1)
    return pl.pallas_call(
        functools.partial(_gmlp_kernel, tm=tm),
        grid=(Bn, S // tm),
        in_specs=[
            pl.BlockSpec((1, tm, C_COLS), lambda b, i: (b, i, B_COLS // C_COLS)),
            pl.BlockSpec((1, C_WIDTH), lambda b, i: (0, 0)),
            pl.BlockSpec((1, C_WIDTH), lambda b, i: (0, 0)),
            pl.BlockSpec((C_GROUPS, CHUNK, CHUNK), lambda b, i: (0, 0, 0)),
            pl.BlockSpec((CHUNK, C_WIDTH), lambda b, i: (0, 0)),
        ],
        out_specs=pl.BlockSpec((1, tm, C_WIDTH), lambda b, i: (b, i, 0)),
        out_shape=jax.ShapeDtypeStruct((Bn, S, C_WIDTH), F32),
        compiler_params=_cparams("arbitrary", "arbitrary"),
        name="gmlp",
    )(pbc, ln_g.reshape(1, -1), ln_b.reshape(1, -1), w_s, bs_wide)


def _mid_kernel(ya_ref, yb_ref, yc_ref, x_ref, woa_ref, wob_ref, woc_ref, gpost_ref, g1_ref,
                gpre_ref, sc_ref, sh_ref, wr_ref, ws1_ref, ws3_ref, ws2_ref,
                xo_ref, h_ref, score_ref, shared_ref):
    y = (_mm(ya_ref[0], woa_ref[...]) + _mm(yb_ref[0], wob_ref[...]) + _mm(yc_ref[0], woc_ref[...]))
    xn = x_ref[0] + g1_ref[0] * (_rms(y) * gpost_ref[...])
    xo_ref[0] = xn
    h = _rms(xn) * gpre_ref[...] * (1.0 + sc_ref[0]) + sh_ref[0]
    h_ref[0] = _pack_bf16_pair(h)
    score_ref[0] = _sigmoid(_mm3(wr_ref[...], h, NT))
    hb = h.astype(BF16)
    t = _silu(jnp.dot(hb, ws1_ref[...], preferred_element_type=F32)) * jnp.dot(
        hb, ws3_ref[...], preferred_element_type=F32)
    shared_ref[0] = jnp.dot(t.astype(BF16), ws2_ref[...], preferred_element_type=F32)


def _mid(ya, yb, yc, x, woa, wob, woc, gpost, g1, gpre, sc, sh, wr, ws1, ws3, ws2, tm):
    Bn, S, D = x.shape
    NR = wr.shape[0]
    F = ws1.shape[1]
    seq = lambda w: pl.BlockSpec((1, tm, w), lambda b, i: (b, i, 0))
    full = lambda shp: pl.BlockSpec(shp, lambda b, i: (0,) * len(shp))
    per_b = pl.BlockSpec((1, 1, D), lambda b, i: (b, 0, 0))
    return pl.pallas_call(
        _mid_kernel,
        grid=(Bn, S // tm),
        in_specs=[seq(A_WIDTH), seq(B_WIDTH), seq(C_WIDTH), seq(D),
                  full((A_WIDTH, D)), full((B_WIDTH, D)), full((C_WIDTH, D)),
                  full((1, D)), per_b, full((1, D)), per_b, per_b,
                  full((NR, D)), full((D, F)), full((D, F)), full((F, D))],
        out_specs=[seq(D), seq(D // 2), pl.BlockSpec((1, NR, tm), lambda b, i: (b, 0, i)), seq(D)],
        out_shape=[jax.ShapeDtypeStruct((Bn, S, D), F32), jax.ShapeDtypeStruct((Bn, S, D // 2), jnp.int32),
                   jax.ShapeDtypeStruct((Bn, NR, S), F32), jax.ShapeDtypeStruct((Bn, S, D), F32)],
        compiler_params=_cparams("arbitrary", "arbitrary"),
        name="mid",
    )(ya, yb, yc, x, woa, wob, woc, gpost.reshape(1, D), g1, gpre.reshape(1, D), sc, sh, wr, ws1, ws3, ws2)


def _first_argmax(vals, iota, n):
    m = jnp.max(vals, axis=0, keepdims=True)
    idx = jnp.min(jnp.where(vals == m, iota, n), axis=0, keepdims=True)
    return m, idx


def _route_kernel(sc_ref, bias_ref, e_ref, w_ref, r_ref, cnt_ref, carry, *, tm):
    @pl.when((pl.program_id(0) == 0) & (pl.program_id(1) == 0))
    def _():
        carry[...] = jnp.zeros_like(carry)

    G = EXPERTS_PER_GROUP
    s = sc_ref[0]
    biased = s + bias_ref[...]
    neg_inf = jnp.float32(-jnp.inf)
    io8 = lax.broadcasted_iota(jnp.int32, (G, tm), 0)
    gs_rows = []
    for g in range(N_GROUPS):
        blk = biased[g * G:(g + 1) * G]
        m1, i1 = _first_argmax(blk, io8, G)
        m2 = jnp.max(jnp.where(io8 == i1, neg_inf, blk), axis=0, keepdims=True)
        gs_rows.append(m1 + m2)
    gs = jnp.concatenate(gs_rows, axis=0)
    gio = lax.broadcasted_iota(jnp.int32, (N_GROUPS, tm), 0)
    gsel = jnp.zeros((N_GROUPS, tm), jnp.bool_)
    for _ in range(TOPK_GROUPS):
        _, gi = _first_argmax(gs, gio, N_GROUPS)
        pick = gio == gi
        gsel = gsel | pick
        gs = jnp.where(pick, neg_inf, gs)
    masked = jnp.concatenate(
        [jnp.where(gsel[g:g + 1], biased[g * G:(g + 1) * G], neg_inf) for g in range(N_GROUPS)], axis=0)

    eio = lax.broadcasted_iota(jnp.int32, (N_EXPERTS, tm), 0)
    picks, e_rows, s_rows = [], [], []
    for _ in range(TOP_K):
        _, ei = _first_argmax(masked, eio, N_EXPERTS)
        pick = eio == ei
        picks.append(pick)
        e_rows.append(ei)
        s_rows.append(jnp.sum(jnp.where(pick, s, 0.0), axis=0, keepdims=True))
        masked = jnp.where(pick, neg_inf, masked)
    top_s = jnp.concatenate(s_rows, axis=0)
    w_ref[...] = top_s / (jnp.sum(top_s, axis=0, keepdims=True) + 1e-20) * ROUTED_SCALE
    e_ref[...] = jnp.concatenate(e_rows, axis=0)

    sel = jnp.zeros((N_EXPERTS, tm), F32)
    for pick in picks:
        sel = sel + pick.astype(F32)
    before = (lax.broadcasted_iota(jnp.int32, (tm, tm), 0) < lax.broadcasted_iota(jnp.int32, (tm, tm), 1))
    pos = carry[...] + jnp.dot(sel.astype(BF16), before.astype(BF16), preferred_element_type=F32)
    r_ref[...] = jnp.concatenate(
        [jnp.sum(jnp.where(pick, pos, 0.0), axis=0, keepdims=True) for pick in picks], axis=0).astype(jnp.int32)
    total = carry[...] + jnp.sum(sel, axis=1, keepdims=True)
    carry[...] = total
    cnt_ref[...] = jnp.broadcast_to(total, cnt_ref.shape).astype(jnp.int32)


def _route(scores_t, e_bias, tm):
    Bn, _, S = scores_t.shape
    T = Bn * S
    nt = S // tm
    tok = pl.BlockSpec((TOP_K, tm), lambda b, i: (0, b * nt + i))
    return pl.pallas_call(
        functools.partial(_route_kernel, tm=tm),
        grid=(Bn, nt),
        in_specs=[pl.BlockSpec((1, N_EXPERTS, tm), lambda b, i: (b, 0, i)),
                  pl.BlockSpec((N_EXPERTS, 1), lambda b, i: (0, 0))],
        out_specs=[tok, tok, tok, pl.BlockSpec((N_EXPERTS, V7X_LANES), lambda b, i: (0, 0))],
        out_shape=[jax.ShapeDtypeStruct((TOP_K, T), jnp.int32), jax.ShapeDtypeStruct((TOP_K, T), F32),
                   jax.ShapeDtypeStruct((TOP_K, T), jnp.int32),
                   jax.ShapeDtypeStruct((N_EXPERTS, V7X_LANES), jnp.int32)],
        scratch_shapes=[pltpu.VMEM((N_EXPERTS, 1), F32)],
        compiler_params=_cparams("arbitrary", "arbitrary"),
        name="route",
    )(scores_t, e_bias.reshape(N_EXPERTS, 1))


def _dest_kernel(start_ref, e_ref, r_ref, o_ref):
    e = e_ref[...]
    acc = r_ref[...]
    for ex in range(N_EXPERTS):
        acc = acc + jnp.where(e == ex, start_ref[ex], 0)
    o_ref[0] = acc


def _dest_rows(pad_start, eidx, rank, tt):
    K_, T = eidx.shape
    grid_spec = pltpu.PrefetchScalarGridSpec(
        num_scalar_prefetch=1,
        grid=(T // tt,),
        in_specs=[pl.BlockSpec((K_, tt), lambda i, st: (0, i)), pl.BlockSpec((K_, tt), lambda i, st: (0, i))],
        out_specs=pl.BlockSpec((1, K_, tt), lambda i, st: (i, 0, 0)),
    )
    return pl.pallas_call(
        _dest_kernel,
        grid_spec=grid_spec,
        out_shape=jax.ShapeDtypeStruct((T // tt, K_, tt), jnp.int32),
        compiler_params=_cparams("arbitrary"),
        name="dest_rows",
    )(pad_start, eidx, rank)


def _expert_kernel(blk_e_ref, n_used_ref, n_valid_ref, x_ref, w1_ref, w3_ref, w2_ref, o_ref, w1b, w3b, w2b):
    i = pl.program_id(0)

    @pl.when((i == 0) | (blk_e_ref[i] != blk_e_ref[jnp.maximum(i - 1, 0)]))
    def _():
        w1b[...] = w1_ref[0].astype(BF16)
        w3b[...] = w3_ref[0].astype(BF16)
        w2b[...] = w2_ref[0].astype(BF16)

    @pl.when(i < n_used_ref[0])
    def _():
        row = lax.broadcasted_iota(jnp.int32, x_ref.shape, 0)
        x_lo, x_hi = _unpack_bf16_pair(jnp.where(row < n_valid_ref[i], x_ref[...], 0))
        x_lo, x_hi = x_lo.astype(BF16), x_hi.astype(BF16)
        half = x_lo.shape[1]

        def up(wb):
            return (jnp.dot(x_lo, wb[:half, :], preferred_element_type=F32)
                    + jnp.dot(x_hi, wb[half:, :], preferred_element_type=F32))

        t = _silu(up(w1b)) * up(w3b)
        o_ref[...] = _pack_bf16_pair(jnp.dot(t.astype(BF16), w2b[...], preferred_element_type=F32))


def _experts(blk_e, n_used, n_valid, xs, w1, w3, w2, layer):
    P, DP = xs.shape
    EB = EXPERT_BLOCK
    n_blocks = blk_e.shape[0]
    D, F = w1.shape[2], w1.shape[3]
    rows = pl.BlockSpec((EB, DP), lambda i, be, nu, nv: (jnp.minimum(i, nu[0] - 1), 0))
    grid_spec = pltpu.PrefetchScalarGridSpec(
        num_scalar_prefetch=3,
        grid=(n_blocks,),
        in_specs=[
            rows,
            pl.BlockSpec((None, 1, D, F), lambda i, be, nu, nv: (layer, be[i], 0, 0)),
            pl.BlockSpec((None, 1, D, F), lambda i, be, nu, nv: (layer, be[i], 0, 0)),
            pl.BlockSpec((None, 1, F, D), lambda i, be, nu, nv: (layer, be[i], 0, 0)),
        ],
        out_specs=rows,
        scratch_shapes=[pltpu.VMEM((D, F), BF16), pltpu.VMEM((D, F), BF16), pltpu.VMEM((F, D), BF16)],
    )
    return pl.pallas_call(
        _expert_kernel,
        grid_spec=grid_spec,
        out_shape=jax.ShapeDtypeStruct((P, DP), jnp.int32),
        compiler_params=_cparams("arbitrary"),
        name="experts",
    )(blk_e, n_used, n_valid, xs, w1, w3, w2)


def _block_layout(counts, n_blocks):
    EB = EXPERT_BLOCK
    padded = (counts + EB - 1) // EB * EB
    ex = jnp.arange(N_EXPERTS)
    pad_end = jnp.sum(jnp.where(ex[:, None] <= ex[None, :], padded[:, None], 0), axis=0)
    pad_start = pad_end - padded
    blk_row = (jnp.arange(n_blocks) * EB)[:, None]
    blk_e = jnp.minimum(jnp.sum((pad_end[None, :] <= blk_row).astype(jnp.int32), axis=1), N_EXPERTS - 1)
    n_used = (jnp.sum(padded) // EB).astype(jnp.int32).reshape(1)
    mine = (pad_start[None, :] <= blk_row) & (blk_row < pad_end[None, :])
    n_valid = jnp.sum(jnp.where(mine, jnp.clip(counts[None, :] - (blk_row - pad_start[None, :]), 0, EB), 0), axis=1)
    return pad_start.astype(jnp.int32), blk_e.astype(jnp.int32), n_used, n_valid.astype(jnp.int32)


SC_GATHER_ROWS = 64


def _sc_gather_rows(table, idx):
    info = plsc.get_sparse_core_info()
    nc, ns = info.num_cores, info.num_subcores
    M = idx.shape[0]
    W = table.shape[1]
    b = SC_GATHER_ROWS
    per_worker = M // (nc * ns)
    steps = per_worker // b
    assert per_worker * nc * ns == M and steps * b == per_worker and steps % 2 == 0
    mesh = plsc.VectorSubcoreMesh(core_axis_name="c", subcore_axis_name="s")

    @functools.partial(
        pl.kernel, mesh=mesh,
        out_type=jax.ShapeDtypeStruct((M, W), table.dtype),
        scratch_types=[pltpu.VMEM((2, b), jnp.int32), pltpu.VMEM((2, b, W), table.dtype),
                       pltpu.SemaphoreType.DMA, pltpu.SemaphoreType.DMA],
        name="sc_gather_rows",
    )
    def gather(table_hbm, idx_hbm, out_hbm, idx_v, rows_v, sem0, sem1):
        wid = lax.axis_index("s") * nc + lax.axis_index("c")
        sems = (sem0, sem1)

        def base(s):
            return pl.multiple_of(wid * per_worker + s * b, b)

        def gather_copy(slot):
            return pltpu.make_async_copy(table_hbm.at[idx_v.at[slot]], rows_v.at[slot], sems[slot])

        def start(s, slot):
            pltpu.sync_copy(idx_hbm.at[pl.ds(base(s), b)], idx_v.at[slot])
            gather_copy(slot).start()

        def finish(s, slot):
            gather_copy(slot).wait()
            pltpu.sync_copy(rows_v.at[slot], out_hbm.at[pl.ds(base(s), b)])

        start(0, 0)

        @pl.loop(0, steps, step=2)
        def _(s):
            start(s + 1, 1)
            finish(s, 0)

            @pl.when(s + 2 < steps)
            def _():
                start(s + 2, 0)

            finish(s + 1, 1)

    return gather(table, idx)


def _sc_scatter_rows(rows, idx, n_out):
    info = plsc.get_sparse_core_info()
    nc, ns = info.num_cores, info.num_subcores
    T, W = rows.shape
    G, K_, b = idx.shape
    steps = G // (nc * ns)
    assert steps * nc * ns == G and G * b == T
    mesh = plsc.VectorSubcoreMesh(core_axis_name="c", subcore_axis_name="s")

    @functools.partial(
        pl.kernel, mesh=mesh,
        out_type=jax.ShapeDtypeStruct((n_out, W), rows.dtype),
        scratch_types=[pltpu.VMEM((K_, b), jnp.int32), pltpu.VMEM((b, W), rows.dtype), pltpu.SemaphoreType.DMA],
        name="sc_scatter_rows",
    )
    def scatter(rows_hbm, idx_hbm, out_hbm, idx_v, rows_v, sem):
        wid = lax.axis_index("s") * nc + lax.axis_index("c")

        @pl.loop(0, steps)
        def _(s):
            g = wid * steps + s
            pltpu.sync_copy(idx_hbm.at[g], idx_v)
            pltpu.sync_copy(rows_hbm.at[pl.ds(pl.multiple_of(g * b, b), b)], rows_v)
            for k in range(K_):
                pltpu.async_copy(rows_v, out_hbm.at[idx_v.at[k]], sem).wait()

    return scatter(rows, idx)


def _combine_dense_kernel(rows_ref, w_ref, x_ref, shared_ref, gpost_ref, g2_ref, o_ref):
    w = w_ref[...]
    tt, half = rows_ref.shape[1], rows_ref.shape[2]
    y_lo = jnp.zeros((tt, half), F32)
    y_hi = jnp.zeros((tt, half), F32)
    for k in range(TOP_K):
        lo, hi = _unpack_bf16_pair(rows_ref[k])
        y_lo = y_lo + w[:, k:k + 1] * lo
        y_hi = y_hi + w[:, k:k + 1] * hi
    y = shared_ref[0] + jnp.concatenate([y_lo, y_hi], axis=1)
    o_ref[0] = x_ref[0] + g2_ref[0] * (_rms(y) * gpost_ref[...])


def _combine_dense(rows, w_tok, x, shared, gpost, g2, tt):
    Bn, S, D = x.shape
    K_, T, DP = rows.shape
    nt = S // tt
    seq = pl.BlockSpec((1, tt, D), lambda b, i: (b, i, 0))
    return pl.pallas_call(
        _combine_dense_kernel,
        grid=(Bn, nt),
        in_specs=[pl.BlockSpec((K_, tt, DP), lambda b, i: (0, b * nt + i, 0)),
                  pl.BlockSpec((tt, K_), lambda b, i: (b * nt + i, 0)), seq, seq,
                  pl.BlockSpec((1, D), lambda b, i: (0, 0)), pl.BlockSpec((1, 1, D), lambda b, i: (b, 0, 0))],
        out_specs=seq,
        out_shape=jax.ShapeDtypeStruct((Bn, S, D), F32),
        compiler_params=_cparams("arbitrary", "arbitrary"),
        name="combine_dense",
    )(rows, w_tok, x, shared, gpost.reshape(1, D), g2)


def kernel(x, c, w_ada, b_ada, norm_pre_mix, norm_post_mix, norm_pre_ffn, norm_post_ffn, w_in, w_out, rel_bias_table, diff_lambda, diff_subln, rwkv_mu, rwkv_w0, rwkv_w2, rwkv_a0, rwkv_a2, rwkv_g2, rwkv_k_k, rwkv_k_a, rwkv_r_k, rwkv_lnx_g, rwkv_lnx_b, gmlp_ln_g, gmlp_ln_b, gmlp_w_s, gmlp_b_s, router_w, router_bias, exp_w1, exp_w3, exp_w2, shared_w1, shared_w3, shared_w2):
    Bn, S, D = x.shape
    depth = w_ada.shape[0]
    tm = min(256, S)
    tq = min(512, S // 2)
    t_rwkv = min(512, S)

    mod = _adaln(c, w_ada, b_ada)
    band_t = _attn_band(rel_bias_table, tq)
    zpad = jnp.zeros((B_DECAY_LORA, B_WIDTH), F32)
    for l in range(depth):
        sh1, sc1, g1, sh2, sc2, g2 = [m.reshape(Bn, 1, D) for m in jnp.split(mod[l], 6, axis=-1)]
        w_in_b = w_in[l].astype(BF16)
        pa, vt, pbc = _inproj(x, norm_pre_mix[l], sc1, sh1, w_in_b[:, :2 * A_WIDTH],
                              jnp.transpose(w_in_b[:, 2 * A_WIDTH:A_COLS]), w_in_b[:, A_COLS:], tm)
        lambda_init = 0.8 - 0.6 * math.exp(-0.3 * l)
        ya = _diff_attention(pa, vt, band_t, diff_lambda[l], diff_subln[l], lambda_init, tq)
        prep = _rwkv_prep(pbc, rwkv_mu[l], rwkv_w0[l], jnp.concatenate([rwkv_w2[l], zpad], axis=0),
                          rwkv_a0[l], jnp.concatenate([zpad, rwkv_a2[l]], axis=0), rwkv_g2[l],
                          rwkv_k_k[l], rwkv_k_a[l], rwkv_r_k[l].reshape(-1), t_rwkv)
        yb = _rwkv_scan(*prep, rwkv_lnx_g[l], rwkv_lnx_b[l], t_rwkv)
        yc = _gmlp(pbc, gmlp_ln_g[l], gmlp_ln_b[l], gmlp_w_s[l], gmlp_b_s[l], tm)

        w_out_b = w_out[l].astype(BF16)
        wr_t = jnp.pad(jnp.transpose(router_w[l]), ((0, V7X_LANES - N_EXPERTS), (0, 0)))
        x, h, scores_t, shared = _mid(
            ya, yb, yc, x, w_out_b[:A_WIDTH], w_out_b[A_WIDTH:A_WIDTH + B_WIDTH], w_out_b[A_WIDTH + B_WIDTH:],
            norm_post_mix[l], g1, norm_pre_ffn[l], sc2, sh2, wr_t,
            shared_w1[l].astype(BF16), shared_w3[l].astype(BF16), shared_w2[l].astype(BF16), tm)

        T = Bn * S
        n_blocks = -(-T * TOP_K // EXPERT_BLOCK) + N_EXPERTS
        eidx, wgt, rank, cnt = _route(scores_t, router_bias[l], tm)
        pad_start, blk_e, n_used, n_valid = _block_layout(cnt[:, 0], n_blocks)
        dest = _dest_rows(pad_start, eidx, rank, tm)
        b = SC_GATHER_ROWS
        dest_sc = jnp.transpose(dest.reshape(T // tm, TOP_K, tm // b, b), (0, 2, 1, 3)).reshape(T // b, TOP_K, b)
        xs = _sc_scatter_rows(h.reshape(T, D // 2), dest_sc, n_blocks * EXPERT_BLOCK)
        ys = _experts(blk_e, n_used, n_valid, xs, exp_w1, exp_w3, exp_w2, l)
        dest_kt = jnp.transpose(dest, (1, 0, 2)).reshape(TOP_K * T)
        rows = _sc_gather_rows(ys, dest_kt).reshape(TOP_K, T, D // 2)
        x = _combine_dense(rows, jnp.transpose(wgt), x, shared, norm_post_ffn[l], g2, tm)
    return x
```

```python
import functools
import math

import jax
import jax.numpy as jnp
from jax import lax
from jax.experimental import pallas as pl
from jax.experimental.pallas import tpu as pltpu
from jax.experimental.pallas import tpu_sc as plsc

F32 = jnp.float32
BF16 = jnp.bfloat16

A_HEADS = 4
A_QK_DIM = 64
A_HEAD_W = 2 * A_QK_DIM
A_WIDTH = A_HEADS * A_HEAD_W
N_BUCKETS = 32
MAX_DISTANCE = 128
B_HEADS = 4
B_HEAD_DIM = 64
B_WIDTH = B_HEADS * B_HEAD_DIM
B_DECAY_LORA = 64
B_AAA_LORA = 64
B_GATE_LORA = 128
B_LNX_EPS = 64e-5
C_GROUPS = 4
C_GROUP_DIM = 64
C_WIDTH = C_GROUPS * C_GROUP_DIM
CHUNK = 128
A_COLS = 3 * A_WIDTH
B_COLS = 3 * B_WIDTH + B_DECAY_LORA + B_AAA_LORA + B_GATE_LORA
C_COLS = 2 * C_WIDTH
N_EXPERTS = 64
TOP_K = 8
N_GROUPS = 8
TOPK_GROUPS = 4
EXPERTS_PER_GROUP = N_EXPERTS // N_GROUPS
ROUTED_SCALE = 2.5
EXPERT_BLOCK = 512
RMS_EPS = 1e-6
LN_EPS = 1e-5
NEG_BIG = -1e30

V7X_LANES = 128
V7X_VMEM_LIMIT_BYTES = 56 * 1024 * 1024
RWKV_CHUNK = 64
RWKV_GROUP = 8

NN = (((1,), (0,)), ((), ()))
NT = (((1,), (1,)), ((), ()))
TN = (((0,), (0,)), ((), ()))


def _cparams(*sem):
    return pltpu.CompilerParams(dimension_semantics=sem, vmem_limit_bytes=V7X_VMEM_LIMIT_BYTES)


def _mm(a, b, dims=NN):
    return lax.dot_general(a.astype(BF16), b.astype(BF16), dims, preferred_element_type=F32)


def _split(a):
    hi = a.astype(BF16)
    lo = (a - hi.astype(F32)).astype(BF16)
    return hi, lo


def _mm3(a, b, dims=NN):
    ah, al = _split(a)
    bh, bl = _split(b)
    d = lambda x, y: lax.dot_general(x, y, dims, preferred_element_type=F32)
    return d(ah, bh) + d(ah, bl) + d(al, bh)


def _mm2(a, b_exact, dims=NN):
    ah, al = _split(a)
    d = lambda x: lax.dot_general(x, b_exact, dims, preferred_element_type=F32)
    return d(ah) + d(al)


def _pack_bf16_pair(x):
    n = x.shape[1] // 2
    bits = lax.bitcast_convert_type(x.astype(BF16).astype(F32), jnp.int32)
    return ((bits[:, :n] >> 16) & 0xFFFF) | bits[:, n:]


def _unpack_bf16_pair(u):
    lo = lax.bitcast_convert_type(u << 16, F32)
    hi = lax.bitcast_convert_type(u & jnp.int32(-65536), F32)
    return lo, hi


def _rms(x, eps=RMS_EPS):
    return x * lax.rsqrt(jnp.mean(x * x, axis=-1, keepdims=True) + eps)


def _sigmoid(x):
    return 1.0 / (1.0 + jnp.exp(-x))


def _silu(x):
    return x * _sigmoid(x)


def _adaln_kernel(c_ref, w_ref, b_ref, o_ref):
    c = c_ref[...]
    o_ref[0] = _mm3(_silu(c), w_ref[0]) + b_ref[0]


def _adaln(c, w_ada, b_ada):
    L, D, N = w_ada.shape
    Bn = c.shape[0]
    tn = min(N, 1536)
    return pl.pallas_call(
        _adaln_kernel,
        grid=(L, N // tn),
        in_specs=[
            pl.BlockSpec((Bn, D), lambda l, j: (0, 0)),
            pl.BlockSpec((1, D, tn), lambda l, j: (l, 0, j)),
            pl.BlockSpec((1, 1, tn), lambda l, j: (l, 0, j)),
        ],
        out_specs=pl.BlockSpec((1, Bn, tn), lambda l, j: (l, 0, j)),
        out_shape=jax.ShapeDtypeStruct((L, Bn, N), F32),
        compiler_params=_cparams("arbitrary", "arbitrary"),
        name="adaln",
    )(c, w_ada, b_ada.reshape(L, 1, N))


def _inproj_kernel(x_ref, g_ref, sc_ref, sh_ref, wa_ref, wvt_ref, wbc_ref, oa_ref, ovt_ref, obc_ref):
    x = x_ref[0]
    h = _rms(x) * g_ref[...] * (1.0 + sc_ref[0]) + sh_ref[0]
    hb = h.astype(BF16)
    oa_ref[0] = jnp.dot(hb, wa_ref[...], preferred_element_type=F32).astype(BF16)
    ovt_ref[0] = lax.dot_general(wvt_ref[...], hb, NT, preferred_element_type=F32).astype(BF16)
    obc_ref[0] = jnp.dot(hb, wbc_ref[...], preferred_element_type=F32)


def _inproj(x, g, sc, sh, wa, wvt, wbc, tm):
    Bn, S, D = x.shape
    na, nv, nbc = wa.shape[1], wvt.shape[0], wbc.shape[1]
    return pl.pallas_call(
        _inproj_kernel,
        grid=(Bn, S // tm),
        in_specs=[
            pl.BlockSpec((1, tm, D), lambda b, i: (b, i, 0)),
            pl.BlockSpec((1, D), lambda b, i: (0, 0)),
            pl.BlockSpec((1, 1, D), lambda b, i: (b, 0, 0)),
            pl.BlockSpec((1, 1, D), lambda b, i: (b, 0, 0)),
            pl.BlockSpec((D, na), lambda b, i: (0, 0)),
            pl.BlockSpec((nv, D), lambda b, i: (0, 0)),
            pl.BlockSpec((D, nbc), lambda b, i: (0, 0)),
        ],
        out_specs=[
            pl.BlockSpec((1, tm, na), lambda b, i: (b, i, 0)),
            pl.BlockSpec((1, nv, tm), lambda b, i: (b, 0, i)),
            pl.BlockSpec((1, tm, nbc), lambda b, i: (b, i, 0)),
        ],
        out_shape=[
            jax.ShapeDtypeStruct((Bn, S, na), BF16),
            jax.ShapeDtypeStruct((Bn, nv, S), BF16),
            jax.ShapeDtypeStruct((Bn, S, nbc), F32),
        ],
        compiler_params=_cparams("arbitrary", "arbitrary"),
        name="inproj",
    )(x, g.reshape(1, D), sc, sh, wa, wvt, wbc)


def _t5_bucket(dist):
    n = jnp.maximum(dist, 0)
    max_exact = N_BUCKETS // 2
    nf = jnp.maximum(n, 1).astype(F32)
    large = max_exact + (jnp.log(nf / max_exact) / math.log(MAX_DISTANCE / max_exact)
                         * (N_BUCKETS - max_exact)).astype(jnp.int32)
    large = jnp.minimum(large, N_BUCKETS - 1)
    return jnp.where(n < max_exact, n, large)


def _attn_band(table, tq):
    far = table[N_BUCKETS - 1].astype(F32)
    H = table.shape[1]
    nb = V7X_LANES
    L = 3 * nb
    m = jnp.arange(L)
    m = jnp.where(m < nb, m, m - L)
    cache = {}

    def block(c):
        if c not in cache:
            if c - (nb - 1) >= MAX_DISTANCE:
                cache[c] = jnp.zeros((H, nb, nb), F32)
            elif c + (nb - 1) < 0:
                cache[c] = jnp.full((H, nb, nb), NEG_BIG, F32)
            else:
                dist = m + c
                vals = jnp.where(dist[None] >= 0,
                                 jnp.transpose(table[_t5_bucket(dist)].astype(F32)) - far[:, None], NEG_BIG)
                cache[c] = jnp.tile(vals, (1, nb))[:, :nb * (L - 1)].reshape(H, nb, L - 1)[:, :, :nb]
        return cache[c]

    bands = []
    for off in (0, tq):
        rows = [jnp.concatenate([block(nb * (a - b) + off) for a in range(tq // nb)], axis=2)
                for b in range(2 * tq // nb)]
        bands.append(jnp.concatenate(rows, axis=1))
    return jnp.stack(bands)


def _attn_kernel(lam_ref, q_ref, k_ref, vt_ref, band_ref, g_ref, o_ref, *, tq, lambda_init):
    i = pl.program_id(2)
    q = q_ref[0] * jnp.asarray(A_QK_DIM ** -0.5, BF16)
    lane = lax.broadcasted_iota(jnp.int32, q.shape, 1)
    zero = jnp.zeros_like(q)
    qq = jnp.concatenate([jnp.where(lane < A_QK_DIM, q, zero),
                          jnp.where(lane >= A_QK_DIM, q, zero)], axis=0)

    kb0 = pl.multiple_of(jnp.maximum(i - 1, 0) * tq, tq)
    kb = k_ref[0, pl.ds(kb0, 2 * tq), :]
    band = band_ref[0, 0]
    s = lax.dot_general(kb, qq, NT, preferred_element_type=F32) + jnp.concatenate([band, band], axis=1)
    m = jnp.max(s, axis=0, keepdims=True)
    p = jnp.exp(s - m)
    l = jnp.sum(p, axis=0, keepdims=True)
    acc = jnp.dot(vt_ref[0, :, pl.ds(kb0, 2 * tq)], p.astype(BF16), preferred_element_type=F32)

    n_far = jnp.maximum(i - 1, 0)

    def logits(j):
        return lax.dot_general(k_ref[0, pl.ds(pl.multiple_of(j * tq, tq), tq), :], qq, NT,
                               preferred_element_type=F32)

    def body(j, carry):
        m, l, acc, s = carry
        s_next = logits(jnp.minimum(j + 1, n_far - 1))
        vtj = vt_ref[0, :, pl.ds(pl.multiple_of(j * tq, tq), tq)]
        m_new = jnp.maximum(m, jnp.max(s, axis=0, keepdims=True))
        alpha = jnp.exp(m - m_new)
        p = jnp.exp(s - m_new)
        l = alpha * l + jnp.sum(p, axis=0, keepdims=True)
        acc = alpha * acc + jnp.dot(vtj, p.astype(BF16), preferred_element_type=F32)
        return m_new, l, acc, s_next

    m, l, acc, _ = lax.fori_loop(0, n_far, body, (m, l, acc, logits(0)))

    lp = lam_ref[...]
    lam = (jnp.exp(jnp.sum(lp[0:1] * lp[1:2], axis=-1, keepdims=True))
           - jnp.exp(jnp.sum(lp[2:3] * lp[3:4], axis=-1, keepdims=True)) + lambda_init)
    o = acc / l
    o = o[:, :tq] - lam * o[:, tq:]
    o = o * lax.rsqrt(jnp.mean(o * o, axis=0, keepdims=True) + RMS_EPS) * g_ref[...] * (1.0 - lambda_init)
    o_ref[0] = jnp.transpose(o)


def _diff_attention(pa, vt, band_t, lam_par, subln_g, lambda_init, tq):
    Bn, S, _ = pa.shape
    W = A_HEAD_W
    kern = functools.partial(_attn_kernel, tq=tq, lambda_init=lambda_init)
    return pl.pallas_call(
        kern,
        grid=(Bn, A_HEADS, S // tq),
        in_specs=[
            pl.BlockSpec((4, A_QK_DIM), lambda b, h, i: (0, 0)),
            pl.BlockSpec((1, tq, W), lambda b, h, i: (b, i, h)),
            pl.BlockSpec((1, S, W), lambda b, h, i: (b, 0, A_HEADS + h)),
            pl.BlockSpec((1, W, S), lambda b, h, i: (b, h, 0)),
            pl.BlockSpec((1, 1, 2 * tq, tq), lambda b, h, i: (jnp.minimum(i, 1), h, 0, 0)),
            pl.BlockSpec((W, 1), lambda b, h, i: (0, 0)),
        ],
        out_specs=pl.BlockSpec((1, tq, W), lambda b, h, i: (b, i, h)),
        out_shape=jax.ShapeDtypeStruct((Bn, S, A_WIDTH), F32),
        compiler_params=_cparams("arbitrary", "arbitrary", "arbitrary"),
        name="diff_attn",
    )(lam_par, pa, pa, vt, band_t, subln_g.reshape(W, 1))


def _head_ones(n):
    r = lax.broadcasted_iota(jnp.int32, (n, n), 0) // B_HEAD_DIM
    c = lax.broadcasted_iota(jnp.int32, (n, n), 1) // B_HEAD_DIM
    return (r == c).astype(BF16)


def _rwkv_prep_kernel(pb_ref, prev_ref, mu_ref, w0_ref, w2_ref, a0_ref, a2_ref, g2_ref,
                      kk_ref, ka_ref, rk_ref,
                      rt_ref, at_ref, kt_ref, bt_ref, v_ref, wc_ref, bonus_ref, g_ref, *, tm):
    i = pl.program_id(1)
    C = RWKV_CHUNK
    x = pb_ref[0]
    row = lax.broadcasted_iota(jnp.int32, x.shape, 0)
    last = prev_ref[0, 7:8, :] * (i > 0).astype(F32)
    prev = jnp.where(row == 0, last, pltpu.roll(x, 1, 0))
    p = x + (prev - x) * mu_ref[...]
    o1, o2, o3 = B_WIDTH, 2 * B_WIDTH, 3 * B_WIDTH
    r, k, v = p[:, :o1], p[:, o1:o2], p[:, o2:o3]
    lora = p[:, o3:o3 + B_DECAY_LORA + B_AAA_LORA]
    gd = p[:, o3 + B_DECAY_LORA + B_AAA_LORA:]

    z = -(w0_ref[...] + _mm3(jnp.tanh(lora), w2_ref[...]))
    softplus = jnp.maximum(z, 0.0) + jnp.log(1.0 + jnp.exp(-jnp.abs(z)))
    logw = -jnp.exp(-softplus - 0.5)
    a = _sigmoid(a0_ref[...] + _mm3(lora, a2_ref[...]))
    g_ref[0] = _mm3(_sigmoid(gd), g2_ref[...])

    ones = _head_ones(B_WIDTH)
    kk = k * kk_ref[...]
    kk = kk * lax.rsqrt(jnp.maximum(_mm2(kk * kk, ones), 1e-24))
    k2 = k * (1.0 + (a - 1.0) * ka_ref[...])
    bonus_ref[0] = _mm2(r * k2 * rk_ref[...], ones) * v

    t_in = lax.broadcasted_iota(jnp.int32, (tm, B_WIDTH), 0) % C
    cum = logw
    sh = 1
    while sh < C:
        cum = cum + jnp.where(t_in >= sh, pltpu.roll(cum, sh, 0), 0.0)
        sh *= 2
    n = tm // C
    wc_ref[0] = jnp.exp(jnp.sum(logw.reshape(n, C, B_WIDTH), axis=1))
    e_pos = jnp.exp(cum)
    e_neg = jnp.exp(-cum)
    rt_ref[0] = r * e_pos
    at_ref[0] = -kk * jnp.exp(cum - logw)
    kt_ref[0] = k2 * e_neg
    bt_ref[0] = kk * a * e_neg
    v_ref[0] = v


def _rwkv_prep(pbc, mu, w0, w2p, a0, a2p, g2, k_k, k_a, r_k, tm):
    Bn, S, _ = pbc.shape
    W = B_WIDTH
    nl = B_DECAY_LORA + B_AAA_LORA
    row = lambda a: a.reshape(1, -1)
    full = lambda shp: pl.BlockSpec(shp, lambda b, i: (0,) * len(shp))
    seq = pl.BlockSpec((1, tm, W), lambda b, i: (b, i, 0))
    seq_shape = jax.ShapeDtypeStruct((Bn, S, W), F32)
    n = tm // RWKV_CHUNK
    return pl.pallas_call(
        functools.partial(_rwkv_prep_kernel, tm=tm),
        grid=(Bn, S // tm),
        in_specs=[
            pl.BlockSpec((1, tm, B_COLS), lambda b, i: (b, i, 0)),
            pl.BlockSpec((1, 8, B_COLS), lambda b, i: (b, jnp.maximum(i * (tm // 8) - 1, 0), 0)),
            full((1, B_COLS)), full((1, W)), full((nl, W)), full((1, W)), full((nl, W)),
            full((B_GATE_LORA, W)), full((1, W)), full((1, W)), full((1, W)),
        ],
        out_specs=[seq, seq, seq, seq, seq,
                   pl.BlockSpec((1, n, W), lambda b, i: (b, i, 0)), seq, seq],
        out_shape=[seq_shape] * 5 + [jax.ShapeDtypeStruct((Bn, S // RWKV_CHUNK, W), F32)] + [seq_shape] * 2,
        compiler_params=_cparams("arbitrary", "arbitrary"),
        name="rwkv_prep",
    )(pbc, pbc, row(mu), row(w0), w2p, row(a0), a2p, g2, row(k_k), row(k_a), row(r_k))


def _rwkv_scan_kernel(rt_ref, at_ref, kt_ref, bt_ref, v_ref, wc_ref, bonus_ref, g_ref,
                      lng_ref, lnb_ref, o_ref, state, *, tt):
    C = RWKV_CHUNK
    W = B_WIDTH

    @pl.when(pl.program_id(1) == 0)
    def _():
        state[...] = jnp.zeros_like(state)

    lane_head = lax.broadcasted_iota(jnp.int32, (C, W), 1) // B_HEAD_DIM
    tt_i = lax.broadcasted_iota(jnp.int32, (C, W), 0)
    ss_i = lax.broadcasted_iota(jnp.int32, (C, W), 1) % C
    strict = tt_i > ss_i
    incl = tt_i >= ss_i
    eye = (tt_i == ss_i).astype(F32)
    ones = _head_ones(W)
    bd_mask = ones.astype(F32)

    head_mask = [(lane_head == h).astype(BF16) for h in range(B_HEADS)]

    def bd_split(x):
        xb = x.astype(BF16)
        return jnp.concatenate([xb * mk for mk in head_mask], axis=0)

    def mm_bd(a, b_bd, dims=NN):
        return lax.dot_general(a.astype(BF16), b_bd, dims, preferred_element_type=F32)

    def state_free(gi):
        G = range(RWKV_GROUP)
        sls = [pl.ds(pl.multiple_of((gi * RWKV_GROUP + j) * C, C), C) for j in G]
        rt = [rt_ref[0, sl, :] for sl in sls]
        at = [at_ref[0, sl, :] for sl in sls]
        kt = [kt_ref[0, sl, :] for sl in sls]
        bt = [bt_ref[0, sl, :] for sl in sls]
        v = [v_ref[0, sl, :] for sl in sls]
        wc = [wc_ref[0, pl.ds(gi * RWKV_GROUP + j, 1), :] for j in G]
        ar = [jnp.concatenate([at[j], rt[j]], axis=0) for j in G]
        bdb = [bd_split(bt[j]) for j in G]
        bdk = [bd_split(kt[j]) for j in G]
        a_b = [mm_bd(ar[j], bdb[j], NT) for j in G]
        a_k = [mm_bd(ar[j], bdk[j], NT) for j in G]
        lo = [jnp.where(strict, a_b[j][:C], 0.0) for j in G]
        a_ak = [jnp.where(strict, a_k[j][:C], 0.0) for j in G]
        a_rb = [jnp.where(incl, a_b[j][C:], 0.0) for j in G]
        a_rk = [jnp.where(incl, a_k[j][C:], 0.0) for j in G]
        pw = lo
        tinv = [eye + lo[j] for j in G]
        bdp = [bd_split(pw[j]) for j in G]
        span = 2
        while span < C:
            pw = [mm_bd(pw[j], bdp[j]) for j in G]
            bdp = [bd_split(pw[j]) for j in G]
            tinv = [tinv[j] + mm_bd(tinv[j], bdp[j]) for j in G]
            span *= 2
        bdv = [bd_split(v[j]) for j in G]
        bda = [bd_split(at[j]) for j in G]
        abar = [mm_bd(tinv[j], bda[j]) for j in G]
        akv = [bd_split(mm_bd(a_ak[j], bdv[j])) for j in G]
        u0 = [mm_bd(tinv[j], akv[j]) for j in G]
        y0 = [mm_bd(a_rk[j], bdv[j]) for j in G]
        kv = [_mm(v[j], kt[j] * wc[j], TN) * bd_mask for j in G]
        return [(jnp.concatenate([abar[j], rt[j]], axis=0), u0[j], y0[j], a_rb[j], bt[j] * wc[j], kv[j], wc[j])
                for j in G]

    def group(gi, carry):
        pre = state_free(gi)
        s = state[...]
        ys = []
        for abar_rt, u0, y0, a_rb, btw, kv, wc in pre:
            a_s = _mm(abar_rt, s, NT)
            u = a_s[:C] + u0
            ys.append(a_s[C:] + y0 + mm_bd(a_rb, bd_split(u)))
            s = s * wc + _mm(u, btw, TN) * bd_mask + kv
        state[...] = s
        y = jnp.concatenate(ys, axis=0)
        sl = pl.ds(pl.multiple_of(gi * (RWKV_GROUP * C), RWKV_GROUP * C), RWKV_GROUP * C)
        mean = _mm2(y, ones) * (1.0 / B_HEAD_DIM)
        d = y - mean
        var = _mm2(d * d, ones) * (1.0 / B_HEAD_DIM)
        yn = d * lax.rsqrt(var + B_LNX_EPS) * lng_ref[...] + lnb_ref[...]
        o_ref[0, sl, :] = (yn + bonus_ref[0, sl, :]) * g_ref[0, sl, :]
        return carry

    lax.fori_loop(0, tt // (RWKV_GROUP * C), group, 0)


def _rwkv_scan(rt, at, kt, bt, v, wc, bonus, g, lnx_g, lnx_b, tt):
    Bn, S, W = rt.shape
    n = tt // RWKV_CHUNK
    seq = pl.BlockSpec((1, tt, W), lambda b, i: (b, i, 0))
    vec = pl.BlockSpec((1, W), lambda b, i: (0, 0))
    return pl.pallas_call(
        functools.partial(_rwkv_scan_kernel, tt=tt),
        grid=(Bn, S // tt),
        in_specs=[seq, seq, seq, seq, seq, pl.BlockSpec((1, n, W), lambda b, i: (b, i, 0)), seq, seq, vec, vec],
        out_specs=seq,
        out_shape=jax.ShapeDtypeStruct((Bn, S, W), F32),
        scratch_shapes=[pltpu.VMEM((B_HEADS * B_HEAD_DIM, W), F32)],
        compiler_params=_cparams("arbitrary", "arbitrary"),
        name="rwkv_scan",
    )(rt, at, kt, bt, v, wc, bonus, g, lnx_g.reshape(1, W), lnx_b.reshape(1, W))


def _gmlp_kernel(pc_ref, lng_ref, lnb_ref, ws_ref, bs_ref, o_ref, *, tm):
    x = pc_ref[0]
    z = x * (0.5 * (1.0 + jnp.tanh(math.sqrt(2.0 / math.pi) * (x + 0.044715 * (x * x * x)))))
    u, v = z[:, :C_WIDTH], z[:, C_WIDTH:]
    mu = jnp.mean(v, axis=-1, keepdims=True)
    d = v - mu
    var = jnp.mean(d * d, axis=-1, keepdims=True)
    vn = d * lax.rsqrt(var + LN_EPS) * lng_ref[...] + lnb_ref[...]
    group = lax.broadcasted_iota(jnp.int32, (CHUNK, C_WIDTH), 1) // C_GROUP_DIM
    tril = (lax.broadcasted_iota(jnp.int32, (CHUNK, CHUNK), 0)
            >= lax.broadcasted_iota(jnp.int32, (CHUNK, CHUNK), 1))
    ws = [jnp.where(tril, ws_ref[gi], 0.0).astype(BF16) for gi in range(C_GROUPS)]
    for c in range(tm // CHUNK):
        sl = slice(c * CHUNK, (c + 1) * CHUNK)
        vc = vn[sl].astype(BF16)
        sv = bs_ref[...]
        for gi in range(C_GROUPS):
            t = jnp.dot(ws[gi], vc, preferred_element_type=F32)
            sv = sv + jnp.where(group == gi, t, 0.0)
        o_ref[0, sl, :] = u[sl] * sv


def _gmlp(pbc, ln_g, ln_b, w_s, b_s, tm):
    Bn, S, _ = pbc.shape
    bs_wide = jnp.repeat(jnp.transpose(b_s), C_GROUP_DIM, axis=1)
    return pl.pallas_call(
        functools.partial(_gmlp_kernel, tm=tm),
        grid=(Bn, S // tm),
        in_specs=[
            pl.BlockSpec((1, tm, C_COLS), lambda b, i: (b, i, B_COLS // C_COLS)),
            pl.BlockSpec((1, C_WIDTH), lambda b, i: (0, 0)),
            pl.BlockSpec((1, C_WIDTH), lambda b, i: (0, 0)),
            pl.BlockSpec((C_GROUPS, CHUNK, CHUNK), lambda b, i: (0, 0, 0)),
            pl.BlockSpec((CHUNK, C_WIDTH), lambda b, i: (0, 0)),
        ],
        out_specs=pl.BlockSpec((1, tm, C_WIDTH), lambda b, i: (b, i, 0)),
        out_shape=jax.ShapeDtypeStruct((Bn, S, C_WIDTH), F32),
        compiler_params=_cparams("arbitrary", "arbitrary"),
        name="gmlp",
    )(pbc, ln_g.reshape(1, -1), ln_b.reshape(1, -1), w_s, bs_wide)


def _mid_kernel(ya_ref, yb_ref, yc_ref, x_ref, woa_ref, wob_ref, woc_ref, gpost_ref, g1_ref,
                gpre_ref, sc_ref, sh_ref, wr_ref, xo_ref, h_ref, score_ref):
    y = (_mm(ya_ref[0], woa_ref[...]) + _mm(yb_ref[0], wob_ref[...]) + _mm(yc_ref[0], woc_ref[...]))
    xn = x_ref[0] + g1_ref[0] * (_rms(y) * gpost_ref[...])
    xo_ref[0] = xn
    h = _rms(xn) * gpre_ref[...] * (1.0 + sc_ref[0]) + sh_ref[0]
    h_ref[0] = _pack_bf16_pair(h)
    score_ref[0] = _sigmoid(_mm3(wr_ref[...], h, NT))


def _mid(ya, yb, yc, x, woa, wob, woc, gpost, g1, gpre, sc, sh, wr, tm):
    Bn, S, D = x.shape
    NR = wr.shape[0]
    seq = lambda w: pl.BlockSpec((1, tm, w), lambda b, i: (b, i, 0))
    full = lambda shp: pl.BlockSpec(shp, lambda b, i: (0,) * len(shp))
    per_b = pl.BlockSpec((1, 1, D), lambda b, i: (b, 0, 0))
    return pl.pallas_call(
        _mid_kernel,
        grid=(Bn, S // tm),
        in_specs=[seq(A_WIDTH), seq(B_WIDTH), seq(C_WIDTH), seq(D),
                  full((A_WIDTH, D)), full((B_WIDTH, D)), full((C_WIDTH, D)),
                  full((1, D)), per_b, full((1, D)), per_b, per_b,
                  full((NR, D))],
        out_specs=[seq(D), seq(D // 2), pl.BlockSpec((1, NR, tm), lambda b, i: (b, 0, i))],
        out_shape=[jax.ShapeDtypeStruct((Bn, S, D), F32), jax.ShapeDtypeStruct((Bn, S, D // 2), jnp.int32),
                   jax.ShapeDtypeStruct((Bn, NR, S), F32)],
        compiler_params=_cparams("arbitrary", "arbitrary"),
        name="mid",
    )(ya, yb, yc, x, woa, wob, woc, gpost.reshape(1, D), g1, gpre.reshape(1, D), sc, sh, wr)


def _shared_expert_kernel(h_ref, ws1_ref, ws3_ref, ws2_ref, o_ref):
    lo, hi = _unpack_bf16_pair(h_ref[0])
    hb = jnp.concatenate([lo, hi], axis=1).astype(BF16)
    t = _silu(jnp.dot(hb, ws1_ref[...], preferred_element_type=F32)) * jnp.dot(
        hb, ws3_ref[...], preferred_element_type=F32)
    o_ref[0] = jnp.dot(t.astype(BF16), ws2_ref[...], preferred_element_type=F32)


def _shared_expert(hp, ws1, ws3, ws2, tm):
    Bn, S, DP = hp.shape
    D, F = ws1.shape
    full = lambda shp: pl.BlockSpec(shp, lambda b, i: (0,) * len(shp))
    return pl.pallas_call(
        _shared_expert_kernel,
        grid=(Bn, S // tm),
        in_specs=[pl.BlockSpec((1, tm, DP), lambda b, i: (b, i, 0)), full((D, F)), full((D, F)), full((F, D))],
        out_specs=pl.BlockSpec((1, tm, D), lambda b, i: (b, i, 0)),
        out_shape=jax.ShapeDtypeStruct((Bn, S, D), F32),
        compiler_params=_cparams("arbitrary", "arbitrary"),
        name="shared_expert",
    )(hp, ws1, ws3, ws2)


def _first_argmax(vals, iota, n):
    m = jnp.max(vals, axis=0, keepdims=True)
    idx = jnp.min(jnp.where(vals == m, iota, n), axis=0, keepdims=True)
    return m, idx


def _route_kernel(sc_ref, bias_ref, e_ref, w_ref, r_ref, cnt_ref, carry, *, tm):
    @pl.when((pl.program_id(0) == 0) & (pl.program_id(1) == 0))
    def _():
        carry[...] = jnp.zeros_like(carry)

    G = EXPERTS_PER_GROUP
    s = sc_ref[0]
    biased = s + bias_ref[...]
    neg_inf = jnp.float32(-jnp.inf)
    io8 = lax.broadcasted_iota(jnp.int32, (G, tm), 0)
    gs_rows = []
    for g in range(N_GROUPS):
        blk = biased[g * G:(g + 1) * G]
        m1, i1 = _first_argmax(blk, io8, G)
        m2 = jnp.max(jnp.where(io8 == i1, neg_inf, blk), axis=0, keepdims=True)
        gs_rows.append(m1 + m2)
    gs = jnp.concatenate(gs_rows, axis=0)
    gio = lax.broadcasted_iota(jnp.int32, (N_GROUPS, tm), 0)
    gsel = jnp.zeros((N_GROUPS, tm), jnp.bool_)
    for _ in range(TOPK_GROUPS):
        _, gi = _first_argmax(gs, gio, N_GROUPS)
        pick = gio == gi
        gsel = gsel | pick
        gs = jnp.where(pick, neg_inf, gs)
    masked = jnp.concatenate(
        [jnp.where(gsel[g:g + 1], biased[g * G:(g + 1) * G], neg_inf) for g in range(N_GROUPS)], axis=0)

    eio = lax.broadcasted_iota(jnp.int32, (N_EXPERTS, tm), 0)
    picks, e_rows, s_rows = [], [], []
    for _ in range(TOP_K):
        _, ei = _first_argmax(masked, eio, N_EXPERTS)
        pick = eio == ei
        picks.append(pick)
        e_rows.append(ei)
        s_rows.append(jnp.sum(jnp.where(pick, s, 0.0), axis=0, keepdims=True))
        masked = jnp.where(pick, neg_inf, masked)
    top_s = jnp.concatenate(s_rows, axis=0)
    w_ref[...] = top_s / (jnp.sum(top_s, axis=0, keepdims=True) + 1e-20) * ROUTED_SCALE
    e_ref[...] = jnp.concatenate(e_rows, axis=0)

    sel = jnp.zeros((N_EXPERTS, tm), F32)
    for pick in picks:
        sel = sel + pick.astype(F32)
    before = (lax.broadcasted_iota(jnp.int32, (tm, tm), 0) < lax.broadcasted_iota(jnp.int32, (tm, tm), 1))
    pos = carry[...] + jnp.dot(sel.astype(BF16), before.astype(BF16), preferred_element_type=F32)
    r_ref[...] = jnp.concatenate(
        [jnp.sum(jnp.where(pick, pos, 0.0), axis=0, keepdims=True) for pick in picks], axis=0).astype(jnp.int32)
    total = carry[...] + jnp.sum(sel, axis=1, keepdims=True)
    carry[...] = total
    cnt_ref[...] = jnp.broadcast_to(total, cnt_ref.shape).astype(jnp.int32)


def _route(scores_t, e_bias, tm):
    Bn, _, S = scores_t.shape
    T = Bn * S
    nt = S // tm
    tok = pl.BlockSpec((TOP_K, tm), lambda b, i: (0, b * nt + i))
    return pl.pallas_call(
        functools.partial(_route_kernel, tm=tm),
        grid=(Bn, nt),
        in_specs=[pl.BlockSpec((1, N_EXPERTS, tm), lambda b, i: (b, 0, i)),
                  pl.BlockSpec((N_EXPERTS, 1), lambda b, i: (0, 0))],
        out_specs=[tok, tok, tok, pl.BlockSpec((N_EXPERTS, V7X_LANES), lambda b, i: (0, 0))],
        out_shape=[jax.ShapeDtypeStruct((TOP_K, T), jnp.int32), jax.ShapeDtypeStruct((TOP_K, T), F32),
                   jax.ShapeDtypeStruct((TOP_K, T), jnp.int32),
                   jax.ShapeDtypeStruct((N_EXPERTS, V7X_LANES), jnp.int32)],
        scratch_shapes=[pltpu.VMEM((N_EXPERTS, 1), F32)],
        compiler_params=_cparams("arbitrary", "arbitrary"),
        name="route",
    )(scores_t, e_bias.reshape(N_EXPERTS, 1))


def _dest_kernel(start_ref, e_ref, r_ref, o_ref):
    e = e_ref[...]
    acc = r_ref[...]
    for ex in range(N_EXPERTS):
        acc = acc + jnp.where(e == ex, start_ref[ex], 0)
    o_ref[0] = acc


def _dest_rows(pad_start, eidx, rank, tt):
    K_, T = eidx.shape
    grid_spec = pltpu.PrefetchScalarGridSpec(
        num_scalar_prefetch=1,
        grid=(T // tt,),
        in_specs=[pl.BlockSpec((K_, tt), lambda i, st: (0, i)), pl.BlockSpec((K_, tt), lambda i, st: (0, i))],
        out_specs=pl.BlockSpec((1, K_, tt), lambda i, st: (i, 0, 0)),
    )
    return pl.pallas_call(
        _dest_kernel,
        grid_spec=grid_spec,
        out_shape=jax.ShapeDtypeStruct((T // tt, K_, tt), jnp.int32),
        compiler_params=_cparams("arbitrary"),
        name="dest_rows",
    )(pad_start, eidx, rank)


def _expert_kernel(blk_e_ref, n_used_ref, n_valid_ref, x_ref, w1_ref, w3_ref, w2_ref, o_ref, w1b, w3b, w2b):
    i = pl.program_id(0)

    @pl.when((i == 0) | (blk_e_ref[i] != blk_e_ref[jnp.maximum(i - 1, 0)]))
    def _():
        w1b[...] = w1_ref[0].astype(BF16)
        w3b[...] = w3_ref[0].astype(BF16)
        w2b[...] = w2_ref[0].astype(BF16)

    @pl.when(i < n_used_ref[0])
    def _():
        row = lax.broadcasted_iota(jnp.int32, x_ref.shape, 0)
        x_lo, x_hi = _unpack_bf16_pair(jnp.where(row < n_valid_ref[i], x_ref[...], 0))
        x_lo, x_hi = x_lo.astype(BF16), x_hi.astype(BF16)
        half = x_lo.shape[1]

        def up(wb):
            return (jnp.dot(x_lo, wb[:half, :], preferred_element_type=F32)
                    + jnp.dot(x_hi, wb[half:, :], preferred_element_type=F32))

        t = _silu(up(w1b)) * up(w3b)
        o_ref[...] = _pack_bf16_pair(jnp.dot(t.astype(BF16), w2b[...], preferred_element_type=F32))


def _experts(blk_e, n_used, n_valid, xs, w1, w3, w2, layer):
    P, DP = xs.shape
    EB = EXPERT_BLOCK
    n_blocks = blk_e.shape[0]
    D, F = w1.shape[2], w1.shape[3]
    rows = pl.BlockSpec((EB, DP), lambda i, be, nu, nv: (jnp.minimum(i, nu[0] - 1), 0))
    grid_spec = pltpu.PrefetchScalarGridSpec(
        num_scalar_prefetch=3,
        grid=(n_blocks,),
        in_specs=[
            rows,
            pl.BlockSpec((None, 1, D, F), lambda i, be, nu, nv: (layer, be[i], 0, 0)),
            pl.BlockSpec((None, 1, D, F), lambda i, be, nu, nv: (layer, be[i], 0, 0)),
            pl.BlockSpec((None, 1, F, D), lambda i, be, nu, nv: (layer, be[i], 0, 0)),
        ],
        out_specs=rows,
        scratch_shapes=[pltpu.VMEM((D, F), BF16), pltpu.VMEM((D, F), BF16), pltpu.VMEM((F, D), BF16)],
    )
    return pl.pallas_call(
        _expert_kernel,
        grid_spec=grid_spec,
        out_shape=jax.ShapeDtypeStruct((P, DP), jnp.int32),
        compiler_params=_cparams("arbitrary"),
        name="experts",
    )(blk_e, n_used, n_valid, xs, w1, w3, w2)


def _block_layout(counts, n_blocks):
    EB = EXPERT_BLOCK
    padded = (counts + EB - 1) // EB * EB
    ex = jnp.arange(N_EXPERTS)
    pad_end = jnp.sum(jnp.where(ex[:, None] <= ex[None, :], padded[:, None], 0), axis=0)
    pad_start = pad_end - padded
    blk_row = (jnp.arange(n_blocks) * EB)[:, None]
    blk_e = jnp.minimum(jnp.sum((pad_end[None, :] <= blk_row).astype(jnp.int32), axis=1), N_EXPERTS - 1)
    n_used = (jnp.sum(padded) // EB).astype(jnp.int32).reshape(1)
    mine = (pad_start[None, :] <= blk_row) & (blk_row < pad_end[None, :])
    n_valid = jnp.sum(jnp.where(mine, jnp.clip(counts[None, :] - (blk_row - pad_start[None, :]), 0, EB), 0), axis=1)
    return pad_start.astype(jnp.int32), blk_e.astype(jnp.int32), n_used, n_valid.astype(jnp.int32)


SC_GATHER_ROWS = 64


def _sc_gather_rows(table, idx):
    info = plsc.get_sparse_core_info()
    nc, ns = info.num_cores, info.num_subcores
    M = idx.shape[0]
    W = table.shape[1]
    b = SC_GATHER_ROWS
    per_worker = M // (nc * ns)
    steps = per_worker // b
    assert per_worker * nc * ns == M and steps * b == per_worker and steps % 2 == 0
    mesh = plsc.VectorSubcoreMesh(core_axis_name="c", subcore_axis_name="s")

    @functools.partial(
        pl.kernel, mesh=mesh,
        out_type=jax.ShapeDtypeStruct((M, W), table.dtype),
        scratch_types=[pltpu.VMEM((2, b), jnp.int32), pltpu.VMEM((2, b, W), table.dtype),
                       pltpu.SemaphoreType.DMA, pltpu.SemaphoreType.DMA],
        name="sc_gather_rows",
    )
    def gather(table_hbm, idx_hbm, out_hbm, idx_v, rows_v, sem0, sem1):
        wid = lax.axis_index("s") * nc + lax.axis_index("c")
        sems = (sem0, sem1)

        def base(s):
            return pl.multiple_of(wid * per_worker + s * b, b)

        def gather_copy(slot):
            return pltpu.make_async_copy(table_hbm.at[idx_v.at[slot]], rows_v.at[slot], sems[slot])

        def start(s, slot):
            pltpu.sync_copy(idx_hbm.at[pl.ds(base(s), b)], idx_v.at[slot])
            gather_copy(slot).start()

        def finish(s, slot):
            gather_copy(slot).wait()
            pltpu.sync_copy(rows_v.at[slot], out_hbm.at[pl.ds(base(s), b)])

        start(0, 0)

        @pl.loop(0, steps, step=2)
        def _(s):
            start(s + 1, 1)
            finish(s, 0)

            @pl.when(s + 2 < steps)
            def _():
                start(s + 2, 0)

            finish(s + 1, 1)

    return gather(table, idx)


def _sc_scatter_rows(rows, idx, n_out):
    info = plsc.get_sparse_core_info()
    nc, ns = info.num_cores, info.num_subcores
    T, W = rows.shape
    G, K_, b = idx.shape
    steps = G // (nc * ns)
    assert steps * nc * ns == G and G * b == T
    mesh = plsc.VectorSubcoreMesh(core_axis_name="c", subcore_axis_name="s")

    @functools.partial(
        pl.kernel, mesh=mesh,
        out_type=jax.ShapeDtypeStruct((n_out, W), rows.dtype),
        scratch_types=[pltpu.VMEM((K_, b), jnp.int32), pltpu.VMEM((b, W), rows.dtype), pltpu.SemaphoreType.DMA],
        name="sc_scatter_rows",
    )
    def scatter(rows_hbm, idx_hbm, out_hbm, idx_v, rows_v, sem):
        wid = lax.axis_index("s") * nc + lax.axis_index("c")

        @pl.loop(0, steps)
        def _(s):
            g = wid * steps + s
            pltpu.sync_copy(idx_hbm.at[g], idx_v)
            pltpu.sync_copy(rows_hbm.at[pl.ds(pl.multiple_of(g * b, b), b)], rows_v)
            copies = [pltpu.async_copy(rows_v, out_hbm.at[idx_v.at[k]], sem) for k in range(K_)]
            for cp in copies:
                cp.wait()

    return scatter(rows, idx)


def _combine_dense_kernel(rows_ref, w_ref, x_ref, shared_ref, gpost_ref, g2_ref, o_ref):
    w = w_ref[...]
    tt, half = rows_ref.shape[1], rows_ref.shape[2]
    y_lo = jnp.zeros((tt, half), F32)
    y_hi = jnp.zeros((tt, half), F32)
    for k in range(TOP_K):
        lo, hi = _unpack_bf16_pair(rows_ref[k])
        y_lo = y_lo + w[:, k:k + 1] * lo
        y_hi = y_hi + w[:, k:k + 1] * hi
    y = shared_ref[0] + jnp.concatenate([y_lo, y_hi], axis=1)
    o_ref[0] = x_ref[0] + g2_ref[0] * (_rms(y) * gpost_ref[...])


def _combine_dense(rows, w_tok, x, shared, gpost, g2, tt):
    Bn, S, D = x.shape
    K_, T, DP = rows.shape
    nt = S // tt
    seq = pl.BlockSpec((1, tt, D), lambda b, i: (b, i, 0))
    return pl.pallas_call(
        _combine_dense_kernel,
        grid=(Bn, nt),
        in_specs=[pl.BlockSpec((K_, tt, DP), lambda b, i: (0, b * nt + i, 0)),
                  pl.BlockSpec((tt, K_), lambda b, i: (b * nt + i, 0)), seq, seq,
                  pl.BlockSpec((1, D), lambda b, i: (0, 0)), pl.BlockSpec((1, 1, D), lambda b, i: (b, 0, 0))],
        out_specs=seq,
        out_shape=jax.ShapeDtypeStruct((Bn, S, D), F32),
        compiler_params=_cparams("arbitrary", "arbitrary"),
        name="combine_dense",
    )(rows, w_tok, x, shared, gpost.reshape(1, D), g2)


def kernel(x, c, w_ada, b_ada, norm_pre_mix, norm_post_mix, norm_pre_ffn, norm_post_ffn, w_in, w_out, rel_bias_table, diff_lambda, diff_subln, rwkv_mu, rwkv_w0, rwkv_w2, rwkv_a0, rwkv_a2, rwkv_g2, rwkv_k_k, rwkv_k_a, rwkv_r_k, rwkv_lnx_g, rwkv_lnx_b, gmlp_ln_g, gmlp_ln_b, gmlp_w_s, gmlp_b_s, router_w, router_bias, exp_w1, exp_w3, exp_w2, shared_w1, shared_w3, shared_w2):
    Bn, S, D = x.shape
    depth = w_ada.shape[0]
    tm = min(256, S)
    tq = min(512, S // 2)
    t_rwkv = min(512, S)

    mod = _adaln(c, w_ada, b_ada)
    band_t = _attn_band(rel_bias_table, tq)
    zpad = jnp.zeros((B_DECAY_LORA, B_WIDTH), F32)
    for l in range(depth):
        sh1, sc1, g1, sh2, sc2, g2 = [m.reshape(Bn, 1, D) for m in jnp.split(mod[l], 6, axis=-1)]
        w_in_b = w_in[l].astype(BF16)
        pa, vt, pbc = _inproj(x, norm_pre_mix[l], sc1, sh1, w_in_b[:, :2 * A_WIDTH],
                              jnp.transpose(w_in_b[:, 2 * A_WIDTH:A_COLS]), w_in_b[:, A_COLS:], tm)
        lambda_init = 0.8 - 0.6 * math.exp(-0.3 * l)
        ya = _diff_attention(pa, vt, band_t, diff_lambda[l], diff_subln[l], lambda_init, tq)
        prep = _rwkv_prep(pbc, rwkv_mu[l], rwkv_w0[l], jnp.concatenate([rwkv_w2[l], zpad], axis=0),
                          rwkv_a0[l], jnp.concatenate([zpad, rwkv_a2[l]], axis=0), rwkv_g2[l],
                          rwkv_k_k[l], rwkv_k_a[l], rwkv_r_k[l].reshape(-1), t_rwkv)
        yb = _rwkv_scan(*prep, rwkv_lnx_g[l], rwkv_lnx_b[l], t_rwkv)
        yc = _gmlp(pbc, gmlp_ln_g[l], gmlp_ln_b[l], gmlp_w_s[l], gmlp_b_s[l], tm)

        w_out_b = w_out[l].astype(BF16)
        wr_t = jnp.pad(jnp.transpose(router_w[l]), ((0, V7X_LANES - N_EXPERTS), (0, 0)))
        x, h, scores_t = _mid(
            ya, yb, yc, x, w_out_b[:A_WIDTH], w_out_b[A_WIDTH:A_WIDTH + B_WIDTH], w_out_b[A_WIDTH + B_WIDTH:],
            norm_post_mix[l], g1, norm_pre_ffn[l], sc2, sh2, wr_t, tm)

        T = Bn * S
        n_blocks = -(-T * TOP_K // EXPERT_BLOCK) + N_EXPERTS
        eidx, wgt, rank, cnt = _route(scores_t, router_bias[l], tm)
        pad_start, blk_e, n_used, n_valid = _block_layout(cnt[:, 0], n_blocks)
        dest = _dest_rows(pad_start, eidx, rank, tm)
        b = SC_GATHER_ROWS
        dest_sc = jnp.transpose(dest.reshape(T // tm, TOP_K, tm // b, b), (0, 2, 1, 3)).reshape(T // b, TOP_K, b)
        xs = _sc_scatter_rows(h.reshape(T, D // 2), dest_sc, n_blocks * EXPERT_BLOCK)
        shared = _shared_expert(h, shared_w1[l].astype(BF16), shared_w3[l].astype(BF16),
                                shared_w2[l].astype(BF16), tm)
        ys = _experts(blk_e, n_used, n_valid, xs, exp_w1, exp_w3, exp_w2, l)
        dest_kt = jnp.transpose(dest, (1, 0, 2)).reshape(TOP_K * T)
        rows = _sc_gather_rows(ys, dest_kt).reshape(TOP_K, T, D // 2)
        x = _combine_dense(rows, jnp.transpose(wgt), x, shared, norm_post_ffn[l], g2, tm)
    return x
```

```python
import functools
import math

import jax
import jax.numpy as jnp
from jax import lax
from jax.experimental import pallas as pl
from jax.experimental.pallas import tpu as pltpu
from jax.experimental.pallas import tpu_sc as plsc

F32 = jnp.float32
BF16 = jnp.bfloat16

A_HEADS = 4
A_QK_DIM = 64
A_HEAD_W = 2 * A_QK_DIM
A_WIDTH = A_HEADS * A_HEAD_W
N_BUCKETS = 32
MAX_DISTANCE = 128
B_HEADS = 4
B_HEAD_DIM = 64
B_WIDTH = B_HEADS * B_HEAD_DIM
B_DECAY_LORA = 64
B_AAA_LORA = 64
B_GATE_LORA = 128
B_LNX_EPS = 64e-5
C_GROUPS = 4
C_GROUP_DIM = 64
C_WIDTH = C_GROUPS * C_GROUP_DIM
CHUNK = 128
A_COLS = 3 * A_WIDTH
B_COLS = 3 * B_WIDTH + B_DECAY_LORA + B_AAA_LORA + B_GATE_LORA
C_COLS = 2 * C_WIDTH
N_EXPERTS = 64
TOP_K = 8
N_GROUPS = 8
TOPK_GROUPS = 4
EXPERTS_PER_GROUP = N_EXPERTS // N_GROUPS
ROUTED_SCALE = 2.5
EXPERT_BLOCK = 512
RMS_EPS = 1e-6
LN_EPS = 1e-5
NEG_BIG = -1e30

V7X_LANES = 128
V7X_VMEM_LIMIT_BYTES = 56 * 1024 * 1024
RWKV_CHUNK = 64
RWKV_GROUP = 8

NN = (((1,), (0,)), ((), ()))
NT = (((1,), (1,)), ((), ()))
TN = (((0,), (0,)), ((), ()))


def _cparams(*sem):
    return pltpu.CompilerParams(dimension_semantics=sem, vmem_limit_bytes=V7X_VMEM_LIMIT_BYTES)


def _mm(a, b, dims=NN):
    return lax.dot_general(a.astype(BF16), b.astype(BF16), dims, preferred_element_type=F32)


def _split(a):
    hi = a.astype(BF16)
    lo = (a - hi.astype(F32)).astype(BF16)
    return hi, lo


def _mm3(a, b, dims=NN):
    ah, al = _split(a)
    bh, bl = _split(b)
    d = lambda x, y: lax.dot_general(x, y, dims, preferred_element_type=F32)
    return d(ah, bh) + d(ah, bl) + d(al, bh)


def _mm2(a, b_exact, dims=NN):
    ah, al = _split(a)
    d = lambda x: lax.dot_general(x, b_exact, dims, preferred_element_type=F32)
    return d(ah) + d(al)


def _pack_bf16_pair(x):
    n = x.shape[1] // 2
    bits = lax.bitcast_convert_type(x.astype(BF16).astype(F32), jnp.int32)
    return ((bits[:, :n] >> 16) & 0xFFFF) | bits[:, n:]


def _unpack_bf16_pair(u):
    lo = lax.bitcast_convert_type(u << 16, F32)
    hi = lax.bitcast_convert_type(u & jnp.int32(-65536), F32)
    return lo, hi


def _rms(x, eps=RMS_EPS):
    return x * lax.rsqrt(jnp.mean(x * x, axis=-1, keepdims=True) + eps)


def _sigmoid(x):
    return 1.0 / (1.0 + jnp.exp(-x))


def _silu(x):
    return x * _sigmoid(x)


def _adaln_kernel(c_ref, w_ref, b_ref, o_ref):
    c = c_ref[...]
    o_ref[0] = _mm3(_silu(c), w_ref[0]) + b_ref[0]


def _adaln(c, w_ada, b_ada):
    L, D, N = w_ada.shape
    Bn = c.shape[0]
    tn = min(N, 1536)
    return pl.pallas_call(
        _adaln_kernel,
        grid=(L, N // tn),
        in_specs=[
            pl.BlockSpec((Bn, D), lambda l, j: (0, 0)),
            pl.BlockSpec((1, D, tn), lambda l, j: (l, 0, j)),
            pl.BlockSpec((1, 1, tn), lambda l, j: (l, 0, j)),
        ],
        out_specs=pl.BlockSpec((1, Bn, tn), lambda l, j: (l, 0, j)),
        out_shape=jax.ShapeDtypeStruct((L, Bn, N), F32),
        compiler_params=_cparams("arbitrary", "arbitrary"),
        name="adaln",
    )(c, w_ada, b_ada.reshape(L, 1, N))


def _inproj_kernel(x_ref, g_ref, sc_ref, sh_ref, wa_ref, wvt_ref, wbc_ref, oa_ref, ovt_ref, obc_ref):
    x = x_ref[0]
    h = _rms(x) * g_ref[...] * (1.0 + sc_ref[0]) + sh_ref[0]
    hb = h.astype(BF16)
    oa_ref[0] = jnp.dot(hb, wa_ref[...], preferred_element_type=F32).astype(BF16)
    ovt_ref[0] = lax.dot_general(wvt_ref[...], hb, NT, preferred_element_type=F32).astype(BF16)
    obc_ref[0] = jnp.dot(hb, wbc_ref[...], preferred_element_type=F32)


def _inproj(x, g, sc, sh, wa, wvt, wbc, tm):
    Bn, S, D = x.shape
    na, nv, nbc = wa.shape[1], wvt.shape[0], wbc.shape[1]
    return pl.pallas_call(
        _inproj_kernel,
        grid=(Bn, S // tm),
        in_specs=[
            pl.BlockSpec((1, tm, D), lambda b, i: (b, i, 0)),
            pl.BlockSpec((1, D), lambda b, i: (0, 0)),
            pl.BlockSpec((1, 1, D), lambda b, i: (b, 0, 0)),
            pl.BlockSpec((1, 1, D), lambda b, i: (b, 0, 0)),
            pl.BlockSpec((D, na), lambda b, i: (0, 0)),
            pl.BlockSpec((nv, D), lambda b, i: (0, 0)),
            pl.BlockSpec((D, nbc), lambda b, i: (0, 0)),
        ],
        out_specs=[
            pl.BlockSpec((1, tm, na), lambda b, i: (b, i, 0)),
            pl.BlockSpec((1, nv, tm), lambda b, i: (b, 0, i)),
            pl.BlockSpec((1, tm, nbc), lambda b, i: (b, i, 0)),
        ],
        out_shape=[
            jax.ShapeDtypeStruct((Bn, S, na), BF16),
            jax.ShapeDtypeStruct((Bn, nv, S), BF16),
            jax.ShapeDtypeStruct((Bn, S, nbc), F32),
        ],
        compiler_params=_cparams("arbitrary", "arbitrary"),
        name="inproj",
    )(x, g.reshape(1, D), sc, sh, wa, wvt, wbc)


def _t5_bucket(dist):
    n = jnp.maximum(dist, 0)
    max_exact = N_BUCKETS // 2
    nf = jnp.maximum(n, 1).astype(F32)
    large = max_exact + (jnp.log(nf / max_exact) / math.log(MAX_DISTANCE / max_exact)
                         * (N_BUCKETS - max_exact)).astype(jnp.int32)
    large = jnp.minimum(large, N_BUCKETS - 1)
    return jnp.where(n < max_exact, n, large)


def _attn_band(table, tq):
    far = table[N_BUCKETS - 1].astype(F32)
    H = table.shape[1]
    nb = V7X_LANES
    L = 3 * nb
    m = jnp.arange(L)
    m = jnp.where(m < nb, m, m - L)
    cache = {}

    def block(c):
        if c not in cache:
            if c - (nb - 1) >= MAX_DISTANCE:
                cache[c] = jnp.zeros((H, nb, nb), F32)
            elif c + (nb - 1) < 0:
                cache[c] = jnp.full((H, nb, nb), NEG_BIG, F32)
            else:
                dist = m + c
                vals = jnp.where(dist[None] >= 0,
                                 jnp.transpose(table[_t5_bucket(dist)].astype(F32)) - far[:, None], NEG_BIG)
                cache[c] = jnp.tile(vals, (1, nb))[:, :nb * (L - 1)].reshape(H, nb, L - 1)[:, :, :nb]
        return cache[c]

    bands = []
    for off in (0, tq):
        rows = [jnp.concatenate([block(nb * (a - b) + off) for a in range(tq // nb)], axis=2)
                for b in range(2 * tq // nb)]
        bands.append(jnp.concatenate(rows, axis=1))
    return jnp.stack(bands)


def _attn_kernel(lam_ref, q_ref, k_ref, vt_ref, band_ref, g_ref, o_ref, *, tq, lambda_init):
    i = pl.program_id(2)
    q = q_ref[0] * jnp.asarray(A_QK_DIM ** -0.5, BF16)
    lane = lax.broadcasted_iota(jnp.int32, q.shape, 1)
    zero = jnp.zeros_like(q)
    qq = jnp.concatenate([jnp.where(lane < A_QK_DIM, q, zero),
                          jnp.where(lane >= A_QK_DIM, q, zero)], axis=0)

    kb0 = pl.multiple_of(jnp.maximum(i - 1, 0) * tq, tq)
    kb = k_ref[0, pl.ds(kb0, 2 * tq), :]
    band = band_ref[0, 0]
    s = lax.dot_general(kb, qq, NT, preferred_element_type=F32) + jnp.concatenate([band, band], axis=1)
    m = jnp.max(s, axis=0, keepdims=True)
    p = jnp.exp(s - m)
    l = jnp.sum(p, axis=0, keepdims=True)
    acc = jnp.dot(vt_ref[0, :, pl.ds(kb0, 2 * tq)], p.astype(BF16), preferred_element_type=F32)

    n_far = jnp.maximum(i - 1, 0)

    def logits(j):
        return lax.dot_general(k_ref[0, pl.ds(pl.multiple_of(j * tq, tq), tq), :], qq, NT,
                               preferred_element_type=F32)

    def body(j, carry):
        m, l, acc, s = carry
        s_next = logits(jnp.minimum(j + 1, n_far - 1))
        vtj = vt_ref[0, :, pl.ds(pl.multiple_of(j * tq, tq), tq)]
        m_new = jnp.maximum(m, jnp.max(s, axis=0, keepdims=True))
        alpha = jnp.exp(m - m_new)
        p = jnp.exp(s - m_new)
        l = alpha * l + jnp.sum(p, axis=0, keepdims=True)
        acc = alpha * acc + jnp.dot(vtj, p.astype(BF16), preferred_element_type=F32)
        return m_new, l, acc, s_next

    m, l, acc, _ = lax.fori_loop(0, n_far, body, (m, l, acc, logits(0)))

    lp = lam_ref[...]
    lam = (jnp.exp(jnp.sum(lp[0:1] * lp[1:2], axis=-1, keepdims=True))
           - jnp.exp(jnp.sum(lp[2:3] * lp[3:4], axis=-1, keepdims=True)) + lambda_init)
    o = acc / l
    o = o[:, :tq] - lam * o[:, tq:]
    o = o * lax.rsqrt(jnp.mean(o * o, axis=0, keepdims=True) + RMS_EPS) * g_ref[...] * (1.0 - lambda_init)
    o_ref[0] = jnp.transpose(o)


def _diff_attention(pa, vt, band_t, lam_par, subln_g, lambda_init, tq):
    Bn, S, _ = pa.shape
    W = A_HEAD_W
    kern = functools.partial(_attn_kernel, tq=tq, lambda_init=lambda_init)
    return pl.pallas_call(
        kern,
        grid=(Bn, A_HEADS, S // tq),
        in_specs=[
            pl.BlockSpec((4, A_QK_DIM), lambda b, h, i: (0, 0)),
            pl.BlockSpec((1, tq, W), lambda b, h, i: (b, i, h)),
            pl.BlockSpec((1, S, W), lambda b, h, i: (b, 0, A_HEADS + h)),
            pl.BlockSpec((1, W, S), lambda b, h, i: (b, h, 0)),
            pl.BlockSpec((1, 1, 2 * tq, tq), lambda b, h, i: (jnp.minimum(i, 1), h, 0, 0)),
            pl.BlockSpec((W, 1), lambda b, h, i: (0, 0)),
        ],
        out_specs=pl.BlockSpec((1, tq, W), lambda b, h, i: (b, i, h)),
        out_shape=jax.ShapeDtypeStruct((Bn, S, A_WIDTH), F32),
        compiler_params=_cparams("arbitrary", "arbitrary", "arbitrary"),
        name="diff_attn",
    )(lam_par, pa, pa, vt, band_t, subln_g.reshape(W, 1))


def _head_ones(n):
    r = lax.broadcasted_iota(jnp.int32, (n, n), 0) // B_HEAD_DIM
    c = lax.broadcasted_iota(jnp.int32, (n, n), 1) // B_HEAD_DIM
    return (r == c).astype(BF16)


def _rwkv_prep_kernel(pb_ref, prev_ref, mu_ref, w0_ref, w2_ref, a0_ref, a2_ref, g2_ref,
                      kk_ref, ka_ref, rk_ref,
                      rt_ref, at_ref, kt_ref, bt_ref, v_ref, wc_ref, bonus_ref, g_ref, *, tm):
    i = pl.program_id(1)
    C = RWKV_CHUNK
    x = pb_ref[0]
    row = lax.broadcasted_iota(jnp.int32, x.shape, 0)
    last = prev_ref[0, 7:8, :] * (i > 0).astype(F32)
    prev = jnp.where(row == 0, last, pltpu.roll(x, 1, 0))
    p = x + (prev - x) * mu_ref[...]
    o1, o2, o3 = B_WIDTH, 2 * B_WIDTH, 3 * B_WIDTH
    r, k, v = p[:, :o1], p[:, o1:o2], p[:, o2:o3]
    lora = p[:, o3:o3 + B_DECAY_LORA + B_AAA_LORA]
    gd = p[:, o3 + B_DECAY_LORA + B_AAA_LORA:]

    z = -(w0_ref[...] + _mm3(jnp.tanh(lora), w2_ref[...]))
    softplus = jnp.maximum(z, 0.0) + jnp.log(1.0 + jnp.exp(-jnp.abs(z)))
    logw = -jnp.exp(-softplus - 0.5)
    a = _sigmoid(a0_ref[...] + _mm3(lora, a2_ref[...]))
    g_ref[0] = _mm3(_sigmoid(gd), g2_ref[...])

    ones = _head_ones(B_WIDTH)
    kk = k * kk_ref[...]
    kk = kk * lax.rsqrt(jnp.maximum(_mm2(kk * kk, ones), 1e-24))
    k2 = k * (1.0 + (a - 1.0) * ka_ref[...])
    bonus_ref[0] = _mm2(r * k2 * rk_ref[...], ones) * v

    t_in = lax.broadcasted_iota(jnp.int32, (tm, B_WIDTH), 0) % C
    cum = logw
    sh = 1
    while sh < C:
        cum = cum + jnp.where(t_in >= sh, pltpu.roll(cum, sh, 0), 0.0)
        sh *= 2
    n = tm // C
    wc_ref[0] = jnp.exp(jnp.sum(logw.reshape(n, C, B_WIDTH), axis=1))
    e_pos = jnp.exp(cum)
    e_neg = jnp.exp(-cum)
    rt_ref[0] = r * e_pos
    at_ref[0] = -kk * jnp.exp(cum - logw)
    kt_ref[0] = k2 * e_neg
    bt_ref[0] = kk * a * e_neg
    v_ref[0] = v


def _rwkv_prep(pbc, mu, w0, w2p, a0, a2p, g2, k_k, k_a, r_k, tm):
    Bn, S, _ = pbc.shape
    W = B_WIDTH
    nl = B_DECAY_LORA + B_AAA_LORA
    row = lambda a: a.reshape(1, -1)
    full = lambda shp: pl.BlockSpec(shp, lambda b, i: (0,) * len(shp))
    seq = pl.BlockSpec((1, tm, W), lambda b, i: (b, i, 0))
    seq_shape = jax.ShapeDtypeStruct((Bn, S, W), F32)
    n = tm // RWKV_CHUNK
    return pl.pallas_call(
        functools.partial(_rwkv_prep_kernel, tm=tm),
        grid=(Bn, S // tm),
        in_specs=[
            pl.BlockSpec((1, tm, B_COLS), lambda b, i: (b, i, 0)),
            pl.BlockSpec((1, 8, B_COLS), lambda b, i: (b, jnp.maximum(i * (tm // 8) - 1, 0), 0)),
            full((1, B_COLS)), full((1, W)), full((nl, W)), full((1, W)), full((nl, W)),
            full((B_GATE_LORA, W)), full((1, W)), full((1, W)), full((1, W)),
        ],
        out_specs=[seq, seq, seq, seq, seq,
                   pl.BlockSpec((1, n, W), lambda b, i: (b, i, 0)), seq, seq],
        out_shape=[seq_shape] * 5 + [jax.ShapeDtypeStruct((Bn, S // RWKV_CHUNK, W), F32)] + [seq_shape] * 2,
        compiler_params=_cparams("arbitrary", "arbitrary"),
        name="rwkv_prep",
    )(pbc, pbc, row(mu), row(w0), w2p, row(a0), a2p, g2, row(k_k), row(k_a), row(r_k))


def _rwkv_scan_kernel(rt_ref, at_ref, kt_ref, bt_ref, v_ref, wc_ref, bonus_ref, g_ref,
                      lng_ref, lnb_ref, o_ref, state, *, tt):
    C = RWKV_CHUNK
    W = B_WIDTH

    @pl.when(pl.program_id(1) == 0)
    def _():
        state[...] = jnp.zeros_like(state)

    lane_head = lax.broadcasted_iota(jnp.int32, (C, W), 1) // B_HEAD_DIM
    tt_i = lax.broadcasted_iota(jnp.int32, (C, W), 0)
    ss_i = lax.broadcasted_iota(jnp.int32, (C, W), 1) % C
    strict = tt_i > ss_i
    incl = tt_i >= ss_i
    eye = (tt_i == ss_i).astype(F32)
    ones = _head_ones(W)
    bd_mask = ones.astype(F32)

    head_mask = [(lane_head == h).astype(BF16) for h in range(B_HEADS)]

    def bd_split(x):
        xb = x.astype(BF16)
        return jnp.concatenate([xb * mk for mk in head_mask], axis=0)

    def mm_bd(a, b_bd, dims=NN):
        return lax.dot_general(a.astype(BF16), b_bd, dims, preferred_element_type=F32)

    def state_free(gi, nb):
        G = range(RWKV_GROUP)
        sls = [pl.ds(pl.multiple_of((gi * RWKV_GROUP + j) * C, C), C) for j in G]
        rt = [rt_ref[nb, sl, :] for sl in sls]
        at = [at_ref[nb, sl, :] for sl in sls]
        kt = [kt_ref[nb, sl, :] for sl in sls]
        bt = [bt_ref[nb, sl, :] for sl in sls]
        v = [v_ref[nb, sl, :] for sl in sls]
        wc = [wc_ref[nb, pl.ds(gi * RWKV_GROUP + j, 1), :] for j in G]
        ar = [jnp.concatenate([at[j], rt[j]], axis=0) for j in G]
        bdb = [bd_split(bt[j]) for j in G]
        bdk = [bd_split(kt[j]) for j in G]
        a_b = [mm_bd(ar[j], bdb[j], NT) for j in G]
        a_k = [mm_bd(ar[j], bdk[j], NT) for j in G]
        lo = [jnp.where(strict, a_b[j][:C], 0.0) for j in G]
        a_ak = [jnp.where(strict, a_k[j][:C], 0.0) for j in G]
        a_rb = [jnp.where(incl, a_b[j][C:], 0.0) for j in G]
        a_rk = [jnp.where(incl, a_k[j][C:], 0.0) for j in G]
        pw = lo
        tinv = [eye + lo[j] for j in G]
        bdp = [bd_split(pw[j]) for j in G]
        span = 2
        while span < C:
            pw = [mm_bd(pw[j], bdp[j]) for j in G]
            bdp = [bd_split(pw[j]) for j in G]
            tinv = [tinv[j] + mm_bd(tinv[j], bdp[j]) for j in G]
            span *= 2
        bdv = [bd_split(v[j]) for j in G]
        bda = [bd_split(at[j]) for j in G]
        abar = [mm_bd(tinv[j], bda[j]) for j in G]
        akv = [bd_split(mm_bd(a_ak[j], bdv[j])) for j in G]
        u0 = [mm_bd(tinv[j], akv[j]) for j in G]
        y0 = [mm_bd(a_rk[j], bdv[j]) for j in G]
        kv = [_mm(v[j], kt[j] * wc[j], TN) * bd_mask for j in G]
        return [(jnp.concatenate([abar[j], rt[j]], axis=0), u0[j], y0[j], a_rb[j], bt[j] * wc[j], kv[j], wc[j])
                for j in G]

    def group(gi, carry):
        seqs = range(rt_ref.shape[0])
        pre = [state_free(gi, nb) for nb in seqs]
        s = [state[nb] for nb in seqs]
        ys = [[] for _ in seqs]
        for j in range(RWKV_GROUP):
            for nb in seqs:
                abar_rt, u0, y0, a_rb, btw, kv, wc = pre[nb][j]
                a_s = _mm(abar_rt, s[nb], NT)
                u = a_s[:C] + u0
                ys[nb].append(a_s[C:] + y0 + mm_bd(a_rb, bd_split(u)))
                s[nb] = s[nb] * wc + _mm(u, btw, TN) * bd_mask + kv
        sl = pl.ds(pl.multiple_of(gi * (RWKV_GROUP * C), RWKV_GROUP * C), RWKV_GROUP * C)
        for nb in seqs:
            state[nb] = s[nb]
            y = jnp.concatenate(ys[nb], axis=0)
            mean = _mm2(y, ones) * (1.0 / B_HEAD_DIM)
            d = y - mean
            var = _mm2(d * d, ones) * (1.0 / B_HEAD_DIM)
            yn = d * lax.rsqrt(var + B_LNX_EPS) * lng_ref[...] + lnb_ref[...]
            o_ref[nb, sl, :] = (yn + bonus_ref[nb, sl, :]) * g_ref[nb, sl, :]
        return carry

    lax.fori_loop(0, tt // (RWKV_GROUP * C), group, 0)


def _rwkv_scan(rt, at, kt, bt, v, wc, bonus, g, lnx_g, lnx_b, tt):
    Bn, S, W = rt.shape
    n = tt // RWKV_CHUNK
    nseq = 2 if Bn % 2 == 0 else 1
    seq = pl.BlockSpec((nseq, tt, W), lambda b, i: (b, i, 0))
    vec = pl.BlockSpec((1, W), lambda b, i: (0, 0))
    return pl.pallas_call(
        functools.partial(_rwkv_scan_kernel, tt=tt),
        grid=(Bn // nseq, S // tt),
        in_specs=[seq, seq, seq, seq, seq, pl.BlockSpec((nseq, n, W), lambda b, i: (b, i, 0)), seq, seq, vec, vec],
        out_specs=seq,
        out_shape=jax.ShapeDtypeStruct((Bn, S, W), F32),
        scratch_shapes=[pltpu.VMEM((nseq, B_HEADS * B_HEAD_DIM, W), F32)],
        compiler_params=_cparams("arbitrary", "arbitrary"),
        name="rwkv_scan",
    )(rt, at, kt, bt, v, wc, bonus, g, lnx_g.reshape(1, W), lnx_b.reshape(1, W))


def _gmlp_kernel(pc_ref, lng_ref, lnb_ref, ws_ref, bs_ref, o_ref, *, tm):
    x = pc_ref[0]
    z = x * (0.5 * (1.0 + jnp.tanh(math.sqrt(2.0 / math.pi) * (x + 0.044715 * (x * x * x)))))
    u, v = z[:, :C_WIDTH], z[:, C_WIDTH:]
    mu = jnp.mean(v, axis=-1, keepdims=True)
    d = v - mu
    var = jnp.mean(d * d, axis=-1, keepdims=True)
    vn = d * lax.rsqrt(var + LN_EPS) * lng_ref[...] + lnb_ref[...]
    group = lax.broadcasted_iota(jnp.int32, (CHUNK, C_WIDTH), 1) // C_GROUP_DIM
    tril = (lax.broadcasted_iota(jnp.int32, (CHUNK, CHUNK), 0)
            >= lax.broadcasted_iota(jnp.int32, (CHUNK, CHUNK), 1))
    ws = [jnp.where(tril, ws_ref[gi], 0.0).astype(BF16) for gi in range(C_GROUPS)]
    for c in range(tm // CHUNK):
        sl = slice(c * CHUNK, (c + 1) * CHUNK)
        vc = vn[sl].astype(BF16)
        sv = bs_ref[...]
        for gi in range(C_GROUPS):
            t = jnp.dot(ws[gi], vc, preferred_element_type=F32)
            sv = sv + jnp.where(group == gi, t, 0.0)
        o_ref[0, sl, :] = u[sl] * sv


def _gmlp(pbc, ln_g, ln_b, w_s, b_s, tm):
    Bn, S, _ = pbc.shape
    bs_wide = jnp.repeat(jnp.transpose(b_s), C_GROUP_DIM, axis=1)
    return pl.pallas_call(
        functools.partial(_gmlp_kernel, tm=tm),
        grid=(Bn, S // tm),
        in_specs=[
            pl.BlockSpec((1, tm, C_COLS), lambda b, i: (b, i, B_COLS // C_COLS)),
            pl.BlockSpec((1, C_WIDTH), lambda b, i: (0, 0)),
            pl.BlockSpec((1, C_WIDTH), lambda b, i: (0, 0)),
            pl.BlockSpec((C_GROUPS, CHUNK, CHUNK), lambda b, i: (0, 0, 0)),
            pl.BlockSpec((CHUNK, C_WIDTH), lambda b, i: (0, 0)),
        ],
        out_specs=pl.BlockSpec((1, tm, C_WIDTH), lambda b, i: (b, i, 0)),
        out_shape=jax.ShapeDtypeStruct((Bn, S, C_WIDTH), F32),
        compiler_params=_cparams("arbitrary", "arbitrary"),
        name="gmlp",
    )(pbc, ln_g.reshape(1, -1), ln_b.reshape(1, -1), w_s, bs_wide)


def _mid_kernel(ya_ref, yb_ref, yc_ref, x_ref, woa_ref, wob_ref, woc_ref, gpost_ref, g1_ref,
                gpre_ref, sc_ref, sh_ref, wr_ref, xo_ref, h_ref, score_ref):
    y = (_mm(ya_ref[0], woa_ref[...]) + _mm(yb_ref[0], wob_ref[...]) + _mm(yc_ref[0], woc_ref[...]))
    xn = x_ref[0] + g1_ref[0] * (_rms(y) * gpost_ref[...])
    xo_ref[0] = xn
    h = _rms(xn) * gpre_ref[...] * (1.0 + sc_ref[0]) + sh_ref[0]
    h_ref[0] = _pack_bf16_pair(h)
    score_ref[0] = _sigmoid(_mm3(wr_ref[...], h, NT))


def _mid(ya, yb, yc, x, woa, wob, woc, gpost, g1, gpre, sc, sh, wr, tm):
    Bn, S, D = x.shape
    NR = wr.shape[0]
    seq = lambda w: pl.BlockSpec((1, tm, w), lambda b, i: (b, i, 0))
    full = lambda shp: pl.BlockSpec(shp, lambda b, i: (0,) * len(shp))
    per_b = pl.BlockSpec((1, 1, D), lambda b, i: (b, 0, 0))
    return pl.pallas_call(
        _mid_kernel,
        grid=(Bn, S // tm),
        in_specs=[seq(A_WIDTH), seq(B_WIDTH), seq(C_WIDTH), seq(D),
                  full((A_WIDTH, D)), full((B_WIDTH, D)), full((C_WIDTH, D)),
                  full((1, D)), per_b, full((1, D)), per_b, per_b,
                  full((NR, D))],
        out_specs=[seq(D), seq(D // 2), pl.BlockSpec((1, NR, tm), lambda b, i: (b, 0, i))],
        out_shape=[jax.ShapeDtypeStruct((Bn, S, D), F32), jax.ShapeDtypeStruct((Bn, S, D // 2), jnp.int32),
                   jax.ShapeDtypeStruct((Bn, NR, S), F32)],
        compiler_params=_cparams("arbitrary", "arbitrary"),
        name="mid",
    )(ya, yb, yc, x, woa, wob, woc, gpost.reshape(1, D), g1, gpre.reshape(1, D), sc, sh, wr)


def _shared_expert_kernel(h_ref, ws1_ref, ws3_ref, ws2_ref, o_ref):
    lo, hi = _unpack_bf16_pair(h_ref[0])
    hb = jnp.concatenate([lo, hi], axis=1).astype(BF16)
    t = _silu(jnp.dot(hb, ws1_ref[...], preferred_element_type=F32)) * jnp.dot(
        hb, ws3_ref[...], preferred_element_type=F32)
    o_ref[0] = jnp.dot(t.astype(BF16), ws2_ref[...], preferred_element_type=F32)


def _shared_expert(hp, ws1, ws3, ws2, tm):
    Bn, S, DP = hp.shape
    D, F = ws1.shape
    full = lambda shp: pl.BlockSpec(shp, lambda b, i: (0,) * len(shp))
    return pl.pallas_call(
        _shared_expert_kernel,
        grid=(Bn, S // tm),
        in_specs=[pl.BlockSpec((1, tm, DP), lambda b, i: (b, i, 0)), full((D, F)), full((D, F)), full((F, D))],
        out_specs=pl.BlockSpec((1, tm, D), lambda b, i: (b, i, 0)),
        out_shape=jax.ShapeDtypeStruct((Bn, S, D), F32),
        compiler_params=_cparams("arbitrary", "arbitrary"),
        name="shared_expert",
    )(hp, ws1, ws3, ws2)


def _first_argmax(vals, iota, n):
    m = jnp.max(vals, axis=0, keepdims=True)
    idx = jnp.min(jnp.where(vals == m, iota, n), axis=0, keepdims=True)
    return m, idx


def _route_kernel(sc_ref, bias_ref, e_ref, w_ref, r_ref, cnt_ref, carry, *, tm):
    @pl.when((pl.program_id(0) == 0) & (pl.program_id(1) == 0))
    def _():
        carry[...] = jnp.zeros_like(carry)

    G = EXPERTS_PER_GROUP
    s = sc_ref[0]
    biased = s + bias_ref[...]
    neg_inf = jnp.float32(-jnp.inf)
    io8 = lax.broadcasted_iota(jnp.int32, (G, tm), 0)
    gs_rows = []
    for g in range(N_GROUPS):
        blk = biased[g * G:(g + 1) * G]
        m1, i1 = _first_argmax(blk, io8, G)
        m2 = jnp.max(jnp.where(io8 == i1, neg_inf, blk), axis=0, keepdims=True)
        gs_rows.append(m1 + m2)
    gs = jnp.concatenate(gs_rows, axis=0)
    gio = lax.broadcasted_iota(jnp.int32, (N_GROUPS, tm), 0)
    gsel = jnp.zeros((N_GROUPS, tm), jnp.bool_)
    for _ in range(TOPK_GROUPS):
        _, gi = _first_argmax(gs, gio, N_GROUPS)
        pick = gio == gi
        gsel = gsel | pick
        gs = jnp.where(pick, neg_inf, gs)
    masked = jnp.concatenate(
        [jnp.where(gsel[g:g + 1], biased[g * G:(g + 1) * G], neg_inf) for g in range(N_GROUPS)], axis=0)

    eio = lax.broadcasted_iota(jnp.int32, (N_EXPERTS, tm), 0)
    picks, e_rows, s_rows = [], [], []
    for _ in range(TOP_K):
        _, ei = _first_argmax(masked, eio, N_EXPERTS)
        pick = eio == ei
        picks.append(pick)
        e_rows.append(ei)
        s_rows.append(jnp.sum(jnp.where(pick, s, 0.0), axis=0, keepdims=True))
        masked = jnp.where(pick, neg_inf, masked)
    top_s = jnp.concatenate(s_rows, axis=0)
    w_ref[...] = top_s / (jnp.sum(top_s, axis=0, keepdims=True) + 1e-20) * ROUTED_SCALE
    e_ref[...] = jnp.concatenate(e_rows, axis=0)

    sel = jnp.zeros((N_EXPERTS, tm), F32)
    for pick in picks:
        sel = sel + pick.astype(F32)
    before = (lax.broadcasted_iota(jnp.int32, (tm, tm), 0) < lax.broadcasted_iota(jnp.int32, (tm, tm), 1))
    pos = carry[...] + jnp.dot(sel.astype(BF16), before.astype(BF16), preferred_element_type=F32)
    r_ref[...] = jnp.concatenate(
        [jnp.sum(jnp.where(pick, pos, 0.0), axis=0, keepdims=True) for pick in picks], axis=0).astype(jnp.int32)
    total = carry[...] + jnp.sum(sel, axis=1, keepdims=True)
    carry[...] = total
    cnt_ref[...] = jnp.broadcast_to(total, cnt_ref.shape).astype(jnp.int32)


def _route(scores_t, e_bias, tm):
    Bn, _, S = scores_t.shape
    T = Bn * S
    nt = S // tm
    tok = pl.BlockSpec((TOP_K, tm), lambda b, i: (0, b * nt + i))
    return pl.pallas_call(
        functools.partial(_route_kernel, tm=tm),
        grid=(Bn, nt),
        in_specs=[pl.BlockSpec((1, N_EXPERTS, tm), lambda b, i: (b, 0, i)),
                  pl.BlockSpec((N_EXPERTS, 1), lambda b, i: (0, 0))],
        out_specs=[tok, tok, tok, pl.BlockSpec((N_EXPERTS, V7X_LANES), lambda b, i: (0, 0))],
        out_shape=[jax.ShapeDtypeStruct((TOP_K, T), jnp.int32), jax.ShapeDtypeStruct((TOP_K, T), F32),
                   jax.ShapeDtypeStruct((TOP_K, T), jnp.int32),
                   jax.ShapeDtypeStruct((N_EXPERTS, V7X_LANES), jnp.int32)],
        scratch_shapes=[pltpu.VMEM((N_EXPERTS, 1), F32)],
        compiler_params=_cparams("arbitrary", "arbitrary"),
        name="route",
    )(scores_t, e_bias.reshape(N_EXPERTS, 1))


def _dest_kernel(start_ref, e_ref, r_ref, o_ref):
    e = e_ref[...]
    acc = r_ref[...]
    for ex in range(N_EXPERTS):
        acc = acc + jnp.where(e == ex, start_ref[ex], 0)
    o_ref[0] = acc


def _dest_rows(pad_start, eidx, rank, tt):
    K_, T = eidx.shape
    grid_spec = pltpu.PrefetchScalarGridSpec(
        num_scalar_prefetch=1,
        grid=(T // tt,),
        in_specs=[pl.BlockSpec((K_, tt), lambda i, st: (0, i)), pl.BlockSpec((K_, tt), lambda i, st: (0, i))],
        out_specs=pl.BlockSpec((1, K_, tt), lambda i, st: (i, 0, 0)),
    )
    return pl.pallas_call(
        _dest_kernel,
        grid_spec=grid_spec,
        out_shape=jax.ShapeDtypeStruct((T // tt, K_, tt), jnp.int32),
        compiler_params=_cparams("arbitrary"),
        name="dest_rows",
    )(pad_start, eidx, rank)


def _expert_kernel(blk_e_ref, n_used_ref, n_valid_ref, x_ref, w1_ref, w3_ref, w2_ref, o_ref, w1b, w3b, w2b):
    i = pl.program_id(0)

    @pl.when((i == 0) | (blk_e_ref[i] != blk_e_ref[jnp.maximum(i - 1, 0)]))
    def _():
        w1b[...] = w1_ref[0].astype(BF16)
        w3b[...] = w3_ref[0].astype(BF16)
        w2b[...] = w2_ref[0].astype(BF16)

    @pl.when(i < n_used_ref[0])
    def _():
        row = lax.broadcasted_iota(jnp.int32, x_ref.shape, 0)
        x_lo, x_hi = _unpack_bf16_pair(jnp.where(row < n_valid_ref[i], x_ref[...], 0))
        x_lo, x_hi = x_lo.astype(BF16), x_hi.astype(BF16)
        half = x_lo.shape[1]

        def up(wb):
            return (jnp.dot(x_lo, wb[:half, :], preferred_element_type=F32)
                    + jnp.dot(x_hi, wb[half:, :], preferred_element_type=F32))

        t = _silu(up(w1b)) * up(w3b)
        o_ref[...] = _pack_bf16_pair(jnp.dot(t.astype(BF16), w2b[...], preferred_element_type=F32))


def _experts(blk_e, n_used, n_valid, xs, w1, w3, w2, layer):
    P, DP = xs.shape
    EB = EXPERT_BLOCK
    n_blocks = blk_e.shape[0]
    D, F = w1.shape[2], w1.shape[3]
    rows = pl.BlockSpec((EB, DP), lambda i, be, nu, nv: (jnp.minimum(i, nu[0] - 1), 0))
    grid_spec = pltpu.PrefetchScalarGridSpec(
        num_scalar_prefetch=3,
        grid=(n_blocks,),
        in_specs=[
            rows,
            pl.BlockSpec((None, 1, D, F), lambda i, be, nu, nv: (layer, be[i], 0, 0)),
            pl.BlockSpec((None, 1, D, F), lambda i, be, nu, nv: (layer, be[i], 0, 0)),
            pl.BlockSpec((None, 1, F, D), lambda i, be, nu, nv: (layer, be[i], 0, 0)),
        ],
        out_specs=rows,
        scratch_shapes=[pltpu.VMEM((D, F), BF16), pltpu.VMEM((D, F), BF16), pltpu.VMEM((F, D), BF16)],
    )
    return pl.pallas_call(
        _expert_kernel,
        grid_spec=grid_spec,
        out_shape=jax.ShapeDtypeStruct((P, DP), jnp.int32),
        compiler_params=_cparams("arbitrary"),
        name="experts",
    )(blk_e, n_used, n_valid, xs, w1, w3, w2)


def _block_layout(counts, n_blocks):
    EB = EXPERT_BLOCK
    padded = (counts + EB - 1) // EB * EB
    ex = jnp.arange(N_EXPERTS)
    pad_end = jnp.sum(jnp.where(ex[:, None] <= ex[None, :], padded[:, None], 0), axis=0)
    pad_start = pad_end - padded
    blk_row = (jnp.arange(n_blocks) * EB)[:, None]
    blk_e = jnp.minimum(jnp.sum((pad_end[None, :] <= blk_row).astype(jnp.int32), axis=1), N_EXPERTS - 1)
    n_used = (jnp.sum(padded) // EB).astype(jnp.int32).reshape(1)
    mine = (pad_start[None, :] <= blk_row) & (blk_row < pad_end[None, :])
    n_valid = jnp.sum(jnp.where(mine, jnp.clip(counts[None, :] - (blk_row - pad_start[None, :]), 0, EB), 0), axis=1)
    return pad_start.astype(jnp.int32), blk_e.astype(jnp.int32), n_used, n_valid.astype(jnp.int32)


SC_GATHER_ROWS = 64


def _sc_gather_rows(table, idx):
    info = plsc.get_sparse_core_info()
    nc, ns = info.num_cores, info.num_subcores
    M = idx.shape[0]
    W = table.shape[1]
    b = SC_GATHER_ROWS
    per_worker = M // (nc * ns)
    steps = per_worker // b
    assert per_worker * nc * ns == M and steps * b == per_worker and steps % 2 == 0
    mesh = plsc.VectorSubcoreMesh(core_axis_name="c", subcore_axis_name="s")

    @functools.partial(
        pl.kernel, mesh=mesh,
        out_type=jax.ShapeDtypeStruct((M, W), table.dtype),
        scratch_types=[pltpu.VMEM((2, b), jnp.int32), pltpu.VMEM((2, b, W), table.dtype),
                       pltpu.SemaphoreType.DMA, pltpu.SemaphoreType.DMA],
        name="sc_gather_rows",
    )
    def gather(table_hbm, idx_hbm, out_hbm, idx_v, rows_v, sem0, sem1):
        wid = lax.axis_index("s") * nc + lax.axis_index("c")
        sems = (sem0, sem1)

        def base(s):
            return pl.multiple_of(wid * per_worker + s * b, b)

        def gather_copy(slot):
            return pltpu.make_async_copy(table_hbm.at[idx_v.at[slot]], rows_v.at[slot], sems[slot])

        def start(s, slot):
            pltpu.sync_copy(idx_hbm.at[pl.ds(base(s), b)], idx_v.at[slot])
            gather_copy(slot).start()

        def finish(s, slot):
            gather_copy(slot).wait()
            pltpu.sync_copy(rows_v.at[slot], out_hbm.at[pl.ds(base(s), b)])

        start(0, 0)

        @pl.loop(0, steps, step=2)
        def _(s):
            start(s + 1, 1)
            finish(s, 0)

            @pl.when(s + 2 < steps)
            def _():
                start(s + 2, 0)

            finish(s + 1, 1)

    return gather(table, idx)


def _sc_scatter_rows(rows, idx, n_out):
    info = plsc.get_sparse_core_info()
    nc, ns = info.num_cores, info.num_subcores
    T, W = rows.shape
    G, K_, b = idx.shape
    steps = G // (nc * ns)
    assert steps * nc * ns == G and G * b == T
    mesh = plsc.VectorSubcoreMesh(core_axis_name="c", subcore_axis_name="s")

    @functools.partial(
        pl.kernel, mesh=mesh,
        out_type=jax.ShapeDtypeStruct((n_out, W), rows.dtype),
        scratch_types=[pltpu.VMEM((K_, b), jnp.int32), pltpu.VMEM((b, W), rows.dtype), pltpu.SemaphoreType.DMA],
        name="sc_scatter_rows",
    )
    def scatter(rows_hbm, idx_hbm, out_hbm, idx_v, rows_v, sem):
        wid = lax.axis_index("s") * nc + lax.axis_index("c")

        @pl.loop(0, steps)
        def _(s):
            g = wid * steps + s
            pltpu.sync_copy(idx_hbm.at[g], idx_v)
            pltpu.sync_copy(rows_hbm.at[pl.ds(pl.multiple_of(g * b, b), b)], rows_v)
            copies = [pltpu.async_copy(rows_v, out_hbm.at[idx_v.at[k]], sem) for k in range(K_)]
            for cp in copies:
                cp.wait()

    return scatter(rows, idx)


def _combine_dense_kernel(rows_ref, w_ref, x_ref, shared_ref, gpost_ref, g2_ref, o_ref):
    w = w_ref[...]
    tt, half = rows_ref.shape[1], rows_ref.shape[2]
    y_lo = jnp.zeros((tt, half), F32)
    y_hi = jnp.zeros((tt, half), F32)
    for k in range(TOP_K):
        lo, hi = _unpack_bf16_pair(rows_ref[k])
        y_lo = y_lo + w[:, k:k + 1] * lo
        y_hi = y_hi + w[:, k:k + 1] * hi
    y = shared_ref[0] + jnp.concatenate([y_lo, y_hi], axis=1)
    o_ref[0] = x_ref[0] + g2_ref[0] * (_rms(y) * gpost_ref[...])


def _combine_dense(rows, w_tok, x, shared, gpost, g2, tt):
    Bn, S, D = x.shape
    K_, T, DP = rows.shape
    nt = S // tt
    seq = pl.BlockSpec((1, tt, D), lambda b, i: (b, i, 0))
    return pl.pallas_call(
        _combine_dense_kernel,
        grid=(Bn, nt),
        in_specs=[pl.BlockSpec((K_, tt, DP), lambda b, i: (0, b * nt + i, 0)),
                  pl.BlockSpec((tt, K_), lambda b, i: (b * nt + i, 0)), seq, seq,
                  pl.BlockSpec((1, D), lambda b, i: (0, 0)), pl.BlockSpec((1, 1, D), lambda b, i: (b, 0, 0))],
        out_specs=seq,
        out_shape=jax.ShapeDtypeStruct((Bn, S, D), F32),
        compiler_params=_cparams("arbitrary", "arbitrary"),
        name="combine_dense",
    )(rows, w_tok, x, shared, gpost.reshape(1, D), g2)


def kernel(x, c, w_ada, b_ada, norm_pre_mix, norm_post_mix, norm_pre_ffn, norm_post_ffn, w_in, w_out, rel_bias_table, diff_lambda, diff_subln, rwkv_mu, rwkv_w0, rwkv_w2, rwkv_a0, rwkv_a2, rwkv_g2, rwkv_k_k, rwkv_k_a, rwkv_r_k, rwkv_lnx_g, rwkv_lnx_b, gmlp_ln_g, gmlp_ln_b, gmlp_w_s, gmlp_b_s, router_w, router_bias, exp_w1, exp_w3, exp_w2, shared_w1, shared_w3, shared_w2):
    Bn, S, D = x.shape
    depth = w_ada.shape[0]
    tm = min(256, S)
    tm_wide = min(512, S)
    tq = min(512, S // 2)
    t_rwkv = min(512, S)

    mod = _adaln(c, w_ada, b_ada)
    band_t = _attn_band(rel_bias_table, tq)
    zpad = jnp.zeros((B_DECAY_LORA, B_WIDTH), F32)
    for l in range(depth):
        sh1, sc1, g1, sh2, sc2, g2 = [m.reshape(Bn, 1, D) for m in jnp.split(mod[l], 6, axis=-1)]
        w_in_b = w_in[l].astype(BF16)
        pa, vt, pbc = _inproj(x, norm_pre_mix[l], sc1, sh1, w_in_b[:, :2 * A_WIDTH],
                              jnp.transpose(w_in_b[:, 2 * A_WIDTH:A_COLS]), w_in_b[:, A_COLS:], tm)
        lambda_init = 0.8 - 0.6 * math.exp(-0.3 * l)
        ya = _diff_attention(pa, vt, band_t, diff_lambda[l], diff_subln[l], lambda_init, tq)
        prep = _rwkv_prep(pbc, rwkv_mu[l], rwkv_w0[l], jnp.concatenate([rwkv_w2[l], zpad], axis=0),
                          rwkv_a0[l], jnp.concatenate([zpad, rwkv_a2[l]], axis=0), rwkv_g2[l],
                          rwkv_k_k[l], rwkv_k_a[l], rwkv_r_k[l].reshape(-1), t_rwkv)
        yb = _rwkv_scan(*prep, rwkv_lnx_g[l], rwkv_lnx_b[l], t_rwkv)
        yc = _gmlp(pbc, gmlp_ln_g[l], gmlp_ln_b[l], gmlp_w_s[l], gmlp_b_s[l], tm)

        w_out_b = w_out[l].astype(BF16)
        wr_t = jnp.pad(jnp.transpose(router_w[l]), ((0, V7X_LANES - N_EXPERTS), (0, 0)))
        x, h, scores_t = _mid(
            ya, yb, yc, x, w_out_b[:A_WIDTH], w_out_b[A_WIDTH:A_WIDTH + B_WIDTH], w_out_b[A_WIDTH + B_WIDTH:],
            norm_post_mix[l], g1, norm_pre_ffn[l], sc2, sh2, wr_t, tm_wide)

        T = Bn * S
        n_blocks = -(-T * TOP_K // EXPERT_BLOCK) + N_EXPERTS
        eidx, wgt, rank, cnt = _route(scores_t, router_bias[l], tm)
        pad_start, blk_e, n_used, n_valid = _block_layout(cnt[:, 0], n_blocks)
        dest = _dest_rows(pad_start, eidx, rank, tm)
        b = SC_GATHER_ROWS
        dest_sc = jnp.transpose(dest.reshape(T // tm, TOP_K, tm // b, b), (0, 2, 1, 3)).reshape(T // b, TOP_K, b)
        xs = _sc_scatter_rows(h.reshape(T, D // 2), dest_sc, n_blocks * EXPERT_BLOCK)
        shared = _shared_expert(h, shared_w1[l].astype(BF16), shared_w3[l].astype(BF16),
                                shared_w2[l].astype(BF16), tm_wide)
        ys = _experts(blk_e, n_used, n_valid, xs, exp_w1, exp_w3, exp_w2, l)
        dest_kt = jnp.transpose(dest, (1, 0, 2)).reshape(TOP_K * T)
        rows = _sc_gather_rows(ys, dest_kt).reshape(TOP_K, T, D // 2)
        x = _combine_dense(rows, jnp.transpose(wgt), x, shared, norm_post_ffn[l], g2, tm)
    return x
```

```python
import functools
import math

import jax
import jax.numpy as jnp
from jax import lax
from jax.experimental import pallas as pl
from jax.experimental.pallas import tpu as pltpu
from jax.experimental.pallas import tpu_sc as plsc

F32 = jnp.float32
BF16 = jnp.bfloat16

A_HEADS = 4
A_QK_DIM = 64
A_HEAD_W = 2 * A_QK_DIM
A_WIDTH = A_HEADS * A_HEAD_W
N_BUCKETS = 32
MAX_DISTANCE = 128
B_HEADS = 4
B_HEAD_DIM = 64
B_WIDTH = B_HEADS * B_HEAD_DIM
B_DECAY_LORA = 64
B_AAA_LORA = 64
B_GATE_LORA = 128
B_LNX_EPS = 64e-5
C_GROUPS = 4
C_GROUP_DIM = 64
C_WIDTH = C_GROUPS * C_GROUP_DIM
CHUNK = 128
A_COLS = 3 * A_WIDTH
B_COLS = 3 * B_WIDTH + B_DECAY_LORA + B_AAA_LORA + B_GATE_LORA
C_COLS = 2 * C_WIDTH
N_EXPERTS = 64
TOP_K = 8
N_GROUPS = 8
TOPK_GROUPS = 4
EXPERTS_PER_GROUP = N_EXPERTS // N_GROUPS
ROUTED_SCALE = 2.5
EXPERT_BLOCK = 512
RMS_EPS = 1e-6
LN_EPS = 1e-5
NEG_BIG = -1e30

V7X_LANES = 128
BF16_SUBLANES = 16
V7X_VMEM_LIMIT_BYTES = 56 * 1024 * 1024
RWKV_CHUNK = 64
RWKV_GROUP = 8

NN = (((1,), (0,)), ((), ()))
NT = (((1,), (1,)), ((), ()))
TN = (((0,), (0,)), ((), ()))


def _cparams(*sem):
    return pltpu.CompilerParams(dimension_semantics=sem, vmem_limit_bytes=V7X_VMEM_LIMIT_BYTES)


def _mm(a, b, dims=NN):
    return lax.dot_general(a.astype(BF16), b.astype(BF16), dims, preferred_element_type=F32)


def _split(a):
    hi = a.astype(BF16)
    lo = (a - hi.astype(F32)).astype(BF16)
    return hi, lo


def _mm3(a, b, dims=NN):
    ah, al = _split(a)
    bh, bl = _split(b)
    d = lambda x, y: lax.dot_general(x, y, dims, preferred_element_type=F32)
    return d(ah, bh) + d(ah, bl) + d(al, bh)


def _mm2(a, b_exact, dims=NN):
    ah, al = _split(a)
    d = lambda x: lax.dot_general(x, b_exact, dims, preferred_element_type=F32)
    return d(ah) + d(al)


def _pack_bf16_pair(x):
    n = x.shape[1] // 2
    bits = lax.bitcast_convert_type(x.astype(BF16).astype(F32), jnp.int32)
    return ((bits[:, :n] >> 16) & 0xFFFF) | bits[:, n:]


def _unpack_bf16_pair(u):
    lo = lax.bitcast_convert_type(u << 16, F32)
    hi = lax.bitcast_convert_type(u & jnp.int32(-65536), F32)
    return lo, hi


def _rms(x, eps=RMS_EPS):
    return x * lax.rsqrt(jnp.mean(x * x, axis=-1, keepdims=True) + eps)


def _sigmoid(x):
    return 1.0 / (1.0 + jnp.exp(-x))


def _silu(x):
    return x * _sigmoid(x)


def _adaln_kernel(c_ref, w_ref, b_ref, o_ref):
    c = c_ref[...]
    o_ref[0] = _mm3(_silu(c), w_ref[0]) + b_ref[0]


def _adaln(c, w_ada, b_ada):
    L, D, N = w_ada.shape
    Bn = c.shape[0]
    tn = min(N, 1536)
    return pl.pallas_call(
        _adaln_kernel,
        grid=(L, N // tn),
        in_specs=[
            pl.BlockSpec((Bn, D), lambda l, j: (0, 0)),
            pl.BlockSpec((1, D, tn), lambda l, j: (l, 0, j)),
            pl.BlockSpec((1, 1, tn), lambda l, j: (l, 0, j)),
        ],
        out_specs=pl.BlockSpec((1, Bn, tn), lambda l, j: (l, 0, j)),
        out_shape=jax.ShapeDtypeStruct((L, Bn, N), F32),
        compiler_params=_cparams("arbitrary", "arbitrary"),
        name="adaln",
    )(c, w_ada, b_ada.reshape(L, 1, N))


def _inproj_kernel(x_ref, g_ref, sc_ref, sh_ref, wa_ref, wvt_ref, wbc_ref, oa_ref, ovt_ref, obc_ref):
    x = x_ref[0]
    h = _rms(x) * g_ref[...] * (1.0 + sc_ref[0]) + sh_ref[0]
    hb = h.astype(BF16)
    oa_ref[0] = jnp.dot(hb, wa_ref[...], preferred_element_type=F32).astype(BF16)
    ovt_ref[0] = lax.dot_general(wvt_ref[...], hb, NT, preferred_element_type=F32).astype(BF16)
    obc_ref[0] = jnp.dot(hb, wbc_ref[...], preferred_element_type=F32)


def _inproj(x, g, sc, sh, wa, wvt, wbc, tm):
    Bn, S, D = x.shape
    na, nv, nbc = wa.shape[1], wvt.shape[0], wbc.shape[1]
    return pl.pallas_call(
        _inproj_kernel,
        grid=(Bn, S // tm),
        in_specs=[
            pl.BlockSpec((1, tm, D), lambda b, i: (b, i, 0)),
            pl.BlockSpec((1, D), lambda b, i: (0, 0)),
            pl.BlockSpec((1, 1, D), lambda b, i: (b, 0, 0)),
            pl.BlockSpec((1, 1, D), lambda b, i: (b, 0, 0)),
            pl.BlockSpec((D, na), lambda b, i: (0, 0)),
            pl.BlockSpec((nv, D), lambda b, i: (0, 0)),
            pl.BlockSpec((D, nbc), lambda b, i: (0, 0)),
        ],
        out_specs=[
            pl.BlockSpec((1, tm, na), lambda b, i: (b, i, 0)),
            pl.BlockSpec((1, nv, tm), lambda b, i: (b, 0, i)),
            pl.BlockSpec((1, tm, nbc), lambda b, i: (b, i, 0)),
        ],
        out_shape=[
            jax.ShapeDtypeStruct((Bn, S, na), BF16),
            jax.ShapeDtypeStruct((Bn, nv, S), BF16),
            jax.ShapeDtypeStruct((Bn, S, nbc), F32),
        ],
        compiler_params=_cparams("arbitrary", "arbitrary"),
        name="inproj",
    )(x, g.reshape(1, D), sc, sh, wa, wvt, wbc)


def _t5_bucket(dist):
    n = jnp.maximum(dist, 0)
    max_exact = N_BUCKETS // 2
    nf = jnp.maximum(n, 1).astype(F32)
    large = max_exact + (jnp.log(nf / max_exact) / math.log(MAX_DISTANCE / max_exact)
                         * (N_BUCKETS - max_exact)).astype(jnp.int32)
    large = jnp.minimum(large, N_BUCKETS - 1)
    return jnp.where(n < max_exact, n, large)


def _attn_band(table, tq):
    far = table[N_BUCKETS - 1].astype(F32)
    H = table.shape[1]
    nb = V7X_LANES
    L = 3 * nb
    m = jnp.arange(L)
    m = jnp.where(m < nb, m, m - L)
    cache = {}

    def block(c):
        if c not in cache:
            if c - (nb - 1) >= MAX_DISTANCE:
                cache[c] = jnp.zeros((H, nb, nb), F32)
            elif c + (nb - 1) < 0:
                cache[c] = jnp.full((H, nb, nb), NEG_BIG, F32)
            else:
                dist = m + c
                vals = jnp.where(dist[None] >= 0,
                                 jnp.transpose(table[_t5_bucket(dist)].astype(F32)) - far[:, None], NEG_BIG)
                cache[c] = jnp.tile(vals, (1, nb))[:, :nb * (L - 1)].reshape(H, nb, L - 1)[:, :, :nb]
        return cache[c]

    bands = []
    for off in (0, tq):
        rows = [jnp.concatenate([block(nb * (a - b) + off) for a in range(tq // nb)], axis=2)
                for b in range(2 * tq // nb)]
        bands.append(jnp.concatenate(rows, axis=1))
    return jnp.stack(bands)


def _attn_kernel(lam_ref, q_ref, k_ref, vt_ref, band_ref, g_ref, o_ref, *, tq, lambda_init):
    i = pl.program_id(2)
    q = q_ref[0] * jnp.asarray(A_QK_DIM ** -0.5, BF16)
    lane = lax.broadcasted_iota(jnp.int32, q.shape, 1)
    zero = jnp.zeros_like(q)
    qq = jnp.concatenate([jnp.where(lane < A_QK_DIM, q, zero),
                          jnp.where(lane >= A_QK_DIM, q, zero)], axis=0)

    kb0 = pl.multiple_of(jnp.maximum(i - 1, 0) * tq, tq)
    kb = k_ref[0, pl.ds(kb0, 2 * tq), :]
    band = band_ref[0, 0]
    s = lax.dot_general(kb, qq, NT, preferred_element_type=F32) + jnp.concatenate([band, band], axis=1)
    m = jnp.max(s, axis=0, keepdims=True)

    def weighted_values(keys, p):
        vt = jnp.concatenate([vt_ref[0, :, keys], jnp.ones((BF16_SUBLANES, p.shape[0]), BF16)], axis=0)
        return jnp.dot(vt, p, preferred_element_type=F32)

    acc = weighted_values(pl.ds(kb0, 2 * tq), jnp.exp((s - m).astype(BF16)))

    n_far = jnp.maximum(i - 1, 0)

    def logits(j):
        return lax.dot_general(k_ref[0, pl.ds(pl.multiple_of(j * tq, tq), tq), :], qq, NT,
                               preferred_element_type=F32)

    def body(j, carry):
        m, acc, s = carry
        s_next = logits(jnp.minimum(j + 1, n_far - 1))
        m_new = jnp.maximum(m, jnp.max(s, axis=0, keepdims=True))
        alpha = jnp.exp(m - m_new)
        p = jnp.exp((s - m_new).astype(BF16))
        acc = alpha * acc + weighted_values(pl.ds(pl.multiple_of(j * tq, tq), tq), p)
        return m_new, acc, s_next

    m, acc, _ = lax.fori_loop(0, n_far, body, (m, acc, logits(0)))

    lp = lam_ref[...]
    lam = (jnp.exp(jnp.sum(lp[0:1] * lp[1:2], axis=-1, keepdims=True))
           - jnp.exp(jnp.sum(lp[2:3] * lp[3:4], axis=-1, keepdims=True)) + lambda_init)
    o = acc[:A_HEAD_W] / acc[A_HEAD_W:A_HEAD_W + 1]
    o = o[:, :tq] - lam * o[:, tq:]
    o = o * lax.rsqrt(jnp.mean(o * o, axis=0, keepdims=True) + RMS_EPS) * g_ref[...] * (1.0 - lambda_init)
    o_ref[0] = jnp.transpose(o)


def _diff_attention(pa, vt, band_t, lam_par, subln_g, lambda_init, tq):
    Bn, S, _ = pa.shape
    W = A_HEAD_W
    kern = functools.partial(_attn_kernel, tq=tq, lambda_init=lambda_init)
    return pl.pallas_call(
        kern,
        grid=(Bn, A_HEADS, S // tq),
        in_specs=[
            pl.BlockSpec((4, A_QK_DIM), lambda b, h, i: (0, 0)),
            pl.BlockSpec((1, tq, W), lambda b, h, i: (b, i, h)),
            pl.BlockSpec((1, S, W), lambda b, h, i: (b, 0, A_HEADS + h)),
            pl.BlockSpec((1, W, S), lambda b, h, i: (b, h, 0)),
            pl.BlockSpec((1, 1, 2 * tq, tq), lambda b, h, i: (jnp.minimum(i, 1), h, 0, 0)),
            pl.BlockSpec((W, 1), lambda b, h, i: (0, 0)),
        ],
        out_specs=pl.BlockSpec((1, tq, W), lambda b, h, i: (b, i, h)),
        out_shape=jax.ShapeDtypeStruct((Bn, S, A_WIDTH), F32),
        compiler_params=_cparams("arbitrary", "arbitrary", "arbitrary"),
        name="diff_attn",
    )(lam_par, pa, pa, vt, band_t, subln_g.reshape(W, 1))


def _head_ones(n):
    r = lax.broadcasted_iota(jnp.int32, (n, n), 0) // B_HEAD_DIM
    c = lax.broadcasted_iota(jnp.int32, (n, n), 1) // B_HEAD_DIM
    return (r == c).astype(BF16)


def _rwkv_prep_kernel(pb_ref, prev_ref, mu_ref, w0_ref, w2_ref, a0_ref, a2_ref, g2_ref,
                      kk_ref, ka_ref, rk_ref,
                      rt_ref, at_ref, kt_ref, bt_ref, v_ref, wc_ref, bonus_ref, g_ref, *, tm):
    i = pl.program_id(1)
    C = RWKV_CHUNK
    x = pb_ref[0]
    row = lax.broadcasted_iota(jnp.int32, x.shape, 0)
    last = prev_ref[0, 7:8, :] * (i > 0).astype(F32)
    prev = jnp.where(row == 0, last, pltpu.roll(x, 1, 0))
    p = x + (prev - x) * mu_ref[...]
    o1, o2, o3 = B_WIDTH, 2 * B_WIDTH, 3 * B_WIDTH
    r, k, v = p[:, :o1], p[:, o1:o2], p[:, o2:o3]
    lora = p[:, o3:o3 + B_DECAY_LORA + B_AAA_LORA]
    gd = p[:, o3 + B_DECAY_LORA + B_AAA_LORA:]

    z = -(w0_ref[...] + _mm3(jnp.tanh(lora), w2_ref[...]))
    softplus = jnp.maximum(z, 0.0) + jnp.log(1.0 + jnp.exp(-jnp.abs(z)))
    logw = -jnp.exp(-softplus - 0.5)
    a = _sigmoid(a0_ref[...] + _mm3(lora, a2_ref[...]))
    g_ref[0] = _mm3(_sigmoid(gd), g2_ref[...])

    ones = _head_ones(B_WIDTH)
    kk = k * kk_ref[...]
    kk = kk * lax.rsqrt(jnp.maximum(_mm2(kk * kk, ones), 1e-24))
    k2 = k * (1.0 + (a - 1.0) * ka_ref[...])
    bonus_ref[0] = _mm2(r * k2 * rk_ref[...], ones) * v

    t_in = lax.broadcasted_iota(jnp.int32, (tm, B_WIDTH), 0) % C
    cum = logw
    sh = 1
    while sh < C:
        cum = cum + jnp.where(t_in >= sh, pltpu.roll(cum, sh, 0), 0.0)
        sh *= 2
    n = tm // C
    wc_ref[0] = jnp.exp(jnp.sum(logw.reshape(n, C, B_WIDTH), axis=1))
    e_pos = jnp.exp(cum)
    e_neg = jnp.exp(-cum)
    rt_ref[0] = r * e_pos
    at_ref[0] = -kk * jnp.exp(cum - logw)
    kt_ref[0] = k2 * e_neg
    bt_ref[0] = kk * a * e_neg
    v_ref[0] = v


def _rwkv_prep(pbc, mu, w0, w2p, a0, a2p, g2, k_k, k_a, r_k, tm):
    Bn, S, _ = pbc.shape
    W = B_WIDTH
    nl = B_DECAY_LORA + B_AAA_LORA
    row = lambda a: a.reshape(1, -1)
    full = lambda shp: pl.BlockSpec(shp, lambda b, i: (0,) * len(shp))
    seq = pl.BlockSpec((1, tm, W), lambda b, i: (b, i, 0))
    seq_shape = jax.ShapeDtypeStruct((Bn, S, W), F32)
    n = tm // RWKV_CHUNK
    return pl.pallas_call(
        functools.partial(_rwkv_prep_kernel, tm=tm),
        grid=(Bn, S // tm),
        in_specs=[
            pl.BlockSpec((1, tm, B_COLS), lambda b, i: (b, i, 0)),
            pl.BlockSpec((1, 8, B_COLS), lambda b, i: (b, jnp.maximum(i * (tm // 8) - 1, 0), 0)),
            full((1, B_COLS)), full((1, W)), full((nl, W)), full((1, W)), full((nl, W)),
            full((B_GATE_LORA, W)), full((1, W)), full((1, W)), full((1, W)),
        ],
        out_specs=[seq, seq, seq, seq, seq,
                   pl.BlockSpec((1, n, W), lambda b, i: (b, i, 0)), seq, seq],
        out_shape=[seq_shape] * 5 + [jax.ShapeDtypeStruct((Bn, S // RWKV_CHUNK, W), F32)] + [seq_shape] * 2,
        compiler_params=_cparams("arbitrary", "arbitrary"),
        name="rwkv_prep",
    )(pbc, pbc, row(mu), row(w0), w2p, row(a0), a2p, g2, row(k_k), row(k_a), row(r_k))


def _rwkv_scan_kernel(rt_ref, at_ref, kt_ref, bt_ref, v_ref, wc_ref, bonus_ref, g_ref,
                      lng_ref, lnb_ref, o_ref, state, *, tt):
    C = RWKV_CHUNK
    W = B_WIDTH

    @pl.when(pl.program_id(1) == 0)
    def _():
        state[...] = jnp.zeros_like(state)

    lane_head = lax.broadcasted_iota(jnp.int32, (C, W), 1) // B_HEAD_DIM
    tt_i = lax.broadcasted_iota(jnp.int32, (C, W), 0)
    ss_i = lax.broadcasted_iota(jnp.int32, (C, W), 1) % C
    strict = tt_i > ss_i
    incl = tt_i >= ss_i
    eye = (tt_i == ss_i).astype(F32)
    ones = _head_ones(W)
    bd_mask = ones.astype(F32)

    head_mask = [(lane_head == h).astype(BF16) for h in range(B_HEADS)]

    def bd_split(x):
        xb = x.astype(BF16)
        return jnp.concatenate([xb * mk for mk in head_mask], axis=0)

    def mm_bd(a, b_bd, dims=NN):
        return lax.dot_general(a.astype(BF16), b_bd, dims, preferred_element_type=F32)

    def state_free(gi, nb):
        G = range(RWKV_GROUP)
        sls = [pl.ds(pl.multiple_of((gi * RWKV_GROUP + j) * C, C), C) for j in G]
        rt = [rt_ref[nb, sl, :] for sl in sls]
        at = [at_ref[nb, sl, :] for sl in sls]
        kt = [kt_ref[nb, sl, :] for sl in sls]
        bt = [bt_ref[nb, sl, :] for sl in sls]
        v = [v_ref[nb, sl, :] for sl in sls]
        wc = [wc_ref[nb, pl.ds(gi * RWKV_GROUP + j, 1), :] for j in G]
        ar = [jnp.concatenate([at[j], rt[j]], axis=0) for j in G]
        bdb = [bd_split(bt[j]) for j in G]
        bdk = [bd_split(kt[j]) for j in G]
        a_b = [mm_bd(ar[j], bdb[j], NT) for j in G]
        a_k = [mm_bd(ar[j], bdk[j], NT) for j in G]
        lo = [jnp.where(strict, a_b[j][:C], 0.0) for j in G]
        a_ak = [jnp.where(strict, a_k[j][:C], 0.0) for j in G]
        a_rb = [jnp.where(incl, a_b[j][C:], 0.0) for j in G]
        a_rk = [jnp.where(incl, a_k[j][C:], 0.0) for j in G]
        pw = lo
        tinv = [eye + lo[j] for j in G]
        bdp = [bd_split(pw[j]) for j in G]
        span = 2
        while span < C:
            pw = [mm_bd(pw[j], bdp[j]) for j in G]
            bdp = [bd_split(pw[j]) for j in G]
            tinv = [tinv[j] + mm_bd(tinv[j], bdp[j]) for j in G]
            span *= 2
        bdv = [bd_split(v[j]) for j in G]
        bda = [bd_split(at[j]) for j in G]
        abar = [mm_bd(tinv[j], bda[j]) for j in G]
        akv = [bd_split(mm_bd(a_ak[j], bdv[j])) for j in G]
        u0 = [mm_bd(tinv[j], akv[j]) for j in G]
        y0 = [mm_bd(a_rk[j], bdv[j]) for j in G]
        kv = [_mm(v[j], kt[j] * wc[j], TN) * bd_mask for j in G]
        return [(jnp.concatenate([abar[j], rt[j]], axis=0), u0[j], y0[j], a_rb[j], bt[j] * wc[j], kv[j], wc[j])
                for j in G]

    def group(gi, carry):
        seqs = range(rt_ref.shape[0])
        pre = [state_free(gi, nb) for nb in seqs]
        s = [state[nb] for nb in seqs]
        ys = [[] for _ in seqs]
        for j in range(RWKV_GROUP):
            for nb in seqs:
                abar_rt, u0, y0, a_rb, btw, kv, wc = pre[nb][j]
                a_s = _mm(abar_rt, s[nb], NT)
                u = a_s[:C] + u0
                ys[nb].append(a_s[C:] + y0 + mm_bd(a_rb, bd_split(u)))
                s[nb] = s[nb] * wc + _mm(u, btw, TN) * bd_mask + kv
        sl = pl.ds(pl.multiple_of(gi * (RWKV_GROUP * C), RWKV_GROUP * C), RWKV_GROUP * C)
        for nb in seqs:
            state[nb] = s[nb]
            y = jnp.concatenate(ys[nb], axis=0)
            mean = _mm2(y, ones) * (1.0 / B_HEAD_DIM)
            d = y - mean
            var = _mm2(d * d, ones) * (1.0 / B_HEAD_DIM)
            yn = d * lax.rsqrt(var + B_LNX_EPS) * lng_ref[...] + lnb_ref[...]
            o_ref[nb, sl, :] = (yn + bonus_ref[nb, sl, :]) * g_ref[nb, sl, :]
        return carry

    lax.fori_loop(0, tt // (RWKV_GROUP * C), group, 0)


def _rwkv_scan(rt, at, kt, bt, v, wc, bonus, g, lnx_g, lnx_b, tt):
    Bn, S, W = rt.shape
    n = tt // RWKV_CHUNK
    nseq = 2 if Bn % 2 == 0 else 1
    seq = pl.BlockSpec((nseq, tt, W), lambda b, i: (b, i, 0))
    vec = pl.BlockSpec((1, W), lambda b, i: (0, 0))
    return pl.pallas_call(
        functools.partial(_rwkv_scan_kernel, tt=tt),
        grid=(Bn // nseq, S // tt),
        in_specs=[seq, seq, seq, seq, seq, pl.BlockSpec((nseq, n, W), lambda b, i: (b, i, 0)), seq, seq, vec, vec],
        out_specs=seq,
        out_shape=jax.ShapeDtypeStruct((Bn, S, W), F32),
        scratch_shapes=[pltpu.VMEM((nseq, B_HEADS * B_HEAD_DIM, W), F32)],
        compiler_params=_cparams("arbitrary", "arbitrary"),
        name="rwkv_scan",
    )(rt, at, kt, bt, v, wc, bonus, g, lnx_g.reshape(1, W), lnx_b.reshape(1, W))


def _gmlp_kernel(pc_ref, lng_ref, lnb_ref, ws_ref, bs_ref, o_ref, *, tm):
    x = pc_ref[0]
    z = x * (0.5 * (1.0 + jnp.tanh(math.sqrt(2.0 / math.pi) * (x + 0.044715 * (x * x * x)))))
    u, v = z[:, :C_WIDTH], z[:, C_WIDTH:]
    mu = jnp.mean(v, axis=-1, keepdims=True)
    d = v - mu
    var = jnp.mean(d * d, axis=-1, keepdims=True)
    vn = d * lax.rsqrt(var + LN_EPS) * lng_ref[...] + lnb_ref[...]
    group = lax.broadcasted_iota(jnp.int32, (CHUNK, C_WIDTH), 1) // C_GROUP_DIM
    tril = (lax.broadcasted_iota(jnp.int32, (CHUNK, CHUNK), 0)
            >= lax.broadcasted_iota(jnp.int32, (CHUNK, CHUNK), 1))
    ws = [jnp.where(tril, ws_ref[gi], 0.0).astype(BF16) for gi in range(C_GROUPS)]
    for c in range(tm // CHUNK):
        sl = slice(c * CHUNK, (c + 1) * CHUNK)
        vc = vn[sl].astype(BF16)
        sv = bs_ref[...]
        for gi in range(C_GROUPS):
            t = jnp.dot(ws[gi], vc, preferred_element_type=F32)
            sv = sv + jnp.where(group == gi, t, 0.0)
        o_ref[0, sl, :] = u[sl] * sv


def _gmlp(pbc, ln_g, ln_b, w_s, b_s, tm):
    Bn, S, _ = pbc.shape
    bs_wide = jnp.repeat(jnp.transpose(b_s), C_GROUP_DIM, axis=1)
    return pl.pallas_call(
        functools.partial(_gmlp_kernel, tm=tm),
        grid=(Bn, S // tm),
        in_specs=[
            pl.BlockSpec((1, tm, C_COLS), lambda b, i: (b, i, B_COLS // C_COLS)),
            pl.BlockSpec((1, C_WIDTH), lambda b, i: (0, 0)),
            pl.BlockSpec((1, C_WIDTH), lambda b, i: (0, 0)),
            pl.BlockSpec((C_GROUPS, CHUNK, CHUNK), lambda b, i: (0, 0, 0)),
            pl.BlockSpec((CHUNK, C_WIDTH), lambda b, i: (0, 0)),
        ],
        out_specs=pl.BlockSpec((1, tm, C_WIDTH), lambda b, i: (b, i, 0)),
        out_shape=jax.ShapeDtypeStruct((Bn, S, C_WIDTH), F32),
        compiler_params=_cparams("arbitrary", "arbitrary"),
        name="gmlp",
    )(pbc, ln_g.reshape(1, -1), ln_b.reshape(1, -1), w_s, bs_wide)


def _mid_kernel(ya_ref, yb_ref, yc_ref, x_ref, woa_ref, wob_ref, woc_ref, gpost_ref, g1_ref,
                gpre_ref, sc_ref, sh_ref, wr_ref, xo_ref, h_ref, score_ref):
    y = (_mm(ya_ref[0], woa_ref[...]) + _mm(yb_ref[0], wob_ref[...]) + _mm(yc_ref[0], woc_ref[...]))
    xn = x_ref[0] + g1_ref[0] * (_rms(y) * gpost_ref[...])
    xo_ref[0] = xn
    h = _rms(xn) * gpre_ref[...] * (1.0 + sc_ref[0]) + sh_ref[0]
    h_ref[0] = _pack_bf16_pair(h)
    score_ref[0] = _sigmoid(_mm3(wr_ref[...], h, NT))


def _mid(ya, yb, yc, x, woa, wob, woc, gpost, g1, gpre, sc, sh, wr, tm):
    Bn, S, D = x.shape
    NR = wr.shape[0]
    seq = lambda w: pl.BlockSpec((1, tm, w), lambda b, i: (b, i, 0))
    full = lambda shp: pl.BlockSpec(shp, lambda b, i: (0,) * len(shp))
    per_b = pl.BlockSpec((1, 1, D), lambda b, i: (b, 0, 0))
    return pl.pallas_call(
        _mid_kernel,
        grid=(Bn, S // tm),
        in_specs=[seq(A_WIDTH), seq(B_WIDTH), seq(C_WIDTH), seq(D),
                  full((A_WIDTH, D)), full((B_WIDTH, D)), full((C_WIDTH, D)),
                  full((1, D)), per_b, full((1, D)), per_b, per_b,
                  full((NR, D))],
        out_specs=[seq(D), seq(D // 2), pl.BlockSpec((1, NR, tm), lambda b, i: (b, 0, i))],
        out_shape=[jax.ShapeDtypeStruct((Bn, S, D), F32), jax.ShapeDtypeStruct((Bn, S, D // 2), jnp.int32),
                   jax.ShapeDtypeStruct((Bn, NR, S), F32)],
        compiler_params=_cparams("arbitrary", "arbitrary"),
        name="mid",
    )(ya, yb, yc, x, woa, wob, woc, gpost.reshape(1, D), g1, gpre.reshape(1, D), sc, sh, wr)


def _shared_expert_kernel(h_ref, ws1_ref, ws3_ref, ws2_ref, o_ref):
    lo, hi = _unpack_bf16_pair(h_ref[0])
    hb = jnp.concatenate([lo, hi], axis=1).astype(BF16)
    t = _silu(jnp.dot(hb, ws1_ref[...], preferred_element_type=F32)) * jnp.dot(
        hb, ws3_ref[...], preferred_element_type=F32)
    o_ref[0] = jnp.dot(t.astype(BF16), ws2_ref[...], preferred_element_type=F32)


def _shared_expert(hp, ws1, ws3, ws2, tm):
    Bn, S, DP = hp.shape
    D, F = ws1.shape
    full = lambda shp: pl.BlockSpec(shp, lambda b, i: (0,) * len(shp))
    return pl.pallas_call(
        _shared_expert_kernel,
        grid=(Bn, S // tm),
        in_specs=[pl.BlockSpec((1, tm, DP), lambda b, i: (b, i, 0)), full((D, F)), full((D, F)), full((F, D))],
        out_specs=pl.BlockSpec((1, tm, D), lambda b, i: (b, i, 0)),
        out_shape=jax.ShapeDtypeStruct((Bn, S, D), F32),
        compiler_params=_cparams("arbitrary", "arbitrary"),
        name="shared_expert",
    )(hp, ws1, ws3, ws2)


def _first_argmax(vals, iota, n):
    m = jnp.max(vals, axis=0, keepdims=True)
    idx = jnp.min(jnp.where(vals == m, iota, n), axis=0, keepdims=True)
    return m, idx


def _route_kernel(sc_ref, bias_ref, e_ref, w_ref, r_ref, cnt_ref, carry, *, tm):
    @pl.when((pl.program_id(0) == 0) & (pl.program_id(1) == 0))
    def _():
        carry[...] = jnp.zeros_like(carry)

    G = EXPERTS_PER_GROUP
    s = sc_ref[0]
    biased = s + bias_ref[...]
    neg_inf = jnp.float32(-jnp.inf)
    io8 = lax.broadcasted_iota(jnp.int32, (G, tm), 0)
    gs_rows = []
    for g in range(N_GROUPS):
        blk = biased[g * G:(g + 1) * G]
        m1, i1 = _first_argmax(blk, io8, G)
        m2 = jnp.max(jnp.where(io8 == i1, neg_inf, blk), axis=0, keepdims=True)
        gs_rows.append(m1 + m2)
    gs = jnp.concatenate(gs_rows, axis=0)
    gio = lax.broadcasted_iota(jnp.int32, (N_GROUPS, tm), 0)
    gsel = jnp.zeros((N_GROUPS, tm), jnp.bool_)
    for _ in range(TOPK_GROUPS):
        _, gi = _first_argmax(gs, gio, N_GROUPS)
        pick = gio == gi
        gsel = gsel | pick
        gs = jnp.where(pick, neg_inf, gs)
    masked = jnp.concatenate(
        [jnp.where(gsel[g:g + 1], biased[g * G:(g + 1) * G], neg_inf) for g in range(N_GROUPS)], axis=0)

    eio = lax.broadcasted_iota(jnp.int32, (N_EXPERTS, tm), 0)
    picks, e_rows, s_rows = [], [], []
    for _ in range(TOP_K):
        _, ei = _first_argmax(masked, eio, N_EXPERTS)
        pick = eio == ei
        picks.append(pick)
        e_rows.append(ei)
        s_rows.append(jnp.sum(jnp.where(pick, s, 0.0), axis=0, keepdims=True))
        masked = jnp.where(pick, neg_inf, masked)
    top_s = jnp.concatenate(s_rows, axis=0)
    w_ref[...] = top_s / (jnp.sum(top_s, axis=0, keepdims=True) + 1e-20) * ROUTED_SCALE
    e_ref[...] = jnp.concatenate(e_rows, axis=0)

    sel = jnp.zeros((N_EXPERTS, tm), F32)
    for pick in picks:
        sel = sel + pick.astype(F32)
    before = (lax.broadcasted_iota(jnp.int32, (tm, tm), 0) < lax.broadcasted_iota(jnp.int32, (tm, tm), 1))
    pos = carry[...] + jnp.dot(sel.astype(BF16), before.astype(BF16), preferred_element_type=F32)
    r_ref[...] = jnp.concatenate(
        [jnp.sum(jnp.where(pick, pos, 0.0), axis=0, keepdims=True) for pick in picks], axis=0).astype(jnp.int32)
    total = carry[...] + jnp.sum(sel, axis=1, keepdims=True)
    carry[...] = total
    cnt_ref[...] = jnp.broadcast_to(total, cnt_ref.shape).astype(jnp.int32)


def _route(scores_t, e_bias, tm):
    Bn, _, S = scores_t.shape
    T = Bn * S
    nt = S // tm
    tok = pl.BlockSpec((TOP_K, tm), lambda b, i: (0, b * nt + i))
    return pl.pallas_call(
        functools.partial(_route_kernel, tm=tm),
        grid=(Bn, nt),
        in_specs=[pl.BlockSpec((1, N_EXPERTS, tm), lambda b, i: (b, 0, i)),
                  pl.BlockSpec((N_EXPERTS, 1), lambda b, i: (0, 0))],
        out_specs=[tok, tok, tok, pl.BlockSpec((N_EXPERTS, V7X_LANES), lambda b, i: (0, 0))],
        out_shape=[jax.ShapeDtypeStruct((TOP_K, T), jnp.int32), jax.ShapeDtypeStruct((TOP_K, T), F32),
                   jax.ShapeDtypeStruct((TOP_K, T), jnp.int32),
                   jax.ShapeDtypeStruct((N_EXPERTS, V7X_LANES), jnp.int32)],
        scratch_shapes=[pltpu.VMEM((N_EXPERTS, 1), F32)],
        compiler_params=_cparams("arbitrary", "arbitrary"),
        name="route",
    )(scores_t, e_bias.reshape(N_EXPERTS, 1))


def _dest_kernel(start_ref, e_ref, r_ref, o_ref):
    e = e_ref[...]
    acc = r_ref[...]
    for ex in range(N_EXPERTS):
        acc = acc + jnp.where(e == ex, start_ref[ex], 0)
    o_ref[0] = acc


def _dest_rows(pad_start, eidx, rank, tt):
    K_, T = eidx.shape
    grid_spec = pltpu.PrefetchScalarGridSpec(
        num_scalar_prefetch=1,
        grid=(T // tt,),
        in_specs=[pl.BlockSpec((K_, tt), lambda i, st: (0, i)), pl.BlockSpec((K_, tt), lambda i, st: (0, i))],
        out_specs=pl.BlockSpec((1, K_, tt), lambda i, st: (i, 0, 0)),
    )
    return pl.pallas_call(
        _dest_kernel,
        grid_spec=grid_spec,
        out_shape=jax.ShapeDtypeStruct((T // tt, K_, tt), jnp.int32),
        compiler_params=_cparams("arbitrary"),
        name="dest_rows",
    )(pad_start, eidx, rank)


def _expert_kernel(blk_e_ref, n_used_ref, n_valid_ref, x_ref, w1_ref, w3_ref, w2_ref, o_ref, w1b, w3b, w2b):
    i = pl.program_id(0)

    @pl.when((i == 0) | (blk_e_ref[i] != blk_e_ref[jnp.maximum(i - 1, 0)]))
    def _():
        w1b[...] = w1_ref[0].astype(BF16)
        w3b[...] = w3_ref[0].astype(BF16)
        w2b[...] = w2_ref[0].astype(BF16)

    @pl.when(i < n_used_ref[0])
    def _():
        row = lax.broadcasted_iota(jnp.int32, x_ref.shape, 0)
        x_lo, x_hi = _unpack_bf16_pair(jnp.where(row < n_valid_ref[i], x_ref[...], 0))
        x_lo, x_hi = x_lo.astype(BF16), x_hi.astype(BF16)
        half = x_lo.shape[1]

        def up(wb):
            return (jnp.dot(x_lo, wb[:half, :], preferred_element_type=F32)
                    + jnp.dot(x_hi, wb[half:, :], preferred_element_type=F32))

        t = _silu(up(w1b)) * up(w3b)
        o_ref[...] = _pack_bf16_pair(jnp.dot(t.astype(BF16), w2b[...], preferred_element_type=F32))


def _experts(blk_e, n_used, n_valid, xs, w1, w3, w2, layer):
    P, DP = xs.shape
    EB = EXPERT_BLOCK
    n_blocks = blk_e.shape[0]
    D, F = w1.shape[2], w1.shape[3]
    rows = pl.BlockSpec((EB, DP), lambda i, be, nu, nv: (jnp.minimum(i, nu[0] - 1), 0))
    grid_spec = pltpu.PrefetchScalarGridSpec(
        num_scalar_prefetch=3,
        grid=(n_blocks,),
        in_specs=[
            rows,
            pl.BlockSpec((None, 1, D, F), lambda i, be, nu, nv: (layer, be[i], 0, 0)),
            pl.BlockSpec((None, 1, D, F), lambda i, be, nu, nv: (layer, be[i], 0, 0)),
            pl.BlockSpec((None, 1, F, D), lambda i, be, nu, nv: (layer, be[i], 0, 0)),
        ],
        out_specs=rows,
        scratch_shapes=[pltpu.VMEM((D, F), BF16), pltpu.VMEM((D, F), BF16), pltpu.VMEM((F, D), BF16)],
    )
    return pl.pallas_call(
        _expert_kernel,
        grid_spec=grid_spec,
        out_shape=jax.ShapeDtypeStruct((P, DP), jnp.int32),
        compiler_params=_cparams("arbitrary"),
        name="experts",
    )(blk_e, n_used, n_valid, xs, w1, w3, w2)


def _block_layout(counts, n_blocks):
    EB = EXPERT_BLOCK
    padded = (counts + EB - 1) // EB * EB
    ex = jnp.arange(N_EXPERTS)
    pad_end = jnp.sum(jnp.where(ex[:, None] <= ex[None, :], padded[:, None], 0), axis=0)
    pad_start = pad_end - padded
    blk_row = (jnp.arange(n_blocks) * EB)[:, None]
    blk_e = jnp.minimum(jnp.sum((pad_end[None, :] <= blk_row).astype(jnp.int32), axis=1), N_EXPERTS - 1)
    n_used = (jnp.sum(padded) // EB).astype(jnp.int32).reshape(1)
    mine = (pad_start[None, :] <= blk_row) & (blk_row < pad_end[None, :])
    n_valid = jnp.sum(jnp.where(mine, jnp.clip(counts[None, :] - (blk_row - pad_start[None, :]), 0, EB), 0), axis=1)
    return pad_start.astype(jnp.int32), blk_e.astype(jnp.int32), n_used, n_valid.astype(jnp.int32)


SC_GATHER_ROWS = 64


def _sc_gather_rows(table, idx):
    info = plsc.get_sparse_core_info()
    nc, ns = info.num_cores, info.num_subcores
    M = idx.shape[0]
    W = table.shape[1]
    b = SC_GATHER_ROWS
    per_worker = M // (nc * ns)
    steps = per_worker // b
    assert per_worker * nc * ns == M and steps * b == per_worker and steps % 2 == 0
    mesh = plsc.VectorSubcoreMesh(core_axis_name="c", subcore_axis_name="s")

    @functools.partial(
        pl.kernel, mesh=mesh,
        out_type=jax.ShapeDtypeStruct((M, W), table.dtype),
        scratch_types=[pltpu.VMEM((2, b), jnp.int32), pltpu.VMEM((2, b, W), table.dtype),
                       pltpu.SemaphoreType.DMA, pltpu.SemaphoreType.DMA],
        name="sc_gather_rows",
    )
    def gather(table_hbm, idx_hbm, out_hbm, idx_v, rows_v, sem0, sem1):
        wid = lax.axis_index("s") * nc + lax.axis_index("c")
        sems = (sem0, sem1)

        def base(s):
            return pl.multiple_of(wid * per_worker + s * b, b)

        def gather_copy(slot):
            return pltpu.make_async_copy(table_hbm.at[idx_v.at[slot]], rows_v.at[slot], sems[slot])

        def start(s, slot):
            pltpu.sync_copy(idx_hbm.at[pl.ds(base(s), b)], idx_v.at[slot])
            gather_copy(slot).start()

        def finish(s, slot):
            gather_copy(slot).wait()
            pltpu.sync_copy(rows_v.at[slot], out_hbm.at[pl.ds(base(s), b)])

        start(0, 0)

        @pl.loop(0, steps, step=2)
        def _(s):
            start(s + 1, 1)
            finish(s, 0)

            @pl.when(s + 2 < steps)
            def _():
                start(s + 2, 0)

            finish(s + 1, 1)

    return gather(table, idx)


def _sc_scatter_rows(rows, idx, n_out):
    info = plsc.get_sparse_core_info()
    nc, ns = info.num_cores, info.num_subcores
    T, W = rows.shape
    G, K_, b = idx.shape
    steps = G // (nc * ns)
    assert steps * nc * ns == G and G * b == T
    mesh = plsc.VectorSubcoreMesh(core_axis_name="c", subcore_axis_name="s")

    @functools.partial(
        pl.kernel, mesh=mesh,
        out_type=jax.ShapeDtypeStruct((n_out, W), rows.dtype),
        scratch_types=[pltpu.VMEM((K_, b), jnp.int32), pltpu.VMEM((b, W), rows.dtype), pltpu.SemaphoreType.DMA],
        name="sc_scatter_rows",
    )
    def scatter(rows_hbm, idx_hbm, out_hbm, idx_v, rows_v, sem):
        wid = lax.axis_index("s") * nc + lax.axis_index("c")

        @pl.loop(0, steps)
        def _(s):
            g = wid * steps + s
            pltpu.sync_copy(idx_hbm.at[g], idx_v)
            pltpu.sync_copy(rows_hbm.at[pl.ds(pl.multiple_of(g * b, b), b)], rows_v)
            copies = [pltpu.async_copy(rows_v, out_hbm.at[idx_v.at[k]], sem) for k in range(K_)]
            for cp in copies:
                cp.wait()

    return scatter(rows, idx)


def _combine_dense_kernel(rows_ref, w_ref, x_ref, shared_ref, gpost_ref, g2_ref, o_ref):
    w = w_ref[...]
    tt, half = rows_ref.shape[1], rows_ref.shape[2]
    y_lo = jnp.zeros((tt, half), F32)
    y_hi = jnp.zeros((tt, half), F32)
    for k in range(TOP_K):
        lo, hi = _unpack_bf16_pair(rows_ref[k])
        y_lo = y_lo + w[:, k:k + 1] * lo
        y_hi = y_hi + w[:, k:k + 1] * hi
    y = shared_ref[0] + jnp.concatenate([y_lo, y_hi], axis=1)
    o_ref[0] = x_ref[0] + g2_ref[0] * (_rms(y) * gpost_ref[...])


def _combine_dense(rows, w_tok, x, shared, gpost, g2, tt):
    Bn, S, D = x.shape
    K_, T, DP = rows.shape
    nt = S // tt
    seq = pl.BlockSpec((1, tt, D), lambda b, i: (b, i, 0))
    return pl.pallas_call(
        _combine_dense_kernel,
        grid=(Bn, nt),
        in_specs=[pl.BlockSpec((K_, tt, DP), lambda b, i: (0, b * nt + i, 0)),
                  pl.BlockSpec((tt, K_), lambda b, i: (b * nt + i, 0)), seq, seq,
                  pl.BlockSpec((1, D), lambda b, i: (0, 0)), pl.BlockSpec((1, 1, D), lambda b, i: (b, 0, 0))],
        out_specs=seq,
        out_shape=jax.ShapeDtypeStruct((Bn, S, D), F32),
        compiler_params=_cparams("arbitrary", "arbitrary"),
        name="combine_dense",
    )(rows, w_tok, x, shared, gpost.reshape(1, D), g2)


def kernel(x, c, w_ada, b_ada, norm_pre_mix, norm_post_mix, norm_pre_ffn, norm_post_ffn, w_in, w_out, rel_bias_table, diff_lambda, diff_subln, rwkv_mu, rwkv_w0, rwkv_w2, rwkv_a0, rwkv_a2, rwkv_g2, rwkv_k_k, rwkv_k_a, rwkv_r_k, rwkv_lnx_g, rwkv_lnx_b, gmlp_ln_g, gmlp_ln_b, gmlp_w_s, gmlp_b_s, router_w, router_bias, exp_w1, exp_w3, exp_w2, shared_w1, shared_w3, shared_w2):
    Bn, S, D = x.shape
    depth = w_ada.shape[0]
    tm = min(256, S)
    tm_wide = min(512, S)
    tq = min(512, S // 2)
    t_rwkv = min(512, S)

    mod = _adaln(c, w_ada, b_ada)
    band_t = _attn_band(rel_bias_table, tq)
    zpad = jnp.zeros((B_DECAY_LORA, B_WIDTH), F32)
    for l in range(depth):
        sh1, sc1, g1, sh2, sc2, g2 = [m.reshape(Bn, 1, D) for m in jnp.split(mod[l], 6, axis=-1)]
        w_in_b = w_in[l].astype(BF16)
        pa, vt, pbc = _inproj(x, norm_pre_mix[l], sc1, sh1, w_in_b[:, :2 * A_WIDTH],
                              jnp.transpose(w_in_b[:, 2 * A_WIDTH:A_COLS]), w_in_b[:, A_COLS:], tm)
        lambda_init = 0.8 - 0.6 * math.exp(-0.3 * l)
        ya = _diff_attention(pa, vt, band_t, diff_lambda[l], diff_subln[l], lambda_init, tq)
        prep = _rwkv_prep(pbc, rwkv_mu[l], rwkv_w0[l], jnp.concatenate([rwkv_w2[l], zpad], axis=0),
                          rwkv_a0[l], jnp.concatenate([zpad, rwkv_a2[l]], axis=0), rwkv_g2[l],
                          rwkv_k_k[l], rwkv_k_a[l], rwkv_r_k[l].reshape(-1), t_rwkv)
        yb = _rwkv_scan(*prep, rwkv_lnx_g[l], rwkv_lnx_b[l], t_rwkv)
        yc = _gmlp(pbc, gmlp_ln_g[l], gmlp_ln_b[l], gmlp_w_s[l], gmlp_b_s[l], tm_wide)

        w_out_b = w_out[l].astype(BF16)
        wr_t = jnp.pad(jnp.transpose(router_w[l]), ((0, V7X_LANES - N_EXPERTS), (0, 0)))
        x, h, scores_t = _mid(
            ya, yb, yc, x, w_out_b[:A_WIDTH], w_out_b[A_WIDTH:A_WIDTH + B_WIDTH], w_out_b[A_WIDTH + B_WIDTH:],
            norm_post_mix[l], g1, norm_pre_ffn[l], sc2, sh2, wr_t, tm_wide)

        T = Bn * S
        n_blocks = -(-T * TOP_K // EXPERT_BLOCK) + N_EXPERTS
        eidx, wgt, rank, cnt = _route(scores_t, router_bias[l], tm)
        pad_start, blk_e, n_used, n_valid = _block_layout(cnt[:, 0], n_blocks)
        dest = _dest_rows(pad_start, eidx, rank, tm)
        b = SC_GATHER_ROWS
        dest_sc = jnp.transpose(dest.reshape(T // tm, TOP_K, tm // b, b), (0, 2, 1, 3)).reshape(T // b, TOP_K, b)
        xs = _sc_scatter_rows(h.reshape(T, D // 2), dest_sc, n_blocks * EXPERT_BLOCK)
        shared = _shared_expert(h, shared_w1[l].astype(BF16), shared_w3[l].astype(BF16),
                                shared_w2[l].astype(BF16), tm_wide)
        ys = _experts(blk_e, n_used, n_valid, xs, exp_w1, exp_w3, exp_w2, l)
        dest_kt = jnp.transpose(dest, (1, 0, 2)).reshape(TOP_K * T)
        rows = _sc_gather_rows(ys, dest_kt).reshape(TOP_K, T, D // 2)
        x = _combine_dense(rows, jnp.transpose(wgt), x, shared, norm_post_ffn[l], g2, tm)
    return x
```

```python
import functools
import math

import jax
import jax.numpy as jnp
from jax import lax
from jax.experimental import pallas as pl
from jax.experimental.pallas import tpu as pltpu
from jax.experimental.pallas import tpu_sc as plsc

F32 = jnp.float32
BF16 = jnp.bfloat16

A_HEADS = 4
A_QK_DIM = 64
A_HEAD_W = 2 * A_QK_DIM
A_WIDTH = A_HEADS * A_HEAD_W
N_BUCKETS = 32
MAX_DISTANCE = 128
B_HEADS = 4
B_HEAD_DIM = 64
B_WIDTH = B_HEADS * B_HEAD_DIM
B_DECAY_LORA = 64
B_AAA_LORA = 64
B_GATE_LORA = 128
B_LNX_EPS = 64e-5
C_GROUPS = 4
C_GROUP_DIM = 64
C_WIDTH = C_GROUPS * C_GROUP_DIM
CHUNK = 128
A_COLS = 3 * A_WIDTH
B_COLS = 3 * B_WIDTH + B_DECAY_LORA + B_AAA_LORA + B_GATE_LORA
C_COLS = 2 * C_WIDTH
N_EXPERTS = 64
TOP_K = 8
N_GROUPS = 8
TOPK_GROUPS = 4
EXPERTS_PER_GROUP = N_EXPERTS // N_GROUPS
ROUTED_SCALE = 2.5
EXPERT_BLOCK = 512
RMS_EPS = 1e-6
LN_EPS = 1e-5
NEG_BIG = -1e30

V7X_LANES = 128
BF16_SUBLANES = 16
V7X_VMEM_LIMIT_BYTES = 56 * 1024 * 1024
RWKV_CHUNK = 64
RWKV_GROUP = 8

NN = (((1,), (0,)), ((), ()))
NT = (((1,), (1,)), ((), ()))
TN = (((0,), (0,)), ((), ()))


def _cparams(*sem):
    return pltpu.CompilerParams(dimension_semantics=sem, vmem_limit_bytes=V7X_VMEM_LIMIT_BYTES)


def _mm(a, b, dims=NN):
    return lax.dot_general(a.astype(BF16), b.astype(BF16), dims, preferred_element_type=F32)


def _split(a):
    hi = a.astype(BF16)
    lo = (a - hi.astype(F32)).astype(BF16)
    return hi, lo


def _mm3(a, b, dims=NN):
    ah, al = _split(a)
    bh, bl = _split(b)
    d = lambda x, y: lax.dot_general(x, y, dims, preferred_element_type=F32)
    return d(ah, bh) + d(ah, bl) + d(al, bh)


def _mm2(a, b_exact, dims=NN):
    ah, al = _split(a)
    d = lambda x: lax.dot_general(x, b_exact, dims, preferred_element_type=F32)
    return d(ah) + d(al)


def _pack_bf16_pair(x):
    n = x.shape[1] // 2
    bits = lax.bitcast_convert_type(x.astype(BF16).astype(F32), jnp.int32)
    return ((bits[:, :n] >> 16) & 0xFFFF) | bits[:, n:]


def _unpack_bf16_pair(u):
    lo = lax.bitcast_convert_type(u << 16, F32)
    hi = lax.bitcast_convert_type(u & jnp.int32(-65536), F32)
    return lo, hi


def _rms(x, eps=RMS_EPS):
    return x * lax.rsqrt(jnp.mean(x * x, axis=-1, keepdims=True) + eps)


def _sigmoid(x):
    return 1.0 / (1.0 + jnp.exp(-x))


def _silu(x):
    return x * _sigmoid(x)


def _adaln_kernel(c_ref, w_ref, b_ref, o_ref):
    c = c_ref[...]
    o_ref[0] = _mm3(_silu(c), w_ref[0]) + b_ref[0]


def _adaln(c, w_ada, b_ada):
    L, D, N = w_ada.shape
    Bn = c.shape[0]
    tn = min(N, 1536)
    return pl.pallas_call(
        _adaln_kernel,
        grid=(L, N // tn),
        in_specs=[
            pl.BlockSpec((Bn, D), lambda l, j: (0, 0)),
            pl.BlockSpec((1, D, tn), lambda l, j: (l, 0, j)),
            pl.BlockSpec((1, 1, tn), lambda l, j: (l, 0, j)),
        ],
        out_specs=pl.BlockSpec((1, Bn, tn), lambda l, j: (l, 0, j)),
        out_shape=jax.ShapeDtypeStruct((L, Bn, N), F32),
        compiler_params=_cparams("arbitrary", "arbitrary"),
        name="adaln",
    )(c, w_ada, b_ada.reshape(L, 1, N))


def _inproj_kernel(x_ref, g_ref, sc_ref, sh_ref, wa_ref, wvt_ref, wbc_ref, oa_ref, ovt_ref, obc_ref):
    x = x_ref[0]
    h = _rms(x) * g_ref[...] * (1.0 + sc_ref[0]) + sh_ref[0]
    hb = h.astype(BF16)
    oa_ref[0] = jnp.dot(hb, wa_ref[...], preferred_element_type=F32).astype(BF16)
    ovt_ref[0] = lax.dot_general(wvt_ref[...], hb, NT, preferred_element_type=F32).astype(BF16)
    obc_ref[0] = jnp.dot(hb, wbc_ref[...], preferred_element_type=F32)


def _inproj(x, g, sc, sh, wa, wvt, wbc, tm):
    Bn, S, D = x.shape
    na, nv, nbc = wa.shape[1], wvt.shape[0], wbc.shape[1]
    return pl.pallas_call(
        _inproj_kernel,
        grid=(Bn, S // tm),
        in_specs=[
            pl.BlockSpec((1, tm, D), lambda b, i: (b, i, 0)),
            pl.BlockSpec((1, D), lambda b, i: (0, 0)),
            pl.BlockSpec((1, 1, D), lambda b, i: (b, 0, 0)),
            pl.BlockSpec((1, 1, D), lambda b, i: (b, 0, 0)),
            pl.BlockSpec((D, na), lambda b, i: (0, 0)),
            pl.BlockSpec((nv, D), lambda b, i: (0, 0)),
            pl.BlockSpec((D, nbc), lambda b, i: (0, 0)),
        ],
        out_specs=[
            pl.BlockSpec((1, tm, na), lambda b, i: (b, i, 0)),
            pl.BlockSpec((1, nv, tm), lambda b, i: (b, 0, i)),
            pl.BlockSpec((1, tm, nbc), lambda b, i: (b, i, 0)),
        ],
        out_shape=[
            jax.ShapeDtypeStruct((Bn, S, na), BF16),
            jax.ShapeDtypeStruct((Bn, nv, S), BF16),
            jax.ShapeDtypeStruct((Bn, S, nbc), F32),
        ],
        compiler_params=_cparams("arbitrary", "arbitrary"),
        name="inproj",
    )(x, g.reshape(1, D), sc, sh, wa, wvt, wbc)


def _t5_bucket(dist):
    n = jnp.maximum(dist, 0)
    max_exact = N_BUCKETS // 2
    nf = jnp.maximum(n, 1).astype(F32)
    large = max_exact + (jnp.log(nf / max_exact) / math.log(MAX_DISTANCE / max_exact)
                         * (N_BUCKETS - max_exact)).astype(jnp.int32)
    large = jnp.minimum(large, N_BUCKETS - 1)
    return jnp.where(n < max_exact, n, large)


def _attn_band(table, tq):
    far = table[N_BUCKETS - 1].astype(F32)
    H = table.shape[1]
    nb = V7X_LANES
    L = 3 * nb
    m = jnp.arange(L)
    m = jnp.where(m < nb, m, m - L)
    cache = {}

    def block(c):
        if c not in cache:
            if c - (nb - 1) >= MAX_DISTANCE:
                cache[c] = jnp.zeros((H, nb, nb), F32)
            elif c + (nb - 1) < 0:
                cache[c] = jnp.full((H, nb, nb), NEG_BIG, F32)
            else:
                dist = m + c
                vals = jnp.where(dist[None] >= 0,
                                 jnp.transpose(table[_t5_bucket(dist)].astype(F32)) - far[:, None], NEG_BIG)
                cache[c] = jnp.tile(vals, (1, nb))[:, :nb * (L - 1)].reshape(H, nb, L - 1)[:, :, :nb]
        return cache[c]

    bands = []
    for off in (0, tq):
        rows = [jnp.concatenate([block(nb * (a - b) + off) for a in range(tq // nb)], axis=2)
                for b in range(2 * tq // nb)]
        bands.append(jnp.concatenate(rows, axis=1))
    return jnp.stack(bands)


def _attn_kernel(lam_ref, q_ref, k_ref, vt_ref, band_ref, g_ref, o_ref, *, tq, lambda_init):
    i = pl.program_id(2)
    q = q_ref[0] * jnp.asarray(A_QK_DIM ** -0.5, BF16)
    lane = lax.broadcasted_iota(jnp.int32, q.shape, 1)
    zero = jnp.zeros_like(q)
    qq = jnp.concatenate([jnp.where(lane < A_QK_DIM, q, zero),
                          jnp.where(lane >= A_QK_DIM, q, zero)], axis=0)

    kb0 = pl.multiple_of(jnp.maximum(i - 1, 0) * tq, tq)
    kb = k_ref[0, pl.ds(kb0, 2 * tq), :]
    band = band_ref[0, 0]
    s = lax.dot_general(kb, qq, NT, preferred_element_type=F32) + jnp.concatenate([band, band], axis=1)
    m = jnp.max(s, axis=0, keepdims=True)

    def weighted_values(keys, p):
        vt = jnp.concatenate([vt_ref[0, :, keys], jnp.ones((BF16_SUBLANES, p.shape[0]), BF16)], axis=0)
        return jnp.dot(vt, p, preferred_element_type=F32)

    acc = weighted_values(pl.ds(kb0, 2 * tq), jnp.exp((s - m).astype(BF16)))

    n_far = jnp.maximum(i - 1, 0)

    def logits(j):
        return lax.dot_general(k_ref[0, pl.ds(pl.multiple_of(j * tq, tq), tq), :], qq, NT,
                               preferred_element_type=F32)

    def body(j, carry):
        m, acc = carry
        s = logits(j)
        m_new = jnp.maximum(m, jnp.max(s, axis=0, keepdims=True))
        alpha = jnp.exp(m - m_new)
        p = jnp.exp((s - m_new).astype(BF16))
        acc = alpha * acc + weighted_values(pl.ds(pl.multiple_of(j * tq, tq), tq), p)
        return m_new, acc

    m, acc = lax.fori_loop(0, n_far, body, (m, acc))

    lp = lam_ref[...]
    lam = (jnp.exp(jnp.sum(lp[0:1] * lp[1:2], axis=-1, keepdims=True))
           - jnp.exp(jnp.sum(lp[2:3] * lp[3:4], axis=-1, keepdims=True)) + lambda_init)
    o = acc[:A_HEAD_W] / acc[A_HEAD_W:A_HEAD_W + 1]
    o = o[:, :tq] - lam * o[:, tq:]
    o = o * lax.rsqrt(jnp.mean(o * o, axis=0, keepdims=True) + RMS_EPS) * g_ref[...] * (1.0 - lambda_init)
    o_ref[0] = jnp.transpose(o)


def _diff_attention(pa, vt, band_t, lam_par, subln_g, lambda_init, tq):
    Bn, S, _ = pa.shape
    W = A_HEAD_W
    kern = functools.partial(_attn_kernel, tq=tq, lambda_init=lambda_init)
    return pl.pallas_call(
        kern,
        grid=(Bn, A_HEADS, S // tq),
        in_specs=[
            pl.BlockSpec((4, A_QK_DIM), lambda b, h, i: (0, 0)),
            pl.BlockSpec((1, tq, W), lambda b, h, i: (b, i, h)),
            pl.BlockSpec((1, S, W), lambda b, h, i: (b, 0, A_HEADS + h)),
            pl.BlockSpec((1, W, S), lambda b, h, i: (b, h, 0)),
            pl.BlockSpec((1, 1, 2 * tq, tq), lambda b, h, i: (jnp.minimum(i, 1), h, 0, 0)),
            pl.BlockSpec((W, 1), lambda b, h, i: (0, 0)),
        ],
        out_specs=pl.BlockSpec((1, tq, W), lambda b, h, i: (b, i, h)),
        out_shape=jax.ShapeDtypeStruct((Bn, S, A_WIDTH), F32),
        compiler_params=_cparams("arbitrary", "arbitrary", "arbitrary"),
        name="diff_attn",
    )(lam_par, pa, pa, vt, band_t, subln_g.reshape(W, 1))


def _head_ones(n):
    r = lax.broadcasted_iota(jnp.int32, (n, n), 0) // B_HEAD_DIM
    c = lax.broadcasted_iota(jnp.int32, (n, n), 1) // B_HEAD_DIM
    return (r == c).astype(BF16)


def _rwkv_prep_kernel(pb_ref, prev_ref, mu_ref, w0_ref, w2_ref, a0_ref, a2_ref, g2_ref,
                      kk_ref, ka_ref, rk_ref,
                      rt_ref, at_ref, kt_ref, bt_ref, v_ref, wc_ref, bonus_ref, g_ref, *, tm):
    i = pl.program_id(1)
    C = RWKV_CHUNK
    x = pb_ref[0]
    row = lax.broadcasted_iota(jnp.int32, x.shape, 0)
    last = prev_ref[0, 7:8, :] * (i > 0).astype(F32)
    prev = jnp.where(row == 0, last, pltpu.roll(x, 1, 0))
    p = x + (prev - x) * mu_ref[...]
    o1, o2, o3 = B_WIDTH, 2 * B_WIDTH, 3 * B_WIDTH
    r, k, v = p[:, :o1], p[:, o1:o2], p[:, o2:o3]
    lora = p[:, o3:o3 + B_DECAY_LORA + B_AAA_LORA]
    gd = p[:, o3 + B_DECAY_LORA + B_AAA_LORA:]

    z = -(w0_ref[...] + _mm3(jnp.tanh(lora), w2_ref[...]))
    softplus = jnp.maximum(z, 0.0) + jnp.log(1.0 + jnp.exp(-jnp.abs(z)))
    logw = -jnp.exp(-softplus - 0.5)
    a = _sigmoid(a0_ref[...] + _mm3(lora, a2_ref[...]))
    g_ref[0] = _mm3(_sigmoid(gd), g2_ref[...])

    ones = _head_ones(B_WIDTH)
    kk = k * kk_ref[...]
    kk = kk * lax.rsqrt(jnp.maximum(_mm2(kk * kk, ones), 1e-24))
    k2 = k * (1.0 + (a - 1.0) * ka_ref[...])
    bonus_ref[0] = _mm2(r * k2 * rk_ref[...], ones) * v

    t_in = lax.broadcasted_iota(jnp.int32, (tm, B_WIDTH), 0) % C
    cum = logw
    sh = 1
    while sh < C:
        cum = cum + jnp.where(t_in >= sh, pltpu.roll(cum, sh, 0), 0.0)
        sh *= 2
    n = tm // C
    wc_ref[0] = jnp.exp(jnp.sum(logw.reshape(n, C, B_WIDTH), axis=1))
    e_pos = jnp.exp(cum)
    e_neg = jnp.exp(-cum)
    rt_ref[0] = r * e_pos
    at_ref[0] = -kk * jnp.exp(cum - logw)
    kt_ref[0] = k2 * e_neg
    bt_ref[0] = kk * a * e_neg
    v_ref[0] = v


def _rwkv_prep(pbc, mu, w0, w2p, a0, a2p, g2, k_k, k_a, r_k, tm):
    Bn, S, _ = pbc.shape
    W = B_WIDTH
    nl = B_DECAY_LORA + B_AAA_LORA
    row = lambda a: a.reshape(1, -1)
    full = lambda shp: pl.BlockSpec(shp, lambda b, i: (0,) * len(shp))
    seq = pl.BlockSpec((1, tm, W), lambda b, i: (b, i, 0))
    seq_shape = jax.ShapeDtypeStruct((Bn, S, W), F32)
    n = tm // RWKV_CHUNK
    return pl.pallas_call(
        functools.partial(_rwkv_prep_kernel, tm=tm),
        grid=(Bn, S // tm),
        in_specs=[
            pl.BlockSpec((1, tm, B_COLS), lambda b, i: (b, i, 0)),
            pl.BlockSpec((1, 8, B_COLS), lambda b, i: (b, jnp.maximum(i * (tm // 8) - 1, 0), 0)),
            full((1, B_COLS)), full((1, W)), full((nl, W)), full((1, W)), full((nl, W)),
            full((B_GATE_LORA, W)), full((1, W)), full((1, W)), full((1, W)),
        ],
        out_specs=[seq, seq, seq, seq, seq,
                   pl.BlockSpec((1, n, W), lambda b, i: (b, i, 0)), seq, seq],
        out_shape=[seq_shape] * 5 + [jax.ShapeDtypeStruct((Bn, S // RWKV_CHUNK, W), F32)] + [seq_shape] * 2,
        compiler_params=_cparams("arbitrary", "arbitrary"),
        name="rwkv_prep",
    )(pbc, pbc, row(mu), row(w0), w2p, row(a0), a2p, g2, row(k_k), row(k_a), row(r_k))


def _rwkv_scan_kernel(rt_ref, at_ref, kt_ref, bt_ref, v_ref, wc_ref, bonus_ref, g_ref,
                      lng_ref, lnb_ref, o_ref, state, *, tt):
    C = RWKV_CHUNK
    W = B_WIDTH

    @pl.when(pl.program_id(1) == 0)
    def _():
        state[...] = jnp.zeros_like(state)

    lane_head = lax.broadcasted_iota(jnp.int32, (C, W), 1) // B_HEAD_DIM
    tt_i = lax.broadcasted_iota(jnp.int32, (C, W), 0)
    ss_i = lax.broadcasted_iota(jnp.int32, (C, W), 1) % C
    strict = tt_i > ss_i
    incl = tt_i >= ss_i
    eye = (tt_i == ss_i).astype(F32)
    ones = _head_ones(W)
    bd_mask = ones.astype(F32)

    head_mask = [(lane_head == h).astype(BF16) for h in range(B_HEADS)]

    def bd_split(x):
        xb = x.astype(BF16)
        return jnp.concatenate([xb * mk for mk in head_mask], axis=0)

    def mm_bd(a, b_bd, dims=NN):
        return lax.dot_general(a.astype(BF16), b_bd, dims, preferred_element_type=F32)

    def state_free(gi, nb):
        G = range(RWKV_GROUP)
        sls = [pl.ds(pl.multiple_of((gi * RWKV_GROUP + j) * C, C), C) for j in G]
        rt = [rt_ref[nb, sl, :] for sl in sls]
        at = [at_ref[nb, sl, :] for sl in sls]
        kt = [kt_ref[nb, sl, :] for sl in sls]
        bt = [bt_ref[nb, sl, :] for sl in sls]
        v = [v_ref[nb, sl, :] for sl in sls]
        wc = [wc_ref[nb, pl.ds(gi * RWKV_GROUP + j, 1), :] for j in G]
        ar = [jnp.concatenate([at[j], rt[j]], axis=0) for j in G]
        bdb = [bd_split(bt[j]) for j in G]
        bdk = [bd_split(kt[j]) for j in G]
        a_b = [mm_bd(ar[j], bdb[j], NT) for j in G]
        a_k = [mm_bd(ar[j], bdk[j], NT) for j in G]
        lo = [jnp.where(strict, a_b[j][:C], 0.0) for j in G]
        a_ak = [jnp.where(strict, a_k[j][:C], 0.0) for j in G]
        a_rb = [jnp.where(incl, a_b[j][C:], 0.0) for j in G]
        a_rk = [jnp.where(incl, a_k[j][C:], 0.0) for j in G]
        pw = lo
        tinv = [eye + lo[j] for j in G]
        bdp = [bd_split(pw[j]) for j in G]
        span = 2
        while span < C:
            pw = [mm_bd(pw[j], bdp[j]) for j in G]
            bdp = [bd_split(pw[j]) for j in G]
            tinv = [tinv[j] + mm_bd(tinv[j], bdp[j]) for j in G]
            span *= 2
        bdv = [bd_split(v[j]) for j in G]
        bda = [bd_split(at[j]) for j in G]
        abar = [mm_bd(tinv[j], bda[j]) for j in G]
        akv = [bd_split(mm_bd(a_ak[j], bdv[j])) for j in G]
        u0 = [mm_bd(tinv[j], akv[j]) for j in G]
        y0 = [mm_bd(a_rk[j], bdv[j]) for j in G]
        kv = [_mm(v[j], kt[j] * wc[j], TN) * bd_mask for j in G]
        return [(jnp.concatenate([abar[j], rt[j]], axis=0), u0[j], y0[j], a_rb[j], bt[j] * wc[j], kv[j], wc[j])
                for j in G]

    def group(gi, carry):
        seqs = range(rt_ref.shape[0])
        pre = [state_free(gi, nb) for nb in seqs]
        s = [state[nb] for nb in seqs]
        ys = [[] for _ in seqs]
        for j in range(RWKV_GROUP):
            for nb in seqs:
                abar_rt, u0, y0, a_rb, btw, kv, wc = pre[nb][j]
                a_s = _mm(abar_rt, s[nb], NT)
                u = a_s[:C] + u0
                ys[nb].append(a_s[C:] + y0 + mm_bd(a_rb, bd_split(u)))
                s[nb] = s[nb] * wc + _mm(u, btw, TN) * bd_mask + kv
        sl = pl.ds(pl.multiple_of(gi * (RWKV_GROUP * C), RWKV_GROUP * C), RWKV_GROUP * C)
        for nb in seqs:
            state[nb] = s[nb]
            y = jnp.concatenate(ys[nb], axis=0)
            mean = _mm2(y, ones) * (1.0 / B_HEAD_DIM)
            d = y - mean
            var = _mm2(d * d, ones) * (1.0 / B_HEAD_DIM)
            yn = d * lax.rsqrt(var + B_LNX_EPS) * lng_ref[...] + lnb_ref[...]
            o_ref[nb, sl, :] = (yn + bonus_ref[nb, sl, :]) * g_ref[nb, sl, :]
        return carry

    lax.fori_loop(0, tt // (RWKV_GROUP * C), group, 0)


def _rwkv_scan(rt, at, kt, bt, v, wc, bonus, g, lnx_g, lnx_b, tt):
    Bn, S, W = rt.shape
    n = tt // RWKV_CHUNK
    nseq = 2 if Bn % 2 == 0 else 1
    seq = pl.BlockSpec((nseq, tt, W), lambda b, i: (b, i, 0))
    vec = pl.BlockSpec((1, W), lambda b, i: (0, 0))
    return pl.pallas_call(
        functools.partial(_rwkv_scan_kernel, tt=tt),
        grid=(Bn // nseq, S // tt),
        in_specs=[seq, seq, seq, seq, seq, pl.BlockSpec((nseq, n, W), lambda b, i: (b, i, 0)), seq, seq, vec, vec],
        out_specs=seq,
        out_shape=jax.ShapeDtypeStruct((Bn, S, W), F32),
        scratch_shapes=[pltpu.VMEM((nseq, B_HEADS * B_HEAD_DIM, W), F32)],
        compiler_params=_cparams("arbitrary", "arbitrary"),
        name="rwkv_scan",
    )(rt, at, kt, bt, v, wc, bonus, g, lnx_g.reshape(1, W), lnx_b.reshape(1, W))


def _gmlp_kernel(pc_ref, lng_ref, lnb_ref, ws_ref, bs_ref, o_ref, *, tm):
    x = pc_ref[0]
    z = x * (0.5 * (1.0 + jnp.tanh(math.sqrt(2.0 / math.pi) * (x + 0.044715 * (x * x * x)))))
    u, v = z[:, :C_WIDTH], z[:, C_WIDTH:]
    mu = jnp.mean(v, axis=-1, keepdims=True)
    d = v - mu
    var = jnp.mean(d * d, axis=-1, keepdims=True)
    vn = d * lax.rsqrt(var + LN_EPS) * lng_ref[...] + lnb_ref[...]
    group = lax.broadcasted_iota(jnp.int32, (CHUNK, C_WIDTH), 1) // C_GROUP_DIM
    tril = (lax.broadcasted_iota(jnp.int32, (CHUNK, CHUNK), 0)
            >= lax.broadcasted_iota(jnp.int32, (CHUNK, CHUNK), 1))
    ws = [jnp.where(tril, ws_ref[gi], 0.0).astype(BF16) for gi in range(C_GROUPS)]
    for c in range(tm // CHUNK):
        sl = slice(c * CHUNK, (c + 1) * CHUNK)
        vc = vn[sl].astype(BF16)
        sv = bs_ref[...]
        for gi in range(C_GROUPS):
            t = jnp.dot(ws[gi], vc, preferred_element_type=F32)
            sv = sv + jnp.where(group == gi, t, 0.0)
        o_ref[0, sl, :] = u[sl] * sv


def _gmlp(pbc, ln_g, ln_b, w_s, b_s, tm):
    Bn, S, _ = pbc.shape
    bs_wide = jnp.repeat(jnp.transpose(b_s), C_GROUP_DIM, axis=1)
    return pl.pallas_call(
        functools.partial(_gmlp_kernel, tm=tm),
        grid=(Bn, S // tm),
        in_specs=[
            pl.BlockSpec((1, tm, C_COLS), lambda b, i: (b, i, B_COLS // C_COLS)),
            pl.BlockSpec((1, C_WIDTH), lambda b, i: (0, 0)),
            pl.BlockSpec((1, C_WIDTH), lambda b, i: (0, 0)),
            pl.BlockSpec((C_GROUPS, CHUNK, CHUNK), lambda b, i: (0, 0, 0)),
            pl.BlockSpec((CHUNK, C_WIDTH), lambda b, i: (0, 0)),
        ],
        out_specs=pl.BlockSpec((1, tm, C_WIDTH), lambda b, i: (b, i, 0)),
        out_shape=jax.ShapeDtypeStruct((Bn, S, C_WIDTH), F32),
        compiler_params=_cparams("arbitrary", "arbitrary"),
        name="gmlp",
    )(pbc, ln_g.reshape(1, -1), ln_b.reshape(1, -1), w_s, bs_wide)


def _mid_kernel(ya_ref, yb_ref, yc_ref, x_ref, woa_ref, wob_ref, woc_ref, gpost_ref, g1_ref,
                gpre_ref, sc_ref, sh_ref, wr_ref, xo_ref, h_ref, score_ref):
    y = (_mm(ya_ref[0], woa_ref[...]) + _mm(yb_ref[0], wob_ref[...]) + _mm(yc_ref[0], woc_ref[...]))
    xn = x_ref[0] + g1_ref[0] * (_rms(y) * gpost_ref[...])
    xo_ref[0] = xn
    h = _rms(xn) * gpre_ref[...] * (1.0 + sc_ref[0]) + sh_ref[0]
    h_ref[0] = _pack_bf16_pair(h)
    score_ref[0] = _sigmoid(_mm3(wr_ref[...], h, NT))


def _mid(ya, yb, yc, x, woa, wob, woc, gpost, g1, gpre, sc, sh, wr, tm):
    Bn, S, D = x.shape
    NR = wr.shape[0]
    seq = lambda w: pl.BlockSpec((1, tm, w), lambda b, i: (b, i, 0))
    full = lambda shp: pl.BlockSpec(shp, lambda b, i: (0,) * len(shp))
    per_b = pl.BlockSpec((1, 1, D), lambda b, i: (b, 0, 0))
    return pl.pallas_call(
        _mid_kernel,
        grid=(Bn, S // tm),
        in_specs=[seq(A_WIDTH), seq(B_WIDTH), seq(C_WIDTH), seq(D),
                  full((A_WIDTH, D)), full((B_WIDTH, D)), full((C_WIDTH, D)),
                  full((1, D)), per_b, full((1, D)), per_b, per_b,
                  full((NR, D))],
        out_specs=[seq(D), seq(D // 2), pl.BlockSpec((1, NR, tm), lambda b, i: (b, 0, i))],
        out_shape=[jax.ShapeDtypeStruct((Bn, S, D), F32), jax.ShapeDtypeStruct((Bn, S, D // 2), jnp.int32),
                   jax.ShapeDtypeStruct((Bn, NR, S), F32)],
        compiler_params=_cparams("arbitrary", "arbitrary"),
        name="mid",
    )(ya, yb, yc, x, woa, wob, woc, gpost.reshape(1, D), g1, gpre.reshape(1, D), sc, sh, wr)


def _shared_expert_kernel(h_ref, ws1_ref, ws3_ref, ws2_ref, o_ref):
    lo, hi = _unpack_bf16_pair(h_ref[0])
    hb = jnp.concatenate([lo, hi], axis=1).astype(BF16)
    t = _silu(jnp.dot(hb, ws1_ref[...], preferred_element_type=F32)) * jnp.dot(
        hb, ws3_ref[...], preferred_element_type=F32)
    o_ref[0] = jnp.dot(t.astype(BF16), ws2_ref[...], preferred_element_type=F32)


def _shared_expert(hp, ws1, ws3, ws2, tm):
    Bn, S, DP = hp.shape
    D, F = ws1.shape
    full = lambda shp: pl.BlockSpec(shp, lambda b, i: (0,) * len(shp))
    return pl.pallas_call(
        _shared_expert_kernel,
        grid=(Bn, S // tm),
        in_specs=[pl.BlockSpec((1, tm, DP), lambda b, i: (b, i, 0)), full((D, F)), full((D, F)), full((F, D))],
        out_specs=pl.BlockSpec((1, tm, D), lambda b, i: (b, i, 0)),
        out_shape=jax.ShapeDtypeStruct((Bn, S, D), F32),
        compiler_params=_cparams("arbitrary", "arbitrary"),
        name="shared_expert",
    )(hp, ws1, ws3, ws2)


def _first_argmax(vals, iota, n):
    m = jnp.max(vals, axis=0, keepdims=True)
    idx = jnp.min(jnp.where(vals == m, iota, n), axis=0, keepdims=True)
    return m, idx


def _route_kernel(sc_ref, bias_ref, e_ref, w_ref, r_ref, cnt_ref, carry, *, tm):
    @pl.when((pl.program_id(0) == 0) & (pl.program_id(1) == 0))
    def _():
        carry[...] = jnp.zeros_like(carry)

    G = EXPERTS_PER_GROUP
    s = sc_ref[0]
    biased = s + bias_ref[...]
    neg_inf = jnp.float32(-jnp.inf)
    io8 = lax.broadcasted_iota(jnp.int32, (G, tm), 0)
    gs_rows = []
    for g in range(N_GROUPS):
        blk = biased[g * G:(g + 1) * G]
        m1, i1 = _first_argmax(blk, io8, G)
        m2 = jnp.max(jnp.where(io8 == i1, neg_inf, blk), axis=0, keepdims=True)
        gs_rows.append(m1 + m2)
    gs = jnp.concatenate(gs_rows, axis=0)
    gio = lax.broadcasted_iota(jnp.int32, (N_GROUPS, tm), 0)
    gsel = jnp.zeros((N_GROUPS, tm), jnp.bool_)
    for _ in range(TOPK_GROUPS):
        _, gi = _first_argmax(gs, gio, N_GROUPS)
        pick = gio == gi
        gsel = gsel | pick
        gs = jnp.where(pick, neg_inf, gs)
    masked = jnp.concatenate(
        [jnp.where(gsel[g:g + 1], biased[g * G:(g + 1) * G], neg_inf) for g in range(N_GROUPS)], axis=0)

    eio = lax.broadcasted_iota(jnp.int32, (N_EXPERTS, tm), 0)
    picks, e_rows, s_rows = [], [], []
    for _ in range(TOP_K):
        _, ei = _first_argmax(masked, eio, N_EXPERTS)
        pick = eio == ei
        picks.append(pick)
        e_rows.append(ei)
        s_rows.append(jnp.sum(jnp.where(pick, s, 0.0), axis=0, keepdims=True))
        masked = jnp.where(pick, neg_inf, masked)
    top_s = jnp.concatenate(s_rows, axis=0)
    w_ref[...] = top_s / (jnp.sum(top_s, axis=0, keepdims=True) + 1e-20) * ROUTED_SCALE
    e_ref[...] = jnp.concatenate(e_rows, axis=0)

    sel = jnp.zeros((N_EXPERTS, tm), F32)
    for pick in picks:
        sel = sel + pick.astype(F32)
    before = (lax.broadcasted_iota(jnp.int32, (tm, tm), 0) < lax.broadcasted_iota(jnp.int32, (tm, tm), 1))
    pos = carry[...] + jnp.dot(sel.astype(BF16), before.astype(BF16), preferred_element_type=F32)
    r_ref[...] = jnp.concatenate(
        [jnp.sum(jnp.where(pick, pos, 0.0), axis=0, keepdims=True) for pick in picks], axis=0).astype(jnp.int32)
    total = carry[...] + jnp.sum(sel, axis=1, keepdims=True)
    carry[...] = total
    cnt_ref[...] = jnp.broadcast_to(total, cnt_ref.shape).astype(jnp.int32)


def _route(scores_t, e_bias, tm):
    Bn, _, S = scores_t.shape
    T = Bn * S
    nt = S // tm
    tok = pl.BlockSpec((TOP_K, tm), lambda b, i: (0, b * nt + i))
    return pl.pallas_call(
        functools.partial(_route_kernel, tm=tm),
        grid=(Bn, nt),
        in_specs=[pl.BlockSpec((1, N_EXPERTS, tm), lambda b, i: (b, 0, i)),
                  pl.BlockSpec((N_EXPERTS, 1), lambda b, i: (0, 0))],
        out_specs=[tok, tok, tok, pl.BlockSpec((N_EXPERTS, V7X_LANES), lambda b, i: (0, 0))],
        out_shape=[jax.ShapeDtypeStruct((TOP_K, T), jnp.int32), jax.ShapeDtypeStruct((TOP_K, T), F32),
                   jax.ShapeDtypeStruct((TOP_K, T), jnp.int32),
                   jax.ShapeDtypeStruct((N_EXPERTS, V7X_LANES), jnp.int32)],
        scratch_shapes=[pltpu.VMEM((N_EXPERTS, 1), F32)],
        compiler_params=_cparams("arbitrary", "arbitrary"),
        name="route",
    )(scores_t, e_bias.reshape(N_EXPERTS, 1))


def _dest_kernel(start_ref, e_ref, r_ref, o_ref):
    e = e_ref[...]
    acc = r_ref[...]
    for ex in range(N_EXPERTS):
        acc = acc + jnp.where(e == ex, start_ref[ex], 0)
    o_ref[0] = acc


def _dest_rows(pad_start, eidx, rank, tt):
    K_, T = eidx.shape
    grid_spec = pltpu.PrefetchScalarGridSpec(
        num_scalar_prefetch=1,
        grid=(T // tt,),
        in_specs=[pl.BlockSpec((K_, tt), lambda i, st: (0, i)), pl.BlockSpec((K_, tt), lambda i, st: (0, i))],
        out_specs=pl.BlockSpec((1, K_, tt), lambda i, st: (i, 0, 0)),
    )
    return pl.pallas_call(
        _dest_kernel,
        grid_spec=grid_spec,
        out_shape=jax.ShapeDtypeStruct((T // tt, K_, tt), jnp.int32),
        compiler_params=_cparams("arbitrary"),
        name="dest_rows",
    )(pad_start, eidx, rank)


def _expert_kernel(blk_e_ref, n_used_ref, n_valid_ref, x_ref, w1_ref, w3_ref, w2_ref, o_ref, w1b, w3b, w2b):
    i = pl.program_id(0)

    @pl.when((i == 0) | (blk_e_ref[i] != blk_e_ref[jnp.maximum(i - 1, 0)]))
    def _():
        w1b[...] = w1_ref[0].astype(BF16)
        w3b[...] = w3_ref[0].astype(BF16)
        w2b[...] = w2_ref[0].astype(BF16)

    @pl.when(i < n_used_ref[0])
    def _():
        row = lax.broadcasted_iota(jnp.int32, x_ref.shape, 0)
        x_lo, x_hi = _unpack_bf16_pair(jnp.where(row < n_valid_ref[i], x_ref[...], 0))
        x_lo, x_hi = x_lo.astype(BF16), x_hi.astype(BF16)
        half = x_lo.shape[1]

        def up(wb):
            return (jnp.dot(x_lo, wb[:half, :], preferred_element_type=F32)
                    + jnp.dot(x_hi, wb[half:, :], preferred_element_type=F32))

        t = _silu(up(w1b)) * up(w3b)
        o_ref[...] = _pack_bf16_pair(jnp.dot(t.astype(BF16), w2b[...], preferred_element_type=F32))


def _experts(blk_e, n_used, n_valid, xs, w1, w3, w2, layer):
    P, DP = xs.shape
    EB = EXPERT_BLOCK
    n_blocks = blk_e.shape[0]
    D, F = w1.shape[2], w1.shape[3]
    rows = pl.BlockSpec((EB, DP), lambda i, be, nu, nv: (jnp.minimum(i, nu[0] - 1), 0))
    grid_spec = pltpu.PrefetchScalarGridSpec(
        num_scalar_prefetch=3,
        grid=(n_blocks,),
        in_specs=[
            rows,
            pl.BlockSpec((None, 1, D, F), lambda i, be, nu, nv: (layer, be[i], 0, 0)),
            pl.BlockSpec((None, 1, D, F), lambda i, be, nu, nv: (layer, be[i], 0, 0)),
            pl.BlockSpec((None, 1, F, D), lambda i, be, nu, nv: (layer, be[i], 0, 0)),
        ],
        out_specs=rows,
        scratch_shapes=[pltpu.VMEM((D, F), BF16), pltpu.VMEM((D, F), BF16), pltpu.VMEM((F, D), BF16)],
    )
    return pl.pallas_call(
        _expert_kernel,
        grid_spec=grid_spec,
        out_shape=jax.ShapeDtypeStruct((P, DP), jnp.int32),
        compiler_params=_cparams("arbitrary"),
        name="experts",
    )(blk_e, n_used, n_valid, xs, w1, w3, w2)


def _block_layout(counts, n_blocks):
    EB = EXPERT_BLOCK
    padded = (counts + EB - 1) // EB * EB
    ex = jnp.arange(N_EXPERTS)
    pad_end = jnp.sum(jnp.where(ex[:, None] <= ex[None, :], padded[:, None], 0), axis=0)
    pad_start = pad_end - padded
    blk_row = (jnp.arange(n_blocks) * EB)[:, None]
    blk_e = jnp.minimum(jnp.sum((pad_end[None, :] <= blk_row).astype(jnp.int32), axis=1), N_EXPERTS - 1)
    n_used = (jnp.sum(padded) // EB).astype(jnp.int32).reshape(1)
    mine = (pad_start[None, :] <= blk_row) & (blk_row < pad_end[None, :])
    n_valid = jnp.sum(jnp.where(mine, jnp.clip(counts[None, :] - (blk_row - pad_start[None, :]), 0, EB), 0), axis=1)
    return pad_start.astype(jnp.int32), blk_e.astype(jnp.int32), n_used, n_valid.astype(jnp.int32)


SC_GATHER_ROWS = 64


def _sc_gather_rows(table, idx):
    info = plsc.get_sparse_core_info()
    nc, ns = info.num_cores, info.num_subcores
    M = idx.shape[0]
    W = table.shape[1]
    b = SC_GATHER_ROWS
    per_worker = M // (nc * ns)
    steps = per_worker // b
    assert per_worker * nc * ns == M and steps * b == per_worker and steps % 2 == 0
    mesh = plsc.VectorSubcoreMesh(core_axis_name="c", subcore_axis_name="s")

    @functools.partial(
        pl.kernel, mesh=mesh,
        out_type=jax.ShapeDtypeStruct((M, W), table.dtype),
        scratch_types=[pltpu.VMEM((2, b), jnp.int32), pltpu.VMEM((2, b, W), table.dtype),
                       pltpu.SemaphoreType.DMA, pltpu.SemaphoreType.DMA],
        name="sc_gather_rows",
    )
    def gather(table_hbm, idx_hbm, out_hbm, idx_v, rows_v, sem0, sem1):
        wid = lax.axis_index("s") * nc + lax.axis_index("c")
        sems = (sem0, sem1)

        def base(s):
            return pl.multiple_of(wid * per_worker + s * b, b)

        def gather_copy(slot):
            return pltpu.make_async_copy(table_hbm.at[idx_v.at[slot]], rows_v.at[slot], sems[slot])

        def start(s, slot):
            pltpu.sync_copy(idx_hbm.at[pl.ds(base(s), b)], idx_v.at[slot])
            gather_copy(slot).start()

        def finish(s, slot):
            gather_copy(slot).wait()
            pltpu.sync_copy(rows_v.at[slot], out_hbm.at[pl.ds(base(s), b)])

        start(0, 0)

        @pl.loop(0, steps, step=2)
        def _(s):
            start(s + 1, 1)
            finish(s, 0)

            @pl.when(s + 2 < steps)
            def _():
                start(s + 2, 0)

            finish(s + 1, 1)

    return gather(table, idx)


def _sc_scatter_rows(rows, idx, n_out):
    info = plsc.get_sparse_core_info()
    nc, ns = info.num_cores, info.num_subcores
    T, W = rows.shape
    G, K_, b = idx.shape
    steps = G // (nc * ns)
    assert steps * nc * ns == G and G * b == T
    mesh = plsc.VectorSubcoreMesh(core_axis_name="c", subcore_axis_name="s")

    @functools.partial(
        pl.kernel, mesh=mesh,
        out_type=jax.ShapeDtypeStruct((n_out, W), rows.dtype),
        scratch_types=[pltpu.VMEM((K_, b), jnp.int32), pltpu.VMEM((b, W), rows.dtype), pltpu.SemaphoreType.DMA],
        name="sc_scatter_rows",
    )
    def scatter(rows_hbm, idx_hbm, out_hbm, idx_v, rows_v, sem):
        wid = lax.axis_index("s") * nc + lax.axis_index("c")

        @pl.loop(0, steps)
        def _(s):
            g = wid * steps + s
            pltpu.sync_copy(idx_hbm.at[g], idx_v)
            pltpu.sync_copy(rows_hbm.at[pl.ds(pl.multiple_of(g * b, b), b)], rows_v)
            copies = [pltpu.async_copy(rows_v, out_hbm.at[idx_v.at[k]], sem) for k in range(K_)]
            for cp in copies:
                cp.wait()

    return scatter(rows, idx)


def _combine_dense_kernel(rows_ref, w_ref, x_ref, shared_ref, gpost_ref, g2_ref, o_ref):
    w = w_ref[...]
    tt, half = rows_ref.shape[1], rows_ref.shape[2]
    y_lo = jnp.zeros((tt, half), F32)
    y_hi = jnp.zeros((tt, half), F32)
    for k in range(TOP_K):
        lo, hi = _unpack_bf16_pair(rows_ref[k])
        y_lo = y_lo + w[:, k:k + 1] * lo
        y_hi = y_hi + w[:, k:k + 1] * hi
    y = shared_ref[0] + jnp.concatenate([y_lo, y_hi], axis=1)
    o_ref[0] = x_ref[0] + g2_ref[0] * (_rms(y) * gpost_ref[...])


def _combine_dense(rows, w_tok, x, shared, gpost, g2, tt):
    Bn, S, D = x.shape
    K_, T, DP = rows.shape
    nt = S // tt
    seq = pl.BlockSpec((1, tt, D), lambda b, i: (b, i, 0))
    return pl.pallas_call(
        _combine_dense_kernel,
        grid=(Bn, nt),
        in_specs=[pl.BlockSpec((K_, tt, DP), lambda b, i: (0, b * nt + i, 0)),
                  pl.BlockSpec((tt, K_), lambda b, i: (b * nt + i, 0)), seq, seq,
                  pl.BlockSpec((1, D), lambda b, i: (0, 0)), pl.BlockSpec((1, 1, D), lambda b, i: (b, 0, 0))],
        out_specs=seq,
        out_shape=jax.ShapeDtypeStruct((Bn, S, D), F32),
        compiler_params=_cparams("arbitrary", "arbitrary"),
        name="combine_dense",
    )(rows, w_tok, x, shared, gpost.reshape(1, D), g2)


def kernel(x, c, w_ada, b_ada, norm_pre_mix, norm_post_mix, norm_pre_ffn, norm_post_ffn, w_in, w_out, rel_bias_table, diff_lambda, diff_subln, rwkv_mu, rwkv_w0, rwkv_w2, rwkv_a0, rwkv_a2, rwkv_g2, rwkv_k_k, rwkv_k_a, rwkv_r_k, rwkv_lnx_g, rwkv_lnx_b, gmlp_ln_g, gmlp_ln_b, gmlp_w_s, gmlp_b_s, router_w, router_bias, exp_w1, exp_w3, exp_w2, shared_w1, shared_w3, shared_w2):
    Bn, S, D = x.shape
    depth = w_ada.shape[0]
    tm = min(256, S)
    tm_wide = min(512, S)
    tq = min(512, S // 2)
    t_rwkv = min(512, S)

    mod = _adaln(c, w_ada, b_ada)
    band_t = _attn_band(rel_bias_table, tq)
    zpad = jnp.zeros((B_DECAY_LORA, B_WIDTH), F32)
    for l in range(depth):
        sh1, sc1, g1, sh2, sc2, g2 = [m.reshape(Bn, 1, D) for m in jnp.split(mod[l], 6, axis=-1)]
        w_in_b = w_in[l].astype(BF16)
        pa, vt, pbc = _inproj(x, norm_pre_mix[l], sc1, sh1, w_in_b[:, :2 * A_WIDTH],
                              jnp.transpose(w_in_b[:, 2 * A_WIDTH:A_COLS]), w_in_b[:, A_COLS:], tm_wide)
        lambda_init = 0.8 - 0.6 * math.exp(-0.3 * l)
        ya = _diff_attention(pa, vt, band_t, diff_lambda[l], diff_subln[l], lambda_init, tq)
        prep = _rwkv_prep(pbc, rwkv_mu[l], rwkv_w0[l], jnp.concatenate([rwkv_w2[l], zpad], axis=0),
                          rwkv_a0[l], jnp.concatenate([zpad, rwkv_a2[l]], axis=0), rwkv_g2[l],
                          rwkv_k_k[l], rwkv_k_a[l], rwkv_r_k[l].reshape(-1), t_rwkv)
        yb = _rwkv_scan(*prep, rwkv_lnx_g[l], rwkv_lnx_b[l], t_rwkv)
        yc = _gmlp(pbc, gmlp_ln_g[l], gmlp_ln_b[l], gmlp_w_s[l], gmlp_b_s[l], tm_wide)

        w_out_b = w_out[l].astype(BF16)
        wr_t = jnp.pad(jnp.transpose(router_w[l]), ((0, V7X_LANES - N_EXPERTS), (0, 0)))
        x, h, scores_t = _mid(
            ya, yb, yc, x, w_out_b[:A_WIDTH], w_out_b[A_WIDTH:A_WIDTH + B_WIDTH], w_out_b[A_WIDTH + B_WIDTH:],
            norm_post_mix[l], g1, norm_pre_ffn[l], sc2, sh2, wr_t, tm_wide)

        T = Bn * S
        n_blocks = -(-T * TOP_K // EXPERT_BLOCK) + N_EXPERTS
        eidx, wgt, rank, cnt = _route(scores_t, router_bias[l], tm)
        pad_start, blk_e, n_used, n_valid = _block_layout(cnt[:, 0], n_blocks)
        dest = _dest_rows(pad_start, eidx, rank, tm)
        b = SC_GATHER_ROWS
        dest_sc = jnp.transpose(dest.reshape(T // tm, TOP_K, tm // b, b), (0, 2, 1, 3)).reshape(T // b, TOP_K, b)
        xs = _sc_scatter_rows(h.reshape(T, D // 2), dest_sc, n_blocks * EXPERT_BLOCK)
        shared = _shared_expert(h, shared_w1[l].astype(BF16), shared_w3[l].astype(BF16),
                                shared_w2[l].astype(BF16), tm_wide)
        ys = _experts(blk_e, n_used, n_valid, xs, exp_w1, exp_w3, exp_w2, l)
        dest_kt = jnp.transpose(dest, (1, 0, 2)).reshape(TOP_K * T)
        rows = _sc_gather_rows(ys, dest_kt).reshape(TOP_K, T, D // 2)
        x = _combine_dense(rows, jnp.transpose(wgt), x, shared, norm_post_ffn[l], g2, tm)
    return x
```

```python
import functools
import math

import jax
import jax.numpy as jnp
from jax import lax
from jax.experimental import pallas as pl
from jax.experimental.pallas import tpu as pltpu
from jax.experimental.pallas import tpu_sc as plsc

F32 = jnp.float32
BF16 = jnp.bfloat16

A_HEADS = 4
A_QK_DIM = 64
A_HEAD_W = 2 * A_QK_DIM
A_WIDTH = A_HEADS * A_HEAD_W
N_BUCKETS = 32
MAX_DISTANCE = 128
B_HEADS = 4
B_HEAD_DIM = 64
B_WIDTH = B_HEADS * B_HEAD_DIM
B_DECAY_LORA = 64
B_AAA_LORA = 64
B_GATE_LORA = 128
B_LNX_EPS = 64e-5
C_GROUPS = 4
C_GROUP_DIM = 64
C_WIDTH = C_GROUPS * C_GROUP_DIM
CHUNK = 128
A_COLS = 3 * A_WIDTH
B_COLS = 3 * B_WIDTH + B_DECAY_LORA + B_AAA_LORA + B_GATE_LORA
C_COLS = 2 * C_WIDTH
N_EXPERTS = 64
TOP_K = 8
N_GROUPS = 8
TOPK_GROUPS = 4
EXPERTS_PER_GROUP = N_EXPERTS // N_GROUPS
ROUTED_SCALE = 2.5
EXPERT_BLOCK = 512
RMS_EPS = 1e-6
LN_EPS = 1e-5
NEG_BIG = -1e30

V7X_LANES = 128
BF16_SUBLANES = 16
V7X_VMEM_LIMIT_BYTES = 56 * 1024 * 1024
RWKV_CHUNK = 64
RWKV_GROUP = 8

NN = (((1,), (0,)), ((), ()))
NT = (((1,), (1,)), ((), ()))
TN = (((0,), (0,)), ((), ()))


def _cparams(*sem):
    return pltpu.CompilerParams(dimension_semantics=sem, vmem_limit_bytes=V7X_VMEM_LIMIT_BYTES)


def _mm(a, b, dims=NN):
    return lax.dot_general(a.astype(BF16), b.astype(BF16), dims, preferred_element_type=F32)


def _split(a):
    hi = a.astype(BF16)
    lo = (a - hi.astype(F32)).astype(BF16)
    return hi, lo


def _mm3(a, b, dims=NN):
    ah, al = _split(a)
    bh, bl = _split(b)
    d = lambda x, y: lax.dot_general(x, y, dims, preferred_element_type=F32)
    return d(ah, bh) + d(ah, bl) + d(al, bh)


def _mm2(a, b_exact, dims=NN):
    ah, al = _split(a)
    d = lambda x: lax.dot_general(x, b_exact, dims, preferred_element_type=F32)
    return d(ah) + d(al)


def _pack_bf16_pair(x):
    n = x.shape[1] // 2
    bits = lax.bitcast_convert_type(x.astype(BF16).astype(F32), jnp.int32)
    return ((bits[:, :n] >> 16) & 0xFFFF) | bits[:, n:]


def _unpack_bf16_pair(u):
    lo = lax.bitcast_convert_type(u << 16, F32)
    hi = lax.bitcast_convert_type(u & jnp.int32(-65536), F32)
    return lo, hi


def _rms(x, eps=RMS_EPS):
    return x * lax.rsqrt(jnp.mean(x * x, axis=-1, keepdims=True) + eps)


def _sigmoid(x):
    return 1.0 / (1.0 + jnp.exp(-x))


def _silu(x):
    return x * _sigmoid(x)


def _adaln_kernel(c_ref, w_ref, b_ref, o_ref):
    c = c_ref[...]
    o_ref[0] = _mm3(_silu(c), w_ref[0]) + b_ref[0]


def _adaln(c, w_ada, b_ada):
    L, D, N = w_ada.shape
    Bn = c.shape[0]
    tn = min(N, 1536)
    return pl.pallas_call(
        _adaln_kernel,
        grid=(L, N // tn),
        in_specs=[
            pl.BlockSpec((Bn, D), lambda l, j: (0, 0)),
            pl.BlockSpec((1, D, tn), lambda l, j: (l, 0, j)),
            pl.BlockSpec((1, 1, tn), lambda l, j: (l, 0, j)),
        ],
        out_specs=pl.BlockSpec((1, Bn, tn), lambda l, j: (l, 0, j)),
        out_shape=jax.ShapeDtypeStruct((L, Bn, N), F32),
        compiler_params=_cparams("arbitrary", "arbitrary"),
        name="adaln",
    )(c, w_ada, b_ada.reshape(L, 1, N))


def _inproj_kernel(x_ref, g_ref, sc_ref, sh_ref, wa_ref, wvt_ref, wbc_ref, oa_ref, ovt_ref, obc_ref):
    x = x_ref[0]
    h = _rms(x) * g_ref[...] * (1.0 + sc_ref[0]) + sh_ref[0]
    hb = h.astype(BF16)
    oa_ref[0] = jnp.dot(hb, wa_ref[...], preferred_element_type=F32).astype(BF16)
    ovt_ref[0] = lax.dot_general(wvt_ref[...], hb, NT, preferred_element_type=F32).astype(BF16)
    obc_ref[0] = jnp.dot(hb, wbc_ref[...], preferred_element_type=F32)


def _inproj(x, g, sc, sh, wa, wvt, wbc, tm):
    Bn, S, D = x.shape
    na, nv, nbc = wa.shape[1], wvt.shape[0], wbc.shape[1]
    return pl.pallas_call(
        _inproj_kernel,
        grid=(Bn, S // tm),
        in_specs=[
            pl.BlockSpec((1, tm, D), lambda b, i: (b, i, 0)),
            pl.BlockSpec((1, D), lambda b, i: (0, 0)),
            pl.BlockSpec((1, 1, D), lambda b, i: (b, 0, 0)),
            pl.BlockSpec((1, 1, D), lambda b, i: (b, 0, 0)),
            pl.BlockSpec((D, na), lambda b, i: (0, 0)),
            pl.BlockSpec((nv, D), lambda b, i: (0, 0)),
            pl.BlockSpec((D, nbc), lambda b, i: (0, 0)),
        ],
        out_specs=[
            pl.BlockSpec((1, tm, na), lambda b, i: (b, i, 0)),
            pl.BlockSpec((1, nv, tm), lambda b, i: (b, 0, i)),
            pl.BlockSpec((1, tm, nbc), lambda b, i: (b, i, 0)),
        ],
        out_shape=[
            jax.ShapeDtypeStruct((Bn, S, na), BF16),
            jax.ShapeDtypeStruct((Bn, nv, S), BF16),
            jax.ShapeDtypeStruct((Bn, S, nbc), F32),
        ],
        compiler_params=_cparams("arbitrary", "arbitrary"),
        name="inproj",
    )(x, g.reshape(1, D), sc, sh, wa, wvt, wbc)


def _t5_bucket(dist):
    n = jnp.maximum(dist, 0)
    max_exact = N_BUCKETS // 2
    nf = jnp.maximum(n, 1).astype(F32)
    large = max_exact + (jnp.log(nf / max_exact) / math.log(MAX_DISTANCE / max_exact)
                         * (N_BUCKETS - max_exact)).astype(jnp.int32)
    large = jnp.minimum(large, N_BUCKETS - 1)
    return jnp.where(n < max_exact, n, large)


def _attn_band(table, tq):
    far = table[N_BUCKETS - 1].astype(F32)
    H = table.shape[1]
    nb = V7X_LANES
    L = 3 * nb
    m = jnp.arange(L)
    m = jnp.where(m < nb, m, m - L)
    cache = {}

    def block(c):
        if c not in cache:
            if c - (nb - 1) >= MAX_DISTANCE:
                cache[c] = jnp.zeros((H, nb, nb), F32)
            elif c + (nb - 1) < 0:
                cache[c] = jnp.full((H, nb, nb), NEG_BIG, F32)
            else:
                dist = m + c
                vals = jnp.where(dist[None] >= 0,
                                 jnp.transpose(table[_t5_bucket(dist)].astype(F32)) - far[:, None], NEG_BIG)
                cache[c] = jnp.tile(vals, (1, nb))[:, :nb * (L - 1)].reshape(H, nb, L - 1)[:, :, :nb]
        return cache[c]

    bands = []
    for off in (0, tq):
        rows = [jnp.concatenate([block(nb * (a - b) + off) for a in range(tq // nb)], axis=2)
                for b in range(2 * tq // nb)]
        bands.append(jnp.concatenate(rows, axis=1))
    return jnp.stack(bands)


def _attn_kernel(lam_ref, q_ref, k_ref, vt_ref, band_ref, g_ref, o_ref, *, tq, lambda_init):
    i = pl.program_id(2)
    q = q_ref[0] * jnp.asarray(A_QK_DIM ** -0.5, BF16)
    lane = lax.broadcasted_iota(jnp.int32, q.shape, 1)
    zero = jnp.zeros_like(q)
    qq = jnp.concatenate([jnp.where(lane < A_QK_DIM, q, zero),
                          jnp.where(lane >= A_QK_DIM, q, zero)], axis=0)

    kb0 = pl.multiple_of(jnp.maximum(i - 1, 0) * tq, tq)

    def weighted_values(keys, p):
        vt = jnp.concatenate([vt_ref[0, :, keys], jnp.ones((BF16_SUBLANES, p.shape[0]), BF16)], axis=0)
        return jnp.dot(vt, p, preferred_element_type=F32)

    def band_step(n_keys):
        keys = pl.ds(kb0, n_keys)
        band = band_ref[0, 0, :n_keys, :]
        s = lax.dot_general(k_ref[0, keys, :], qq, NT, preferred_element_type=F32) + jnp.concatenate(
            [band, band], axis=1)
        m = jnp.max(s, axis=0, keepdims=True)
        return m, weighted_values(keys, jnp.exp((s - m).astype(BF16)))

    m, acc = lax.cond(i == 0, lambda: band_step(tq), lambda: band_step(2 * tq))

    n_far = jnp.maximum(i - 1, 0)

    def logits(j):
        return lax.dot_general(k_ref[0, pl.ds(pl.multiple_of(j * tq, tq), tq), :], qq, NT,
                               preferred_element_type=F32)

    def body(j, carry):
        m, acc = carry
        s = logits(j)
        m_new = jnp.maximum(m, jnp.max(s, axis=0, keepdims=True))
        alpha = jnp.exp(m - m_new)
        p = jnp.exp((s - m_new).astype(BF16))
        acc = alpha * acc + weighted_values(pl.ds(pl.multiple_of(j * tq, tq), tq), p)
        return m_new, acc

    m, acc = lax.fori_loop(0, n_far, body, (m, acc))

    lp = lam_ref[...]
    lam = (jnp.exp(jnp.sum(lp[0:1] * lp[1:2], axis=-1, keepdims=True))
           - jnp.exp(jnp.sum(lp[2:3] * lp[3:4], axis=-1, keepdims=True)) + lambda_init)
    o = acc[:A_HEAD_W] / acc[A_HEAD_W:A_HEAD_W + 1]
    o = o[:, :tq] - lam * o[:, tq:]
    o = o * lax.rsqrt(jnp.mean(o * o, axis=0, keepdims=True) + RMS_EPS) * g_ref[...] * (1.0 - lambda_init)
    o_ref[0] = jnp.transpose(o)


def _diff_attention(pa, vt, band_t, lam_par, subln_g, lambda_init, tq):
    Bn, S, _ = pa.shape
    W = A_HEAD_W
    kern = functools.partial(_attn_kernel, tq=tq, lambda_init=lambda_init)
    return pl.pallas_call(
        kern,
        grid=(Bn, A_HEADS, S // tq),
        in_specs=[
            pl.BlockSpec((4, A_QK_DIM), lambda b, h, i: (0, 0)),
            pl.BlockSpec((1, tq, W), lambda b, h, i: (b, i, h)),
            pl.BlockSpec((1, S, W), lambda b, h, i: (b, 0, A_HEADS + h)),
            pl.BlockSpec((1, W, S), lambda b, h, i: (b, h, 0)),
            pl.BlockSpec((1, 1, 2 * tq, tq), lambda b, h, i: (jnp.minimum(i, 1), h, 0, 0)),
            pl.BlockSpec((W, 1), lambda b, h, i: (0, 0)),
        ],
        out_specs=pl.BlockSpec((1, tq, W), lambda b, h, i: (b, i, h)),
        out_shape=jax.ShapeDtypeStruct((Bn, S, A_WIDTH), F32),
        compiler_params=_cparams("arbitrary", "arbitrary", "arbitrary"),
        name="diff_attn",
    )(lam_par, pa, pa, vt, band_t, subln_g.reshape(W, 1))


def _head_ones(n):
    r = lax.broadcasted_iota(jnp.int32, (n, n), 0) // B_HEAD_DIM
    c = lax.broadcasted_iota(jnp.int32, (n, n), 1) // B_HEAD_DIM
    return (r == c).astype(BF16)


def _rwkv_prep_kernel(pb_ref, prev_ref, mu_ref, w0_ref, w2_ref, a0_ref, a2_ref, g2_ref,
                      kk_ref, ka_ref, rk_ref,
                      rt_ref, at_ref, kt_ref, bt_ref, v_ref, wc_ref, bonus_ref, g_ref, *, tm):
    i = pl.program_id(1)
    C = RWKV_CHUNK
    x = pb_ref[0]
    row = lax.broadcasted_iota(jnp.int32, x.shape, 0)
    last = prev_ref[0, 7:8, :] * (i > 0).astype(F32)
    prev = jnp.where(row == 0, last, pltpu.roll(x, 1, 0))
    p = x + (prev - x) * mu_ref[...]
    o1, o2, o3 = B_WIDTH, 2 * B_WIDTH, 3 * B_WIDTH
    r, k, v = p[:, :o1], p[:, o1:o2], p[:, o2:o3]
    lora = p[:, o3:o3 + B_DECAY_LORA + B_AAA_LORA]
    gd = p[:, o3 + B_DECAY_LORA + B_AAA_LORA:]

    z = -(w0_ref[...] + _mm3(jnp.tanh(lora), w2_ref[...]))
    softplus = jnp.maximum(z, 0.0) + jnp.log(1.0 + jnp.exp(-jnp.abs(z)))
    logw = -jnp.exp(-softplus - 0.5)
    a = _sigmoid(a0_ref[...] + _mm3(lora, a2_ref[...]))
    g_ref[0] = _mm3(_sigmoid(gd), g2_ref[...])

    ones = _head_ones(B_WIDTH)
    kk = k * kk_ref[...]
    kk = kk * lax.rsqrt(jnp.maximum(_mm2(kk * kk, ones), 1e-24))
    k2 = k * (1.0 + (a - 1.0) * ka_ref[...])
    bonus_ref[0] = _mm2(r * k2 * rk_ref[...], ones) * v

    t_in = lax.broadcasted_iota(jnp.int32, (tm, B_WIDTH), 0) % C
    cum = logw
    sh = 1
    while sh < C:
        cum = cum + jnp.where(t_in >= sh, pltpu.roll(cum, sh, 0), 0.0)
        sh *= 2
    n = tm // C
    wc_ref[0] = jnp.exp(jnp.sum(logw.reshape(n, C, B_WIDTH), axis=1))
    e_pos = jnp.exp(cum)
    e_neg = jnp.exp(-cum)
    rt_ref[0] = r * e_pos
    at_ref[0] = -kk * jnp.exp(cum - logw)
    kt_ref[0] = k2 * e_neg
    bt_ref[0] = kk * a * e_neg
    v_ref[0] = v


def _rwkv_prep(pbc, mu, w0, w2p, a0, a2p, g2, k_k, k_a, r_k, tm):
    Bn, S, _ = pbc.shape
    W = B_WIDTH
    nl = B_DECAY_LORA + B_AAA_LORA
    row = lambda a: a.reshape(1, -1)
    full = lambda shp: pl.BlockSpec(shp, lambda b, i: (0,) * len(shp))
    seq = pl.BlockSpec((1, tm, W), lambda b, i: (b, i, 0))
    seq_shape = jax.ShapeDtypeStruct((Bn, S, W), F32)
    n = tm // RWKV_CHUNK
    return pl.pallas_call(
        functools.partial(_rwkv_prep_kernel, tm=tm),
        grid=(Bn, S // tm),
        in_specs=[
            pl.BlockSpec((1, tm, B_COLS), lambda b, i: (b, i, 0)),
            pl.BlockSpec((1, 8, B_COLS), lambda b, i: (b, jnp.maximum(i * (tm // 8) - 1, 0), 0)),
            full((1, B_COLS)), full((1, W)), full((nl, W)), full((1, W)), full((nl, W)),
            full((B_GATE_LORA, W)), full((1, W)), full((1, W)), full((1, W)),
        ],
        out_specs=[seq, seq, seq, seq, seq,
                   pl.BlockSpec((1, n, W), lambda b, i: (b, i, 0)), seq, seq],
        out_shape=[seq_shape] * 5 + [jax.ShapeDtypeStruct((Bn, S // RWKV_CHUNK, W), F32)] + [seq_shape] * 2,
        compiler_params=_cparams("arbitrary", "arbitrary"),
        name="rwkv_prep",
    )(pbc, pbc, row(mu), row(w0), w2p, row(a0), a2p, g2, row(k_k), row(k_a), row(r_k))


def _rwkv_scan_kernel(rt_ref, at_ref, kt_ref, bt_ref, v_ref, wc_ref, bonus_ref, g_ref,
                      lng_ref, lnb_ref, o_ref, state, *, tt):
    C = RWKV_CHUNK
    W = B_WIDTH

    @pl.when(pl.program_id(1) == 0)
    def _():
        state[...] = jnp.zeros_like(state)

    lane_head = lax.broadcasted_iota(jnp.int32, (C, W), 1) // B_HEAD_DIM
    tt_i = lax.broadcasted_iota(jnp.int32, (C, W), 0)
    ss_i = lax.broadcasted_iota(jnp.int32, (C, W), 1) % C
    strict = tt_i > ss_i
    incl = tt_i >= ss_i
    eye = (tt_i == ss_i).astype(F32)
    ones = _head_ones(W)
    bd_mask = ones.astype(F32)

    head_mask = [(lane_head == h).astype(BF16) for h in range(B_HEADS)]

    def bd_split(x):
        xb = x.astype(BF16)
        return jnp.concatenate([xb * mk for mk in head_mask], axis=0)

    def mm_bd(a, b_bd, dims=NN):
        return lax.dot_general(a.astype(BF16), b_bd, dims, preferred_element_type=F32)

    def state_free(gi, nb):
        G = range(RWKV_GROUP)
        sls = [pl.ds(pl.multiple_of((gi * RWKV_GROUP + j) * C, C), C) for j in G]
        rt = [rt_ref[nb, sl, :] for sl in sls]
        at = [at_ref[nb, sl, :] for sl in sls]
        kt = [kt_ref[nb, sl, :] for sl in sls]
        bt = [bt_ref[nb, sl, :] for sl in sls]
        v = [v_ref[nb, sl, :] for sl in sls]
        wc = [wc_ref[nb, pl.ds(gi * RWKV_GROUP + j, 1), :] for j in G]
        ar = [jnp.concatenate([at[j], rt[j]], axis=0) for j in G]
        bdb = [bd_split(bt[j]) for j in G]
        bdk = [bd_split(kt[j]) for j in G]
        a_b = [mm_bd(ar[j], bdb[j], NT) for j in G]
        a_k = [mm_bd(ar[j], bdk[j], NT) for j in G]
        lo = [jnp.where(strict, a_b[j][:C], 0.0) for j in G]
        a_ak = [jnp.where(strict, a_k[j][:C], 0.0) for j in G]
        a_rb = [jnp.where(incl, a_b[j][C:], 0.0) for j in G]
        a_rk = [jnp.where(incl, a_k[j][C:], 0.0) for j in G]
        pw = lo
        tinv = [eye + lo[j] for j in G]
        bdp = [bd_split(pw[j]) for j in G]
        span = 2
        while span < C:
            pw = [mm_bd(pw[j], bdp[j]) for j in G]
            bdp = [bd_split(pw[j]) for j in G]
            tinv = [tinv[j] + mm_bd(tinv[j], bdp[j]) for j in G]
            span *= 2
        bdv = [bd_split(v[j]) for j in G]
        bda = [bd_split(at[j]) for j in G]
        abar = [mm_bd(tinv[j], bda[j]) for j in G]
        akv = [bd_split(mm_bd(a_ak[j], bdv[j])) for j in G]
        u0 = [mm_bd(tinv[j], akv[j]) for j in G]
        y0 = [mm_bd(a_rk[j], bdv[j]) for j in G]
        kv = [_mm(v[j], kt[j] * wc[j], TN) * bd_mask for j in G]
        return [(jnp.concatenate([abar[j], rt[j]], axis=0), u0[j], y0[j], a_rb[j], bt[j] * wc[j], kv[j], wc[j])
                for j in G]

    def group(gi, carry):
        seqs = range(rt_ref.shape[0])
        pre = [state_free(gi, nb) for nb in seqs]
        s = [state[nb] for nb in seqs]
        ys = [[] for _ in seqs]
        for j in range(RWKV_GROUP):
            for nb in seqs:
                abar_rt, u0, y0, a_rb, btw, kv, wc = pre[nb][j]
                a_s = _mm(abar_rt, s[nb], NT)
                u = a_s[:C] + u0
                ys[nb].append(a_s[C:] + y0 + mm_bd(a_rb, bd_split(u)))
                s[nb] = s[nb] * wc + _mm(u, btw, TN) * bd_mask + kv
        sl = pl.ds(pl.multiple_of(gi * (RWKV_GROUP * C), RWKV_GROUP * C), RWKV_GROUP * C)
        for nb in seqs:
            state[nb] = s[nb]
            y = jnp.concatenate(ys[nb], axis=0)
            mean = _mm2(y, ones) * (1.0 / B_HEAD_DIM)
            d = y - mean
            var = _mm2(d * d, ones) * (1.0 / B_HEAD_DIM)
            yn = d * lax.rsqrt(var + B_LNX_EPS) * lng_ref[...] + lnb_ref[...]
            o_ref[nb, sl, :] = (yn + bonus_ref[nb, sl, :]) * g_ref[nb, sl, :]
        return carry

    lax.fori_loop(0, tt // (RWKV_GROUP * C), group, 0)


def _rwkv_scan(rt, at, kt, bt, v, wc, bonus, g, lnx_g, lnx_b, tt):
    Bn, S, W = rt.shape
    n = tt // RWKV_CHUNK
    nseq = 2 if Bn % 2 == 0 else 1
    seq = pl.BlockSpec((nseq, tt, W), lambda b, i: (b, i, 0))
    vec = pl.BlockSpec((1, W), lambda b, i: (0, 0))
    return pl.pallas_call(
        functools.partial(_rwkv_scan_kernel, tt=tt),
        grid=(Bn // nseq, S // tt),
        in_specs=[seq, seq, seq, seq, seq, pl.BlockSpec((nseq, n, W), lambda b, i: (b, i, 0)), seq, seq, vec, vec],
        out_specs=seq,
        out_shape=jax.ShapeDtypeStruct((Bn, S, W), F32),
        scratch_shapes=[pltpu.VMEM((nseq, B_HEADS * B_HEAD_DIM, W), F32)],
        compiler_params=_cparams("arbitrary", "arbitrary"),
        name="rwkv_scan",
    )(rt, at, kt, bt, v, wc, bonus, g, lnx_g.reshape(1, W), lnx_b.reshape(1, W))


def _gmlp_kernel(pc_ref, lng_ref, lnb_ref, ws_ref, bs_ref, o_ref, *, tm):
    x = pc_ref[0]
    z = x * (0.5 * (1.0 + jnp.tanh(math.sqrt(2.0 / math.pi) * (x + 0.044715 * (x * x * x)))))
    u, v = z[:, :C_WIDTH], z[:, C_WIDTH:]
    mu = jnp.mean(v, axis=-1, keepdims=True)
    d = v - mu
    var = jnp.mean(d * d, axis=-1, keepdims=True)
    vn = d * lax.rsqrt(var + LN_EPS) * lng_ref[...] + lnb_ref[...]
    group = lax.broadcasted_iota(jnp.int32, (CHUNK, C_WIDTH), 1) // C_GROUP_DIM
    tril = (lax.broadcasted_iota(jnp.int32, (CHUNK, CHUNK), 0)
            >= lax.broadcasted_iota(jnp.int32, (CHUNK, CHUNK), 1))
    ws = [jnp.where(tril, ws_ref[gi], 0.0).astype(BF16) for gi in range(C_GROUPS)]
    for c in range(tm // CHUNK):
        sl = slice(c * CHUNK, (c + 1) * CHUNK)
        vc = vn[sl].astype(BF16)
        sv = bs_ref[...]
        for gi in range(C_GROUPS):
            t = jnp.dot(ws[gi], vc, preferred_element_type=F32)
            sv = sv + jnp.where(group == gi, t, 0.0)
        o_ref[0, sl, :] = u[sl] * sv


def _gmlp(pbc, ln_g, ln_b, w_s, b_s, tm):
    Bn, S, _ = pbc.shape
    bs_wide = jnp.repeat(jnp.transpose(b_s), C_GROUP_DIM, axis=1)
    return pl.pallas_call(
        functools.partial(_gmlp_kernel, tm=tm),
        grid=(Bn, S // tm),
        in_specs=[
            pl.BlockSpec((1, tm, C_COLS), lambda b, i: (b, i, B_COLS // C_COLS)),
            pl.BlockSpec((1, C_WIDTH), lambda b, i: (0, 0)),
            pl.BlockSpec((1, C_WIDTH), lambda b, i: (0, 0)),
            pl.BlockSpec((C_GROUPS, CHUNK, CHUNK), lambda b, i: (0, 0, 0)),
            pl.BlockSpec((CHUNK, C_WIDTH), lambda b, i: (0, 0)),
        ],
        out_specs=pl.BlockSpec((1, tm, C_WIDTH), lambda b, i: (b, i, 0)),
        out_shape=jax.ShapeDtypeStruct((Bn, S, C_WIDTH), F32),
        compiler_params=_cparams("arbitrary", "arbitrary"),
        name="gmlp",
    )(pbc, ln_g.reshape(1, -1), ln_b.reshape(1, -1), w_s, bs_wide)


def _mid_kernel(ya_ref, yb_ref, yc_ref, x_ref, woa_ref, wob_ref, woc_ref, gpost_ref, g1_ref,
                gpre_ref, sc_ref, sh_ref, wr_ref, xo_ref, h_ref, score_ref):
    y = (_mm(ya_ref[0], woa_ref[...]) + _mm(yb_ref[0], wob_ref[...]) + _mm(yc_ref[0], woc_ref[...]))
    xn = x_ref[0] + g1_ref[0] * (_rms(y) * gpost_ref[...])
    xo_ref[0] = xn
    h = _rms(xn) * gpre_ref[...] * (1.0 + sc_ref[0]) + sh_ref[0]
    h_ref[0] = _pack_bf16_pair(h)
    score_ref[0] = _sigmoid(_mm3(wr_ref[...], h, NT))


def _mid(ya, yb, yc, x, woa, wob, woc, gpost, g1, gpre, sc, sh, wr, tm):
    Bn, S, D = x.shape
    NR = wr.shape[0]
    seq = lambda w: pl.BlockSpec((1, tm, w), lambda b, i: (b, i, 0))
    full = lambda shp: pl.BlockSpec(shp, lambda b, i: (0,) * len(shp))
    per_b = pl.BlockSpec((1, 1, D), lambda b, i: (b, 0, 0))
    return pl.pallas_call(
        _mid_kernel,
        grid=(Bn, S // tm),
        in_specs=[seq(A_WIDTH), seq(B_WIDTH), seq(C_WIDTH), seq(D),
                  full((A_WIDTH, D)), full((B_WIDTH, D)), full((C_WIDTH, D)),
                  full((1, D)), per_b, full((1, D)), per_b, per_b,
                  full((NR, D))],
        out_specs=[seq(D), seq(D // 2), pl.BlockSpec((1, NR, tm), lambda b, i: (b, 0, i))],
        out_shape=[jax.ShapeDtypeStruct((Bn, S, D), F32), jax.ShapeDtypeStruct((Bn, S, D // 2), jnp.int32),
                   jax.ShapeDtypeStruct((Bn, NR, S), F32)],
        compiler_params=_cparams("arbitrary", "arbitrary"),
        name="mid",
    )(ya, yb, yc, x, woa, wob, woc, gpost.reshape(1, D), g1, gpre.reshape(1, D), sc, sh, wr)


def _shared_expert_kernel(h_ref, ws1_ref, ws3_ref, ws2_ref, o_ref):
    lo, hi = _unpack_bf16_pair(h_ref[0])
    hb = jnp.concatenate([lo, hi], axis=1).astype(BF16)
    t = _silu(jnp.dot(hb, ws1_ref[...], preferred_element_type=F32)) * jnp.dot(
        hb, ws3_ref[...], preferred_element_type=F32)
    o_ref[0] = jnp.dot(t.astype(BF16), ws2_ref[...], preferred_element_type=F32)


def _shared_expert(hp, ws1, ws3, ws2, tm):
    Bn, S, DP = hp.shape
    D, F = ws1.shape
    full = lambda shp: pl.BlockSpec(shp, lambda b, i: (0,) * len(shp))
    return pl.pallas_call(
        _shared_expert_kernel,
        grid=(Bn, S // tm),
        in_specs=[pl.BlockSpec((1, tm, DP), lambda b, i: (b, i, 0)), full((D, F)), full((D, F)), full((F, D))],
        out_specs=pl.BlockSpec((1, tm, D), lambda b, i: (b, i, 0)),
        out_shape=jax.ShapeDtypeStruct((Bn, S, D), F32),
        compiler_params=_cparams("arbitrary", "arbitrary"),
        name="shared_expert",
    )(hp, ws1, ws3, ws2)


def _first_argmax(vals, iota, n):
    m = jnp.max(vals, axis=0, keepdims=True)
    idx = jnp.min(jnp.where(vals == m, iota, n), axis=0, keepdims=True)
    return m, idx


def _route_kernel(sc_ref, bias_ref, e_ref, w_ref, r_ref, cnt_ref, carry, *, tm):
    @pl.when((pl.program_id(0) == 0) & (pl.program_id(1) == 0))
    def _():
        carry[...] = jnp.zeros_like(carry)

    G = EXPERTS_PER_GROUP
    s = sc_ref[0]
    biased = s + bias_ref[...]
    neg_inf = jnp.float32(-jnp.inf)
    io8 = lax.broadcasted_iota(jnp.int32, (G, tm), 0)
    gs_rows = []
    for g in range(N_GROUPS):
        blk = biased[g * G:(g + 1) * G]
        m1, i1 = _first_argmax(blk, io8, G)
        m2 = jnp.max(jnp.where(io8 == i1, neg_inf, blk), axis=0, keepdims=True)
        gs_rows.append(m1 + m2)
    gs = jnp.concatenate(gs_rows, axis=0)
    gio = lax.broadcasted_iota(jnp.int32, (N_GROUPS, tm), 0)
    gsel = jnp.zeros((N_GROUPS, tm), jnp.bool_)
    for _ in range(TOPK_GROUPS):
        _, gi = _first_argmax(gs, gio, N_GROUPS)
        pick = gio == gi
        gsel = gsel | pick
        gs = jnp.where(pick, neg_inf, gs)
    masked = jnp.concatenate(
        [jnp.where(gsel[g:g + 1], biased[g * G:(g + 1) * G], neg_inf) for g in range(N_GROUPS)], axis=0)

    eio = lax.broadcasted_iota(jnp.int32, (N_EXPERTS, tm), 0)
    picks, e_rows, s_rows = [], [], []
    for _ in range(TOP_K):
        _, ei = _first_argmax(masked, eio, N_EXPERTS)
        pick = eio == ei
        picks.append(pick)
        e_rows.append(ei)
        s_rows.append(jnp.sum(jnp.where(pick, s, 0.0), axis=0, keepdims=True))
        masked = jnp.where(pick, neg_inf, masked)
    top_s = jnp.concatenate(s_rows, axis=0)
    w_ref[...] = top_s / (jnp.sum(top_s, axis=0, keepdims=True) + 1e-20) * ROUTED_SCALE
    e_ref[...] = jnp.concatenate(e_rows, axis=0)

    sel = jnp.zeros((N_EXPERTS, tm), F32)
    for pick in picks:
        sel = sel + pick.astype(F32)
    before = (lax.broadcasted_iota(jnp.int32, (tm, tm), 0) < lax.broadcasted_iota(jnp.int32, (tm, tm), 1))
    pos = carry[...] + jnp.dot(sel.astype(BF16), before.astype(BF16), preferred_element_type=F32)
    r_ref[...] = jnp.concatenate(
        [jnp.sum(jnp.where(pick, pos, 0.0), axis=0, keepdims=True) for pick in picks], axis=0).astype(jnp.int32)
    total = carry[...] + jnp.sum(sel, axis=1, keepdims=True)
    carry[...] = total
    cnt_ref[...] = jnp.broadcast_to(total, cnt_ref.shape).astype(jnp.int32)


def _route(scores_t, e_bias, tm):
    Bn, _, S = scores_t.shape
    T = Bn * S
    nt = S // tm
    tok = pl.BlockSpec((TOP_K, tm), lambda b, i: (0, b * nt + i))
    return pl.pallas_call(
        functools.partial(_route_kernel, tm=tm),
        grid=(Bn, nt),
        in_specs=[pl.BlockSpec((1, N_EXPERTS, tm), lambda b, i: (b, 0, i)),
                  pl.BlockSpec((N_EXPERTS, 1), lambda b, i: (0, 0))],
        out_specs=[tok, tok, tok, pl.BlockSpec((N_EXPERTS, V7X_LANES), lambda b, i: (0, 0))],
        out_shape=[jax.ShapeDtypeStruct((TOP_K, T), jnp.int32), jax.ShapeDtypeStruct((TOP_K, T), F32),
                   jax.ShapeDtypeStruct((TOP_K, T), jnp.int32),
                   jax.ShapeDtypeStruct((N_EXPERTS, V7X_LANES), jnp.int32)],
        scratch_shapes=[pltpu.VMEM((N_EXPERTS, 1), F32)],
        compiler_params=_cparams("arbitrary", "arbitrary"),
        name="route",
    )(scores_t, e_bias.reshape(N_EXPERTS, 1))


def _dest_kernel(start_ref, e_ref, r_ref, o_ref):
    e = e_ref[...]
    acc = r_ref[...]
    for ex in range(N_EXPERTS):
        acc = acc + jnp.where(e == ex, start_ref[ex], 0)
    o_ref[0] = acc


def _dest_rows(pad_start, eidx, rank, tt):
    K_, T = eidx.shape
    grid_spec = pltpu.PrefetchScalarGridSpec(
        num_scalar_prefetch=1,
        grid=(T // tt,),
        in_specs=[pl.BlockSpec((K_, tt), lambda i, st: (0, i)), pl.BlockSpec((K_, tt), lambda i, st: (0, i))],
        out_specs=pl.BlockSpec((1, K_, tt), lambda i, st: (i, 0, 0)),
    )
    return pl.pallas_call(
        _dest_kernel,
        grid_spec=grid_spec,
        out_shape=jax.ShapeDtypeStruct((T // tt, K_, tt), jnp.int32),
        compiler_params=_cparams("arbitrary"),
        name="dest_rows",
    )(pad_start, eidx, rank)


def _expert_kernel(blk_e_ref, n_used_ref, n_valid_ref, x_ref, w1_ref, w3_ref, w2_ref, o_ref, w1b, w3b, w2b):
    i = pl.program_id(0)

    @pl.when((i == 0) | (blk_e_ref[i] != blk_e_ref[jnp.maximum(i - 1, 0)]))
    def _():
        w1b[...] = w1_ref[0].astype(BF16)
        w3b[...] = w3_ref[0].astype(BF16)
        w2b[...] = w2_ref[0].astype(BF16)

    @pl.when(i < n_used_ref[0])
    def _():
        row = lax.broadcasted_iota(jnp.int32, x_ref.shape, 0)
        x_lo, x_hi = _unpack_bf16_pair(jnp.where(row < n_valid_ref[i], x_ref[...], 0))
        x_lo, x_hi = x_lo.astype(BF16), x_hi.astype(BF16)
        half = x_lo.shape[1]

        def up(wb):
            return (jnp.dot(x_lo, wb[:half, :], preferred_element_type=F32)
                    + jnp.dot(x_hi, wb[half:, :], preferred_element_type=F32))

        t = _silu(up(w1b)) * up(w3b)
        o_ref[...] = _pack_bf16_pair(jnp.dot(t.astype(BF16), w2b[...], preferred_element_type=F32))


def _experts(blk_e, n_used, n_valid, xs, w1, w3, w2, layer):
    P, DP = xs.shape
    EB = EXPERT_BLOCK
    n_blocks = blk_e.shape[0]
    D, F = w1.shape[2], w1.shape[3]
    rows = pl.BlockSpec((EB, DP), lambda i, be, nu, nv: (jnp.minimum(i, nu[0] - 1), 0))
    grid_spec = pltpu.PrefetchScalarGridSpec(
        num_scalar_prefetch=3,
        grid=(n_blocks,),
        in_specs=[
            rows,
            pl.BlockSpec((None, 1, D, F), lambda i, be, nu, nv: (layer, be[i], 0, 0)),
            pl.BlockSpec((None, 1, D, F), lambda i, be, nu, nv: (layer, be[i], 0, 0)),
            pl.BlockSpec((None, 1, F, D), lambda i, be, nu, nv: (layer, be[i], 0, 0)),
        ],
        out_specs=rows,
        scratch_shapes=[pltpu.VMEM((D, F), BF16), pltpu.VMEM((D, F), BF16), pltpu.VMEM((F, D), BF16)],
    )
    return pl.pallas_call(
        _expert_kernel,
        grid_spec=grid_spec,
        out_shape=jax.ShapeDtypeStruct((P, DP), jnp.int32),
        compiler_params=_cparams("arbitrary"),
        name="experts",
    )(blk_e, n_used, n_valid, xs, w1, w3, w2)


def _block_layout(counts, n_blocks):
    EB = EXPERT_BLOCK
    padded = (counts + EB - 1) // EB * EB
    ex = jnp.arange(N_EXPERTS)
    pad_end = jnp.sum(jnp.where(ex[:, None] <= ex[None, :], padded[:, None], 0), axis=0)
    pad_start = pad_end - padded
    blk_row = (jnp.arange(n_blocks) * EB)[:, None]
    blk_e = jnp.minimum(jnp.sum((pad_end[None, :] <= blk_row).astype(jnp.int32), axis=1), N_EXPERTS - 1)
    n_used = (jnp.sum(padded) // EB).astype(jnp.int32).reshape(1)
    mine = (pad_start[None, :] <= blk_row) & (blk_row < pad_end[None, :])
    n_valid = jnp.sum(jnp.where(mine, jnp.clip(counts[None, :] - (blk_row - pad_start[None, :]), 0, EB), 0), axis=1)
    return pad_start.astype(jnp.int32), blk_e.astype(jnp.int32), n_used, n_valid.astype(jnp.int32)


SC_GATHER_ROWS = 64


def _sc_gather_rows(table, idx):
    info = plsc.get_sparse_core_info()
    nc, ns = info.num_cores, info.num_subcores
    M = idx.shape[0]
    W = table.shape[1]
    b = SC_GATHER_ROWS
    per_worker = M // (nc * ns)
    steps = per_worker // b
    assert per_worker * nc * ns == M and steps * b == per_worker and steps % 2 == 0
    mesh = plsc.VectorSubcoreMesh(core_axis_name="c", subcore_axis_name="s")

    @functools.partial(
        pl.kernel, mesh=mesh,
        out_type=jax.ShapeDtypeStruct((M, W), table.dtype),
        scratch_types=[pltpu.VMEM((2, b), jnp.int32), pltpu.VMEM((2, b, W), table.dtype),
                       pltpu.SemaphoreType.DMA, pltpu.SemaphoreType.DMA],
        name="sc_gather_rows",
    )
    def gather(table_hbm, idx_hbm, out_hbm, idx_v, rows_v, sem0, sem1):
        wid = lax.axis_index("s") * nc + lax.axis_index("c")
        sems = (sem0, sem1)

        def base(s):
            return pl.multiple_of(wid * per_worker + s * b, b)

        def gather_copy(slot):
            return pltpu.make_async_copy(table_hbm.at[idx_v.at[slot]], rows_v.at[slot], sems[slot])

        def start(s, slot):
            pltpu.sync_copy(idx_hbm.at[pl.ds(base(s), b)], idx_v.at[slot])
            gather_copy(slot).start()

        def finish(s, slot):
            gather_copy(slot).wait()
            pltpu.sync_copy(rows_v.at[slot], out_hbm.at[pl.ds(base(s), b)])

        start(0, 0)

        @pl.loop(0, steps, step=2)
        def _(s):
            start(s + 1, 1)
            finish(s, 0)

            @pl.when(s + 2 < steps)
            def _():
                start(s + 2, 0)

            finish(s + 1, 1)

    return gather(table, idx)


def _sc_scatter_rows(rows, idx, n_out):
    info = plsc.get_sparse_core_info()
    nc, ns = info.num_cores, info.num_subcores
    T, W = rows.shape
    G, K_, b = idx.shape
    steps = G // (nc * ns)
    assert steps * nc * ns == G and G * b == T
    mesh = plsc.VectorSubcoreMesh(core_axis_name="c", subcore_axis_name="s")

    @functools.partial(
        pl.kernel, mesh=mesh,
        out_type=jax.ShapeDtypeStruct((n_out, W), rows.dtype),
        scratch_types=[pltpu.VMEM((K_, b), jnp.int32), pltpu.VMEM((b, W), rows.dtype), pltpu.SemaphoreType.DMA],
        name="sc_scatter_rows",
    )
    def scatter(rows_hbm, idx_hbm, out_hbm, idx_v, rows_v, sem):
        wid = lax.axis_index("s") * nc + lax.axis_index("c")

        @pl.loop(0, steps)
        def _(s):
            g = wid * steps + s
            pltpu.sync_copy(idx_hbm.at[g], idx_v)
            pltpu.sync_copy(rows_hbm.at[pl.ds(pl.multiple_of(g * b, b), b)], rows_v)
            copies = [pltpu.async_copy(rows_v, out_hbm.at[idx_v.at[k]], sem) for k in range(K_)]
            for cp in copies:
                cp.wait()

    return scatter(rows, idx)


def _combine_dense_kernel(rows_ref, w_ref, x_ref, shared_ref, gpost_ref, g2_ref, o_ref):
    w = w_ref[...]
    tt, half = rows_ref.shape[1], rows_ref.shape[2]
    y_lo = jnp.zeros((tt, half), F32)
    y_hi = jnp.zeros((tt, half), F32)
    for k in range(TOP_K):
        lo, hi = _unpack_bf16_pair(rows_ref[k])
        y_lo = y_lo + w[:, k:k + 1] * lo
        y_hi = y_hi + w[:, k:k + 1] * hi
    y = shared_ref[0] + jnp.concatenate([y_lo, y_hi], axis=1)
    o_ref[0] = x_ref[0] + g2_ref[0] * (_rms(y) * gpost_ref[...])


def _combine_dense(rows, w_tok, x, shared, gpost, g2, tt):
    Bn, S, D = x.shape
    K_, T, DP = rows.shape
    nt = S // tt
    seq = pl.BlockSpec((1, tt, D), lambda b, i: (b, i, 0))
    return pl.pallas_call(
        _combine_dense_kernel,
        grid=(Bn, nt),
        in_specs=[pl.BlockSpec((K_, tt, DP), lambda b, i: (0, b * nt + i, 0)),
                  pl.BlockSpec((tt, K_), lambda b, i: (b * nt + i, 0)), seq, seq,
                  pl.BlockSpec((1, D), lambda b, i: (0, 0)), pl.BlockSpec((1, 1, D), lambda b, i: (b, 0, 0))],
        out_specs=seq,
        out_shape=jax.ShapeDtypeStruct((Bn, S, D), F32),
        compiler_params=_cparams("arbitrary", "arbitrary"),
        name="combine_dense",
    )(rows, w_tok, x, shared, gpost.reshape(1, D), g2)


def kernel(x, c, w_ada, b_ada, norm_pre_mix, norm_post_mix, norm_pre_ffn, norm_post_ffn, w_in, w_out, rel_bias_table, diff_lambda, diff_subln, rwkv_mu, rwkv_w0, rwkv_w2, rwkv_a0, rwkv_a2, rwkv_g2, rwkv_k_k, rwkv_k_a, rwkv_r_k, rwkv_lnx_g, rwkv_lnx_b, gmlp_ln_g, gmlp_ln_b, gmlp_w_s, gmlp_b_s, router_w, router_bias, exp_w1, exp_w3, exp_w2, shared_w1, shared_w3, shared_w2):
    Bn, S, D = x.shape
    depth = w_ada.shape[0]
    tm = min(256, S)
    tm_wide = min(512, S)
    tq = min(512, S // 2)
    t_rwkv = min(512, S)

    mod = _adaln(c, w_ada, b_ada)
    band_t = _attn_band(rel_bias_table, tq)
    zpad = jnp.zeros((B_DECAY_LORA, B_WIDTH), F32)
    for l in range(depth):
        sh1, sc1, g1, sh2, sc2, g2 = [m.reshape(Bn, 1, D) for m in jnp.split(mod[l], 6, axis=-1)]
        w_in_b = w_in[l].astype(BF16)
        pa, vt, pbc = _inproj(x, norm_pre_mix[l], sc1, sh1, w_in_b[:, :2 * A_WIDTH],
                              jnp.transpose(w_in_b[:, 2 * A_WIDTH:A_COLS]), w_in_b[:, A_COLS:], tm_wide)
        lambda_init = 0.8 - 0.6 * math.exp(-0.3 * l)
        ya = _diff_attention(pa, vt, band_t, diff_lambda[l], diff_subln[l], lambda_init, tq)
        prep = _rwkv_prep(pbc, rwkv_mu[l], rwkv_w0[l], jnp.concatenate([rwkv_w2[l], zpad], axis=0),
                          rwkv_a0[l], jnp.concatenate([zpad, rwkv_a2[l]], axis=0), rwkv_g2[l],
                          rwkv_k_k[l], rwkv_k_a[l], rwkv_r_k[l].reshape(-1), t_rwkv)
        yb = _rwkv_scan(*prep, rwkv_lnx_g[l], rwkv_lnx_b[l], t_rwkv)
        yc = _gmlp(pbc, gmlp_ln_g[l], gmlp_ln_b[l], gmlp_w_s[l], gmlp_b_s[l], tm_wide)

        w_out_b = w_out[l].astype(BF16)
        wr_t = jnp.pad(jnp.transpose(router_w[l]), ((0, V7X_LANES - N_EXPERTS), (0, 0)))
        x, h, scores_t = _mid(
            ya, yb, yc, x, w_out_b[:A_WIDTH], w_out_b[A_WIDTH:A_WIDTH + B_WIDTH], w_out_b[A_WIDTH + B_WIDTH:],
            norm_post_mix[l], g1, norm_pre_ffn[l], sc2, sh2, wr_t, tm_wide)

        T = Bn * S
        n_blocks = -(-T * TOP_K // EXPERT_BLOCK) + N_EXPERTS
        eidx, wgt, rank, cnt = _route(scores_t, router_bias[l], tm)
        pad_start, blk_e, n_used, n_valid = _block_layout(cnt[:, 0], n_blocks)
        dest = _dest_rows(pad_start, eidx, rank, tm)
        b = SC_GATHER_ROWS
        dest_sc = jnp.transpose(dest.reshape(T // tm, TOP_K, tm // b, b), (0, 2, 1, 3)).reshape(T // b, TOP_K, b)
        xs = _sc_scatter_rows(h.reshape(T, D // 2), dest_sc, n_blocks * EXPERT_BLOCK)
        shared = _shared_expert(h, shared_w1[l].astype(BF16), shared_w3[l].astype(BF16),
                                shared_w2[l].astype(BF16), tm_wide)
        ys = _experts(blk_e, n_used, n_valid, xs, exp_w1, exp_w3, exp_w2, l)
        dest_kt = jnp.transpose(dest, (1, 0, 2)).reshape(TOP_K * T)
        rows = _sc_gather_rows(ys, dest_kt).reshape(TOP_K, T, D // 2)
        x = _combine_dense(rows, jnp.transpose(wgt), x, shared, norm_post_ffn[l], g2, tm)
    return x
```

```python
import functools
import math

import jax
import jax.numpy as jnp
from jax import lax
from jax.experimental import pallas as pl
from jax.experimental.pallas import tpu as pltpu
from jax.experimental.pallas import tpu_sc as plsc

F32 = jnp.float32
BF16 = jnp.bfloat16

A_HEADS = 4
A_QK_DIM = 64
A_HEAD_W = 2 * A_QK_DIM
A_WIDTH = A_HEADS * A_HEAD_W
N_BUCKETS = 32
MAX_DISTANCE = 128
B_HEADS = 4
B_HEAD_DIM = 64
B_WIDTH = B_HEADS * B_HEAD_DIM
B_DECAY_LORA = 64
B_AAA_LORA = 64
B_GATE_LORA = 128
B_LNX_EPS = 64e-5
C_GROUPS = 4
C_GROUP_DIM = 64
C_WIDTH = C_GROUPS * C_GROUP_DIM
CHUNK = 128
A_COLS = 3 * A_WIDTH
B_COLS = 3 * B_WIDTH + B_DECAY_LORA + B_AAA_LORA + B_GATE_LORA
C_COLS = 2 * C_WIDTH
N_EXPERTS = 64
TOP_K = 8
N_GROUPS = 8
TOPK_GROUPS = 4
EXPERTS_PER_GROUP = N_EXPERTS // N_GROUPS
ROUTED_SCALE = 2.5
EXPERT_BLOCK = 512
RMS_EPS = 1e-6
LN_EPS = 1e-5
NEG_BIG = -1e30

V7X_LANES = 128
BF16_SUBLANES = 16
V7X_VMEM_LIMIT_BYTES = 56 * 1024 * 1024
RWKV_CHUNK = 64
RWKV_GROUP = 8

NN = (((1,), (0,)), ((), ()))
NT = (((1,), (1,)), ((), ()))
TN = (((0,), (0,)), ((), ()))


def _cparams(*sem):
    return pltpu.CompilerParams(dimension_semantics=sem, vmem_limit_bytes=V7X_VMEM_LIMIT_BYTES)


def _mm(a, b, dims=NN):
    return lax.dot_general(a.astype(BF16), b.astype(BF16), dims, preferred_element_type=F32)


def _split(a):
    hi = a.astype(BF16)
    lo = (a - hi.astype(F32)).astype(BF16)
    return hi, lo


def _mm3(a, b, dims=NN):
    ah, al = _split(a)
    bh, bl = _split(b)
    d = lambda x, y: lax.dot_general(x, y, dims, preferred_element_type=F32)
    return d(ah, bh) + d(ah, bl) + d(al, bh)


def _mm2(a, b_exact, dims=NN):
    ah, al = _split(a)
    d = lambda x: lax.dot_general(x, b_exact, dims, preferred_element_type=F32)
    return d(ah) + d(al)


def _pack_bf16_pair(x):
    n = x.shape[1] // 2
    bits = lax.bitcast_convert_type(x.astype(BF16).astype(F32), jnp.int32)
    return ((bits[:, :n] >> 16) & 0xFFFF) | bits[:, n:]


def _unpack_bf16_pair(u):
    lo = lax.bitcast_convert_type(u << 16, F32)
    hi = lax.bitcast_convert_type(u & jnp.int32(-65536), F32)
    return lo, hi


def _rms(x, eps=RMS_EPS):
    return x * lax.rsqrt(jnp.mean(x * x, axis=-1, keepdims=True) + eps)


def _sigmoid(x):
    return 1.0 / (1.0 + jnp.exp(-x))


def _silu(x):
    return x * _sigmoid(x)


def _adaln_kernel(c_ref, w_ref, b_ref, o_ref):
    c = c_ref[...]
    o_ref[0] = _mm3(_silu(c), w_ref[0]) + b_ref[0]


def _adaln(c, w_ada, b_ada):
    L, D, N = w_ada.shape
    Bn = c.shape[0]
    tn = min(N, 1536)
    return pl.pallas_call(
        _adaln_kernel,
        grid=(L, N // tn),
        in_specs=[
            pl.BlockSpec((Bn, D), lambda l, j: (0, 0)),
            pl.BlockSpec((1, D, tn), lambda l, j: (l, 0, j)),
            pl.BlockSpec((1, 1, tn), lambda l, j: (l, 0, j)),
        ],
        out_specs=pl.BlockSpec((1, Bn, tn), lambda l, j: (l, 0, j)),
        out_shape=jax.ShapeDtypeStruct((L, Bn, N), F32),
        compiler_params=_cparams("arbitrary", "arbitrary"),
        name="adaln",
    )(c, w_ada, b_ada.reshape(L, 1, N))


def _inproj_kernel(x_ref, g_ref, sc_ref, sh_ref, wa_ref, wvt_ref, wbc_ref, oa_ref, ovt_ref, obc_ref):
    x = x_ref[0]
    h = _rms(x) * g_ref[...] * (1.0 + sc_ref[0]) + sh_ref[0]
    hb = h.astype(BF16)
    oa_ref[0] = jnp.dot(hb, wa_ref[...], preferred_element_type=F32).astype(BF16)
    ovt_ref[0] = lax.dot_general(wvt_ref[...], hb, NT, preferred_element_type=F32).astype(BF16)
    obc_ref[0] = jnp.dot(hb, wbc_ref[...], preferred_element_type=F32)


def _inproj(x, g, sc, sh, wa, wvt, wbc, tm):
    Bn, S, D = x.shape
    na, nv, nbc = wa.shape[1], wvt.shape[0], wbc.shape[1]
    return pl.pallas_call(
        _inproj_kernel,
        grid=(Bn, S // tm),
        in_specs=[
            pl.BlockSpec((1, tm, D), lambda b, i: (b, i, 0)),
            pl.BlockSpec((1, D), lambda b, i: (0, 0)),
            pl.BlockSpec((1, 1, D), lambda b, i: (b, 0, 0)),
            pl.BlockSpec((1, 1, D), lambda b, i: (b, 0, 0)),
            pl.BlockSpec((D, na), lambda b, i: (0, 0)),
            pl.BlockSpec((nv, D), lambda b, i: (0, 0)),
            pl.BlockSpec((D, nbc), lambda b, i: (0, 0)),
        ],
        out_specs=[
            pl.BlockSpec((1, tm, na), lambda b, i: (b, i, 0)),
            pl.BlockSpec((1, nv, tm), lambda b, i: (b, 0, i)),
            pl.BlockSpec((1, tm, nbc), lambda b, i: (b, i, 0)),
        ],
        out_shape=[
            jax.ShapeDtypeStruct((Bn, S, na), BF16),
            jax.ShapeDtypeStruct((Bn, nv, S), BF16),
            jax.ShapeDtypeStruct((Bn, S, nbc), F32),
        ],
        compiler_params=_cparams("arbitrary", "arbitrary"),
        name="inproj",
    )(x, g.reshape(1, D), sc, sh, wa, wvt, wbc)


def _t5_bucket(dist):
    n = jnp.maximum(dist, 0)
    max_exact = N_BUCKETS // 2
    nf = jnp.maximum(n, 1).astype(F32)
    large = max_exact + (jnp.log(nf / max_exact) / math.log(MAX_DISTANCE / max_exact)
                         * (N_BUCKETS - max_exact)).astype(jnp.int32)
    large = jnp.minimum(large, N_BUCKETS - 1)
    return jnp.where(n < max_exact, n, large)


def _attn_band(table, tq):
    far = table[N_BUCKETS - 1].astype(F32)
    H = table.shape[1]
    nb = V7X_LANES
    L = 3 * nb
    m = jnp.arange(L)
    m = jnp.where(m < nb, m, m - L)
    cache = {}

    def block(c):
        if c not in cache:
            if c - (nb - 1) >= MAX_DISTANCE:
                cache[c] = jnp.zeros((H, nb, nb), F32)
            elif c + (nb - 1) < 0:
                cache[c] = jnp.full((H, nb, nb), NEG_BIG, F32)
            else:
                dist = m + c
                vals = jnp.where(dist[None] >= 0,
                                 jnp.transpose(table[_t5_bucket(dist)].astype(F32)) - far[:, None], NEG_BIG)
                cache[c] = jnp.tile(vals, (1, nb))[:, :nb * (L - 1)].reshape(H, nb, L - 1)[:, :, :nb]
        return cache[c]

    bands = []
    for off in (0, tq):
        rows = [jnp.concatenate([block(nb * (a - b) + off) for a in range(tq // nb)], axis=2)
                for b in range(2 * tq // nb)]
        bands.append(jnp.concatenate(rows, axis=1))
    return jnp.stack(bands)


def _attn_kernel(lam_ref, q_ref, k_ref, vt_ref, band_ref, g_ref, o_ref, *, tq, lambda_init):
    i = pl.program_id(2)
    q = q_ref[0] * jnp.asarray(A_QK_DIM ** -0.5, BF16)
    lane = lax.broadcasted_iota(jnp.int32, q.shape, 1)
    zero = jnp.zeros_like(q)
    qq = jnp.concatenate([jnp.where(lane < A_QK_DIM, q, zero),
                          jnp.where(lane >= A_QK_DIM, q, zero)], axis=0)

    kb0 = pl.multiple_of(jnp.maximum(i - 1, 0) * tq, tq)
    kb = k_ref[0, pl.ds(kb0, 2 * tq), :]
    band = band_ref[0, 0]
    s = lax.dot_general(kb, qq, NT, preferred_element_type=F32) + jnp.concatenate([band, band], axis=1)
    m = jnp.max(s, axis=0, keepdims=True)

    def weighted_values(keys, p):
        vt = jnp.concatenate([vt_ref[0, :, keys], jnp.ones((BF16_SUBLANES, p.shape[0]), BF16)], axis=0)
        return jnp.dot(vt, p, preferred_element_type=F32)

    acc = weighted_values(pl.ds(kb0, 2 * tq), jnp.exp((s - m).astype(BF16)))

    n_far = jnp.maximum(i - 1, 0)

    def logits(j):
        return lax.dot_general(k_ref[0, pl.ds(pl.multiple_of(j * tq, tq), tq), :], qq, NT,
                               preferred_element_type=F32)

    def body(j, carry):
        m, acc = carry
        s = logits(j)
        m_new = jnp.maximum(m, jnp.max(s, axis=0, keepdims=True))
        alpha = jnp.exp(m - m_new)
        p = jnp.exp((s - m_new).astype(BF16))
        acc = alpha * acc + weighted_values(pl.ds(pl.multiple_of(j * tq, tq), tq), p)
        return m_new, acc

    m, acc = lax.fori_loop(0, n_far, body, (m, acc))

    lp = lam_ref[...]
    lam = (jnp.exp(jnp.sum(lp[0:1] * lp[1:2], axis=-1, keepdims=True))
           - jnp.exp(jnp.sum(lp[2:3] * lp[3:4], axis=-1, keepdims=True)) + lambda_init)
    o = acc[:A_HEAD_W] / acc[A_HEAD_W:A_HEAD_W + 1]
    o = o[:, :tq] - lam * o[:, tq:]
    o = o * lax.rsqrt(jnp.mean(o * o, axis=0, keepdims=True) + RMS_EPS) * g_ref[...] * (1.0 - lambda_init)
    o_ref[0] = jnp.transpose(o)


def _diff_attention(pa, vt, band_t, lam_par, subln_g, lambda_init, tq):
    Bn, S, _ = pa.shape
    W = A_HEAD_W
    kern = functools.partial(_attn_kernel, tq=tq, lambda_init=lambda_init)
    return pl.pallas_call(
        kern,
        grid=(Bn, A_HEADS, S // tq),
        in_specs=[
            pl.BlockSpec((4, A_QK_DIM), lambda b, h, i: (0, 0)),
            pl.BlockSpec((1, tq, W), lambda b, h, i: (b, i, h)),
            pl.BlockSpec((1, S, W), lambda b, h, i: (b, 0, A_HEADS + h)),
            pl.BlockSpec((1, W, S), lambda b, h, i: (b, h, 0)),
            pl.BlockSpec((1, 1, 2 * tq, tq), lambda b, h, i: (jnp.minimum(i, 1), h, 0, 0)),
            pl.BlockSpec((W, 1), lambda b, h, i: (0, 0)),
        ],
        out_specs=pl.BlockSpec((1, tq, W), lambda b, h, i: (b, i, h)),
        out_shape=jax.ShapeDtypeStruct((Bn, S, A_WIDTH), F32),
        compiler_params=_cparams("arbitrary", "arbitrary", "arbitrary"),
        name="diff_attn",
    )(lam_par, pa, pa, vt, band_t, subln_g.reshape(W, 1))


def _head_ones(n):
    r = lax.broadcasted_iota(jnp.int32, (n, n), 0) // B_HEAD_DIM
    c = lax.broadcasted_iota(jnp.int32, (n, n), 1) // B_HEAD_DIM
    return (r == c).astype(BF16)


def _rwkv_prep_kernel(pb_ref, prev_ref, mu_ref, w0_ref, w2_ref, a0_ref, a2_ref, g2_ref,
                      kk_ref, ka_ref, rk_ref,
                      rt_ref, at_ref, kt_ref, bt_ref, v_ref, wc_ref, bonus_ref, g_ref, *, tm):
    i = pl.program_id(1)
    C = RWKV_CHUNK
    x = pb_ref[0]
    row = lax.broadcasted_iota(jnp.int32, x.shape, 0)
    last = prev_ref[0, 7:8, :] * (i > 0).astype(F32)
    prev = jnp.where(row == 0, last, pltpu.roll(x, 1, 0))
    p = x + (prev - x) * mu_ref[...]
    o1, o2, o3 = B_WIDTH, 2 * B_WIDTH, 3 * B_WIDTH
    r, k, v = p[:, :o1], p[:, o1:o2], p[:, o2:o3]
    lora = p[:, o3:o3 + B_DECAY_LORA + B_AAA_LORA]
    gd = p[:, o3 + B_DECAY_LORA + B_AAA_LORA:]

    z = -(w0_ref[...] + _mm3(jnp.tanh(lora), w2_ref[...]))
    softplus = jnp.maximum(z, 0.0) + jnp.log(1.0 + jnp.exp(-jnp.abs(z)))
    logw = -jnp.exp(-softplus - 0.5)
    a = _sigmoid(a0_ref[...] + _mm3(lora, a2_ref[...]))
    g_ref[0] = _mm3(_sigmoid(gd), g2_ref[...])

    ones = _head_ones(B_WIDTH)
    kk = k * kk_ref[...]
    kk = kk * lax.rsqrt(jnp.maximum(_mm2(kk * kk, ones), 1e-24))
    k2 = k * (1.0 + (a - 1.0) * ka_ref[...])
    bonus_ref[0] = _mm2(r * k2 * rk_ref[...], ones) * v

    t_in = lax.broadcasted_iota(jnp.int32, (tm, B_WIDTH), 0) % C
    cum = logw
    sh = 1
    while sh < C:
        cum = cum + jnp.where(t_in >= sh, pltpu.roll(cum, sh, 0), 0.0)
        sh *= 2
    n = tm // C
    wc_ref[0] = jnp.exp(jnp.sum(logw.reshape(n, C, B_WIDTH), axis=1))
    e_pos = jnp.exp(cum)
    e_neg = jnp.exp(-cum)
    rt_ref[0] = r * e_pos
    at_ref[0] = -kk * jnp.exp(cum - logw)
    kt_ref[0] = k2 * e_neg
    bt_ref[0] = kk * a * e_neg
    v_ref[0] = v


def _rwkv_prep(pbc, mu, w0, w2p, a0, a2p, g2, k_k, k_a, r_k, tm):
    Bn, S, _ = pbc.shape
    W = B_WIDTH
    nl = B_DECAY_LORA + B_AAA_LORA
    row = lambda a: a.reshape(1, -1)
    full = lambda shp: pl.BlockSpec(shp, lambda b, i: (0,) * len(shp))
    seq = pl.BlockSpec((1, tm, W), lambda b, i: (b, i, 0))
    seq_shape = jax.ShapeDtypeStruct((Bn, S, W), F32)
    n = tm // RWKV_CHUNK
    return pl.pallas_call(
        functools.partial(_rwkv_prep_kernel, tm=tm),
        grid=(Bn, S // tm),
        in_specs=[
            pl.BlockSpec((1, tm, B_COLS), lambda b, i: (b, i, 0)),
            pl.BlockSpec((1, 8, B_COLS), lambda b, i: (b, jnp.maximum(i * (tm // 8) - 1, 0), 0)),
            full((1, B_COLS)), full((1, W)), full((nl, W)), full((1, W)), full((nl, W)),
            full((B_GATE_LORA, W)), full((1, W)), full((1, W)), full((1, W)),
        ],
        out_specs=[seq, seq, seq, seq, seq,
                   pl.BlockSpec((1, n, W), lambda b, i: (b, i, 0)), seq, seq],
        out_shape=[seq_shape] * 5 + [jax.ShapeDtypeStruct((Bn, S // RWKV_CHUNK, W), F32)] + [seq_shape] * 2,
        compiler_params=_cparams("arbitrary", "arbitrary"),
        name="rwkv_prep",
    )(pbc, pbc, row(mu), row(w0), w2p, row(a0), a2p, g2, row(k_k), row(k_a), row(r_k))


def _rwkv_scan_kernel(rt_ref, at_ref, kt_ref, bt_ref, v_ref, wc_ref, bonus_ref, g_ref,
                      lng_ref, lnb_ref, o_ref, state, *, tt):
    C = RWKV_CHUNK
    W = B_WIDTH

    @pl.when(pl.program_id(1) == 0)
    def _():
        state[...] = jnp.zeros_like(state)

    lane_head = lax.broadcasted_iota(jnp.int32, (C, W), 1) // B_HEAD_DIM
    tt_i = lax.broadcasted_iota(jnp.int32, (C, W), 0)
    ss_i = lax.broadcasted_iota(jnp.int32, (C, W), 1) % C
    strict = tt_i > ss_i
    incl = tt_i >= ss_i
    eye = (tt_i == ss_i).astype(F32)
    ones = _head_ones(W)
    bd_mask = ones.astype(F32)

    head_mask = [(lane_head == h).astype(BF16) for h in range(B_HEADS)]

    def bd_split(x):
        xb = x.astype(BF16)
        return jnp.concatenate([xb * mk for mk in head_mask], axis=0)

    def mm_bd(a, b_bd, dims=NN):
        return lax.dot_general(a.astype(BF16), b_bd, dims, preferred_element_type=F32)

    def state_free(gi, nb):
        G = range(RWKV_GROUP)
        sls = [pl.ds(pl.multiple_of((gi * RWKV_GROUP + j) * C, C), C) for j in G]
        rt = [rt_ref[nb, sl, :] for sl in sls]
        at = [at_ref[nb, sl, :] for sl in sls]
        kt = [kt_ref[nb, sl, :] for sl in sls]
        bt = [bt_ref[nb, sl, :] for sl in sls]
        v = [v_ref[nb, sl, :] for sl in sls]
        wc = [wc_ref[nb, pl.ds(gi * RWKV_GROUP + j, 1), :] for j in G]
        ar = [jnp.concatenate([at[j], rt[j]], axis=0) for j in G]
        bdb = [bd_split(bt[j]) for j in G]
        bdk = [bd_split(kt[j]) for j in G]
        a_b = [mm_bd(ar[j], bdb[j], NT) for j in G]
        a_k = [mm_bd(ar[j], bdk[j], NT) for j in G]
        lo = [jnp.where(strict, a_b[j][:C], 0.0) for j in G]
        a_ak = [jnp.where(strict, a_k[j][:C], 0.0) for j in G]
        a_rb = [jnp.where(incl, a_b[j][C:], 0.0) for j in G]
        a_rk = [jnp.where(incl, a_k[j][C:], 0.0) for j in G]
        pw = lo
        tinv = [eye + lo[j] for j in G]
        bdp = [bd_split(pw[j]) for j in G]
        span = 2
        while span < C:
            pw = [mm_bd(pw[j], bdp[j]) for j in G]
            bdp = [bd_split(pw[j]) for j in G]
            tinv = [tinv[j] + mm_bd(tinv[j], bdp[j]) for j in G]
            span *= 2
        bdv = [bd_split(v[j]) for j in G]
        bda = [bd_split(at[j]) for j in G]
        abar = [mm_bd(tinv[j], bda[j]) for j in G]
        akv = [bd_split(mm_bd(a_ak[j], bdv[j])) for j in G]
        u0 = [mm_bd(tinv[j], akv[j]) for j in G]
        y0 = [mm_bd(a_rk[j], bdv[j]) for j in G]
        kv = [_mm(v[j], kt[j] * wc[j], TN) * bd_mask for j in G]
        return [(jnp.concatenate([abar[j], rt[j]], axis=0), u0[j], y0[j], a_rb[j], bt[j] * wc[j], kv[j], wc[j])
                for j in G]

    def group(gi, carry):
        seqs = range(rt_ref.shape[0])
        pre = [state_free(gi, nb) for nb in seqs]
        s = [state[nb] for nb in seqs]
        ys = [[] for _ in seqs]
        for j in range(RWKV_GROUP):
            for nb in seqs:
                abar_rt, u0, y0, a_rb, btw, kv, wc = pre[nb][j]
                a_s = _mm(abar_rt, s[nb], NT)
                u = a_s[:C] + u0
                ys[nb].append(a_s[C:] + y0 + mm_bd(a_rb, bd_split(u)))
                s[nb] = s[nb] * wc + _mm(u, btw, TN) * bd_mask + kv
        sl = pl.ds(pl.multiple_of(gi * (RWKV_GROUP * C), RWKV_GROUP * C), RWKV_GROUP * C)
        for nb in seqs:
            state[nb] = s[nb]
            y = jnp.concatenate(ys[nb], axis=0)
            mean = _mm2(y, ones) * (1.0 / B_HEAD_DIM)
            d = y - mean
            var = _mm2(d * d, ones) * (1.0 / B_HEAD_DIM)
            yn = d * lax.rsqrt(var + B_LNX_EPS) * lng_ref[...] + lnb_ref[...]
            o_ref[nb, sl, :] = (yn + bonus_ref[nb, sl, :]) * g_ref[nb, sl, :]
        return carry

    lax.fori_loop(0, tt // (RWKV_GROUP * C), group, 0)


def _rwkv_scan(rt, at, kt, bt, v, wc, bonus, g, lnx_g, lnx_b, tt):
    Bn, S, W = rt.shape
    n = tt // RWKV_CHUNK
    nseq = 2 if Bn % 2 == 0 else 1
    seq = pl.BlockSpec((nseq, tt, W), lambda b, i: (b, i, 0))
    vec = pl.BlockSpec((1, W), lambda b, i: (0, 0))
    return pl.pallas_call(
        functools.partial(_rwkv_scan_kernel, tt=tt),
        grid=(Bn // nseq, S // tt),
        in_specs=[seq, seq, seq, seq, seq, pl.BlockSpec((nseq, n, W), lambda b, i: (b, i, 0)), seq, seq, vec, vec],
        out_specs=seq,
        out_shape=jax.ShapeDtypeStruct((Bn, S, W), F32),
        scratch_shapes=[pltpu.VMEM((nseq, B_HEADS * B_HEAD_DIM, W), F32)],
        compiler_params=_cparams("arbitrary", "arbitrary"),
        name="rwkv_scan",
    )(rt, at, kt, bt, v, wc, bonus, g, lnx_g.reshape(1, W), lnx_b.reshape(1, W))


def _gmlp_kernel(pc_ref, lng_ref, lnb_ref, ws_ref, bs_ref, o_ref, *, tm):
    x = pc_ref[0]
    z = x * (0.5 * (1.0 + jnp.tanh(math.sqrt(2.0 / math.pi) * (x + 0.044715 * (x * x * x)))))
    u, v = z[:, :C_WIDTH], z[:, C_WIDTH:]
    mu = jnp.mean(v, axis=-1, keepdims=True)
    d = v - mu
    var = jnp.mean(d * d, axis=-1, keepdims=True)
    vn = d * lax.rsqrt(var + LN_EPS) * lng_ref[...] + lnb_ref[...]
    group = lax.broadcasted_iota(jnp.int32, (CHUNK, C_WIDTH), 1) // C_GROUP_DIM
    tril = (lax.broadcasted_iota(jnp.int32, (CHUNK, CHUNK), 0)
            >= lax.broadcasted_iota(jnp.int32, (CHUNK, CHUNK), 1))
    ws = [jnp.where(tril, ws_ref[gi], 0.0).astype(BF16) for gi in range(C_GROUPS)]
    for c in range(tm // CHUNK):
        sl = slice(c * CHUNK, (c + 1) * CHUNK)
        vc = vn[sl].astype(BF16)
        sv = bs_ref[...]
        for gi in range(C_GROUPS):
            t = jnp.dot(ws[gi], vc, preferred_element_type=F32)
            sv = sv + jnp.where(group == gi, t, 0.0)
        o_ref[0, sl, :] = u[sl] * sv


def _gmlp(pbc, ln_g, ln_b, w_s, b_s, tm):
    Bn, S, _ = pbc.shape
    bs_wide = jnp.repeat(jnp.transpose(b_s), C_GROUP_DIM, axis=1)
    return pl.pallas_call(
        functools.partial(_gmlp_kernel, tm=tm),
        grid=(Bn, S // tm),
        in_specs=[
            pl.BlockSpec((1, tm, C_COLS), lambda b, i: (b, i, B_COLS // C_COLS)),
            pl.BlockSpec((1, C_WIDTH), lambda b, i: (0, 0)),
            pl.BlockSpec((1, C_WIDTH), lambda b, i: (0, 0)),
            pl.BlockSpec((C_GROUPS, CHUNK, CHUNK), lambda b, i: (0, 0, 0)),
            pl.BlockSpec((CHUNK, C_WIDTH), lambda b, i: (0, 0)),
        ],
        out_specs=pl.BlockSpec((1, tm, C_WIDTH), lambda b, i: (b, i, 0)),
        out_shape=jax.ShapeDtypeStruct((Bn, S, C_WIDTH), F32),
        compiler_params=_cparams("arbitrary", "arbitrary"),
        name="gmlp",
    )(pbc, ln_g.reshape(1, -1), ln_b.reshape(1, -1), w_s, bs_wide)


def _mid_kernel(ya_ref, yb_ref, yc_ref, x_ref, woa_ref, wob_ref, woc_ref, gpost_ref, g1_ref,
                gpre_ref, sc_ref, sh_ref, wr_ref, xo_ref, h_ref, score_ref):
    y = (_mm(ya_ref[0], woa_ref[...]) + _mm(yb_ref[0], wob_ref[...]) + _mm(yc_ref[0], woc_ref[...]))
    xn = x_ref[0] + g1_ref[0] * (_rms(y) * gpost_ref[...])
    xo_ref[0] = xn
    h = _rms(xn) * gpre_ref[...] * (1.0 + sc_ref[0]) + sh_ref[0]
    h_ref[0] = _pack_bf16_pair(h)
    score_ref[0] = _sigmoid(_mm3(wr_ref[...], h, NT))


def _mid(ya, yb, yc, x, woa, wob, woc, gpost, g1, gpre, sc, sh, wr, tm):
    Bn, S, D = x.shape
    NR = wr.shape[0]
    seq = lambda w: pl.BlockSpec((1, tm, w), lambda b, i: (b, i, 0))
    full = lambda shp: pl.BlockSpec(shp, lambda b, i: (0,) * len(shp))
    per_b = pl.BlockSpec((1, 1, D), lambda b, i: (b, 0, 0))
    return pl.pallas_call(
        _mid_kernel,
        grid=(Bn, S // tm),
        in_specs=[seq(A_WIDTH), seq(B_WIDTH), seq(C_WIDTH), seq(D),
                  full((A_WIDTH, D)), full((B_WIDTH, D)), full((C_WIDTH, D)),
                  full((1, D)), per_b, full((1, D)), per_b, per_b,
                  full((NR, D))],
        out_specs=[seq(D), seq(D // 2), pl.BlockSpec((1, NR, tm), lambda b, i: (b, 0, i))],
        out_shape=[jax.ShapeDtypeStruct((Bn, S, D), F32), jax.ShapeDtypeStruct((Bn, S, D // 2), jnp.int32),
                   jax.ShapeDtypeStruct((Bn, NR, S), F32)],
        compiler_params=_cparams("arbitrary", "arbitrary"),
        name="mid",
    )(ya, yb, yc, x, woa, wob, woc, gpost.reshape(1, D), g1, gpre.reshape(1, D), sc, sh, wr)


def _shared_expert_kernel(h_ref, ws1_ref, ws3_ref, ws2_ref, o_ref):
    lo, hi = _unpack_bf16_pair(h_ref[0])
    hb = jnp.concatenate([lo, hi], axis=1).astype(BF16)
    t = _silu(jnp.dot(hb, ws1_ref[...], preferred_element_type=F32)) * jnp.dot(
        hb, ws3_ref[...], preferred_element_type=F32)
    o_ref[0] = jnp.dot(t.astype(BF16), ws2_ref[...], preferred_element_type=F32)


def _shared_expert(hp, ws1, ws3, ws2, tm):
    Bn, S, DP = hp.shape
    D, F = ws1.shape
    full = lambda shp: pl.BlockSpec(shp, lambda b, i: (0,) * len(shp))
    return pl.pallas_call(
        _shared_expert_kernel,
        grid=(Bn, S // tm),
        in_specs=[pl.BlockSpec((1, tm, DP), lambda b, i: (b, i, 0)), full((D, F)), full((D, F)), full((F, D))],
        out_specs=pl.BlockSpec((1, tm, D), lambda b, i: (b, i, 0)),
        out_shape=jax.ShapeDtypeStruct((Bn, S, D), F32),
        compiler_params=_cparams("arbitrary", "arbitrary"),
        name="shared_expert",
    )(hp, ws1, ws3, ws2)


def _first_argmax(vals, iota, n):
    m = jnp.max(vals, axis=0, keepdims=True)
    idx = jnp.min(jnp.where(vals == m, iota, n), axis=0, keepdims=True)
    return m, idx


def _route_kernel(sc_ref, bias_ref, e_ref, w_ref, r_ref, cnt_ref, carry, *, tm):
    @pl.when((pl.program_id(0) == 0) & (pl.program_id(1) == 0))
    def _():
        carry[...] = jnp.zeros_like(carry)

    G = EXPERTS_PER_GROUP
    s = sc_ref[0]
    biased = s + bias_ref[...]
    neg_inf = jnp.float32(-jnp.inf)
    io8 = lax.broadcasted_iota(jnp.int32, (G, tm), 0)
    gs_rows = []
    for g in range(N_GROUPS):
        blk = biased[g * G:(g + 1) * G]
        m1, i1 = _first_argmax(blk, io8, G)
        m2 = jnp.max(jnp.where(io8 == i1, neg_inf, blk), axis=0, keepdims=True)
        gs_rows.append(m1 + m2)
    gs = jnp.concatenate(gs_rows, axis=0)
    gio = lax.broadcasted_iota(jnp.int32, (N_GROUPS, tm), 0)
    gsel = jnp.zeros((N_GROUPS, tm), jnp.bool_)
    for _ in range(TOPK_GROUPS):
        _, gi = _first_argmax(gs, gio, N_GROUPS)
        pick = gio == gi
        gsel = gsel | pick
        gs = jnp.where(pick, neg_inf, gs)
    masked = jnp.concatenate(
        [jnp.where(gsel[g:g + 1], biased[g * G:(g + 1) * G], neg_inf) for g in range(N_GROUPS)], axis=0)

    eio = lax.broadcasted_iota(jnp.int32, (N_EXPERTS, tm), 0)
    picks, e_rows, s_rows = [], [], []
    for _ in range(TOP_K):
        _, ei = _first_argmax(masked, eio, N_EXPERTS)
        pick = eio == ei
        picks.append(pick)
        e_rows.append(ei)
        s_rows.append(jnp.sum(jnp.where(pick, s, 0.0), axis=0, keepdims=True))
        masked = jnp.where(pick, neg_inf, masked)
    top_s = jnp.concatenate(s_rows, axis=0)
    w_ref[...] = top_s / (jnp.sum(top_s, axis=0, keepdims=True) + 1e-20) * ROUTED_SCALE
    e_ref[...] = jnp.concatenate(e_rows, axis=0)

    sel = jnp.zeros((N_EXPERTS, tm), F32)
    for pick in picks:
        sel = sel + pick.astype(F32)
    before = (lax.broadcasted_iota(jnp.int32, (tm, tm), 0) < lax.broadcasted_iota(jnp.int32, (tm, tm), 1))
    pos = carry[...] + jnp.dot(sel.astype(BF16), before.astype(BF16), preferred_element_type=F32)
    r_ref[...] = jnp.concatenate(
        [jnp.sum(jnp.where(pick, pos, 0.0), axis=0, keepdims=True) for pick in picks], axis=0).astype(jnp.int32)
    total = carry[...] + jnp.sum(sel, axis=1, keepdims=True)
    carry[...] = total
    cnt_ref[...] = jnp.broadcast_to(total, cnt_ref.shape).astype(jnp.int32)


def _route(scores_t, e_bias, tm):
    Bn, _, S = scores_t.shape
    T = Bn * S
    nt = S // tm
    tok = pl.BlockSpec((TOP_K, tm), lambda b, i: (0, b * nt + i))
    return pl.pallas_call(
        functools.partial(_route_kernel, tm=tm),
        grid=(Bn, nt),
        in_specs=[pl.BlockSpec((1, N_EXPERTS, tm), lambda b, i: (b, 0, i)),
                  pl.BlockSpec((N_EXPERTS, 1), lambda b, i: (0, 0))],
        out_specs=[tok, tok, tok, pl.BlockSpec((N_EXPERTS, V7X_LANES), lambda b, i: (0, 0))],
        out_shape=[jax.ShapeDtypeStruct((TOP_K, T), jnp.int32), jax.ShapeDtypeStruct((TOP_K, T), F32),
                   jax.ShapeDtypeStruct((TOP_K, T), jnp.int32),
                   jax.ShapeDtypeStruct((N_EXPERTS, V7X_LANES), jnp.int32)],
        scratch_shapes=[pltpu.VMEM((N_EXPERTS, 1), F32)],
        compiler_params=_cparams("arbitrary", "arbitrary"),
        name="route",
    )(scores_t, e_bias.reshape(N_EXPERTS, 1))


def _dest_kernel(start_ref, e_ref, r_ref, o_ref):
    e = e_ref[...]
    acc = r_ref[...]
    for ex in range(N_EXPERTS):
        acc = acc + jnp.where(e == ex, start_ref[ex], 0)
    o_ref[0] = acc


def _dest_rows(pad_start, eidx, rank, tt):
    K_, T = eidx.shape
    grid_spec = pltpu.PrefetchScalarGridSpec(
        num_scalar_prefetch=1,
        grid=(T // tt,),
        in_specs=[pl.BlockSpec((K_, tt), lambda i, st: (0, i)), pl.BlockSpec((K_, tt), lambda i, st: (0, i))],
        out_specs=pl.BlockSpec((1, K_, tt), lambda i, st: (i, 0, 0)),
    )
    return pl.pallas_call(
        _dest_kernel,
        grid_spec=grid_spec,
        out_shape=jax.ShapeDtypeStruct((T // tt, K_, tt), jnp.int32),
        compiler_params=_cparams("arbitrary"),
        name="dest_rows",
    )(pad_start, eidx, rank)


def _expert_kernel(blk_e_ref, n_used_ref, n_valid_ref, x_ref, w1_ref, w3_ref, w2_ref, o_ref, w1b, w3b, w2b):
    i = pl.program_id(0)

    @pl.when((i == 0) | (blk_e_ref[i] != blk_e_ref[jnp.maximum(i - 1, 0)]))
    def _():
        w1b[...] = w1_ref[0].astype(BF16)
        w3b[...] = w3_ref[0].astype(BF16)
        w2b[...] = w2_ref[0].astype(BF16)

    @pl.when(i < n_used_ref[0])
    def _():
        row = lax.broadcasted_iota(jnp.int32, x_ref.shape, 0)
        x_lo, x_hi = _unpack_bf16_pair(jnp.where(row < n_valid_ref[i], x_ref[...], 0))
        x_lo, x_hi = x_lo.astype(BF16), x_hi.astype(BF16)
        half = x_lo.shape[1]

        def up(wb):
            return (jnp.dot(x_lo, wb[:half, :], preferred_element_type=F32)
                    + jnp.dot(x_hi, wb[half:, :], preferred_element_type=F32))

        t = _silu(up(w1b)) * up(w3b)
        o_ref[...] = _pack_bf16_pair(jnp.dot(t.astype(BF16), w2b[...], preferred_element_type=F32))


def _experts(blk_e, n_used, n_valid, xs, w1, w3, w2, layer):
    P, DP = xs.shape
    EB = EXPERT_BLOCK
    n_blocks = blk_e.shape[0]
    D, F = w1.shape[2], w1.shape[3]
    rows = pl.BlockSpec((EB, DP), lambda i, be, nu, nv: (jnp.minimum(i, nu[0] - 1), 0))
    grid_spec = pltpu.PrefetchScalarGridSpec(
        num_scalar_prefetch=3,
        grid=(n_blocks,),
        in_specs=[
            rows,
            pl.BlockSpec((None, 1, D, F), lambda i, be, nu, nv: (layer, be[i], 0, 0)),
            pl.BlockSpec((None, 1, D, F), lambda i, be, nu, nv: (layer, be[i], 0, 0)),
            pl.BlockSpec((None, 1, F, D), lambda i, be, nu, nv: (layer, be[i], 0, 0)),
        ],
        out_specs=rows,
        scratch_shapes=[pltpu.VMEM((D, F), BF16), pltpu.VMEM((D, F), BF16), pltpu.VMEM((F, D), BF16)],
    )
    return pl.pallas_call(
        _expert_kernel,
        grid_spec=grid_spec,
        out_shape=jax.ShapeDtypeStruct((P, DP), jnp.int32),
        compiler_params=_cparams("arbitrary"),
        name="experts",
    )(blk_e, n_used, n_valid, xs, w1, w3, w2)


def _block_layout(counts, n_blocks):
    EB = EXPERT_BLOCK
    padded = (counts + EB - 1) // EB * EB
    ex = jnp.arange(N_EXPERTS)
    pad_end = jnp.sum(jnp.where(ex[:, None] <= ex[None, :], padded[:, None], 0), axis=0)
    pad_start = pad_end - padded
    blk_row = (jnp.arange(n_blocks) * EB)[:, None]
    blk_e = jnp.minimum(jnp.sum((pad_end[None, :] <= blk_row).astype(jnp.int32), axis=1), N_EXPERTS - 1)
    n_used = (jnp.sum(padded) // EB).astype(jnp.int32).reshape(1)
    mine = (pad_start[None, :] <= blk_row) & (blk_row < pad_end[None, :])
    n_valid = jnp.sum(jnp.where(mine, jnp.clip(counts[None, :] - (blk_row - pad_start[None, :]), 0, EB), 0), axis=1)
    return pad_start.astype(jnp.int32), blk_e.astype(jnp.int32), n_used, n_valid.astype(jnp.int32)


SC_GATHER_ROWS = 64


def _sc_gather_rows(table, idx):
    info = plsc.get_sparse_core_info()
    nc, ns = info.num_cores, info.num_subcores
    M = idx.shape[0]
    W = table.shape[1]
    b = SC_GATHER_ROWS
    per_worker = M // (nc * ns)
    steps = per_worker // b
    assert per_worker * nc * ns == M and steps * b == per_worker and steps % 2 == 0
    mesh = plsc.VectorSubcoreMesh(core_axis_name="c", subcore_axis_name="s")

    @functools.partial(
        pl.kernel, mesh=mesh,
        out_type=jax.ShapeDtypeStruct((M, W), table.dtype),
        scratch_types=[pltpu.VMEM((2, b), jnp.int32), pltpu.VMEM((2, b, W), table.dtype),
                       pltpu.SemaphoreType.DMA, pltpu.SemaphoreType.DMA],
        name="sc_gather_rows",
    )
    def gather(table_hbm, idx_hbm, out_hbm, idx_v, rows_v, sem0, sem1):
        wid = lax.axis_index("s") * nc + lax.axis_index("c")
        sems = (sem0, sem1)

        def base(s):
            return pl.multiple_of(wid * per_worker + s * b, b)

        def gather_copy(slot):
            return pltpu.make_async_copy(table_hbm.at[idx_v.at[slot]], rows_v.at[slot], sems[slot])

        def start(s, slot):
            pltpu.sync_copy(idx_hbm.at[pl.ds(base(s), b)], idx_v.at[slot])
            gather_copy(slot).start()

        def finish(s, slot):
            gather_copy(slot).wait()
            pltpu.sync_copy(rows_v.at[slot], out_hbm.at[pl.ds(base(s), b)])

        start(0, 0)

        @pl.loop(0, steps, step=2)
        def _(s):
            start(s + 1, 1)
            finish(s, 0)

            @pl.when(s + 2 < steps)
            def _():
                start(s + 2, 0)

            finish(s + 1, 1)

    return gather(table, idx)


def _sc_scatter_rows(rows, idx, n_out):
    info = plsc.get_sparse_core_info()
    nc, ns = info.num_cores, info.num_subcores
    T, W = rows.shape
    G, K_, b = idx.shape
    steps = G // (nc * ns)
    assert steps * nc * ns == G and G * b == T
    mesh = plsc.VectorSubcoreMesh(core_axis_name="c", subcore_axis_name="s")

    @functools.partial(
        pl.kernel, mesh=mesh,
        out_type=jax.ShapeDtypeStruct((n_out, W), rows.dtype),
        scratch_types=[pltpu.VMEM((K_, b), jnp.int32), pltpu.VMEM((b, W), rows.dtype), pltpu.SemaphoreType.DMA],
        name="sc_scatter_rows",
    )
    def scatter(rows_hbm, idx_hbm, out_hbm, idx_v, rows_v, sem):
        wid = lax.axis_index("s") * nc + lax.axis_index("c")

        @pl.loop(0, steps)
        def _(s):
            g = wid * steps + s
            pltpu.sync_copy(idx_hbm.at[g], idx_v)
            pltpu.sync_copy(rows_hbm.at[pl.ds(pl.multiple_of(g * b, b), b)], rows_v)
            copies = [pltpu.async_copy(rows_v, out_hbm.at[idx_v.at[k]], sem) for k in range(K_)]
            for cp in copies:
                cp.wait()

    return scatter(rows, idx)


def _combine_dense_kernel(rows_ref, w_ref, x_ref, shared_ref, gpost_ref, g2_ref, o_ref):
    w = w_ref[...]
    tt, half = rows_ref.shape[1], rows_ref.shape[2]
    y_lo = jnp.zeros((tt, half), F32)
    y_hi = jnp.zeros((tt, half), F32)
    for k in range(TOP_K):
        lo, hi = _unpack_bf16_pair(rows_ref[k])
        y_lo = y_lo + w[:, k:k + 1] * lo
        y_hi = y_hi + w[:, k:k + 1] * hi
    y = shared_ref[0] + jnp.concatenate([y_lo, y_hi], axis=1)
    o_ref[0] = x_ref[0] + g2_ref[0] * (_rms(y) * gpost_ref[...])


def _combine_dense(rows, w_tok, x, shared, gpost, g2, tt):
    Bn, S, D = x.shape
    K_, T, DP = rows.shape
    nt = S // tt
    seq = pl.BlockSpec((1, tt, D), lambda b, i: (b, i, 0))
    return pl.pallas_call(
        _combine_dense_kernel,
        grid=(Bn, nt),
        in_specs=[pl.BlockSpec((K_, tt, DP), lambda b, i: (0, b * nt + i, 0)),
                  pl.BlockSpec((tt, K_), lambda b, i: (b * nt + i, 0)), seq, seq,
                  pl.BlockSpec((1, D), lambda b, i: (0, 0)), pl.BlockSpec((1, 1, D), lambda b, i: (b, 0, 0))],
        out_specs=seq,
        out_shape=jax.ShapeDtypeStruct((Bn, S, D), F32),
        compiler_params=_cparams("arbitrary", "arbitrary"),
        name="combine_dense",
    )(rows, w_tok, x, shared, gpost.reshape(1, D), g2)


def kernel(x, c, w_ada, b_ada, norm_pre_mix, norm_post_mix, norm_pre_ffn, norm_post_ffn, w_in, w_out, rel_bias_table, diff_lambda, diff_subln, rwkv_mu, rwkv_w0, rwkv_w2, rwkv_a0, rwkv_a2, rwkv_g2, rwkv_k_k, rwkv_k_a, rwkv_r_k, rwkv_lnx_g, rwkv_lnx_b, gmlp_ln_g, gmlp_ln_b, gmlp_w_s, gmlp_b_s, router_w, router_bias, exp_w1, exp_w3, exp_w2, shared_w1, shared_w3, shared_w2):
    Bn, S, D = x.shape
    depth = w_ada.shape[0]
    tm = min(256, S)
    tm_wide = min(512, S)
    tq = min(512, S // 2)
    t_rwkv = min(512, S)

    mod = _adaln(c, w_ada, b_ada)
    band_t = _attn_band(rel_bias_table, tq)
    zpad = jnp.zeros((B_DECAY_LORA, B_WIDTH), F32)
    for l in range(depth):
        sh1, sc1, g1, sh2, sc2, g2 = [m.reshape(Bn, 1, D) for m in jnp.split(mod[l], 6, axis=-1)]
        w_in_b = w_in[l].astype(BF16)
        pa, vt, pbc = _inproj(x, norm_pre_mix[l], sc1, sh1, w_in_b[:, :2 * A_WIDTH],
                              jnp.transpose(w_in_b[:, 2 * A_WIDTH:A_COLS]), w_in_b[:, A_COLS:], tm_wide)
        lambda_init = 0.8 - 0.6 * math.exp(-0.3 * l)
        ya = _diff_attention(pa, vt, band_t, diff_lambda[l], diff_subln[l], lambda_init, tq)
        prep = _rwkv_prep(pbc, rwkv_mu[l], rwkv_w0[l], jnp.concatenate([rwkv_w2[l], zpad], axis=0),
                          rwkv_a0[l], jnp.concatenate([zpad, rwkv_a2[l]], axis=0), rwkv_g2[l],
                          rwkv_k_k[l], rwkv_k_a[l], rwkv_r_k[l].reshape(-1), t_rwkv)
        yb = _rwkv_scan(*prep, rwkv_lnx_g[l], rwkv_lnx_b[l], t_rwkv)
        yc = _gmlp(pbc, gmlp_ln_g[l], gmlp_ln_b[l], gmlp_w_s[l], gmlp_b_s[l], tm_wide)

        w_out_b = w_out[l].astype(BF16)
        wr_t = jnp.pad(jnp.transpose(router_w[l]), ((0, V7X_LANES - N_EXPERTS), (0, 0)))
        x, h, scores_t = _mid(
            ya, yb, yc, x, w_out_b[:A_WIDTH], w_out_b[A_WIDTH:A_WIDTH + B_WIDTH], w_out_b[A_WIDTH + B_WIDTH:],
            norm_post_mix[l], g1, norm_pre_ffn[l], sc2, sh2, wr_t, tm_wide)

        T = Bn * S
        n_blocks = -(-T * TOP_K // EXPERT_BLOCK) + N_EXPERTS
        eidx, wgt, rank, cnt = _route(scores_t, router_bias[l], tm_wide)
        pad_start, blk_e, n_used, n_valid = _block_layout(cnt[:, 0], n_blocks)
        dest = _dest_rows(pad_start, eidx, rank, tm)
        b = SC_GATHER_ROWS
        dest_sc = jnp.transpose(dest.reshape(T // tm, TOP_K, tm // b, b), (0, 2, 1, 3)).reshape(T // b, TOP_K, b)
        xs = _sc_scatter_rows(h.reshape(T, D // 2), dest_sc, n_blocks * EXPERT_BLOCK)
        shared = _shared_expert(h, shared_w1[l].astype(BF16), shared_w3[l].astype(BF16),
                                shared_w2[l].astype(BF16), tm_wide)
        ys = _experts(blk_e, n_used, n_valid, xs, exp_w1, exp_w3, exp_w2, l)
        dest_kt = jnp.transpose(dest, (1, 0, 2)).reshape(TOP_K * T)
        rows = _sc_gather_rows(ys, dest_kt).reshape(TOP_K, T, D // 2)
        x = _combine_dense(rows, jnp.transpose(wgt), x, shared, norm_post_ffn[l], g2, tm_wide)
    return x
```

```python
import functools
import math

import jax
import jax.numpy as jnp
from jax import lax
from jax.experimental import pallas as pl
from jax.experimental.pallas import tpu as pltpu
from jax.experimental.pallas import tpu_sc as plsc

F32 = jnp.float32
BF16 = jnp.bfloat16

A_HEADS = 4
A_QK_DIM = 64
A_HEAD_W = 2 * A_QK_DIM
A_WIDTH = A_HEADS * A_HEAD_W
N_BUCKETS = 32
MAX_DISTANCE = 128
B_HEADS = 4
B_HEAD_DIM = 64
B_WIDTH = B_HEADS * B_HEAD_DIM
B_DECAY_LORA = 64
B_AAA_LORA = 64
B_GATE_LORA = 128
B_LNX_EPS = 64e-5
C_GROUPS = 4
C_GROUP_DIM = 64
C_WIDTH = C_GROUPS * C_GROUP_DIM
CHUNK = 128
A_COLS = 3 * A_WIDTH
B_COLS = 3 * B_WIDTH + B_DECAY_LORA + B_AAA_LORA + B_GATE_LORA
C_COLS = 2 * C_WIDTH
N_EXPERTS = 64
TOP_K = 8
N_GROUPS = 8
TOPK_GROUPS = 4
EXPERTS_PER_GROUP = N_EXPERTS // N_GROUPS
ROUTED_SCALE = 2.5
EXPERT_BLOCK = 512
RMS_EPS = 1e-6
LN_EPS = 1e-5
NEG_BIG = -1e30

V7X_LANES = 128
BF16_SUBLANES = 16
V7X_VMEM_LIMIT_BYTES = 56 * 1024 * 1024
RWKV_CHUNK = 64
RWKV_GROUP = 8

NN = (((1,), (0,)), ((), ()))
NT = (((1,), (1,)), ((), ()))
TN = (((0,), (0,)), ((), ()))


def _cparams(*sem):
    return pltpu.CompilerParams(dimension_semantics=sem, vmem_limit_bytes=V7X_VMEM_LIMIT_BYTES)


def _mm(a, b, dims=NN):
    return lax.dot_general(a.astype(BF16), b.astype(BF16), dims, preferred_element_type=F32)


def _split(a):
    hi = a.astype(BF16)
    lo = (a - hi.astype(F32)).astype(BF16)
    return hi, lo


def _mm3(a, b, dims=NN):
    ah, al = _split(a)
    bh, bl = _split(b)
    d = lambda x, y: lax.dot_general(x, y, dims, preferred_element_type=F32)
    return d(ah, bh) + d(ah, bl) + d(al, bh)


def _mm2(a, b_exact, dims=NN):
    ah, al = _split(a)
    d = lambda x: lax.dot_general(x, b_exact, dims, preferred_element_type=F32)
    return d(ah) + d(al)


def _pack_bf16_pair(x):
    n = x.shape[1] // 2
    bits = lax.bitcast_convert_type(x.astype(BF16).astype(F32), jnp.int32)
    return ((bits[:, :n] >> 16) & 0xFFFF) | bits[:, n:]


def _unpack_bf16_pair(u):
    lo = lax.bitcast_convert_type(u << 16, F32)
    hi = lax.bitcast_convert_type(u & jnp.int32(-65536), F32)
    return lo, hi


def _rms(x, eps=RMS_EPS):
    return x * lax.rsqrt(jnp.mean(x * x, axis=-1, keepdims=True) + eps)


def _sigmoid(x):
    return 1.0 / (1.0 + jnp.exp(-x))


def _silu(x):
    return x * _sigmoid(x)


def _adaln_kernel(c_ref, w_ref, b_ref, o_ref):
    c = c_ref[...]
    o_ref[0] = _mm3(_silu(c), w_ref[0]) + b_ref[0]


def _adaln(c, w_ada, b_ada):
    L, D, N = w_ada.shape
    Bn = c.shape[0]
    tn = min(N, 1536)
    return pl.pallas_call(
        _adaln_kernel,
        grid=(L, N // tn),
        in_specs=[
            pl.BlockSpec((Bn, D), lambda l, j: (0, 0)),
            pl.BlockSpec((1, D, tn), lambda l, j: (l, 0, j)),
            pl.BlockSpec((1, 1, tn), lambda l, j: (l, 0, j)),
        ],
        out_specs=pl.BlockSpec((1, Bn, tn), lambda l, j: (l, 0, j)),
        out_shape=jax.ShapeDtypeStruct((L, Bn, N), F32),
        compiler_params=_cparams("arbitrary", "arbitrary"),
        name="adaln",
    )(c, w_ada, b_ada.reshape(L, 1, N))


def _inproj_kernel(x_ref, g_ref, sc_ref, sh_ref, wa_ref, wvt_ref, wbc_ref, oa_ref, ovt_ref, obc_ref):
    x = x_ref[0]
    h = _rms(x) * g_ref[...] * (1.0 + sc_ref[0]) + sh_ref[0]
    hb = h.astype(BF16)
    oa_ref[0] = jnp.dot(hb, wa_ref[...], preferred_element_type=F32).astype(BF16)
    ovt_ref[0] = lax.dot_general(wvt_ref[...], hb, NT, preferred_element_type=F32).astype(BF16)
    obc_ref[0] = jnp.dot(hb, wbc_ref[...], preferred_element_type=F32)


def _inproj(x, g, sc, sh, wa, wvt, wbc, tm):
    Bn, S, D = x.shape
    na, nv, nbc = wa.shape[1], wvt.shape[0], wbc.shape[1]
    return pl.pallas_call(
        _inproj_kernel,
        grid=(Bn, S // tm),
        in_specs=[
            pl.BlockSpec((1, tm, D), lambda b, i: (b, i, 0)),
            pl.BlockSpec((1, D), lambda b, i: (0, 0)),
            pl.BlockSpec((1, 1, D), lambda b, i: (b, 0, 0)),
            pl.BlockSpec((1, 1, D), lambda b, i: (b, 0, 0)),
            pl.BlockSpec((D, na), lambda b, i: (0, 0)),
            pl.BlockSpec((nv, D), lambda b, i: (0, 0)),
            pl.BlockSpec((D, nbc), lambda b, i: (0, 0)),
        ],
        out_specs=[
            pl.BlockSpec((1, tm, na), lambda b, i: (b, i, 0)),
            pl.BlockSpec((1, nv, tm), lambda b, i: (b, 0, i)),
            pl.BlockSpec((1, tm, nbc), lambda b, i: (b, i, 0)),
        ],
        out_shape=[
            jax.ShapeDtypeStruct((Bn, S, na), BF16),
            jax.ShapeDtypeStruct((Bn, nv, S), BF16),
            jax.ShapeDtypeStruct((Bn, S, nbc), F32),
        ],
        compiler_params=_cparams("arbitrary", "arbitrary"),
        name="inproj",
    )(x, g.reshape(1, D), sc, sh, wa, wvt, wbc)


def _t5_bucket(dist):
    n = jnp.maximum(dist, 0)
    max_exact = N_BUCKETS // 2
    nf = jnp.maximum(n, 1).astype(F32)
    large = max_exact + (jnp.log(nf / max_exact) / math.log(MAX_DISTANCE / max_exact)
                         * (N_BUCKETS - max_exact)).astype(jnp.int32)
    large = jnp.minimum(large, N_BUCKETS - 1)
    return jnp.where(n < max_exact, n, large)


def _attn_band(table, tq):
    far = table[N_BUCKETS - 1].astype(F32)
    H = table.shape[1]
    nb = V7X_LANES
    L = 3 * nb
    m = jnp.arange(L)
    m = jnp.where(m < nb, m, m - L)
    cache = {}

    def block(c):
        if c not in cache:
            if c - (nb - 1) >= MAX_DISTANCE:
                cache[c] = jnp.zeros((H, nb, nb), F32)
            elif c + (nb - 1) < 0:
                cache[c] = jnp.full((H, nb, nb), NEG_BIG, F32)
            else:
                dist = m + c
                vals = jnp.where(dist[None] >= 0,
                                 jnp.transpose(table[_t5_bucket(dist)].astype(F32)) - far[:, None], NEG_BIG)
                cache[c] = jnp.tile(vals, (1, nb))[:, :nb * (L - 1)].reshape(H, nb, L - 1)[:, :, :nb]
        return cache[c]

    bands = []
    for off in (0, tq):
        rows = [jnp.concatenate([block(nb * (a - b) + off) for a in range(tq // nb)], axis=2)
                for b in range(2 * tq // nb)]
        bands.append(jnp.concatenate(rows, axis=1))
    return jnp.stack(bands)


def _attn_kernel(lam_ref, q_ref, k_ref, vt_ref, band_ref, g_ref, o_ref, *, tq, lambda_init):
    i = pl.program_id(2)
    q = q_ref[0] * jnp.asarray(A_QK_DIM ** -0.5, BF16)
    lane = lax.broadcasted_iota(jnp.int32, q.shape, 1)
    zero = jnp.zeros_like(q)
    qq = jnp.concatenate([jnp.where(lane < A_QK_DIM, q, zero),
                          jnp.where(lane >= A_QK_DIM, q, zero)], axis=0)

    kb0 = pl.multiple_of(jnp.maximum(i - 1, 0) * tq, tq)
    kb = k_ref[0, pl.ds(kb0, 2 * tq), :]
    band = band_ref[0, 0]
    s = lax.dot_general(kb, qq, NT, preferred_element_type=F32) + jnp.concatenate([band, band], axis=1)
    m = jnp.max(s, axis=0, keepdims=True)

    def weighted_values(keys, p):
        vt = jnp.concatenate([vt_ref[0, :, keys], jnp.ones((BF16_SUBLANES, p.shape[0]), BF16)], axis=0)
        return jnp.dot(vt, p, preferred_element_type=F32)

    acc = weighted_values(pl.ds(kb0, 2 * tq), jnp.exp((s - m).astype(BF16)))

    n_far = jnp.maximum(i - 1, 0)

    def logits(j):
        return lax.dot_general(k_ref[0, pl.ds(pl.multiple_of(j * tq, tq), tq), :], qq, NT,
                               preferred_element_type=F32)

    def body(j, carry):
        m, acc = carry
        s = logits(j)
        m_new = jnp.maximum(m, jnp.max(s, axis=0, keepdims=True))
        alpha = jnp.exp(m - m_new)
        p = jnp.exp((s - m_new).astype(BF16))
        acc = alpha * acc + weighted_values(pl.ds(pl.multiple_of(j * tq, tq), tq), p)
        return m_new, acc

    m, acc = lax.fori_loop(0, n_far, body, (m, acc))

    lp = lam_ref[...]
    lam = (jnp.exp(jnp.sum(lp[0:1] * lp[1:2], axis=-1, keepdims=True))
           - jnp.exp(jnp.sum(lp[2:3] * lp[3:4], axis=-1, keepdims=True)) + lambda_init)
    o = acc[:A_HEAD_W] / acc[A_HEAD_W:A_HEAD_W + 1]
    o = o[:, :tq] - lam * o[:, tq:]
    o = o * lax.rsqrt(jnp.mean(o * o, axis=0, keepdims=True) + RMS_EPS) * g_ref[...] * (1.0 - lambda_init)
    o_ref[0] = jnp.transpose(o)


def _diff_attention(pa, vt, band_t, lam_par, subln_g, lambda_init, tq):
    Bn, S, _ = pa.shape
    W = A_HEAD_W
    kern = functools.partial(_attn_kernel, tq=tq, lambda_init=lambda_init)
    return pl.pallas_call(
        kern,
        grid=(Bn, A_HEADS, S // tq),
        in_specs=[
            pl.BlockSpec((4, A_QK_DIM), lambda b, h, i: (0, 0)),
            pl.BlockSpec((1, tq, W), lambda b, h, i: (b, i, h)),
            pl.BlockSpec((1, S, W), lambda b, h, i: (b, 0, A_HEADS + h)),
            pl.BlockSpec((1, W, S), lambda b, h, i: (b, h, 0)),
            pl.BlockSpec((1, 1, 2 * tq, tq), lambda b, h, i: (jnp.minimum(i, 1), h, 0, 0)),
            pl.BlockSpec((W, 1), lambda b, h, i: (0, 0)),
        ],
        out_specs=pl.BlockSpec((1, tq, W), lambda b, h, i: (b, i, h)),
        out_shape=jax.ShapeDtypeStruct((Bn, S, A_WIDTH), F32),
        compiler_params=_cparams("arbitrary", "arbitrary", "arbitrary"),
        name="diff_attn",
    )(lam_par, pa, pa, vt, band_t, subln_g.reshape(W, 1))


def _head_ones(n):
    r = lax.broadcasted_iota(jnp.int32, (n, n), 0) // B_HEAD_DIM
    c = lax.broadcasted_iota(jnp.int32, (n, n), 1) // B_HEAD_DIM
    return (r == c).astype(BF16)


def _rwkv_prep_kernel(pb_ref, prev_ref, mu_ref, w0_ref, w2_ref, a0_ref, a2_ref, g2_ref,
                      kk_ref, ka_ref, rk_ref,
                      rt_ref, at_ref, kt_ref, bt_ref, v_ref, wc_ref, bonus_ref, g_ref, *, tm):
    i = pl.program_id(1)
    C = RWKV_CHUNK
    x = pb_ref[0]
    row = lax.broadcasted_iota(jnp.int32, x.shape, 0)
    last = prev_ref[0, 7:8, :] * (i > 0).astype(F32)
    prev = jnp.where(row == 0, last, pltpu.roll(x, 1, 0))
    p = x + (prev - x) * mu_ref[...]
    o1, o2, o3 = B_WIDTH, 2 * B_WIDTH, 3 * B_WIDTH
    r, k, v = p[:, :o1], p[:, o1:o2], p[:, o2:o3]
    lora = p[:, o3:o3 + B_DECAY_LORA + B_AAA_LORA]
    gd = p[:, o3 + B_DECAY_LORA + B_AAA_LORA:]

    z = -(w0_ref[...] + _mm3(jnp.tanh(lora), w2_ref[...]))
    softplus = jnp.maximum(z, 0.0) + jnp.log(1.0 + jnp.exp(-jnp.abs(z)))
    logw = -jnp.exp(-softplus - 0.5)
    a = _sigmoid(a0_ref[...] + _mm3(lora, a2_ref[...]))
    g_ref[0] = _mm3(_sigmoid(gd), g2_ref[...])

    ones = _head_ones(B_WIDTH)
    kk = k * kk_ref[...]
    kk = kk * lax.rsqrt(jnp.maximum(_mm2(kk * kk, ones), 1e-24))
    k2 = k * (1.0 + (a - 1.0) * ka_ref[...])
    bonus_ref[0] = _mm2(r * k2 * rk_ref[...], ones) * v

    t_in = lax.broadcasted_iota(jnp.int32, (tm, B_WIDTH), 0) % C
    cum = logw
    sh = 1
    while sh < C:
        cum = cum + jnp.where(t_in >= sh, pltpu.roll(cum, sh, 0), 0.0)
        sh *= 2
    n = tm // C
    wc_ref[0] = jnp.exp(jnp.sum(logw.reshape(n, C, B_WIDTH), axis=1))
    e_pos = jnp.exp(cum)
    e_neg = jnp.exp(-cum)
    rt_ref[0] = r * e_pos
    at_ref[0] = -kk * jnp.exp(cum - logw)
    kt_ref[0] = k2 * e_neg
    bt_ref[0] = kk * a * e_neg
    v_ref[0] = v


def _rwkv_prep(pbc, mu, w0, w2p, a0, a2p, g2, k_k, k_a, r_k, tm):
    Bn, S, _ = pbc.shape
    W = B_WIDTH
    nl = B_DECAY_LORA + B_AAA_LORA
    row = lambda a: a.reshape(1, -1)
    full = lambda shp: pl.BlockSpec(shp, lambda b, i: (0,) * len(shp))
    seq = pl.BlockSpec((1, tm, W), lambda b, i: (b, i, 0))
    seq_shape = jax.ShapeDtypeStruct((Bn, S, W), F32)
    n = tm // RWKV_CHUNK
    return pl.pallas_call(
        functools.partial(_rwkv_prep_kernel, tm=tm),
        grid=(Bn, S // tm),
        in_specs=[
            pl.BlockSpec((1, tm, B_COLS), lambda b, i: (b, i, 0)),
            pl.BlockSpec((1, 8, B_COLS), lambda b, i: (b, jnp.maximum(i * (tm // 8) - 1, 0), 0)),
            full((1, B_COLS)), full((1, W)), full((nl, W)), full((1, W)), full((nl, W)),
            full((B_GATE_LORA, W)), full((1, W)), full((1, W)), full((1, W)),
        ],
        out_specs=[seq, seq, seq, seq, seq,
                   pl.BlockSpec((1, n, W), lambda b, i: (b, i, 0)), seq, seq],
        out_shape=[seq_shape] * 5 + [jax.ShapeDtypeStruct((Bn, S // RWKV_CHUNK, W), F32)] + [seq_shape] * 2,
        compiler_params=_cparams("arbitrary", "arbitrary"),
        name="rwkv_prep",
    )(pbc, pbc, row(mu), row(w0), w2p, row(a0), a2p, g2, row(k_k), row(k_a), row(r_k))


def _rwkv_scan_kernel(rt_ref, at_ref, kt_ref, bt_ref, v_ref, wc_ref, bonus_ref, g_ref,
                      lng_ref, lnb_ref, o_ref, state, *, tt):
    C = RWKV_CHUNK
    W = B_WIDTH

    @pl.when(pl.program_id(1) == 0)
    def _():
        state[...] = jnp.zeros_like(state)

    lane_head = lax.broadcasted_iota(jnp.int32, (C, W), 1) // B_HEAD_DIM
    tt_i = lax.broadcasted_iota(jnp.int32, (C, W), 0)
    ss_i = lax.broadcasted_iota(jnp.int32, (C, W), 1) % C
    strict = tt_i > ss_i
    incl = tt_i >= ss_i
    eye = (tt_i == ss_i).astype(F32)
    ones = _head_ones(W)
    bd_mask = ones.astype(F32)

    head_mask = [(lane_head == h).astype(BF16) for h in range(B_HEADS)]

    def bd_split(x):
        xb = x.astype(BF16)
        return jnp.concatenate([xb * mk for mk in head_mask], axis=0)

    def mm_bd(a, b_bd, dims=NN):
        return lax.dot_general(a.astype(BF16), b_bd, dims, preferred_element_type=F32)

    def state_free(gi, nb):
        G = range(RWKV_GROUP)
        sls = [pl.ds(pl.multiple_of((gi * RWKV_GROUP + j) * C, C), C) for j in G]
        rt = [rt_ref[nb, sl, :] for sl in sls]
        at = [at_ref[nb, sl, :] for sl in sls]
        kt = [kt_ref[nb, sl, :] for sl in sls]
        bt = [bt_ref[nb, sl, :] for sl in sls]
        v = [v_ref[nb, sl, :] for sl in sls]
        wc = [wc_ref[nb, pl.ds(gi * RWKV_GROUP + j, 1), :] for j in G]
        ar = [jnp.concatenate([at[j], rt[j]], axis=0) for j in G]
        bdb = [bd_split(bt[j]) for j in G]
        bdk = [bd_split(kt[j]) for j in G]
        a_b = [mm_bd(ar[j], bdb[j], NT) for j in G]
        a_k = [mm_bd(ar[j], bdk[j], NT) for j in G]
        lo = [jnp.where(strict, a_b[j][:C], 0.0) for j in G]
        a_ak = [jnp.where(strict, a_k[j][:C], 0.0) for j in G]
        a_rb = [jnp.where(incl, a_b[j][C:], 0.0) for j in G]
        a_rk = [jnp.where(incl, a_k[j][C:], 0.0) for j in G]
        pw = lo
        tinv = [eye + lo[j] for j in G]
        bdp = [bd_split(pw[j]) for j in G]
        span = 2
        while span < C:
            pw = [mm_bd(pw[j], bdp[j]) for j in G]
            bdp = [bd_split(pw[j]) for j in G]
            tinv = [tinv[j] + mm_bd(tinv[j], bdp[j]) for j in G]
            span *= 2
        bdv = [bd_split(v[j]) for j in G]
        bda = [bd_split(at[j]) for j in G]
        abar = [mm_bd(tinv[j], bda[j]) for j in G]
        akv = [bd_split(mm_bd(a_ak[j], bdv[j])) for j in G]
        u0 = [mm_bd(tinv[j], akv[j]) for j in G]
        y0 = [mm_bd(a_rk[j], bdv[j]) for j in G]
        kv = [_mm(v[j], kt[j] * wc[j], TN) * bd_mask for j in G]
        return [(jnp.concatenate([abar[j], rt[j]], axis=0), u0[j], y0[j], a_rb[j], bt[j] * wc[j], kv[j], wc[j])
                for j in G]

    def group(gi, carry):
        seqs = range(rt_ref.shape[0])
        pre = [state_free(gi, nb) for nb in seqs]
        s = [state[nb] for nb in seqs]
        ys = [[] for _ in seqs]
        for j in range(RWKV_GROUP):
            for nb in seqs:
                abar_rt, u0, y0, a_rb, btw, kv, wc = pre[nb][j]
                a_s = _mm(abar_rt, s[nb], NT)
                u = a_s[:C] + u0
                ys[nb].append(a_s[C:] + y0 + mm_bd(a_rb, bd_split(u)))
                s[nb] = s[nb] * wc + _mm(u, btw, TN) * bd_mask + kv
        sl = pl.ds(pl.multiple_of(gi * (RWKV_GROUP * C), RWKV_GROUP * C), RWKV_GROUP * C)
        for nb in seqs:
            state[nb] = s[nb]
            y = jnp.concatenate(ys[nb], axis=0)
            mean = _mm2(y, ones) * (1.0 / B_HEAD_DIM)
            d = y - mean
            var = _mm2(d * d, ones) * (1.0 / B_HEAD_DIM)
            yn = d * lax.rsqrt(var + B_LNX_EPS) * lng_ref[...] + lnb_ref[...]
            o_ref[nb, sl, :] = (yn + bonus_ref[nb, sl, :]) * g_ref[nb, sl, :]
        return carry

    lax.fori_loop(0, tt // (RWKV_GROUP * C), group, 0)


def _rwkv_scan(rt, at, kt, bt, v, wc, bonus, g, lnx_g, lnx_b, tt):
    Bn, S, W = rt.shape
    n = tt // RWKV_CHUNK
    nseq = 2 if Bn % 2 == 0 else 1
    seq = pl.BlockSpec((nseq, tt, W), lambda b, i: (b, i, 0))
    vec = pl.BlockSpec((1, W), lambda b, i: (0, 0))
    return pl.pallas_call(
        functools.partial(_rwkv_scan_kernel, tt=tt),
        grid=(Bn // nseq, S // tt),
        in_specs=[seq, seq, seq, seq, seq, pl.BlockSpec((nseq, n, W), lambda b, i: (b, i, 0)), seq, seq, vec, vec],
        out_specs=seq,
        out_shape=jax.ShapeDtypeStruct((Bn, S, W), F32),
        scratch_shapes=[pltpu.VMEM((nseq, B_HEADS * B_HEAD_DIM, W), F32)],
        compiler_params=_cparams("arbitrary", "arbitrary"),
        name="rwkv_scan",
    )(rt, at, kt, bt, v, wc, bonus, g, lnx_g.reshape(1, W), lnx_b.reshape(1, W))


def _gmlp_kernel(pc_ref, lng_ref, lnb_ref, ws_ref, bs_ref, o_ref, *, tm):
    x = pc_ref[0]
    z = x * (0.5 * (1.0 + jnp.tanh(math.sqrt(2.0 / math.pi) * (x + 0.044715 * (x * x * x)))))
    u, v = z[:, :C_WIDTH], z[:, C_WIDTH:]
    mu = jnp.mean(v, axis=-1, keepdims=True)
    d = v - mu
    var = jnp.mean(d * d, axis=-1, keepdims=True)
    vn = d * lax.rsqrt(var + LN_EPS) * lng_ref[...] + lnb_ref[...]
    group = lax.broadcasted_iota(jnp.int32, (CHUNK, C_WIDTH), 1) // C_GROUP_DIM
    tril = (lax.broadcasted_iota(jnp.int32, (CHUNK, CHUNK), 0)
            >= lax.broadcasted_iota(jnp.int32, (CHUNK, CHUNK), 1))
    ws = [jnp.where(tril, ws_ref[gi], 0.0).astype(BF16) for gi in range(C_GROUPS)]
    for c in range(tm // CHUNK):
        sl = slice(c * CHUNK, (c + 1) * CHUNK)
        vc = vn[sl].astype(BF16)
        sv = bs_ref[...]
        for gi in range(C_GROUPS):
            t = jnp.dot(ws[gi], vc, preferred_element_type=F32)
            sv = sv + jnp.where(group == gi, t, 0.0)
        o_ref[0, sl, :] = u[sl] * sv


def _gmlp(pbc, ln_g, ln_b, w_s, b_s, tm):
    Bn, S, _ = pbc.shape
    bs_wide = jnp.repeat(jnp.transpose(b_s), C_GROUP_DIM, axis=1)
    return pl.pallas_call(
        functools.partial(_gmlp_kernel, tm=tm),
        grid=(Bn, S // tm),
        in_specs=[
            pl.BlockSpec((1, tm, C_COLS), lambda b, i: (b, i, B_COLS // C_COLS)),
            pl.BlockSpec((1, C_WIDTH), lambda b, i: (0, 0)),
            pl.BlockSpec((1, C_WIDTH), lambda b, i: (0, 0)),
            pl.BlockSpec((C_GROUPS, CHUNK, CHUNK), lambda b, i: (0, 0, 0)),
            pl.BlockSpec((CHUNK, C_WIDTH), lambda b, i: (0, 0)),
        ],
        out_specs=pl.BlockSpec((1, tm, C_WIDTH), lambda b, i: (b, i, 0)),
        out_shape=jax.ShapeDtypeStruct((Bn, S, C_WIDTH), F32),
        compiler_params=_cparams("arbitrary", "arbitrary"),
        name="gmlp",
    )(pbc, ln_g.reshape(1, -1), ln_b.reshape(1, -1), w_s, bs_wide)


def _mid_kernel(ya_ref, yb_ref, yc_ref, x_ref, woa_ref, wob_ref, woc_ref, gpost_ref, g1_ref,
                gpre_ref, sc_ref, sh_ref, wr_ref, xo_ref, h_ref, score_ref):
    y = (_mm(ya_ref[0], woa_ref[...]) + _mm(yb_ref[0], wob_ref[...]) + _mm(yc_ref[0], woc_ref[...]))
    xn = x_ref[0] + g1_ref[0] * (_rms(y) * gpost_ref[...])
    xo_ref[0] = xn
    h = _rms(xn) * gpre_ref[...] * (1.0 + sc_ref[0]) + sh_ref[0]
    h_ref[0] = _pack_bf16_pair(h)
    score_ref[0] = _sigmoid(_mm3(wr_ref[...], h, NT))


def _mid(ya, yb, yc, x, woa, wob, woc, gpost, g1, gpre, sc, sh, wr, tm):
    Bn, S, D = x.shape
    NR = wr.shape[0]
    seq = lambda w: pl.BlockSpec((1, tm, w), lambda b, i: (b, i, 0))
    full = lambda shp: pl.BlockSpec(shp, lambda b, i: (0,) * len(shp))
    per_b = pl.BlockSpec((1, 1, D), lambda b, i: (b, 0, 0))
    return pl.pallas_call(
        _mid_kernel,
        grid=(Bn, S // tm),
        in_specs=[seq(A_WIDTH), seq(B_WIDTH), seq(C_WIDTH), seq(D),
                  full((A_WIDTH, D)), full((B_WIDTH, D)), full((C_WIDTH, D)),
                  full((1, D)), per_b, full((1, D)), per_b, per_b,
                  full((NR, D))],
        out_specs=[seq(D), seq(D // 2), pl.BlockSpec((1, NR, tm), lambda b, i: (b, 0, i))],
        out_shape=[jax.ShapeDtypeStruct((Bn, S, D), F32), jax.ShapeDtypeStruct((Bn, S, D // 2), jnp.int32),
                   jax.ShapeDtypeStruct((Bn, NR, S), F32)],
        compiler_params=_cparams("arbitrary", "arbitrary"),
        name="mid",
    )(ya, yb, yc, x, woa, wob, woc, gpost.reshape(1, D), g1, gpre.reshape(1, D), sc, sh, wr)


def _shared_expert_kernel(h_ref, ws1_ref, ws3_ref, ws2_ref, o_ref):
    lo, hi = _unpack_bf16_pair(h_ref[0])
    hb = jnp.concatenate([lo, hi], axis=1).astype(BF16)
    t = _silu(jnp.dot(hb, ws1_ref[...], preferred_element_type=F32)) * jnp.dot(
        hb, ws3_ref[...], preferred_element_type=F32)
    o_ref[0] = jnp.dot(t.astype(BF16), ws2_ref[...], preferred_element_type=F32)


def _shared_expert(hp, ws1, ws3, ws2, tm):
    Bn, S, DP = hp.shape
    D, F = ws1.shape
    full = lambda shp: pl.BlockSpec(shp, lambda b, i: (0,) * len(shp))
    return pl.pallas_call(
        _shared_expert_kernel,
        grid=(Bn, S // tm),
        in_specs=[pl.BlockSpec((1, tm, DP), lambda b, i: (b, i, 0)), full((D, F)), full((D, F)), full((F, D))],
        out_specs=pl.BlockSpec((1, tm, D), lambda b, i: (b, i, 0)),
        out_shape=jax.ShapeDtypeStruct((Bn, S, D), F32),
        compiler_params=_cparams("arbitrary", "arbitrary"),
        name="shared_expert",
    )(hp, ws1, ws3, ws2)


def _first_argmax(vals, iota, n):
    m = jnp.max(vals, axis=0, keepdims=True)
    idx = jnp.min(jnp.where(vals == m, iota, n), axis=0, keepdims=True)
    return m, idx


def _route_kernel(sc_ref, bias_ref, e_ref, w_ref, r_ref, cnt_ref, carry, *, tm):
    @pl.when((pl.program_id(0) == 0) & (pl.program_id(1) == 0))
    def _():
        carry[...] = jnp.zeros_like(carry)

    G = EXPERTS_PER_GROUP
    s = sc_ref[0]
    biased = s + bias_ref[...]
    neg_inf = jnp.float32(-jnp.inf)
    io8 = lax.broadcasted_iota(jnp.int32, (G, tm), 0)
    gs_rows = []
    for g in range(N_GROUPS):
        blk = biased[g * G:(g + 1) * G]
        m1, i1 = _first_argmax(blk, io8, G)
        m2 = jnp.max(jnp.where(io8 == i1, neg_inf, blk), axis=0, keepdims=True)
        gs_rows.append(m1 + m2)
    gs = jnp.concatenate(gs_rows, axis=0)
    gio = lax.broadcasted_iota(jnp.int32, (N_GROUPS, tm), 0)
    gsel = jnp.zeros((N_GROUPS, tm), jnp.bool_)
    for _ in range(TOPK_GROUPS):
        _, gi = _first_argmax(gs, gio, N_GROUPS)
        pick = gio == gi
        gsel = gsel | pick
        gs = jnp.where(pick, neg_inf, gs)
    masked = jnp.concatenate(
        [jnp.where(gsel[g:g + 1], biased[g * G:(g + 1) * G], neg_inf) for g in range(N_GROUPS)], axis=0)

    eio = lax.broadcasted_iota(jnp.int32, (N_EXPERTS, tm), 0)
    picks, e_rows, s_rows = [], [], []
    for _ in range(TOP_K):
        _, ei = _first_argmax(masked, eio, N_EXPERTS)
        pick = eio == ei
        picks.append(pick)
        e_rows.append(ei)
        s_rows.append(jnp.sum(jnp.where(pick, s, 0.0), axis=0, keepdims=True))
        masked = jnp.where(pick, neg_inf, masked)
    top_s = jnp.concatenate(s_rows, axis=0)
    w_ref[...] = top_s / (jnp.sum(top_s, axis=0, keepdims=True) + 1e-20) * ROUTED_SCALE
    e_ref[...] = jnp.concatenate(e_rows, axis=0)

    sel = jnp.zeros((N_EXPERTS, tm), F32)
    for pick in picks:
        sel = sel + pick.astype(F32)
    before = (lax.broadcasted_iota(jnp.int32, (tm, tm), 0) < lax.broadcasted_iota(jnp.int32, (tm, tm), 1))
    pos = carry[...] + jnp.dot(sel.astype(BF16), before.astype(BF16), preferred_element_type=F32)
    r_ref[...] = jnp.concatenate(
        [jnp.sum(jnp.where(pick, pos, 0.0), axis=0, keepdims=True) for pick in picks], axis=0).astype(jnp.int32)
    total = carry[...] + jnp.sum(sel, axis=1, keepdims=True)
    carry[...] = total
    cnt_ref[...] = jnp.broadcast_to(total, cnt_ref.shape).astype(jnp.int32)


def _route(scores_t, e_bias, tm):
    Bn, _, S = scores_t.shape
    T = Bn * S
    nt = S // tm
    tok = pl.BlockSpec((TOP_K, tm), lambda b, i: (0, b * nt + i))
    return pl.pallas_call(
        functools.partial(_route_kernel, tm=tm),
        grid=(Bn, nt),
        in_specs=[pl.BlockSpec((1, N_EXPERTS, tm), lambda b, i: (b, 0, i)),
                  pl.BlockSpec((N_EXPERTS, 1), lambda b, i: (0, 0))],
        out_specs=[tok, tok, tok, pl.BlockSpec((N_EXPERTS, V7X_LANES), lambda b, i: (0, 0))],
        out_shape=[jax.ShapeDtypeStruct((TOP_K, T), jnp.int32), jax.ShapeDtypeStruct((TOP_K, T), F32),
                   jax.ShapeDtypeStruct((TOP_K, T), jnp.int32),
                   jax.ShapeDtypeStruct((N_EXPERTS, V7X_LANES), jnp.int32)],
        scratch_shapes=[pltpu.VMEM((N_EXPERTS, 1), F32)],
        compiler_params=_cparams("arbitrary", "arbitrary"),
        name="route",
    )(scores_t, e_bias.reshape(N_EXPERTS, 1))


def _dest_kernel(start_ref, e_ref, r_ref, o_ref):
    e = e_ref[...]
    acc = r_ref[...]
    for ex in range(N_EXPERTS):
        acc = acc + jnp.where(e == ex, start_ref[ex], 0)
    o_ref[0] = acc


def _dest_rows(pad_start, eidx, rank, tt):
    K_, T = eidx.shape
    grid_spec = pltpu.PrefetchScalarGridSpec(
        num_scalar_prefetch=1,
        grid=(T // tt,),
        in_specs=[pl.BlockSpec((K_, tt), lambda i, st: (0, i)), pl.BlockSpec((K_, tt), lambda i, st: (0, i))],
        out_specs=pl.BlockSpec((1, K_, tt), lambda i, st: (i, 0, 0)),
    )
    return pl.pallas_call(
        _dest_kernel,
        grid_spec=grid_spec,
        out_shape=jax.ShapeDtypeStruct((T // tt, K_, tt), jnp.int32),
        compiler_params=_cparams("arbitrary"),
        name="dest_rows",
    )(pad_start, eidx, rank)


def _expert_kernel(blk_e_ref, n_used_ref, n_valid_ref, x_ref, w1_ref, w3_ref, w2_ref, o_ref, w1b, w3b, w2b):
    i = pl.program_id(0)

    @pl.when((i == 0) | (blk_e_ref[i] != blk_e_ref[jnp.maximum(i - 1, 0)]))
    def _():
        w1b[...] = w1_ref[0].astype(BF16)
        w3b[...] = w3_ref[0].astype(BF16)
        w2b[...] = w2_ref[0].astype(BF16)

    @pl.when(i < n_used_ref[0])
    def _():
        row = lax.broadcasted_iota(jnp.int32, x_ref.shape, 0)
        x_lo, x_hi = _unpack_bf16_pair(jnp.where(row < n_valid_ref[i], x_ref[...], 0))
        x_lo, x_hi = x_lo.astype(BF16), x_hi.astype(BF16)
        half = x_lo.shape[1]

        def up(wb):
            return (jnp.dot(x_lo, wb[:half, :], preferred_element_type=F32)
                    + jnp.dot(x_hi, wb[half:, :], preferred_element_type=F32))

        t = _silu(up(w1b)) * up(w3b)
        o_ref[...] = _pack_bf16_pair(jnp.dot(t.astype(BF16), w2b[...], preferred_element_type=F32))


def _experts(blk_e, n_used, n_valid, xs, w1, w3, w2, layer):
    P, DP = xs.shape
    EB = EXPERT_BLOCK
    n_blocks = blk_e.shape[0]
    D, F = w1.shape[2], w1.shape[3]
    rows = pl.BlockSpec((EB, DP), lambda i, be, nu, nv: (jnp.minimum(i, nu[0] - 1), 0))
    grid_spec = pltpu.PrefetchScalarGridSpec(
        num_scalar_prefetch=3,
        grid=(n_blocks,),
        in_specs=[
            rows,
            pl.BlockSpec((None, 1, D, F), lambda i, be, nu, nv: (layer, be[i], 0, 0)),
            pl.BlockSpec((None, 1, D, F), lambda i, be, nu, nv: (layer, be[i], 0, 0)),
            pl.BlockSpec((None, 1, F, D), lambda i, be, nu, nv: (layer, be[i], 0, 0)),
        ],
        out_specs=rows,
        scratch_shapes=[pltpu.VMEM((D, F), BF16), pltpu.VMEM((D, F), BF16), pltpu.VMEM((F, D), BF16)],
    )
    return pl.pallas_call(
        _expert_kernel,
        grid_spec=grid_spec,
        out_shape=jax.ShapeDtypeStruct((P, DP), jnp.int32),
        compiler_params=_cparams("arbitrary"),
        name="experts",
    )(blk_e, n_used, n_valid, xs, w1, w3, w2)


def _block_layout(counts, n_blocks):
    EB = EXPERT_BLOCK
    padded = (counts + EB - 1) // EB * EB
    ex = jnp.arange(N_EXPERTS)
    pad_end = jnp.sum(jnp.where(ex[:, None] <= ex[None, :], padded[:, None], 0), axis=0)
    pad_start = pad_end - padded
    blk_row = (jnp.arange(n_blocks) * EB)[:, None]
    blk_e = jnp.minimum(jnp.sum((pad_end[None, :] <= blk_row).astype(jnp.int32), axis=1), N_EXPERTS - 1)
    n_used = (jnp.sum(padded) // EB).astype(jnp.int32).reshape(1)
    mine = (pad_start[None, :] <= blk_row) & (blk_row < pad_end[None, :])
    n_valid = jnp.sum(jnp.where(mine, jnp.clip(counts[None, :] - (blk_row - pad_start[None, :]), 0, EB), 0), axis=1)
    return pad_start.astype(jnp.int32), blk_e.astype(jnp.int32), n_used, n_valid.astype(jnp.int32)


SC_GATHER_ROWS = 64


def _sc_gather_rows(table, idx):
    info = plsc.get_sparse_core_info()
    nc, ns = info.num_cores, info.num_subcores
    M = idx.shape[0]
    W = table.shape[1]
    b = SC_GATHER_ROWS
    per_worker = M // (nc * ns)
    steps = per_worker // b
    assert per_worker * nc * ns == M and steps * b == per_worker and steps % 2 == 0
    mesh = plsc.VectorSubcoreMesh(core_axis_name="c", subcore_axis_name="s")

    @functools.partial(
        pl.kernel, mesh=mesh,
        out_type=jax.ShapeDtypeStruct((M, W), table.dtype),
        scratch_types=[pltpu.VMEM((2, b), jnp.int32), pltpu.VMEM((2, b, W), table.dtype),
                       pltpu.SemaphoreType.DMA, pltpu.SemaphoreType.DMA],
        name="sc_gather_rows",
    )
    def gather(table_hbm, idx_hbm, out_hbm, idx_v, rows_v, sem0, sem1):
        wid = lax.axis_index("s") * nc + lax.axis_index("c")
        sems = (sem0, sem1)

        def base(s):
            return pl.multiple_of(wid * per_worker + s * b, b)

        def gather_copy(slot):
            return pltpu.make_async_copy(table_hbm.at[idx_v.at[slot]], rows_v.at[slot], sems[slot])

        def start(s, slot):
            pltpu.sync_copy(idx_hbm.at[pl.ds(base(s), b)], idx_v.at[slot])
            gather_copy(slot).start()

        def finish(s, slot):
            gather_copy(slot).wait()
            pltpu.sync_copy(rows_v.at[slot], out_hbm.at[pl.ds(base(s), b)])

        start(0, 0)

        @pl.loop(0, steps, step=2)
        def _(s):
            start(s + 1, 1)
            finish(s, 0)

            @pl.when(s + 2 < steps)
            def _():
                start(s + 2, 0)

            finish(s + 1, 1)

    return gather(table, idx)


def _sc_scatter_rows(rows, idx, n_out):
    info = plsc.get_sparse_core_info()
    nc, ns = info.num_cores, info.num_subcores
    T, W = rows.shape
    G, K_, b = idx.shape
    steps = G // (nc * ns)
    assert steps * nc * ns == G and G * b == T
    mesh = plsc.VectorSubcoreMesh(core_axis_name="c", subcore_axis_name="s")

    @functools.partial(
        pl.kernel, mesh=mesh,
        out_type=jax.ShapeDtypeStruct((n_out, W), rows.dtype),
        scratch_types=[pltpu.VMEM((K_, b), jnp.int32), pltpu.VMEM((b, W), rows.dtype), pltpu.SemaphoreType.DMA],
        name="sc_scatter_rows",
    )
    def scatter(rows_hbm, idx_hbm, out_hbm, idx_v, rows_v, sem):
        wid = lax.axis_index("s") * nc + lax.axis_index("c")

        @pl.loop(0, steps)
        def _(s):
            g = wid * steps + s
            pltpu.sync_copy(idx_hbm.at[g], idx_v)
            pltpu.sync_copy(rows_hbm.at[pl.ds(pl.multiple_of(g * b, b), b)], rows_v)
            copies = [pltpu.async_copy(rows_v, out_hbm.at[idx_v.at[k]], sem) for k in range(K_)]
            for cp in copies:
                cp.wait()

    return scatter(rows, idx)


def _combine_dense_kernel(rows_ref, w_ref, x_ref, shared_ref, gpost_ref, g2_ref, o_ref):
    w = w_ref[...]
    tt, half = rows_ref.shape[1], rows_ref.shape[2]
    y_lo = jnp.zeros((tt, half), F32)
    y_hi = jnp.zeros((tt, half), F32)
    for k in range(TOP_K):
        lo, hi = _unpack_bf16_pair(rows_ref[k])
        y_lo = y_lo + w[:, k:k + 1] * lo
        y_hi = y_hi + w[:, k:k + 1] * hi
    y = shared_ref[0] + jnp.concatenate([y_lo, y_hi], axis=1)
    o_ref[0] = x_ref[0] + g2_ref[0] * (_rms(y) * gpost_ref[...])


def _combine_dense(rows, w_tok, x, shared, gpost, g2, tt):
    Bn, S, D = x.shape
    K_, T, DP = rows.shape
    nt = S // tt
    seq = pl.BlockSpec((1, tt, D), lambda b, i: (b, i, 0))
    return pl.pallas_call(
        _combine_dense_kernel,
        grid=(Bn, nt),
        in_specs=[pl.BlockSpec((K_, tt, DP), lambda b, i: (0, b * nt + i, 0)),
                  pl.BlockSpec((tt, K_), lambda b, i: (b * nt + i, 0)), seq, seq,
                  pl.BlockSpec((1, D), lambda b, i: (0, 0)), pl.BlockSpec((1, 1, D), lambda b, i: (b, 0, 0))],
        out_specs=seq,
        out_shape=jax.ShapeDtypeStruct((Bn, S, D), F32),
        compiler_params=_cparams("arbitrary", "arbitrary"),
        name="combine_dense",
    )(rows, w_tok, x, shared, gpost.reshape(1, D), g2)


def kernel(x, c, w_ada, b_ada, norm_pre_mix, norm_post_mix, norm_pre_ffn, norm_post_ffn, w_in, w_out, rel_bias_table, diff_lambda, diff_subln, rwkv_mu, rwkv_w0, rwkv_w2, rwkv_a0, rwkv_a2, rwkv_g2, rwkv_k_k, rwkv_k_a, rwkv_r_k, rwkv_lnx_g, rwkv_lnx_b, gmlp_ln_g, gmlp_ln_b, gmlp_w_s, gmlp_b_s, router_w, router_bias, exp_w1, exp_w3, exp_w2, shared_w1, shared_w3, shared_w2):
    Bn, S, D = x.shape
    depth = w_ada.shape[0]
    tm = min(256, S)
    tm_wide = min(512, S)
    tq = min(512, S // 2)
    t_rwkv = min(512, S)

    mod = _adaln(c, w_ada, b_ada)
    band_t = _attn_band(rel_bias_table, tq)
    zpad = jnp.zeros((B_DECAY_LORA, B_WIDTH), F32)
    for l in range(depth):
        sh1, sc1, g1, sh2, sc2, g2 = [m.reshape(Bn, 1, D) for m in jnp.split(mod[l], 6, axis=-1)]
        w_in_b = w_in[l].astype(BF16)
        pa, vt, pbc = _inproj(x, norm_pre_mix[l], sc1, sh1, w_in_b[:, :2 * A_WIDTH],
                              jnp.transpose(w_in_b[:, 2 * A_WIDTH:A_COLS]), w_in_b[:, A_COLS:], tm_wide)
        lambda_init = 0.8 - 0.6 * math.exp(-0.3 * l)
        ya = _diff_attention(pa, vt, band_t, diff_lambda[l], diff_subln[l], lambda_init, tq)
        prep = _rwkv_prep(pbc, rwkv_mu[l], rwkv_w0[l], jnp.concatenate([rwkv_w2[l], zpad], axis=0),
                          rwkv_a0[l], jnp.concatenate([zpad, rwkv_a2[l]], axis=0), rwkv_g2[l],
                          rwkv_k_k[l], rwkv_k_a[l], rwkv_r_k[l].reshape(-1), min(2 * t_rwkv, S))
        yb = _rwkv_scan(*prep, rwkv_lnx_g[l], rwkv_lnx_b[l], t_rwkv)
        yc = _gmlp(pbc, gmlp_ln_g[l], gmlp_ln_b[l], gmlp_w_s[l], gmlp_b_s[l], tm_wide)

        w_out_b = w_out[l].astype(BF16)
        wr_t = jnp.pad(jnp.transpose(router_w[l]), ((0, V7X_LANES - N_EXPERTS), (0, 0)))
        x, h, scores_t = _mid(
            ya, yb, yc, x, w_out_b[:A_WIDTH], w_out_b[A_WIDTH:A_WIDTH + B_WIDTH], w_out_b[A_WIDTH + B_WIDTH:],
            norm_post_mix[l], g1, norm_pre_ffn[l], sc2, sh2, wr_t, tm_wide)

        T = Bn * S
        n_blocks = -(-T * TOP_K // EXPERT_BLOCK) + N_EXPERTS
        eidx, wgt, rank, cnt = _route(scores_t, router_bias[l], tm_wide)
        pad_start, blk_e, n_used, n_valid = _block_layout(cnt[:, 0], n_blocks)
        dest = _dest_rows(pad_start, eidx, rank, tm)
        b = SC_GATHER_ROWS
        dest_sc = jnp.transpose(dest.reshape(T // tm, TOP_K, tm // b, b), (0, 2, 1, 3)).reshape(T // b, TOP_K, b)
        xs = _sc_scatter_rows(h.reshape(T, D // 2), dest_sc, n_blocks * EXPERT_BLOCK)
        shared = _shared_expert(h, shared_w1[l].astype(BF16), shared_w3[l].astype(BF16),
                                shared_w2[l].astype(BF16), tm_wide)
        ys = _experts(blk_e, n_used, n_valid, xs, exp_w1, exp_w3, exp_w2, l)
        dest_kt = jnp.transpose(dest, (1, 0, 2)).reshape(TOP_K * T)
        rows = _sc_gather_rows(ys, dest_kt).reshape(TOP_K, T, D // 2)
        x = _combine_dense(rows, jnp.transpose(wgt), x, shared, norm_post_ffn[l], g2, tm_wide)
    return x
```

```python
import functools
import math

import jax
import jax.numpy as jnp
from jax import lax
from jax.experimental import pallas as pl
from jax.experimental.pallas import tpu as pltpu
from jax.experimental.pallas import tpu_sc as plsc

F32 = jnp.float32
BF16 = jnp.bfloat16

A_HEADS = 4
A_QK_DIM = 64
A_HEAD_W = 2 * A_QK_DIM
A_WIDTH = A_HEADS * A_HEAD_W
N_BUCKETS = 32
MAX_DISTANCE = 128
B_HEADS = 4
B_HEAD_DIM = 64
B_WIDTH = B_HEADS * B_HEAD_DIM
B_DECAY_LORA = 64
B_AAA_LORA = 64
B_GATE_LORA = 128
B_LNX_EPS = 64e-5
C_GROUPS = 4
C_GROUP_DIM = 64
C_WIDTH = C_GROUPS * C_GROUP_DIM
CHUNK = 128
A_COLS = 3 * A_WIDTH
B_COLS = 3 * B_WIDTH + B_DECAY_LORA + B_AAA_LORA + B_GATE_LORA
C_COLS = 2 * C_WIDTH
N_EXPERTS = 64
TOP_K = 8
N_GROUPS = 8
TOPK_GROUPS = 4
EXPERTS_PER_GROUP = N_EXPERTS // N_GROUPS
ROUTED_SCALE = 2.5
EXPERT_BLOCK = 512
RMS_EPS = 1e-6
LN_EPS = 1e-5
NEG_BIG = -1e30

V7X_LANES = 128
BF16_SUBLANES = 16
V7X_VMEM_LIMIT_BYTES = 56 * 1024 * 1024
RWKV_CHUNK = 64
RWKV_GROUP = 8

NN = (((1,), (0,)), ((), ()))
NT = (((1,), (1,)), ((), ()))
TN = (((0,), (0,)), ((), ()))


def _cparams(*sem):
    return pltpu.CompilerParams(dimension_semantics=sem, vmem_limit_bytes=V7X_VMEM_LIMIT_BYTES)


def _mm(a, b, dims=NN):
    return lax.dot_general(a.astype(BF16), b.astype(BF16), dims, preferred_element_type=F32)


def _split(a):
    hi = a.astype(BF16)
    lo = (a - hi.astype(F32)).astype(BF16)
    return hi, lo


def _mm3(a, b, dims=NN):
    ah, al = _split(a)
    bh, bl = _split(b)
    d = lambda x, y: lax.dot_general(x, y, dims, preferred_element_type=F32)
    return d(ah, bh) + d(ah, bl) + d(al, bh)


def _mm2(a, b_exact, dims=NN):
    ah, al = _split(a)
    d = lambda x: lax.dot_general(x, b_exact, dims, preferred_element_type=F32)
    return d(ah) + d(al)


def _pack_bf16_pair(x):
    n = x.shape[1] // 2
    bits = lax.bitcast_convert_type(x.astype(BF16).astype(F32), jnp.int32)
    return ((bits[:, :n] >> 16) & 0xFFFF) | bits[:, n:]


def _unpack_bf16_pair(u):
    lo = lax.bitcast_convert_type(u << 16, F32)
    hi = lax.bitcast_convert_type(u & jnp.int32(-65536), F32)
    return lo, hi


def _rms(x, eps=RMS_EPS):
    return x * lax.rsqrt(jnp.mean(x * x, axis=-1, keepdims=True) + eps)


def _sigmoid(x):
    return 1.0 / (1.0 + jnp.exp(-x))


def _silu(x):
    return x * _sigmoid(x)


def _adaln_kernel(c_ref, w_ref, b_ref, o_ref):
    c = c_ref[...]
    o_ref[0] = _mm3(_silu(c), w_ref[0]) + b_ref[0]


def _adaln(c, w_ada, b_ada):
    L, D, N = w_ada.shape
    Bn = c.shape[0]
    tn = min(N, 1536)
    return pl.pallas_call(
        _adaln_kernel,
        grid=(L, N // tn),
        in_specs=[
            pl.BlockSpec((Bn, D), lambda l, j: (0, 0)),
            pl.BlockSpec((1, D, tn), lambda l, j: (l, 0, j)),
            pl.BlockSpec((1, 1, tn), lambda l, j: (l, 0, j)),
        ],
        out_specs=pl.BlockSpec((1, Bn, tn), lambda l, j: (l, 0, j)),
        out_shape=jax.ShapeDtypeStruct((L, Bn, N), F32),
        compiler_params=_cparams("arbitrary", "arbitrary"),
        name="adaln",
    )(c, w_ada, b_ada.reshape(L, 1, N))


def _inproj_kernel(x_ref, g_ref, sc_ref, sh_ref, wa_ref, wvt_ref, wbc_ref, oa_ref, ovt_ref, obc_ref):
    x = x_ref[0]
    h = _rms(x) * g_ref[...] * (1.0 + sc_ref[0]) + sh_ref[0]
    hb = h.astype(BF16)
    oa_ref[0] = jnp.dot(hb, wa_ref[...], preferred_element_type=F32).astype(BF16)
    ovt_ref[0] = lax.dot_general(wvt_ref[...], hb, NT, preferred_element_type=F32).astype(BF16)
    obc_ref[0] = jnp.dot(hb, wbc_ref[...], preferred_element_type=F32)


def _inproj(x, g, sc, sh, wa, wvt, wbc, tm):
    Bn, S, D = x.shape
    na, nv, nbc = wa.shape[1], wvt.shape[0], wbc.shape[1]
    return pl.pallas_call(
        _inproj_kernel,
        grid=(Bn, S // tm),
        in_specs=[
            pl.BlockSpec((1, tm, D), lambda b, i: (b, i, 0)),
            pl.BlockSpec((1, D), lambda b, i: (0, 0)),
            pl.BlockSpec((1, 1, D), lambda b, i: (b, 0, 0)),
            pl.BlockSpec((1, 1, D), lambda b, i: (b, 0, 0)),
            pl.BlockSpec((D, na), lambda b, i: (0, 0)),
            pl.BlockSpec((nv, D), lambda b, i: (0, 0)),
            pl.BlockSpec((D, nbc), lambda b, i: (0, 0)),
        ],
        out_specs=[
            pl.BlockSpec((1, tm, na), lambda b, i: (b, i, 0)),
            pl.BlockSpec((1, nv, tm), lambda b, i: (b, 0, i)),
            pl.BlockSpec((1, tm, nbc), lambda b, i: (b, i, 0)),
        ],
        out_shape=[
            jax.ShapeDtypeStruct((Bn, S, na), BF16),
            jax.ShapeDtypeStruct((Bn, nv, S), BF16),
            jax.ShapeDtypeStruct((Bn, S, nbc), F32),
        ],
        compiler_params=_cparams("arbitrary", "arbitrary"),
        name="inproj",
    )(x, g.reshape(1, D), sc, sh, wa, wvt, wbc)


def _t5_bucket(dist):
    n = jnp.maximum(dist, 0)
    max_exact = N_BUCKETS // 2
    nf = jnp.maximum(n, 1).astype(F32)
    large = max_exact + (jnp.log(nf / max_exact) / math.log(MAX_DISTANCE / max_exact)
                         * (N_BUCKETS - max_exact)).astype(jnp.int32)
    large = jnp.minimum(large, N_BUCKETS - 1)
    return jnp.where(n < max_exact, n, large)


def _attn_band(table, tq):
    far = table[N_BUCKETS - 1].astype(F32)
    H = table.shape[1]
    nb = V7X_LANES
    L = 3 * nb
    m = jnp.arange(L)
    m = jnp.where(m < nb, m, m - L)
    cache = {}

    def block(c):
        if c not in cache:
            if c - (nb - 1) >= MAX_DISTANCE:
                cache[c] = jnp.zeros((H, nb, nb), F32)
            elif c + (nb - 1) < 0:
                cache[c] = jnp.full((H, nb, nb), NEG_BIG, F32)
            else:
                dist = m + c
                vals = jnp.where(dist[None] >= 0,
                                 jnp.transpose(table[_t5_bucket(dist)].astype(F32)) - far[:, None], NEG_BIG)
                cache[c] = jnp.tile(vals, (1, nb))[:, :nb * (L - 1)].reshape(H, nb, L - 1)[:, :, :nb]
        return cache[c]

    bands = []
    for off in (0, tq):
        rows = [jnp.concatenate([block(nb * (a - b) + off) for a in range(tq // nb)], axis=2)
                for b in range(2 * tq // nb)]
        bands.append(jnp.concatenate(rows, axis=1))
    return jnp.stack(bands)


def _attn_kernel(lam_ref, q_ref, k_ref, vt_ref, band_ref, g_ref, o_ref, *, tq, lambda_init):
    i = pl.program_id(2)
    q = q_ref[0] * jnp.asarray(A_QK_DIM ** -0.5, BF16)
    lane = lax.broadcasted_iota(jnp.int32, q.shape, 1)
    zero = jnp.zeros_like(q)
    qq = jnp.concatenate([jnp.where(lane < A_QK_DIM, q, zero),
                          jnp.where(lane >= A_QK_DIM, q, zero)], axis=0)

    kb0 = pl.multiple_of(jnp.maximum(i - 1, 0) * tq, tq)
    kb = k_ref[0, pl.ds(kb0, 2 * tq), :]
    band = band_ref[0, 0]
    s = lax.dot_general(kb, qq, NT, preferred_element_type=F32) + jnp.concatenate([band, band], axis=1)
    m = jnp.max(s, axis=0, keepdims=True)

    def weighted_values(keys, p):
        vt = jnp.concatenate([vt_ref[0, :, keys], jnp.ones((BF16_SUBLANES, p.shape[0]), BF16)], axis=0)
        return jnp.dot(vt, p, preferred_element_type=F32)

    acc = weighted_values(pl.ds(kb0, 2 * tq), jnp.exp((s - m).astype(BF16)))

    n_far = jnp.maximum(i - 1, 0)

    def logits(j):
        return lax.dot_general(k_ref[0, pl.ds(pl.multiple_of(j * tq, tq), tq), :], qq, NT,
                               preferred_element_type=F32)

    def body(j, carry):
        m, acc = carry
        s = logits(j)
        m_new = jnp.maximum(m, jnp.max(s, axis=0, keepdims=True))
        alpha = jnp.exp(m - m_new)
        p = jnp.exp((s - m_new).astype(BF16))
        acc = alpha * acc + weighted_values(pl.ds(pl.multiple_of(j * tq, tq), tq), p)
        return m_new, acc

    m, acc = lax.fori_loop(0, n_far, body, (m, acc))

    lp = lam_ref[...]
    lam = (jnp.exp(jnp.sum(lp[0:1] * lp[1:2], axis=-1, keepdims=True))
           - jnp.exp(jnp.sum(lp[2:3] * lp[3:4], axis=-1, keepdims=True)) + lambda_init)
    o = acc[:A_HEAD_W] / acc[A_HEAD_W:A_HEAD_W + 1]
    o = o[:, :tq] - lam * o[:, tq:]
    o = o * lax.rsqrt(jnp.mean(o * o, axis=0, keepdims=True) + RMS_EPS) * g_ref[...] * (1.0 - lambda_init)
    o_ref[0] = jnp.transpose(o)


def _diff_attention(pa, vt, band_t, lam_par, subln_g, lambda_init, tq):
    Bn, S, _ = pa.shape
    W = A_HEAD_W
    kern = functools.partial(_attn_kernel, tq=tq, lambda_init=lambda_init)
    return pl.pallas_call(
        kern,
        grid=(Bn, A_HEADS, S // tq),
        in_specs=[
            pl.BlockSpec((4, A_QK_DIM), lambda b, h, i: (0, 0)),
            pl.BlockSpec((1, tq, W), lambda b, h, i: (b, i, h)),
            pl.BlockSpec((1, S, W), lambda b, h, i: (b, 0, A_HEADS + h)),
            pl.BlockSpec((1, W, S), lambda b, h, i: (b, h, 0)),
            pl.BlockSpec((1, 1, 2 * tq, tq), lambda b, h, i: (jnp.minimum(i, 1), h, 0, 0)),
            pl.BlockSpec((W, 1), lambda b, h, i: (0, 0)),
        ],
        out_specs=pl.BlockSpec((1, tq, W), lambda b, h, i: (b, i, h)),
        out_shape=jax.ShapeDtypeStruct((Bn, S, A_WIDTH), F32),
        compiler_params=_cparams("arbitrary", "arbitrary", "arbitrary"),
        name="diff_attn",
    )(lam_par, pa, pa, vt, band_t, subln_g.reshape(W, 1))


def _head_ones(n):
    r = lax.broadcasted_iota(jnp.int32, (n, n), 0) // B_HEAD_DIM
    c = lax.broadcasted_iota(jnp.int32, (n, n), 1) // B_HEAD_DIM
    return (r == c).astype(BF16)


def _rwkv_prep_kernel(pb_ref, prev_ref, mu_ref, w0_ref, w2_ref, a0_ref, a2_ref, g2_ref,
                      kk_ref, ka_ref, rk_ref,
                      rt_ref, at_ref, kt_ref, bt_ref, v_ref, wc_ref, bonus_ref, g_ref, *, tm):
    i = pl.program_id(1)
    C = RWKV_CHUNK
    x = pb_ref[0]
    row = lax.broadcasted_iota(jnp.int32, x.shape, 0)
    last = prev_ref[0, 7:8, :] * (i > 0).astype(F32)
    prev = jnp.where(row == 0, last, pltpu.roll(x, 1, 0))
    p = x + (prev - x) * mu_ref[...]
    o1, o2, o3 = B_WIDTH, 2 * B_WIDTH, 3 * B_WIDTH
    r, k, v = p[:, :o1], p[:, o1:o2], p[:, o2:o3]
    lora = p[:, o3:o3 + B_DECAY_LORA + B_AAA_LORA]
    gd = p[:, o3 + B_DECAY_LORA + B_AAA_LORA:]

    z = -(w0_ref[...] + _mm3(jnp.tanh(lora), w2_ref[...]))
    softplus = jnp.maximum(z, 0.0) + jnp.log(1.0 + jnp.exp(-jnp.abs(z)))
    logw = -jnp.exp(-softplus - 0.5)
    a = _sigmoid(a0_ref[...] + _mm3(lora, a2_ref[...]))
    g_ref[0] = _mm3(_sigmoid(gd), g2_ref[...])

    ones = _head_ones(B_WIDTH)
    kk = k * kk_ref[...]
    kk = kk * lax.rsqrt(jnp.maximum(_mm2(kk * kk, ones), 1e-24))
    k2 = k * (1.0 + (a - 1.0) * ka_ref[...])
    bonus_ref[0] = _mm2(r * k2 * rk_ref[...], ones) * v

    t_in = lax.broadcasted_iota(jnp.int32, (tm, B_WIDTH), 0) % C
    cum = logw
    sh = 1
    while sh < C:
        cum = cum + jnp.where(t_in >= sh, pltpu.roll(cum, sh, 0), 0.0)
        sh *= 2
    n = tm // C
    wc_ref[0] = jnp.exp(jnp.sum(logw.reshape(n, C, B_WIDTH), axis=1))
    e_pos = jnp.exp(cum)
    e_neg = jnp.exp(-cum)
    rt_ref[0] = r * e_pos
    at_ref[0] = -kk * jnp.exp(cum - logw)
    kt_ref[0] = k2 * e_neg
    bt_ref[0] = kk * a * e_neg
    v_ref[0] = v


def _rwkv_prep(pbc, mu, w0, w2p, a0, a2p, g2, k_k, k_a, r_k, tm):
    Bn, S, _ = pbc.shape
    W = B_WIDTH
    nl = B_DECAY_LORA + B_AAA_LORA
    row = lambda a: a.reshape(1, -1)
    full = lambda shp: pl.BlockSpec(shp, lambda b, i: (0,) * len(shp))
    seq = pl.BlockSpec((1, tm, W), lambda b, i: (b, i, 0))
    seq_shape = jax.ShapeDtypeStruct((Bn, S, W), F32)
    n = tm // RWKV_CHUNK
    return pl.pallas_call(
        functools.partial(_rwkv_prep_kernel, tm=tm),
        grid=(Bn, S // tm),
        in_specs=[
            pl.BlockSpec((1, tm, B_COLS), lambda b, i: (b, i, 0)),
            pl.BlockSpec((1, 8, B_COLS), lambda b, i: (b, jnp.maximum(i * (tm // 8) - 1, 0), 0)),
            full((1, B_COLS)), full((1, W)), full((nl, W)), full((1, W)), full((nl, W)),
            full((B_GATE_LORA, W)), full((1, W)), full((1, W)), full((1, W)),
        ],
        out_specs=[seq, seq, seq, seq, seq,
                   pl.BlockSpec((1, n, W), lambda b, i: (b, i, 0)), seq, seq],
        out_shape=[seq_shape] * 5 + [jax.ShapeDtypeStruct((Bn, S // RWKV_CHUNK, W), F32)] + [seq_shape] * 2,
        compiler_params=_cparams("arbitrary", "arbitrary"),
        name="rwkv_prep",
    )(pbc, pbc, row(mu), row(w0), w2p, row(a0), a2p, g2, row(k_k), row(k_a), row(r_k))


def _rwkv_scan_kernel(rt_ref, at_ref, kt_ref, bt_ref, v_ref, wc_ref, bonus_ref, g_ref,
                      lng_ref, lnb_ref, o_ref, state, *, tt):
    C = RWKV_CHUNK
    W = B_WIDTH

    @pl.when(pl.program_id(1) == 0)
    def _():
        state[...] = jnp.zeros_like(state)

    lane_head = lax.broadcasted_iota(jnp.int32, (C, W), 1) // B_HEAD_DIM
    tt_i = lax.broadcasted_iota(jnp.int32, (C, W), 0)
    ss_i = lax.broadcasted_iota(jnp.int32, (C, W), 1) % C
    strict = tt_i > ss_i
    incl = tt_i >= ss_i
    eye = (tt_i == ss_i).astype(F32)
    ones = _head_ones(W)
    bd_mask = ones.astype(F32)

    head_mask = [(lane_head == h).astype(BF16) for h in range(B_HEADS)]

    def bd_split(x):
        xb = x.astype(BF16)
        return jnp.concatenate([xb * mk for mk in head_mask], axis=0)

    def mm_bd(a, b_bd, dims=NN):
        return lax.dot_general(a.astype(BF16), b_bd, dims, preferred_element_type=F32)

    def state_free(gi, nb):
        G = range(RWKV_GROUP)
        sls = [pl.ds(pl.multiple_of((gi * RWKV_GROUP + j) * C, C), C) for j in G]
        rt = [rt_ref[nb, sl, :] for sl in sls]
        at = [at_ref[nb, sl, :] for sl in sls]
        kt = [kt_ref[nb, sl, :] for sl in sls]
        bt = [bt_ref[nb, sl, :] for sl in sls]
        v = [v_ref[nb, sl, :] for sl in sls]
        wc = [wc_ref[nb, pl.ds(gi * RWKV_GROUP + j, 1), :] for j in G]
        ar = [jnp.concatenate([at[j], rt[j]], axis=0) for j in G]
        bdb = [bd_split(bt[j]) for j in G]
        bdk = [bd_split(kt[j]) for j in G]
        a_b = [mm_bd(ar[j], bdb[j], NT) for j in G]
        a_k = [mm_bd(ar[j], bdk[j], NT) for j in G]
        lo = [jnp.where(strict, a_b[j][:C], 0.0) for j in G]
        a_ak = [jnp.where(strict, a_k[j][:C], 0.0) for j in G]
        a_rb = [jnp.where(incl, a_b[j][C:], 0.0) for j in G]
        a_rk = [jnp.where(incl, a_k[j][C:], 0.0) for j in G]
        pw = lo
        tinv = [eye + lo[j] for j in G]
        bdp = [bd_split(pw[j]) for j in G]
        span = 2
        while span < C:
            pw = [mm_bd(pw[j], bdp[j]) for j in G]
            bdp = [bd_split(pw[j]) for j in G]
            tinv = [tinv[j] + mm_bd(tinv[j], bdp[j]) for j in G]
            span *= 2
        bdv = [bd_split(v[j]) for j in G]
        bda = [bd_split(at[j]) for j in G]
        abar = [mm_bd(tinv[j], bda[j]) for j in G]
        akv = [bd_split(mm_bd(a_ak[j], bdv[j])) for j in G]
        u0 = [mm_bd(tinv[j], akv[j]) for j in G]
        y0 = [mm_bd(a_rk[j], bdv[j]) for j in G]
        kv = [_mm(v[j], kt[j] * wc[j], TN) * bd_mask for j in G]
        return [(jnp.concatenate([abar[j], rt[j]], axis=0), u0[j], y0[j], a_rb[j], bt[j] * wc[j], kv[j], wc[j])
                for j in G]

    def group(gi, carry):
        seqs = range(rt_ref.shape[0])
        pre = [state_free(gi, nb) for nb in seqs]
        s = [state[nb] for nb in seqs]
        ys = [[] for _ in seqs]
        for j in range(RWKV_GROUP):
            for nb in seqs:
                abar_rt, u0, y0, a_rb, btw, kv, wc = pre[nb][j]
                a_s = _mm(abar_rt, s[nb], NT)
                u = a_s[:C] + u0
                ys[nb].append(a_s[C:] + y0 + mm_bd(a_rb, bd_split(u)))
                s[nb] = s[nb] * wc + _mm(u, btw, TN) * bd_mask + kv
        sl = pl.ds(pl.multiple_of(gi * (RWKV_GROUP * C), RWKV_GROUP * C), RWKV_GROUP * C)
        for nb in seqs:
            state[nb] = s[nb]
            y = jnp.concatenate(ys[nb], axis=0)
            mean = _mm2(y, ones) * (1.0 / B_HEAD_DIM)
            d = y - mean
            var = _mm2(d * d, ones) * (1.0 / B_HEAD_DIM)
            yn = d * lax.rsqrt(var + B_LNX_EPS) * lng_ref[...] + lnb_ref[...]
            o_ref[nb, sl, :] = (yn + bonus_ref[nb, sl, :]) * g_ref[nb, sl, :]
        return carry

    lax.fori_loop(0, tt // (RWKV_GROUP * C), group, 0)


def _rwkv_scan(rt, at, kt, bt, v, wc, bonus, g, lnx_g, lnx_b, tt):
    Bn, S, W = rt.shape
    n = tt // RWKV_CHUNK
    nseq = 2 if Bn % 2 == 0 else 1
    seq = pl.BlockSpec((nseq, tt, W), lambda b, i: (b, i, 0))
    vec = pl.BlockSpec((1, W), lambda b, i: (0, 0))
    return pl.pallas_call(
        functools.partial(_rwkv_scan_kernel, tt=tt),
        grid=(Bn // nseq, S // tt),
        in_specs=[seq, seq, seq, seq, seq, pl.BlockSpec((nseq, n, W), lambda b, i: (b, i, 0)), seq, seq, vec, vec],
        out_specs=seq,
        out_shape=jax.ShapeDtypeStruct((Bn, S, W), F32),
        scratch_shapes=[pltpu.VMEM((nseq, B_HEADS * B_HEAD_DIM, W), F32)],
        compiler_params=_cparams("arbitrary", "arbitrary"),
        name="rwkv_scan",
    )(rt, at, kt, bt, v, wc, bonus, g, lnx_g.reshape(1, W), lnx_b.reshape(1, W))


def _gmlp_kernel(pc_ref, lng_ref, lnb_ref, ws_ref, bs_ref, o_ref, *, tm):
    x = pc_ref[0]
    z = x * (0.5 * (1.0 + jnp.tanh(math.sqrt(2.0 / math.pi) * (x + 0.044715 * (x * x * x)))))
    u, v = z[:, :C_WIDTH], z[:, C_WIDTH:]
    mu = jnp.mean(v, axis=-1, keepdims=True)
    d = v - mu
    var = jnp.mean(d * d, axis=-1, keepdims=True)
    vn = d * lax.rsqrt(var + LN_EPS) * lng_ref[...] + lnb_ref[...]
    group = lax.broadcasted_iota(jnp.int32, (CHUNK, C_WIDTH), 1) // C_GROUP_DIM
    tril = (lax.broadcasted_iota(jnp.int32, (CHUNK, CHUNK), 0)
            >= lax.broadcasted_iota(jnp.int32, (CHUNK, CHUNK), 1))
    ws = [jnp.where(tril, ws_ref[gi], 0.0).astype(BF16) for gi in range(C_GROUPS)]
    for c in range(tm // CHUNK):
        sl = slice(c * CHUNK, (c + 1) * CHUNK)
        vc = vn[sl].astype(BF16)
        sv = bs_ref[...]
        for gi in range(C_GROUPS):
            t = jnp.dot(ws[gi], vc, preferred_element_type=F32)
            sv = sv + jnp.where(group == gi, t, 0.0)
        o_ref[0, sl, :] = u[sl] * sv


def _gmlp(pbc, ln_g, ln_b, w_s, b_s, tm):
    Bn, S, _ = pbc.shape
    bs_wide = jnp.repeat(jnp.transpose(b_s), C_GROUP_DIM, axis=1)
    return pl.pallas_call(
        functools.partial(_gmlp_kernel, tm=tm),
        grid=(Bn, S // tm),
        in_specs=[
            pl.BlockSpec((1, tm, C_COLS), lambda b, i: (b, i, B_COLS // C_COLS)),
            pl.BlockSpec((1, C_WIDTH), lambda b, i: (0, 0)),
            pl.BlockSpec((1, C_WIDTH), lambda b, i: (0, 0)),
            pl.BlockSpec((C_GROUPS, CHUNK, CHUNK), lambda b, i: (0, 0, 0)),
            pl.BlockSpec((CHUNK, C_WIDTH), lambda b, i: (0, 0)),
        ],
        out_specs=pl.BlockSpec((1, tm, C_WIDTH), lambda b, i: (b, i, 0)),
        out_shape=jax.ShapeDtypeStruct((Bn, S, C_WIDTH), F32),
        compiler_params=_cparams("arbitrary", "arbitrary"),
        name="gmlp",
    )(pbc, ln_g.reshape(1, -1), ln_b.reshape(1, -1), w_s, bs_wide)


def _mid_kernel(ya_ref, yb_ref, yc_ref, x_ref, woa_ref, wob_ref, woc_ref, gpost_ref, g1_ref,
                gpre_ref, sc_ref, sh_ref, wr_ref, xo_ref, h_ref, score_ref):
    y = (_mm(ya_ref[0], woa_ref[...]) + _mm(yb_ref[0], wob_ref[...]) + _mm(yc_ref[0], woc_ref[...]))
    xn = x_ref[0] + g1_ref[0] * (_rms(y) * gpost_ref[...])
    xo_ref[0] = xn
    h = _rms(xn) * gpre_ref[...] * (1.0 + sc_ref[0]) + sh_ref[0]
    h_ref[0] = _pack_bf16_pair(h)
    score_ref[0] = _sigmoid(_mm3(wr_ref[...], h, NT))


def _mid(ya, yb, yc, x, woa, wob, woc, gpost, g1, gpre, sc, sh, wr, tm):
    Bn, S, D = x.shape
    NR = wr.shape[0]
    seq = lambda w: pl.BlockSpec((1, tm, w), lambda b, i: (b, i, 0))
    full = lambda shp: pl.BlockSpec(shp, lambda b, i: (0,) * len(shp))
    per_b = pl.BlockSpec((1, 1, D), lambda b, i: (b, 0, 0))
    return pl.pallas_call(
        _mid_kernel,
        grid=(Bn, S // tm),
        in_specs=[seq(A_WIDTH), seq(B_WIDTH), seq(C_WIDTH), seq(D),
                  full((A_WIDTH, D)), full((B_WIDTH, D)), full((C_WIDTH, D)),
                  full((1, D)), per_b, full((1, D)), per_b, per_b,
                  full((NR, D))],
        out_specs=[seq(D), seq(D // 2), pl.BlockSpec((1, NR, tm), lambda b, i: (b, 0, i))],
        out_shape=[jax.ShapeDtypeStruct((Bn, S, D), F32), jax.ShapeDtypeStruct((Bn, S, D // 2), jnp.int32),
                   jax.ShapeDtypeStruct((Bn, NR, S), F32)],
        compiler_params=_cparams("arbitrary", "arbitrary"),
        name="mid",
    )(ya, yb, yc, x, woa, wob, woc, gpost.reshape(1, D), g1, gpre.reshape(1, D), sc, sh, wr)


def _shared_expert_kernel(h_ref, ws1_ref, ws3_ref, ws2_ref, o_ref):
    lo, hi = _unpack_bf16_pair(h_ref[0])
    hb = jnp.concatenate([lo, hi], axis=1).astype(BF16)
    t = _silu(jnp.dot(hb, ws1_ref[...], preferred_element_type=F32)) * jnp.dot(
        hb, ws3_ref[...], preferred_element_type=F32)
    o_ref[0] = jnp.dot(t.astype(BF16), ws2_ref[...], preferred_element_type=F32)


def _shared_expert(hp, ws1, ws3, ws2, tm):
    Bn, S, DP = hp.shape
    D, F = ws1.shape
    full = lambda shp: pl.BlockSpec(shp, lambda b, i: (0,) * len(shp))
    return pl.pallas_call(
        _shared_expert_kernel,
        grid=(Bn, S // tm),
        in_specs=[pl.BlockSpec((1, tm, DP), lambda b, i: (b, i, 0)), full((D, F)), full((D, F)), full((F, D))],
        out_specs=pl.BlockSpec((1, tm, D), lambda b, i: (b, i, 0)),
        out_shape=jax.ShapeDtypeStruct((Bn, S, D), F32),
        compiler_params=_cparams("arbitrary", "arbitrary"),
        name="shared_expert",
    )(hp, ws1, ws3, ws2)


def _first_argmax(vals, iota, n):
    m = jnp.max(vals, axis=0, keepdims=True)
    idx = jnp.min(jnp.where(vals == m, iota, n), axis=0, keepdims=True)
    return m, idx


def _route_kernel(sc_ref, bias_ref, e_ref, w_ref, r_ref, cnt_ref, carry, *, tm):
    @pl.when((pl.program_id(0) == 0) & (pl.program_id(1) == 0))
    def _():
        carry[...] = jnp.zeros_like(carry)

    G = EXPERTS_PER_GROUP
    s = sc_ref[0]
    biased = s + bias_ref[...]
    neg_inf = jnp.float32(-jnp.inf)
    io8 = lax.broadcasted_iota(jnp.int32, (G, tm), 0)
    gs_rows = []
    for g in range(N_GROUPS):
        blk = biased[g * G:(g + 1) * G]
        m1, i1 = _first_argmax(blk, io8, G)
        m2 = jnp.max(jnp.where(io8 == i1, neg_inf, blk), axis=0, keepdims=True)
        gs_rows.append(m1 + m2)
    gs = jnp.concatenate(gs_rows, axis=0)
    gio = lax.broadcasted_iota(jnp.int32, (N_GROUPS, tm), 0)
    gsel = jnp.zeros((N_GROUPS, tm), jnp.bool_)
    for _ in range(TOPK_GROUPS):
        _, gi = _first_argmax(gs, gio, N_GROUPS)
        pick = gio == gi
        gsel = gsel | pick
        gs = jnp.where(pick, neg_inf, gs)
    masked = jnp.concatenate(
        [jnp.where(gsel[g:g + 1], biased[g * G:(g + 1) * G], neg_inf) for g in range(N_GROUPS)], axis=0)

    eio = lax.broadcasted_iota(jnp.int32, (N_EXPERTS, tm), 0)
    picks, e_rows, s_rows = [], [], []
    for _ in range(TOP_K):
        _, ei = _first_argmax(masked, eio, N_EXPERTS)
        pick = eio == ei
        picks.append(pick)
        e_rows.append(ei)
        s_rows.append(jnp.sum(jnp.where(pick, s, 0.0), axis=0, keepdims=True))
        masked = jnp.where(pick, neg_inf, masked)
    top_s = jnp.concatenate(s_rows, axis=0)
    w_ref[...] = top_s / (jnp.sum(top_s, axis=0, keepdims=True) + 1e-20) * ROUTED_SCALE
    e_ref[...] = jnp.concatenate(e_rows, axis=0)

    sel = jnp.zeros((N_EXPERTS, tm), F32)
    for pick in picks:
        sel = sel + pick.astype(F32)
    before = (lax.broadcasted_iota(jnp.int32, (tm, tm), 0) < lax.broadcasted_iota(jnp.int32, (tm, tm), 1))
    pos = carry[...] + jnp.dot(sel.astype(BF16), before.astype(BF16), preferred_element_type=F32)
    r_ref[...] = jnp.concatenate(
        [jnp.sum(jnp.where(pick, pos, 0.0), axis=0, keepdims=True) for pick in picks], axis=0).astype(jnp.int32)
    total = carry[...] + jnp.sum(sel, axis=1, keepdims=True)
    carry[...] = total
    cnt_ref[...] = jnp.broadcast_to(total, cnt_ref.shape).astype(jnp.int32)


def _route(scores_t, e_bias, tm):
    Bn, _, S = scores_t.shape
    T = Bn * S
    nt = S // tm
    tok = pl.BlockSpec((TOP_K, tm), lambda b, i: (0, b * nt + i))
    return pl.pallas_call(
        functools.partial(_route_kernel, tm=tm),
        grid=(Bn, nt),
        in_specs=[pl.BlockSpec((1, N_EXPERTS, tm), lambda b, i: (b, 0, i)),
                  pl.BlockSpec((N_EXPERTS, 1), lambda b, i: (0, 0))],
        out_specs=[tok, tok, tok, pl.BlockSpec((N_EXPERTS, V7X_LANES), lambda b, i: (0, 0))],
        out_shape=[jax.ShapeDtypeStruct((TOP_K, T), jnp.int32), jax.ShapeDtypeStruct((TOP_K, T), F32),
                   jax.ShapeDtypeStruct((TOP_K, T), jnp.int32),
                   jax.ShapeDtypeStruct((N_EXPERTS, V7X_LANES), jnp.int32)],
        scratch_shapes=[pltpu.VMEM((N_EXPERTS, 1), F32)],
        compiler_params=_cparams("arbitrary", "arbitrary"),
        name="route",
    )(scores_t, e_bias.reshape(N_EXPERTS, 1))


def _dest_kernel(start_ref, e_ref, r_ref, o_ref):
    e = e_ref[...]
    acc = r_ref[...]
    for ex in range(N_EXPERTS):
        acc = acc + jnp.where(e == ex, start_ref[ex], 0)
    o_ref[0] = acc


def _dest_rows(pad_start, eidx, rank, tt):
    K_, T = eidx.shape
    grid_spec = pltpu.PrefetchScalarGridSpec(
        num_scalar_prefetch=1,
        grid=(T // tt,),
        in_specs=[pl.BlockSpec((K_, tt), lambda i, st: (0, i)), pl.BlockSpec((K_, tt), lambda i, st: (0, i))],
        out_specs=pl.BlockSpec((1, K_, tt), lambda i, st: (i, 0, 0)),
    )
    return pl.pallas_call(
        _dest_kernel,
        grid_spec=grid_spec,
        out_shape=jax.ShapeDtypeStruct((T // tt, K_, tt), jnp.int32),
        compiler_params=_cparams("arbitrary"),
        name="dest_rows",
    )(pad_start, eidx, rank)


def _expert_kernel(blk_e_ref, n_used_ref, n_valid_ref, x_ref, w1_ref, w3_ref, w2_ref, o_ref, w1b, w3b, w2b):
    i = pl.program_id(0)

    @pl.when((i == 0) | (blk_e_ref[i] != blk_e_ref[jnp.maximum(i - 1, 0)]))
    def _():
        w1b[...] = w1_ref[0].astype(BF16)
        w3b[...] = w3_ref[0].astype(BF16)
        w2b[...] = w2_ref[0].astype(BF16)

    @pl.when(i < n_used_ref[0])
    def _():
        row = lax.broadcasted_iota(jnp.int32, x_ref.shape, 0)
        x_lo, x_hi = _unpack_bf16_pair(jnp.where(row < n_valid_ref[i], x_ref[...], 0))
        x_lo, x_hi = x_lo.astype(BF16), x_hi.astype(BF16)
        half = x_lo.shape[1]

        def up(wb):
            return (jnp.dot(x_lo, wb[:half, :], preferred_element_type=F32)
                    + jnp.dot(x_hi, wb[half:, :], preferred_element_type=F32))

        t = _silu(up(w1b)) * up(w3b)
        o_ref[...] = _pack_bf16_pair(jnp.dot(t.astype(BF16), w2b[...], preferred_element_type=F32))


def _experts(blk_e, n_used, n_valid, xs, w1, w3, w2, layer):
    P, DP = xs.shape
    EB = EXPERT_BLOCK
    n_blocks = blk_e.shape[0]
    D, F = w1.shape[2], w1.shape[3]
    rows = pl.BlockSpec((EB, DP), lambda i, be, nu, nv: (jnp.minimum(i, nu[0] - 1), 0))
    grid_spec = pltpu.PrefetchScalarGridSpec(
        num_scalar_prefetch=3,
        grid=(n_blocks,),
        in_specs=[
            rows,
            pl.BlockSpec((None, 1, D, F), lambda i, be, nu, nv: (layer, be[i], 0, 0)),
            pl.BlockSpec((None, 1, D, F), lambda i, be, nu, nv: (layer, be[i], 0, 0)),
            pl.BlockSpec((None, 1, F, D), lambda i, be, nu, nv: (layer, be[i], 0, 0)),
        ],
        out_specs=rows,
        scratch_shapes=[pltpu.VMEM((D, F), BF16), pltpu.VMEM((D, F), BF16), pltpu.VMEM((F, D), BF16)],
    )
    return pl.pallas_call(
        _expert_kernel,
        grid_spec=grid_spec,
        out_shape=jax.ShapeDtypeStruct((P, DP), jnp.int32),
        compiler_params=_cparams("arbitrary"),
        name="experts",
    )(blk_e, n_used, n_valid, xs, w1, w3, w2)


def _block_layout(counts, n_blocks):
    EB = EXPERT_BLOCK
    padded = (counts + EB - 1) // EB * EB
    ex = jnp.arange(N_EXPERTS)
    pad_end = jnp.sum(jnp.where(ex[:, None] <= ex[None, :], padded[:, None], 0), axis=0)
    pad_start = pad_end - padded
    blk_row = (jnp.arange(n_blocks) * EB)[:, None]
    blk_e = jnp.minimum(jnp.sum((pad_end[None, :] <= blk_row).astype(jnp.int32), axis=1), N_EXPERTS - 1)
    n_used = (jnp.sum(padded) // EB).astype(jnp.int32).reshape(1)
    mine = (pad_start[None, :] <= blk_row) & (blk_row < pad_end[None, :])
    n_valid = jnp.sum(jnp.where(mine, jnp.clip(counts[None, :] - (blk_row - pad_start[None, :]), 0, EB), 0), axis=1)
    return pad_start.astype(jnp.int32), blk_e.astype(jnp.int32), n_used, n_valid.astype(jnp.int32)


SC_GATHER_ROWS = 64


def _sc_gather_rows(table, idx):
    info = plsc.get_sparse_core_info()
    nc, ns = info.num_cores, info.num_subcores
    M = idx.shape[0]
    W = table.shape[1]
    b = SC_GATHER_ROWS
    per_worker = M // (nc * ns)
    steps = per_worker // b
    assert per_worker * nc * ns == M and steps * b == per_worker and steps % 2 == 0
    mesh = plsc.VectorSubcoreMesh(core_axis_name="c", subcore_axis_name="s")

    @functools.partial(
        pl.kernel, mesh=mesh,
        out_type=jax.ShapeDtypeStruct((M, W), table.dtype),
        scratch_types=[pltpu.VMEM((2, b), jnp.int32), pltpu.VMEM((2, b, W), table.dtype),
                       pltpu.SemaphoreType.DMA, pltpu.SemaphoreType.DMA],
        name="sc_gather_rows",
    )
    def gather(table_hbm, idx_hbm, out_hbm, idx_v, rows_v, sem0, sem1):
        wid = lax.axis_index("s") * nc + lax.axis_index("c")
        sems = (sem0, sem1)

        def base(s):
            return pl.multiple_of(wid * per_worker + s * b, b)

        def gather_copy(slot):
            return pltpu.make_async_copy(table_hbm.at[idx_v.at[slot]], rows_v.at[slot], sems[slot])

        def start(s, slot):
            pltpu.sync_copy(idx_hbm.at[pl.ds(base(s), b)], idx_v.at[slot])
            gather_copy(slot).start()

        def finish(s, slot):
            gather_copy(slot).wait()
            pltpu.sync_copy(rows_v.at[slot], out_hbm.at[pl.ds(base(s), b)])

        start(0, 0)

        @pl.loop(0, steps, step=2)
        def _(s):
            start(s + 1, 1)
            finish(s, 0)

            @pl.when(s + 2 < steps)
            def _():
                start(s + 2, 0)

            finish(s + 1, 1)

    return gather(table, idx)


def _sc_scatter_rows(rows, idx, n_out):
    info = plsc.get_sparse_core_info()
    nc, ns = info.num_cores, info.num_subcores
    T, W = rows.shape
    G, K_, b = idx.shape
    steps = G // (nc * ns)
    assert steps * nc * ns == G and G * b == T
    mesh = plsc.VectorSubcoreMesh(core_axis_name="c", subcore_axis_name="s")

    @functools.partial(
        pl.kernel, mesh=mesh,
        out_type=jax.ShapeDtypeStruct((n_out, W), rows.dtype),
        scratch_types=[pltpu.VMEM((K_, b), jnp.int32), pltpu.VMEM((b, W), rows.dtype), pltpu.SemaphoreType.DMA],
        name="sc_scatter_rows",
    )
    def scatter(rows_hbm, idx_hbm, out_hbm, idx_v, rows_v, sem):
        wid = lax.axis_index("s") * nc + lax.axis_index("c")

        @pl.loop(0, steps)
        def _(s):
            g = wid * steps + s
            pltpu.sync_copy(idx_hbm.at[g], idx_v)
            pltpu.sync_copy(rows_hbm.at[pl.ds(pl.multiple_of(g * b, b), b)], rows_v)
            copies = [pltpu.async_copy(rows_v, out_hbm.at[idx_v.at[k]], sem) for k in range(K_)]
            for cp in copies:
                cp.wait()

    return scatter(rows, idx)


def _combine_dense_kernel(rows_ref, w_ref, x_ref, shared_ref, gpost_ref, g2_ref, o_ref):
    w = w_ref[...]
    tt, half = rows_ref.shape[1], rows_ref.shape[2]
    y_lo = jnp.zeros((tt, half), F32)
    y_hi = jnp.zeros((tt, half), F32)
    for k in range(TOP_K):
        lo, hi = _unpack_bf16_pair(rows_ref[k])
        y_lo = y_lo + w[:, k:k + 1] * lo
        y_hi = y_hi + w[:, k:k + 1] * hi
    y = shared_ref[0] + jnp.concatenate([y_lo, y_hi], axis=1)
    o_ref[0] = x_ref[0] + g2_ref[0] * (_rms(y) * gpost_ref[...])


def _combine_dense(rows, w_tok, x, shared, gpost, g2, tt):
    Bn, S, D = x.shape
    K_, T, DP = rows.shape
    nt = S // tt
    seq = pl.BlockSpec((1, tt, D), lambda b, i: (b, i, 0))
    return pl.pallas_call(
        _combine_dense_kernel,
        grid=(Bn, nt),
        in_specs=[pl.BlockSpec((K_, tt, DP), lambda b, i: (0, b * nt + i, 0)),
                  pl.BlockSpec((tt, K_), lambda b, i: (b * nt + i, 0)), seq, seq,
                  pl.BlockSpec((1, D), lambda b, i: (0, 0)), pl.BlockSpec((1, 1, D), lambda b, i: (b, 0, 0))],
        out_specs=seq,
        out_shape=jax.ShapeDtypeStruct((Bn, S, D), F32),
        compiler_params=_cparams("arbitrary", "arbitrary"),
        name="combine_dense",
    )(rows, w_tok, x, shared, gpost.reshape(1, D), g2)


def kernel(x, c, w_ada, b_ada, norm_pre_mix, norm_post_mix, norm_pre_ffn, norm_post_ffn, w_in, w_out, rel_bias_table, diff_lambda, diff_subln, rwkv_mu, rwkv_w0, rwkv_w2, rwkv_a0, rwkv_a2, rwkv_g2, rwkv_k_k, rwkv_k_a, rwkv_r_k, rwkv_lnx_g, rwkv_lnx_b, gmlp_ln_g, gmlp_ln_b, gmlp_w_s, gmlp_b_s, router_w, router_bias, exp_w1, exp_w3, exp_w2, shared_w1, shared_w3, shared_w2):
    Bn, S, D = x.shape
    depth = w_ada.shape[0]
    tm = min(256, S)
    tm_wide = min(512, S)
    tq = min(512, S // 2)
    t_rwkv = min(512, S)

    mod = _adaln(c, w_ada, b_ada)
    band_t = _attn_band(rel_bias_table, tq)
    zpad = jnp.zeros((B_DECAY_LORA, B_WIDTH), F32)
    for l in range(depth):
        sh1, sc1, g1, sh2, sc2, g2 = [m.reshape(Bn, 1, D) for m in jnp.split(mod[l], 6, axis=-1)]
        w_in_b = w_in[l].astype(BF16)
        pa, vt, pbc = _inproj(x, norm_pre_mix[l], sc1, sh1, w_in_b[:, :2 * A_WIDTH],
                              jnp.transpose(w_in_b[:, 2 * A_WIDTH:A_COLS]), w_in_b[:, A_COLS:], tm_wide)
        lambda_init = 0.8 - 0.6 * math.exp(-0.3 * l)
        ya = _diff_attention(pa, vt, band_t, diff_lambda[l], diff_subln[l], lambda_init, tq)
        prep = _rwkv_prep(pbc, rwkv_mu[l], rwkv_w0[l], jnp.concatenate([rwkv_w2[l], zpad], axis=0),
                          rwkv_a0[l], jnp.concatenate([zpad, rwkv_a2[l]], axis=0), rwkv_g2[l],
                          rwkv_k_k[l], rwkv_k_a[l], rwkv_r_k[l].reshape(-1), min(2 * t_rwkv, S))
        yb = _rwkv_scan(*prep, rwkv_lnx_g[l], rwkv_lnx_b[l], t_rwkv)
        yc = _gmlp(pbc, gmlp_ln_g[l], gmlp_ln_b[l], gmlp_w_s[l], gmlp_b_s[l], tm_wide)

        w_out_b = w_out[l].astype(BF16)
        wr_t = jnp.pad(jnp.transpose(router_w[l]), ((0, V7X_LANES - N_EXPERTS), (0, 0)))
        x, h, scores_t = _mid(
            ya, yb, yc, x, w_out_b[:A_WIDTH], w_out_b[A_WIDTH:A_WIDTH + B_WIDTH], w_out_b[A_WIDTH + B_WIDTH:],
            norm_post_mix[l], g1, norm_pre_ffn[l], sc2, sh2, wr_t, min(2 * tm_wide, S))

        T = Bn * S
        n_blocks = -(-T * TOP_K // EXPERT_BLOCK) + N_EXPERTS
        eidx, wgt, rank, cnt = _route(scores_t, router_bias[l], tm_wide)
        pad_start, blk_e, n_used, n_valid = _block_layout(cnt[:, 0], n_blocks)
        dest = _dest_rows(pad_start, eidx, rank, tm)
        b = SC_GATHER_ROWS
        dest_sc = jnp.transpose(dest.reshape(T // tm, TOP_K, tm // b, b), (0, 2, 1, 3)).reshape(T // b, TOP_K, b)
        xs = _sc_scatter_rows(h.reshape(T, D // 2), dest_sc, n_blocks * EXPERT_BLOCK)
        shared = _shared_expert(h, shared_w1[l].astype(BF16), shared_w3[l].astype(BF16),
                                shared_w2[l].astype(BF16), tm_wide)
        ys = _experts(blk_e, n_used, n_valid, xs, exp_w1, exp_w3, exp_w2, l)
        dest_kt = jnp.transpose(dest, (1, 0, 2)).reshape(TOP_K * T)
        rows = _sc_gather_rows(ys, dest_kt).reshape(TOP_K, T, D // 2)
        x = _combine_dense(rows, jnp.transpose(wgt), x, shared, norm_post_ffn[l], g2, tm_wide)
    return x
```
